```python
import jax, jax.numpy as jnp
from jax import lax
import numpy as np

D_MODEL = 1024
BATCH = 16
SEQ = 2048
DEPTH = 2

CHUNK = 64
Q_BLOCK = 128
EPS = 1e-6
NEG = -1e30

A_HEADS = 6
A_NOPE = 64
A_ROPE = 32
A_V = 64
A_Q_RANK = 384
A_KV_RANK = 256
A_WIDTH = A_HEADS * A_V
ROPE_THETA = 10000.0

B_HEADS = 5
B_HD = 64
B_WIDTH = B_HEADS * B_HD
B_LEFT_CHUNKS = 8
B_BAND = (B_LEFT_CHUNKS + 1) * CHUNK
REL_CLIP = 128

C_HEADS = 5
C_HD = 64
C_WIDTH = C_HEADS * C_HD
FORGET_BIAS_INIT = 2.0

D_MIX = A_WIDTH + B_WIDTH + C_WIDTH

IN_SIZES = (A_Q_RANK, A_KV_RANK, A_ROPE, A_WIDTH,
            B_WIDTH, B_WIDTH, B_WIDTH, B_WIDTH,
            C_WIDTH, C_WIDTH, C_WIDTH, C_HEADS, C_WIDTH)
N_IN = (A_Q_RANK + A_KV_RANK + A_ROPE + A_WIDTH + 4 * B_WIDTH + 4 * C_WIDTH + C_HEADS)

kernel_name = "hybrid_mla_chunkband_fox_encoder"


def rmsnorm(x, g):
    xf = x.astype(jnp.float32)
    y = xf * lax.rsqrt(jnp.mean(xf * xf, axis=-1, keepdims=True) + EPS)
    return (y * g.astype(jnp.float32)).astype(x.dtype)


def split_cols(z, sizes):
    idx, o = [], 0
    for s in sizes[:-1]:
        o += s
        idx.append(o)
    return jnp.split(z, idx, axis=-1)


def rope_tables(positions):
    inv = ROPE_THETA ** (-jnp.arange(0, A_ROPE, 2, dtype=jnp.float32) / A_ROPE)
    ang = positions.astype(jnp.float32)[..., None] * inv
    return jnp.cos(ang), jnp.sin(ang)


def apply_rope(x, cos, sin):
    x1, x2 = jnp.split(x.astype(jnp.float32), 2, axis=-1)
    out = jnp.concatenate([x1 * cos - x2 * sin, x1 * sin + x2 * cos], axis=-1)
    return out.astype(x.dtype)


def attend(s, mask, v):
    p = jax.nn.softmax(jnp.where(mask, s, NEG), axis=-1)
    return jnp.einsum('bhqk,bhkd->bhqd', p.astype(v.dtype), v)


def mla_mixer(c_q, c_kv, k_pe, q_norm_g, w_uq, kv_norm_g, w_ukv, cos, sin):
    Bn, S, _ = c_q.shape
    q = (rmsnorm(c_q, q_norm_g) @ w_uq).reshape(Bn, S, A_HEADS, A_NOPE + A_ROPE)
    q_nope, q_pe = q[..., :A_NOPE], q[..., A_NOPE:]
    q_pe = apply_rope(q_pe, cos[:, :, None, :], sin[:, :, None, :])
    kv = (rmsnorm(c_kv, kv_norm_g) @ w_ukv).reshape(Bn, S, A_HEADS, A_NOPE + A_V)
    k_nope, v = kv[..., :A_NOPE], kv[..., A_NOPE:]
    k_pe = apply_rope(k_pe, cos, sin)
    k_pe = jnp.broadcast_to(k_pe[:, :, None, :], (Bn, S, A_HEADS, A_ROPE))
    q = jnp.concatenate([q_nope, q_pe], axis=-1).transpose(0, 2, 1, 3)
    k = jnp.concatenate([k_nope, k_pe], axis=-1).transpose(0, 2, 1, 3)
    v = v.transpose(0, 2, 1, 3)
    scale = (A_NOPE + A_ROPE) ** -0.5
    outs = []
    for i in range(S // Q_BLOCK):
        q0 = i * Q_BLOCK
        kend = q0 + Q_BLOCK
        s = jnp.einsum('bhqd,bhkd->bhqk', q[:, :, q0:kend], k[:, :, :kend],
                       preferred_element_type=jnp.float32) * scale
        q_chunk = (q0 + jnp.arange(Q_BLOCK)) // CHUNK
        k_chunk = jnp.arange(kend) // CHUNK
        mask = k_chunk[None, :] <= q_chunk[:, None]
        outs.append(attend(s, mask, v[:, :, :kend]))
    o = jnp.concatenate(outs, axis=2)
    return o.transpose(0, 2, 1, 3).reshape(Bn, S, A_WIDTH)


def chunk_band_mixer(q, k, v, rel_bias):
    Bn, S, _ = q.shape
    NC = S // CHUNK
    qc = q.reshape(Bn, NC, CHUNK, B_HEADS, B_HD)

    def band(t):
        t = t.reshape(Bn, S, B_HEADS, B_HD)
        t = jnp.pad(t, ((0, 0), (B_LEFT_CHUNKS * CHUNK, 0), (0, 0), (0, 0)))
        t = t.reshape(Bn, NC + B_LEFT_CHUNKS, CHUNK, B_HEADS, B_HD)
        t = jnp.stack([t[:, j:j + NC] for j in range(B_LEFT_CHUNKS + 1)], axis=2)
        return t.reshape(Bn, NC, B_BAND, B_HEADS, B_HD)

    kb, vb = band(k), band(v)
    s = jnp.einsum('bnqhd,bnkhd->bnhqk', qc, kb,
                   preferred_element_type=jnp.float32) * (B_HD ** -0.5)
    rel = (B_LEFT_CHUNKS * CHUNK + jnp.arange(CHUNK))[:, None] - jnp.arange(B_BAND)[None, :]
    idx = jnp.clip(rel, -REL_CLIP, REL_CLIP) + REL_CLIP
    s = s + rel_bias[:, idx].astype(jnp.float32)
    k_chunk = jnp.arange(NC)[:, None] - B_LEFT_CHUNKS + (jnp.arange(B_BAND) // CHUNK)[None, :]
    mask = (k_chunk >= 0)[None, :, None, None, :]
    p = jax.nn.softmax(jnp.where(mask, s, NEG), axis=-1)
    o = jnp.einsum('bnhqk,bnkhd->bnqhd', p.astype(vb.dtype), vb)
    return o.reshape(Bn, S, B_WIDTH)


def forgetting_mixer(q, k, v, f_logit, f_bias):
    Bn, S, _ = q.shape
    q = q.reshape(Bn, S, C_HEADS, C_HD).transpose(0, 2, 1, 3)
    k = k.reshape(Bn, S, C_HEADS, C_HD).transpose(0, 2, 1, 3)
    v = v.reshape(Bn, S, C_HEADS, C_HD).transpose(0, 2, 1, 3)
    log_f = jax.nn.log_sigmoid(f_logit.astype(jnp.float32) + f_bias.astype(jnp.float32))
    F = jnp.cumsum(log_f, axis=1).transpose(0, 2, 1)
    scale = C_HD ** -0.5
    outs = []
    for i in range(S // Q_BLOCK):
        q0 = i * Q_BLOCK
        kend = q0 + Q_BLOCK
        s = jnp.einsum('bhqd,bhkd->bhqk', q[:, :, q0:kend], k[:, :, :kend],
                       preferred_element_type=jnp.float32) * scale
        s = s + F[:, :, q0:kend, None] - F[:, :, None, :kend]
        mask = jnp.arange(kend)[None, :] <= (q0 + jnp.arange(Q_BLOCK))[:, None]
        outs.append(attend(s, mask, v[:, :, :kend]))
    o = jnp.concatenate(outs, axis=2)
    return o.transpose(0, 2, 1, 3).reshape(Bn, S, C_WIDTH)


def hybrid_layer(x, c_act, cos, sin, w_ada, b_ada, norm_g, w_in, a_q_norm_g, a_w_uq,
                 a_kv_norm_g, a_w_ukv, b_rel_bias, c_forget_b, w_out):
    mod = c_act @ w_ada + b_ada
    shift, scale, gate = jnp.split(mod, 3, axis=-1)
    h = rmsnorm(x, norm_g) * (1.0 + scale[:, None, :]) + shift[:, None, :]
    z = h @ w_in
    (a_cq, a_ckv, a_kpe, a_gate,
     b_q, b_k, b_v, b_gate,
     c_q, c_k, c_v, c_f, c_gate) = split_cols(z, IN_SIZES)
    a_out = mla_mixer(a_cq, a_ckv, a_kpe, a_q_norm_g, a_w_uq, a_kv_norm_g, a_w_ukv, cos, sin)
    b_out = chunk_band_mixer(b_q, b_k, b_v, b_rel_bias)
    c_out = forgetting_mixer(c_q, c_k, c_v, c_f, c_forget_b)
    y = jnp.concatenate([a_out * jax.nn.silu(a_gate),
                         b_out * jax.nn.silu(b_gate),
                         c_out * jax.nn.silu(c_gate)], axis=-1) @ w_out
    return x + gate[:, None, :] * y


def _fwd_setup_inputs(seed: int = 0) -> dict:
    key = jax.random.key(seed)
    ks = jax.random.split(key, 16)
    f32 = jnp.float32
    x = jax.random.normal(ks[0], (BATCH, SEQ, D_MODEL), f32)
    c = jax.random.normal(ks[1], (BATCH, D_MODEL), f32)
    start = jax.random.randint(ks[2], (BATCH, 1), 0, 64, dtype=jnp.int32) * CHUNK
    positions = (start + jnp.arange(SEQ, dtype=jnp.int32)[None, :]).astype(jnp.int32)
    w_ada = jax.random.normal(ks[3], (DEPTH, D_MODEL, 3 * D_MODEL), f32) * D_MODEL ** -0.5
    b_ada = jax.random.normal(ks[4], (DEPTH, 3 * D_MODEL), f32) * 0.02
    norm_g = 1.0 + 0.02 * jax.random.normal(ks[5], (DEPTH, D_MODEL), f32)
    w_in = jax.random.normal(ks[6], (DEPTH, D_MODEL, N_IN), f32) * D_MODEL ** -0.5
    a_q_norm_g = 1.0 + 0.02 * jax.random.normal(ks[7], (DEPTH, A_Q_RANK), f32)
    a_w_uq = jax.random.normal(ks[8], (DEPTH, A_Q_RANK, A_HEADS * (A_NOPE + A_ROPE)), f32) * A_Q_RANK ** -0.5
    a_kv_norm_g = 1.0 + 0.02 * jax.random.normal(ks[9], (DEPTH, A_KV_RANK), f32)
    a_w_ukv = jax.random.normal(ks[10], (DEPTH, A_KV_RANK, A_HEADS * (A_NOPE + A_V)), f32) * A_KV_RANK ** -0.5
    b_rel_bias = 0.2 * jax.random.normal(ks[11], (DEPTH, B_HEADS, 2 * REL_CLIP + 1), f32)
    c_forget_b = FORGET_BIAS_INIT + 0.1 * jax.random.normal(ks[12], (DEPTH, C_HEADS), f32)
    w_out = jax.random.normal(ks[13], (DEPTH, D_MIX, D_MODEL), f32) * D_MIX ** -0.5
    final_g = 1.0 + 0.02 * jax.random.normal(ks[14], (D_MODEL,), f32)
    return {"x": x, "c": c, "positions": positions, "w_ada": w_ada, "b_ada": b_ada,
            "norm_g": norm_g, "w_in": w_in, "a_q_norm_g": a_q_norm_g, "a_w_uq": a_w_uq,
            "a_kv_norm_g": a_kv_norm_g, "a_w_ukv": a_w_ukv, "b_rel_bias": b_rel_bias,
            "c_forget_b": c_forget_b, "w_out": w_out, "final_g": final_g}


def _fwd_reference(x, c, positions, w_ada, b_ada, norm_g, w_in, a_q_norm_g, a_w_uq,
              a_kv_norm_g, a_w_ukv, b_rel_bias, c_forget_b, w_out, final_g):
    cos, sin = rope_tables(positions)
    c_act = jax.nn.silu(c)
    for l in range(DEPTH):
        x = hybrid_layer(x, c_act, cos, sin, w_ada[l], b_ada[l], norm_g[l], w_in[l],
                         a_q_norm_g[l], a_w_uq[l], a_kv_norm_g[l], a_w_ukv[l],
                         b_rel_bias[l], c_forget_b[l], w_out[l])
    return rmsnorm(x, final_g)


import jax as _jax
import jax.numpy as _jnp

TWIN_FORMAT = 'train_step'
FWD_PARAMS = ['x', 'c', 'positions', 'w_ada', 'b_ada', 'norm_g', 'w_in', 'a_q_norm_g', 'a_w_uq', 'a_kv_norm_g', 'a_w_ukv', 'b_rel_bias', 'c_forget_b', 'w_out', 'final_g']
TWIN_WEIGHTS = ['w_ada', 'b_ada', 'norm_g', 'w_in', 'a_q_norm_g', 'a_w_uq', 'a_kv_norm_g', 'a_w_ukv', 'b_rel_bias', 'c_forget_b', 'w_out', 'final_g']
TWIN_DIFF_INPUT = 'x'
TWIN_INPUTS = ['x', 'c', 'positions', 'w_ada', 'b_ada', 'norm_g', 'w_in', 'a_q_norm_g', 'a_w_uq', 'a_kv_norm_g', 'a_w_ukv', 'b_rel_bias', 'c_forget_b', 'w_out', 'final_g', 'loss_target', 'm_w_ada', 'm_b_ada', 'm_norm_g', 'm_w_in', 'm_a_q_norm_g', 'm_a_w_uq', 'm_a_kv_norm_g', 'm_a_w_ukv', 'm_b_rel_bias', 'm_c_forget_b', 'm_w_out', 'm_final_g', 'v_w_ada', 'v_b_ada', 'v_norm_g', 'v_w_in', 'v_a_q_norm_g', 'v_a_w_uq', 'v_a_kv_norm_g', 'v_a_w_ukv', 'v_b_rel_bias', 'v_c_forget_b', 'v_w_out', 'v_final_g']
TWIN_OUTPUTS = ['loss', 'grad_x', 'grad_w_ada', 'grad_b_ada', 'grad_norm_g', 'grad_w_in', 'grad_a_q_norm_g', 'grad_a_w_uq', 'grad_a_kv_norm_g', 'grad_a_w_ukv', 'grad_b_rel_bias', 'grad_c_forget_b', 'grad_w_out', 'grad_final_g', 'delta_w_ada', 'delta_b_ada', 'delta_norm_g', 'delta_w_in', 'delta_a_q_norm_g', 'delta_a_w_uq', 'delta_a_kv_norm_g', 'delta_a_w_ukv', 'delta_b_rel_bias', 'delta_c_forget_b', 'delta_w_out', 'delta_final_g', 'new_m_w_ada', 'new_m_b_ada', 'new_m_norm_g', 'new_m_w_in', 'new_m_a_q_norm_g', 'new_m_a_w_uq', 'new_m_a_kv_norm_g', 'new_m_a_w_ukv', 'new_m_b_rel_bias', 'new_m_c_forget_b', 'new_m_w_out', 'new_m_final_g', 'new_v_w_ada', 'new_v_b_ada', 'new_v_norm_g', 'new_v_w_in', 'new_v_a_q_norm_g', 'new_v_a_w_uq', 'new_v_a_kv_norm_g', 'new_v_a_w_ukv', 'new_v_b_rel_bias', 'new_v_c_forget_b', 'new_v_w_out', 'new_v_final_g']
TWIN_LEAF_KINDS = {'loss': 'loss', 'grad_x': 'grad_x', 'grad_w_ada': 'grad_w', 'grad_b_ada': 'grad_w', 'grad_norm_g': 'grad_w', 'grad_w_in': 'grad_w', 'grad_a_q_norm_g': 'grad_w', 'grad_a_w_uq': 'grad_w', 'grad_a_kv_norm_g': 'grad_w', 'grad_a_w_ukv': 'grad_w', 'grad_b_rel_bias': 'grad_w', 'grad_c_forget_b': 'grad_w', 'grad_w_out': 'grad_w', 'grad_final_g': 'grad_w', 'delta_w_ada': 'delta_w', 'delta_b_ada': 'delta_w', 'delta_norm_g': 'delta_w', 'delta_w_in': 'delta_w', 'delta_a_q_norm_g': 'delta_w', 'delta_a_w_uq': 'delta_w', 'delta_a_kv_norm_g': 'delta_w', 'delta_a_w_ukv': 'delta_w', 'delta_b_rel_bias': 'delta_w', 'delta_c_forget_b': 'delta_w', 'delta_w_out': 'delta_w', 'delta_final_g': 'delta_w', 'new_m_w_ada': 'new_m', 'new_m_b_ada': 'new_m', 'new_m_norm_g': 'new_m', 'new_m_w_in': 'new_m', 'new_m_a_q_norm_g': 'new_m', 'new_m_a_w_uq': 'new_m', 'new_m_a_kv_norm_g': 'new_m', 'new_m_a_w_ukv': 'new_m', 'new_m_b_rel_bias': 'new_m', 'new_m_c_forget_b': 'new_m', 'new_m_w_out': 'new_m', 'new_m_final_g': 'new_m', 'new_v_w_ada': 'new_v', 'new_v_b_ada': 'new_v', 'new_v_norm_g': 'new_v', 'new_v_w_in': 'new_v', 'new_v_a_q_norm_g': 'new_v', 'new_v_a_w_uq': 'new_v', 'new_v_a_kv_norm_g': 'new_v', 'new_v_a_w_ukv': 'new_v', 'new_v_b_rel_bias': 'new_v', 'new_v_c_forget_b': 'new_v', 'new_v_w_out': 'new_v', 'new_v_final_g': 'new_v'}


def _forward(args):
    return _fwd_reference(*[args[k] for k in FWD_PARAMS])


def _output_shape():
    out = _jax.eval_shape(lambda: _forward(_fwd_setup_inputs(0)))
    return out.shape, out.dtype

N_MICROBATCH = 1
ADAM_LR = 0.001
ADAM_B1 = 0.9
ADAM_B2 = 0.999
ADAM_EPS = 1e-08
ADAM_WD = 0.01
ADAM_STEP = 10
PER_EXAMPLE_BATCH_AXIS = {'x': 0, 'c': 0, 'positions': 0, 'loss_target': 0}
SHARED_INPUTS = []
_WEIGHT_DTYPES = {'w_ada': _jnp.float32, 'b_ada': _jnp.float32, 'norm_g': _jnp.float32, 'w_in': _jnp.float32, 'a_q_norm_g': _jnp.float32, 'a_w_uq': _jnp.float32, 'a_kv_norm_g': _jnp.float32, 'a_w_ukv': _jnp.float32, 'b_rel_bias': _jnp.float32, 'c_forget_b': _jnp.float32, 'w_out': _jnp.float32, 'final_g': _jnp.float32}
MOMENT_SCALE = {'w_ada': 5.626381e-02, 'b_ada': 9.189709e-02, 'norm_g': 8.140965e-02, 'w_in': 5.399230e-02, 'a_q_norm_g': 1.186039e-02, 'a_w_uq': 9.363961e-03, 'a_kv_norm_g': 5.532033e-02, 'a_w_ukv': 3.182733e-02, 'b_rel_bias': 1.026957e-02, 'c_forget_b': 2.559589e-01, 'w_out': 6.117286e-02, 'final_g': 3.213809e+01}


def _to_microbatches(a, axis):
    t = _jnp.moveaxis(a, axis, 0)
    t = t.reshape((N_MICROBATCH, t.shape[0] // N_MICROBATCH) + t.shape[1:])
    return _jnp.moveaxis(t, 1, axis + 1)


def setup_inputs(seed: int = 0) -> dict:
    inp = _fwd_setup_inputs(seed)
    key = _jax.random.fold_in(_jax.random.key(seed), 7919)
    shape, _ = _output_shape()
    out = dict(inp)
    out["loss_target"] = _jax.random.normal(_jax.random.fold_in(key, 0), shape, _jnp.float32)
    for i, name in enumerate(TWIN_WEIGHTS):
        w = inp[name].astype(_jnp.float32)
        if MOMENT_SCALE is None:
            s = _jnp.sqrt(_jnp.mean(_jnp.square(w)) + 1e-30)
        else:
            s = MOMENT_SCALE[name]
        km, kv = _jax.random.split(_jax.random.fold_in(key, i + 1))
        out[name] = w
        out["m_" + name] = s * _jax.random.normal(km, w.shape, _jnp.float32)
        out["v_" + name] = (s * s) * _jax.random.uniform(kv, w.shape, _jnp.float32, 0.5, 1.5)
    if N_MICROBATCH > 1:
        for name, axis in PER_EXAMPLE_BATCH_AXIS.items():
            out[name] = _to_microbatches(out[name], axis)
    return {'x': out['x'], 'c': out['c'], 'positions': out['positions'], 'w_ada': out['w_ada'], 'b_ada': out['b_ada'], 'norm_g': out['norm_g'], 'w_in': out['w_in'], 'a_q_norm_g': out['a_q_norm_g'], 'a_w_uq': out['a_w_uq'], 'a_kv_norm_g': out['a_kv_norm_g'], 'a_w_ukv': out['a_w_ukv'], 'b_rel_bias': out['b_rel_bias'], 'c_forget_b': out['c_forget_b'], 'w_out': out['w_out'], 'final_g': out['final_g'], 'loss_target': out['loss_target'], 'm_w_ada': out['m_w_ada'], 'm_b_ada': out['m_b_ada'], 'm_norm_g': out['m_norm_g'], 'm_w_in': out['m_w_in'], 'm_a_q_norm_g': out['m_a_q_norm_g'], 'm_a_w_uq': out['m_a_w_uq'], 'm_a_kv_norm_g': out['m_a_kv_norm_g'], 'm_a_w_ukv': out['m_a_w_ukv'], 'm_b_rel_bias': out['m_b_rel_bias'], 'm_c_forget_b': out['m_c_forget_b'], 'm_w_out': out['m_w_out'], 'm_final_g': out['m_final_g'], 'v_w_ada': out['v_w_ada'], 'v_b_ada': out['v_b_ada'], 'v_norm_g': out['v_norm_g'], 'v_w_in': out['v_w_in'], 'v_a_q_norm_g': out['v_a_q_norm_g'], 'v_a_w_uq': out['v_a_w_uq'], 'v_a_kv_norm_g': out['v_a_kv_norm_g'], 'v_a_w_ukv': out['v_a_w_ukv'], 'v_b_rel_bias': out['v_b_rel_bias'], 'v_c_forget_b': out['v_c_forget_b'], 'v_w_out': out['v_w_out'], 'v_final_g': out['v_final_g']}


def _loss(weights, diff, rest, loss_target):
    with _jax.named_scope("forward"):
        args = {**rest, TWIN_DIFF_INPUT: diff, **{k: w.astype(_WEIGHT_DTYPES[k]) for k, w in weights.items()}}
        y = _forward(args)
    with _jax.named_scope("loss_head"):
        err = _jnp.square(y.astype(_jnp.float32) - loss_target)
        return 0.5 * _jnp.sum(_jnp.mean(err, axis=-1)) if err.ndim else 0.5 * err


def _adamw(w, g, m, v):
    m = ADAM_B1 * m + (1.0 - ADAM_B1) * g
    v = ADAM_B2 * v + (1.0 - ADAM_B2) * _jnp.square(g)
    m_hat = m / (1.0 - ADAM_B1 ** ADAM_STEP)
    v_hat = v / (1.0 - ADAM_B2 ** ADAM_STEP)
    delta = -ADAM_LR * (m_hat / (_jnp.sqrt(v_hat) + ADAM_EPS) + ADAM_WD * w)
    return delta, m, v


def reference(x, c, positions, w_ada, b_ada, norm_g, w_in, a_q_norm_g, a_w_uq, a_kv_norm_g, a_w_ukv, b_rel_bias, c_forget_b, w_out, final_g, loss_target, m_w_ada, m_b_ada, m_norm_g, m_w_in, m_a_q_norm_g, m_a_w_uq, m_a_kv_norm_g, m_a_w_ukv, m_b_rel_bias, m_c_forget_b, m_w_out, m_final_g, v_w_ada, v_b_ada, v_norm_g, v_w_in, v_a_q_norm_g, v_a_w_uq, v_a_kv_norm_g, v_a_w_ukv, v_b_rel_bias, v_c_forget_b, v_w_out, v_final_g):
    given = dict(x=x, c=c, positions=positions, w_ada=w_ada, b_ada=b_ada, norm_g=norm_g, w_in=w_in, a_q_norm_g=a_q_norm_g, a_w_uq=a_w_uq, a_kv_norm_g=a_kv_norm_g, a_w_ukv=a_w_ukv, b_rel_bias=b_rel_bias, c_forget_b=c_forget_b, w_out=w_out, final_g=final_g, loss_target=loss_target, m_w_ada=m_w_ada, m_b_ada=m_b_ada, m_norm_g=m_norm_g, m_w_in=m_w_in, m_a_q_norm_g=m_a_q_norm_g, m_a_w_uq=m_a_w_uq, m_a_kv_norm_g=m_a_kv_norm_g, m_a_w_ukv=m_a_w_ukv, m_b_rel_bias=m_b_rel_bias, m_c_forget_b=m_c_forget_b, m_w_out=m_w_out, m_final_g=m_final_g, v_w_ada=v_w_ada, v_b_ada=v_b_ada, v_norm_g=v_norm_g, v_w_in=v_w_in, v_a_q_norm_g=v_a_q_norm_g, v_a_w_uq=v_a_w_uq, v_a_kv_norm_g=v_a_kv_norm_g, v_a_w_ukv=v_a_w_ukv, v_b_rel_bias=v_b_rel_bias, v_c_forget_b=v_c_forget_b, v_w_out=v_w_out, v_final_g=v_final_g)
    weights = {n: given[n] for n in TWIN_WEIGHTS}
    shared = {n: given[n] for n in SHARED_INPUTS}
    per_example = {n: given[n] for n in ['x', 'c', 'positions']}
    grad_fn = _jax.value_and_grad(_loss, argnums=(0, 1))

    def one_microbatch(ex, loss_target):
        ex = dict(ex)
        diff = ex.pop(TWIN_DIFF_INPUT)
        return grad_fn(weights, diff, {**shared, **ex}, loss_target)

    if N_MICROBATCH == 1:
        loss, (grad_w, grad_x) = one_microbatch(per_example, given["loss_target"])
    else:
        def body(carry, xs):
            loss_sum, grad_sum = carry
            l_k, (gw_k, gx_k) = one_microbatch(xs[0], xs[1])
            with _jax.named_scope("update"):
                return (loss_sum + l_k, _jax.tree.map(_jnp.add, grad_sum, gw_k)), gx_k

        init = (_jnp.zeros((), _jnp.float32), _jax.tree.map(_jnp.zeros_like, weights))
        (loss, grad_w), grad_x = _jax.lax.scan(body, init, (per_example, given["loss_target"]))
    with _jax.named_scope("update"):
        delta_w, new_m, new_v = {}, {}, {}
        for n in TWIN_WEIGHTS:
            delta_w[n], new_m[n], new_v[n] = _adamw(weights[n], grad_w[n], given["m_" + n], given["v_" + n])
    return (loss, grad_x, *[grad_w[n] for n in TWIN_WEIGHTS], *[delta_w[n] for n in TWIN_WEIGHTS],
            *[new_m[n] for n in TWIN_WEIGHTS], *[new_v[n] for n in TWIN_WEIGHTS])
```

```python
import functools

import jax
import jax.numpy as jnp
from jax import lax
from jax.experimental import pallas as pl
from jax.experimental.pallas import tpu as pltpu

F32 = jnp.float32
BF16 = jnp.bfloat16
HI = lax.Precision.HIGHEST

N_DEV = 8
AXES = ("x", "y", "c")
D_MODEL = 1024
DEPTH = 2
CHUNK = 64
EPS = 1e-6
NEG = -1e30
A_HEADS = 6
A_NOPE = 64
A_ROPE = 32
A_Q_RANK = 384
A_KV_RANK = 256
ROPE_THETA = 10000.0
B_HEADS = 5
B_LEFT = 512
REL_CLIP = 128
N_REL = 2 * REL_CLIP + 1
C_HEADS = 5
HEAD_PAD = 128
GW = 384
N_IN = 3621
ADAM_LR = 0.001
ADAM_B1 = 0.9
ADAM_B2 = 0.999
ADAM_EPS = 1e-08
ADAM_WD = 0.01
ADAM_STEP = 10
VMEM_LIMIT = 56 * 1024 * 1024

Z_SEGS = (
    ("cq", 0, 384, F32), ("ckv", 384, 256, F32), ("kpe", 640, 128, F32), ("gates", 768, 1152, F32),
    ("bq", 1920, 384, BF16), ("bk", 2304, 384, BF16), ("bv", 2688, 384, BF16),
    ("cq2", 3072, 384, BF16), ("ck", 3456, 384, BF16), ("cv", 3840, 384, BF16), ("cf", 4224, 128, F32),
)
N_PAD = 4352
IN_RUNS = (
    (0, 384, 0), (384, 256, 384), (640 + 64, 32, 640),
    (768, 384, 672), (768 + 384, 320, 2016), (768 + 768, 320, 3301),
    (1920, 320, 1056), (2304, 320, 1376), (2688, 320, 1696),
    (3072, 320, 2336), (3456, 320, 2656), (3840, 320, 2976), (4224, 5, 3296),
)
OUT_RUNS = ((0, 384, 0), (384, 320, 384), (768, 320, 704))
U_PAD = 1152


def _cparams(sem=None, vmem=VMEM_LIMIT):
    return pltpu.CompilerParams(dimension_semantics=sem, vmem_limit_bytes=vmem)


def _pad_runs(w, runs, total, axis):
    order = sorted(runs)
    parts, pos = [], 0
    for off, wd, src in order:
        if off > pos:
            shp = list(w.shape)
            shp[axis] = off - pos
            parts.append(jnp.zeros(shp, w.dtype))
        parts.append(lax.slice_in_dim(w, src, src + wd, axis=axis))
        pos = off + wd
    if pos < total:
        shp = list(w.shape)
        shp[axis] = total - pos
        parts.append(jnp.zeros(shp, w.dtype))
    return jnp.concatenate(parts, axis=axis)


def _unpad_runs(w, runs, axis):
    order = sorted(runs, key=lambda r: r[2])
    return jnp.concatenate([lax.slice_in_dim(w, off, off + wd, axis=axis) for off, wd, _ in order], axis=axis)


def _sigmoid(x):
    return 1.0 / (1.0 + jnp.exp(-x))


def _exchange(arrs, mode, name):
    n = len(arrs)
    if mode == "gather":
        out_shape = [jax.ShapeDtypeStruct((N_DEV,) + a.shape, a.dtype) for a in arrs]
    else:
        out_shape = [jax.ShapeDtypeStruct(a.shape, a.dtype) for a in arrs]

    def body(*refs):
        ins, outs = refs[:n], refs[n:2 * n]
        send_sems, recv_sems, local_sems = refs[2 * n:]
        x, y, c = lax.axis_index("x"), lax.axis_index("y"), lax.axis_index("c")
        me = 4 * x + 2 * y + c
        copies = []
        for a in range(n):
            src = ins[a] if mode == "gather" else ins[a].at[me]
            loc = pltpu.make_async_copy(src, outs[a].at[me], local_sems.at[a])
            loc.start()
            copies.append(loc)
        for k in range(1, N_DEV):
            px = (1 - x) if (k >> 2) & 1 else x
            py = (1 - y) if (k >> 1) & 1 else y
            pc = (1 - c) if k & 1 else c
            peer = 4 * px + 2 * py + pc
            for a in range(n):
                src = ins[a] if mode == "gather" else ins[a].at[peer]
                cp = pltpu.make_async_remote_copy(
                    src_ref=src, dst_ref=outs[a].at[me],
                    send_sem=send_sems.at[a, k - 1], recv_sem=recv_sems.at[a, k - 1],
                    device_id=(px, py, pc), device_id_type=pl.DeviceIdType.MESH)
                cp.start()
                copies.append(cp)
        for cp in copies:
            cp.wait()

    any_spec = pl.BlockSpec(memory_space=pl.ANY)
    return pl.pallas_call(
        body, name=name, out_shape=out_shape,
        in_specs=[any_spec] * n, out_specs=[any_spec] * n,
        scratch_shapes=[pltpu.SemaphoreType.DMA((n, N_DEV - 1)), pltpu.SemaphoreType.DMA((n, N_DEV - 1)),
                        pltpu.SemaphoreType.DMA((n,))],
    )(*arrs)


def _sum_slots(x, name):
    _, r, c = x.shape

    def body(x_ref, o_ref):
        acc = x_ref[0]
        for j in range(1, N_DEV):
            acc = acc + x_ref[j]
        o_ref[...] = acc

    return pl.pallas_call(body, name=name, out_shape=jax.ShapeDtypeStruct((r, c), F32))(x)


def _ada_fwd(c_all, w_ada):
    nb = c_all.shape[0]
    cols = w_ada.shape[2]

    def body(c_ref, w_ref, act_ref, mod_ref):
        cv = c_ref[...]
        act = cv * _sigmoid(cv)
        act_ref[...] = act
        for l in range(DEPTH):
            mod_ref[l] = jnp.dot(act, w_ref[l], precision=HI, preferred_element_type=F32)

    return pl.pallas_call(
        body, name="ada_fwd",
        out_shape=[jax.ShapeDtypeStruct((nb, D_MODEL), F32), jax.ShapeDtypeStruct((DEPTH, nb, cols), F32)],
        compiler_params=_cparams(),
    )(c_all, w_ada)


def _ada_bwd(c_act, dmod_all, dmod_mine):
    nb = c_act.shape[0]
    cols = dmod_mine.shape[2]

    def body(act_ref, dall_ref, dmine_ref, gw_ref, gb_ref):
        act = act_ref[...]
        for l in range(DEPTH):
            gw_ref[l] = lax.dot_general(act, dmine_ref[l], (((0,), (0,)), ((), ())),
                                        precision=HI, preferred_element_type=F32)
            gb_ref[l:l + 1, :] = jnp.sum(dall_ref[l], axis=0, keepdims=True)

    return pl.pallas_call(
        body, name="ada_bwd",
        out_shape=[jax.ShapeDtypeStruct((DEPTH, D_MODEL, cols), F32),
                   jax.ShapeDtypeStruct((DEPTH, 3 * D_MODEL), F32)],
        compiler_params=_cparams(),
    )(c_act, dmod_all, dmod_mine)


def _ln_in(x, ss, g, w, seq, tm=256):
    t = x.shape[0]
    tps = seq // tm

    def body(x_ref, ss_ref, g_ref, w_ref, h_ref, *outs):
        xv = x_ref[...]
        xn = xv * lax.rsqrt(jnp.mean(xv * xv, axis=-1, keepdims=True) + EPS)
        h = xn * g_ref[...] * ss_ref[0, 1:2, :] + ss_ref[0, 0:1, :]
        hb = h.astype(BF16)
        h_ref[...] = hb
        for o_ref, (_, off, wd, _) in zip(outs, Z_SEGS):
            o_ref[...] = jnp.dot(hb, w_ref[:, off:off + wd], preferred_element_type=F32).astype(o_ref.dtype)

    row = lambda wd: pl.BlockSpec((tm, wd), lambda i: (i, 0))
    return pl.pallas_call(
        body, name="ln_in", grid=(t // tm,),
        in_specs=[row(D_MODEL), pl.BlockSpec((1, 2, D_MODEL), lambda i: (i // tps, 0, 0)),
                  pl.BlockSpec((1, D_MODEL), lambda i: (0, 0)), pl.BlockSpec((D_MODEL, N_PAD), lambda i: (0, 0))],
        out_specs=[row(D_MODEL)] + [row(wd) for _, _, wd, _ in Z_SEGS],
        out_shape=[jax.ShapeDtypeStruct((t, D_MODEL), BF16)]
        + [jax.ShapeDtypeStruct((t, wd), dt) for _, _, wd, dt in Z_SEGS],
        compiler_params=_cparams(("arbitrary",)),
    )(x, ss, g, w)


def _ln_in_bwd(dz, w_t, x, ss, g, dxo, seq, tm=256):
    t = x.shape[0]
    tps = seq // tm
    nb = t // seq
    nz = len(Z_SEGS)

    def body(*refs):
        dz_refs = refs[:nz]
        wt_ref, x_ref, ss_ref, g_ref, dxo_ref, dx_ref, dss_ref, dg_ref = refs[nz:]
        i = pl.program_id(0)
        dh = None
        for r, (_, off, wd, _) in zip(dz_refs, Z_SEGS):
            part = jnp.dot(r[...].astype(BF16), wt_ref[off:off + wd, :], preferred_element_type=F32)
            dh = part if dh is None else dh + part
        xv = x_ref[...]
        rstd = lax.rsqrt(jnp.mean(xv * xv, axis=-1, keepdims=True) + EPS)
        xn = xv * rstd
        gv = g_ref[...]
        s1 = ss_ref[0, 1:2, :]
        dxg = dh * s1
        dxn = dxg * gv
        dx = rstd * (dxn - xn * jnp.mean(dxn * xn, axis=-1, keepdims=True))
        dx_ref[...] = dxo_ref[...] + dx
        dshift = jnp.sum(dh, axis=0, keepdims=True)
        dscale = jnp.sum(dh * (xn * gv), axis=0, keepdims=True)
        dgp = jnp.sum(dxg * xn, axis=0, keepdims=True)

        @pl.when(i % tps == 0)
        def _():
            dss_ref[0, 0:1, :] = dshift
            dss_ref[0, 1:2, :] = dscale

        @pl.when(i % tps != 0)
        def _():
            dss_ref[0, 0:1, :] += dshift
            dss_ref[0, 1:2, :] += dscale

        @pl.when(i == 0)
        def _():
            dg_ref[...] = dgp

        @pl.when(i != 0)
        def _():
            dg_ref[...] += dgp

    row = lambda wd: pl.BlockSpec((tm, wd), lambda i: (i, 0))
    return pl.pallas_call(
        body, name="ln_in_bwd", grid=(t // tm,),
        in_specs=[row(wd) for _, _, wd, _ in Z_SEGS]
        + [pl.BlockSpec((N_PAD, D_MODEL), lambda i: (0, 0)), row(D_MODEL),
           pl.BlockSpec((1, 2, D_MODEL), lambda i: (i // tps, 0, 0)),
           pl.BlockSpec((1, D_MODEL), lambda i: (0, 0)), row(D_MODEL)],
        out_specs=[row(D_MODEL), pl.BlockSpec((1, 2, D_MODEL), lambda i: (i // tps, 0, 0)),
                   pl.BlockSpec((1, D_MODEL), lambda i: (0, 0))],
        out_shape=[jax.ShapeDtypeStruct((t, D_MODEL), F32), jax.ShapeDtypeStruct((nb, 2, D_MODEL), F32),
                   jax.ShapeDtypeStruct((1, D_MODEL), F32)],
        compiler_params=_cparams(("arbitrary",)),
    )(*dz, w_t, x, ss, g, dxo)


def _matmul_tn(a, b, name, tm=512):
    t, k = a.shape
    n = b.shape[1]
    tm = min(tm, t)

    def body(a_ref, b_ref, o_ref):
        i = pl.program_id(0)
        part = lax.dot_general(a_ref[...].astype(BF16), b_ref[...].astype(BF16), (((0,), (0,)), ((), ())),
                               preferred_element_type=F32)

        @pl.when(i == 0)
        def _():
            o_ref[...] = part

        @pl.when(i != 0)
        def _():
            o_ref[...] += part

    return pl.pallas_call(
        body, name=name, grid=(t // tm,),
        in_specs=[pl.BlockSpec((tm, k), lambda i: (i, 0)), pl.BlockSpec((tm, n), lambda i: (i, 0))],
        out_specs=pl.BlockSpec((k, n), lambda i: (0, 0)),
        out_shape=jax.ShapeDtypeStruct((k, n), F32),
        compiler_params=_cparams(("arbitrary",)),
    )(a, b)


def _rope(blk, cos_t, sin_a, sin_b):
    return blk * cos_t + pltpu.roll(blk, 112, 1) * sin_a + pltpu.roll(blk, 16, 1) * sin_b


def _unrope(d, cos_t, sin_a, sin_b):
    return d * cos_t + pltpu.roll(d * sin_a, 16, 1) + pltpu.roll(d * sin_b, 112, 1)


def _mla_prep(cq, ckv, kpe, gq, gkv, wuq, wk, wv, cos_t, sin_a, sin_b, tm=256):
    t = cq.shape[0]
    qw = A_HEADS * HEAD_PAD

    def body(cq_ref, ckv_ref, kpe_ref, gq_ref, gkv_ref, wuq_ref, wk_ref, wv_ref, c_ref, sa_ref, sb_ref,
             q_ref, k_ref, v_ref, cqn_ref, ckvn_ref):
        ct, sa, sb = c_ref[...], sa_ref[...], sb_ref[...]
        a = cq_ref[...]
        cqn = (a * lax.rsqrt(jnp.mean(a * a, axis=-1, keepdims=True) + EPS) * gq_ref[...]).astype(BF16)
        cqn_ref[...] = cqn
        b = ckv_ref[...]
        ckvn = (b * lax.rsqrt(jnp.mean(b * b, axis=-1, keepdims=True) + EPS) * gkv_ref[...]).astype(BF16)
        ckvn_ref[...] = ckvn
        qlin = jnp.dot(cqn, wuq_ref[...], preferred_element_type=F32)
        klin = jnp.dot(ckvn, wk_ref[...], preferred_element_type=F32)
        v_ref[...] = jnp.dot(ckvn, wv_ref[...], preferred_element_type=F32).astype(BF16)
        kr = _rope(kpe_ref[...], ct, sa, sb)
        for h in range(A_HEADS):
            sl = slice(h * HEAD_PAD, (h + 1) * HEAD_PAD)
            q_ref[:, sl] = _rope(qlin[:, sl], ct, sa, sb).astype(BF16)
            k_ref[:, sl] = (klin[:, sl] + kr).astype(BF16)

    row = lambda wd: pl.BlockSpec((tm, wd), lambda i: (i, 0))
    full = lambda r, c: pl.BlockSpec((r, c), lambda i: (0, 0))
    return pl.pallas_call(
        body, name="mla_prep", grid=(t // tm,),
        in_specs=[row(A_Q_RANK), row(A_KV_RANK), row(128), full(1, A_Q_RANK), full(1, A_KV_RANK),
                  full(A_Q_RANK, qw), full(A_KV_RANK, qw), full(A_KV_RANK, GW), row(128), row(128), row(128)],
        out_specs=[row(qw), row(qw), row(GW), row(A_Q_RANK), row(A_KV_RANK)],
        out_shape=[jax.ShapeDtypeStruct((t, qw), BF16), jax.ShapeDtypeStruct((t, qw), BF16),
                   jax.ShapeDtypeStruct((t, GW), BF16), jax.ShapeDtypeStruct((t, A_Q_RANK), BF16),
                   jax.ShapeDtypeStruct((t, A_KV_RANK), BF16)],
        compiler_params=_cparams(("arbitrary",)),
    )(cq, ckv, kpe, gq, gkv, wuq, wk, wv, cos_t, sin_a, sin_b)


def _mla_prep_bwd(dq, dk, dv, cq, ckv, gq, gkv, wuq_t, wk_t, wv_t, cos_t, sin_a, sin_b, tm=256):
    t = cq.shape[0]
    qw = A_HEADS * HEAD_PAD

    def body(dq_ref, dk_ref, dv_ref, cq_ref, ckv_ref, gq_ref, gkv_ref, wuqt_ref, wkt_ref, wvt_ref,
             c_ref, sa_ref, sb_ref, dcq_ref, dckv_ref, dkpe_ref, dql_ref, dkl_ref, dgq_ref, dgkv_ref):
        i = pl.program_id(0)
        ct, sa, sb = c_ref[...], sa_ref[...], sb_ref[...]
        lane = lax.broadcasted_iota(jnp.int32, (1, HEAD_PAD), 1)
        nope = lane < A_NOPE
        rope = (lane >= A_NOPE) & (lane < A_NOPE + A_ROPE)
        dksum = None
        for h in range(A_HEADS):
            sl = slice(h * HEAD_PAD, (h + 1) * HEAD_PAD)
            dql_ref[:, sl] = _unrope(dq_ref[:, sl], ct, sa, sb).astype(BF16)
            dkh = dk_ref[:, sl]
            dkl_ref[:, sl] = jnp.where(nope, dkh, 0.0).astype(BF16)
            dksum = dkh if dksum is None else dksum + dkh
        dkpe_ref[...] = jnp.where(rope, _unrope(jnp.where(rope, dksum, 0.0), ct, sa, sb), 0.0)
        dcqn = jnp.dot(dql_ref[...], wuqt_ref[...], preferred_element_type=F32)
        dckvn = (jnp.dot(dkl_ref[...], wkt_ref[...], preferred_element_type=F32)
                 + jnp.dot(dv_ref[...].astype(BF16), wvt_ref[...], preferred_element_type=F32))

        def norm_bwd(xv, gv, dy):
            rstd = lax.rsqrt(jnp.mean(xv * xv, axis=-1, keepdims=True) + EPS)
            xn = xv * rstd
            dxn = dy * gv
            dx = rstd * (dxn - xn * jnp.mean(dxn * xn, axis=-1, keepdims=True))
            return dx, jnp.sum(dy * xn, axis=0, keepdims=True)

        dcq, dgq = norm_bwd(cq_ref[...], gq_ref[...], dcqn)
        dckv, dgkv = norm_bwd(ckv_ref[...], gkv_ref[...], dckvn)
        dcq_ref[...] = dcq
        dckv_ref[...] = dckv

        @pl.when(i == 0)
        def _():
            dgq_ref[...] = dgq
            dgkv_ref[...] = dgkv

        @pl.when(i != 0)
        def _():
            dgq_ref[...] += dgq
            dgkv_ref[...] += dgkv

    row = lambda wd: pl.BlockSpec((tm, wd), lambda i: (i, 0))
    full = lambda r, c: pl.BlockSpec((r, c), lambda i: (0, 0))
    return pl.pallas_call(
        body, name="mla_prep_bwd", grid=(t // tm,),
        in_specs=[row(qw), row(qw), row(GW), row(A_Q_RANK), row(A_KV_RANK), full(1, A_Q_RANK), full(1, A_KV_RANK),
                  full(qw, A_Q_RANK), full(qw, A_KV_RANK), full(GW, A_KV_RANK), row(128), row(128), row(128)],
        out_specs=[row(A_Q_RANK), row(A_KV_RANK), row(128), row(qw), row(qw), full(1, A_Q_RANK), full(1, A_KV_RANK)],
        out_shape=[jax.ShapeDtypeStruct((t, A_Q_RANK), F32), jax.ShapeDtypeStruct((t, A_KV_RANK), F32),
                   jax.ShapeDtypeStruct((t, 128), F32), jax.ShapeDtypeStruct((t, qw), BF16),
                   jax.ShapeDtypeStruct((t, qw), BF16), jax.ShapeDtypeStruct((1, A_Q_RANK), F32),
                   jax.ShapeDtypeStruct((1, A_KV_RANK), F32)],
        compiler_params=_cparams(("arbitrary",)),
    )(dq, dk, dv, cq, ckv, gq, gkv, wuq_t, wk_t, wv_t, cos_t, sin_a, sin_b)


def _nt(a, b):
    return lax.dot_general(a, b, (((1,), (1,)), ((), ())), preferred_element_type=F32)


def _tn(a, b):
    return lax.dot_general(a, b, (((0,), (0,)), ((), ())), preferred_element_type=F32)


def _causal_mask(kind, q0, k0, tq, tk):
    qpos = q0 + lax.broadcasted_iota(jnp.int32, (tq, tk), 0)
    kpos = k0 + lax.broadcasted_iota(jnp.int32, (tq, tk), 1)
    if kind == "mla":
        return lax.shift_right_logical(kpos, 6) <= lax.shift_right_logical(qpos, 6)
    return kpos <= qpos


def _attn_fwd(kind, q, k, v, f, seq, scale, tq=256, tk=256):
    t = v.shape[0]
    nb = t // seq
    nq = seq // tq
    hw = 256 if kind == "mla" else 128
    use_f = f is not None

    def body(*refs):
        if use_f:
            q_ref, k_ref, v_ref, f_ref, o_ref, st_ref = refs
        else:
            q_ref, k_ref, v_ref, o_ref, st_ref = refs
        qi = pl.program_id(2)
        q0 = qi * tq
        lane = lax.broadcasted_iota(jnp.int32, (1, 128), 1)
        half = lane >= 64
        qall = q_ref[...]
        outs, lses = [], []
        for j in range(2):
            if kind == "mla":
                qh = qall[:, j * 128:(j + 1) * 128]
            else:
                qh = jnp.where(half == bool(j), qall, jnp.zeros_like(qall))

            def kstep(kb, carry, qh=qh, j=j):
                m, l, acc = carry
                k0 = pl.multiple_of(kb * tk, tk)
                if kind == "mla":
                    kh = k_ref[pl.ds(k0, tk), j * 128:(j + 1) * 128]
                else:
                    kh = k_ref[pl.ds(k0, tk), :]
                s = _nt(qh, kh) * scale
                if use_f:
                    s = s - f_ref[0, 0, j:j + 1, pl.ds(k0, tk)]
                s = jnp.where(_causal_mask(kind, q0, k0, tq, tk), s, NEG)
                mn = jnp.maximum(m, jnp.max(s, axis=-1, keepdims=True))
                alpha = jnp.exp(m - mn)
                p = jnp.exp(s - mn)
                l = alpha * l + jnp.sum(p, axis=-1, keepdims=True)
                acc = alpha * acc + jnp.dot(p.astype(BF16), v_ref[pl.ds(k0, tk), :], preferred_element_type=F32)
                return mn, l, acc

            init = (jnp.full((tq, 1), NEG, F32), jnp.zeros((tq, 1), F32), jnp.zeros((tq, 128), F32))
            m, l, acc = lax.fori_loop(0, (qi + 1) * (tq // tk), kstep, init)
            outs.append(acc / l)
            lses.append(m + jnp.log(l))
        o_ref[...] = jnp.where(half, outs[1], outs[0])
        st_ref[...] = jnp.where(lane == 0, lses[0], jnp.where(lane == 1, lses[1], 0.0))

    in_specs = [pl.BlockSpec((tq, hw), lambda b, p, i: (b * nq + i, p)),
                pl.BlockSpec((seq, hw), lambda b, p, i: (b, p)),
                pl.BlockSpec((seq, 128), lambda b, p, i: (b, p))]
    args = [q, k, v]
    if use_f:
        in_specs.append(pl.BlockSpec((1, 1, 8, seq), lambda b, p, i: (b, p, 0, 0)))
        args.append(f)
    oblk = pl.BlockSpec((tq, 128), lambda b, p, i: (b * nq + i, p))
    return pl.pallas_call(
        body, name="attn_fwd_" + kind, grid=(nb, 3, nq), in_specs=in_specs, out_specs=[oblk, oblk],
        out_shape=[jax.ShapeDtypeStruct((t, GW), F32), jax.ShapeDtypeStruct((t, GW), F32)],
        compiler_params=_cparams(("arbitrary", "arbitrary", "arbitrary")),
    )(*args)


def _attn_bwd(kind, q, k, v, f, o, st, do, seq, scale, tq=256, tk=256):
    t = v.shape[0]
    nb = t // seq
    nq = seq // tq
    nk = seq // tk
    hw = 256 if kind == "mla" else 128
    use_f = f is not None

    def body(*refs):
        if use_f:
            q_ref, k_ref, v_ref, f_ref, o_ref, st_ref, do_ref, dq_ref, dk_ref, dv_ref, df_ref, dfq_ref = refs
        else:
            q_ref, k_ref, v_ref, o_ref, st_ref, do_ref, dq_ref, dk_ref, dv_ref = refs
        kj = pl.program_id(2)
        k0 = kj * tk
        lane = lax.broadcasted_iota(jnp.int32, (1, 128), 1)
        half = lane >= 64

        @pl.when(kj == 0)
        def _():
            dq_ref[...] = jnp.zeros_like(dq_ref)
            if use_f:
                dfq_ref[...] = jnp.zeros_like(dfq_ref)

        dk_ref[...] = jnp.zeros_like(dk_ref)
        dv_ref[...] = jnp.zeros_like(dv_ref)
        if use_f:
            df_ref[...] = jnp.zeros_like(df_ref)
        vv = v_ref[...]

        def qstep(qi, carry):
            q0 = pl.multiple_of(qi * tq, tq)
            rows = pl.ds(q0, tq)
            dov = do_ref[rows, :]
            dd = dov * o_ref[rows, :]
            stv = st_ref[rows, :]
            mask = _causal_mask(kind, q0, k0, tq, tk)
            for j in range(2):
                hm = half == bool(j)
                delta = jnp.sum(jnp.where(hm, dd, 0.0), axis=-1, keepdims=True)
                lse = stv[:, j:j + 1]
                if kind == "mla":
                    cols = slice(j * 128, (j + 1) * 128)
                    qh = q_ref[rows, cols]
                    kh = k_ref[:, cols]
                else:
                    cols = slice(0, 128)
                    qa = q_ref[rows, :]
                    qh = jnp.where(hm, qa, jnp.zeros_like(qa))
                    kh = k_ref[...]
                s = _nt(qh, kh) * scale
                if use_f:
                    s = s - f_ref[0, 0, j:j + 1, :]
                p = jnp.exp(jnp.where(mask, s, NEG) - lse)
                doh = jnp.where(hm, dov, 0.0).astype(BF16)
                ds = p * (_nt(doh, vv) - delta)
                dsb = (ds * scale).astype(BF16)
                dv_ref[...] += _tn(p.astype(BF16), doh)
                dk_ref[:, cols] += _tn(dsb, qh)
                dqc = jnp.dot(dsb, kh, preferred_element_type=F32)
                if kind != "mla":
                    dqc = jnp.where(hm, dqc, 0.0)
                dq_ref[rows, cols] += dqc
                if use_f:
                    df_ref[0, 0, j:j + 1, :] += -jnp.sum(ds, axis=0, keepdims=True)
                    dfq_ref[rows, :] += jnp.where(lane == j, jnp.sum(ds, axis=-1, keepdims=True), 0.0)
            return carry

        lax.fori_loop(kj * (tk // tq), nq, qstep, 0)

    full_q = lambda wd: pl.BlockSpec((seq, wd), lambda b, p, i: (b, p))
    kblk = lambda wd: pl.BlockSpec((tk, wd), lambda b, p, i: (b * nk + i, p))
    in_specs = [full_q(hw), kblk(hw), kblk(128)]
    args = [q, k, v]
    if use_f:
        in_specs.append(pl.BlockSpec((1, 1, 8, tk), lambda b, p, i: (b, p, 0, i)))
        args.append(f)
    in_specs += [full_q(128), full_q(128), full_q(128)]
    args += [o, st, do]
    out_specs = [full_q(hw), kblk(hw), kblk(128)]
    out_shape = [jax.ShapeDtypeStruct((t, 3 * hw), F32), jax.ShapeDtypeStruct((t, 3 * hw), F32),
                 jax.ShapeDtypeStruct((t, GW), F32)]
    if use_f:
        out_specs += [pl.BlockSpec((1, 1, 8, tk), lambda b, p, i: (b, p, 0, i)), full_q(128)]
        out_shape += [jax.ShapeDtypeStruct((nb, 3, 8, seq), F32), jax.ShapeDtypeStruct((t, GW), F32)]
    return pl.pallas_call(
        body, name="attn_bwd_" + kind, grid=(nb, 3, nk), in_specs=in_specs, out_specs=out_specs,
        out_shape=out_shape, compiler_params=_cparams(("arbitrary", "arbitrary", "arbitrary")),
    )(*args)


BQ = 128
BWIN = BQ + B_LEFT


def _band_index():
    r = lax.broadcasted_iota(jnp.int32, (BQ, BWIN), 0)
    j = lax.broadcasted_iota(jnp.int32, (BQ, BWIN), 1)
    idx = jnp.clip(r + B_LEFT - j, -REL_CLIP, REL_CLIP) + REL_CLIP
    rc = lax.shift_right_logical(r, 6)
    jc = lax.shift_right_logical(j, 6)
    allowed = (jc - 8 <= rc) & (rc <= jc)
    return idx, allowed


IDX_LO = REL_CLIP - (CHUNK - 1)


def _band_table(rel_bias):
    def body(b_ref, o_ref):
        hh = jnp.minimum(pl.program_id(0), B_HEADS - 1)
        idx, allowed = _band_index()

        def step(kk, tbl):
            return jnp.where(idx == kk, b_ref[hh, kk], tbl)

        tbl = lax.fori_loop(IDX_LO, N_REL, step, jnp.full((BQ, BWIN), NEG, F32))
        o_ref[0] = jnp.where(allowed, tbl, NEG)

    return pl.pallas_call(
        body, name="band_table", grid=(6,),
        in_specs=[pl.BlockSpec(memory_space=pltpu.SMEM)],
        out_specs=pl.BlockSpec((1, BQ, BWIN), lambda h: (h, 0, 0)),
        out_shape=jax.ShapeDtypeStruct((6, BQ, BWIN), F32),
        compiler_params=_cparams(("arbitrary",)),
    )(rel_bias)


def _band_table_bwd(gtab):
    def body(g_ref, o_ref):
        idx, _ = _band_index()
        gv = g_ref[0]
        lane = lax.broadcasted_iota(jnp.int32, (8, GW), 1)

        def step(kk, acc):
            sel = jnp.where(idx == kk, gv, 0.0)
            val = jnp.sum(jnp.sum(sel, axis=-1, keepdims=True), axis=0, keepdims=True)
            return jnp.where(lane == kk, val, acc)

        o_ref[0] = lax.fori_loop(IDX_LO, N_REL, step, jnp.zeros((8, GW), F32))

    return pl.pallas_call(
        body, name="band_table_bwd", grid=(B_HEADS,),
        in_specs=[pl.BlockSpec((1, BQ, BWIN), lambda h: (h, 0, 0))],
        out_specs=pl.BlockSpec((1, 8, GW), lambda h: (h, 0, 0)),
        out_shape=jax.ShapeDtypeStruct((B_HEADS, 8, GW), F32),
        compiler_params=_cparams(("arbitrary",)),
    )(gtab)


def _band_fwd(q, k, v, table, seq, scale):
    t = q.shape[0]
    nb = t // seq
    nq = seq // BQ

    def body(q_ref, k_ref, v_ref, tb_ref, o_ref, st_ref, kpad, vpad):
        qi = pl.program_id(2)
        q0 = pl.multiple_of(qi * BQ, BQ)
        lane = lax.broadcasted_iota(jnp.int32, (1, 128), 1)
        half = lane >= 64

        @pl.when(qi == 0)
        def _():
            kpad[0:B_LEFT, :] = jnp.zeros((B_LEFT, 128), BF16)
            vpad[0:B_LEFT, :] = jnp.zeros((B_LEFT, 128), BF16)
            kpad[B_LEFT:, :] = k_ref[...]
            vpad[B_LEFT:, :] = v_ref[...]

        kw = kpad[pl.ds(q0, BWIN), :]
        vw = vpad[pl.ds(q0, BWIN), :]
        inside = lax.broadcasted_iota(jnp.int32, (BQ, BWIN), 1) >= B_LEFT - q0
        qall = q_ref[...]
        outs, lses = [], []
        for j in range(2):
            qh = jnp.where(half == bool(j), qall, jnp.zeros_like(qall))
            s = jnp.where(inside, _nt(qh, kw) * scale + tb_ref[j], NEG)
            m = jnp.max(s, axis=-1, keepdims=True)
            p = jnp.exp(s - m)
            l = jnp.sum(p, axis=-1, keepdims=True)
            outs.append(jnp.dot(p.astype(BF16), vw, preferred_element_type=F32) / l)
            lses.append(m + jnp.log(l))
        o_ref[...] = jnp.where(half, outs[1], outs[0])
        st_ref[...] = jnp.where(lane == 0, lses[0], jnp.where(lane == 1, lses[1], 0.0))

    qblk = pl.BlockSpec((BQ, 128), lambda b, p, i: (b * nq + i, p))
    full = pl.BlockSpec((seq, 128), lambda b, p, i: (b, p))
    return pl.pallas_call(
        body, name="band_fwd", grid=(nb, 3, nq),
        in_specs=[qblk, full, full, pl.BlockSpec((2, BQ, BWIN), lambda b, p, i: (p, 0, 0))],
        out_specs=[qblk, qblk],
        out_shape=[jax.ShapeDtypeStruct((t, GW), F32), jax.ShapeDtypeStruct((t, GW), F32)],
        scratch_shapes=[pltpu.VMEM((seq + B_LEFT, 128), BF16), pltpu.VMEM((seq + B_LEFT, 128), BF16)],
        compiler_params=_cparams(("arbitrary", "arbitrary", "arbitrary")),
    )(q, k, v, table)


def _band_bwd(q, k, v, table, o, st, do, seq, scale):
    t = q.shape[0]
    nb = t // seq
    nq = seq // BQ

    def body(q_ref, k_ref, v_ref, tb_ref, o_ref, st_ref, do_ref, dq_ref, dk_ref, dv_ref, g_ref,
             kpad, vpad, dkpad, dvpad):
        b = pl.program_id(1)
        qi = pl.program_id(2)
        q0 = pl.multiple_of(qi * BQ, BQ)
        lane = lax.broadcasted_iota(jnp.int32, (1, 128), 1)
        half = lane >= 64

        @pl.when(qi == 0)
        def _():
            kpad[0:B_LEFT, :] = jnp.zeros((B_LEFT, 128), BF16)
            vpad[0:B_LEFT, :] = jnp.zeros((B_LEFT, 128), BF16)
            kpad[B_LEFT:, :] = k_ref[...]
            vpad[B_LEFT:, :] = v_ref[...]
            dkpad[...] = jnp.zeros_like(dkpad)
            dvpad[...] = jnp.zeros_like(dvpad)

        @pl.when((qi == 0) & (b == 0))
        def _():
            g_ref[...] = jnp.zeros_like(g_ref)

        win = pl.ds(q0, BWIN)
        kw = kpad[win, :]
        vw = vpad[win, :]
        inside = lax.broadcasted_iota(jnp.int32, (BQ, BWIN), 1) >= B_LEFT - q0
        qall = q_ref[...]
        dov = do_ref[...]
        dd = dov * o_ref[...]
        stv = st_ref[...]
        dq = jnp.zeros((BQ, 128), F32)
        for j in range(2):
            hm = half == bool(j)
            qh = jnp.where(hm, qall, jnp.zeros_like(qall))
            delta = jnp.sum(jnp.where(hm, dd, 0.0), axis=-1, keepdims=True)
            s = jnp.where(inside, _nt(qh, kw) * scale + tb_ref[j], NEG)
            p = jnp.exp(s - stv[:, j:j + 1])
            doh = jnp.where(hm, dov, 0.0).astype(BF16)
            ds = p * (_nt(doh, vw) - delta)
            g_ref[j] += ds
            dsb = (ds * scale).astype(BF16)
            dvpad[win, :] += _tn(p.astype(BF16), doh)
            dkpad[win, :] += _tn(dsb, qh)
            dq = dq + jnp.where(hm, jnp.dot(dsb, kw, preferred_element_type=F32), 0.0)
        dq_ref[...] = dq

        @pl.when(qi == nq - 1)
        def _():
            dk_ref[...] = dkpad[B_LEFT:, :]
            dv_ref[...] = dvpad[B_LEFT:, :]

    qblk = pl.BlockSpec((BQ, 128), lambda p, b, i: (b * nq + i, p))
    full = pl.BlockSpec((seq, 128), lambda p, b, i: (b, p))
    tblk = pl.BlockSpec((2, BQ, BWIN), lambda p, b, i: (p, 0, 0))
    return pl.pallas_call(
        body, name="band_bwd", grid=(3, nb, nq),
        in_specs=[qblk, full, full, tblk, qblk, qblk, qblk],
        out_specs=[qblk, full, full, tblk],
        out_shape=[jax.ShapeDtypeStruct((t, GW), F32), jax.ShapeDtypeStruct((t, GW), F32),
                   jax.ShapeDtypeStruct((t, GW), F32), jax.ShapeDtypeStruct((6, BQ, BWIN), F32)],
        scratch_shapes=[pltpu.VMEM((seq + B_LEFT, 128), BF16), pltpu.VMEM((seq + B_LEFT, 128), BF16),
                        pltpu.VMEM((seq + B_LEFT, 128), F32), pltpu.VMEM((seq + B_LEFT, 128), F32)],
        compiler_params=_cparams(("arbitrary", "arbitrary", "arbitrary")),
    )(q, k, v, table, o, st, do)


def _fox_prep(cf, fb, seq):
    nb = cf.shape[0] // seq
    nblk = seq // 128

    def body(cf_ref, fb_ref, f_ref):
        x = cf_ref[...] + fb_ref[...]
        lf = jnp.minimum(x, 0.0) - jnp.log1p(jnp.exp(-jnp.abs(x)))
        rows = lf.T[0:8, :]
        upper = (lax.broadcasted_iota(jnp.int32, (128, 128), 0)
                 <= lax.broadcasted_iota(jnp.int32, (128, 128), 1)).astype(F32)
        carry = jnp.zeros((8, 1), F32)
        for blk in range(nblk):
            sl = slice(blk * 128, (blk + 1) * 128)
            cs = jnp.dot(rows[:, sl], upper, precision=HI, preferred_element_type=F32) + carry
            carry = cs[:, 127:128]
            f_ref[0, 0, :, sl] = cs
            f_ref[0, 1, :, sl] = pltpu.roll(cs, 6, 0)
            f_ref[0, 2, :, sl] = pltpu.roll(cs, 4, 0)

    return pl.pallas_call(
        body, name="fox_prep", grid=(nb,),
        in_specs=[pl.BlockSpec((seq, 128), lambda b: (b, 0)), pl.BlockSpec((1, 128), lambda b: (0, 0))],
        out_specs=pl.BlockSpec((1, 3, 8, seq), lambda b: (b, 0, 0, 0)),
        out_shape=jax.ShapeDtypeStruct((nb, 3, 8, seq), F32),
        compiler_params=_cparams(("arbitrary",)),
    )(cf, fb)


def _fox_prep_bwd(df, dfq, cf, fb, seq):
    nb = cf.shape[0] // seq
    nblk = seq // 128

    def body(df_ref, dfq_ref, cf_ref, fb_ref, dcf_ref, dfb_ref, wide):
        b = pl.program_id(0)
        row = lax.broadcasted_iota(jnp.int32, (8, seq), 0)
        dfh = None
        for p in range(3):
            both = df_ref[0, p] + dfq_ref[:, p * 128:(p + 1) * 128].T[0:8, :]
            both = jnp.where(row < 2, both, 0.0)
            if p:
                both = pltpu.roll(both, 2 * p, 0)
            dfh = both if dfh is None else dfh + both
        lower = (lax.broadcasted_iota(jnp.int32, (128, 128), 0)
                 >= lax.broadcasted_iota(jnp.int32, (128, 128), 1)).astype(F32)
        wide[...] = jnp.zeros_like(wide)
        carry = jnp.zeros((8, 1), F32)
        for blk in reversed(range(nblk)):
            sl = slice(blk * 128, (blk + 1) * 128)
            rc = jnp.dot(dfh[:, sl], lower, precision=HI, preferred_element_type=F32) + carry
            carry = rc[:, 0:1]
            wide[0:8, sl] = rc
        dl = wide[...].T
        x = cf_ref[...] + fb_ref[...]
        dcf = dl * (1.0 / (1.0 + jnp.exp(x)))
        dcf_ref[...] = dcf
        part = jnp.sum(dcf, axis=0, keepdims=True)

        @pl.when(b == 0)
        def _():
            dfb_ref[...] = part

        @pl.when(b != 0)
        def _():
            dfb_ref[...] += part

    return pl.pallas_call(
        body, name="fox_prep_bwd", grid=(nb,),
        in_specs=[pl.BlockSpec((1, 3, 8, seq), lambda b: (b, 0, 0, 0)), pl.BlockSpec((seq, GW), lambda b: (b, 0)),
                  pl.BlockSpec((seq, 128), lambda b: (b, 0)), pl.BlockSpec((1, 128), lambda b: (0, 0))],
        out_specs=[pl.BlockSpec((seq, 128), lambda b: (b, 0)), pl.BlockSpec((1, 128), lambda b: (0, 0))],
        out_shape=[jax.ShapeDtypeStruct(cf.shape, F32), jax.ShapeDtypeStruct((1, 128), F32)],
        scratch_shapes=[pltpu.VMEM((128, seq), F32)],
        compiler_params=_cparams(("arbitrary",)),
    )(df, dfq, cf, fb)


def _gate_out(oa, ob, oc, gates, w, x, gate, seq, tm=256):
    t = x.shape[0]
    tps = seq // tm

    def body(oa_ref, ob_ref, oc_ref, g_ref, w_ref, x_ref, gt_ref, xo_ref, y_ref, u_ref):
        for n, o_ref in enumerate((oa_ref, ob_ref, oc_ref)):
            sl = slice(n * GW, (n + 1) * GW)
            gv = g_ref[:, sl]
            u_ref[:, sl] = (o_ref[...] * (gv * _sigmoid(gv))).astype(BF16)
        y = jnp.dot(u_ref[...], w_ref[...], preferred_element_type=F32)
        y_ref[...] = y
        xo_ref[...] = x_ref[...] + gt_ref[0] * y

    row = lambda wd: pl.BlockSpec((tm, wd), lambda i: (i, 0))
    return pl.pallas_call(
        body, name="gate_out", grid=(t // tm,),
        in_specs=[row(GW), row(GW), row(GW), row(U_PAD), pl.BlockSpec((U_PAD, D_MODEL), lambda i: (0, 0)),
                  row(D_MODEL), pl.BlockSpec((1, 1, D_MODEL), lambda i: (i // tps, 0, 0))],
        out_specs=[row(D_MODEL), row(D_MODEL), row(U_PAD)],
        out_shape=[jax.ShapeDtypeStruct((t, D_MODEL), F32), jax.ShapeDtypeStruct((t, D_MODEL), F32),
                   jax.ShapeDtypeStruct((t, U_PAD), BF16)],
        compiler_params=_cparams(("arbitrary",)),
    )(oa, ob, oc, gates, w, x, gate)


def _gate_out_bwd(dxo, y, gate, oa, ob, oc, gates, w_t, seq, tm=256):
    t = dxo.shape[0]
    tps = seq // tm
    nb = t // seq

    def body(dxo_ref, y_ref, gt_ref, oa_ref, ob_ref, oc_ref, g_ref, wt_ref,
             dy_ref, doa_ref, dob_ref, doc_ref, dg_ref, dgt_ref):
        i = pl.program_id(0)
        dxo_v = dxo_ref[...]
        dgt = jnp.sum(dxo_v * y_ref[...], axis=0, keepdims=True)
        dyb = (dxo_v * gt_ref[0]).astype(BF16)
        dy_ref[...] = dyb
        du = jnp.dot(dyb, wt_ref[...], preferred_element_type=F32)
        for n, (o_ref, do_ref) in enumerate(((oa_ref, doa_ref), (ob_ref, dob_ref), (oc_ref, doc_ref))):
            sl = slice(n * GW, (n + 1) * GW)
            gv = g_ref[:, sl]
            sg = _sigmoid(gv)
            dun = du[:, sl]
            do_ref[...] = dun * (gv * sg)
            dg_ref[:, sl] = dun * o_ref[...] * (sg * (1.0 + gv * (1.0 - sg)))

        @pl.when(i % tps == 0)
        def _():
            dgt_ref[0] = dgt

        @pl.when(i % tps != 0)
        def _():
            dgt_ref[0] += dgt

    row = lambda wd: pl.BlockSpec((tm, wd), lambda i: (i, 0))
    per_b = pl.BlockSpec((1, 1, D_MODEL), lambda i: (i // tps, 0, 0))
    return pl.pallas_call(
        body, name="gate_out_bwd", grid=(t // tm,),
        in_specs=[row(D_MODEL), row(D_MODEL), per_b, row(GW), row(GW), row(GW), row(U_PAD),
                  pl.BlockSpec((D_MODEL, U_PAD), lambda i: (0, 0))],
        out_specs=[row(D_MODEL), row(GW), row(GW), row(GW), row(U_PAD), per_b],
        out_shape=[jax.ShapeDtypeStruct((t, D_MODEL), BF16), jax.ShapeDtypeStruct((t, GW), F32),
                   jax.ShapeDtypeStruct((t, GW), F32), jax.ShapeDtypeStruct((t, GW), F32),
                   jax.ShapeDtypeStruct((t, U_PAD), F32), jax.ShapeDtypeStruct((nb, 1, D_MODEL), F32)],
        compiler_params=_cparams(("arbitrary",)),
    )(dxo, y, gate, oa, ob, oc, gates, w_t)


def _final_loss(x, target, g, tm=256):
    t = x.shape[0]

    def body(x_ref, t_ref, g_ref, dx_ref, loss_ref, dg_ref):
        i = pl.program_id(0)
        xv = x_ref[...]
        rstd = lax.rsqrt(jnp.mean(xv * xv, axis=-1, keepdims=True) + EPS)
        xn = xv * rstd
        gv = g_ref[...]
        err = xn * gv - t_ref[...]
        dy = err * (1.0 / D_MODEL)
        dxn = dy * gv
        dx_ref[...] = rstd * (dxn - xn * jnp.mean(dxn * xn, axis=-1, keepdims=True))
        lp = jnp.sum(err * err, axis=0, keepdims=True) * (0.5 / D_MODEL)
        dgp = jnp.sum(dy * xn, axis=0, keepdims=True)

        @pl.when(i == 0)
        def _():
            loss_ref[...] = lp
            dg_ref[...] = dgp

        @pl.when(i != 0)
        def _():
            loss_ref[...] += lp
            dg_ref[...] += dgp

    row = pl.BlockSpec((tm, D_MODEL), lambda i: (i, 0))
    vec = pl.BlockSpec((1, D_MODEL), lambda i: (0, 0))
    return pl.pallas_call(
        body, name="final_loss", grid=(t // tm,),
        in_specs=[row, row, vec], out_specs=[row, vec, vec],
        out_shape=[jax.ShapeDtypeStruct((t, D_MODEL), F32), jax.ShapeDtypeStruct((1, D_MODEL), F32),
                   jax.ShapeDtypeStruct((1, D_MODEL), F32)],
        compiler_params=_cparams(("arbitrary",)),
    )(x, target, g)


def _adamw(w, gslots, m, v, name, tr=None):
    r, c = w.shape
    ns = gslots.shape[0]
    tr = r if tr is None else tr

    def body(w_ref, g_ref, m_ref, v_ref, go_ref, d_ref, mo_ref, vo_ref):
        g = g_ref[0]
        for j in range(1, ns):
            g = g + g_ref[j]
        mn = ADAM_B1 * m_ref[...] + (1.0 - ADAM_B1) * g
        vn = ADAM_B2 * v_ref[...] + (1.0 - ADAM_B2) * jnp.square(g)
        m_hat = mn / (1.0 - ADAM_B1 ** ADAM_STEP)
        v_hat = vn / (1.0 - ADAM_B2 ** ADAM_STEP)
        go_ref[...] = g
        d_ref[...] = -ADAM_LR * (m_hat / (jnp.sqrt(v_hat) + ADAM_EPS) + ADAM_WD * w_ref[...])
        mo_ref[...] = mn
        vo_ref[...] = vn

    blk = pl.BlockSpec((tr, c), lambda i: (i, 0))
    return pl.pallas_call(
        body, name=name, grid=(r // tr,),
        in_specs=[blk, pl.BlockSpec((ns, tr, c), lambda i: (0, i, 0)), blk, blk],
        out_specs=[blk] * 4, out_shape=[jax.ShapeDtypeStruct((r, c), F32)] * 4,
        compiler_params=_cparams(("arbitrary",)),
    )(w, gslots, m, v)


def _rope_tables(positions):
    inv = ROPE_THETA ** (-jnp.arange(0, A_ROPE, 2, dtype=F32) / A_ROPE)
    ang = positions.astype(F32)[:, None] * inv
    cos, sin = jnp.cos(ang), jnp.sin(ang)
    t = positions.shape[0]
    one = jnp.ones((t, 64), F32)
    zero16 = jnp.zeros((t, 16), F32)
    cos_t = jnp.concatenate([one, cos, cos, jnp.ones((t, 32), F32)], axis=1)
    sin_a = jnp.concatenate([jnp.zeros((t, 64), F32), -sin, zero16, jnp.zeros((t, 32), F32)], axis=1)
    sin_b = jnp.concatenate([jnp.zeros((t, 64), F32), zero16, sin, jnp.zeros((t, 32), F32)], axis=1)
    return cos_t, sin_a, sin_b


def _pad_heads(w, real, padded, nheads, axis):
    shp = w.shape[:axis] + (nheads, real) + w.shape[axis + 1:]
    w = w.reshape(shp)
    pad = [(0, 0)] * w.ndim
    pad[axis + 1] = (0, padded - real)
    w = jnp.pad(w, pad)
    return w.reshape(w.shape[:axis] + (nheads * padded,) + w.shape[axis + 2:])


def kernel(x, c, positions, w_ada, b_ada, norm_g, w_in, a_q_norm_g, a_w_uq, a_kv_norm_g, a_w_ukv, b_rel_bias, c_forget_b, w_out, final_g, loss_target, m_w_ada, m_b_ada, m_norm_g, m_w_in, m_a_q_norm_g, m_a_w_uq, m_a_kv_norm_g, m_a_w_ukv, m_b_rel_bias, m_c_forget_b, m_w_out, m_final_g, v_w_ada, v_b_ada, v_norm_g, v_w_in, v_a_q_norm_g, v_a_w_uq, v_a_kv_norm_g, v_a_w_ukv, v_b_rel_bias, v_c_forget_b, v_w_out, v_final_g):
    nb, seq, _ = x.shape
    t = nb * seq
    me = 4 * lax.axis_index("x") + 2 * lax.axis_index("y") + lax.axis_index("c")
    x2 = x.reshape(t, D_MODEL)
    tgt = loss_target.reshape(t, D_MODEL)
    cos_t, sin_a, sin_b = _rope_tables(positions.reshape(t))

    shards = []
    for l in range(DEPTH):
        shards += [w_in[l].astype(BF16), w_out[l].astype(BF16), a_w_uq[l].astype(BF16), a_w_ukv[l].astype(BF16)]
    gathered = _exchange(shards + [c], "gather", "gather_weights")
    c_all = gathered[-1].reshape(N_DEV * nb, D_MODEL)
    w_in_p, w_in_t, w_out_p, w_out_t, wuq_p, wuq_t, wk_p, wk_t, wv_p, wv_t = ([] for _ in range(10))
    for l in range(DEPTH):
        gi, go, gq, gkv = gathered[4 * l:4 * l + 4]
        wi = _pad_runs(gi.reshape(D_MODEL, N_IN), IN_RUNS, N_PAD, 1)
        wo = _pad_runs(go.reshape(D_MODEL, D_MODEL), OUT_RUNS, U_PAD, 0)
        wq = jnp.transpose(gq, (1, 0, 2)).reshape(A_Q_RANK, A_HEADS * (A_NOPE + A_ROPE))
        wq = _pad_heads(wq, A_NOPE + A_ROPE, HEAD_PAD, A_HEADS, 1)
        wkv = jnp.transpose(gkv, (1, 0, 2)).reshape(A_KV_RANK, A_HEADS, 2 * A_NOPE)
        wk = jnp.pad(wkv[:, :, :A_NOPE], ((0, 0), (0, 0), (0, HEAD_PAD - A_NOPE))).reshape(A_KV_RANK, A_HEADS * HEAD_PAD)
        wv = wkv[:, :, A_NOPE:].reshape(A_KV_RANK, GW)
        w_in_p.append(wi); w_in_t.append(wi.T); w_out_p.append(wo); w_out_t.append(wo.T)
        wuq_p.append(wq); wuq_t.append(wq.T); wk_p.append(wk); wk_t.append(wk.T); wv_p.append(wv); wv_t.append(wv.T)

    c_act, mod_cols = _ada_fwd(c_all, w_ada)
    (mod_g,) = _exchange([mod_cols], "gather", "gather_mod")
    mod_all = jnp.transpose(mod_g, (1, 2, 0, 3)).reshape(DEPTH, N_DEV * nb, 3 * D_MODEL)
    mod = lax.dynamic_slice_in_dim(mod_all, me * nb, nb, axis=1) + b_ada[:, None, :]

    fb_pad = jnp.pad(c_forget_b, ((0, 0), (0, 128 - C_HEADS)))
    a_scale = (A_NOPE + A_ROPE) ** -0.5
    h_scale = CHUNK ** -0.5

    saved = []
    xl = x2
    for l in range(DEPTH):
        shift, scale, gate = mod[l, :, :D_MODEL], mod[l, :, D_MODEL:2 * D_MODEL], mod[l, :, 2 * D_MODEL:]
        ss = jnp.stack([shift, 1.0 + scale], axis=1)
        gate3 = gate[:, None, :]
        h, cq, ckv, kpe, gates, bq, bk, bv, cq2, ck, cv, cf = _ln_in(xl, ss, norm_g[l:l + 1], w_in_p[l], seq)
        q, k, v, cqn, ckvn = _mla_prep(cq, ckv, kpe, a_q_norm_g[l:l + 1], a_kv_norm_g[l:l + 1],
                                       wuq_p[l], wk_p[l], wv_p[l], cos_t, sin_a, sin_b)
        oa, sta = _attn_fwd("mla", q, k, v, None, seq, a_scale)
        table = _band_table(b_rel_bias[l])
        ob, stb = _band_fwd(bq, bk, bv, table, seq, h_scale)
        fcum = _fox_prep(cf, fb_pad[l:l + 1], seq)
        oc, stc = _attn_fwd("fox", cq2, ck, cv, fcum, seq, h_scale)
        xn, y, u = _gate_out(oa, ob, oc, gates, w_out_p[l], xl, gate3, seq)
        saved.append(dict(x=xl, ss=ss, gate3=gate3, h=h, cq=cq, ckv=ckv, gates=gates, bq=bq, bk=bk, bv=bv,
                          cq2=cq2, ck=ck, cv=cv, cf=cf, q=q, k=k, v=v, cqn=cqn, ckvn=ckvn, oa=oa, sta=sta,
                          table=table, ob=ob, stb=stb, fcum=fcum, oc=oc, stc=stc, y=y, u=u))
        xl = xn

    dx, loss_lanes, g_final = _final_loss(xl, tgt, final_g[None, :])
    loss = lax.psum(jnp.sum(loss_lanes), AXES)

    g_in, g_out, g_uq, g_ukv, dmods, smalls = [None] * DEPTH, [None] * DEPTH, [None] * DEPTH, [None] * DEPTH, [None] * DEPTH, [None] * DEPTH
    for l in reversed(range(DEPTH)):
        s = saved[l]
        dy, doa, dob, doc, dgates, dgate = _gate_out_bwd(dx, s["y"], s["gate3"], s["oa"], s["ob"], s["oc"],
                                                         s["gates"], w_out_t[l], seq)
        g_out[l] = _unpad_runs(_matmul_tn(s["u"], dy, "dw_out"), OUT_RUNS, 0)
        dq, dk, dv = _attn_bwd("mla", s["q"], s["k"], s["v"], None, s["oa"], s["sta"], doa, seq, a_scale)
        dbq, dbk, dbv, gtab = _band_bwd(s["bq"], s["bk"], s["bv"], s["table"], s["ob"], s["stb"], dob, seq, h_scale)
        g_rel = _band_table_bwd(gtab)[:, 0, :N_REL]
        dcq2, dck, dcv, dfc, dfq = _attn_bwd("fox", s["cq2"], s["ck"], s["cv"], s["fcum"], s["oc"], s["stc"], doc,
                                             seq, h_scale)
        dcf, dfb = _fox_prep_bwd(dfc, dfq, s["cf"], fb_pad[l:l + 1], seq)
        dcq, dckv, dkpe, dqlin, dklin, dgq, dgkv = _mla_prep_bwd(
            dq, dk, dv, s["cq"], s["ckv"], a_q_norm_g[l:l + 1], a_kv_norm_g[l:l + 1],
            wuq_t[l], wk_t[l], wv_t[l], cos_t, sin_a, sin_b)
        gq_pad = _matmul_tn(s["cqn"], dqlin, "dw_uq")
        g_uq[l] = gq_pad.reshape(A_Q_RANK, A_HEADS, HEAD_PAD)[:, :, :A_NOPE + A_ROPE].reshape(A_Q_RANK, -1)
        gk_pad = _matmul_tn(s["ckvn"], dklin, "dw_uk").reshape(A_KV_RANK, A_HEADS, HEAD_PAD)[:, :, :A_NOPE]
        gv_pad = _matmul_tn(s["ckvn"], dv, "dw_uv").reshape(A_KV_RANK, A_HEADS, A_NOPE)
        g_ukv[l] = jnp.concatenate([gk_pad, gv_pad], axis=2).reshape(A_KV_RANK, -1)
        dz = [dcq, dckv, dkpe, dgates, dbq, dbk, dbv, dcq2, dck, dcv, dcf]
        g_in[l] = _unpad_runs(jnp.concatenate([_matmul_tn(s["h"], d, "dw_in_" + nm)
                                               for d, (nm, _, _, _) in zip(dz, Z_SEGS)], axis=1), IN_RUNS, 1)
        dx, dss, dg_norm = _ln_in_bwd(dz, w_in_t[l], s["x"], s["ss"], norm_g[l:l + 1], dx, seq)
        dmods[l] = jnp.concatenate([dss[:, 0, :], dss[:, 1, :], dgate[:, 0, :]], axis=1)
        smalls[l] = [dg_norm.reshape(-1), dgq.reshape(-1), dgkv.reshape(-1), g_rel.reshape(-1),
                     dfb[0, :C_HEADS]]
    grad_x = dx.reshape(nb, seq, D_MODEL)

    small = jnp.concatenate([p for l in range(DEPTH) for p in smalls[l]] + [g_final.reshape(-1)])
    n_small = small.shape[0]
    small_rows = -(-n_small // 1024) * 8
    small = jnp.pad(small, (0, small_rows * 128 - n_small)).reshape(small_rows, 128)
    dmod_local = jnp.stack(dmods)
    dmod_g, small_g = _exchange([dmod_local, small], "gather", "gather_small")
    dmod_all = jnp.transpose(dmod_g, (1, 0, 2, 3)).reshape(DEPTH, N_DEV * nb, 3 * D_MODEL)
    cols = 3 * D_MODEL // N_DEV
    dmod_mine = lax.dynamic_slice_in_dim(dmod_all, me * cols, cols, axis=2)
    g_w_ada, g_b_ada = _ada_bwd(c_act, dmod_all, dmod_mine)
    small_sum = _sum_slots(small_g, "sum_small").reshape(-1)

    rows = D_MODEL // N_DEV
    a2a_in = [jnp.stack(g_in).reshape(DEPTH, N_DEV, rows, N_IN).transpose(1, 0, 2, 3),
              jnp.stack(g_out).reshape(DEPTH, N_DEV, rows, D_MODEL).transpose(1, 0, 2, 3),
              jnp.stack(g_uq).reshape(DEPTH, A_Q_RANK, N_DEV, -1).transpose(2, 0, 1, 3),
              jnp.stack(g_ukv).reshape(DEPTH, A_KV_RANK, N_DEV, -1).transpose(2, 0, 1, 3)]
    p_in, p_out, p_uq, p_ukv = _exchange(a2a_in, "a2a", "a2a_grads")

    def split_small():
        out, pos = [], 0
        sizes = [D_MODEL, A_Q_RANK, A_KV_RANK, B_HEADS * N_REL, C_HEADS]
        per_layer = []
        for l in range(DEPTH):
            parts = []
            for sz in sizes:
                parts.append(small_sum[pos:pos + sz])
                pos += sz
            per_layer.append(parts)
        for j in range(len(sizes)):
            out.append(jnp.stack([per_layer[l][j] for l in range(DEPTH)]))
        out.append(small_sum[pos:pos + D_MODEL])
        return out

    g_norm, g_qn, g_kvn, g_relb, g_fb, g_fin = split_small()

    def adam(w, g, m, v, name, tr=None):
        shp = w.shape
        w2 = w.reshape(-1, shp[-1]) if w.ndim > 1 else w.reshape(1, -1)
        gs = g.reshape((-1,) + w2.shape) if g.size != w.size else g.reshape((1,) + w2.shape)
        outs = _adamw(w2, gs, m.reshape(w2.shape), v.reshape(w2.shape), name, tr)
        return [o.reshape(shp) for o in outs]

    res = {
        "w_ada": adam(w_ada, g_w_ada, m_w_ada, v_w_ada, "adam_w_ada", 256),
        "b_ada": adam(b_ada, g_b_ada, m_b_ada, v_b_ada, "adam_b_ada"),
        "norm_g": adam(norm_g, g_norm, m_norm_g, v_norm_g, "adam_norm_g"),
        "w_in": adam(w_in, p_in, m_w_in, v_w_in, "adam_w_in", 32),
        "a_q_norm_g": adam(a_q_norm_g, g_qn, m_a_q_norm_g, v_a_q_norm_g, "adam_q_norm"),
        "a_w_uq": adam(a_w_uq, p_uq, m_a_w_uq, v_a_w_uq, "adam_w_uq"),
        "a_kv_norm_g": adam(a_kv_norm_g, g_kvn, m_a_kv_norm_g, v_a_kv_norm_g, "adam_kv_norm"),
        "a_w_ukv": adam(a_w_ukv, p_ukv, m_a_w_ukv, v_a_w_ukv, "adam_w_ukv"),
        "b_rel_bias": adam(b_rel_bias, g_relb.reshape(b_rel_bias.shape), m_b_rel_bias, v_b_rel_bias, "adam_rel_bias"),
        "c_forget_b": adam(c_forget_b, g_fb, m_c_forget_b, v_c_forget_b, "adam_forget_b"),
        "w_out": adam(w_out, p_out, m_w_out, v_w_out, "adam_w_out", 64),
        "final_g": adam(final_g, g_fin, m_final_g, v_final_g, "adam_final_g"),
    }
    names = ["w_ada", "b_ada", "norm_g", "w_in", "a_q_norm_g", "a_w_uq", "a_kv_norm_g", "a_w_ukv", "b_rel_bias",
             "c_forget_b", "w_out", "final_g"]
    outs = [loss, grad_x]
    for j in range(4):
        outs += [res[n][j] for n in names]
    return tuple(outs)
```

```python
import functools

import jax
import jax.numpy as jnp
from jax import lax
from jax.experimental import pallas as pl
from jax.experimental.pallas import tpu as pltpu

F32 = jnp.float32
BF16 = jnp.bfloat16
HI = lax.Precision.HIGHEST

N_DEV = 8
AXES = ("x", "y", "c")
D_MODEL = 1024
DEPTH = 2
CHUNK = 64
EPS = 1e-6
NEG = -1e30
A_HEADS = 6
A_NOPE = 64
A_ROPE = 32
A_Q_RANK = 384
A_KV_RANK = 256
ROPE_THETA = 10000.0
B_HEADS = 5
B_LEFT = 512
REL_CLIP = 128
N_REL = 2 * REL_CLIP + 1
C_HEADS = 5
HEAD_PAD = 128
GW = 384
N_IN = 3621
ADAM_LR = 0.001
ADAM_B1 = 0.9
ADAM_B2 = 0.999
ADAM_EPS = 1e-08
ADAM_WD = 0.01
ADAM_STEP = 10
VMEM_LIMIT = 56 * 1024 * 1024

Z_SEGS = (
    ("cq", 0, 384, F32), ("ckv", 384, 256, F32), ("kpe", 640, 128, F32), ("gates", 768, 1152, F32),
    ("bq", 1920, 384, BF16), ("bk", 2304, 384, BF16), ("bv", 2688, 384, BF16),
    ("cq2", 3072, 384, BF16), ("ck", 3456, 384, BF16), ("cv", 3840, 384, BF16), ("cf", 4224, 128, F32),
)
N_PAD = 4352
IN_RUNS = (
    (0, 384, 0), (384, 256, 384), (640 + 64, 32, 640),
    (768, 384, 672), (768 + 384, 320, 2016), (768 + 768, 320, 3301),
    (1920, 320, 1056), (2304, 320, 1376), (2688, 320, 1696),
    (3072, 320, 2336), (3456, 320, 2656), (3840, 320, 2976), (4224, 5, 3296),
)
OUT_RUNS = ((0, 384, 0), (384, 320, 384), (768, 320, 704))
U_PAD = 1152


def _cparams(sem=None, vmem=VMEM_LIMIT):
    return pltpu.CompilerParams(dimension_semantics=sem, vmem_limit_bytes=vmem)


def _pad_runs(w, runs, total, axis):
    order = sorted(runs)
    parts, pos = [], 0
    for off, wd, src in order:
        if off > pos:
            shp = list(w.shape)
            shp[axis] = off - pos
            parts.append(jnp.zeros(shp, w.dtype))
        parts.append(lax.slice_in_dim(w, src, src + wd, axis=axis))
        pos = off + wd
    if pos < total:
        shp = list(w.shape)
        shp[axis] = total - pos
        parts.append(jnp.zeros(shp, w.dtype))
    return jnp.concatenate(parts, axis=axis)


def _unpad_runs(w, runs, axis):
    order = sorted(runs, key=lambda r: r[2])
    return jnp.concatenate([lax.slice_in_dim(w, off, off + wd, axis=axis) for off, wd, _ in order], axis=axis)


def _sigmoid(x):
    return 1.0 / (1.0 + jnp.exp(-x))


def _exchange(arrs, mode, name):
    n = len(arrs)
    if mode == "gather":
        out_shape = [jax.ShapeDtypeStruct((N_DEV,) + a.shape, a.dtype) for a in arrs]
    else:
        out_shape = [jax.ShapeDtypeStruct(a.shape, a.dtype) for a in arrs]

    def body(*refs):
        ins, outs = refs[:n], refs[n:2 * n]
        send_sems, recv_sems, local_sems = refs[2 * n:]
        x, y, c = lax.axis_index("x"), lax.axis_index("y"), lax.axis_index("c")
        me = 4 * x + 2 * y + c
        copies = []
        for a in range(n):
            src = ins[a] if mode == "gather" else ins[a].at[me]
            loc = pltpu.make_async_copy(src, outs[a].at[me], local_sems.at[a])
            loc.start()
            copies.append(loc)
        for k in range(1, N_DEV):
            px = (1 - x) if (k >> 2) & 1 else x
            py = (1 - y) if (k >> 1) & 1 else y
            pc = (1 - c) if k & 1 else c
            peer = 4 * px + 2 * py + pc
            for a in range(n):
                src = ins[a] if mode == "gather" else ins[a].at[peer]
                cp = pltpu.make_async_remote_copy(
                    src_ref=src, dst_ref=outs[a].at[me],
                    send_sem=send_sems.at[a, k - 1], recv_sem=recv_sems.at[a, k - 1],
                    device_id=(px, py, pc), device_id_type=pl.DeviceIdType.MESH)
                cp.start()
                copies.append(cp)
        for cp in copies:
            cp.wait()

    any_spec = pl.BlockSpec(memory_space=pl.ANY)
    return pl.pallas_call(
        body, name=name, out_shape=out_shape,
        in_specs=[any_spec] * n, out_specs=[any_spec] * n,
        scratch_shapes=[pltpu.SemaphoreType.DMA((n, N_DEV - 1)), pltpu.SemaphoreType.DMA((n, N_DEV - 1)),
                        pltpu.SemaphoreType.DMA((n,))],
    )(*arrs)


def _sum_slots(x, name):
    _, r, c = x.shape

    def body(x_ref, o_ref):
        acc = x_ref[0]
        for j in range(1, N_DEV):
            acc = acc + x_ref[j]
        o_ref[...] = acc

    return pl.pallas_call(body, name=name, out_shape=jax.ShapeDtypeStruct((r, c), F32))(x)


def _ada_fwd(c_all, w_ada):
    nb = c_all.shape[0]
    cols = w_ada.shape[2]

    def body(c_ref, w_ref, act_ref, mod_ref):
        cv = c_ref[...]
        act = cv * _sigmoid(cv)
        act_ref[...] = act
        for l in range(DEPTH):
            mod_ref[l] = jnp.dot(act, w_ref[l], precision=HI, preferred_element_type=F32)

    return pl.pallas_call(
        body, name="ada_fwd",
        out_shape=[jax.ShapeDtypeStruct((nb, D_MODEL), F32), jax.ShapeDtypeStruct((DEPTH, nb, cols), F32)],
        compiler_params=_cparams(),
    )(c_all, w_ada)


def _ada_bwd(c_act, dmod_all, dmod_mine):
    nb = c_act.shape[0]
    cols = dmod_mine.shape[2]

    def body(act_ref, dall_ref, dmine_ref, gw_ref, gb_ref):
        act = act_ref[...]
        for l in range(DEPTH):
            gw_ref[l] = lax.dot_general(act, dmine_ref[l], (((0,), (0,)), ((), ())),
                                        precision=HI, preferred_element_type=F32)
            gb_ref[l:l + 1, :] = jnp.sum(dall_ref[l], axis=0, keepdims=True)

    return pl.pallas_call(
        body, name="ada_bwd",
        out_shape=[jax.ShapeDtypeStruct((DEPTH, D_MODEL, cols), F32),
                   jax.ShapeDtypeStruct((DEPTH, 3 * D_MODEL), F32)],
        compiler_params=_cparams(),
    )(c_act, dmod_all, dmod_mine)


def _ln_in(x, ss, g, w, seq, tm=256):
    t = x.shape[0]
    tps = seq // tm

    def body(x_ref, ss_ref, g_ref, w_ref, h_ref, *outs):
        xv = x_ref[...]
        xn = xv * lax.rsqrt(jnp.mean(xv * xv, axis=-1, keepdims=True) + EPS)
        h = xn * g_ref[...] * ss_ref[0, 1:2, :] + ss_ref[0, 0:1, :]
        hb = h.astype(BF16)
        h_ref[...] = hb
        for o_ref, (_, off, wd, _) in zip(outs, Z_SEGS):
            o_ref[...] = jnp.dot(hb, w_ref[:, off:off + wd], preferred_element_type=F32).astype(o_ref.dtype)

    row = lambda wd: pl.BlockSpec((tm, wd), lambda i: (i, 0))
    return pl.pallas_call(
        body, name="ln_in", grid=(t // tm,),
        in_specs=[row(D_MODEL), pl.BlockSpec((1, 2, D_MODEL), lambda i: (i // tps, 0, 0)),
                  pl.BlockSpec((1, D_MODEL), lambda i: (0, 0)), pl.BlockSpec((D_MODEL, N_PAD), lambda i: (0, 0))],
        out_specs=[row(D_MODEL)] + [row(wd) for _, _, wd, _ in Z_SEGS],
        out_shape=[jax.ShapeDtypeStruct((t, D_MODEL), BF16)]
        + [jax.ShapeDtypeStruct((t, wd), dt) for _, _, wd, dt in Z_SEGS],
        compiler_params=_cparams(("arbitrary",)),
    )(x, ss, g, w)


def _ln_in_bwd(dz, w_t, x, ss, g, dxo, seq, tm=256):
    t = x.shape[0]
    tps = seq // tm
    nb = t // seq
    nz = len(Z_SEGS)

    def body(*refs):
        dz_refs = refs[:nz]
        wt_ref, x_ref, ss_ref, g_ref, dxo_ref, dx_ref, dss_ref, dg_ref = refs[nz:]
        i = pl.program_id(0)
        dh = None
        for r, (_, off, wd, _) in zip(dz_refs, Z_SEGS):
            part = jnp.dot(r[...].astype(BF16), wt_ref[off:off + wd, :], preferred_element_type=F32)
            dh = part if dh is None else dh + part
        xv = x_ref[...]
        rstd = lax.rsqrt(jnp.mean(xv * xv, axis=-1, keepdims=True) + EPS)
        xn = xv * rstd
        gv = g_ref[...]
        s1 = ss_ref[0, 1:2, :]
        dxg = dh * s1
        dxn = dxg * gv
        dx = rstd * (dxn - xn * jnp.mean(dxn * xn, axis=-1, keepdims=True))
        dx_ref[...] = dxo_ref[...] + dx
        dshift = jnp.sum(dh, axis=0, keepdims=True)
        dscale = jnp.sum(dh * (xn * gv), axis=0, keepdims=True)
        dgp = jnp.sum(dxg * xn, axis=0, keepdims=True)

        @pl.when(i % tps == 0)
        def _():
            dss_ref[0, 0:1, :] = dshift
            dss_ref[0, 1:2, :] = dscale

        @pl.when(i % tps != 0)
        def _():
            dss_ref[0, 0:1, :] += dshift
            dss_ref[0, 1:2, :] += dscale

        @pl.when(i == 0)
        def _():
            dg_ref[...] = dgp

        @pl.when(i != 0)
        def _():
            dg_ref[...] += dgp

    row = lambda wd: pl.BlockSpec((tm, wd), lambda i: (i, 0))
    return pl.pallas_call(
        body, name="ln_in_bwd", grid=(t // tm,),
        in_specs=[row(wd) for _, _, wd, _ in Z_SEGS]
        + [pl.BlockSpec((N_PAD, D_MODEL), lambda i: (0, 0)), row(D_MODEL),
           pl.BlockSpec((1, 2, D_MODEL), lambda i: (i // tps, 0, 0)),
           pl.BlockSpec((1, D_MODEL), lambda i: (0, 0)), row(D_MODEL)],
        out_specs=[row(D_MODEL), pl.BlockSpec((1, 2, D_MODEL), lambda i: (i // tps, 0, 0)),
                   pl.BlockSpec((1, D_MODEL), lambda i: (0, 0))],
        out_shape=[jax.ShapeDtypeStruct((t, D_MODEL), F32), jax.ShapeDtypeStruct((nb, 2, D_MODEL), F32),
                   jax.ShapeDtypeStruct((1, D_MODEL), F32)],
        compiler_params=_cparams(("arbitrary",)),
    )(*dz, w_t, x, ss, g, dxo)


def _matmul_tn(a, b, name, tm=512):
    t, k = a.shape
    n = b.shape[1]
    tm = min(tm, t)

    def body(a_ref, b_ref, o_ref):
        i = pl.program_id(0)
        part = lax.dot_general(a_ref[...].astype(BF16), b_ref[...].astype(BF16), (((0,), (0,)), ((), ())),
                               preferred_element_type=F32)

        @pl.when(i == 0)
        def _():
            o_ref[...] = part

        @pl.when(i != 0)
        def _():
            o_ref[...] += part

    return pl.pallas_call(
        body, name=name, grid=(t // tm,),
        in_specs=[pl.BlockSpec((tm, k), lambda i: (i, 0)), pl.BlockSpec((tm, n), lambda i: (i, 0))],
        out_specs=pl.BlockSpec((k, n), lambda i: (0, 0)),
        out_shape=jax.ShapeDtypeStruct((k, n), F32),
        compiler_params=_cparams(("arbitrary",)),
    )(a, b)


def _rope(blk, cos_t, sin_a, sin_b):
    return blk * cos_t + pltpu.roll(blk, 112, 1) * sin_a + pltpu.roll(blk, 16, 1) * sin_b


def _unrope(d, cos_t, sin_a, sin_b):
    return d * cos_t + pltpu.roll(d * sin_a, 16, 1) + pltpu.roll(d * sin_b, 112, 1)


def _mla_prep(cq, ckv, kpe, gq, gkv, wuq, wk, wv, cos_t, sin_a, sin_b, tm=256):
    t = cq.shape[0]
    qw = A_HEADS * HEAD_PAD

    def body(cq_ref, ckv_ref, kpe_ref, gq_ref, gkv_ref, wuq_ref, wk_ref, wv_ref, c_ref, sa_ref, sb_ref,
             q_ref, k_ref, v_ref, cqn_ref, ckvn_ref):
        ct, sa, sb = c_ref[...], sa_ref[...], sb_ref[...]
        a = cq_ref[...]
        cqn = (a * lax.rsqrt(jnp.mean(a * a, axis=-1, keepdims=True) + EPS) * gq_ref[...]).astype(BF16)
        cqn_ref[...] = cqn
        b = ckv_ref[...]
        ckvn = (b * lax.rsqrt(jnp.mean(b * b, axis=-1, keepdims=True) + EPS) * gkv_ref[...]).astype(BF16)
        ckvn_ref[...] = ckvn
        qlin = jnp.dot(cqn, wuq_ref[...], preferred_element_type=F32)
        klin = jnp.dot(ckvn, wk_ref[...], preferred_element_type=F32)
        v_ref[...] = jnp.dot(ckvn, wv_ref[...], preferred_element_type=F32).astype(BF16)
        kr = _rope(kpe_ref[...], ct, sa, sb)
        for h in range(A_HEADS):
            sl = slice(h * HEAD_PAD, (h + 1) * HEAD_PAD)
            q_ref[:, sl] = _rope(qlin[:, sl], ct, sa, sb).astype(BF16)
            k_ref[:, sl] = (klin[:, sl] + kr).astype(BF16)

    row = lambda wd: pl.BlockSpec((tm, wd), lambda i: (i, 0))
    full = lambda r, c: pl.BlockSpec((r, c), lambda i: (0, 0))
    return pl.pallas_call(
        body, name="mla_prep", grid=(t // tm,),
        in_specs=[row(A_Q_RANK), row(A_KV_RANK), row(128), full(1, A_Q_RANK), full(1, A_KV_RANK),
                  full(A_Q_RANK, qw), full(A_KV_RANK, qw), full(A_KV_RANK, GW), row(128), row(128), row(128)],
        out_specs=[row(qw), row(qw), row(GW), row(A_Q_RANK), row(A_KV_RANK)],
        out_shape=[jax.ShapeDtypeStruct((t, qw), BF16), jax.ShapeDtypeStruct((t, qw), BF16),
                   jax.ShapeDtypeStruct((t, GW), BF16), jax.ShapeDtypeStruct((t, A_Q_RANK), BF16),
                   jax.ShapeDtypeStruct((t, A_KV_RANK), BF16)],
        compiler_params=_cparams(("arbitrary",)),
    )(cq, ckv, kpe, gq, gkv, wuq, wk, wv, cos_t, sin_a, sin_b)


def _mla_prep_bwd(dq, dk, dv, cq, ckv, gq, gkv, wuq_t, wk_t, wv_t, cos_t, sin_a, sin_b, tm=256):
    t = cq.shape[0]
    qw = A_HEADS * HEAD_PAD

    def body(dq_ref, dk_ref, dv_ref, cq_ref, ckv_ref, gq_ref, gkv_ref, wuqt_ref, wkt_ref, wvt_ref,
             c_ref, sa_ref, sb_ref, dcq_ref, dckv_ref, dkpe_ref, dql_ref, dkl_ref, dgq_ref, dgkv_ref):
        i = pl.program_id(0)
        ct, sa, sb = c_ref[...], sa_ref[...], sb_ref[...]
        lane = lax.broadcasted_iota(jnp.int32, (1, HEAD_PAD), 1)
        nope = lane < A_NOPE
        rope = (lane >= A_NOPE) & (lane < A_NOPE + A_ROPE)
        dksum = None
        for h in range(A_HEADS):
            sl = slice(h * HEAD_PAD, (h + 1) * HEAD_PAD)
            dql_ref[:, sl] = _unrope(dq_ref[:, sl], ct, sa, sb).astype(BF16)
            dkh = dk_ref[:, sl]
            dkl_ref[:, sl] = jnp.where(nope, dkh, 0.0).astype(BF16)
            dksum = dkh if dksum is None else dksum + dkh
        dkpe_ref[...] = jnp.where(rope, _unrope(jnp.where(rope, dksum, 0.0), ct, sa, sb), 0.0)
        dcqn = jnp.dot(dql_ref[...], wuqt_ref[...], preferred_element_type=F32)
        dckvn = (jnp.dot(dkl_ref[...], wkt_ref[...], preferred_element_type=F32)
                 + jnp.dot(dv_ref[...].astype(BF16), wvt_ref[...], preferred_element_type=F32))

        def norm_bwd(xv, gv, dy):
            rstd = lax.rsqrt(jnp.mean(xv * xv, axis=-1, keepdims=True) + EPS)
            xn = xv * rstd
            dxn = dy * gv
            dx = rstd * (dxn - xn * jnp.mean(dxn * xn, axis=-1, keepdims=True))
            return dx, jnp.sum(dy * xn, axis=0, keepdims=True)

        dcq, dgq = norm_bwd(cq_ref[...], gq_ref[...], dcqn)
        dckv, dgkv = norm_bwd(ckv_ref[...], gkv_ref[...], dckvn)
        dcq_ref[...] = dcq
        dckv_ref[...] = dckv

        @pl.when(i == 0)
        def _():
            dgq_ref[...] = dgq
            dgkv_ref[...] = dgkv

        @pl.when(i != 0)
        def _():
            dgq_ref[...] += dgq
            dgkv_ref[...] += dgkv

    row = lambda wd: pl.BlockSpec((tm, wd), lambda i: (i, 0))
    full = lambda r, c: pl.BlockSpec((r, c), lambda i: (0, 0))
    return pl.pallas_call(
        body, name="mla_prep_bwd", grid=(t // tm,),
        in_specs=[row(qw), row(qw), row(GW), row(A_Q_RANK), row(A_KV_RANK), full(1, A_Q_RANK), full(1, A_KV_RANK),
                  full(qw, A_Q_RANK), full(qw, A_KV_RANK), full(GW, A_KV_RANK), row(128), row(128), row(128)],
        out_specs=[row(A_Q_RANK), row(A_KV_RANK), row(128), row(qw), row(qw), full(1, A_Q_RANK), full(1, A_KV_RANK)],
        out_shape=[jax.ShapeDtypeStruct((t, A_Q_RANK), F32), jax.ShapeDtypeStruct((t, A_KV_RANK), F32),
                   jax.ShapeDtypeStruct((t, 128), F32), jax.ShapeDtypeStruct((t, qw), BF16),
                   jax.ShapeDtypeStruct((t, qw), BF16), jax.ShapeDtypeStruct((1, A_Q_RANK), F32),
                   jax.ShapeDtypeStruct((1, A_KV_RANK), F32)],
        compiler_params=_cparams(("arbitrary",)),
    )(dq, dk, dv, cq, ckv, gq, gkv, wuq_t, wk_t, wv_t, cos_t, sin_a, sin_b)


def _nt(a, b):
    return lax.dot_general(a, b, (((1,), (1,)), ((), ())), preferred_element_type=F32)


def _tn(a, b):
    return lax.dot_general(a, b, (((0,), (0,)), ((), ())), preferred_element_type=F32)


def _causal_mask(kind, q0, k0, tq, tk):
    qpos = q0 + lax.broadcasted_iota(jnp.int32, (tq, tk), 0)
    kpos = k0 + lax.broadcasted_iota(jnp.int32, (tq, tk), 1)
    if kind == "mla":
        return lax.shift_right_logical(kpos, 6) <= lax.shift_right_logical(qpos, 6)
    return kpos <= qpos


def _attn_fwd(kind, q, k, v, f, seq, scale, tq=512, tk=512):
    t = v.shape[0]
    nb = t // seq
    nq = seq // tq
    hw = 256 if kind == "mla" else 128
    use_f = f is not None
    tq, tk = min(tq, seq), min(tk, seq)
    nq = seq // tq
    assert tk % tq == 0

    def body(*refs):
        if use_f:
            q_ref, k_ref, v_ref, f_ref, o_ref, st_ref = refs
        else:
            q_ref, k_ref, v_ref, o_ref, st_ref = refs
        qi = pl.program_id(2)
        q0 = qi * tq
        lane = lax.broadcasted_iota(jnp.int32, (1, 128), 1)
        half = lane >= 64
        qall = q_ref[...]
        if kind == "mla":
            qhs = [qall[:, 0:128], qall[:, 128:256]]
        else:
            qhs = [jnp.where(half, jnp.zeros_like(qall), qall), jnp.where(half, qall, jnp.zeros_like(qall))]
        nfull = q0 // tk
        kd = pl.multiple_of(nfull * tk, tk)
        diag = _causal_mask(kind, q0 - kd, 0, tq, tk)

        def block(j, k0, state, masked):
            m, l, acc = state
            kh = k_ref[pl.ds(k0, tk), j * 128:(j + 1) * 128] if kind == "mla" else k_ref[pl.ds(k0, tk), :]
            s = _nt(qhs[j], kh) * scale
            if use_f:
                s = s - f_ref[0, 0, j:j + 1, pl.ds(k0, tk)]
            if masked:
                s = jnp.where(diag, s, NEG)
            mn = jnp.maximum(m, jnp.max(s, axis=-1, keepdims=True))
            alpha = jnp.exp(m - mn)
            p = jnp.exp(s - mn)
            l = alpha * l + jnp.sum(p, axis=-1, keepdims=True)
            acc = alpha * acc + jnp.dot(p.astype(BF16), v_ref[pl.ds(k0, tk), :], preferred_element_type=F32)
            return mn, l, acc

        def kstep(kb, carry):
            k0 = pl.multiple_of(kb * tk, tk)
            return block(0, k0, carry[:3], False) + block(1, k0, carry[3:], False)

        init = (jnp.full((tq, 1), NEG, F32), jnp.zeros((tq, 1), F32), jnp.zeros((tq, 128), F32)) * 2
        carry = lax.fori_loop(0, nfull, kstep, init)
        m0, l0, a0 = block(0, kd, carry[:3], True)
        m1, l1, a1 = block(1, kd, carry[3:], True)
        o_ref[...] = jnp.where(half, a1 / l1, a0 / l0)
        st_ref[...] = jnp.where(lane == 0, m0 + jnp.log(l0), jnp.where(lane == 1, m1 + jnp.log(l1), 0.0))

    in_specs = [pl.BlockSpec((tq, hw), lambda b, p, i: (b * nq + i, p)),
                pl.BlockSpec((seq, hw), lambda b, p, i: (b, p)),
                pl.BlockSpec((seq, 128), lambda b, p, i: (b, p))]
    args = [q, k, v]
    if use_f:
        in_specs.append(pl.BlockSpec((1, 1, 8, seq), lambda b, p, i: (b, p, 0, 0)))
        args.append(f)
    oblk = pl.BlockSpec((tq, 128), lambda b, p, i: (b * nq + i, p))
    return pl.pallas_call(
        body, name="attn_fwd_" + kind, grid=(nb, 3, nq), in_specs=in_specs, out_specs=[oblk, oblk],
        out_shape=[jax.ShapeDtypeStruct((t, GW), F32), jax.ShapeDtypeStruct((t, GW), F32)],
        compiler_params=_cparams(("arbitrary", "arbitrary", "arbitrary")),
    )(*args)


def _attn_bwd(kind, q, k, v, f, o, st, do, seq, scale, tq=512, tk=512):
    t = v.shape[0]
    nb = t // seq
    tq, tk = min(tq, seq), min(tk, seq)
    nq = seq // tq
    nk = seq // tk
    hw = 256 if kind == "mla" else 128
    use_f = f is not None
    assert tq == tk

    def body(*refs):
        if use_f:
            q_ref, k_ref, v_ref, f_ref, o_ref, st_ref, do_ref, dq_ref, dk_ref, dv_ref, df_ref, dfq_ref = refs
        else:
            q_ref, k_ref, v_ref, o_ref, st_ref, do_ref, dq_ref, dk_ref, dv_ref = refs
        kj = pl.program_id(2)
        k0 = kj * tk
        lane = lax.broadcasted_iota(jnp.int32, (1, 128), 1)
        half = lane >= 64

        @pl.when(kj == 0)
        def _():
            dq_ref[...] = jnp.zeros_like(dq_ref)
            if use_f:
                dfq_ref[...] = jnp.zeros_like(dfq_ref)

        dk_ref[...] = jnp.zeros_like(dk_ref)
        dv_ref[...] = jnp.zeros_like(dv_ref)
        if use_f:
            df_ref[...] = jnp.zeros_like(df_ref)
        vv = v_ref[...]
        diag = _causal_mask(kind, 0, 0, tq, tk)

        def qstep(qi, masked):
            q0 = pl.multiple_of(qi * tq, tq)
            rows = pl.ds(q0, tq)
            dov = do_ref[rows, :]
            dd = dov * o_ref[rows, :]
            stv = st_ref[rows, :]
            for j in range(2):
                hm = half == bool(j)
                delta = jnp.sum(jnp.where(hm, dd, 0.0), axis=-1, keepdims=True)
                lse = stv[:, j:j + 1]
                if kind == "mla":
                    cols = slice(j * 128, (j + 1) * 128)
                    qh = q_ref[rows, cols]
                    kh = k_ref[:, cols]
                else:
                    cols = slice(0, 128)
                    qa = q_ref[rows, :]
                    qh = jnp.where(hm, qa, jnp.zeros_like(qa))
                    kh = k_ref[...]
                s = _nt(qh, kh) * scale
                if use_f:
                    s = s - f_ref[0, 0, j:j + 1, :]
                if masked:
                    s = jnp.where(diag, s, NEG)
                p = jnp.exp(s - lse)
                doh = jnp.where(hm, dov, 0.0).astype(BF16)
                ds = p * (_nt(doh, vv) - delta)
                dsb = (ds * scale).astype(BF16)
                dv_ref[...] += _tn(p.astype(BF16), doh)
                dk_ref[:, cols] += _tn(dsb, qh)
                dqc = jnp.dot(dsb, kh, preferred_element_type=F32)
                if kind != "mla":
                    dqc = jnp.where(hm, dqc, 0.0)
                dq_ref[rows, cols] += dqc
                if use_f:
                    df_ref[0, 0, j:j + 1, :] += -jnp.sum(ds, axis=0, keepdims=True)
                    dfq_ref[rows, :] += jnp.where(lane == j, jnp.sum(ds, axis=-1, keepdims=True), 0.0)

        qstep(kj, True)

        def rest(qi, carry):
            qstep(qi, False)
            return carry

        lax.fori_loop(kj + 1, nq, rest, 0)

    full_q = lambda wd: pl.BlockSpec((seq, wd), lambda b, p, i: (b, p))
    kblk = lambda wd: pl.BlockSpec((tk, wd), lambda b, p, i: (b * nk + i, p))
    in_specs = [full_q(hw), kblk(hw), kblk(128)]
    args = [q, k, v]
    if use_f:
        in_specs.append(pl.BlockSpec((1, 1, 8, tk), lambda b, p, i: (b, p, 0, i)))
        args.append(f)
    in_specs += [full_q(128), full_q(128), full_q(128)]
    args += [o, st, do]
    out_specs = [full_q(hw), kblk(hw), kblk(128)]
    out_shape = [jax.ShapeDtypeStruct((t, 3 * hw), F32), jax.ShapeDtypeStruct((t, 3 * hw), F32),
                 jax.ShapeDtypeStruct((t, GW), F32)]
    if use_f:
        out_specs += [pl.BlockSpec((1, 1, 8, tk), lambda b, p, i: (b, p, 0, i)), full_q(128)]
        out_shape += [jax.ShapeDtypeStruct((nb, 3, 8, seq), F32), jax.ShapeDtypeStruct((t, GW), F32)]
    return pl.pallas_call(
        body, name="attn_bwd_" + kind, grid=(nb, 3, nk), in_specs=in_specs, out_specs=out_specs,
        out_shape=out_shape, compiler_params=_cparams(("arbitrary", "arbitrary", "arbitrary")),
    )(*args)


BQ = 256
BWIN = BQ + B_LEFT


def _band_geometry():
    r = lax.broadcasted_iota(jnp.int32, (BQ, BWIN), 0)
    j = lax.broadcasted_iota(jnp.int32, (BQ, BWIN), 1)
    rc = lax.shift_right_logical(r, 6)
    jc = lax.shift_right_logical(j, 6)
    allowed = (jc - 8 <= rc) & (rc <= jc)
    return (r + B_LEFT - j) >= REL_CLIP, allowed, j < r


def _band_onehot(transposed, offset=0):
    shape = (BWIN, GW) if transposed else (GW, BWIN)
    kk = lax.broadcasted_iota(jnp.int32, shape, 1 if transposed else 0)
    x = lax.broadcasted_iota(jnp.int32, shape, 0 if transposed else 1) - offset
    x = jnp.where(x < 0, x + BWIN, x)
    return (kk == jnp.clip(B_LEFT - x, -REL_CLIP, REL_CLIP) + REL_CLIP).astype(F32)


def _band_table(rel_bias8):
    def body(b_ref, o_ref):
        hh = pl.program_id(0)
        u8 = jnp.dot(b_ref[...], _band_onehot(False), precision=HI, preferred_element_type=F32)
        rid = lax.broadcasted_iota(jnp.int32, (8, BWIN), 0)
        row = jnp.sum(jnp.where(rid == hh, u8, 0.0), axis=0, keepdims=True)
        far, allowed, _ = _band_geometry()
        tbl = pltpu.roll(jnp.broadcast_to(row, (BQ, BWIN)), 0, 1, stride=1, stride_axis=0)
        tbl = jnp.where(far, row[:, 0:1], tbl)
        o_ref[0] = jnp.where(allowed, tbl, NEG)

    return pl.pallas_call(
        body, name="band_table", grid=(6,),
        in_specs=[pl.BlockSpec((8, GW), lambda h: (0, 0))],
        out_specs=pl.BlockSpec((1, BQ, BWIN), lambda h: (h, 0, 0)),
        out_shape=jax.ShapeDtypeStruct((6, BQ, BWIN), F32),
        compiler_params=_cparams(("arbitrary",)),
    )(rel_bias8)


def _band_table_bwd(gtab):
    def body(g_ref, o_ref):
        gv = g_ref[0]
        _, _, wrapped = _band_geometry()
        gfar = jnp.sum(jnp.sum(jnp.where(wrapped, gv, 0.0), axis=-1, keepdims=True), axis=0, keepdims=True)
        anti = (lax.broadcasted_iota(jnp.int32, (BQ, BQ), 0) + lax.broadcasted_iota(jnp.int32, (BQ, BQ), 1)
                == BQ - 1).astype(F32)
        grev = jnp.dot(anti, jnp.where(wrapped, 0.0, gv), precision=HI, preferred_element_type=F32)
        near = pltpu.roll(grev, 0, 1, stride=1, stride_axis=0)
        y = jnp.broadcast_to(jnp.sum(near, axis=0, keepdims=True), (8, BWIN))
        gb = jnp.dot(y, _band_onehot(True, BQ - 1), precision=HI, preferred_element_type=F32)
        lane = lax.broadcasted_iota(jnp.int32, (8, GW), 1)
        o_ref[0] = gb + jnp.where(lane == 2 * REL_CLIP, gfar, 0.0)

    return pl.pallas_call(
        body, name="band_table_bwd", grid=(B_HEADS,),
        in_specs=[pl.BlockSpec((1, BQ, BWIN), lambda h: (h, 0, 0))],
        out_specs=pl.BlockSpec((1, 8, GW), lambda h: (h, 0, 0)),
        out_shape=jax.ShapeDtypeStruct((B_HEADS, 8, GW), F32),
        compiler_params=_cparams(("arbitrary",)),
    )(gtab)


def _band_fwd(q, k, v, table, seq, scale):
    t = q.shape[0]
    nb = t // seq
    nq = seq // BQ

    def body(q_ref, k_ref, v_ref, tb_ref, o_ref, st_ref, kpad, vpad):
        qi = pl.program_id(2)
        q0 = pl.multiple_of(qi * BQ, BQ)
        lane = lax.broadcasted_iota(jnp.int32, (1, 128), 1)
        half = lane >= 64

        @pl.when(qi == 0)
        def _():
            kpad[0:B_LEFT, :] = jnp.zeros((B_LEFT, 128), BF16)
            vpad[0:B_LEFT, :] = jnp.zeros((B_LEFT, 128), BF16)
            kpad[B_LEFT:, :] = k_ref[...]
            vpad[B_LEFT:, :] = v_ref[...]

        kw = kpad[pl.ds(q0, BWIN), :]
        vw = vpad[pl.ds(q0, BWIN), :]
        inside = lax.broadcasted_iota(jnp.int32, (BQ, BWIN), 1) >= B_LEFT - q0
        qall = q_ref[...]
        outs, lses = [], []
        for j in range(2):
            qh = jnp.where(half == bool(j), qall, jnp.zeros_like(qall))
            s = jnp.where(inside, _nt(qh, kw) * scale + tb_ref[j], NEG)
            m = jnp.max(s, axis=-1, keepdims=True)
            p = jnp.exp(s - m)
            l = jnp.sum(p, axis=-1, keepdims=True)
            outs.append(jnp.dot(p.astype(BF16), vw, preferred_element_type=F32) / l)
            lses.append(m + jnp.log(l))
        o_ref[...] = jnp.where(half, outs[1], outs[0])
        st_ref[...] = jnp.where(lane == 0, lses[0], jnp.where(lane == 1, lses[1], 0.0))

    qblk = pl.BlockSpec((BQ, 128), lambda b, p, i: (b * nq + i, p))
    full = pl.BlockSpec((seq, 128), lambda b, p, i: (b, p))
    return pl.pallas_call(
        body, name="band_fwd", grid=(nb, 3, nq),
        in_specs=[qblk, full, full, pl.BlockSpec((2, BQ, BWIN), lambda b, p, i: (p, 0, 0))],
        out_specs=[qblk, qblk],
        out_shape=[jax.ShapeDtypeStruct((t, GW), F32), jax.ShapeDtypeStruct((t, GW), F32)],
        scratch_shapes=[pltpu.VMEM((seq + B_LEFT, 128), BF16), pltpu.VMEM((seq + B_LEFT, 128), BF16)],
        compiler_params=_cparams(("arbitrary", "arbitrary", "arbitrary")),
    )(q, k, v, table)


def _band_bwd(q, k, v, table, o, st, do, seq, scale):
    t = q.shape[0]
    nb = t // seq
    nq = seq // BQ

    def body(q_ref, k_ref, v_ref, tb_ref, o_ref, st_ref, do_ref, dq_ref, dk_ref, dv_ref, g_ref,
             kpad, vpad, dkpad, dvpad):
        b = pl.program_id(1)
        qi = pl.program_id(2)
        q0 = pl.multiple_of(qi * BQ, BQ)
        lane = lax.broadcasted_iota(jnp.int32, (1, 128), 1)
        half = lane >= 64

        @pl.when(qi == 0)
        def _():
            kpad[0:B_LEFT, :] = jnp.zeros((B_LEFT, 128), BF16)
            vpad[0:B_LEFT, :] = jnp.zeros((B_LEFT, 128), BF16)
            kpad[B_LEFT:, :] = k_ref[...]
            vpad[B_LEFT:, :] = v_ref[...]
            dkpad[...] = jnp.zeros_like(dkpad)
            dvpad[...] = jnp.zeros_like(dvpad)

        @pl.when((qi == 0) & (b == 0))
        def _():
            g_ref[...] = jnp.zeros_like(g_ref)

        win = pl.ds(q0, BWIN)
        kw = kpad[win, :]
        vw = vpad[win, :]
        inside = lax.broadcasted_iota(jnp.int32, (BQ, BWIN), 1) >= B_LEFT - q0
        qall = q_ref[...]
        dov = do_ref[...]
        dd = dov * o_ref[...]
        stv = st_ref[...]
        dq = jnp.zeros((BQ, 128), F32)
        for j in range(2):
            hm = half == bool(j)
            qh = jnp.where(hm, qall, jnp.zeros_like(qall))
            delta = jnp.sum(jnp.where(hm, dd, 0.0), axis=-1, keepdims=True)
            s = jnp.where(inside, _nt(qh, kw) * scale + tb_ref[j], NEG)
            p = jnp.exp(s - stv[:, j:j + 1])
            doh = jnp.where(hm, dov, 0.0).astype(BF16)
            ds = p * (_nt(doh, vw) - delta)
            g_ref[j] += ds
            dsb = (ds * scale).astype(BF16)
            dvpad[win, :] += _tn(p.astype(BF16), doh)
            dkpad[win, :] += _tn(dsb, qh)
            dq = dq + jnp.where(hm, jnp.dot(dsb, kw, preferred_element_type=F32), 0.0)
        dq_ref[...] = dq

        @pl.when(qi == nq - 1)
        def _():
            dk_ref[...] = dkpad[B_LEFT:, :]
            dv_ref[...] = dvpad[B_LEFT:, :]

    qblk = pl.BlockSpec((BQ, 128), lambda p, b, i: (b * nq + i, p))
    full = pl.BlockSpec((seq, 128), lambda p, b, i: (b, p))
    tblk = pl.BlockSpec((2, BQ, BWIN), lambda p, b, i: (p, 0, 0))
    return pl.pallas_call(
        body, name="band_bwd", grid=(3, nb, nq),
        in_specs=[qblk, full, full, tblk, qblk, qblk, qblk],
        out_specs=[qblk, full, full, tblk],
        out_shape=[jax.ShapeDtypeStruct((t, GW), F32), jax.ShapeDtypeStruct((t, GW), F32),
                   jax.ShapeDtypeStruct((t, GW), F32), jax.ShapeDtypeStruct((6, BQ, BWIN), F32)],
        scratch_shapes=[pltpu.VMEM((seq + B_LEFT, 128), BF16), pltpu.VMEM((seq + B_LEFT, 128), BF16),
                        pltpu.VMEM((seq + B_LEFT, 128), F32), pltpu.VMEM((seq + B_LEFT, 128), F32)],
        compiler_params=_cparams(("arbitrary", "arbitrary", "arbitrary")),
    )(q, k, v, table, o, st, do)


def _fox_prep(cf, fb, seq):
    nb = cf.shape[0] // seq
    nblk = seq // 128

    def body(cf_ref, fb_ref, f_ref):
        x = cf_ref[...] + fb_ref[...]
        lf = jnp.minimum(x, 0.0) - jnp.log1p(jnp.exp(-jnp.abs(x)))
        rows = lf.T[0:8, :]
        upper = (lax.broadcasted_iota(jnp.int32, (128, 128), 0)
                 <= lax.broadcasted_iota(jnp.int32, (128, 128), 1)).astype(F32)
        carry = jnp.zeros((8, 1), F32)
        for blk in range(nblk):
            sl = slice(blk * 128, (blk + 1) * 128)
            cs = jnp.dot(rows[:, sl], upper, precision=HI, preferred_element_type=F32) + carry
            carry = cs[:, 127:128]
            f_ref[0, 0, :, sl] = cs
            f_ref[0, 1, :, sl] = pltpu.roll(cs, 6, 0)
            f_ref[0, 2, :, sl] = pltpu.roll(cs, 4, 0)

    return pl.pallas_call(
        body, name="fox_prep", grid=(nb,),
        in_specs=[pl.BlockSpec((seq, 128), lambda b: (b, 0)), pl.BlockSpec((1, 128), lambda b: (0, 0))],
        out_specs=pl.BlockSpec((1, 3, 8, seq), lambda b: (b, 0, 0, 0)),
        out_shape=jax.ShapeDtypeStruct((nb, 3, 8, seq), F32),
        compiler_params=_cparams(("arbitrary",)),
    )(cf, fb)


def _fox_prep_bwd(df, dfq, cf, fb, seq):
    nb = cf.shape[0] // seq
    nblk = seq // 128

    def body(df_ref, dfq_ref, cf_ref, fb_ref, dcf_ref, dfb_ref, wide):
        b = pl.program_id(0)
        row = lax.broadcasted_iota(jnp.int32, (8, seq), 0)
        dfh = None
        for p in range(3):
            both = df_ref[0, p] + dfq_ref[:, p * 128:(p + 1) * 128].T[0:8, :]
            both = jnp.where(row < 2, both, 0.0)
            if p:
                both = pltpu.roll(both, 2 * p, 0)
            dfh = both if dfh is None else dfh + both
        lower = (lax.broadcasted_iota(jnp.int32, (128, 128), 0)
                 >= lax.broadcasted_iota(jnp.int32, (128, 128), 1)).astype(F32)
        wide[...] = jnp.zeros_like(wide)
        carry = jnp.zeros((8, 1), F32)
        for blk in reversed(range(nblk)):
            sl = slice(blk * 128, (blk + 1) * 128)
            rc = jnp.dot(dfh[:, sl], lower, precision=HI, preferred_element_type=F32) + carry
            carry = rc[:, 0:1]
            wide[0:8, sl] = rc
        dl = wide[...].T
        x = cf_ref[...] + fb_ref[...]
        dcf = dl * (1.0 / (1.0 + jnp.exp(x)))
        dcf_ref[...] = dcf
        part = jnp.sum(dcf, axis=0, keepdims=True)

        @pl.when(b == 0)
        def _():
            dfb_ref[...] = part

        @pl.when(b != 0)
        def _():
            dfb_ref[...] += part

    return pl.pallas_call(
        body, name="fox_prep_bwd", grid=(nb,),
        in_specs=[pl.BlockSpec((1, 3, 8, seq), lambda b: (b, 0, 0, 0)), pl.BlockSpec((seq, GW), lambda b: (b, 0)),
                  pl.BlockSpec((seq, 128), lambda b: (b, 0)), pl.BlockSpec((1, 128), lambda b: (0, 0))],
        out_specs=[pl.BlockSpec((seq, 128), lambda b: (b, 0)), pl.BlockSpec((1, 128), lambda b: (0, 0))],
        out_shape=[jax.ShapeDtypeStruct(cf.shape, F32), jax.ShapeDtypeStruct((1, 128), F32)],
        scratch_shapes=[pltpu.VMEM((128, seq), F32)],
        compiler_params=_cparams(("arbitrary",)),
    )(df, dfq, cf, fb)


def _gate_out(oa, ob, oc, gates, w, x, gate, seq, tm=256):
    t = x.shape[0]
    tps = seq // tm

    def body(oa_ref, ob_ref, oc_ref, g_ref, w_ref, x_ref, gt_ref, xo_ref, y_ref, u_ref):
        for n, o_ref in enumerate((oa_ref, ob_ref, oc_ref)):
            sl = slice(n * GW, (n + 1) * GW)
            gv = g_ref[:, sl]
            u_ref[:, sl] = (o_ref[...] * (gv * _sigmoid(gv))).astype(BF16)
        y = jnp.dot(u_ref[...], w_ref[...], preferred_element_type=F32)
        y_ref[...] = y
        xo_ref[...] = x_ref[...] + gt_ref[0] * y

    row = lambda wd: pl.BlockSpec((tm, wd), lambda i: (i, 0))
    return pl.pallas_call(
        body, name="gate_out", grid=(t // tm,),
        in_specs=[row(GW), row(GW), row(GW), row(U_PAD), pl.BlockSpec((U_PAD, D_MODEL), lambda i: (0, 0)),
                  row(D_MODEL), pl.BlockSpec((1, 1, D_MODEL), lambda i: (i // tps, 0, 0))],
        out_specs=[row(D_MODEL), row(D_MODEL), row(U_PAD)],
        out_shape=[jax.ShapeDtypeStruct((t, D_MODEL), F32), jax.ShapeDtypeStruct((t, D_MODEL), F32),
                   jax.ShapeDtypeStruct((t, U_PAD), BF16)],
        compiler_params=_cparams(("arbitrary",)),
    )(oa, ob, oc, gates, w, x, gate)


def _gate_out_bwd(dxo, y, gate, oa, ob, oc, gates, w_t, seq, tm=256):
    t = dxo.shape[0]
    tps = seq // tm
    nb = t // seq

    def body(dxo_ref, y_ref, gt_ref, oa_ref, ob_ref, oc_ref, g_ref, wt_ref,
             dy_ref, doa_ref, dob_ref, doc_ref, dg_ref, dgt_ref):
        i = pl.program_id(0)
        dxo_v = dxo_ref[...]
        dgt = jnp.sum(dxo_v * y_ref[...], axis=0, keepdims=True)
        dyb = (dxo_v * gt_ref[0]).astype(BF16)
        dy_ref[...] = dyb
        du = jnp.dot(dyb, wt_ref[...], preferred_element_type=F32)
        for n, (o_ref, do_ref) in enumerate(((oa_ref, doa_ref), (ob_ref, dob_ref), (oc_ref, doc_ref))):
            sl = slice(n * GW, (n + 1) * GW)
            gv = g_ref[:, sl]
            sg = _sigmoid(gv)
            dun = du[:, sl]
            do_ref[...] = dun * (gv * sg)
            dg_ref[:, sl] = dun * o_ref[...] * (sg * (1.0 + gv * (1.0 - sg)))

        @pl.when(i % tps == 0)
        def _():
            dgt_ref[0] = dgt

        @pl.when(i % tps != 0)
        def _():
            dgt_ref[0] += dgt

    row = lambda wd: pl.BlockSpec((tm, wd), lambda i: (i, 0))
    per_b = pl.BlockSpec((1, 1, D_MODEL), lambda i: (i // tps, 0, 0))
    return pl.pallas_call(
        body, name="gate_out_bwd", grid=(t // tm,),
        in_specs=[row(D_MODEL), row(D_MODEL), per_b, row(GW), row(GW), row(GW), row(U_PAD),
                  pl.BlockSpec((D_MODEL, U_PAD), lambda i: (0, 0))],
        out_specs=[row(D_MODEL), row(GW), row(GW), row(GW), row(U_PAD), per_b],
        out_shape=[jax.ShapeDtypeStruct((t, D_MODEL), BF16), jax.ShapeDtypeStruct((t, GW), F32),
                   jax.ShapeDtypeStruct((t, GW), F32), jax.ShapeDtypeStruct((t, GW), F32),
                   jax.ShapeDtypeStruct((t, U_PAD), F32), jax.ShapeDtypeStruct((nb, 1, D_MODEL), F32)],
        compiler_params=_cparams(("arbitrary",)),
    )(dxo, y, gate, oa, ob, oc, gates, w_t)


def _final_loss(x, target, g, tm=256):
    t = x.shape[0]

    def body(x_ref, t_ref, g_ref, dx_ref, loss_ref, dg_ref):
        i = pl.program_id(0)
        xv = x_ref[...]
        rstd = lax.rsqrt(jnp.mean(xv * xv, axis=-1, keepdims=True) + EPS)
        xn = xv * rstd
        gv = g_ref[...]
        err = xn * gv - t_ref[...]
        dy = err * (1.0 / D_MODEL)
        dxn = dy * gv
        dx_ref[...] = rstd * (dxn - xn * jnp.mean(dxn * xn, axis=-1, keepdims=True))
        lp = jnp.sum(err * err, axis=0, keepdims=True) * (0.5 / D_MODEL)
        dgp = jnp.sum(dy * xn, axis=0, keepdims=True)

        @pl.when(i == 0)
        def _():
            loss_ref[...] = lp
            dg_ref[...] = dgp

        @pl.when(i != 0)
        def _():
            loss_ref[...] += lp
            dg_ref[...] += dgp

    row = pl.BlockSpec((tm, D_MODEL), lambda i: (i, 0))
    vec = pl.BlockSpec((1, D_MODEL), lambda i: (0, 0))
    return pl.pallas_call(
        body, name="final_loss", grid=(t // tm,),
        in_specs=[row, row, vec], out_specs=[row, vec, vec],
        out_shape=[jax.ShapeDtypeStruct((t, D_MODEL), F32), jax.ShapeDtypeStruct((1, D_MODEL), F32),
                   jax.ShapeDtypeStruct((1, D_MODEL), F32)],
        compiler_params=_cparams(("arbitrary",)),
    )(x, target, g)


def _adamw(w, gslots, m, v, name, tr=None):
    r, c = w.shape
    ns = gslots.shape[0]
    tr = r if tr is None else tr

    def body(w_ref, g_ref, m_ref, v_ref, go_ref, d_ref, mo_ref, vo_ref):
        g = g_ref[0]
        for j in range(1, ns):
            g = g + g_ref[j]
        mn = ADAM_B1 * m_ref[...] + (1.0 - ADAM_B1) * g
        vn = ADAM_B2 * v_ref[...] + (1.0 - ADAM_B2) * jnp.square(g)
        m_hat = mn / (1.0 - ADAM_B1 ** ADAM_STEP)
        v_hat = vn / (1.0 - ADAM_B2 ** ADAM_STEP)
        go_ref[...] = g
        d_ref[...] = -ADAM_LR * (m_hat / (jnp.sqrt(v_hat) + ADAM_EPS) + ADAM_WD * w_ref[...])
        mo_ref[...] = mn
        vo_ref[...] = vn

    blk = pl.BlockSpec((tr, c), lambda i: (i, 0))
    return pl.pallas_call(
        body, name=name, grid=(r // tr,),
        in_specs=[blk, pl.BlockSpec((ns, tr, c), lambda i: (0, i, 0)), blk, blk],
        out_specs=[blk] * 4, out_shape=[jax.ShapeDtypeStruct((r, c), F32)] * 4,
        compiler_params=_cparams(("arbitrary",)),
    )(w, gslots, m, v)


def _rope_tables(positions):
    inv = ROPE_THETA ** (-jnp.arange(0, A_ROPE, 2, dtype=F32) / A_ROPE)
    ang = positions.astype(F32)[:, None] * inv
    cos, sin = jnp.cos(ang), jnp.sin(ang)
    t = positions.shape[0]
    one = jnp.ones((t, 64), F32)
    zero16 = jnp.zeros((t, 16), F32)
    cos_t = jnp.concatenate([one, cos, cos, jnp.ones((t, 32), F32)], axis=1)
    sin_a = jnp.concatenate([jnp.zeros((t, 64), F32), -sin, zero16, jnp.zeros((t, 32), F32)], axis=1)
    sin_b = jnp.concatenate([jnp.zeros((t, 64), F32), zero16, sin, jnp.zeros((t, 32), F32)], axis=1)
    return cos_t, sin_a, sin_b


def _pad_heads(w, real, padded, nheads, axis):
    shp = w.shape[:axis] + (nheads, real) + w.shape[axis + 1:]
    w = w.reshape(shp)
    pad = [(0, 0)] * w.ndim
    pad[axis + 1] = (0, padded - real)
    w = jnp.pad(w, pad)
    return w.reshape(w.shape[:axis] + (nheads * padded,) + w.shape[axis + 2:])


def kernel(x, c, positions, w_ada, b_ada, norm_g, w_in, a_q_norm_g, a_w_uq, a_kv_norm_g, a_w_ukv, b_rel_bias, c_forget_b, w_out, final_g, loss_target, m_w_ada, m_b_ada, m_norm_g, m_w_in, m_a_q_norm_g, m_a_w_uq, m_a_kv_norm_g, m_a_w_ukv, m_b_rel_bias, m_c_forget_b, m_w_out, m_final_g, v_w_ada, v_b_ada, v_norm_g, v_w_in, v_a_q_norm_g, v_a_w_uq, v_a_kv_norm_g, v_a_w_ukv, v_b_rel_bias, v_c_forget_b, v_w_out, v_final_g):
    nb, seq, _ = x.shape
    t = nb * seq
    me = 4 * lax.axis_index("x") + 2 * lax.axis_index("y") + lax.axis_index("c")
    x2 = x.reshape(t, D_MODEL)
    tgt = loss_target.reshape(t, D_MODEL)
    cos_t, sin_a, sin_b = _rope_tables(positions.reshape(t))

    shards = []
    for l in range(DEPTH):
        shards += [w_in[l].astype(BF16), w_out[l].astype(BF16), a_w_uq[l].astype(BF16), a_w_ukv[l].astype(BF16)]
    gathered = _exchange(shards + [c], "gather", "gather_weights")
    c_all = gathered[-1].reshape(N_DEV * nb, D_MODEL)
    w_in_p, w_in_t, w_out_p, w_out_t, wuq_p, wuq_t, wk_p, wk_t, wv_p, wv_t = ([] for _ in range(10))
    for l in range(DEPTH):
        gi, go, gq, gkv = gathered[4 * l:4 * l + 4]
        wi = _pad_runs(gi.reshape(D_MODEL, N_IN), IN_RUNS, N_PAD, 1)
        wo = _pad_runs(go.reshape(D_MODEL, D_MODEL), OUT_RUNS, U_PAD, 0)
        wq = jnp.transpose(gq, (1, 0, 2)).reshape(A_Q_RANK, A_HEADS * (A_NOPE + A_ROPE))
        wq = _pad_heads(wq, A_NOPE + A_ROPE, HEAD_PAD, A_HEADS, 1)
        wkv = jnp.transpose(gkv, (1, 0, 2)).reshape(A_KV_RANK, A_HEADS, 2 * A_NOPE)
        wk = jnp.pad(wkv[:, :, :A_NOPE], ((0, 0), (0, 0), (0, HEAD_PAD - A_NOPE))).reshape(A_KV_RANK, A_HEADS * HEAD_PAD)
        wv = wkv[:, :, A_NOPE:].reshape(A_KV_RANK, GW)
        w_in_p.append(wi); w_in_t.append(wi.T); w_out_p.append(wo); w_out_t.append(wo.T)
        wuq_p.append(wq); wuq_t.append(wq.T); wk_p.append(wk); wk_t.append(wk.T); wv_p.append(wv); wv_t.append(wv.T)

    c_act, mod_cols = _ada_fwd(c_all, w_ada)
    (mod_g,) = _exchange([mod_cols], "gather", "gather_mod")
    mod_all = jnp.transpose(mod_g, (1, 2, 0, 3)).reshape(DEPTH, N_DEV * nb, 3 * D_MODEL)
    mod = lax.dynamic_slice_in_dim(mod_all, me * nb, nb, axis=1) + b_ada[:, None, :]

    fb_pad = jnp.pad(c_forget_b, ((0, 0), (0, 128 - C_HEADS)))
    a_scale = (A_NOPE + A_ROPE) ** -0.5
    h_scale = CHUNK ** -0.5

    saved = []
    xl = x2
    for l in range(DEPTH):
        shift, scale, gate = mod[l, :, :D_MODEL], mod[l, :, D_MODEL:2 * D_MODEL], mod[l, :, 2 * D_MODEL:]
        ss = jnp.stack([shift, 1.0 + scale], axis=1)
        gate3 = gate[:, None, :]
        h, cq, ckv, kpe, gates, bq, bk, bv, cq2, ck, cv, cf = _ln_in(xl, ss, norm_g[l:l + 1], w_in_p[l], seq)
        q, k, v, cqn, ckvn = _mla_prep(cq, ckv, kpe, a_q_norm_g[l:l + 1], a_kv_norm_g[l:l + 1],
                                       wuq_p[l], wk_p[l], wv_p[l], cos_t, sin_a, sin_b)
        oa, sta = _attn_fwd("mla", q, k, v, None, seq, a_scale)
        table = _band_table(jnp.pad(b_rel_bias[l], ((0, 8 - B_HEADS), (0, GW - N_REL))))
        ob, stb = _band_fwd(bq, bk, bv, table, seq, h_scale)
        fcum = _fox_prep(cf, fb_pad[l:l + 1], seq)
        oc, stc = _attn_fwd("fox", cq2, ck, cv, fcum, seq, h_scale)
        xn, y, u = _gate_out(oa, ob, oc, gates, w_out_p[l], xl, gate3, seq)
        saved.append(dict(x=xl, ss=ss, gate3=gate3, h=h, cq=cq, ckv=ckv, gates=gates, bq=bq, bk=bk, bv=bv,
                          cq2=cq2, ck=ck, cv=cv, cf=cf, q=q, k=k, v=v, cqn=cqn, ckvn=ckvn, oa=oa, sta=sta,
                          table=table, ob=ob, stb=stb, fcum=fcum, oc=oc, stc=stc, y=y, u=u))
        xl = xn

    dx, loss_lanes, g_final = _final_loss(xl, tgt, final_g[None, :])
    loss = lax.psum(jnp.sum(loss_lanes), AXES)

    g_in, g_out, g_uq, g_ukv, dmods, smalls = [None] * DEPTH, [None] * DEPTH, [None] * DEPTH, [None] * DEPTH, [None] * DEPTH, [None] * DEPTH
    for l in reversed(range(DEPTH)):
        s = saved[l]
        dy, doa, dob, doc, dgates, dgate = _gate_out_bwd(dx, s["y"], s["gate3"], s["oa"], s["ob"], s["oc"],
                                                         s["gates"], w_out_t[l], seq)
        g_out[l] = _unpad_runs(_matmul_tn(s["u"], dy, "dw_out"), OUT_RUNS, 0)
        dq, dk, dv = _attn_bwd("mla", s["q"], s["k"], s["v"], None, s["oa"], s["sta"], doa, seq, a_scale)
        dbq, dbk, dbv, gtab = _band_bwd(s["bq"], s["bk"], s["bv"], s["table"], s["ob"], s["stb"], dob, seq, h_scale)
        g_rel = _band_table_bwd(gtab)[:, 0, :N_REL]
        dcq2, dck, dcv, dfc, dfq = _attn_bwd("fox", s["cq2"], s["ck"], s["cv"], s["fcum"], s["oc"], s["stc"], doc,
                                             seq, h_scale)
        dcf, dfb = _fox_prep_bwd(dfc, dfq, s["cf"], fb_pad[l:l + 1], seq)
        dcq, dckv, dkpe, dqlin, dklin, dgq, dgkv = _mla_prep_bwd(
            dq, dk, dv, s["cq"], s["ckv"], a_q_norm_g[l:l + 1], a_kv_norm_g[l:l + 1],
            wuq_t[l], wk_t[l], wv_t[l], cos_t, sin_a, sin_b)
        gq_pad = _matmul_tn(s["cqn"], dqlin, "dw_uq")
        g_uq[l] = gq_pad.reshape(A_Q_RANK, A_HEADS, HEAD_PAD)[:, :, :A_NOPE + A_ROPE].reshape(A_Q_RANK, -1)
        gk_pad = _matmul_tn(s["ckvn"], dklin, "dw_uk").reshape(A_KV_RANK, A_HEADS, HEAD_PAD)[:, :, :A_NOPE]
        gv_pad = _matmul_tn(s["ckvn"], dv, "dw_uv").reshape(A_KV_RANK, A_HEADS, A_NOPE)
        g_ukv[l] = jnp.concatenate([gk_pad, gv_pad], axis=2).reshape(A_KV_RANK, -1)
        dz = [dcq, dckv, dkpe, dgates, dbq, dbk, dbv, dcq2, dck, dcv, dcf]
        g_in[l] = _unpad_runs(jnp.concatenate([_matmul_tn(s["h"], d, "dw_in_" + nm)
                                               for d, (nm, _, _, _) in zip(dz, Z_SEGS)], axis=1), IN_RUNS, 1)
        dx, dss, dg_norm = _ln_in_bwd(dz, w_in_t[l], s["x"], s["ss"], norm_g[l:l + 1], dx, seq)
        dmods[l] = jnp.concatenate([dss[:, 0, :], dss[:, 1, :], dgate[:, 0, :]], axis=1)
        smalls[l] = [dg_norm.reshape(-1), dgq.reshape(-1), dgkv.reshape(-1), g_rel.reshape(-1),
                     dfb[0, :C_HEADS]]
    grad_x = dx.reshape(nb, seq, D_MODEL)

    small = jnp.concatenate([p for l in range(DEPTH) for p in smalls[l]] + [g_final.reshape(-1)])
    n_small = small.shape[0]
    small_rows = -(-n_small // 1024) * 8
    small = jnp.pad(small, (0, small_rows * 128 - n_small)).reshape(small_rows, 128)
    dmod_local = jnp.stack(dmods)
    dmod_g, small_g = _exchange([dmod_local, small], "gather", "gather_small")
    dmod_all = jnp.transpose(dmod_g, (1, 0, 2, 3)).reshape(DEPTH, N_DEV * nb, 3 * D_MODEL)
    cols = 3 * D_MODEL // N_DEV
    dmod_mine = lax.dynamic_slice_in_dim(dmod_all, me * cols, cols, axis=2)
    g_w_ada, g_b_ada = _ada_bwd(c_act, dmod_all, dmod_mine)
    small_sum = _sum_slots(small_g, "sum_small").reshape(-1)

    rows = D_MODEL // N_DEV
    a2a_in = [jnp.stack(g_in).reshape(DEPTH, N_DEV, rows, N_IN).transpose(1, 0, 2, 3),
              jnp.stack(g_out).reshape(DEPTH, N_DEV, rows, D_MODEL).transpose(1, 0, 2, 3),
              jnp.stack(g_uq).reshape(DEPTH, A_Q_RANK, N_DEV, -1).transpose(2, 0, 1, 3),
              jnp.stack(g_ukv).reshape(DEPTH, A_KV_RANK, N_DEV, -1).transpose(2, 0, 1, 3)]
    p_in, p_out, p_uq, p_ukv = _exchange(a2a_in, "a2a", "a2a_grads")

    def split_small():
        out, pos = [], 0
        sizes = [D_MODEL, A_Q_RANK, A_KV_RANK, B_HEADS * N_REL, C_HEADS]
        per_layer = []
        for l in range(DEPTH):
            parts = []
            for sz in sizes:
                parts.append(small_sum[pos:pos + sz])
                pos += sz
            per_layer.append(parts)
        for j in range(len(sizes)):
            out.append(jnp.stack([per_layer[l][j] for l in range(DEPTH)]))
        out.append(small_sum[pos:pos + D_MODEL])
        return out

    g_norm, g_qn, g_kvn, g_relb, g_fb, g_fin = split_small()

    def adam(w, g, m, v, name, tr=None):
        shp = w.shape
        w2 = w.reshape(-1, shp[-1]) if w.ndim > 1 else w.reshape(1, -1)
        gs = g.reshape((-1,) + w2.shape) if g.size != w.size else g.reshape((1,) + w2.shape)
        outs = _adamw(w2, gs, m.reshape(w2.shape), v.reshape(w2.shape), name, tr)
        return [o.reshape(shp) for o in outs]

    res = {
        "w_ada": adam(w_ada, g_w_ada, m_w_ada, v_w_ada, "adam_w_ada", 256),
        "b_ada": adam(b_ada, g_b_ada, m_b_ada, v_b_ada, "adam_b_ada"),
        "norm_g": adam(norm_g, g_norm, m_norm_g, v_norm_g, "adam_norm_g"),
        "w_in": adam(w_in, p_in, m_w_in, v_w_in, "adam_w_in", 32),
        "a_q_norm_g": adam(a_q_norm_g, g_qn, m_a_q_norm_g, v_a_q_norm_g, "adam_q_norm"),
        "a_w_uq": adam(a_w_uq, p_uq, m_a_w_uq, v_a_w_uq, "adam_w_uq"),
        "a_kv_norm_g": adam(a_kv_norm_g, g_kvn, m_a_kv_norm_g, v_a_kv_norm_g, "adam_kv_norm"),
        "a_w_ukv": adam(a_w_ukv, p_ukv, m_a_w_ukv, v_a_w_ukv, "adam_w_ukv"),
        "b_rel_bias": adam(b_rel_bias, g_relb.reshape(b_rel_bias.shape), m_b_rel_bias, v_b_rel_bias, "adam_rel_bias"),
        "c_forget_b": adam(c_forget_b, g_fb, m_c_forget_b, v_c_forget_b, "adam_forget_b"),
        "w_out": adam(w_out, p_out, m_w_out, v_w_out, "adam_w_out", 64),
        "final_g": adam(final_g, g_fin, m_final_g, v_final_g, "adam_final_g"),
    }
    names = ["w_ada", "b_ada", "norm_g", "w_in", "a_q_norm_g", "a_w_uq", "a_kv_norm_g", "a_w_ukv", "b_rel_bias",
             "c_forget_b", "w_out", "final_g"]
    outs = [loss, grad_x]
    for j in range(4):
        outs += [res[n][j] for n in names]
    return tuple(outs)
```

```python
import functools

import jax
import jax.numpy as jnp
from jax import lax
from jax.experimental import pallas as pl
from jax.experimental.pallas import tpu as pltpu

F32 = jnp.float32
BF16 = jnp.bfloat16
HI = lax.Precision.HIGHEST

N_DEV = 8
AXES = ("x", "y", "c")
D_MODEL = 1024
DEPTH = 2
CHUNK = 64
EPS = 1e-6
NEG = -1e30
A_HEADS = 6
A_NOPE = 64
A_ROPE = 32
A_Q_RANK = 384
A_KV_RANK = 256
ROPE_THETA = 10000.0
B_HEADS = 5
B_LEFT = 512
REL_CLIP = 128
N_REL = 2 * REL_CLIP + 1
C_HEADS = 5
HEAD_PAD = 128
GW = 384
N_IN = 3621
ADAM_LR = 0.001
ADAM_B1 = 0.9
ADAM_B2 = 0.999
ADAM_EPS = 1e-08
ADAM_WD = 0.01
ADAM_STEP = 10
VMEM_LIMIT = 56 * 1024 * 1024

Z_SEGS = (
    ("cq", 0, 384, F32), ("ckv", 384, 256, F32), ("kpe", 640, 128, F32), ("gates", 768, 1152, F32),
    ("bq", 1920, 384, BF16), ("bk", 2304, 384, BF16), ("bv", 2688, 384, BF16),
    ("cq2", 3072, 384, BF16), ("ck", 3456, 384, BF16), ("cv", 3840, 384, BF16), ("cf", 4224, 128, F32),
)
N_PAD = 4352
IN_RUNS = (
    (0, 384, 0), (384, 256, 384), (640 + 64, 32, 640),
    (768, 384, 672), (768 + 384, 320, 2016), (768 + 768, 320, 3301),
    (1920, 320, 1056), (2304, 320, 1376), (2688, 320, 1696),
    (3072, 320, 2336), (3456, 320, 2656), (3840, 320, 2976), (4224, 5, 3296),
)
OUT_RUNS = ((0, 384, 0), (384, 320, 384), (768, 320, 704))
U_PAD = 1152


def _cparams(sem=None, vmem=VMEM_LIMIT):
    return pltpu.CompilerParams(dimension_semantics=sem, vmem_limit_bytes=vmem)


def _pad_runs(w, runs, total, axis):
    order = sorted(runs)
    parts, pos = [], 0
    for off, wd, src in order:
        if off > pos:
            shp = list(w.shape)
            shp[axis] = off - pos
            parts.append(jnp.zeros(shp, w.dtype))
        parts.append(lax.slice_in_dim(w, src, src + wd, axis=axis))
        pos = off + wd
    if pos < total:
        shp = list(w.shape)
        shp[axis] = total - pos
        parts.append(jnp.zeros(shp, w.dtype))
    return jnp.concatenate(parts, axis=axis)


def _unpad_runs(w, runs, axis):
    order = sorted(runs, key=lambda r: r[2])
    return jnp.concatenate([lax.slice_in_dim(w, off, off + wd, axis=axis) for off, wd, _ in order], axis=axis)


def _sigmoid(x):
    return 1.0 / (1.0 + jnp.exp(-x))


N_CHIP = 4
ANY_SPEC = pl.BlockSpec(memory_space=pl.ANY)
MESH_ID = pl.DeviceIdType.MESH


def _gather(arrs, name):
    n = len(arrs)

    def body(*refs):
        ins, outs = refs[:n], refs[n:2 * n]
        send_sems, recv_sems, local_sems = refs[2 * n:]
        x, y, c = lax.axis_index("x"), lax.axis_index("y"), lax.axis_index("c")
        me, sib = (x, y, c), (x, y, 1 - c)
        chips = [(1 - x, y), (x, 1 - y), (1 - x, 1 - y)]

        def slot(px, py, pc):
            return 4 * px + 2 * py + pc

        def copy(a, k, block, to, src=None):
            dst = outs[a].at[slot(*block)]
            return pltpu.make_async_remote_copy(
                src_ref=dst if src is None else src, dst_ref=dst, send_sem=send_sems.at[a, k],
                recv_sem=recv_sems.at[a, k], device_id=to, device_id_type=MESH_ID)

        local = [pltpu.make_async_copy(ins[a], outs[a].at[slot(*me)], local_sems.at[a]) for a in range(n)]
        first = []
        for a in range(n):
            first.append(copy(a, 0, me, sib, src=ins[a]))
            first += [copy(a, 1 + j, me, (*chip, c), src=ins[a]) for j, chip in enumerate(chips)]
        for cp in local + first:
            cp.start()
        passed = []
        for j, chip in enumerate(chips):
            for a in range(n):
                copy(a, 1 + j, (*chip, c), me).wait_recv()
                fwd = copy(a, 4 + j, (*chip, c), sib)
                fwd.start()
                passed.append(fwd)
        for a in range(n):
            copy(a, 0, sib, me).wait_recv()
            for j, chip in enumerate(chips):
                copy(a, 4 + j, (*chip, 1 - c), me).wait_recv()
        for cp in first + passed:
            cp.wait_send()
        for cp in local:
            cp.wait()

    return pl.pallas_call(
        body, name=name, out_shape=[jax.ShapeDtypeStruct((N_DEV,) + a.shape, a.dtype) for a in arrs],
        in_specs=[ANY_SPEC] * n, out_specs=[ANY_SPEC] * n,
        scratch_shapes=[pltpu.SemaphoreType.DMA((n, N_DEV - 1)), pltpu.SemaphoreType.DMA((n, N_DEV - 1)),
                        pltpu.SemaphoreType.DMA((n,))],
    )(*arrs)


def _pair_swap(arrs, name):
    n = len(arrs)

    def body(*refs):
        ins, outs = refs[:n], refs[n:2 * n]
        send_sems, recv_sems = refs[2 * n:]
        x, y, c = lax.axis_index("x"), lax.axis_index("y"), lax.axis_index("c")
        copies = []
        for a in range(n):
            for q in range(N_CHIP):
                cp = pltpu.make_async_remote_copy(
                    src_ref=ins[a].at[2 * q + 1 - c], dst_ref=outs[a].at[q], send_sem=send_sems.at[a, q],
                    recv_sem=recv_sems.at[a, q], device_id=(x, y, 1 - c), device_id_type=MESH_ID)
                cp.start()
                copies.append(cp)
        for cp in copies:
            cp.wait()

    return pl.pallas_call(
        body, name=name, out_shape=[jax.ShapeDtypeStruct((N_CHIP,) + a.shape[1:], a.dtype) for a in arrs],
        in_specs=[ANY_SPEC] * n, out_specs=[ANY_SPEC] * n,
        scratch_shapes=[pltpu.SemaphoreType.DMA((n, N_CHIP)), pltpu.SemaphoreType.DMA((n, N_CHIP))],
    )(*arrs)


def _chip_a2a(arrs, name):
    n = len(arrs)

    def body(*refs):
        ins, outs = refs[:n], refs[n:2 * n]
        send_sems, recv_sems, local_sems = refs[2 * n:]
        x, y, c = lax.axis_index("x"), lax.axis_index("y"), lax.axis_index("c")
        mine = 2 * x + y
        copies = []
        for a in range(n):
            loc = pltpu.make_async_copy(ins[a].at[mine], outs[a].at[mine], local_sems.at[a])
            loc.start()
            copies.append(loc)
            for k in range(1, N_CHIP):
                px = (1 - x) if (k >> 1) & 1 else x
                py = (1 - y) if k & 1 else y
                cp = pltpu.make_async_remote_copy(
                    src_ref=ins[a].at[2 * px + py], dst_ref=outs[a].at[mine], send_sem=send_sems.at[a, k - 1],
                    recv_sem=recv_sems.at[a, k - 1], device_id=(px, py, c), device_id_type=MESH_ID)
                cp.start()
                copies.append(cp)
        for cp in copies:
            cp.wait()

    return pl.pallas_call(
        body, name=name, out_shape=[jax.ShapeDtypeStruct(a.shape, a.dtype) for a in arrs],
        in_specs=[ANY_SPEC] * n, out_specs=[ANY_SPEC] * n,
        scratch_shapes=[pltpu.SemaphoreType.DMA((n, N_CHIP - 1)), pltpu.SemaphoreType.DMA((n, N_CHIP - 1)),
                        pltpu.SemaphoreType.DMA((n,))],
    )(*arrs)


def _pair_add(a, b, name, tr):
    _, r, c = a.shape

    def body(a_ref, b_ref, o_ref):
        o_ref[...] = (a_ref[...] + b_ref[...]).astype(BF16)

    blk = pl.BlockSpec((1, tr, c), lambda q, i: (q, i, 0))
    return pl.pallas_call(
        body, name=name, grid=(N_CHIP, r // tr), in_specs=[blk, blk], out_specs=blk,
        out_shape=jax.ShapeDtypeStruct(a.shape, BF16), compiler_params=_cparams(("arbitrary", "arbitrary")),
    )(a, b)


def _sum_slots(x, name):
    _, r, c = x.shape

    def body(x_ref, o_ref):
        acc = x_ref[0]
        for j in range(1, N_DEV):
            acc = acc + x_ref[j]
        o_ref[...] = acc

    return pl.pallas_call(body, name=name, out_shape=jax.ShapeDtypeStruct((r, c), F32))(x)


def _ada_fwd(c_all, w_ada):
    nb = c_all.shape[0]
    cols = w_ada.shape[2]

    def body(c_ref, w_ref, act_ref, mod_ref):
        cv = c_ref[...]
        act = cv * _sigmoid(cv)
        act_ref[...] = act
        for l in range(DEPTH):
            mod_ref[l] = jnp.dot(act, w_ref[l], precision=HI, preferred_element_type=F32)

    return pl.pallas_call(
        body, name="ada_fwd",
        out_shape=[jax.ShapeDtypeStruct((nb, D_MODEL), F32), jax.ShapeDtypeStruct((DEPTH, nb, cols), F32)],
        compiler_params=_cparams(),
    )(c_all, w_ada)


def _ada_bwd(c_act, dmod_all, dmod_mine):
    nb = c_act.shape[0]
    cols = dmod_mine.shape[2]

    def body(act_ref, dall_ref, dmine_ref, gw_ref, gb_ref):
        act = act_ref[...]
        for l in range(DEPTH):
            gw_ref[l] = lax.dot_general(act, dmine_ref[l], (((0,), (0,)), ((), ())),
                                        precision=HI, preferred_element_type=F32)
            gb_ref[l:l + 1, :] = jnp.sum(dall_ref[l], axis=0, keepdims=True)

    return pl.pallas_call(
        body, name="ada_bwd",
        out_shape=[jax.ShapeDtypeStruct((DEPTH, D_MODEL, cols), F32),
                   jax.ShapeDtypeStruct((DEPTH, 3 * D_MODEL), F32)],
        compiler_params=_cparams(),
    )(c_act, dmod_all, dmod_mine)


def _ln_in(x, ss, g, w, seq, tm=256):
    t = x.shape[0]
    tps = seq // tm

    def body(x_ref, ss_ref, g_ref, w_ref, h_ref, *outs):
        xv = x_ref[...]
        xn = xv * lax.rsqrt(jnp.mean(xv * xv, axis=-1, keepdims=True) + EPS)
        h = xn * g_ref[...] * ss_ref[0, 1:2, :] + ss_ref[0, 0:1, :]
        hb = h.astype(BF16)
        h_ref[...] = hb
        for o_ref, (_, off, wd, _) in zip(outs, Z_SEGS):
            o_ref[...] = jnp.dot(hb, w_ref[:, off:off + wd], preferred_element_type=F32).astype(o_ref.dtype)

    row = lambda wd: pl.BlockSpec((tm, wd), lambda i: (i, 0))
    return pl.pallas_call(
        body, name="ln_in", grid=(t // tm,),
        in_specs=[row(D_MODEL), pl.BlockSpec((1, 2, D_MODEL), lambda i: (i // tps, 0, 0)),
                  pl.BlockSpec((1, D_MODEL), lambda i: (0, 0)), pl.BlockSpec((D_MODEL, N_PAD), lambda i: (0, 0))],
        out_specs=[row(D_MODEL)] + [row(wd) for _, _, wd, _ in Z_SEGS],
        out_shape=[jax.ShapeDtypeStruct((t, D_MODEL), BF16)]
        + [jax.ShapeDtypeStruct((t, wd), dt) for _, _, wd, dt in Z_SEGS],
        compiler_params=_cparams(("arbitrary",)),
    )(x, ss, g, w)


def _ln_in_bwd(dz, w_t, x, ss, g, dxo, seq, tm=256):
    t = x.shape[0]
    tps = seq // tm
    nb = t // seq
    nz = len(Z_SEGS)

    def body(*refs):
        dz_refs = refs[:nz]
        wt_ref, x_ref, ss_ref, g_ref, dxo_ref, dx_ref, dss_ref, dg_ref = refs[nz:]
        i = pl.program_id(0)
        dh = None
        for r, (_, off, wd, _) in zip(dz_refs, Z_SEGS):
            part = jnp.dot(r[...].astype(BF16), wt_ref[off:off + wd, :], preferred_element_type=F32)
            dh = part if dh is None else dh + part
        xv = x_ref[...]
        rstd = lax.rsqrt(jnp.mean(xv * xv, axis=-1, keepdims=True) + EPS)
        xn = xv * rstd
        gv = g_ref[...]
        s1 = ss_ref[0, 1:2, :]
        dxg = dh * s1
        dxn = dxg * gv
        dx = rstd * (dxn - xn * jnp.mean(dxn * xn, axis=-1, keepdims=True))
        dx_ref[...] = dxo_ref[...] + dx
        dshift = jnp.sum(dh, axis=0, keepdims=True)
        dscale = jnp.sum(dh * (xn * gv), axis=0, keepdims=True)
        dgp = jnp.sum(dxg * xn, axis=0, keepdims=True)

        @pl.when(i % tps == 0)
        def _():
            dss_ref[0, 0:1, :] = dshift
            dss_ref[0, 1:2, :] = dscale

        @pl.when(i % tps != 0)
        def _():
            dss_ref[0, 0:1, :] += dshift
            dss_ref[0, 1:2, :] += dscale

        @pl.when(i == 0)
        def _():
            dg_ref[...] = dgp

        @pl.when(i != 0)
        def _():
            dg_ref[...] += dgp

    row = lambda wd: pl.BlockSpec((tm, wd), lambda i: (i, 0))
    return pl.pallas_call(
        body, name="ln_in_bwd", grid=(t // tm,),
        in_specs=[row(wd) for _, _, wd, _ in Z_SEGS]
        + [pl.BlockSpec((N_PAD, D_MODEL), lambda i: (0, 0)), row(D_MODEL),
           pl.BlockSpec((1, 2, D_MODEL), lambda i: (i // tps, 0, 0)),
           pl.BlockSpec((1, D_MODEL), lambda i: (0, 0)), row(D_MODEL)],
        out_specs=[row(D_MODEL), pl.BlockSpec((1, 2, D_MODEL), lambda i: (i // tps, 0, 0)),
                   pl.BlockSpec((1, D_MODEL), lambda i: (0, 0))],
        out_shape=[jax.ShapeDtypeStruct((t, D_MODEL), F32), jax.ShapeDtypeStruct((nb, 2, D_MODEL), F32),
                   jax.ShapeDtypeStruct((1, D_MODEL), F32)],
        compiler_params=_cparams(("arbitrary",)),
    )(*dz, w_t, x, ss, g, dxo)


def _matmul_tn(a, b, name, tm=512):
    t, k = a.shape
    n = b.shape[1]
    tm = min(tm, t)

    def body(a_ref, b_ref, o_ref):
        i = pl.program_id(0)
        part = lax.dot_general(a_ref[...].astype(BF16), b_ref[...].astype(BF16), (((0,), (0,)), ((), ())),
                               preferred_element_type=F32)

        @pl.when(i == 0)
        def _():
            o_ref[...] = part

        @pl.when(i != 0)
        def _():
            o_ref[...] += part

    return pl.pallas_call(
        body, name=name, grid=(t // tm,),
        in_specs=[pl.BlockSpec((tm, k), lambda i: (i, 0)), pl.BlockSpec((tm, n), lambda i: (i, 0))],
        out_specs=pl.BlockSpec((k, n), lambda i: (0, 0)),
        out_shape=jax.ShapeDtypeStruct((k, n), F32),
        compiler_params=_cparams(("arbitrary",)),
    )(a, b)


def _rope(blk, cos_t, sin_a, sin_b):
    return blk * cos_t + pltpu.roll(blk, 112, 1) * sin_a + pltpu.roll(blk, 16, 1) * sin_b


def _unrope(d, cos_t, sin_a, sin_b):
    return d * cos_t + pltpu.roll(d * sin_a, 16, 1) + pltpu.roll(d * sin_b, 112, 1)


def _mla_prep(cq, ckv, kpe, gq, gkv, wuq, wk, wv, cos_t, sin_a, sin_b, tm=256):
    t = cq.shape[0]
    qw = A_HEADS * HEAD_PAD

    def body(cq_ref, ckv_ref, kpe_ref, gq_ref, gkv_ref, wuq_ref, wk_ref, wv_ref, c_ref, sa_ref, sb_ref,
             q_ref, k_ref, v_ref, cqn_ref, ckvn_ref):
        ct, sa, sb = c_ref[...], sa_ref[...], sb_ref[...]
        a = cq_ref[...]
        cqn = (a * lax.rsqrt(jnp.mean(a * a, axis=-1, keepdims=True) + EPS) * gq_ref[...]).astype(BF16)
        cqn_ref[...] = cqn
        b = ckv_ref[...]
        ckvn = (b * lax.rsqrt(jnp.mean(b * b, axis=-1, keepdims=True) + EPS) * gkv_ref[...]).astype(BF16)
        ckvn_ref[...] = ckvn
        qlin = jnp.dot(cqn, wuq_ref[...], preferred_element_type=F32)
        klin = jnp.dot(ckvn, wk_ref[...], preferred_element_type=F32)
        v_ref[...] = jnp.dot(ckvn, wv_ref[...], preferred_element_type=F32).astype(BF16)
        kr = _rope(kpe_ref[...], ct, sa, sb)
        for h in range(A_HEADS):
            sl = slice(h * HEAD_PAD, (h + 1) * HEAD_PAD)
            q_ref[:, sl] = _rope(qlin[:, sl], ct, sa, sb).astype(BF16)
            k_ref[:, sl] = (klin[:, sl] + kr).astype(BF16)

    row = lambda wd: pl.BlockSpec((tm, wd), lambda i: (i, 0))
    full = lambda r, c: pl.BlockSpec((r, c), lambda i: (0, 0))
    return pl.pallas_call(
        body, name="mla_prep", grid=(t // tm,),
        in_specs=[row(A_Q_RANK), row(A_KV_RANK), row(128), full(1, A_Q_RANK), full(1, A_KV_RANK),
                  full(A_Q_RANK, qw), full(A_KV_RANK, qw), full(A_KV_RANK, GW), row(128), row(128), row(128)],
        out_specs=[row(qw), row(qw), row(GW), row(A_Q_RANK), row(A_KV_RANK)],
        out_shape=[jax.ShapeDtypeStruct((t, qw), BF16), jax.ShapeDtypeStruct((t, qw), BF16),
                   jax.ShapeDtypeStruct((t, GW), BF16), jax.ShapeDtypeStruct((t, A_Q_RANK), BF16),
                   jax.ShapeDtypeStruct((t, A_KV_RANK), BF16)],
        compiler_params=_cparams(("arbitrary",)),
    )(cq, ckv, kpe, gq, gkv, wuq, wk, wv, cos_t, sin_a, sin_b)


def _mla_prep_bwd(dq, dk, dv, cq, ckv, gq, gkv, wuq_t, wk_t, wv_t, cos_t, sin_a, sin_b, tm=256):
    t = cq.shape[0]
    qw = A_HEADS * HEAD_PAD

    def body(dq_ref, dk_ref, dv_ref, cq_ref, ckv_ref, gq_ref, gkv_ref, wuqt_ref, wkt_ref, wvt_ref,
             c_ref, sa_ref, sb_ref, dcq_ref, dckv_ref, dkpe_ref, dql_ref, dkl_ref, dgq_ref, dgkv_ref):
        i = pl.program_id(0)
        ct, sa, sb = c_ref[...], sa_ref[...], sb_ref[...]
        lane = lax.broadcasted_iota(jnp.int32, (1, HEAD_PAD), 1)
        nope = lane < A_NOPE
        rope = (lane >= A_NOPE) & (lane < A_NOPE + A_ROPE)
        dksum = None
        for h in range(A_HEADS):
            sl = slice(h * HEAD_PAD, (h + 1) * HEAD_PAD)
            dql_ref[:, sl] = _unrope(dq_ref[:, sl], ct, sa, sb).astype(BF16)
            dkh = dk_ref[:, sl]
            dkl_ref[:, sl] = jnp.where(nope, dkh, 0.0).astype(BF16)
            dksum = dkh if dksum is None else dksum + dkh
        dkpe_ref[...] = jnp.where(rope, _unrope(jnp.where(rope, dksum, 0.0), ct, sa, sb), 0.0)
        dcqn = jnp.dot(dql_ref[...], wuqt_ref[...], preferred_element_type=F32)
        dckvn = (jnp.dot(dkl_ref[...], wkt_ref[...], preferred_element_type=F32)
                 + jnp.dot(dv_ref[...].astype(BF16), wvt_ref[...], preferred_element_type=F32))

        def norm_bwd(xv, gv, dy):
            rstd = lax.rsqrt(jnp.mean(xv * xv, axis=-1, keepdims=True) + EPS)
            xn = xv * rstd
            dxn = dy * gv
            dx = rstd * (dxn - xn * jnp.mean(dxn * xn, axis=-1, keepdims=True))
            return dx, jnp.sum(dy * xn, axis=0, keepdims=True)

        dcq, dgq = norm_bwd(cq_ref[...], gq_ref[...], dcqn)
        dckv, dgkv = norm_bwd(ckv_ref[...], gkv_ref[...], dckvn)
        dcq_ref[...] = dcq
        dckv_ref[...] = dckv

        @pl.when(i == 0)
        def _():
            dgq_ref[...] = dgq
            dgkv_ref[...] = dgkv

        @pl.when(i != 0)
        def _():
            dgq_ref[...] += dgq
            dgkv_ref[...] += dgkv

    row = lambda wd: pl.BlockSpec((tm, wd), lambda i: (i, 0))
    full = lambda r, c: pl.BlockSpec((r, c), lambda i: (0, 0))
    return pl.pallas_call(
        body, name="mla_prep_bwd", grid=(t // tm,),
        in_specs=[row(qw), row(qw), row(GW), row(A_Q_RANK), row(A_KV_RANK), full(1, A_Q_RANK), full(1, A_KV_RANK),
                  full(qw, A_Q_RANK), full(qw, A_KV_RANK), full(GW, A_KV_RANK), row(128), row(128), row(128)],
        out_specs=[row(A_Q_RANK), row(A_KV_RANK), row(128), row(qw), row(qw), full(1, A_Q_RANK), full(1, A_KV_RANK)],
        out_shape=[jax.ShapeDtypeStruct((t, A_Q_RANK), F32), jax.ShapeDtypeStruct((t, A_KV_RANK), F32),
                   jax.ShapeDtypeStruct((t, 128), F32), jax.ShapeDtypeStruct((t, qw), BF16),
                   jax.ShapeDtypeStruct((t, qw), BF16), jax.ShapeDtypeStruct((1, A_Q_RANK), F32),
                   jax.ShapeDtypeStruct((1, A_KV_RANK), F32)],
        compiler_params=_cparams(("arbitrary",)),
    )(dq, dk, dv, cq, ckv, gq, gkv, wuq_t, wk_t, wv_t, cos_t, sin_a, sin_b)


def _nt(a, b):
    return lax.dot_general(a, b, (((1,), (1,)), ((), ())), preferred_element_type=F32)


def _tn(a, b):
    return lax.dot_general(a, b, (((0,), (0,)), ((), ())), preferred_element_type=F32)


def _causal_mask(kind, q0, k0, tq, tk):
    qpos = q0 + lax.broadcasted_iota(jnp.int32, (tq, tk), 0)
    kpos = k0 + lax.broadcasted_iota(jnp.int32, (tq, tk), 1)
    if kind == "mla":
        return lax.shift_right_logical(kpos, 6) <= lax.shift_right_logical(qpos, 6)
    return kpos <= qpos


def _attn_fwd(kind, q, k, v, f, seq, scale, tq=512, tk=512):
    t = v.shape[0]
    nb = t // seq
    nq = seq // tq
    hw = 256 if kind == "mla" else 128
    use_f = f is not None
    tq, tk = min(tq, seq), min(tk, seq)
    nq = seq // tq
    assert tk % tq == 0

    def body(*refs):
        if use_f:
            q_ref, k_ref, v_ref, f_ref, o_ref, st_ref = refs
        else:
            q_ref, k_ref, v_ref, o_ref, st_ref = refs
        qi = pl.program_id(2)
        q0 = qi * tq
        lane = lax.broadcasted_iota(jnp.int32, (1, 128), 1)
        half = lane >= 64
        qall = q_ref[...]
        if kind == "mla":
            qhs = [qall[:, 0:128], qall[:, 128:256]]
        else:
            qhs = [jnp.where(half, jnp.zeros_like(qall), qall), jnp.where(half, qall, jnp.zeros_like(qall))]
        nfull = q0 // tk
        kd = pl.multiple_of(nfull * tk, tk)
        diag = _causal_mask(kind, q0 - kd, 0, tq, tk)

        def block(j, k0, state, masked):
            m, l, acc = state
            kh = k_ref[pl.ds(k0, tk), j * 128:(j + 1) * 128] if kind == "mla" else k_ref[pl.ds(k0, tk), :]
            s = _nt(qhs[j], kh) * scale
            if use_f:
                s = s - f_ref[0, 0, j:j + 1, pl.ds(k0, tk)]
            if masked:
                s = jnp.where(diag, s, NEG)
            mn = jnp.maximum(m, jnp.max(s, axis=-1, keepdims=True))
            alpha = jnp.exp(m - mn)
            p = jnp.exp(s - mn)
            l = alpha * l + jnp.sum(p, axis=-1, keepdims=True)
            acc = alpha * acc + jnp.dot(p.astype(BF16), v_ref[pl.ds(k0, tk), :], preferred_element_type=F32)
            return mn, l, acc

        def kstep(kb, carry):
            k0 = pl.multiple_of(kb * tk, tk)
            return block(0, k0, carry[:3], False) + block(1, k0, carry[3:], False)

        init = (jnp.full((tq, 1), NEG, F32), jnp.zeros((tq, 1), F32), jnp.zeros((tq, 128), F32)) * 2
        carry = lax.fori_loop(0, nfull, kstep, init)
        m0, l0, a0 = block(0, kd, carry[:3], True)
        m1, l1, a1 = block(1, kd, carry[3:], True)
        o_ref[...] = jnp.where(half, a1 / l1, a0 / l0)
        st_ref[...] = jnp.where(lane == 0, m0 + jnp.log(l0), jnp.where(lane == 1, m1 + jnp.log(l1), 0.0))

    in_specs = [pl.BlockSpec((tq, hw), lambda b, p, i: (b * nq + i, p)),
                pl.BlockSpec((seq, hw), lambda b, p, i: (b, p)),
                pl.BlockSpec((seq, 128), lambda b, p, i: (b, p))]
    args = [q, k, v]
    if use_f:
        in_specs.append(pl.BlockSpec((1, 1, 8, seq), lambda b, p, i: (b, p, 0, 0)))
        args.append(f)
    oblk = pl.BlockSpec((tq, 128), lambda b, p, i: (b * nq + i, p))
    return pl.pallas_call(
        body, name="attn_fwd_" + kind, grid=(nb, 3, nq), in_specs=in_specs, out_specs=[oblk, oblk],
        out_shape=[jax.ShapeDtypeStruct((t, GW), F32), jax.ShapeDtypeStruct((t, GW), F32)],
        compiler_params=_cparams(("arbitrary", "arbitrary", "arbitrary")),
    )(*args)


def _attn_bwd(kind, q, k, v, f, o, st, do, seq, scale, tq=512, tk=512):
    t = v.shape[0]
    nb = t // seq
    tq, tk = min(tq, seq), min(tk, seq)
    nq = seq // tq
    nk = seq // tk
    hw = 256 if kind == "mla" else 128
    use_f = f is not None
    assert tq == tk

    def body(*refs):
        if use_f:
            q_ref, k_ref, v_ref, f_ref, o_ref, st_ref, do_ref, dq_ref, dk_ref, dv_ref, df_ref, dfq_ref = refs
        else:
            q_ref, k_ref, v_ref, o_ref, st_ref, do_ref, dq_ref, dk_ref, dv_ref = refs
        kj = pl.program_id(2)
        k0 = kj * tk
        lane = lax.broadcasted_iota(jnp.int32, (1, 128), 1)
        half = lane >= 64

        @pl.when(kj == 0)
        def _():
            dq_ref[...] = jnp.zeros_like(dq_ref)
            if use_f:
                dfq_ref[...] = jnp.zeros_like(dfq_ref)

        dk_ref[...] = jnp.zeros_like(dk_ref)
        dv_ref[...] = jnp.zeros_like(dv_ref)
        if use_f:
            df_ref[...] = jnp.zeros_like(df_ref)
        vv = v_ref[...]
        diag = _causal_mask(kind, 0, 0, tq, tk)

        def qstep(qi, masked):
            q0 = pl.multiple_of(qi * tq, tq)
            rows = pl.ds(q0, tq)
            dov = do_ref[rows, :]
            dd = dov * o_ref[rows, :]
            stv = st_ref[rows, :]
            for j in range(2):
                hm = half == bool(j)
                delta = jnp.sum(jnp.where(hm, dd, 0.0), axis=-1, keepdims=True)
                lse = stv[:, j:j + 1]
                if kind == "mla":
                    cols = slice(j * 128, (j + 1) * 128)
                    qh = q_ref[rows, cols]
                    kh = k_ref[:, cols]
                else:
                    cols = slice(0, 128)
                    qa = q_ref[rows, :]
                    qh = jnp.where(hm, qa, jnp.zeros_like(qa))
                    kh = k_ref[...]
                s = _nt(qh, kh) * scale
                if use_f:
                    s = s - f_ref[0, 0, j:j + 1, :]
                if masked:
                    s = jnp.where(diag, s, NEG)
                p = jnp.exp(s - lse)
                doh = jnp.where(hm, dov, 0.0).astype(BF16)
                ds = p * (_nt(doh, vv) - delta)
                dsb = (ds * scale).astype(BF16)
                dv_ref[...] += _tn(p.astype(BF16), doh)
                dk_ref[:, cols] += _tn(dsb, qh)
                dqc = jnp.dot(dsb, kh, preferred_element_type=F32)
                if kind != "mla":
                    dqc = jnp.where(hm, dqc, 0.0)
                dq_ref[rows, cols] += dqc
                if use_f:
                    df_ref[0, 0, j:j + 1, :] += -jnp.sum(ds, axis=0, keepdims=True)
                    dfq_ref[rows, :] += jnp.where(lane == j, jnp.sum(ds, axis=-1, keepdims=True), 0.0)

        qstep(kj, True)

        def rest(qi, carry):
            qstep(qi, False)
            return carry

        lax.fori_loop(kj + 1, nq, rest, 0)

    full_q = lambda wd: pl.BlockSpec((seq, wd), lambda b, p, i: (b, p))
    kblk = lambda wd: pl.BlockSpec((tk, wd), lambda b, p, i: (b * nk + i, p))
    in_specs = [full_q(hw), kblk(hw), kblk(128)]
    args = [q, k, v]
    if use_f:
        in_specs.append(pl.BlockSpec((1, 1, 8, tk), lambda b, p, i: (b, p, 0, i)))
        args.append(f)
    in_specs += [full_q(128), full_q(128), full_q(128)]
    args += [o, st, do]
    out_specs = [full_q(hw), kblk(hw), kblk(128)]
    out_shape = [jax.ShapeDtypeStruct((t, 3 * hw), F32), jax.ShapeDtypeStruct((t, 3 * hw), F32),
                 jax.ShapeDtypeStruct((t, GW), F32)]
    if use_f:
        out_specs += [pl.BlockSpec((1, 1, 8, tk), lambda b, p, i: (b, p, 0, i)), full_q(128)]
        out_shape += [jax.ShapeDtypeStruct((nb, 3, 8, seq), F32), jax.ShapeDtypeStruct((t, GW), F32)]
    return pl.pallas_call(
        body, name="attn_bwd_" + kind, grid=(nb, 3, nk), in_specs=in_specs, out_specs=out_specs,
        out_shape=out_shape, compiler_params=_cparams(("arbitrary", "arbitrary", "arbitrary")),
    )(*args)


BQ = 256
BWIN = BQ + B_LEFT


def _band_geometry():
    r = lax.broadcasted_iota(jnp.int32, (BQ, BWIN), 0)
    j = lax.broadcasted_iota(jnp.int32, (BQ, BWIN), 1)
    rc = lax.shift_right_logical(r, 6)
    jc = lax.shift_right_logical(j, 6)
    allowed = (jc - 8 <= rc) & (rc <= jc)
    return (r + B_LEFT - j) >= REL_CLIP, allowed, j < r


def _band_onehot(transposed, offset=0):
    shape = (BWIN, GW) if transposed else (GW, BWIN)
    kk = lax.broadcasted_iota(jnp.int32, shape, 1 if transposed else 0)
    x = lax.broadcasted_iota(jnp.int32, shape, 0 if transposed else 1) - offset
    x = jnp.where(x < 0, x + BWIN, x)
    return (kk == jnp.clip(B_LEFT - x, -REL_CLIP, REL_CLIP) + REL_CLIP).astype(F32)


def _band_table(rel_bias8):
    def body(b_ref, o_ref):
        hh = pl.program_id(0)
        u8 = jnp.dot(b_ref[...], _band_onehot(False), precision=HI, preferred_element_type=F32)
        rid = lax.broadcasted_iota(jnp.int32, (8, BWIN), 0)
        row = jnp.sum(jnp.where(rid == hh, u8, 0.0), axis=0, keepdims=True)
        far, allowed, _ = _band_geometry()
        tbl = pltpu.roll(jnp.broadcast_to(row, (BQ, BWIN)), 0, 1, stride=1, stride_axis=0)
        tbl = jnp.where(far, row[:, 0:1], tbl)
        o_ref[0] = jnp.where(allowed, tbl, NEG)

    return pl.pallas_call(
        body, name="band_table", grid=(6,),
        in_specs=[pl.BlockSpec((8, GW), lambda h: (0, 0))],
        out_specs=pl.BlockSpec((1, BQ, BWIN), lambda h: (h, 0, 0)),
        out_shape=jax.ShapeDtypeStruct((6, BQ, BWIN), F32),
        compiler_params=_cparams(("arbitrary",)),
    )(rel_bias8)


def _band_table_bwd(gtab):
    def body(g_ref, o_ref):
        gv = g_ref[0]
        _, _, wrapped = _band_geometry()
        gfar = jnp.sum(jnp.sum(jnp.where(wrapped, gv, 0.0), axis=-1, keepdims=True), axis=0, keepdims=True)
        anti = (lax.broadcasted_iota(jnp.int32, (BQ, BQ), 0) + lax.broadcasted_iota(jnp.int32, (BQ, BQ), 1)
                == BQ - 1).astype(F32)
        grev = jnp.dot(anti, jnp.where(wrapped, 0.0, gv), precision=HI, preferred_element_type=F32)
        near = pltpu.roll(grev, 0, 1, stride=1, stride_axis=0)
        y = jnp.broadcast_to(jnp.sum(near, axis=0, keepdims=True), (8, BWIN))
        gb = jnp.dot(y, _band_onehot(True, BQ - 1), precision=HI, preferred_element_type=F32)
        lane = lax.broadcasted_iota(jnp.int32, (8, GW), 1)
        o_ref[0] = gb + jnp.where(lane == 2 * REL_CLIP, gfar, 0.0)

    return pl.pallas_call(
        body, name="band_table_bwd", grid=(B_HEADS,),
        in_specs=[pl.BlockSpec((1, BQ, BWIN), lambda h: (h, 0, 0))],
        out_specs=pl.BlockSpec((1, 8, GW), lambda h: (h, 0, 0)),
        out_shape=jax.ShapeDtypeStruct((B_HEADS, 8, GW), F32),
        compiler_params=_cparams(("arbitrary",)),
    )(gtab)


def _band_fwd(q, k, v, table, seq, scale):
    t = q.shape[0]
    nb = t // seq
    nq = seq // BQ

    def body(q_ref, k_ref, v_ref, tb_ref, o_ref, st_ref, kpad, vpad):
        qi = pl.program_id(2)
        q0 = pl.multiple_of(qi * BQ, BQ)
        lane = lax.broadcasted_iota(jnp.int32, (1, 128), 1)
        half = lane >= 64

        @pl.when(qi == 0)
        def _():
            kpad[0:B_LEFT, :] = jnp.zeros((B_LEFT, 128), BF16)
            vpad[0:B_LEFT, :] = jnp.zeros((B_LEFT, 128), BF16)
            kpad[B_LEFT:, :] = k_ref[...]
            vpad[B_LEFT:, :] = v_ref[...]

        kw = kpad[pl.ds(q0, BWIN), :]
        vw = vpad[pl.ds(q0, BWIN), :]
        inside = lax.broadcasted_iota(jnp.int32, (BQ, BWIN), 1) >= B_LEFT - q0
        qall = q_ref[...]
        outs, lses = [], []
        for j in range(2):
            qh = jnp.where(half == bool(j), qall, jnp.zeros_like(qall))
            s = jnp.where(inside, _nt(qh, kw) * scale + tb_ref[j], NEG)
            m = jnp.max(s, axis=-1, keepdims=True)
            p = jnp.exp(s - m)
            l = jnp.sum(p, axis=-1, keepdims=True)
            outs.append(jnp.dot(p.astype(BF16), vw, preferred_element_type=F32) / l)
            lses.append(m + jnp.log(l))
        o_ref[...] = jnp.where(half, outs[1], outs[0])
        st_ref[...] = jnp.where(lane == 0, lses[0], jnp.where(lane == 1, lses[1], 0.0))

    qblk = pl.BlockSpec((BQ, 128), lambda b, p, i: (b * nq + i, p))
    full = pl.BlockSpec((seq, 128), lambda b, p, i: (b, p))
    return pl.pallas_call(
        body, name="band_fwd", grid=(nb, 3, nq),
        in_specs=[qblk, full, full, pl.BlockSpec((2, BQ, BWIN), lambda b, p, i: (p, 0, 0))],
        out_specs=[qblk, qblk],
        out_shape=[jax.ShapeDtypeStruct((t, GW), F32), jax.ShapeDtypeStruct((t, GW), F32)],
        scratch_shapes=[pltpu.VMEM((seq + B_LEFT, 128), BF16), pltpu.VMEM((seq + B_LEFT, 128), BF16)],
        compiler_params=_cparams(("arbitrary", "arbitrary", "arbitrary")),
    )(q, k, v, table)


def _band_bwd(q, k, v, table, o, st, do, seq, scale):
    t = q.shape[0]
    nb = t // seq
    nq = seq // BQ

    def body(q_ref, k_ref, v_ref, tb_ref, o_ref, st_ref, do_ref, dq_ref, dk_ref, dv_ref, g_ref,
             kpad, vpad, dkpad, dvpad):
        b = pl.program_id(1)
        qi = pl.program_id(2)
        q0 = pl.multiple_of(qi * BQ, BQ)
        lane = lax.broadcasted_iota(jnp.int32, (1, 128), 1)
        half = lane >= 64

        @pl.when(qi == 0)
        def _():
            kpad[0:B_LEFT, :] = jnp.zeros((B_LEFT, 128), BF16)
            vpad[0:B_LEFT, :] = jnp.zeros((B_LEFT, 128), BF16)
            kpad[B_LEFT:, :] = k_ref[...]
            vpad[B_LEFT:, :] = v_ref[...]
            dkpad[...] = jnp.zeros_like(dkpad)
            dvpad[...] = jnp.zeros_like(dvpad)

        @pl.when((qi == 0) & (b == 0))
        def _():
            g_ref[...] = jnp.zeros_like(g_ref)

        win = pl.ds(q0, BWIN)
        kw = kpad[win, :]
        vw = vpad[win, :]
        inside = lax.broadcasted_iota(jnp.int32, (BQ, BWIN), 1) >= B_LEFT - q0
        qall = q_ref[...]
        dov = do_ref[...]
        dd = dov * o_ref[...]
        stv = st_ref[...]
        dq = jnp.zeros((BQ, 128), F32)
        for j in range(2):
            hm = half == bool(j)
            qh = jnp.where(hm, qall, jnp.zeros_like(qall))
            delta = jnp.sum(jnp.where(hm, dd, 0.0), axis=-1, keepdims=True)
            s = jnp.where(inside, _nt(qh, kw) * scale + tb_ref[j], NEG)
            p = jnp.exp(s - stv[:, j:j + 1])
            doh = jnp.where(hm, dov, 0.0).astype(BF16)
            ds = p * (_nt(doh, vw) - delta)
            g_ref[j] += ds
            dsb = (ds * scale).astype(BF16)
            dvpad[win, :] += _tn(p.astype(BF16), doh)
            dkpad[win, :] += _tn(dsb, qh)
            dq = dq + jnp.where(hm, jnp.dot(dsb, kw, preferred_element_type=F32), 0.0)
        dq_ref[...] = dq

        @pl.when(qi == nq - 1)
        def _():
            dk_ref[...] = dkpad[B_LEFT:, :]
            dv_ref[...] = dvpad[B_LEFT:, :]

    qblk = pl.BlockSpec((BQ, 128), lambda p, b, i: (b * nq + i, p))
    full = pl.BlockSpec((seq, 128), lambda p, b, i: (b, p))
    tblk = pl.BlockSpec((2, BQ, BWIN), lambda p, b, i: (p, 0, 0))
    return pl.pallas_call(
        body, name="band_bwd", grid=(3, nb, nq),
        in_specs=[qblk, full, full, tblk, qblk, qblk, qblk],
        out_specs=[qblk, full, full, tblk],
        out_shape=[jax.ShapeDtypeStruct((t, GW), F32), jax.ShapeDtypeStruct((t, GW), F32),
                   jax.ShapeDtypeStruct((t, GW), F32), jax.ShapeDtypeStruct((6, BQ, BWIN), F32)],
        scratch_shapes=[pltpu.VMEM((seq + B_LEFT, 128), BF16), pltpu.VMEM((seq + B_LEFT, 128), BF16),
                        pltpu.VMEM((seq + B_LEFT, 128), F32), pltpu.VMEM((seq + B_LEFT, 128), F32)],
        compiler_params=_cparams(("arbitrary", "arbitrary", "arbitrary")),
    )(q, k, v, table, o, st, do)


def _fox_prep(cf, fb, seq):
    nb = cf.shape[0] // seq
    nblk = seq // 128

    def body(cf_ref, fb_ref, f_ref):
        x = cf_ref[...] + fb_ref[...]
        lf = jnp.minimum(x, 0.0) - jnp.log1p(jnp.exp(-jnp.abs(x)))
        rows = lf.T[0:8, :]
        upper = (lax.broadcasted_iota(jnp.int32, (128, 128), 0)
                 <= lax.broadcasted_iota(jnp.int32, (128, 128), 1)).astype(F32)
        carry = jnp.zeros((8, 1), F32)
        for blk in range(nblk):
            sl = slice(blk * 128, (blk + 1) * 128)
            cs = jnp.dot(rows[:, sl], upper, precision=HI, preferred_element_type=F32) + carry
            carry = cs[:, 127:128]
            f_ref[0, 0, :, sl] = cs
            f_ref[0, 1, :, sl] = pltpu.roll(cs, 6, 0)
            f_ref[0, 2, :, sl] = pltpu.roll(cs, 4, 0)

    return pl.pallas_call(
        body, name="fox_prep", grid=(nb,),
        in_specs=[pl.BlockSpec((seq, 128), lambda b: (b, 0)), pl.BlockSpec((1, 128), lambda b: (0, 0))],
        out_specs=pl.BlockSpec((1, 3, 8, seq), lambda b: (b, 0, 0, 0)),
        out_shape=jax.ShapeDtypeStruct((nb, 3, 8, seq), F32),
        compiler_params=_cparams(("arbitrary",)),
    )(cf, fb)


def _fox_prep_bwd(df, dfq, cf, fb, seq):
    nb = cf.shape[0] // seq
    nblk = seq // 128

    def body(df_ref, dfq_ref, cf_ref, fb_ref, dcf_ref, dfb_ref, wide):
        b = pl.program_id(0)
        row = lax.broadcasted_iota(jnp.int32, (8, seq), 0)
        dfh = None
        for p in range(3):
            both = df_ref[0, p] + dfq_ref[:, p * 128:(p + 1) * 128].T[0:8, :]
            both = jnp.where(row < 2, both, 0.0)
            if p:
                both = pltpu.roll(both, 2 * p, 0)
            dfh = both if dfh is None else dfh + both
        lower = (lax.broadcasted_iota(jnp.int32, (128, 128), 0)
                 >= lax.broadcasted_iota(jnp.int32, (128, 128), 1)).astype(F32)
        wide[...] = jnp.zeros_like(wide)
        carry = jnp.zeros((8, 1), F32)
        for blk in reversed(range(nblk)):
            sl = slice(blk * 128, (blk + 1) * 128)
            rc = jnp.dot(dfh[:, sl], lower, precision=HI, preferred_element_type=F32) + carry
            carry = rc[:, 0:1]
            wide[0:8, sl] = rc
        dl = wide[...].T
        x = cf_ref[...] + fb_ref[...]
        dcf = dl * (1.0 / (1.0 + jnp.exp(x)))
        dcf_ref[...] = dcf
        part = jnp.sum(dcf, axis=0, keepdims=True)

        @pl.when(b == 0)
        def _():
            dfb_ref[...] = part

        @pl.when(b != 0)
        def _():
            dfb_ref[...] += part

    return pl.pallas_call(
        body, name="fox_prep_bwd", grid=(nb,),
        in_specs=[pl.BlockSpec((1, 3, 8, seq), lambda b: (b, 0, 0, 0)), pl.BlockSpec((seq, GW), lambda b: (b, 0)),
                  pl.BlockSpec((seq, 128), lambda b: (b, 0)), pl.BlockSpec((1, 128), lambda b: (0, 0))],
        out_specs=[pl.BlockSpec((seq, 128), lambda b: (b, 0)), pl.BlockSpec((1, 128), lambda b: (0, 0))],
        out_shape=[jax.ShapeDtypeStruct(cf.shape, F32), jax.ShapeDtypeStruct((1, 128), F32)],
        scratch_shapes=[pltpu.VMEM((128, seq), F32)],
        compiler_params=_cparams(("arbitrary",)),
    )(df, dfq, cf, fb)


def _gate_out(oa, ob, oc, gates, w, x, gate, seq, tm=256):
    t = x.shape[0]
    tps = seq // tm

    def body(oa_ref, ob_ref, oc_ref, g_ref, w_ref, x_ref, gt_ref, xo_ref, y_ref, u_ref):
        for n, o_ref in enumerate((oa_ref, ob_ref, oc_ref)):
            sl = slice(n * GW, (n + 1) * GW)
            gv = g_ref[:, sl]
            u_ref[:, sl] = (o_ref[...] * (gv * _sigmoid(gv))).astype(BF16)
        y = jnp.dot(u_ref[...], w_ref[...], preferred_element_type=F32)
        y_ref[...] = y
        xo_ref[...] = x_ref[...] + gt_ref[0] * y

    row = lambda wd: pl.BlockSpec((tm, wd), lambda i: (i, 0))
    return pl.pallas_call(
        body, name="gate_out", grid=(t // tm,),
        in_specs=[row(GW), row(GW), row(GW), row(U_PAD), pl.BlockSpec((U_PAD, D_MODEL), lambda i: (0, 0)),
                  row(D_MODEL), pl.BlockSpec((1, 1, D_MODEL), lambda i: (i // tps, 0, 0))],
        out_specs=[row(D_MODEL), row(D_MODEL), row(U_PAD)],
        out_shape=[jax.ShapeDtypeStruct((t, D_MODEL), F32), jax.ShapeDtypeStruct((t, D_MODEL), F32),
                   jax.ShapeDtypeStruct((t, U_PAD), BF16)],
        compiler_params=_cparams(("arbitrary",)),
    )(oa, ob, oc, gates, w, x, gate)


def _gate_out_bwd(dxo, y, gate, oa, ob, oc, gates, w_t, seq, tm=256):
    t = dxo.shape[0]
    tps = seq // tm
    nb = t // seq

    def body(dxo_ref, y_ref, gt_ref, oa_ref, ob_ref, oc_ref, g_ref, wt_ref,
             dy_ref, doa_ref, dob_ref, doc_ref, dg_ref, dgt_ref):
        i = pl.program_id(0)
        dxo_v = dxo_ref[...]
        dgt = jnp.sum(dxo_v * y_ref[...], axis=0, keepdims=True)
        dyb = (dxo_v * gt_ref[0]).astype(BF16)
        dy_ref[...] = dyb
        du = jnp.dot(dyb, wt_ref[...], preferred_element_type=F32)
        for n, (o_ref, do_ref) in enumerate(((oa_ref, doa_ref), (ob_ref, dob_ref), (oc_ref, doc_ref))):
            sl = slice(n * GW, (n + 1) * GW)
            gv = g_ref[:, sl]
            sg = _sigmoid(gv)
            dun = du[:, sl]
            do_ref[...] = dun * (gv * sg)
            dg_ref[:, sl] = dun * o_ref[...] * (sg * (1.0 + gv * (1.0 - sg)))

        @pl.when(i % tps == 0)
        def _():
            dgt_ref[0] = dgt

        @pl.when(i % tps != 0)
        def _():
            dgt_ref[0] += dgt

    row = lambda wd: pl.BlockSpec((tm, wd), lambda i: (i, 0))
    per_b = pl.BlockSpec((1, 1, D_MODEL), lambda i: (i // tps, 0, 0))
    return pl.pallas_call(
        body, name="gate_out_bwd", grid=(t // tm,),
        in_specs=[row(D_MODEL), row(D_MODEL), per_b, row(GW), row(GW), row(GW), row(U_PAD),
                  pl.BlockSpec((D_MODEL, U_PAD), lambda i: (0, 0))],
        out_specs=[row(D_MODEL), row(GW), row(GW), row(GW), row(U_PAD), per_b],
        out_shape=[jax.ShapeDtypeStruct((t, D_MODEL), BF16), jax.ShapeDtypeStruct((t, GW), F32),
                   jax.ShapeDtypeStruct((t, GW), F32), jax.ShapeDtypeStruct((t, GW), F32),
                   jax.ShapeDtypeStruct((t, U_PAD), F32), jax.ShapeDtypeStruct((nb, 1, D_MODEL), F32)],
        compiler_params=_cparams(("arbitrary",)),
    )(dxo, y, gate, oa, ob, oc, gates, w_t)


def _final_loss(x, target, g, tm=256):
    t = x.shape[0]

    def body(x_ref, t_ref, g_ref, dx_ref, loss_ref, dg_ref):
        i = pl.program_id(0)
        xv = x_ref[...]
        rstd = lax.rsqrt(jnp.mean(xv * xv, axis=-1, keepdims=True) + EPS)
        xn = xv * rstd
        gv = g_ref[...]
        err = xn * gv - t_ref[...]
        dy = err * (1.0 / D_MODEL)
        dxn = dy * gv
        dx_ref[...] = rstd * (dxn - xn * jnp.mean(dxn * xn, axis=-1, keepdims=True))
        lp = jnp.sum(err * err, axis=0, keepdims=True) * (0.5 / D_MODEL)
        dgp = jnp.sum(dy * xn, axis=0, keepdims=True)

        @pl.when(i == 0)
        def _():
            loss_ref[...] = lp
            dg_ref[...] = dgp

        @pl.when(i != 0)
        def _():
            loss_ref[...] += lp
            dg_ref[...] += dgp

    row = pl.BlockSpec((tm, D_MODEL), lambda i: (i, 0))
    vec = pl.BlockSpec((1, D_MODEL), lambda i: (0, 0))
    return pl.pallas_call(
        body, name="final_loss", grid=(t // tm,),
        in_specs=[row, row, vec], out_specs=[row, vec, vec],
        out_shape=[jax.ShapeDtypeStruct((t, D_MODEL), F32), jax.ShapeDtypeStruct((1, D_MODEL), F32),
                   jax.ShapeDtypeStruct((1, D_MODEL), F32)],
        compiler_params=_cparams(("arbitrary",)),
    )(x, target, g)


def _adamw(w, gslots, m, v, name, tr=None):
    r, c = w.shape
    ns = gslots.shape[0]
    tr = r if tr is None else tr

    def body(w_ref, g_ref, m_ref, v_ref, go_ref, d_ref, mo_ref, vo_ref):
        g = g_ref[0].astype(F32)
        for j in range(1, ns):
            g = g + g_ref[j].astype(F32)
        mn = ADAM_B1 * m_ref[...] + (1.0 - ADAM_B1) * g
        vn = ADAM_B2 * v_ref[...] + (1.0 - ADAM_B2) * jnp.square(g)
        m_hat = mn / (1.0 - ADAM_B1 ** ADAM_STEP)
        v_hat = vn / (1.0 - ADAM_B2 ** ADAM_STEP)
        go_ref[...] = g
        d_ref[...] = -ADAM_LR * (m_hat / (jnp.sqrt(v_hat) + ADAM_EPS) + ADAM_WD * w_ref[...])
        mo_ref[...] = mn
        vo_ref[...] = vn

    blk = pl.BlockSpec((tr, c), lambda i: (i, 0))
    return pl.pallas_call(
        body, name=name, grid=(r // tr,),
        in_specs=[blk, pl.BlockSpec((ns, tr, c), lambda i: (0, i, 0)), blk, blk],
        out_specs=[blk] * 4, out_shape=[jax.ShapeDtypeStruct((r, c), F32)] * 4,
        compiler_params=_cparams(("arbitrary",)),
    )(w, gslots, m, v)


def _rope_tables(positions):
    inv = ROPE_THETA ** (-jnp.arange(0, A_ROPE, 2, dtype=F32) / A_ROPE)
    ang = positions.astype(F32)[:, None] * inv
    cos, sin = jnp.cos(ang), jnp.sin(ang)
    t = positions.shape[0]
    one = jnp.ones((t, 64), F32)
    zero16 = jnp.zeros((t, 16), F32)
    cos_t = jnp.concatenate([one, cos, cos, jnp.ones((t, 32), F32)], axis=1)
    sin_a = jnp.concatenate([jnp.zeros((t, 64), F32), -sin, zero16, jnp.zeros((t, 32), F32)], axis=1)
    sin_b = jnp.concatenate([jnp.zeros((t, 64), F32), zero16, sin, jnp.zeros((t, 32), F32)], axis=1)
    return cos_t, sin_a, sin_b


def _pad_heads(w, real, padded, nheads, axis):
    shp = w.shape[:axis] + (nheads, real) + w.shape[axis + 1:]
    w = w.reshape(shp)
    pad = [(0, 0)] * w.ndim
    pad[axis + 1] = (0, padded - real)
    w = jnp.pad(w, pad)
    return w.reshape(w.shape[:axis] + (nheads * padded,) + w.shape[axis + 2:])


def kernel(x, c, positions, w_ada, b_ada, norm_g, w_in, a_q_norm_g, a_w_uq, a_kv_norm_g, a_w_ukv, b_rel_bias, c_forget_b, w_out, final_g, loss_target, m_w_ada, m_b_ada, m_norm_g, m_w_in, m_a_q_norm_g, m_a_w_uq, m_a_kv_norm_g, m_a_w_ukv, m_b_rel_bias, m_c_forget_b, m_w_out, m_final_g, v_w_ada, v_b_ada, v_norm_g, v_w_in, v_a_q_norm_g, v_a_w_uq, v_a_kv_norm_g, v_a_w_ukv, v_b_rel_bias, v_c_forget_b, v_w_out, v_final_g):
    nb, seq, _ = x.shape
    t = nb * seq
    me = 4 * lax.axis_index("x") + 2 * lax.axis_index("y") + lax.axis_index("c")
    x2 = x.reshape(t, D_MODEL)
    tgt = loss_target.reshape(t, D_MODEL)
    cos_t, sin_a, sin_b = _rope_tables(positions.reshape(t))

    shards = []
    for l in range(DEPTH):
        shards += [w_in[l].astype(BF16), w_out[l].astype(BF16), a_w_uq[l].astype(BF16), a_w_ukv[l].astype(BF16)]
    gathered = _gather(shards + [c], "gather_weights")
    c_all = gathered[-1].reshape(N_DEV * nb, D_MODEL)
    w_in_p, w_in_t, w_out_p, w_out_t, wuq_p, wuq_t, wk_p, wk_t, wv_p, wv_t = ([] for _ in range(10))
    for l in range(DEPTH):
        gi, go, gq, gkv = gathered[4 * l:4 * l + 4]
        wi = _pad_runs(gi.reshape(D_MODEL, N_IN), IN_RUNS, N_PAD, 1)
        wo = _pad_runs(go.reshape(D_MODEL, D_MODEL), OUT_RUNS, U_PAD, 0)
        wq = jnp.transpose(gq, (1, 0, 2)).reshape(A_Q_RANK, A_HEADS * (A_NOPE + A_ROPE))
        wq = _pad_heads(wq, A_NOPE + A_ROPE, HEAD_PAD, A_HEADS, 1)
        wkv = jnp.transpose(gkv, (1, 0, 2)).reshape(A_KV_RANK, A_HEADS, 2 * A_NOPE)
        wk = jnp.pad(wkv[:, :, :A_NOPE], ((0, 0), (0, 0), (0, HEAD_PAD - A_NOPE))).reshape(A_KV_RANK, A_HEADS * HEAD_PAD)
        wv = wkv[:, :, A_NOPE:].reshape(A_KV_RANK, GW)
        w_in_p.append(wi); w_in_t.append(wi.T); w_out_p.append(wo); w_out_t.append(wo.T)
        wuq_p.append(wq); wuq_t.append(wq.T); wk_p.append(wk); wk_t.append(wk.T); wv_p.append(wv); wv_t.append(wv.T)

    c_act, mod_cols = _ada_fwd(c_all, w_ada)
    (mod_g,) = _gather([mod_cols], "gather_mod")
    mod_all = jnp.transpose(mod_g, (1, 2, 0, 3)).reshape(DEPTH, N_DEV * nb, 3 * D_MODEL)
    mod = lax.dynamic_slice_in_dim(mod_all, me * nb, nb, axis=1) + b_ada[:, None, :]

    fb_pad = jnp.pad(c_forget_b, ((0, 0), (0, 128 - C_HEADS)))
    a_scale = (A_NOPE + A_ROPE) ** -0.5
    h_scale = CHUNK ** -0.5

    saved = []
    xl = x2
    for l in range(DEPTH):
        shift, scale, gate = mod[l, :, :D_MODEL], mod[l, :, D_MODEL:2 * D_MODEL], mod[l, :, 2 * D_MODEL:]
        ss = jnp.stack([shift, 1.0 + scale], axis=1)
        gate3 = gate[:, None, :]
        h, cq, ckv, kpe, gates, bq, bk, bv, cq2, ck, cv, cf = _ln_in(xl, ss, norm_g[l:l + 1], w_in_p[l], seq)
        q, k, v, cqn, ckvn = _mla_prep(cq, ckv, kpe, a_q_norm_g[l:l + 1], a_kv_norm_g[l:l + 1],
                                       wuq_p[l], wk_p[l], wv_p[l], cos_t, sin_a, sin_b)
        oa, sta = _attn_fwd("mla", q, k, v, None, seq, a_scale)
        table = _band_table(jnp.pad(b_rel_bias[l], ((0, 8 - B_HEADS), (0, GW - N_REL))))
        ob, stb = _band_fwd(bq, bk, bv, table, seq, h_scale)
        fcum = _fox_prep(cf, fb_pad[l:l + 1], seq)
        oc, stc = _attn_fwd("fox", cq2, ck, cv, fcum, seq, h_scale)
        xn, y, u = _gate_out(oa, ob, oc, gates, w_out_p[l], xl, gate3, seq)
        saved.append(dict(x=xl, ss=ss, gate3=gate3, h=h, cq=cq, ckv=ckv, gates=gates, bq=bq, bk=bk, bv=bv,
                          cq2=cq2, ck=ck, cv=cv, cf=cf, q=q, k=k, v=v, cqn=cqn, ckvn=ckvn, oa=oa, sta=sta,
                          table=table, ob=ob, stb=stb, fcum=fcum, oc=oc, stc=stc, y=y, u=u))
        xl = xn

    dx, loss_lanes, g_final = _final_loss(xl, tgt, final_g[None, :])
    loss = lax.psum(jnp.sum(loss_lanes), AXES)

    g_in, g_out, g_uq, g_ukv, dmods, smalls = [None] * DEPTH, [None] * DEPTH, [None] * DEPTH, [None] * DEPTH, [None] * DEPTH, [None] * DEPTH
    for l in reversed(range(DEPTH)):
        s = saved[l]
        dy, doa, dob, doc, dgates, dgate = _gate_out_bwd(dx, s["y"], s["gate3"], s["oa"], s["ob"], s["oc"],
                                                         s["gates"], w_out_t[l], seq)
        g_out[l] = _unpad_runs(_matmul_tn(s["u"], dy, "dw_out"), OUT_RUNS, 0)
        dq, dk, dv = _attn_bwd("mla", s["q"], s["k"], s["v"], None, s["oa"], s["sta"], doa, seq, a_scale)
        dbq, dbk, dbv, gtab = _band_bwd(s["bq"], s["bk"], s["bv"], s["table"], s["ob"], s["stb"], dob, seq, h_scale)
        g_rel = _band_table_bwd(gtab)[:, 0, :N_REL]
        dcq2, dck, dcv, dfc, dfq = _attn_bwd("fox", s["cq2"], s["ck"], s["cv"], s["fcum"], s["oc"], s["stc"], doc,
                                             seq, h_scale)
        dcf, dfb = _fox_prep_bwd(dfc, dfq, s["cf"], fb_pad[l:l + 1], seq)
        dcq, dckv, dkpe, dqlin, dklin, dgq, dgkv = _mla_prep_bwd(
            dq, dk, dv, s["cq"], s["ckv"], a_q_norm_g[l:l + 1], a_kv_norm_g[l:l + 1],
            wuq_t[l], wk_t[l], wv_t[l], cos_t, sin_a, sin_b)
        gq_pad = _matmul_tn(s["cqn"], dqlin, "dw_uq")
        g_uq[l] = gq_pad.reshape(A_Q_RANK, A_HEADS, HEAD_PAD)[:, :, :A_NOPE + A_ROPE].reshape(A_Q_RANK, -1)
        gk_pad = _matmul_tn(s["ckvn"], dklin, "dw_uk").reshape(A_KV_RANK, A_HEADS, HEAD_PAD)[:, :, :A_NOPE]
        gv_pad = _matmul_tn(s["ckvn"], dv, "dw_uv").reshape(A_KV_RANK, A_HEADS, A_NOPE)
        g_ukv[l] = jnp.concatenate([gk_pad, gv_pad], axis=2).reshape(A_KV_RANK, -1)
        dz = [dcq, dckv, dkpe, dgates, dbq, dbk, dbv, dcq2, dck, dcv, dcf]
        g_in[l] = _unpad_runs(jnp.concatenate([_matmul_tn(s["h"], d, "dw_in_" + nm)
                                               for d, (nm, _, _, _) in zip(dz, Z_SEGS)], axis=1), IN_RUNS, 1)
        dx, dss, dg_norm = _ln_in_bwd(dz, w_in_t[l], s["x"], s["ss"], norm_g[l:l + 1], dx, seq)
        dmods[l] = jnp.concatenate([dss[:, 0, :], dss[:, 1, :], dgate[:, 0, :]], axis=1)
        smalls[l] = [dg_norm.reshape(-1), dgq.reshape(-1), dgkv.reshape(-1), g_rel.reshape(-1),
                     dfb[0, :C_HEADS]]
    grad_x = dx.reshape(nb, seq, D_MODEL)

    small = jnp.concatenate([p for l in range(DEPTH) for p in smalls[l]] + [g_final.reshape(-1)])
    n_small = small.shape[0]
    small_rows = -(-n_small // 1024) * 8
    small = jnp.pad(small, (0, small_rows * 128 - n_small)).reshape(small_rows, 128)
    dmod_local = jnp.stack(dmods)
    dmod_g, small_g = _gather([dmod_local, small], "gather_small")
    dmod_all = jnp.transpose(dmod_g, (1, 0, 2, 3)).reshape(DEPTH, N_DEV * nb, 3 * D_MODEL)
    cols = 3 * D_MODEL // N_DEV
    dmod_mine = lax.dynamic_slice_in_dim(dmod_all, me * cols, cols, axis=2)
    g_w_ada, g_b_ada = _ada_bwd(c_act, dmod_all, dmod_mine)
    small_sum = _sum_slots(small_g, "sum_small").reshape(-1)

    rows = D_MODEL // N_DEV
    a2a_in = [jnp.stack(g_in).reshape(DEPTH, N_DEV, rows, N_IN).transpose(1, 0, 2, 3),
              jnp.stack(g_out).reshape(DEPTH, N_DEV, rows, D_MODEL).transpose(1, 0, 2, 3),
              jnp.stack(g_uq).reshape(DEPTH, A_Q_RANK, N_DEV, -1).transpose(2, 0, 1, 3),
              jnp.stack(g_ukv).reshape(DEPTH, A_KV_RANK, N_DEV, -1).transpose(2, 0, 1, 3)]
    from_sib = _pair_swap(a2a_in, "grads_pair")
    core = lax.axis_index("c")
    chip_sums = []
    for a, r, nm, tr in zip(a2a_in, from_sib, ("in", "out", "uq", "ukv"), (32, 64, None, None)):
        own = lax.dynamic_index_in_dim(a.reshape((N_CHIP, 2) + a.shape[1:]), core, axis=1, keepdims=False)
        own, r = own.reshape(N_CHIP, -1, a.shape[-1]), r.reshape(N_CHIP, -1, a.shape[-1])
        chip_sums.append(_pair_add(own, r, "grads_add_" + nm, own.shape[1] if tr is None else tr))
    p_in, p_out, p_uq, p_ukv = _chip_a2a(chip_sums, "grads_chips")

    def split_small():
        out, pos = [], 0
        sizes = [D_MODEL, A_Q_RANK, A_KV_RANK, B_HEADS * N_REL, C_HEADS]
        per_layer = []
        for l in range(DEPTH):
            parts = []
            for sz in sizes:
                parts.append(small_sum[pos:pos + sz])
                pos += sz
            per_layer.append(parts)
        for j in range(len(sizes)):
            out.append(jnp.stack([per_layer[l][j] for l in range(DEPTH)]))
        out.append(small_sum[pos:pos + D_MODEL])
        return out

    g_norm, g_qn, g_kvn, g_relb, g_fb, g_fin = split_small()

    def adam(w, g, m, v, name, tr=None):
        shp = w.shape
        w2 = w.reshape(-1, shp[-1]) if w.ndim > 1 else w.reshape(1, -1)
        gs = g.reshape((-1,) + w2.shape) if g.size != w.size else g.reshape((1,) + w2.shape)
        outs = _adamw(w2, gs, m.reshape(w2.shape), v.reshape(w2.shape), name, tr)
        return [o.reshape(shp) for o in outs]

    res = {
        "w_ada": adam(w_ada, g_w_ada, m_w_ada, v_w_ada, "adam_w_ada", 256),
        "b_ada": adam(b_ada, g_b_ada, m_b_ada, v_b_ada, "adam_b_ada"),
        "norm_g": adam(norm_g, g_norm, m_norm_g, v_norm_g, "adam_norm_g"),
        "w_in": adam(w_in, p_in, m_w_in, v_w_in, "adam_w_in", 32),
        "a_q_norm_g": adam(a_q_norm_g, g_qn, m_a_q_norm_g, v_a_q_norm_g, "adam_q_norm"),
        "a_w_uq": adam(a_w_uq, p_uq, m_a_w_uq, v_a_w_uq, "adam_w_uq"),
        "a_kv_norm_g": adam(a_kv_norm_g, g_kvn, m_a_kv_norm_g, v_a_kv_norm_g, "adam_kv_norm"),
        "a_w_ukv": adam(a_w_ukv, p_ukv, m_a_w_ukv, v_a_w_ukv, "adam_w_ukv"),
        "b_rel_bias": adam(b_rel_bias, g_relb.reshape(b_rel_bias.shape), m_b_rel_bias, v_b_rel_bias, "adam_rel_bias"),
        "c_forget_b": adam(c_forget_b, g_fb, m_c_forget_b, v_c_forget_b, "adam_forget_b"),
        "w_out": adam(w_out, p_out, m_w_out, v_w_out, "adam_w_out", 64),
        "final_g": adam(final_g, g_fin, m_final_g, v_final_g, "adam_final_g"),
    }
    names = ["w_ada", "b_ada", "norm_g", "w_in", "a_q_norm_g", "a_w_uq", "a_kv_norm_g", "a_w_ukv", "b_rel_bias",
             "c_forget_b", "w_out", "final_g"]
    outs = [loss, grad_x]
    for j in range(4):
        outs += [res[n][j] for n in names]
    return tuple(outs)
```

```python
import functools

import jax
import jax.numpy as jnp
from jax import lax
from jax.experimental import pallas as pl
from jax.experimental.pallas import tpu as pltpu

F32 = jnp.float32
BF16 = jnp.bfloat16
HI = lax.Precision.HIGHEST

N_DEV = 8
AXES = ("x", "y", "c")
D_MODEL = 1024
DEPTH = 2
CHUNK = 64
EPS = 1e-6
NEG = -1e30
A_HEADS = 6
A_NOPE = 64
A_ROPE = 32
A_Q_RANK = 384
A_KV_RANK = 256
ROPE_THETA = 10000.0
B_HEADS = 5
B_LEFT = 512
REL_CLIP = 128
N_REL = 2 * REL_CLIP + 1
C_HEADS = 5
HEAD_PAD = 128
GW = 384
N_IN = 3621
ADAM_LR = 0.001
ADAM_B1 = 0.9
ADAM_B2 = 0.999
ADAM_EPS = 1e-08
ADAM_WD = 0.01
ADAM_STEP = 10
VMEM_LIMIT = 56 * 1024 * 1024

Z_SEGS = (
    ("cq", 0, 384, F32), ("ckv", 384, 256, F32), ("kpe", 640, 128, F32), ("gates", 768, 1152, F32),
    ("bq", 1920, 384, BF16), ("bk", 2304, 384, BF16), ("bv", 2688, 384, BF16),
    ("cq2", 3072, 384, BF16), ("ck", 3456, 384, BF16), ("cv", 3840, 384, BF16), ("cf", 4224, 128, F32),
)
N_PAD = 4352
IN_RUNS = (
    (0, 384, 0), (384, 256, 384), (640 + 64, 32, 640),
    (768, 384, 672), (768 + 384, 320, 2016), (768 + 768, 320, 3301),
    (1920, 320, 1056), (2304, 320, 1376), (2688, 320, 1696),
    (3072, 320, 2336), (3456, 320, 2656), (3840, 320, 2976), (4224, 5, 3296),
)
OUT_RUNS = ((0, 384, 0), (384, 320, 384), (768, 320, 704))
U_PAD = 1152


def _cparams(sem=None, vmem=VMEM_LIMIT):
    return pltpu.CompilerParams(dimension_semantics=sem, vmem_limit_bytes=vmem)


def _pad_runs(w, runs, total, axis):
    order = sorted(runs)
    parts, pos = [], 0
    for off, wd, src in order:
        if off > pos:
            shp = list(w.shape)
            shp[axis] = off - pos
            parts.append(jnp.zeros(shp, w.dtype))
        parts.append(lax.slice_in_dim(w, src, src + wd, axis=axis))
        pos = off + wd
    if pos < total:
        shp = list(w.shape)
        shp[axis] = total - pos
        parts.append(jnp.zeros(shp, w.dtype))
    return jnp.concatenate(parts, axis=axis)


def _unpad_runs(w, runs, axis):
    order = sorted(runs, key=lambda r: r[2])
    return jnp.concatenate([lax.slice_in_dim(w, off, off + wd, axis=axis) for off, wd, _ in order], axis=axis)


def _sigmoid(x):
    return 1.0 / (1.0 + jnp.exp(-x))


N_CHIP = 4
ANY_SPEC = pl.BlockSpec(memory_space=pl.ANY)
MESH_ID = pl.DeviceIdType.MESH


def _gather(arrs, name):
    n = len(arrs)

    def body(*refs):
        ins, outs = refs[:n], refs[n:2 * n]
        send_sems, recv_sems, local_sems = refs[2 * n:]
        x, y, c = lax.axis_index("x"), lax.axis_index("y"), lax.axis_index("c")
        me, sib = (x, y, c), (x, y, 1 - c)
        chips = [(1 - x, y), (x, 1 - y), (1 - x, 1 - y)]

        def slot(px, py, pc):
            return 4 * px + 2 * py + pc

        def copy(a, k, block, to, src=None):
            dst = outs[a].at[slot(*block)]
            return pltpu.make_async_remote_copy(
                src_ref=dst if src is None else src, dst_ref=dst, send_sem=send_sems.at[a, k],
                recv_sem=recv_sems.at[a, k], device_id=to, device_id_type=MESH_ID)

        local = [pltpu.make_async_copy(ins[a], outs[a].at[slot(*me)], local_sems.at[a]) for a in range(n)]
        first = []
        for a in range(n):
            first.append(copy(a, 0, me, sib, src=ins[a]))
            first += [copy(a, 1 + j, me, (*chip, c), src=ins[a]) for j, chip in enumerate(chips)]
        for cp in local + first:
            cp.start()
        passed = []
        for j, chip in enumerate(chips):
            for a in range(n):
                copy(a, 1 + j, (*chip, c), me).wait_recv()
                fwd = copy(a, 4 + j, (*chip, c), sib)
                fwd.start()
                passed.append(fwd)
        for a in range(n):
            copy(a, 0, sib, me).wait_recv()
            for j, chip in enumerate(chips):
                copy(a, 4 + j, (*chip, 1 - c), me).wait_recv()
        for cp in first + passed:
            cp.wait_send()
        for cp in local:
            cp.wait()

    return pl.pallas_call(
        body, name=name, out_shape=[jax.ShapeDtypeStruct((N_DEV,) + a.shape, a.dtype) for a in arrs],
        in_specs=[ANY_SPEC] * n, out_specs=[ANY_SPEC] * n,
        scratch_shapes=[pltpu.SemaphoreType.DMA((n, N_DEV - 1)), pltpu.SemaphoreType.DMA((n, N_DEV - 1)),
                        pltpu.SemaphoreType.DMA((n,))],
    )(*arrs)


def _pair_swap(arrs, name):
    n = len(arrs)

    def body(*refs):
        ins, outs = refs[:n], refs[n:2 * n]
        send_sems, recv_sems = refs[2 * n:]
        x, y, c = lax.axis_index("x"), lax.axis_index("y"), lax.axis_index("c")
        copies = []
        for a in range(n):
            for q in range(N_CHIP):
                cp = pltpu.make_async_remote_copy(
                    src_ref=ins[a].at[2 * q + 1 - c], dst_ref=outs[a].at[q], send_sem=send_sems.at[a, q],
                    recv_sem=recv_sems.at[a, q], device_id=(x, y, 1 - c), device_id_type=MESH_ID)
                cp.start()
                copies.append(cp)
        for cp in copies:
            cp.wait()

    return pl.pallas_call(
        body, name=name, out_shape=[jax.ShapeDtypeStruct((N_CHIP,) + a.shape[1:], a.dtype) for a in arrs],
        in_specs=[ANY_SPEC] * n, out_specs=[ANY_SPEC] * n,
        scratch_shapes=[pltpu.SemaphoreType.DMA((n, N_CHIP)), pltpu.SemaphoreType.DMA((n, N_CHIP))],
    )(*arrs)


def _chip_a2a(arrs, name):
    n = len(arrs)

    def body(*refs):
        ins, outs = refs[:n], refs[n:2 * n]
        send_sems, recv_sems, local_sems = refs[2 * n:]
        x, y, c = lax.axis_index("x"), lax.axis_index("y"), lax.axis_index("c")
        mine = 2 * x + y
        copies = []
        for a in range(n):
            loc = pltpu.make_async_copy(ins[a].at[mine], outs[a].at[mine], local_sems.at[a])
            loc.start()
            copies.append(loc)
            for k in range(1, N_CHIP):
                px = (1 - x) if (k >> 1) & 1 else x
                py = (1 - y) if k & 1 else y
                cp = pltpu.make_async_remote_copy(
                    src_ref=ins[a].at[2 * px + py], dst_ref=outs[a].at[mine], send_sem=send_sems.at[a, k - 1],
                    recv_sem=recv_sems.at[a, k - 1], device_id=(px, py, c), device_id_type=MESH_ID)
                cp.start()
                copies.append(cp)
        for cp in copies:
            cp.wait()

    return pl.pallas_call(
        body, name=name, out_shape=[jax.ShapeDtypeStruct(a.shape, a.dtype) for a in arrs],
        in_specs=[ANY_SPEC] * n, out_specs=[ANY_SPEC] * n,
        scratch_shapes=[pltpu.SemaphoreType.DMA((n, N_CHIP - 1)), pltpu.SemaphoreType.DMA((n, N_CHIP - 1)),
                        pltpu.SemaphoreType.DMA((n,))],
    )(*arrs)


def _pair_add(core, a8, b4, name, tr):
    _, r, c = b4.shape

    def body(core_ref, a_ref, b_ref, o_ref):
        o_ref[...] = (a_ref[...] + b_ref[...]).astype(BF16)

    blk = pl.BlockSpec((1, tr, c), lambda q, i, core_ref: (q, i, 0))
    grid_spec = pltpu.PrefetchScalarGridSpec(
        num_scalar_prefetch=1, grid=(N_CHIP, r // tr),
        in_specs=[pl.BlockSpec((1, tr, c), lambda q, i, core_ref: (2 * q + core_ref[0], i, 0)), blk], out_specs=blk)
    return pl.pallas_call(
        body, name=name, grid_spec=grid_spec, out_shape=jax.ShapeDtypeStruct(b4.shape, BF16),
        compiler_params=_cparams(("arbitrary", "arbitrary")),
    )(core, a8, b4)


def _sum_slots(x, name):
    _, r, c = x.shape

    def body(x_ref, o_ref):
        acc = x_ref[0]
        for j in range(1, N_DEV):
            acc = acc + x_ref[j]
        o_ref[...] = acc

    return pl.pallas_call(body, name=name, out_shape=jax.ShapeDtypeStruct((r, c), F32))(x)


def _ada_fwd(c_all, w_ada):
    nb = c_all.shape[0]
    cols = w_ada.shape[2]

    def body(c_ref, w_ref, act_ref, mod_ref):
        cv = c_ref[...]
        act = cv * _sigmoid(cv)
        act_ref[...] = act
        for l in range(DEPTH):
            mod_ref[l] = jnp.dot(act, w_ref[l], precision=HI, preferred_element_type=F32)

    return pl.pallas_call(
        body, name="ada_fwd",
        out_shape=[jax.ShapeDtypeStruct((nb, D_MODEL), F32), jax.ShapeDtypeStruct((DEPTH, nb, cols), F32)],
        compiler_params=_cparams(),
    )(c_all, w_ada)


def _ada_bwd(c_act, dmod_all, dmod_mine):
    nb = c_act.shape[0]
    cols = dmod_mine.shape[2]

    def body(act_ref, dall_ref, dmine_ref, gw_ref, gb_ref):
        act = act_ref[...]
        for l in range(DEPTH):
            gw_ref[l] = lax.dot_general(act, dmine_ref[l], (((0,), (0,)), ((), ())),
                                        precision=HI, preferred_element_type=F32)
            gb_ref[l:l + 1, :] = jnp.sum(dall_ref[l], axis=0, keepdims=True)

    return pl.pallas_call(
        body, name="ada_bwd",
        out_shape=[jax.ShapeDtypeStruct((DEPTH, D_MODEL, cols), F32),
                   jax.ShapeDtypeStruct((DEPTH, 3 * D_MODEL), F32)],
        compiler_params=_cparams(),
    )(c_act, dmod_all, dmod_mine)


def _ln_in(x, ss, g, w, seq, tm=256):
    t = x.shape[0]
    tps = seq // tm

    def body(x_ref, ss_ref, g_ref, w_ref, h_ref, *outs):
        xv = x_ref[...]
        xn = xv * lax.rsqrt(jnp.mean(xv * xv, axis=-1, keepdims=True) + EPS)
        h = xn * g_ref[...] * ss_ref[0, 1:2, :] + ss_ref[0, 0:1, :]
        hb = h.astype(BF16)
        h_ref[...] = hb
        for o_ref, (_, off, wd, _) in zip(outs, Z_SEGS):
            o_ref[...] = jnp.dot(hb, w_ref[:, off:off + wd], preferred_element_type=F32).astype(o_ref.dtype)

    row = lambda wd: pl.BlockSpec((tm, wd), lambda i: (i, 0))
    return pl.pallas_call(
        body, name="ln_in", grid=(t // tm,),
        in_specs=[row(D_MODEL), pl.BlockSpec((1, 2, D_MODEL), lambda i: (i // tps, 0, 0)),
                  pl.BlockSpec((1, D_MODEL), lambda i: (0, 0)), pl.BlockSpec((D_MODEL, N_PAD), lambda i: (0, 0))],
        out_specs=[row(D_MODEL)] + [row(wd) for _, _, wd, _ in Z_SEGS],
        out_shape=[jax.ShapeDtypeStruct((t, D_MODEL), BF16)]
        + [jax.ShapeDtypeStruct((t, wd), dt) for _, _, wd, dt in Z_SEGS],
        compiler_params=_cparams(("arbitrary",)),
    )(x, ss, g, w)


def _ln_in_bwd(dz, w_t, x, ss, g, dxo, seq, tm=256):
    t = x.shape[0]
    tps = seq // tm
    nb = t // seq
    nz = len(Z_SEGS)

    def body(*refs):
        dz_refs = refs[:nz]
        wt_ref, x_ref, ss_ref, g_ref, dxo_ref, dx_ref, dss_ref, dg_ref = refs[nz:]
        i = pl.program_id(0)
        dh = None
        for r, (_, off, wd, _) in zip(dz_refs, Z_SEGS):
            part = jnp.dot(r[...].astype(BF16), wt_ref[off:off + wd, :], preferred_element_type=F32)
            dh = part if dh is None else dh + part
        xv = x_ref[...]
        rstd = lax.rsqrt(jnp.mean(xv * xv, axis=-1, keepdims=True) + EPS)
        xn = xv * rstd
        gv = g_ref[...]
        s1 = ss_ref[0, 1:2, :]
        dxg = dh * s1
        dxn = dxg * gv
        dx = rstd * (dxn - xn * jnp.mean(dxn * xn, axis=-1, keepdims=True))
        dx_ref[...] = dxo_ref[...] + dx
        dshift = jnp.sum(dh, axis=0, keepdims=True)
        dscale = jnp.sum(dh * (xn * gv), axis=0, keepdims=True)
        dgp = jnp.sum(dxg * xn, axis=0, keepdims=True)

        @pl.when(i % tps == 0)
        def _():
            dss_ref[0, 0:1, :] = dshift
            dss_ref[0, 1:2, :] = dscale

        @pl.when(i % tps != 0)
        def _():
            dss_ref[0, 0:1, :] += dshift
            dss_ref[0, 1:2, :] += dscale

        @pl.when(i == 0)
        def _():
            dg_ref[...] = dgp

        @pl.when(i != 0)
        def _():
            dg_ref[...] += dgp

    row = lambda wd: pl.BlockSpec((tm, wd), lambda i: (i, 0))
    return pl.pallas_call(
        body, name="ln_in_bwd", grid=(t // tm,),
        in_specs=[row(wd) for _, _, wd, _ in Z_SEGS]
        + [pl.BlockSpec((N_PAD, D_MODEL), lambda i: (0, 0)), row(D_MODEL),
           pl.BlockSpec((1, 2, D_MODEL), lambda i: (i // tps, 0, 0)),
           pl.BlockSpec((1, D_MODEL), lambda i: (0, 0)), row(D_MODEL)],
        out_specs=[row(D_MODEL), pl.BlockSpec((1, 2, D_MODEL), lambda i: (i // tps, 0, 0)),
                   pl.BlockSpec((1, D_MODEL), lambda i: (0, 0))],
        out_shape=[jax.ShapeDtypeStruct((t, D_MODEL), F32), jax.ShapeDtypeStruct((nb, 2, D_MODEL), F32),
                   jax.ShapeDtypeStruct((1, D_MODEL), F32)],
        compiler_params=_cparams(("arbitrary",)),
    )(*dz, w_t, x, ss, g, dxo)


def _matmul_tn(a, bs, name, tm=512):
    bs = list(bs) if isinstance(bs, (list, tuple)) else [bs]
    t, k = a.shape
    widths = [b.shape[1] for b in bs]
    n = sum(widths)
    tm = min(tm, t)

    def body(a_ref, *refs):
        b_refs, o_ref = refs[:-1], refs[-1]
        i = pl.program_id(0)
        av = a_ref[...].astype(BF16)
        off = 0
        for b_ref, wd in zip(b_refs, widths):
            part = lax.dot_general(av, b_ref[...].astype(BF16), (((0,), (0,)), ((), ())), preferred_element_type=F32)
            sl = slice(off, off + wd)

            @pl.when(i == 0)
            def _(part=part, sl=sl):
                o_ref[:, sl] = part

            @pl.when(i != 0)
            def _(part=part, sl=sl):
                o_ref[:, sl] += part

            off += wd

    return pl.pallas_call(
        body, name=name, grid=(t // tm,),
        in_specs=[pl.BlockSpec((tm, k), lambda i: (i, 0))] + [pl.BlockSpec((tm, wd), lambda i: (i, 0)) for wd in widths],
        out_specs=pl.BlockSpec((k, n), lambda i: (0, 0)),
        out_shape=jax.ShapeDtypeStruct((k, n), F32),
        compiler_params=_cparams(("arbitrary",)),
    )(a, *bs)


def _rope(blk, cos_t, sin_a, sin_b):
    return blk * cos_t + pltpu.roll(blk, 112, 1) * sin_a + pltpu.roll(blk, 16, 1) * sin_b


def _unrope(d, cos_t, sin_a, sin_b):
    return d * cos_t + pltpu.roll(d * sin_a, 16, 1) + pltpu.roll(d * sin_b, 112, 1)


def _mla_prep(cq, ckv, kpe, gq, gkv, wuq, wk, wv, cos_t, sin_a, sin_b, tm=256):
    t = cq.shape[0]
    qw = A_HEADS * HEAD_PAD

    def body(cq_ref, ckv_ref, kpe_ref, gq_ref, gkv_ref, wuq_ref, wk_ref, wv_ref, c_ref, sa_ref, sb_ref,
             q_ref, k_ref, v_ref, cqn_ref, ckvn_ref):
        ct, sa, sb = c_ref[...], sa_ref[...], sb_ref[...]
        a = cq_ref[...]
        cqn = (a * lax.rsqrt(jnp.mean(a * a, axis=-1, keepdims=True) + EPS) * gq_ref[...]).astype(BF16)
        cqn_ref[...] = cqn
        b = ckv_ref[...]
        ckvn = (b * lax.rsqrt(jnp.mean(b * b, axis=-1, keepdims=True) + EPS) * gkv_ref[...]).astype(BF16)
        ckvn_ref[...] = ckvn
        qlin = jnp.dot(cqn, wuq_ref[...], preferred_element_type=F32)
        klin = jnp.dot(ckvn, wk_ref[...], preferred_element_type=F32)
        v_ref[...] = jnp.dot(ckvn, wv_ref[...], preferred_element_type=F32).astype(BF16)
        kr = _rope(kpe_ref[...], ct, sa, sb)
        for h in range(A_HEADS):
            sl = slice(h * HEAD_PAD, (h + 1) * HEAD_PAD)
            q_ref[:, sl] = _rope(qlin[:, sl], ct, sa, sb).astype(BF16)
            k_ref[:, sl] = (klin[:, sl] + kr).astype(BF16)

    row = lambda wd: pl.BlockSpec((tm, wd), lambda i: (i, 0))
    full = lambda r, c: pl.BlockSpec((r, c), lambda i: (0, 0))
    return pl.pallas_call(
        body, name="mla_prep", grid=(t // tm,),
        in_specs=[row(A_Q_RANK), row(A_KV_RANK), row(128), full(1, A_Q_RANK), full(1, A_KV_RANK),
                  full(A_Q_RANK, qw), full(A_KV_RANK, qw), full(A_KV_RANK, GW), row(128), row(128), row(128)],
        out_specs=[row(qw), row(qw), row(GW), row(A_Q_RANK), row(A_KV_RANK)],
        out_shape=[jax.ShapeDtypeStruct((t, qw), BF16), jax.ShapeDtypeStruct((t, qw), BF16),
                   jax.ShapeDtypeStruct((t, GW), BF16), jax.ShapeDtypeStruct((t, A_Q_RANK), BF16),
                   jax.ShapeDtypeStruct((t, A_KV_RANK), BF16)],
        compiler_params=_cparams(("arbitrary",)),
    )(cq, ckv, kpe, gq, gkv, wuq, wk, wv, cos_t, sin_a, sin_b)


def _mla_prep_bwd(dq, dk, dv, cq, ckv, gq, gkv, wuq_t, wk_t, wv_t, cos_t, sin_a, sin_b, tm=256):
    t = cq.shape[0]
    qw = A_HEADS * HEAD_PAD

    def body(dq_ref, dk_ref, dv_ref, cq_ref, ckv_ref, gq_ref, gkv_ref, wuqt_ref, wkt_ref, wvt_ref,
             c_ref, sa_ref, sb_ref, dcq_ref, dckv_ref, dkpe_ref, dql_ref, dkl_ref, dgq_ref, dgkv_ref):
        i = pl.program_id(0)
        ct, sa, sb = c_ref[...], sa_ref[...], sb_ref[...]
        lane = lax.broadcasted_iota(jnp.int32, (1, HEAD_PAD), 1)
        nope = lane < A_NOPE
        rope = (lane >= A_NOPE) & (lane < A_NOPE + A_ROPE)
        dksum = None
        for h in range(A_HEADS):
            sl = slice(h * HEAD_PAD, (h + 1) * HEAD_PAD)
            dql_ref[:, sl] = _unrope(dq_ref[:, sl], ct, sa, sb).astype(BF16)
            dkh = dk_ref[:, sl]
            dkl_ref[:, sl] = jnp.where(nope, dkh, 0.0).astype(BF16)
            dksum = dkh if dksum is None else dksum + dkh
        dkpe_ref[...] = jnp.where(rope, _unrope(jnp.where(rope, dksum, 0.0), ct, sa, sb), 0.0)
        dcqn = jnp.dot(dql_ref[...], wuqt_ref[...], preferred_element_type=F32)
        dckvn = (jnp.dot(dkl_ref[...], wkt_ref[...], preferred_element_type=F32)
                 + jnp.dot(dv_ref[...].astype(BF16), wvt_ref[...], preferred_element_type=F32))

        def norm_bwd(xv, gv, dy):
            rstd = lax.rsqrt(jnp.mean(xv * xv, axis=-1, keepdims=True) + EPS)
            xn = xv * rstd
            dxn = dy * gv
            dx = rstd * (dxn - xn * jnp.mean(dxn * xn, axis=-1, keepdims=True))
            return dx, jnp.sum(dy * xn, axis=0, keepdims=True)

        dcq, dgq = norm_bwd(cq_ref[...], gq_ref[...], dcqn)
        dckv, dgkv = norm_bwd(ckv_ref[...], gkv_ref[...], dckvn)
        dcq_ref[...] = dcq
        dckv_ref[...] = dckv

        @pl.when(i == 0)
        def _():
            dgq_ref[...] = dgq
            dgkv_ref[...] = dgkv

        @pl.when(i != 0)
        def _():
            dgq_ref[...] += dgq
            dgkv_ref[...] += dgkv

    row = lambda wd: pl.BlockSpec((tm, wd), lambda i: (i, 0))
    full = lambda r, c: pl.BlockSpec((r, c), lambda i: (0, 0))
    return pl.pallas_call(
        body, name="mla_prep_bwd", grid=(t // tm,),
        in_specs=[row(qw), row(qw), row(GW), row(A_Q_RANK), row(A_KV_RANK), full(1, A_Q_RANK), full(1, A_KV_RANK),
                  full(qw, A_Q_RANK), full(qw, A_KV_RANK), full(GW, A_KV_RANK), row(128), row(128), row(128)],
        out_specs=[row(A_Q_RANK), row(A_KV_RANK), row(128), row(qw), row(qw), full(1, A_Q_RANK), full(1, A_KV_RANK)],
        out_shape=[jax.ShapeDtypeStruct((t, A_Q_RANK), F32), jax.ShapeDtypeStruct((t, A_KV_RANK), F32),
                   jax.ShapeDtypeStruct((t, 128), F32), jax.ShapeDtypeStruct((t, qw), BF16),
                   jax.ShapeDtypeStruct((t, qw), BF16), jax.ShapeDtypeStruct((1, A_Q_RANK), F32),
                   jax.ShapeDtypeStruct((1, A_KV_RANK), F32)],
        compiler_params=_cparams(("arbitrary",)),
    )(dq, dk, dv, cq, ckv, gq, gkv, wuq_t, wk_t, wv_t, cos_t, sin_a, sin_b)


def _nt(a, b):
    return lax.dot_general(a, b, (((1,), (1,)), ((), ())), preferred_element_type=F32)


def _tn(a, b):
    return lax.dot_general(a, b, (((0,), (0,)), ((), ())), preferred_element_type=F32)


def _causal_mask(kind, q0, k0, tq, tk):
    qpos = q0 + lax.broadcasted_iota(jnp.int32, (tq, tk), 0)
    kpos = k0 + lax.broadcasted_iota(jnp.int32, (tq, tk), 1)
    if kind == "mla":
        return lax.shift_right_logical(kpos, 6) <= lax.shift_right_logical(qpos, 6)
    return kpos <= qpos


def _attn_fwd(kind, q, k, v, f, seq, scale, tq=512, tk=512):
    t = v.shape[0]
    nb = t // seq
    nq = seq // tq
    hw = 256 if kind == "mla" else 128
    use_f = f is not None
    tq, tk = min(tq, seq), min(tk, seq)
    nq = seq // tq
    assert tk % tq == 0

    def body(*refs):
        if use_f:
            q_ref, k_ref, v_ref, f_ref, o_ref, st_ref = refs
        else:
            q_ref, k_ref, v_ref, o_ref, st_ref = refs
        qi = pl.program_id(2)
        q0 = qi * tq
        lane = lax.broadcasted_iota(jnp.int32, (1, 128), 1)
        half = lane >= 64
        qall = q_ref[...]
        if kind == "mla":
            qhs = [qall[:, 0:128], qall[:, 128:256]]
        else:
            qhs = [jnp.where(half, jnp.zeros_like(qall), qall), jnp.where(half, qall, jnp.zeros_like(qall))]
        nfull = q0 // tk
        kd = pl.multiple_of(nfull * tk, tk)
        diag = _causal_mask(kind, q0 - kd, 0, tq, tk)

        def block(j, k0, state, masked):
            m, l, acc = state
            kh = k_ref[pl.ds(k0, tk), j * 128:(j + 1) * 128] if kind == "mla" else k_ref[pl.ds(k0, tk), :]
            s = _nt(qhs[j], kh) * scale
            if use_f:
                s = s - f_ref[0, 0, j:j + 1, pl.ds(k0, tk)]
            if masked:
                s = jnp.where(diag, s, NEG)
            mn = jnp.maximum(m, jnp.max(s, axis=-1, keepdims=True))
            alpha = jnp.exp(m - mn)
            p = jnp.exp(s - mn)
            l = alpha * l + jnp.sum(p, axis=-1, keepdims=True)
            acc = alpha * acc + jnp.dot(p.astype(BF16), v_ref[pl.ds(k0, tk), :], preferred_element_type=F32)
            return mn, l, acc

        def kstep(kb, carry):
            k0 = pl.multiple_of(kb * tk, tk)
            return block(0, k0, carry[:3], False) + block(1, k0, carry[3:], False)

        init = (jnp.full((tq, 1), NEG, F32), jnp.zeros((tq, 1), F32), jnp.zeros((tq, 128), F32)) * 2
        carry = lax.fori_loop(0, nfull, kstep, init)
        m0, l0, a0 = block(0, kd, carry[:3], True)
        m1, l1, a1 = block(1, kd, carry[3:], True)
        o_ref[...] = jnp.where(half, a1 / l1, a0 / l0)
        st_ref[...] = jnp.where(lane == 0, m0 + jnp.log(l0), jnp.where(lane == 1, m1 + jnp.log(l1), 0.0))

    in_specs = [pl.BlockSpec((tq, hw), lambda b, p, i: (b * nq + i, p)),
                pl.BlockSpec((seq, hw), lambda b, p, i: (b, p)),
                pl.BlockSpec((seq, 128), lambda b, p, i: (b, p))]
    args = [q, k, v]
    if use_f:
        in_specs.append(pl.BlockSpec((1, 1, 8, seq), lambda b, p, i: (b, p, 0, 0)))
        args.append(f)
    oblk = pl.BlockSpec((tq, 128), lambda b, p, i: (b * nq + i, p))
    return pl.pallas_call(
        body, name="attn_fwd_" + kind, grid=(nb, 3, nq), in_specs=in_specs, out_specs=[oblk, oblk],
        out_shape=[jax.ShapeDtypeStruct((t, GW), F32), jax.ShapeDtypeStruct((t, GW), F32)],
        compiler_params=_cparams(("arbitrary", "arbitrary", "arbitrary")),
    )(*args)


def _attn_bwd(kind, q, k, v, f, o, st, do, seq, scale, tq=512, tk=512):
    t = v.shape[0]
    nb = t // seq
    tq, tk = min(tq, seq), min(tk, seq)
    nq = seq // tq
    nk = seq // tk
    hw = 256 if kind == "mla" else 128
    use_f = f is not None
    assert tq == tk

    def body(*refs):
        if use_f:
            q_ref, k_ref, v_ref, f_ref, o_ref, st_ref, do_ref, dq_ref, dk_ref, dv_ref, df_ref, dfq_ref = refs
        else:
            q_ref, k_ref, v_ref, o_ref, st_ref, do_ref, dq_ref, dk_ref, dv_ref = refs
        kj = pl.program_id(2)
        k0 = kj * tk
        lane = lax.broadcasted_iota(jnp.int32, (1, 128), 1)
        half = lane >= 64

        @pl.when(kj == 0)
        def _():
            dq_ref[...] = jnp.zeros_like(dq_ref)
            if use_f:
                dfq_ref[...] = jnp.zeros_like(dfq_ref)

        dk_ref[...] = jnp.zeros_like(dk_ref)
        dv_ref[...] = jnp.zeros_like(dv_ref)
        if use_f:
            df_ref[...] = jnp.zeros_like(df_ref)
        vv = v_ref[...]
        diag = _causal_mask(kind, 0, 0, tq, tk)

        def qstep(qi, masked):
            q0 = pl.multiple_of(qi * tq, tq)
            rows = pl.ds(q0, tq)
            dov = do_ref[rows, :]
            dd = dov * o_ref[rows, :]
            stv = st_ref[rows, :]
            for j in range(2):
                hm = half == bool(j)
                delta = jnp.sum(jnp.where(hm, dd, 0.0), axis=-1, keepdims=True)
                lse = stv[:, j:j + 1]
                if kind == "mla":
                    cols = slice(j * 128, (j + 1) * 128)
                    qh = q_ref[rows, cols]
                    kh = k_ref[:, cols]
                else:
                    cols = slice(0, 128)
                    qa = q_ref[rows, :]
                    qh = jnp.where(hm, qa, jnp.zeros_like(qa))
                    kh = k_ref[...]
                s = _nt(qh, kh) * scale
                if use_f:
                    s = s - f_ref[0, 0, j:j + 1, :]
                if masked:
                    s = jnp.where(diag, s, NEG)
                p = jnp.exp(s - lse)
                doh = jnp.where(hm, dov, 0.0).astype(BF16)
                ds = p * (_nt(doh, vv) - delta)
                dsb = (ds * scale).astype(BF16)
                dv_ref[...] += _tn(p.astype(BF16), doh)
                dk_ref[:, cols] += _tn(dsb, qh)
                dqc = jnp.dot(dsb, kh, preferred_element_type=F32)
                if kind != "mla":
                    dqc = jnp.where(hm, dqc, 0.0)
                dq_ref[rows, cols] += dqc
                if use_f:
                    df_ref[0, 0, j:j + 1, :] += -jnp.sum(ds, axis=0, keepdims=True)
                    dfq_ref[rows, :] += jnp.where(lane == j, jnp.sum(ds, axis=-1, keepdims=True), 0.0)

        qstep(kj, True)

        def rest(qi, carry):
            qstep(qi, False)
            return carry

        lax.fori_loop(kj + 1, nq, rest, 0)

    full_q = lambda wd: pl.BlockSpec((seq, wd), lambda b, p, i: (b, p))
    kblk = lambda wd: pl.BlockSpec((tk, wd), lambda b, p, i: (b * nk + i, p))
    in_specs = [full_q(hw), kblk(hw), kblk(128)]
    args = [q, k, v]
    if use_f:
        in_specs.append(pl.BlockSpec((1, 1, 8, tk), lambda b, p, i: (b, p, 0, i)))
        args.append(f)
    in_specs += [full_q(128), full_q(128), full_q(128)]
    args += [o, st, do]
    out_specs = [full_q(hw), kblk(hw), kblk(128)]
    out_shape = [jax.ShapeDtypeStruct((t, 3 * hw), F32), jax.ShapeDtypeStruct((t, 3 * hw), F32),
                 jax.ShapeDtypeStruct((t, GW), F32)]
    if use_f:
        out_specs += [pl.BlockSpec((1, 1, 8, tk), lambda b, p, i: (b, p, 0, i)), full_q(128)]
        out_shape += [jax.ShapeDtypeStruct((nb, 3, 8, seq), F32), jax.ShapeDtypeStruct((t, GW), F32)]
    return pl.pallas_call(
        body, name="attn_bwd_" + kind, grid=(nb, 3, nk), in_specs=in_specs, out_specs=out_specs,
        out_shape=out_shape, compiler_params=_cparams(("arbitrary", "arbitrary", "arbitrary")),
    )(*args)


BQ = 256
BWIN = BQ + B_LEFT


def _band_geometry():
    r = lax.broadcasted_iota(jnp.int32, (BQ, BWIN), 0)
    j = lax.broadcasted_iota(jnp.int32, (BQ, BWIN), 1)
    rc = lax.shift_right_logical(r, 6)
    jc = lax.shift_right_logical(j, 6)
    allowed = (jc - 8 <= rc) & (rc <= jc)
    return (r + B_LEFT - j) >= REL_CLIP, allowed, j < r


def _band_onehot(transposed, offset=0):
    shape = (BWIN, GW) if transposed else (GW, BWIN)
    kk = lax.broadcasted_iota(jnp.int32, shape, 1 if transposed else 0)
    x = lax.broadcasted_iota(jnp.int32, shape, 0 if transposed else 1) - offset
    x = jnp.where(x < 0, x + BWIN, x)
    return (kk == jnp.clip(B_LEFT - x, -REL_CLIP, REL_CLIP) + REL_CLIP).astype(F32)


def _band_table(rel_bias8):
    def body(b_ref, o_ref):
        hh = pl.program_id(0)
        u8 = jnp.dot(b_ref[...], _band_onehot(False), precision=HI, preferred_element_type=F32)
        rid = lax.broadcasted_iota(jnp.int32, (8, BWIN), 0)
        row = jnp.sum(jnp.where(rid == hh, u8, 0.0), axis=0, keepdims=True)
        far, allowed, _ = _band_geometry()
        tbl = pltpu.roll(jnp.broadcast_to(row, (BQ, BWIN)), 0, 1, stride=1, stride_axis=0)
        tbl = jnp.where(far, row[:, 0:1], tbl)
        o_ref[0] = jnp.where(allowed, tbl, NEG)

    return pl.pallas_call(
        body, name="band_table", grid=(6,),
        in_specs=[pl.BlockSpec((8, GW), lambda h: (0, 0))],
        out_specs=pl.BlockSpec((1, BQ, BWIN), lambda h: (h, 0, 0)),
        out_shape=jax.ShapeDtypeStruct((6, BQ, BWIN), F32),
        compiler_params=_cparams(("arbitrary",)),
    )(rel_bias8)


def _band_table_bwd(gtab):
    def body(g_ref, o_ref):
        gv = g_ref[0]
        _, _, wrapped = _band_geometry()
        gfar = jnp.sum(jnp.sum(jnp.where(wrapped, gv, 0.0), axis=-1, keepdims=True), axis=0, keepdims=True)
        anti = (lax.broadcasted_iota(jnp.int32, (BQ, BQ), 0) + lax.broadcasted_iota(jnp.int32, (BQ, BQ), 1)
                == BQ - 1).astype(F32)
        grev = jnp.dot(anti, jnp.where(wrapped, 0.0, gv), precision=HI, preferred_element_type=F32)
        near = pltpu.roll(grev, 0, 1, stride=1, stride_axis=0)
        y = jnp.broadcast_to(jnp.sum(near, axis=0, keepdims=True), (8, BWIN))
        gb = jnp.dot(y, _band_onehot(True, BQ - 1), precision=HI, preferred_element_type=F32)
        lane = lax.broadcasted_iota(jnp.int32, (8, GW), 1)
        o_ref[0] = gb + jnp.where(lane == 2 * REL_CLIP, gfar, 0.0)

    return pl.pallas_call(
        body, name="band_table_bwd", grid=(B_HEADS,),
        in_specs=[pl.BlockSpec((1, BQ, BWIN), lambda h: (h, 0, 0))],
        out_specs=pl.BlockSpec((1, 8, GW), lambda h: (h, 0, 0)),
        out_shape=jax.ShapeDtypeStruct((B_HEADS, 8, GW), F32),
        compiler_params=_cparams(("arbitrary",)),
    )(gtab)


def _band_fwd(q, k, v, table, seq, scale):
    t = q.shape[0]
    nb = t // seq
    nq = seq // BQ

    def body(q_ref, k_ref, v_ref, tb_ref, o_ref, st_ref, kpad, vpad):
        qi = pl.program_id(2)
        q0 = pl.multiple_of(qi * BQ, BQ)
        lane = lax.broadcasted_iota(jnp.int32, (1, 128), 1)
        half = lane >= 64

        @pl.when(qi == 0)
        def _():
            kpad[0:B_LEFT, :] = jnp.zeros((B_LEFT, 128), BF16)
            vpad[0:B_LEFT, :] = jnp.zeros((B_LEFT, 128), BF16)
            kpad[B_LEFT:, :] = k_ref[...]
            vpad[B_LEFT:, :] = v_ref[...]

        kw = kpad[pl.ds(q0, BWIN), :]
        vw = vpad[pl.ds(q0, BWIN), :]
        inside = lax.broadcasted_iota(jnp.int32, (BQ, BWIN), 1) >= B_LEFT - q0
        qall = q_ref[...]
        outs, lses = [], []
        for j in range(2):
            qh = jnp.where(half == bool(j), qall, jnp.zeros_like(qall))
            s = jnp.where(inside, _nt(qh, kw) * scale + tb_ref[j], NEG)
            m = jnp.max(s, axis=-1, keepdims=True)
            p = jnp.exp(s - m)
            l = jnp.sum(p, axis=-1, keepdims=True)
            outs.append(jnp.dot(p.astype(BF16), vw, preferred_element_type=F32) / l)
            lses.append(m + jnp.log(l))
        o_ref[...] = jnp.where(half, outs[1], outs[0])
        st_ref[...] = jnp.where(lane == 0, lses[0], jnp.where(lane == 1, lses[1], 0.0))

    qblk = pl.BlockSpec((BQ, 128), lambda b, p, i: (b * nq + i, p))
    full = pl.BlockSpec((seq, 128), lambda b, p, i: (b, p))
    return pl.pallas_call(
        body, name="band_fwd", grid=(nb, 3, nq),
        in_specs=[qblk, full, full, pl.BlockSpec((2, BQ, BWIN), lambda b, p, i: (p, 0, 0))],
        out_specs=[qblk, qblk],
        out_shape=[jax.ShapeDtypeStruct((t, GW), F32), jax.ShapeDtypeStruct((t, GW), F32)],
        scratch_shapes=[pltpu.VMEM((seq + B_LEFT, 128), BF16), pltpu.VMEM((seq + B_LEFT, 128), BF16)],
        compiler_params=_cparams(("arbitrary", "arbitrary", "arbitrary")),
    )(q, k, v, table)


def _band_bwd(q, k, v, table, o, st, do, seq, scale):
    t = q.shape[0]
    nb = t // seq
    nq = seq // BQ

    def body(q_ref, k_ref, v_ref, tb_ref, o_ref, st_ref, do_ref, dq_ref, dk_ref, dv_ref, g_ref,
             kpad, vpad, dkpad, dvpad):
        b = pl.program_id(1)
        qi = pl.program_id(2)
        q0 = pl.multiple_of(qi * BQ, BQ)
        lane = lax.broadcasted_iota(jnp.int32, (1, 128), 1)
        half = lane >= 64

        @pl.when(qi == 0)
        def _():
            kpad[0:B_LEFT, :] = jnp.zeros((B_LEFT, 128), BF16)
            vpad[0:B_LEFT, :] = jnp.zeros((B_LEFT, 128), BF16)
            kpad[B_LEFT:, :] = k_ref[...]
            vpad[B_LEFT:, :] = v_ref[...]
            dkpad[...] = jnp.zeros_like(dkpad)
            dvpad[...] = jnp.zeros_like(dvpad)

        @pl.when((qi == 0) & (b == 0))
        def _():
            g_ref[...] = jnp.zeros_like(g_ref)

        win = pl.ds(q0, BWIN)
        kw = kpad[win, :]
        vw = vpad[win, :]
        inside = lax.broadcasted_iota(jnp.int32, (BQ, BWIN), 1) >= B_LEFT - q0
        qall = q_ref[...]
        dov = do_ref[...]
        dd = dov * o_ref[...]
        stv = st_ref[...]
        dq = jnp.zeros((BQ, 128), F32)
        for j in range(2):
            hm = half == bool(j)
            qh = jnp.where(hm, qall, jnp.zeros_like(qall))
            delta = jnp.sum(jnp.where(hm, dd, 0.0), axis=-1, keepdims=True)
            s = jnp.where(inside, _nt(qh, kw) * scale + tb_ref[j], NEG)
            p = jnp.exp(s - stv[:, j:j + 1])
            doh = jnp.where(hm, dov, 0.0).astype(BF16)
            ds = p * (_nt(doh, vw) - delta)
            g_ref[j] += ds
            dsb = (ds * scale).astype(BF16)
            dvpad[win, :] += _tn(p.astype(BF16), doh)
            dkpad[win, :] += _tn(dsb, qh)
            dq = dq + jnp.where(hm, jnp.dot(dsb, kw, preferred_element_type=F32), 0.0)
        dq_ref[...] = dq

        @pl.when(qi == nq - 1)
        def _():
            dk_ref[...] = dkpad[B_LEFT:, :]
            dv_ref[...] = dvpad[B_LEFT:, :]

    qblk = pl.BlockSpec((BQ, 128), lambda p, b, i: (b * nq + i, p))
    full = pl.BlockSpec((seq, 128), lambda p, b, i: (b, p))
    tblk = pl.BlockSpec((2, BQ, BWIN), lambda p, b, i: (p, 0, 0))
    return pl.pallas_call(
        body, name="band_bwd", grid=(3, nb, nq),
        in_specs=[qblk, full, full, tblk, qblk, qblk, qblk],
        out_specs=[qblk, full, full, tblk],
        out_shape=[jax.ShapeDtypeStruct((t, GW), F32), jax.ShapeDtypeStruct((t, GW), F32),
                   jax.ShapeDtypeStruct((t, GW), F32), jax.ShapeDtypeStruct((6, BQ, BWIN), F32)],
        scratch_shapes=[pltpu.VMEM((seq + B_LEFT, 128), BF16), pltpu.VMEM((seq + B_LEFT, 128), BF16),
                        pltpu.VMEM((seq + B_LEFT, 128), F32), pltpu.VMEM((seq + B_LEFT, 128), F32)],
        compiler_params=_cparams(("arbitrary", "arbitrary", "arbitrary")),
    )(q, k, v, table, o, st, do)


def _fox_prep(cf, fb, seq):
    nb = cf.shape[0] // seq
    nblk = seq // 128

    def body(cf_ref, fb_ref, f_ref):
        x = cf_ref[...] + fb_ref[...]
        lf = jnp.minimum(x, 0.0) - jnp.log1p(jnp.exp(-jnp.abs(x)))
        rows = lf.T[0:8, :]
        upper = (lax.broadcasted_iota(jnp.int32, (128, 128), 0)
                 <= lax.broadcasted_iota(jnp.int32, (128, 128), 1)).astype(F32)
        carry = jnp.zeros((8, 1), F32)
        for blk in range(nblk):
            sl = slice(blk * 128, (blk + 1) * 128)
            cs = jnp.dot(rows[:, sl], upper, precision=HI, preferred_element_type=F32) + carry
            carry = cs[:, 127:128]
            f_ref[0, 0, :, sl] = cs
            f_ref[0, 1, :, sl] = pltpu.roll(cs, 6, 0)
            f_ref[0, 2, :, sl] = pltpu.roll(cs, 4, 0)

    return pl.pallas_call(
        body, name="fox_prep", grid=(nb,),
        in_specs=[pl.BlockSpec((seq, 128), lambda b: (b, 0)), pl.BlockSpec((1, 128), lambda b: (0, 0))],
        out_specs=pl.BlockSpec((1, 3, 8, seq), lambda b: (b, 0, 0, 0)),
        out_shape=jax.ShapeDtypeStruct((nb, 3, 8, seq), F32),
        compiler_params=_cparams(("arbitrary",)),
    )(cf, fb)


def _fox_prep_bwd(df, dfq, cf, fb, seq):
    nb = cf.shape[0] // seq
    nblk = seq // 128

    def body(df_ref, dfq_ref, cf_ref, fb_ref, dcf_ref, dfb_ref, wide):
        b = pl.program_id(0)
        row = lax.broadcasted_iota(jnp.int32, (8, seq), 0)
        dfh = None
        for p in range(3):
            both = df_ref[0, p] + dfq_ref[:, p * 128:(p + 1) * 128].T[0:8, :]
            both = jnp.where(row < 2, both, 0.0)
            if p:
                both = pltpu.roll(both, 2 * p, 0)
            dfh = both if dfh is None else dfh + both
        lower = (lax.broadcasted_iota(jnp.int32, (128, 128), 0)
                 >= lax.broadcasted_iota(jnp.int32, (128, 128), 1)).astype(F32)
        wide[...] = jnp.zeros_like(wide)
        carry = jnp.zeros((8, 1), F32)
        for blk in reversed(range(nblk)):
            sl = slice(blk * 128, (blk + 1) * 128)
            rc = jnp.dot(dfh[:, sl], lower, precision=HI, preferred_element_type=F32) + carry
            carry = rc[:, 0:1]
            wide[0:8, sl] = rc
        dl = wide[...].T
        x = cf_ref[...] + fb_ref[...]
        dcf = dl * (1.0 / (1.0 + jnp.exp(x)))
        dcf_ref[...] = dcf
        part = jnp.sum(dcf, axis=0, keepdims=True)

        @pl.when(b == 0)
        def _():
            dfb_ref[...] = part

        @pl.when(b != 0)
        def _():
            dfb_ref[...] += part

    return pl.pallas_call(
        body, name="fox_prep_bwd", grid=(nb,),
        in_specs=[pl.BlockSpec((1, 3, 8, seq), lambda b: (b, 0, 0, 0)), pl.BlockSpec((seq, GW), lambda b: (b, 0)),
                  pl.BlockSpec((seq, 128), lambda b: (b, 0)), pl.BlockSpec((1, 128), lambda b: (0, 0))],
        out_specs=[pl.BlockSpec((seq, 128), lambda b: (b, 0)), pl.BlockSpec((1, 128), lambda b: (0, 0))],
        out_shape=[jax.ShapeDtypeStruct(cf.shape, F32), jax.ShapeDtypeStruct((1, 128), F32)],
        scratch_shapes=[pltpu.VMEM((128, seq), F32)],
        compiler_params=_cparams(("arbitrary",)),
    )(df, dfq, cf, fb)


def _gate_out(oa, ob, oc, gates, w, x, gate, seq, tm=256):
    t = x.shape[0]
    tps = seq // tm

    def body(oa_ref, ob_ref, oc_ref, g_ref, w_ref, x_ref, gt_ref, xo_ref, y_ref, u_ref):
        for n, o_ref in enumerate((oa_ref, ob_ref, oc_ref)):
            sl = slice(n * GW, (n + 1) * GW)
            gv = g_ref[:, sl]
            u_ref[:, sl] = (o_ref[...] * (gv * _sigmoid(gv))).astype(BF16)
        y = jnp.dot(u_ref[...], w_ref[...], preferred_element_type=F32)
        y_ref[...] = y
        xo_ref[...] = x_ref[...] + gt_ref[0] * y

    row = lambda wd: pl.BlockSpec((tm, wd), lambda i: (i, 0))
    return pl.pallas_call(
        body, name="gate_out", grid=(t // tm,),
        in_specs=[row(GW), row(GW), row(GW), row(U_PAD), pl.BlockSpec((U_PAD, D_MODEL), lambda i: (0, 0)),
                  row(D_MODEL), pl.BlockSpec((1, 1, D_MODEL), lambda i: (i // tps, 0, 0))],
        out_specs=[row(D_MODEL), row(D_MODEL), row(U_PAD)],
        out_shape=[jax.ShapeDtypeStruct((t, D_MODEL), F32), jax.ShapeDtypeStruct((t, D_MODEL), F32),
                   jax.ShapeDtypeStruct((t, U_PAD), BF16)],
        compiler_params=_cparams(("arbitrary",)),
    )(oa, ob, oc, gates, w, x, gate)


def _gate_out_bwd(dxo, y, gate, oa, ob, oc, gates, w_t, seq, tm=256):
    t = dxo.shape[0]
    tps = seq // tm
    nb = t // seq

    def body(dxo_ref, y_ref, gt_ref, oa_ref, ob_ref, oc_ref, g_ref, wt_ref,
             dy_ref, doa_ref, dob_ref, doc_ref, dg_ref, dgt_ref):
        i = pl.program_id(0)
        dxo_v = dxo_ref[...]
        dgt = jnp.sum(dxo_v * y_ref[...], axis=0, keepdims=True)
        dyb = (dxo_v * gt_ref[0]).astype(BF16)
        dy_ref[...] = dyb
        du = jnp.dot(dyb, wt_ref[...], preferred_element_type=F32)
        for n, (o_ref, do_ref) in enumerate(((oa_ref, doa_ref), (ob_ref, dob_ref), (oc_ref, doc_ref))):
            sl = slice(n * GW, (n + 1) * GW)
            gv = g_ref[:, sl]
            sg = _sigmoid(gv)
            dun = du[:, sl]
            do_ref[...] = dun * (gv * sg)
            dg_ref[:, sl] = dun * o_ref[...] * (sg * (1.0 + gv * (1.0 - sg)))

        @pl.when(i % tps == 0)
        def _():
            dgt_ref[0] = dgt

        @pl.when(i % tps != 0)
        def _():
            dgt_ref[0] += dgt

    row = lambda wd: pl.BlockSpec((tm, wd), lambda i: (i, 0))
    per_b = pl.BlockSpec((1, 1, D_MODEL), lambda i: (i // tps, 0, 0))
    return pl.pallas_call(
        body, name="gate_out_bwd", grid=(t // tm,),
        in_specs=[row(D_MODEL), row(D_MODEL), per_b, row(GW), row(GW), row(GW), row(U_PAD),
                  pl.BlockSpec((D_MODEL, U_PAD), lambda i: (0, 0))],
        out_specs=[row(D_MODEL), row(GW), row(GW), row(GW), row(U_PAD), per_b],
        out_shape=[jax.ShapeDtypeStruct((t, D_MODEL), BF16), jax.ShapeDtypeStruct((t, GW), F32),
                   jax.ShapeDtypeStruct((t, GW), F32), jax.ShapeDtypeStruct((t, GW), F32),
                   jax.ShapeDtypeStruct((t, U_PAD), F32), jax.ShapeDtypeStruct((nb, 1, D_MODEL), F32)],
        compiler_params=_cparams(("arbitrary",)),
    )(dxo, y, gate, oa, ob, oc, gates, w_t)


def _final_loss(x, target, g, tm=256):
    t = x.shape[0]

    def body(x_ref, t_ref, g_ref, dx_ref, loss_ref, dg_ref):
        i = pl.program_id(0)
        xv = x_ref[...]
        rstd = lax.rsqrt(jnp.mean(xv * xv, axis=-1, keepdims=True) + EPS)
        xn = xv * rstd
        gv = g_ref[...]
        err = xn * gv - t_ref[...]
        dy = err * (1.0 / D_MODEL)
        dxn = dy * gv
        dx_ref[...] = rstd * (dxn - xn * jnp.mean(dxn * xn, axis=-1, keepdims=True))
        lp = jnp.sum(err * err, axis=0, keepdims=True) * (0.5 / D_MODEL)
        dgp = jnp.sum(dy * xn, axis=0, keepdims=True)

        @pl.when(i == 0)
        def _():
            loss_ref[...] = lp
            dg_ref[...] = dgp

        @pl.when(i != 0)
        def _():
            loss_ref[...] += lp
            dg_ref[...] += dgp

    row = pl.BlockSpec((tm, D_MODEL), lambda i: (i, 0))
    vec = pl.BlockSpec((1, D_MODEL), lambda i: (0, 0))
    return pl.pallas_call(
        body, name="final_loss", grid=(t // tm,),
        in_specs=[row, row, vec], out_specs=[row, vec, vec],
        out_shape=[jax.ShapeDtypeStruct((t, D_MODEL), F32), jax.ShapeDtypeStruct((1, D_MODEL), F32),
                   jax.ShapeDtypeStruct((1, D_MODEL), F32)],
        compiler_params=_cparams(("arbitrary",)),
    )(x, target, g)


def _adamw(w, gslots, m, v, name, tr=None):
    r, c = w.shape
    ns = gslots.shape[0]
    tr = r if tr is None else tr

    def body(w_ref, g_ref, m_ref, v_ref, go_ref, d_ref, mo_ref, vo_ref):
        g = g_ref[0].astype(F32)
        for j in range(1, ns):
            g = g + g_ref[j].astype(F32)
        mn = ADAM_B1 * m_ref[...] + (1.0 - ADAM_B1) * g
        vn = ADAM_B2 * v_ref[...] + (1.0 - ADAM_B2) * jnp.square(g)
        m_hat = mn / (1.0 - ADAM_B1 ** ADAM_STEP)
        v_hat = vn / (1.0 - ADAM_B2 ** ADAM_STEP)
        go_ref[...] = g
        d_ref[...] = -ADAM_LR * (m_hat / (jnp.sqrt(v_hat) + ADAM_EPS) + ADAM_WD * w_ref[...])
        mo_ref[...] = mn
        vo_ref[...] = vn

    blk = pl.BlockSpec((tr, c), lambda i: (i, 0))
    return pl.pallas_call(
        body, name=name, grid=(r // tr,),
        in_specs=[blk, pl.BlockSpec((ns, tr, c), lambda i: (0, i, 0)), blk, blk],
        out_specs=[blk] * 4, out_shape=[jax.ShapeDtypeStruct((r, c), F32)] * 4,
        compiler_params=_cparams(("arbitrary",)),
    )(w, gslots, m, v)


def _rope_tables(positions):
    inv = ROPE_THETA ** (-jnp.arange(0, A_ROPE, 2, dtype=F32) / A_ROPE)
    ang = positions.astype(F32)[:, None] * inv
    cos, sin = jnp.cos(ang), jnp.sin(ang)
    t = positions.shape[0]
    one = jnp.ones((t, 64), F32)
    zero16 = jnp.zeros((t, 16), F32)
    cos_t = jnp.concatenate([one, cos, cos, jnp.ones((t, 32), F32)], axis=1)
    sin_a = jnp.concatenate([jnp.zeros((t, 64), F32), -sin, zero16, jnp.zeros((t, 32), F32)], axis=1)
    sin_b = jnp.concatenate([jnp.zeros((t, 64), F32), zero16, sin, jnp.zeros((t, 32), F32)], axis=1)
    return cos_t, sin_a, sin_b


def _pad_heads(w, real, padded, nheads, axis):
    shp = w.shape[:axis] + (nheads, real) + w.shape[axis + 1:]
    w = w.reshape(shp)
    pad = [(0, 0)] * w.ndim
    pad[axis + 1] = (0, padded - real)
    w = jnp.pad(w, pad)
    return w.reshape(w.shape[:axis] + (nheads * padded,) + w.shape[axis + 2:])


def kernel(x, c, positions, w_ada, b_ada, norm_g, w_in, a_q_norm_g, a_w_uq, a_kv_norm_g, a_w_ukv, b_rel_bias, c_forget_b, w_out, final_g, loss_target, m_w_ada, m_b_ada, m_norm_g, m_w_in, m_a_q_norm_g, m_a_w_uq, m_a_kv_norm_g, m_a_w_ukv, m_b_rel_bias, m_c_forget_b, m_w_out, m_final_g, v_w_ada, v_b_ada, v_norm_g, v_w_in, v_a_q_norm_g, v_a_w_uq, v_a_kv_norm_g, v_a_w_ukv, v_b_rel_bias, v_c_forget_b, v_w_out, v_final_g):
    nb, seq, _ = x.shape
    t = nb * seq
    me = 4 * lax.axis_index("x") + 2 * lax.axis_index("y") + lax.axis_index("c")
    x2 = x.reshape(t, D_MODEL)
    tgt = loss_target.reshape(t, D_MODEL)
    cos_t, sin_a, sin_b = _rope_tables(positions.reshape(t))

    shards = []
    for l in range(DEPTH):
        shards += [_pad_runs(w_in[l].astype(BF16), IN_RUNS, N_PAD, 1), w_out[l].astype(BF16),
                   a_w_uq[l].astype(BF16), a_w_ukv[l].astype(BF16)]
    gathered = _gather(shards + [c], "gather_weights")
    c_all = gathered[-1].reshape(N_DEV * nb, D_MODEL)
    w_in_p, w_in_t, w_out_p, w_out_t, wuq_p, wuq_t, wk_p, wk_t, wv_p, wv_t = ([] for _ in range(10))
    for l in range(DEPTH):
        gi, go, gq, gkv = gathered[4 * l:4 * l + 4]
        wi = gi.reshape(D_MODEL, N_PAD)
        wo = _pad_runs(go.reshape(D_MODEL, D_MODEL), OUT_RUNS, U_PAD, 0)
        wq = jnp.transpose(gq, (1, 0, 2)).reshape(A_Q_RANK, A_HEADS * (A_NOPE + A_ROPE))
        wq = _pad_heads(wq, A_NOPE + A_ROPE, HEAD_PAD, A_HEADS, 1)
        wkv = jnp.transpose(gkv, (1, 0, 2)).reshape(A_KV_RANK, A_HEADS, 2 * A_NOPE)
        wk = jnp.pad(wkv[:, :, :A_NOPE], ((0, 0), (0, 0), (0, HEAD_PAD - A_NOPE))).reshape(A_KV_RANK, A_HEADS * HEAD_PAD)
        wv = wkv[:, :, A_NOPE:].reshape(A_KV_RANK, GW)
        w_in_p.append(wi); w_in_t.append(wi.T); w_out_p.append(wo); w_out_t.append(wo.T)
        wuq_p.append(wq); wuq_t.append(wq.T); wk_p.append(wk); wk_t.append(wk.T); wv_p.append(wv); wv_t.append(wv.T)

    c_act, mod_cols = _ada_fwd(c_all, w_ada)
    (mod_g,) = _gather([mod_cols], "gather_mod")
    mod_all = jnp.transpose(mod_g, (1, 2, 0, 3)).reshape(DEPTH, N_DEV * nb, 3 * D_MODEL)
    mod = lax.dynamic_slice_in_dim(mod_all, me * nb, nb, axis=1) + b_ada[:, None, :]

    fb_pad = jnp.pad(c_forget_b, ((0, 0), (0, 128 - C_HEADS)))
    a_scale = (A_NOPE + A_ROPE) ** -0.5
    h_scale = CHUNK ** -0.5

    saved = []
    xl = x2
    for l in range(DEPTH):
        shift, scale, gate = mod[l, :, :D_MODEL], mod[l, :, D_MODEL:2 * D_MODEL], mod[l, :, 2 * D_MODEL:]
        ss = jnp.stack([shift, 1.0 + scale], axis=1)
        gate3 = gate[:, None, :]
        h, cq, ckv, kpe, gates, bq, bk, bv, cq2, ck, cv, cf = _ln_in(xl, ss, norm_g[l:l + 1], w_in_p[l], seq)
        q, k, v, cqn, ckvn = _mla_prep(cq, ckv, kpe, a_q_norm_g[l:l + 1], a_kv_norm_g[l:l + 1],
                                       wuq_p[l], wk_p[l], wv_p[l], cos_t, sin_a, sin_b)
        oa, sta = _attn_fwd("mla", q, k, v, None, seq, a_scale)
        table = _band_table(jnp.pad(b_rel_bias[l], ((0, 8 - B_HEADS), (0, GW - N_REL))))
        ob, stb = _band_fwd(bq, bk, bv, table, seq, h_scale)
        fcum = _fox_prep(cf, fb_pad[l:l + 1], seq)
        oc, stc = _attn_fwd("fox", cq2, ck, cv, fcum, seq, h_scale)
        xn, y, u = _gate_out(oa, ob, oc, gates, w_out_p[l], xl, gate3, seq)
        saved.append(dict(x=xl, ss=ss, gate3=gate3, h=h, cq=cq, ckv=ckv, gates=gates, bq=bq, bk=bk, bv=bv,
                          cq2=cq2, ck=ck, cv=cv, cf=cf, q=q, k=k, v=v, cqn=cqn, ckvn=ckvn, oa=oa, sta=sta,
                          table=table, ob=ob, stb=stb, fcum=fcum, oc=oc, stc=stc, y=y, u=u))
        xl = xn

    dx, loss_lanes, g_final = _final_loss(xl, tgt, final_g[None, :])
    loss = lax.psum(jnp.sum(loss_lanes), AXES)

    g_in_a, g_in_b, g_out, g_uq, g_ukv, dmods, smalls = ([None] * DEPTH for _ in range(7))
    n_seg_a = 4
    for l in reversed(range(DEPTH)):
        s = saved[l]
        dy, doa, dob, doc, dgates, dgate = _gate_out_bwd(dx, s["y"], s["gate3"], s["oa"], s["ob"], s["oc"],
                                                         s["gates"], w_out_t[l], seq)
        g_out[l] = _unpad_runs(_matmul_tn(s["u"], dy, "dw_out"), OUT_RUNS, 0)
        dq, dk, dv = _attn_bwd("mla", s["q"], s["k"], s["v"], None, s["oa"], s["sta"], doa, seq, a_scale)
        dbq, dbk, dbv, gtab = _band_bwd(s["bq"], s["bk"], s["bv"], s["table"], s["ob"], s["stb"], dob, seq, h_scale)
        g_rel = _band_table_bwd(gtab)[:, 0, :N_REL]
        dcq2, dck, dcv, dfc, dfq = _attn_bwd("fox", s["cq2"], s["ck"], s["cv"], s["fcum"], s["oc"], s["stc"], doc,
                                             seq, h_scale)
        dcf, dfb = _fox_prep_bwd(dfc, dfq, s["cf"], fb_pad[l:l + 1], seq)
        dcq, dckv, dkpe, dqlin, dklin, dgq, dgkv = _mla_prep_bwd(
            dq, dk, dv, s["cq"], s["ckv"], a_q_norm_g[l:l + 1], a_kv_norm_g[l:l + 1],
            wuq_t[l], wk_t[l], wv_t[l], cos_t, sin_a, sin_b)
        gq_pad = _matmul_tn(s["cqn"], dqlin, "dw_uq")
        g_uq[l] = gq_pad.reshape(A_Q_RANK, A_HEADS, HEAD_PAD)[:, :, :A_NOPE + A_ROPE].reshape(A_Q_RANK, -1)
        gk_pad = _matmul_tn(s["ckvn"], dklin, "dw_uk").reshape(A_KV_RANK, A_HEADS, HEAD_PAD)[:, :, :A_NOPE]
        gv_pad = _matmul_tn(s["ckvn"], dv, "dw_uv").reshape(A_KV_RANK, A_HEADS, A_NOPE)
        g_ukv[l] = jnp.concatenate([gk_pad, gv_pad], axis=2).reshape(A_KV_RANK, -1)
        dz = [dcq, dckv, dkpe, dgates, dbq, dbk, dbv, dcq2, dck, dcv, dcf]
        g_in_a[l] = _matmul_tn(s["h"], dz[:n_seg_a], "dw_in_a", 256)
        g_in_b[l] = _matmul_tn(s["h"], dz[n_seg_a:], "dw_in_b", 256)
        dx, dss, dg_norm = _ln_in_bwd(dz, w_in_t[l], s["x"], s["ss"], norm_g[l:l + 1], dx, seq)
        dmods[l] = jnp.concatenate([dss[:, 0, :], dss[:, 1, :], dgate[:, 0, :]], axis=1)
        smalls[l] = [dg_norm.reshape(-1), dgq.reshape(-1), dgkv.reshape(-1), g_rel.reshape(-1),
                     dfb[0, :C_HEADS]]
    grad_x = dx.reshape(nb, seq, D_MODEL)

    small = jnp.concatenate([p for l in range(DEPTH) for p in smalls[l]] + [g_final.reshape(-1)])
    n_small = small.shape[0]
    small_rows = -(-n_small // 1024) * 8
    small = jnp.pad(small, (0, small_rows * 128 - n_small)).reshape(small_rows, 128)
    dmod_local = jnp.stack(dmods)
    dmod_g, small_g = _gather([dmod_local, small], "gather_small")
    dmod_all = jnp.transpose(dmod_g, (1, 0, 2, 3)).reshape(DEPTH, N_DEV * nb, 3 * D_MODEL)
    cols = 3 * D_MODEL // N_DEV
    dmod_mine = lax.dynamic_slice_in_dim(dmod_all, me * cols, cols, axis=2)
    g_w_ada, g_b_ada = _ada_bwd(c_act, dmod_all, dmod_mine)
    small_sum = _sum_slots(small_g, "sum_small").reshape(-1)

    rows = D_MODEL // N_DEV
    a2a_in, a2a_names = [], []
    for l in range(DEPTH):
        a2a_in += [g_in_a[l].reshape(N_DEV, rows, -1), g_in_b[l].reshape(N_DEV, rows, -1),
                   g_out[l].reshape(N_DEV, rows, D_MODEL)]
        a2a_names += ["in_a", "in_b", "out"]
    a2a_in += [jnp.stack(g_uq).reshape(DEPTH, A_Q_RANK, N_DEV, -1).transpose(2, 0, 1, 3).reshape(N_DEV, DEPTH * A_Q_RANK, -1),
               jnp.stack(g_ukv).reshape(DEPTH, A_KV_RANK, N_DEV, -1).transpose(2, 0, 1, 3).reshape(N_DEV, DEPTH * A_KV_RANK, -1)]
    a2a_names += ["uq", "ukv"]
    from_sib = _pair_swap(a2a_in, "grads_pair")
    core = lax.axis_index("c").astype(jnp.int32).reshape(1)
    chip_sums = [_pair_add(core, a, r, "grads_add_" + nm, r.shape[1]) for a, r, nm in zip(a2a_in, from_sib, a2a_names)]
    parts = _chip_a2a(chip_sums, "grads_chips")
    p_in = jnp.stack([_unpad_runs(jnp.concatenate(parts[3 * l:3 * l + 2], axis=2), IN_RUNS, 2) for l in range(DEPTH)], axis=1)
    p_out = jnp.stack([parts[3 * l + 2] for l in range(DEPTH)], axis=1)
    p_uq, p_ukv = parts[3 * DEPTH], parts[3 * DEPTH + 1]

    def split_small():
        out, pos = [], 0
        sizes = [D_MODEL, A_Q_RANK, A_KV_RANK, B_HEADS * N_REL, C_HEADS]
        per_layer = []
        for l in range(DEPTH):
            parts = []
            for sz in sizes:
                parts.append(small_sum[pos:pos + sz])
                pos += sz
            per_layer.append(parts)
        for j in range(len(sizes)):
            out.append(jnp.stack([per_layer[l][j] for l in range(DEPTH)]))
        out.append(small_sum[pos:pos + D_MODEL])
        return out

    g_norm, g_qn, g_kvn, g_relb, g_fb, g_fin = split_small()

    def adam(w, g, m, v, name, tr=None):
        shp = w.shape
        w2 = w.reshape(-1, shp[-1]) if w.ndim > 1 else w.reshape(1, -1)
        gs = g.reshape((-1,) + w2.shape) if g.size != w.size else g.reshape((1,) + w2.shape)
        outs = _adamw(w2, gs, m.reshape(w2.shape), v.reshape(w2.shape), name, tr)
        return [o.reshape(shp) for o in outs]

    res = {
        "w_ada": adam(w_ada, g_w_ada, m_w_ada, v_w_ada, "adam_w_ada", 256),
        "b_ada": adam(b_ada, g_b_ada, m_b_ada, v_b_ada, "adam_b_ada"),
        "norm_g": adam(norm_g, g_norm, m_norm_g, v_norm_g, "adam_norm_g"),
        "w_in": adam(w_in, p_in, m_w_in, v_w_in, "adam_w_in", 32),
        "a_q_norm_g": adam(a_q_norm_g, g_qn, m_a_q_norm_g, v_a_q_norm_g, "adam_q_norm"),
        "a_w_uq": adam(a_w_uq, p_uq, m_a_w_uq, v_a_w_uq, "adam_w_uq"),
        "a_kv_norm_g": adam(a_kv_norm_g, g_kvn, m_a_kv_norm_g, v_a_kv_norm_g, "adam_kv_norm"),
        "a_w_ukv": adam(a_w_ukv, p_ukv, m_a_w_ukv, v_a_w_ukv, "adam_w_ukv"),
        "b_rel_bias": adam(b_rel_bias, g_relb.reshape(b_rel_bias.shape), m_b_rel_bias, v_b_rel_bias, "adam_rel_bias"),
        "c_forget_b": adam(c_forget_b, g_fb, m_c_forget_b, v_c_forget_b, "adam_forget_b"),
        "w_out": adam(w_out, p_out, m_w_out, v_w_out, "adam_w_out", 64),
        "final_g": adam(final_g, g_fin, m_final_g, v_final_g, "adam_final_g"),
    }
    names = ["w_ada", "b_ada", "norm_g", "w_in", "a_q_norm_g", "a_w_uq", "a_kv_norm_g", "a_w_ukv", "b_rel_bias",
             "c_forget_b", "w_out", "final_g"]
    outs = [loss, grad_x]
    for j in range(4):
        outs += [res[n][j] for n in names]
    return tuple(outs)
```

```python
import functools

import jax
import jax.numpy as jnp
from jax import lax
from jax.experimental import pallas as pl
from jax.experimental.pallas import tpu as pltpu

F32 = jnp.float32
BF16 = jnp.bfloat16
HI = lax.Precision.HIGHEST

N_DEV = 8
AXES = ("x", "y", "c")
D_MODEL = 1024
DEPTH = 2
CHUNK = 64
EPS = 1e-6
NEG = -1e30
A_HEADS = 6
A_NOPE = 64
A_ROPE = 32
A_Q_RANK = 384
A_KV_RANK = 256
ROPE_THETA = 10000.0
B_HEADS = 5
B_LEFT = 512
REL_CLIP = 128
N_REL = 2 * REL_CLIP + 1
C_HEADS = 5
HEAD_PAD = 128
GW = 384
N_IN = 3621
ADAM_LR = 0.001
ADAM_B1 = 0.9
ADAM_B2 = 0.999
ADAM_EPS = 1e-08
ADAM_WD = 0.01
ADAM_STEP = 10
VMEM_LIMIT = 56 * 1024 * 1024

Z_SEGS = (
    ("cq", 0, 384, F32), ("ckv", 384, 256, F32), ("kpe", 640, 128, F32), ("gates", 768, 1152, F32),
    ("bq", 1920, 384, BF16), ("bk", 2304, 384, BF16), ("bv", 2688, 384, BF16),
    ("cq2", 3072, 384, BF16), ("ck", 3456, 384, BF16), ("cv", 3840, 384, BF16), ("cf", 4224, 128, F32),
)
N_PAD = 4352
IN_RUNS = (
    (0, 384, 0), (384, 256, 384), (640 + 64, 32, 640),
    (768, 384, 672), (768 + 384, 320, 2016), (768 + 768, 320, 3301),
    (1920, 320, 1056), (2304, 320, 1376), (2688, 320, 1696),
    (3072, 320, 2336), (3456, 320, 2656), (3840, 320, 2976), (4224, 5, 3296),
)
OUT_RUNS = ((0, 384, 0), (384, 320, 384), (768, 320, 704))
U_PAD = 1152


def _cparams(sem=None, vmem=VMEM_LIMIT):
    return pltpu.CompilerParams(dimension_semantics=sem, vmem_limit_bytes=vmem)


def _pad_runs(w, runs, total, axis):
    order = sorted(runs)
    parts, pos = [], 0
    for off, wd, src in order:
        if off > pos:
            shp = list(w.shape)
            shp[axis] = off - pos
            parts.append(jnp.zeros(shp, w.dtype))
        parts.append(lax.slice_in_dim(w, src, src + wd, axis=axis))
        pos = off + wd
    if pos < total:
        shp = list(w.shape)
        shp[axis] = total - pos
        parts.append(jnp.zeros(shp, w.dtype))
    return jnp.concatenate(parts, axis=axis)


def _unpad_runs(w, runs, axis):
    order = sorted(runs, key=lambda r: r[2])
    return jnp.concatenate([lax.slice_in_dim(w, off, off + wd, axis=axis) for off, wd, _ in order], axis=axis)


def _sigmoid(x):
    return 1.0 / (1.0 + jnp.exp(-x))


N_CHIP = 4
ANY_SPEC = pl.BlockSpec(memory_space=pl.ANY)
MESH_ID = pl.DeviceIdType.MESH


def _gather(arrs, name):
    n = len(arrs)

    def body(*refs):
        ins, outs = refs[:n], refs[n:2 * n]
        send_sems, recv_sems, local_sems = refs[2 * n:]
        x, y, c = lax.axis_index("x"), lax.axis_index("y"), lax.axis_index("c")
        me, sib = (x, y, c), (x, y, 1 - c)
        chips = [(1 - x, y), (x, 1 - y), (1 - x, 1 - y)]

        def slot(px, py, pc):
            return 4 * px + 2 * py + pc

        def copy(a, k, block, to, src=None):
            dst = outs[a].at[slot(*block)]
            return pltpu.make_async_remote_copy(
                src_ref=dst if src is None else src, dst_ref=dst, send_sem=send_sems.at[a, k],
                recv_sem=recv_sems.at[a, k], device_id=to, device_id_type=MESH_ID)

        local = [pltpu.make_async_copy(ins[a], outs[a].at[slot(*me)], local_sems.at[a]) for a in range(n)]
        first = []
        for a in range(n):
            first.append(copy(a, 0, me, sib, src=ins[a]))
            first += [copy(a, 1 + j, me, (*chip, c), src=ins[a]) for j, chip in enumerate(chips)]
        for cp in local + first:
            cp.start()
        passed = []
        for j, chip in enumerate(chips):
            for a in range(n):
                copy(a, 1 + j, (*chip, c), me).wait_recv()
                fwd = copy(a, 4 + j, (*chip, c), sib)
                fwd.start()
                passed.append(fwd)
        for a in range(n):
            copy(a, 0, sib, me).wait_recv()
            for j, chip in enumerate(chips):
                copy(a, 4 + j, (*chip, 1 - c), me).wait_recv()
        for cp in first + passed:
            cp.wait_send()
        for cp in local:
            cp.wait()

    return pl.pallas_call(
        body, name=name, out_shape=[jax.ShapeDtypeStruct((N_DEV,) + a.shape, a.dtype) for a in arrs],
        in_specs=[ANY_SPEC] * n, out_specs=[ANY_SPEC] * n,
        scratch_shapes=[pltpu.SemaphoreType.DMA((n, N_DEV - 1)), pltpu.SemaphoreType.DMA((n, N_DEV - 1)),
                        pltpu.SemaphoreType.DMA((n,))],
    )(*arrs)


def _pair_swap(arrs, name):
    n = len(arrs)

    def body(*refs):
        ins, outs = refs[:n], refs[n:2 * n]
        send_sems, recv_sems = refs[2 * n:]
        x, y, c = lax.axis_index("x"), lax.axis_index("y"), lax.axis_index("c")
        copies = []
        for a in range(n):
            for q in range(N_CHIP):
                cp = pltpu.make_async_remote_copy(
                    src_ref=ins[a].at[2 * q + 1 - c], dst_ref=outs[a].at[q], send_sem=send_sems.at[a, q],
                    recv_sem=recv_sems.at[a, q], device_id=(x, y, 1 - c), device_id_type=MESH_ID)
                cp.start()
                copies.append(cp)
        for cp in copies:
            cp.wait()

    return pl.pallas_call(
        body, name=name, out_shape=[jax.ShapeDtypeStruct((N_CHIP,) + a.shape[1:], a.dtype) for a in arrs],
        in_specs=[ANY_SPEC] * n, out_specs=[ANY_SPEC] * n,
        scratch_shapes=[pltpu.SemaphoreType.DMA((n, N_CHIP)), pltpu.SemaphoreType.DMA((n, N_CHIP))],
    )(*arrs)


def _chip_a2a(arrs, name):
    n = len(arrs)

    def body(*refs):
        ins, outs = refs[:n], refs[n:2 * n]
        send_sems, recv_sems, local_sems = refs[2 * n:]
        x, y, c = lax.axis_index("x"), lax.axis_index("y"), lax.axis_index("c")
        mine = 2 * x + y
        copies = []
        for a in range(n):
            loc = pltpu.make_async_copy(ins[a].at[mine], outs[a].at[mine], local_sems.at[a])
            loc.start()
            copies.append(loc)
            for k in range(1, N_CHIP):
                px = (1 - x) if (k >> 1) & 1 else x
                py = (1 - y) if k & 1 else y
                cp = pltpu.make_async_remote_copy(
                    src_ref=ins[a].at[2 * px + py], dst_ref=outs[a].at[mine], send_sem=send_sems.at[a, k - 1],
                    recv_sem=recv_sems.at[a, k - 1], device_id=(px, py, c), device_id_type=MESH_ID)
                cp.start()
                copies.append(cp)
        for cp in copies:
            cp.wait()

    return pl.pallas_call(
        body, name=name, out_shape=[jax.ShapeDtypeStruct(a.shape, a.dtype) for a in arrs],
        in_specs=[ANY_SPEC] * n, out_specs=[ANY_SPEC] * n,
        scratch_shapes=[pltpu.SemaphoreType.DMA((n, N_CHIP - 1)), pltpu.SemaphoreType.DMA((n, N_CHIP - 1)),
                        pltpu.SemaphoreType.DMA((n,))],
    )(*arrs)


def _pair_add(core, a8, b4, name, tr):
    _, r, c = b4.shape

    def body(core_ref, a_ref, b_ref, o_ref):
        o_ref[...] = (a_ref[...] + b_ref[...]).astype(BF16)

    blk = pl.BlockSpec((1, tr, c), lambda q, i, core_ref: (q, i, 0))
    grid_spec = pltpu.PrefetchScalarGridSpec(
        num_scalar_prefetch=1, grid=(N_CHIP, r // tr),
        in_specs=[pl.BlockSpec((1, tr, c), lambda q, i, core_ref: (2 * q + core_ref[0], i, 0)), blk], out_specs=blk)
    return pl.pallas_call(
        body, name=name, grid_spec=grid_spec, out_shape=jax.ShapeDtypeStruct(b4.shape, BF16),
        compiler_params=_cparams(("arbitrary", "arbitrary")),
    )(core, a8, b4)


def _sum_slots(x, name):
    _, r, c = x.shape

    def body(x_ref, o_ref):
        acc = x_ref[0]
        for j in range(1, N_DEV):
            acc = acc + x_ref[j]
        o_ref[...] = acc

    return pl.pallas_call(body, name=name, out_shape=jax.ShapeDtypeStruct((r, c), F32))(x)


def _ada_fwd(c_all, w_ada):
    nb = c_all.shape[0]
    cols = w_ada.shape[2]

    def body(c_ref, w_ref, act_ref, mod_ref):
        cv = c_ref[...]
        act = cv * _sigmoid(cv)
        act_ref[...] = act
        for l in range(DEPTH):
            mod_ref[l] = jnp.dot(act, w_ref[l], precision=HI, preferred_element_type=F32)

    return pl.pallas_call(
        body, name="ada_fwd",
        out_shape=[jax.ShapeDtypeStruct((nb, D_MODEL), F32), jax.ShapeDtypeStruct((DEPTH, nb, cols), F32)],
        compiler_params=_cparams(),
    )(c_all, w_ada)


def _ada_bwd(c_act, dmod_all, dmod_mine):
    nb = c_act.shape[0]
    cols = dmod_mine.shape[2]

    def body(act_ref, dall_ref, dmine_ref, gw_ref, gb_ref):
        act = act_ref[...]
        for l in range(DEPTH):
            gw_ref[l] = lax.dot_general(act, dmine_ref[l], (((0,), (0,)), ((), ())),
                                        precision=HI, preferred_element_type=F32)
            gb_ref[l:l + 1, :] = jnp.sum(dall_ref[l], axis=0, keepdims=True)

    return pl.pallas_call(
        body, name="ada_bwd",
        out_shape=[jax.ShapeDtypeStruct((DEPTH, D_MODEL, cols), F32),
                   jax.ShapeDtypeStruct((DEPTH, 3 * D_MODEL), F32)],
        compiler_params=_cparams(),
    )(c_act, dmod_all, dmod_mine)


def _ln_in(x, ss, g, w, seq, tm=256):
    t = x.shape[0]
    tps = seq // tm

    def body(x_ref, ss_ref, g_ref, w_ref, h_ref, *outs):
        xv = x_ref[...]
        xn = xv * lax.rsqrt(jnp.mean(xv * xv, axis=-1, keepdims=True) + EPS)
        h = xn * g_ref[...] * ss_ref[0, 1:2, :] + ss_ref[0, 0:1, :]
        hb = h.astype(BF16)
        h_ref[...] = hb
        z = jnp.dot(hb, w_ref[...], preferred_element_type=F32)
        for o_ref, (_, off, wd, _) in zip(outs, Z_SEGS):
            o_ref[...] = z[:, off:off + wd].astype(o_ref.dtype)

    row = lambda wd: pl.BlockSpec((tm, wd), lambda i: (i, 0))
    return pl.pallas_call(
        body, name="ln_in", grid=(t // tm,),
        in_specs=[row(D_MODEL), pl.BlockSpec((1, 2, D_MODEL), lambda i: (i // tps, 0, 0)),
                  pl.BlockSpec((1, D_MODEL), lambda i: (0, 0)), pl.BlockSpec((D_MODEL, N_PAD), lambda i: (0, 0))],
        out_specs=[row(D_MODEL)] + [row(wd) for _, _, wd, _ in Z_SEGS],
        out_shape=[jax.ShapeDtypeStruct((t, D_MODEL), BF16)]
        + [jax.ShapeDtypeStruct((t, wd), dt) for _, _, wd, dt in Z_SEGS],
        compiler_params=_cparams(("arbitrary",)),
    )(x, ss, g, w)


def _ln_in_bwd(dz, w_t, x, ss, g, dxo, seq, tm=256):
    t = x.shape[0]
    tps = seq // tm
    nb = t // seq
    nz = len(Z_SEGS)

    def body(*refs):
        dz_refs = refs[:nz]
        wt_ref, x_ref, ss_ref, g_ref, dxo_ref, dx_ref, dss_ref, dg_ref = refs[nz:]
        i = pl.program_id(0)
        dzc = jnp.concatenate([r[...].astype(BF16) for r in dz_refs], axis=1)
        dh = jnp.dot(dzc, wt_ref[...], preferred_element_type=F32)
        xv = x_ref[...]
        rstd = lax.rsqrt(jnp.mean(xv * xv, axis=-1, keepdims=True) + EPS)
        xn = xv * rstd
        gv = g_ref[...]
        s1 = ss_ref[0, 1:2, :]
        dxg = dh * s1
        dxn = dxg * gv
        dx = rstd * (dxn - xn * jnp.mean(dxn * xn, axis=-1, keepdims=True))
        dx_ref[...] = dxo_ref[...] + dx
        dshift = jnp.sum(dh, axis=0, keepdims=True)
        dscale = jnp.sum(dh * (xn * gv), axis=0, keepdims=True)
        dgp = jnp.sum(dxg * xn, axis=0, keepdims=True)

        @pl.when(i % tps == 0)
        def _():
            dss_ref[0, 0:1, :] = dshift
            dss_ref[0, 1:2, :] = dscale

        @pl.when(i % tps != 0)
        def _():
            dss_ref[0, 0:1, :] += dshift
            dss_ref[0, 1:2, :] += dscale

        @pl.when(i == 0)
        def _():
            dg_ref[...] = dgp

        @pl.when(i != 0)
        def _():
            dg_ref[...] += dgp

    row = lambda wd: pl.BlockSpec((tm, wd), lambda i: (i, 0))
    return pl.pallas_call(
        body, name="ln_in_bwd", grid=(t // tm,),
        in_specs=[row(wd) for _, _, wd, _ in Z_SEGS]
        + [pl.BlockSpec((N_PAD, D_MODEL), lambda i: (0, 0)), row(D_MODEL),
           pl.BlockSpec((1, 2, D_MODEL), lambda i: (i // tps, 0, 0)),
           pl.BlockSpec((1, D_MODEL), lambda i: (0, 0)), row(D_MODEL)],
        out_specs=[row(D_MODEL), pl.BlockSpec((1, 2, D_MODEL), lambda i: (i // tps, 0, 0)),
                   pl.BlockSpec((1, D_MODEL), lambda i: (0, 0))],
        out_shape=[jax.ShapeDtypeStruct((t, D_MODEL), F32), jax.ShapeDtypeStruct((nb, 2, D_MODEL), F32),
                   jax.ShapeDtypeStruct((1, D_MODEL), F32)],
        compiler_params=_cparams(("arbitrary",)),
    )(*dz, w_t, x, ss, g, dxo)


def _matmul_tn(a, bs, name, tm=1024):
    bs = list(bs) if isinstance(bs, (list, tuple)) else [bs]
    t, k = a.shape
    widths = [b.shape[1] for b in bs]
    n = sum(widths)
    tm = min(tm, t)

    def body(a_ref, *refs):
        b_refs, o_ref = refs[:-1], refs[-1]
        i = pl.program_id(0)
        av = a_ref[...].astype(BF16)
        parts = [b_ref[...].astype(BF16) for b_ref in b_refs]
        bv = parts[0] if len(parts) == 1 else jnp.concatenate(parts, axis=1)
        part = lax.dot_general(av, bv, (((0,), (0,)), ((), ())), preferred_element_type=F32)

        @pl.when(i == 0)
        def _():
            o_ref[...] = part

        @pl.when(i != 0)
        def _():
            o_ref[...] += part

    return pl.pallas_call(
        body, name=name, grid=(t // tm,),
        in_specs=[pl.BlockSpec((tm, k), lambda i: (i, 0))] + [pl.BlockSpec((tm, wd), lambda i: (i, 0)) for wd in widths],
        out_specs=pl.BlockSpec((k, n), lambda i: (0, 0)),
        out_shape=jax.ShapeDtypeStruct((k, n), F32),
        compiler_params=_cparams(("arbitrary",)),
    )(a, *bs)


def _rope(blk, cos_t, sin_a, sin_b):
    return blk * cos_t + pltpu.roll(blk, 112, 1) * sin_a + pltpu.roll(blk, 16, 1) * sin_b


def _unrope(d, cos_t, sin_a, sin_b):
    return d * cos_t + pltpu.roll(d * sin_a, 16, 1) + pltpu.roll(d * sin_b, 112, 1)


def _mla_prep(cq, ckv, kpe, gq, gkv, wuq, wk, wv, cos_t, sin_a, sin_b, tm=256):
    t = cq.shape[0]
    qw = A_HEADS * HEAD_PAD

    def body(cq_ref, ckv_ref, kpe_ref, gq_ref, gkv_ref, wuq_ref, wk_ref, wv_ref, c_ref, sa_ref, sb_ref,
             q_ref, k_ref, v_ref, cqn_ref, ckvn_ref):
        ct, sa, sb = c_ref[...], sa_ref[...], sb_ref[...]
        a = cq_ref[...]
        cqn = (a * lax.rsqrt(jnp.mean(a * a, axis=-1, keepdims=True) + EPS) * gq_ref[...]).astype(BF16)
        cqn_ref[...] = cqn
        b = ckv_ref[...]
        ckvn = (b * lax.rsqrt(jnp.mean(b * b, axis=-1, keepdims=True) + EPS) * gkv_ref[...]).astype(BF16)
        ckvn_ref[...] = ckvn
        qlin = jnp.dot(cqn, wuq_ref[...], preferred_element_type=F32)
        klin = jnp.dot(ckvn, wk_ref[...], preferred_element_type=F32)
        v_ref[...] = jnp.dot(ckvn, wv_ref[...], preferred_element_type=F32).astype(BF16)
        kr = _rope(kpe_ref[...], ct, sa, sb)
        for h in range(A_HEADS):
            sl = slice(h * HEAD_PAD, (h + 1) * HEAD_PAD)
            q_ref[:, sl] = _rope(qlin[:, sl], ct, sa, sb).astype(BF16)
            k_ref[:, sl] = (klin[:, sl] + kr).astype(BF16)

    row = lambda wd: pl.BlockSpec((tm, wd), lambda i: (i, 0))
    full = lambda r, c: pl.BlockSpec((r, c), lambda i: (0, 0))
    return pl.pallas_call(
        body, name="mla_prep", grid=(t // tm,),
        in_specs=[row(A_Q_RANK), row(A_KV_RANK), row(128), full(1, A_Q_RANK), full(1, A_KV_RANK),
                  full(A_Q_RANK, qw), full(A_KV_RANK, qw), full(A_KV_RANK, GW), row(128), row(128), row(128)],
        out_specs=[row(qw), row(qw), row(GW), row(A_Q_RANK), row(A_KV_RANK)],
        out_shape=[jax.ShapeDtypeStruct((t, qw), BF16), jax.ShapeDtypeStruct((t, qw), BF16),
                   jax.ShapeDtypeStruct((t, GW), BF16), jax.ShapeDtypeStruct((t, A_Q_RANK), BF16),
                   jax.ShapeDtypeStruct((t, A_KV_RANK), BF16)],
        compiler_params=_cparams(("arbitrary",)),
    )(cq, ckv, kpe, gq, gkv, wuq, wk, wv, cos_t, sin_a, sin_b)


def _mla_prep_bwd(dq, dk, dv, cq, ckv, gq, gkv, wuq_t, wk_t, wv_t, cos_t, sin_a, sin_b, tm=256):
    t = cq.shape[0]
    qw = A_HEADS * HEAD_PAD

    def body(dq_ref, dk_ref, dv_ref, cq_ref, ckv_ref, gq_ref, gkv_ref, wuqt_ref, wkt_ref, wvt_ref,
             c_ref, sa_ref, sb_ref, dcq_ref, dckv_ref, dkpe_ref, dql_ref, dkl_ref, dgq_ref, dgkv_ref):
        i = pl.program_id(0)
        ct, sa, sb = c_ref[...], sa_ref[...], sb_ref[...]
        lane = lax.broadcasted_iota(jnp.int32, (1, HEAD_PAD), 1)
        nope = lane < A_NOPE
        rope = (lane >= A_NOPE) & (lane < A_NOPE + A_ROPE)
        dksum = None
        for h in range(A_HEADS):
            sl = slice(h * HEAD_PAD, (h + 1) * HEAD_PAD)
            dql_ref[:, sl] = _unrope(dq_ref[:, sl], ct, sa, sb).astype(BF16)
            dkh = dk_ref[:, sl]
            dkl_ref[:, sl] = jnp.where(nope, dkh, 0.0).astype(BF16)
            dksum = dkh if dksum is None else dksum + dkh
        dkpe_ref[...] = jnp.where(rope, _unrope(jnp.where(rope, dksum, 0.0), ct, sa, sb), 0.0)
        dcqn = jnp.dot(dql_ref[...], wuqt_ref[...], preferred_element_type=F32)
        dckvn = (jnp.dot(dkl_ref[...], wkt_ref[...], preferred_element_type=F32)
                 + jnp.dot(dv_ref[...].astype(BF16), wvt_ref[...], preferred_element_type=F32))

        def norm_bwd(xv, gv, dy):
            rstd = lax.rsqrt(jnp.mean(xv * xv, axis=-1, keepdims=True) + EPS)
            xn = xv * rstd
            dxn = dy * gv
            dx = rstd * (dxn - xn * jnp.mean(dxn * xn, axis=-1, keepdims=True))
            return dx, jnp.sum(dy * xn, axis=0, keepdims=True)

        dcq, dgq = norm_bwd(cq_ref[...], gq_ref[...], dcqn)
        dckv, dgkv = norm_bwd(ckv_ref[...], gkv_ref[...], dckvn)
        dcq_ref[...] = dcq
        dckv_ref[...] = dckv

        @pl.when(i == 0)
        def _():
            dgq_ref[...] = dgq
            dgkv_ref[...] = dgkv

        @pl.when(i != 0)
        def _():
            dgq_ref[...] += dgq
            dgkv_ref[...] += dgkv

    row = lambda wd: pl.BlockSpec((tm, wd), lambda i: (i, 0))
    full = lambda r, c: pl.BlockSpec((r, c), lambda i: (0, 0))
    return pl.pallas_call(
        body, name="mla_prep_bwd", grid=(t // tm,),
        in_specs=[row(qw), row(qw), row(GW), row(A_Q_RANK), row(A_KV_RANK), full(1, A_Q_RANK), full(1, A_KV_RANK),
                  full(qw, A_Q_RANK), full(qw, A_KV_RANK), full(GW, A_KV_RANK), row(128), row(128), row(128)],
        out_specs=[row(A_Q_RANK), row(A_KV_RANK), row(128), row(qw), row(qw), full(1, A_Q_RANK), full(1, A_KV_RANK)],
        out_shape=[jax.ShapeDtypeStruct((t, A_Q_RANK), F32), jax.ShapeDtypeStruct((t, A_KV_RANK), F32),
                   jax.ShapeDtypeStruct((t, 128), F32), jax.ShapeDtypeStruct((t, qw), BF16),
                   jax.ShapeDtypeStruct((t, qw), BF16), jax.ShapeDtypeStruct((1, A_Q_RANK), F32),
                   jax.ShapeDtypeStruct((1, A_KV_RANK), F32)],
        compiler_params=_cparams(("arbitrary",)),
    )(dq, dk, dv, cq, ckv, gq, gkv, wuq_t, wk_t, wv_t, cos_t, sin_a, sin_b)


def _nt(a, b):
    return lax.dot_general(a, b, (((1,), (1,)), ((), ())), preferred_element_type=F32)


def _tn(a, b):
    return lax.dot_general(a, b, (((0,), (0,)), ((), ())), preferred_element_type=F32)


def _causal_mask(kind, q0, k0, tq, tk):
    qpos = q0 + lax.broadcasted_iota(jnp.int32, (tq, tk), 0)
    kpos = k0 + lax.broadcasted_iota(jnp.int32, (tq, tk), 1)
    if kind == "mla":
        return lax.shift_right_logical(kpos, 6) <= lax.shift_right_logical(qpos, 6)
    return kpos <= qpos


def _attn_fwd(kind, q, k, v, f, seq, scale, tq=512, tk=512):
    t = v.shape[0]
    nb = t // seq
    nq = seq // tq
    hw = 256 if kind == "mla" else 128
    use_f = f is not None
    tq, tk = min(tq, seq), min(tk, seq)
    nq = seq // tq
    assert tk % tq == 0

    def body(*refs):
        if use_f:
            q_ref, k_ref, v_ref, f_ref, o_ref, st_ref = refs
        else:
            q_ref, k_ref, v_ref, o_ref, st_ref = refs
        qi = pl.program_id(2)
        q0 = qi * tq
        lane = lax.broadcasted_iota(jnp.int32, (1, 128), 1)
        half = lane >= 64
        qall = q_ref[...]
        if kind == "mla":
            qhs = [qall[:, 0:128], qall[:, 128:256]]
        else:
            qhs = [jnp.where(half, jnp.zeros_like(qall), qall), jnp.where(half, qall, jnp.zeros_like(qall))]
        nfull = q0 // tk
        kd = pl.multiple_of(nfull * tk, tk)
        diag = _causal_mask(kind, q0 - kd, 0, tq, tk)

        def block(j, k0, state, masked):
            m, l, acc = state
            kh = k_ref[pl.ds(k0, tk), j * 128:(j + 1) * 128] if kind == "mla" else k_ref[pl.ds(k0, tk), :]
            s = _nt(qhs[j], kh) * scale
            if use_f:
                s = s - f_ref[0, 0, j:j + 1, pl.ds(k0, tk)]
            if masked:
                s = jnp.where(diag, s, NEG)
            mn = jnp.maximum(m, jnp.max(s, axis=-1, keepdims=True))
            alpha = jnp.exp(m - mn)
            p = jnp.exp(s - mn)
            l = alpha * l + jnp.sum(p, axis=-1, keepdims=True)
            acc = alpha * acc + jnp.dot(p.astype(BF16), v_ref[pl.ds(k0, tk), :], preferred_element_type=F32)
            return mn, l, acc

        def kstep(kb, carry):
            k0 = pl.multiple_of(kb * tk, tk)
            return block(0, k0, carry[:3], False) + block(1, k0, carry[3:], False)

        init = (jnp.full((tq, 1), NEG, F32), jnp.zeros((tq, 1), F32), jnp.zeros((tq, 128), F32)) * 2
        carry = lax.fori_loop(0, nfull, kstep, init)
        m0, l0, a0 = block(0, kd, carry[:3], True)
        m1, l1, a1 = block(1, kd, carry[3:], True)
        o_ref[...] = jnp.where(half, a1 / l1, a0 / l0)
        st_ref[...] = jnp.where(lane == 0, m0 + jnp.log(l0), jnp.where(lane == 1, m1 + jnp.log(l1), 0.0))

    in_specs = [pl.BlockSpec((tq, hw), lambda b, p, i: (b * nq + i, p)),
                pl.BlockSpec((seq, hw), lambda b, p, i: (b, p)),
                pl.BlockSpec((seq, 128), lambda b, p, i: (b, p))]
    args = [q, k, v]
    if use_f:
        in_specs.append(pl.BlockSpec((1, 1, 8, seq), lambda b, p, i: (b, p, 0, 0)))
        args.append(f)
    oblk = pl.BlockSpec((tq, 128), lambda b, p, i: (b * nq + i, p))
    return pl.pallas_call(
        body, name="attn_fwd_" + kind, grid=(nb, 3, nq), in_specs=in_specs, out_specs=[oblk, oblk],
        out_shape=[jax.ShapeDtypeStruct((t, GW), F32), jax.ShapeDtypeStruct((t, GW), F32)],
        compiler_params=_cparams(("arbitrary", "arbitrary", "arbitrary")),
    )(*args)


def _attn_bwd(kind, q, k, v, f, o, st, do, seq, scale, tq=512, tk=512):
    t = v.shape[0]
    nb = t // seq
    tq, tk = min(tq, seq), min(tk, seq)
    nq = seq // tq
    nk = seq // tk
    hw = 256 if kind == "mla" else 128
    use_f = f is not None
    assert tq == tk

    def body(*refs):
        if use_f:
            q_ref, k_ref, v_ref, f_ref, o_ref, st_ref, do_ref, dq_ref, dk_ref, dv_ref, df_ref, dfq_ref = refs
        else:
            q_ref, k_ref, v_ref, o_ref, st_ref, do_ref, dq_ref, dk_ref, dv_ref = refs
        kj = pl.program_id(2)
        k0 = kj * tk
        lane = lax.broadcasted_iota(jnp.int32, (1, 128), 1)
        half = lane >= 64

        @pl.when(kj == 0)
        def _():
            dq_ref[...] = jnp.zeros_like(dq_ref)
            if use_f:
                dfq_ref[...] = jnp.zeros_like(dfq_ref)

        dk_ref[...] = jnp.zeros_like(dk_ref)
        dv_ref[...] = jnp.zeros_like(dv_ref)
        if use_f:
            df_ref[...] = jnp.zeros_like(df_ref)
        vv = v_ref[...]
        diag = _causal_mask(kind, 0, 0, tq, tk)

        def qstep(qi, masked):
            q0 = pl.multiple_of(qi * tq, tq)
            rows = pl.ds(q0, tq)
            dov = do_ref[rows, :]
            dd = dov * o_ref[rows, :]
            stv = st_ref[rows, :]
            for j in range(2):
                hm = half == bool(j)
                delta = jnp.sum(jnp.where(hm, dd, 0.0), axis=-1, keepdims=True)
                lse = stv[:, j:j + 1]
                if kind == "mla":
                    cols = slice(j * 128, (j + 1) * 128)
                    qh = q_ref[rows, cols]
                    kh = k_ref[:, cols]
                else:
                    cols = slice(0, 128)
                    qa = q_ref[rows, :]
                    qh = jnp.where(hm, qa, jnp.zeros_like(qa))
                    kh = k_ref[...]
                s = _nt(qh, kh) * scale
                if use_f:
                    s = s - f_ref[0, 0, j:j + 1, :]
                if masked:
                    s = jnp.where(diag, s, NEG)
                p = jnp.exp(s - lse)
                doh = jnp.where(hm, dov, 0.0).astype(BF16)
                ds = p * (_nt(doh, vv) - delta)
                dsb = (ds * scale).astype(BF16)
                dv_ref[...] += _tn(p.astype(BF16), doh)
                dk_ref[:, cols] += _tn(dsb, qh)
                dqc = jnp.dot(dsb, kh, preferred_element_type=F32)
                if kind != "mla":
                    dqc = jnp.where(hm, dqc, 0.0)
                dq_ref[rows, cols] += dqc
                if use_f:
                    df_ref[0, 0, j:j + 1, :] += -jnp.sum(ds, axis=0, keepdims=True)
                    dfq_ref[rows, :] += jnp.where(lane == j, jnp.sum(ds, axis=-1, keepdims=True), 0.0)

        qstep(kj, True)

        def rest(qi, carry):
            qstep(qi, False)
            return carry

        lax.fori_loop(kj + 1, nq, rest, 0)

    full_q = lambda wd: pl.BlockSpec((seq, wd), lambda b, p, i: (b, p))
    kblk = lambda wd: pl.BlockSpec((tk, wd), lambda b, p, i: (b * nk + i, p))
    in_specs = [full_q(hw), kblk(hw), kblk(128)]
    args = [q, k, v]
    if use_f:
        in_specs.append(pl.BlockSpec((1, 1, 8, tk), lambda b, p, i: (b, p, 0, i)))
        args.append(f)
    in_specs += [full_q(128), full_q(128), full_q(128)]
    args += [o, st, do]
    out_specs = [full_q(hw), kblk(hw), kblk(128)]
    out_shape = [jax.ShapeDtypeStruct((t, 3 * hw), F32), jax.ShapeDtypeStruct((t, 3 * hw), F32),
                 jax.ShapeDtypeStruct((t, GW), F32)]
    if use_f:
        out_specs += [pl.BlockSpec((1, 1, 8, tk), lambda b, p, i: (b, p, 0, i)), full_q(128)]
        out_shape += [jax.ShapeDtypeStruct((nb, 3, 8, seq), F32), jax.ShapeDtypeStruct((t, GW), F32)]
    return pl.pallas_call(
        body, name="attn_bwd_" + kind, grid=(nb, 3, nk), in_specs=in_specs, out_specs=out_specs,
        out_shape=out_shape, compiler_params=_cparams(("arbitrary", "arbitrary", "arbitrary")),
    )(*args)


BQ = 256
BWIN = BQ + B_LEFT


def _band_geometry():
    r = lax.broadcasted_iota(jnp.int32, (BQ, BWIN), 0)
    j = lax.broadcasted_iota(jnp.int32, (BQ, BWIN), 1)
    rc = lax.shift_right_logical(r, 6)
    jc = lax.shift_right_logical(j, 6)
    allowed = (jc - 8 <= rc) & (rc <= jc)
    return (r + B_LEFT - j) >= REL_CLIP, allowed, j < r


def _band_onehot(transposed, offset=0):
    shape = (BWIN, GW) if transposed else (GW, BWIN)
    kk = lax.broadcasted_iota(jnp.int32, shape, 1 if transposed else 0)
    x = lax.broadcasted_iota(jnp.int32, shape, 0 if transposed else 1) - offset
    x = jnp.where(x < 0, x + BWIN, x)
    return (kk == jnp.clip(B_LEFT - x, -REL_CLIP, REL_CLIP) + REL_CLIP).astype(F32)


def _band_table(rel_bias8):
    def body(b_ref, o_ref):
        hh = pl.program_id(0)
        u8 = jnp.dot(b_ref[...], _band_onehot(False), precision=HI, preferred_element_type=F32)
        rid = lax.broadcasted_iota(jnp.int32, (8, BWIN), 0)
        row = jnp.sum(jnp.where(rid == hh, u8, 0.0), axis=0, keepdims=True)
        far, allowed, _ = _band_geometry()
        tbl = pltpu.roll(jnp.broadcast_to(row, (BQ, BWIN)), 0, 1, stride=1, stride_axis=0)
        tbl = jnp.where(far, row[:, 0:1], tbl)
        o_ref[0] = jnp.where(allowed, tbl, NEG)

    return pl.pallas_call(
        body, name="band_table", grid=(6,),
        in_specs=[pl.BlockSpec((8, GW), lambda h: (0, 0))],
        out_specs=pl.BlockSpec((1, BQ, BWIN), lambda h: (h, 0, 0)),
        out_shape=jax.ShapeDtypeStruct((6, BQ, BWIN), F32),
        compiler_params=_cparams(("arbitrary",)),
    )(rel_bias8)


def _band_table_bwd(gtab):
    def body(g_ref, o_ref):
        gv = g_ref[0]
        _, _, wrapped = _band_geometry()
        gfar = jnp.sum(jnp.sum(jnp.where(wrapped, gv, 0.0), axis=-1, keepdims=True), axis=0, keepdims=True)
        anti = (lax.broadcasted_iota(jnp.int32, (BQ, BQ), 0) + lax.broadcasted_iota(jnp.int32, (BQ, BQ), 1)
                == BQ - 1).astype(F32)
        grev = jnp.dot(anti, jnp.where(wrapped, 0.0, gv), precision=HI, preferred_element_type=F32)
        near = pltpu.roll(grev, 0, 1, stride=1, stride_axis=0)
        y = jnp.broadcast_to(jnp.sum(near, axis=0, keepdims=True), (8, BWIN))
        gb = jnp.dot(y, _band_onehot(True, BQ - 1), precision=HI, preferred_element_type=F32)
        lane = lax.broadcasted_iota(jnp.int32, (8, GW), 1)
        o_ref[0] = gb + jnp.where(lane == 2 * REL_CLIP, gfar, 0.0)

    return pl.pallas_call(
        body, name="band_table_bwd", grid=(B_HEADS,),
        in_specs=[pl.BlockSpec((1, BQ, BWIN), lambda h: (h, 0, 0))],
        out_specs=pl.BlockSpec((1, 8, GW), lambda h: (h, 0, 0)),
        out_shape=jax.ShapeDtypeStruct((B_HEADS, 8, GW), F32),
        compiler_params=_cparams(("arbitrary",)),
    )(gtab)


def _band_fwd(q, k, v, table, seq, scale):
    t = q.shape[0]
    nb = t // seq
    nq = seq // BQ

    def body(q_ref, k_ref, v_ref, tb_ref, o_ref, st_ref, kpad, vpad):
        qi = pl.program_id(2)
        q0 = pl.multiple_of(qi * BQ, BQ)
        lane = lax.broadcasted_iota(jnp.int32, (1, 128), 1)
        half = lane >= 64

        @pl.when(qi == 0)
        def _():
            kpad[0:B_LEFT, :] = jnp.zeros((B_LEFT, 128), BF16)
            vpad[0:B_LEFT, :] = jnp.zeros((B_LEFT, 128), BF16)
            kpad[B_LEFT:, :] = k_ref[...]
            vpad[B_LEFT:, :] = v_ref[...]

        kw = kpad[pl.ds(q0, BWIN), :]
        vw = vpad[pl.ds(q0, BWIN), :]
        inside = lax.broadcasted_iota(jnp.int32, (BQ, BWIN), 1) >= B_LEFT - q0
        qall = q_ref[...]
        outs, lses = [], []
        for j in range(2):
            qh = jnp.where(half == bool(j), qall, jnp.zeros_like(qall))
            s = jnp.where(inside, _nt(qh, kw) * scale + tb_ref[j], NEG)
            m = jnp.max(s, axis=-1, keepdims=True)
            p = jnp.exp(s - m)
            l = jnp.sum(p, axis=-1, keepdims=True)
            outs.append(jnp.dot(p.astype(BF16), vw, preferred_element_type=F32) / l)
            lses.append(m + jnp.log(l))
        o_ref[...] = jnp.where(half, outs[1], outs[0])
        st_ref[...] = jnp.where(lane == 0, lses[0], jnp.where(lane == 1, lses[1], 0.0))

    qblk = pl.BlockSpec((BQ, 128), lambda b, p, i: (b * nq + i, p))
    full = pl.BlockSpec((seq, 128), lambda b, p, i: (b, p))
    return pl.pallas_call(
        body, name="band_fwd", grid=(nb, 3, nq),
        in_specs=[qblk, full, full, pl.BlockSpec((2, BQ, BWIN), lambda b, p, i: (p, 0, 0))],
        out_specs=[qblk, qblk],
        out_shape=[jax.ShapeDtypeStruct((t, GW), F32), jax.ShapeDtypeStruct((t, GW), F32)],
        scratch_shapes=[pltpu.VMEM((seq + B_LEFT, 128), BF16), pltpu.VMEM((seq + B_LEFT, 128), BF16)],
        compiler_params=_cparams(("arbitrary", "arbitrary", "arbitrary")),
    )(q, k, v, table)


def _band_bwd(q, k, v, table, o, st, do, seq, scale):
    t = q.shape[0]
    nb = t // seq
    nq = seq // BQ

    def body(q_ref, k_ref, v_ref, tb_ref, o_ref, st_ref, do_ref, dq_ref, dk_ref, dv_ref, g_ref,
             kpad, vpad, dkpad, dvpad):
        b = pl.program_id(1)
        qi = pl.program_id(2)
        q0 = pl.multiple_of(qi * BQ, BQ)
        lane = lax.broadcasted_iota(jnp.int32, (1, 128), 1)
        half = lane >= 64

        @pl.when(qi == 0)
        def _():
            kpad[0:B_LEFT, :] = jnp.zeros((B_LEFT, 128), BF16)
            vpad[0:B_LEFT, :] = jnp.zeros((B_LEFT, 128), BF16)
            kpad[B_LEFT:, :] = k_ref[...]
            vpad[B_LEFT:, :] = v_ref[...]
            dkpad[...] = jnp.zeros_like(dkpad)
            dvpad[...] = jnp.zeros_like(dvpad)

        @pl.when((qi == 0) & (b == 0))
        def _():
            g_ref[...] = jnp.zeros_like(g_ref)

        win = pl.ds(q0, BWIN)
        kw = kpad[win, :]
        vw = vpad[win, :]
        inside = lax.broadcasted_iota(jnp.int32, (BQ, BWIN), 1) >= B_LEFT - q0
        qall = q_ref[...]
        dov = do_ref[...]
        dd = dov * o_ref[...]
        stv = st_ref[...]
        dq = jnp.zeros((BQ, 128), F32)
        for j in range(2):
            hm = half == bool(j)
            qh = jnp.where(hm, qall, jnp.zeros_like(qall))
            delta = jnp.sum(jnp.where(hm, dd, 0.0), axis=-1, keepdims=True)
            s = jnp.where(inside, _nt(qh, kw) * scale + tb_ref[j], NEG)
            p = jnp.exp(s - stv[:, j:j + 1])
            doh = jnp.where(hm, dov, 0.0).astype(BF16)
            ds = p * (_nt(doh, vw) - delta)
            g_ref[j] += ds
            dsb = (ds * scale).astype(BF16)
            dvpad[win, :] += _tn(p.astype(BF16), doh)
            dkpad[win, :] += _tn(dsb, qh)
            dq = dq + jnp.where(hm, jnp.dot(dsb, kw, preferred_element_type=F32), 0.0)
        dq_ref[...] = dq

        @pl.when(qi == nq - 1)
        def _():
            dk_ref[...] = dkpad[B_LEFT:, :]
            dv_ref[...] = dvpad[B_LEFT:, :]

    qblk = pl.BlockSpec((BQ, 128), lambda p, b, i: (b * nq + i, p))
    full = pl.BlockSpec((seq, 128), lambda p, b, i: (b, p))
    tblk = pl.BlockSpec((2, BQ, BWIN), lambda p, b, i: (p, 0, 0))
    return pl.pallas_call(
        body, name="band_bwd", grid=(3, nb, nq),
        in_specs=[qblk, full, full, tblk, qblk, qblk, qblk],
        out_specs=[qblk, full, full, tblk],
        out_shape=[jax.ShapeDtypeStruct((t, GW), F32), jax.ShapeDtypeStruct((t, GW), F32),
                   jax.ShapeDtypeStruct((t, GW), F32), jax.ShapeDtypeStruct((6, BQ, BWIN), F32)],
        scratch_shapes=[pltpu.VMEM((seq + B_LEFT, 128), BF16), pltpu.VMEM((seq + B_LEFT, 128), BF16),
                        pltpu.VMEM((seq + B_LEFT, 128), F32), pltpu.VMEM((seq + B_LEFT, 128), F32)],
        compiler_params=_cparams(("arbitrary", "arbitrary", "arbitrary")),
    )(q, k, v, table, o, st, do)


def _fox_prep(cf, fb, seq):
    nb = cf.shape[0] // seq
    nblk = seq // 128

    def body(cf_ref, fb_ref, f_ref):
        x = cf_ref[...] + fb_ref[...]
        lf = jnp.minimum(x, 0.0) - jnp.log1p(jnp.exp(-jnp.abs(x)))
        rows = lf.T[0:8, :]
        upper = (lax.broadcasted_iota(jnp.int32, (128, 128), 0)
                 <= lax.broadcasted_iota(jnp.int32, (128, 128), 1)).astype(F32)
        carry = jnp.zeros((8, 1), F32)
        for blk in range(nblk):
            sl = slice(blk * 128, (blk + 1) * 128)
            cs = jnp.dot(rows[:, sl], upper, precision=HI, preferred_element_type=F32) + carry
            carry = cs[:, 127:128]
            f_ref[0, 0, :, sl] = cs
            f_ref[0, 1, :, sl] = pltpu.roll(cs, 6, 0)
            f_ref[0, 2, :, sl] = pltpu.roll(cs, 4, 0)

    return pl.pallas_call(
        body, name="fox_prep", grid=(nb,),
        in_specs=[pl.BlockSpec((seq, 128), lambda b: (b, 0)), pl.BlockSpec((1, 128), lambda b: (0, 0))],
        out_specs=pl.BlockSpec((1, 3, 8, seq), lambda b: (b, 0, 0, 0)),
        out_shape=jax.ShapeDtypeStruct((nb, 3, 8, seq), F32),
        compiler_params=_cparams(("arbitrary",)),
    )(cf, fb)


def _fox_prep_bwd(df, dfq, cf, fb, seq):
    nb = cf.shape[0] // seq
    nblk = seq // 128

    def body(df_ref, dfq_ref, cf_ref, fb_ref, dcf_ref, dfb_ref, wide):
        b = pl.program_id(0)
        row = lax.broadcasted_iota(jnp.int32, (8, seq), 0)
        dfh = None
        for p in range(3):
            both = df_ref[0, p] + dfq_ref[:, p * 128:(p + 1) * 128].T[0:8, :]
            both = jnp.where(row < 2, both, 0.0)
            if p:
                both = pltpu.roll(both, 2 * p, 0)
            dfh = both if dfh is None else dfh + both
        lower = (lax.broadcasted_iota(jnp.int32, (128, 128), 0)
                 >= lax.broadcasted_iota(jnp.int32, (128, 128), 1)).astype(F32)
        wide[...] = jnp.zeros_like(wide)
        carry = jnp.zeros((8, 1), F32)
        for blk in reversed(range(nblk)):
            sl = slice(blk * 128, (blk + 1) * 128)
            rc = jnp.dot(dfh[:, sl], lower, precision=HI, preferred_element_type=F32) + carry
            carry = rc[:, 0:1]
            wide[0:8, sl] = rc
        dl = wide[...].T
        x = cf_ref[...] + fb_ref[...]
        dcf = dl * (1.0 / (1.0 + jnp.exp(x)))
        dcf_ref[...] = dcf
        part = jnp.sum(dcf, axis=0, keepdims=True)

        @pl.when(b == 0)
        def _():
            dfb_ref[...] = part

        @pl.when(b != 0)
        def _():
            dfb_ref[...] += part

    return pl.pallas_call(
        body, name="fox_prep_bwd", grid=(nb,),
        in_specs=[pl.BlockSpec((1, 3, 8, seq), lambda b: (b, 0, 0, 0)), pl.BlockSpec((seq, GW), lambda b: (b, 0)),
                  pl.BlockSpec((seq, 128), lambda b: (b, 0)), pl.BlockSpec((1, 128), lambda b: (0, 0))],
        out_specs=[pl.BlockSpec((seq, 128), lambda b: (b, 0)), pl.BlockSpec((1, 128), lambda b: (0, 0))],
        out_shape=[jax.ShapeDtypeStruct(cf.shape, F32), jax.ShapeDtypeStruct((1, 128), F32)],
        scratch_shapes=[pltpu.VMEM((128, seq), F32)],
        compiler_params=_cparams(("arbitrary",)),
    )(df, dfq, cf, fb)


def _gate_out(oa, ob, oc, gates, w, x, gate, seq, tm=256):
    t = x.shape[0]
    tps = seq // tm

    def body(oa_ref, ob_ref, oc_ref, g_ref, w_ref, x_ref, gt_ref, xo_ref, y_ref, u_ref):
        for n, o_ref in enumerate((oa_ref, ob_ref, oc_ref)):
            sl = slice(n * GW, (n + 1) * GW)
            gv = g_ref[:, sl]
            u_ref[:, sl] = (o_ref[...] * (gv * _sigmoid(gv))).astype(BF16)
        y = jnp.dot(u_ref[...], w_ref[...], preferred_element_type=F32)
        y_ref[...] = y
        xo_ref[...] = x_ref[...] + gt_ref[0] * y

    row = lambda wd: pl.BlockSpec((tm, wd), lambda i: (i, 0))
    return pl.pallas_call(
        body, name="gate_out", grid=(t // tm,),
        in_specs=[row(GW), row(GW), row(GW), row(U_PAD), pl.BlockSpec((U_PAD, D_MODEL), lambda i: (0, 0)),
                  row(D_MODEL), pl.BlockSpec((1, 1, D_MODEL), lambda i: (i // tps, 0, 0))],
        out_specs=[row(D_MODEL), row(D_MODEL), row(U_PAD)],
        out_shape=[jax.ShapeDtypeStruct((t, D_MODEL), F32), jax.ShapeDtypeStruct((t, D_MODEL), F32),
                   jax.ShapeDtypeStruct((t, U_PAD), BF16)],
        compiler_params=_cparams(("arbitrary",)),
    )(oa, ob, oc, gates, w, x, gate)


def _gate_out_bwd(dxo, y, gate, oa, ob, oc, gates, w_t, seq, tm=256):
    t = dxo.shape[0]
    tps = seq // tm
    nb = t // seq

    def body(dxo_ref, y_ref, gt_ref, oa_ref, ob_ref, oc_ref, g_ref, wt_ref,
             dy_ref, doa_ref, dob_ref, doc_ref, dg_ref, dgt_ref):
        i = pl.program_id(0)
        dxo_v = dxo_ref[...]
        dgt = jnp.sum(dxo_v * y_ref[...], axis=0, keepdims=True)
        dyb = (dxo_v * gt_ref[0]).astype(BF16)
        dy_ref[...] = dyb
        du = jnp.dot(dyb, wt_ref[...], preferred_element_type=F32)
        for n, (o_ref, do_ref) in enumerate(((oa_ref, doa_ref), (ob_ref, dob_ref), (oc_ref, doc_ref))):
            sl = slice(n * GW, (n + 1) * GW)
            gv = g_ref[:, sl]
            sg = _sigmoid(gv)
            dun = du[:, sl]
            do_ref[...] = dun * (gv * sg)
            dg_ref[:, sl] = dun * o_ref[...] * (sg * (1.0 + gv * (1.0 - sg)))

        @pl.when(i % tps == 0)
        def _():
            dgt_ref[0] = dgt

        @pl.when(i % tps != 0)
        def _():
            dgt_ref[0] += dgt

    row = lambda wd: pl.BlockSpec((tm, wd), lambda i: (i, 0))
    per_b = pl.BlockSpec((1, 1, D_MODEL), lambda i: (i // tps, 0, 0))
    return pl.pallas_call(
        body, name="gate_out_bwd", grid=(t // tm,),
        in_specs=[row(D_MODEL), row(D_MODEL), per_b, row(GW), row(GW), row(GW), row(U_PAD),
                  pl.BlockSpec((D_MODEL, U_PAD), lambda i: (0, 0))],
        out_specs=[row(D_MODEL), row(GW), row(GW), row(GW), row(U_PAD), per_b],
        out_shape=[jax.ShapeDtypeStruct((t, D_MODEL), BF16), jax.ShapeDtypeStruct((t, GW), F32),
                   jax.ShapeDtypeStruct((t, GW), F32), jax.ShapeDtypeStruct((t, GW), F32),
                   jax.ShapeDtypeStruct((t, U_PAD), F32), jax.ShapeDtypeStruct((nb, 1, D_MODEL), F32)],
        compiler_params=_cparams(("arbitrary",)),
    )(dxo, y, gate, oa, ob, oc, gates, w_t)


def _final_loss(x, target, g, tm=256):
    t = x.shape[0]

    def body(x_ref, t_ref, g_ref, dx_ref, loss_ref, dg_ref):
        i = pl.program_id(0)
        xv = x_ref[...]
        rstd = lax.rsqrt(jnp.mean(xv * xv, axis=-1, keepdims=True) + EPS)
        xn = xv * rstd
        gv = g_ref[...]
        err = xn * gv - t_ref[...]
        dy = err * (1.0 / D_MODEL)
        dxn = dy * gv
        dx_ref[...] = rstd * (dxn - xn * jnp.mean(dxn * xn, axis=-1, keepdims=True))
        lp = jnp.sum(err * err, axis=0, keepdims=True) * (0.5 / D_MODEL)
        dgp = jnp.sum(dy * xn, axis=0, keepdims=True)

        @pl.when(i == 0)
        def _():
            loss_ref[...] = lp
            dg_ref[...] = dgp

        @pl.when(i != 0)
        def _():
            loss_ref[...] += lp
            dg_ref[...] += dgp

    row = pl.BlockSpec((tm, D_MODEL), lambda i: (i, 0))
    vec = pl.BlockSpec((1, D_MODEL), lambda i: (0, 0))
    return pl.pallas_call(
        body, name="final_loss", grid=(t // tm,),
        in_specs=[row, row, vec], out_specs=[row, vec, vec],
        out_shape=[jax.ShapeDtypeStruct((t, D_MODEL), F32), jax.ShapeDtypeStruct((1, D_MODEL), F32),
                   jax.ShapeDtypeStruct((1, D_MODEL), F32)],
        compiler_params=_cparams(("arbitrary",)),
    )(x, target, g)


def _adamw(w, gslots, m, v, name, tr=None):
    r, c = w.shape
    ns = gslots.shape[0]
    tr = r if tr is None else tr

    def body(w_ref, g_ref, m_ref, v_ref, go_ref, d_ref, mo_ref, vo_ref):
        g = g_ref[0].astype(F32)
        for j in range(1, ns):
            g = g + g_ref[j].astype(F32)
        mn = ADAM_B1 * m_ref[...] + (1.0 - ADAM_B1) * g
        vn = ADAM_B2 * v_ref[...] + (1.0 - ADAM_B2) * jnp.square(g)
        m_hat = mn / (1.0 - ADAM_B1 ** ADAM_STEP)
        v_hat = vn / (1.0 - ADAM_B2 ** ADAM_STEP)
        go_ref[...] = g
        d_ref[...] = -ADAM_LR * (m_hat / (jnp.sqrt(v_hat) + ADAM_EPS) + ADAM_WD * w_ref[...])
        mo_ref[...] = mn
        vo_ref[...] = vn

    blk = pl.BlockSpec((tr, c), lambda i: (i, 0))
    return pl.pallas_call(
        body, name=name, grid=(r // tr,),
        in_specs=[blk, pl.BlockSpec((ns, tr, c), lambda i: (0, i, 0)), blk, blk],
        out_specs=[blk] * 4, out_shape=[jax.ShapeDtypeStruct((r, c), F32)] * 4,
        compiler_params=_cparams(("arbitrary",)),
    )(w, gslots, m, v)


def _rope_tables(positions):
    inv = ROPE_THETA ** (-jnp.arange(0, A_ROPE, 2, dtype=F32) / A_ROPE)
    ang = positions.astype(F32)[:, None] * inv
    cos, sin = jnp.cos(ang), jnp.sin(ang)
    t = positions.shape[0]
    one = jnp.ones((t, 64), F32)
    zero16 = jnp.zeros((t, 16), F32)
    cos_t = jnp.concatenate([one, cos, cos, jnp.ones((t, 32), F32)], axis=1)
    sin_a = jnp.concatenate([jnp.zeros((t, 64), F32), -sin, zero16, jnp.zeros((t, 32), F32)], axis=1)
    sin_b = jnp.concatenate([jnp.zeros((t, 64), F32), zero16, sin, jnp.zeros((t, 32), F32)], axis=1)
    return cos_t, sin_a, sin_b


def _pad_heads(w, real, padded, nheads, axis):
    shp = w.shape[:axis] + (nheads, real) + w.shape[axis + 1:]
    w = w.reshape(shp)
    pad = [(0, 0)] * w.ndim
    pad[axis + 1] = (0, padded - real)
    w = jnp.pad(w, pad)
    return w.reshape(w.shape[:axis] + (nheads * padded,) + w.shape[axis + 2:])


def kernel(x, c, positions, w_ada, b_ada, norm_g, w_in, a_q_norm_g, a_w_uq, a_kv_norm_g, a_w_ukv, b_rel_bias, c_forget_b, w_out, final_g, loss_target, m_w_ada, m_b_ada, m_norm_g, m_w_in, m_a_q_norm_g, m_a_w_uq, m_a_kv_norm_g, m_a_w_ukv, m_b_rel_bias, m_c_forget_b, m_w_out, m_final_g, v_w_ada, v_b_ada, v_norm_g, v_w_in, v_a_q_norm_g, v_a_w_uq, v_a_kv_norm_g, v_a_w_ukv, v_b_rel_bias, v_c_forget_b, v_w_out, v_final_g):
    nb, seq, _ = x.shape
    t = nb * seq
    me = 4 * lax.axis_index("x") + 2 * lax.axis_index("y") + lax.axis_index("c")
    x2 = x.reshape(t, D_MODEL)
    tgt = loss_target.reshape(t, D_MODEL)
    cos_t, sin_a, sin_b = _rope_tables(positions.reshape(t))

    shards = []
    for l in range(DEPTH):
        shards += [_pad_runs(w_in[l].astype(BF16), IN_RUNS, N_PAD, 1), w_out[l].astype(BF16),
                   a_w_uq[l].astype(BF16), a_w_ukv[l].astype(BF16)]
    gathered = _gather(shards + [c], "gather_weights")
    c_all = gathered[-1].reshape(N_DEV * nb, D_MODEL)
    w_in_p, w_in_t, w_out_p, w_out_t, wuq_p, wuq_t, wk_p, wk_t, wv_p, wv_t = ([] for _ in range(10))
    for l in range(DEPTH):
        gi, go, gq, gkv = gathered[4 * l:4 * l + 4]
        wi = gi.reshape(D_MODEL, N_PAD)
        wo = _pad_runs(go.reshape(D_MODEL, D_MODEL), OUT_RUNS, U_PAD, 0)
        wq = jnp.transpose(gq, (1, 0, 2)).reshape(A_Q_RANK, A_HEADS * (A_NOPE + A_ROPE))
        wq = _pad_heads(wq, A_NOPE + A_ROPE, HEAD_PAD, A_HEADS, 1)
        wkv = jnp.transpose(gkv, (1, 0, 2)).reshape(A_KV_RANK, A_HEADS, 2 * A_NOPE)
        wk = jnp.pad(wkv[:, :, :A_NOPE], ((0, 0), (0, 0), (0, HEAD_PAD - A_NOPE))).reshape(A_KV_RANK, A_HEADS * HEAD_PAD)
        wv = wkv[:, :, A_NOPE:].reshape(A_KV_RANK, GW)
        w_in_p.append(wi); w_in_t.append(wi.T); w_out_p.append(wo); w_out_t.append(wo.T)
        wuq_p.append(wq); wuq_t.append(wq.T); wk_p.append(wk); wk_t.append(wk.T); wv_p.append(wv); wv_t.append(wv.T)

    c_act, mod_cols = _ada_fwd(c_all, w_ada)
    (mod_g,) = _gather([mod_cols], "gather_mod")
    mod_all = jnp.transpose(mod_g, (1, 2, 0, 3)).reshape(DEPTH, N_DEV * nb, 3 * D_MODEL)
    mod = lax.dynamic_slice_in_dim(mod_all, me * nb, nb, axis=1) + b_ada[:, None, :]

    fb_pad = jnp.pad(c_forget_b, ((0, 0), (0, 128 - C_HEADS)))
    a_scale = (A_NOPE + A_ROPE) ** -0.5
    h_scale = CHUNK ** -0.5

    saved = []
    xl = x2
    for l in range(DEPTH):
        shift, scale, gate = mod[l, :, :D_MODEL], mod[l, :, D_MODEL:2 * D_MODEL], mod[l, :, 2 * D_MODEL:]
        ss = jnp.stack([shift, 1.0 + scale], axis=1)
        gate3 = gate[:, None, :]
        h, cq, ckv, kpe, gates, bq, bk, bv, cq2, ck, cv, cf = _ln_in(xl, ss, norm_g[l:l + 1], w_in_p[l], seq)
        q, k, v, cqn, ckvn = _mla_prep(cq, ckv, kpe, a_q_norm_g[l:l + 1], a_kv_norm_g[l:l + 1],
                                       wuq_p[l], wk_p[l], wv_p[l], cos_t, sin_a, sin_b)
        oa, sta = _attn_fwd("mla", q, k, v, None, seq, a_scale)
        table = _band_table(jnp.pad(b_rel_bias[l], ((0, 8 - B_HEADS), (0, GW - N_REL))))
        ob, stb = _band_fwd(bq, bk, bv, table, seq, h_scale)
        fcum = _fox_prep(cf, fb_pad[l:l + 1], seq)
        oc, stc = _attn_fwd("fox", cq2, ck, cv, fcum, seq, h_scale)
        xn, y, u = _gate_out(oa, ob, oc, gates, w_out_p[l], xl, gate3, seq)
        saved.append(dict(x=xl, ss=ss, gate3=gate3, h=h, cq=cq, ckv=ckv, gates=gates, bq=bq, bk=bk, bv=bv,
                          cq2=cq2, ck=ck, cv=cv, cf=cf, q=q, k=k, v=v, cqn=cqn, ckvn=ckvn, oa=oa, sta=sta,
                          table=table, ob=ob, stb=stb, fcum=fcum, oc=oc, stc=stc, y=y, u=u))
        xl = xn

    dx, loss_lanes, g_final = _final_loss(xl, tgt, final_g[None, :])
    loss = lax.psum(jnp.sum(loss_lanes), AXES)

    g_in_a, g_in_b, g_out, g_uq, g_ukv, dmods, smalls = ([None] * DEPTH for _ in range(7))
    n_seg_a = 4
    for l in reversed(range(DEPTH)):
        s = saved[l]
        dy, doa, dob, doc, dgates, dgate = _gate_out_bwd(dx, s["y"], s["gate3"], s["oa"], s["ob"], s["oc"],
                                                         s["gates"], w_out_t[l], seq)
        g_out[l] = _unpad_runs(_matmul_tn(s["u"], dy, "dw_out"), OUT_RUNS, 0)
        dq, dk, dv = _attn_bwd("mla", s["q"], s["k"], s["v"], None, s["oa"], s["sta"], doa, seq, a_scale)
        dbq, dbk, dbv, gtab = _band_bwd(s["bq"], s["bk"], s["bv"], s["table"], s["ob"], s["stb"], dob, seq, h_scale)
        g_rel = _band_table_bwd(gtab)[:, 0, :N_REL]
        dcq2, dck, dcv, dfc, dfq = _attn_bwd("fox", s["cq2"], s["ck"], s["cv"], s["fcum"], s["oc"], s["stc"], doc,
                                             seq, h_scale)
        dcf, dfb = _fox_prep_bwd(dfc, dfq, s["cf"], fb_pad[l:l + 1], seq)
        dcq, dckv, dkpe, dqlin, dklin, dgq, dgkv = _mla_prep_bwd(
            dq, dk, dv, s["cq"], s["ckv"], a_q_norm_g[l:l + 1], a_kv_norm_g[l:l + 1],
            wuq_t[l], wk_t[l], wv_t[l], cos_t, sin_a, sin_b)
        gq_pad = _matmul_tn(s["cqn"], dqlin, "dw_uq")
        g_uq[l] = gq_pad.reshape(A_Q_RANK, A_HEADS, HEAD_PAD)[:, :, :A_NOPE + A_ROPE].reshape(A_Q_RANK, -1)
        gkv_pad = _matmul_tn(s["ckvn"], [dklin, dv], "dw_ukv")
        gk_pad = gkv_pad[:, :A_HEADS * HEAD_PAD].reshape(A_KV_RANK, A_HEADS, HEAD_PAD)[:, :, :A_NOPE]
        gv_pad = gkv_pad[:, A_HEADS * HEAD_PAD:].reshape(A_KV_RANK, A_HEADS, A_NOPE)
        g_ukv[l] = jnp.concatenate([gk_pad, gv_pad], axis=2).reshape(A_KV_RANK, -1)
        dz = [dcq, dckv, dkpe, dgates, dbq, dbk, dbv, dcq2, dck, dcv, dcf]
        g_in_a[l] = _matmul_tn(s["h"], dz[:n_seg_a], "dw_in_a")
        g_in_b[l] = _matmul_tn(s["h"], dz[n_seg_a:], "dw_in_b")
        dx, dss, dg_norm = _ln_in_bwd(dz, w_in_t[l], s["x"], s["ss"], norm_g[l:l + 1], dx, seq)
        dmods[l] = jnp.concatenate([dss[:, 0, :], dss[:, 1, :], dgate[:, 0, :]], axis=1)
        smalls[l] = [dg_norm.reshape(-1), dgq.reshape(-1), dgkv.reshape(-1), g_rel.reshape(-1),
                     dfb[0, :C_HEADS]]
    grad_x = dx.reshape(nb, seq, D_MODEL)

    small = jnp.concatenate([p for l in range(DEPTH) for p in smalls[l]] + [g_final.reshape(-1)])
    n_small = small.shape[0]
    small_rows = -(-n_small // 1024) * 8
    small = jnp.pad(small, (0, small_rows * 128 - n_small)).reshape(small_rows, 128)
    dmod_local = jnp.stack(dmods)
    dmod_g, small_g = _gather([dmod_local, small], "gather_small")
    dmod_all = jnp.transpose(dmod_g, (1, 0, 2, 3)).reshape(DEPTH, N_DEV * nb, 3 * D_MODEL)
    cols = 3 * D_MODEL // N_DEV
    dmod_mine = lax.dynamic_slice_in_dim(dmod_all, me * cols, cols, axis=2)
    g_w_ada, g_b_ada = _ada_bwd(c_act, dmod_all, dmod_mine)
    small_sum = _sum_slots(small_g, "sum_small").reshape(-1)

    rows = D_MODEL // N_DEV
    a2a_in, a2a_names = [], []
    for l in range(DEPTH):
        a2a_in += [g_in_a[l].reshape(N_DEV, rows, -1), g_in_b[l].reshape(N_DEV, rows, -1),
                   g_out[l].reshape(N_DEV, rows, D_MODEL)]
        a2a_names += ["in_a", "in_b", "out"]
    a2a_in += [jnp.stack(g_uq).reshape(DEPTH, A_Q_RANK, N_DEV, -1).transpose(2, 0, 1, 3).reshape(N_DEV, DEPTH * A_Q_RANK, -1),
               jnp.stack(g_ukv).reshape(DEPTH, A_KV_RANK, N_DEV, -1).transpose(2, 0, 1, 3).reshape(N_DEV, DEPTH * A_KV_RANK, -1)]
    a2a_names += ["uq", "ukv"]
    from_sib = _pair_swap(a2a_in, "grads_pair")
    core = lax.axis_index("c").astype(jnp.int32).reshape(1)
    chip_sums = [_pair_add(core, a, r, "grads_add_" + nm, r.shape[1]) for a, r, nm in zip(a2a_in, from_sib, a2a_names)]
    parts = _chip_a2a(chip_sums, "grads_chips")
    p_in = jnp.stack([_unpad_runs(jnp.concatenate(parts[3 * l:3 * l + 2], axis=2), IN_RUNS, 2) for l in range(DEPTH)], axis=1)
    p_out = jnp.stack([parts[3 * l + 2] for l in range(DEPTH)], axis=1)
    p_uq, p_ukv = parts[3 * DEPTH], parts[3 * DEPTH + 1]

    def split_small():
        out, pos = [], 0
        sizes = [D_MODEL, A_Q_RANK, A_KV_RANK, B_HEADS * N_REL, C_HEADS]
        per_layer = []
        for l in range(DEPTH):
            parts = []
            for sz in sizes:
                parts.append(small_sum[pos:pos + sz])
                pos += sz
            per_layer.append(parts)
        for j in range(len(sizes)):
            out.append(jnp.stack([per_layer[l][j] for l in range(DEPTH)]))
        out.append(small_sum[pos:pos + D_MODEL])
        return out

    g_norm, g_qn, g_kvn, g_relb, g_fb, g_fin = split_small()

    def adam(w, g, m, v, name, tr=None):
        shp = w.shape
        w2 = w.reshape(-1, shp[-1]) if w.ndim > 1 else w.reshape(1, -1)
        gs = g.reshape((-1,) + w2.shape) if g.size != w.size else g.reshape((1,) + w2.shape)
        outs = _adamw(w2, gs, m.reshape(w2.shape), v.reshape(w2.shape), name, tr)
        return [o.reshape(shp) for o in outs]

    res = {
        "w_ada": adam(w_ada, g_w_ada, m_w_ada, v_w_ada, "adam_w_ada", 256),
        "b_ada": adam(b_ada, g_b_ada, m_b_ada, v_b_ada, "adam_b_ada"),
        "norm_g": adam(norm_g, g_norm, m_norm_g, v_norm_g, "adam_norm_g"),
        "w_in": adam(w_in, p_in, m_w_in, v_w_in, "adam_w_in", 32),
        "a_q_norm_g": adam(a_q_norm_g, g_qn, m_a_q_norm_g, v_a_q_norm_g, "adam_q_norm"),
        "a_w_uq": adam(a_w_uq, p_uq, m_a_w_uq, v_a_w_uq, "adam_w_uq"),
        "a_kv_norm_g": adam(a_kv_norm_g, g_kvn, m_a_kv_norm_g, v_a_kv_norm_g, "adam_kv_norm"),
        "a_w_ukv": adam(a_w_ukv, p_ukv, m_a_w_ukv, v_a_w_ukv, "adam_w_ukv"),
        "b_rel_bias": adam(b_rel_bias, g_relb.reshape(b_rel_bias.shape), m_b_rel_bias, v_b_rel_bias, "adam_rel_bias"),
        "c_forget_b": adam(c_forget_b, g_fb, m_c_forget_b, v_c_forget_b, "adam_forget_b"),
        "w_out": adam(w_out, p_out, m_w_out, v_w_out, "adam_w_out", 64),
        "final_g": adam(final_g, g_fin, m_final_g, v_final_g, "adam_final_g"),
    }
    names = ["w_ada", "b_ada", "norm_g", "w_in", "a_q_norm_g", "a_w_uq", "a_kv_norm_g", "a_w_ukv", "b_rel_bias",
             "c_forget_b", "w_out", "final_g"]
    outs = [loss, grad_x]
    for j in range(4):
        outs += [res[n][j] for n in names]
    return tuple(outs)
```

```python
import functools

import jax
import jax.numpy as jnp
from jax import lax
from jax.experimental import pallas as pl
from jax.experimental.pallas import tpu as pltpu

F32 = jnp.float32
BF16 = jnp.bfloat16
HI = lax.Precision.HIGHEST

N_DEV = 8
AXES = ("x", "y", "c")
D_MODEL = 1024
DEPTH = 2
CHUNK = 64
EPS = 1e-6
NEG = -1e30
A_HEADS = 6
A_NOPE = 64
A_ROPE = 32
A_Q_RANK = 384
A_KV_RANK = 256
ROPE_THETA = 10000.0
B_HEADS = 5
B_LEFT = 512
REL_CLIP = 128
N_REL = 2 * REL_CLIP + 1
C_HEADS = 5
HEAD_PAD = 128
GW = 384
N_IN = 3621
ADAM_LR = 0.001
ADAM_B1 = 0.9
ADAM_B2 = 0.999
ADAM_EPS = 1e-08
ADAM_WD = 0.01
ADAM_STEP = 10
VMEM_LIMIT = 56 * 1024 * 1024

Z_SEGS = (
    ("cq", 0, 384, F32), ("ckv", 384, 256, F32), ("kpe", 640, 128, F32), ("gates", 768, 1152, F32),
    ("bq", 1920, 384, BF16), ("bk", 2304, 384, BF16), ("bv", 2688, 384, BF16),
    ("cq2", 3072, 384, BF16), ("ck", 3456, 384, BF16), ("cv", 3840, 384, BF16), ("cf", 4224, 128, F32),
)
N_PAD = 4352
IN_RUNS = (
    (0, 384, 0), (384, 256, 384), (640 + 64, 32, 640),
    (768, 384, 672), (768 + 384, 320, 2016), (768 + 768, 320, 3301),
    (1920, 320, 1056), (2304, 320, 1376), (2688, 320, 1696),
    (3072, 320, 2336), (3456, 320, 2656), (3840, 320, 2976), (4224, 5, 3296),
)
OUT_RUNS = ((0, 384, 0), (384, 320, 384), (768, 320, 704))
U_PAD = 1152


def _cparams(sem=None, vmem=VMEM_LIMIT):
    return pltpu.CompilerParams(dimension_semantics=sem, vmem_limit_bytes=vmem)


def _after(dep, body, in_specs, args):
    if dep is None:
        return body, in_specs, args
    n = len(args)

    def ordered(*refs):
        return body(*refs[:n], *refs[n + 1:])

    return ordered, list(in_specs) + [pl.BlockSpec((8, 128), lambda *_: (0, 0))], list(args) + [dep]


def _pad_runs(w, runs, total, axis):
    order = sorted(runs)
    parts, pos = [], 0
    for off, wd, src in order:
        if off > pos:
            shp = list(w.shape)
            shp[axis] = off - pos
            parts.append(jnp.zeros(shp, w.dtype))
        parts.append(lax.slice_in_dim(w, src, src + wd, axis=axis))
        pos = off + wd
    if pos < total:
        shp = list(w.shape)
        shp[axis] = total - pos
        parts.append(jnp.zeros(shp, w.dtype))
    return jnp.concatenate(parts, axis=axis)


def _unpad_runs(w, runs, axis):
    order = sorted(runs, key=lambda r: r[2])
    return jnp.concatenate([lax.slice_in_dim(w, off, off + wd, axis=axis) for off, wd, _ in order], axis=axis)


def _sigmoid(x):
    return 1.0 / (1.0 + jnp.exp(-x))


N_CHIP = 4
ANY_SPEC = pl.BlockSpec(memory_space=pl.ANY)
MESH_ID = pl.DeviceIdType.MESH


def _gather(arrs, name):
    n = len(arrs)

    def body(*refs):
        ins, outs = refs[:n], refs[n:2 * n]
        send_sems, recv_sems, local_sems = refs[2 * n:]
        x, y, c = lax.axis_index("x"), lax.axis_index("y"), lax.axis_index("c")
        me, sib = (x, y, c), (x, y, 1 - c)
        chips = [(1 - x, y), (x, 1 - y), (1 - x, 1 - y)]

        def slot(px, py, pc):
            return 4 * px + 2 * py + pc

        def copy(a, k, block, to, src=None):
            dst = outs[a].at[slot(*block)]
            return pltpu.make_async_remote_copy(
                src_ref=dst if src is None else src, dst_ref=dst, send_sem=send_sems.at[a, k],
                recv_sem=recv_sems.at[a, k], device_id=to, device_id_type=MESH_ID)

        local = [pltpu.make_async_copy(ins[a], outs[a].at[slot(*me)], local_sems.at[a]) for a in range(n)]
        first = []
        for a in range(n):
            first.append(copy(a, 0, me, sib, src=ins[a]))
            first += [copy(a, 1 + j, me, (*chip, c), src=ins[a]) for j, chip in enumerate(chips)]
        for cp in local + first:
            cp.start()
        passed = []
        for j, chip in enumerate(chips):
            for a in range(n):
                copy(a, 1 + j, (*chip, c), me).wait_recv()
                fwd = copy(a, 4 + j, (*chip, c), sib)
                fwd.start()
                passed.append(fwd)
        for a in range(n):
            copy(a, 0, sib, me).wait_recv()
            for j, chip in enumerate(chips):
                copy(a, 4 + j, (*chip, 1 - c), me).wait_recv()
        for cp in first + passed:
            cp.wait_send()
        for cp in local:
            cp.wait()

    return pl.pallas_call(
        body, name=name, out_shape=[jax.ShapeDtypeStruct((N_DEV,) + a.shape, a.dtype) for a in arrs],
        in_specs=[ANY_SPEC] * n, out_specs=[ANY_SPEC] * n,
        scratch_shapes=[pltpu.SemaphoreType.DMA((n, N_DEV - 1)), pltpu.SemaphoreType.DMA((n, N_DEV - 1)),
                        pltpu.SemaphoreType.DMA((n,))],
    )(*arrs)


def _pair_swap(arrs, name):
    n = len(arrs)

    def body(*refs):
        ins, outs = refs[:n], refs[n:2 * n]
        send_sems, recv_sems = refs[2 * n:]
        x, y, c = lax.axis_index("x"), lax.axis_index("y"), lax.axis_index("c")
        copies = []
        for a in range(n):
            for q in range(N_CHIP):
                cp = pltpu.make_async_remote_copy(
                    src_ref=ins[a].at[2 * q + 1 - c], dst_ref=outs[a].at[q], send_sem=send_sems.at[a, q],
                    recv_sem=recv_sems.at[a, q], device_id=(x, y, 1 - c), device_id_type=MESH_ID)
                cp.start()
                copies.append(cp)
        for cp in copies:
            cp.wait()

    return pl.pallas_call(
        body, name=name, out_shape=[jax.ShapeDtypeStruct((N_CHIP,) + a.shape[1:], a.dtype) for a in arrs],
        in_specs=[ANY_SPEC] * n, out_specs=[ANY_SPEC] * n,
        scratch_shapes=[pltpu.SemaphoreType.DMA((n, N_CHIP)), pltpu.SemaphoreType.DMA((n, N_CHIP))],
    )(*arrs)


def _chip_a2a(arrs, name):
    n = len(arrs)

    def body(*refs):
        ins, outs = refs[:n], refs[n:2 * n]
        send_sems, recv_sems, local_sems = refs[2 * n:]
        x, y, c = lax.axis_index("x"), lax.axis_index("y"), lax.axis_index("c")
        mine = 2 * x + y
        copies = []
        for a in range(n):
            loc = pltpu.make_async_copy(ins[a].at[mine], outs[a].at[mine], local_sems.at[a])
            loc.start()
            copies.append(loc)
            for k in range(1, N_CHIP):
                px = (1 - x) if (k >> 1) & 1 else x
                py = (1 - y) if k & 1 else y
                cp = pltpu.make_async_remote_copy(
                    src_ref=ins[a].at[2 * px + py], dst_ref=outs[a].at[mine], send_sem=send_sems.at[a, k - 1],
                    recv_sem=recv_sems.at[a, k - 1], device_id=(px, py, c), device_id_type=MESH_ID)
                cp.start()
                copies.append(cp)
        for cp in copies:
            cp.wait()

    return pl.pallas_call(
        body, name=name, out_shape=[jax.ShapeDtypeStruct(a.shape, a.dtype) for a in arrs],
        in_specs=[ANY_SPEC] * n, out_specs=[ANY_SPEC] * n,
        scratch_shapes=[pltpu.SemaphoreType.DMA((n, N_CHIP - 1)), pltpu.SemaphoreType.DMA((n, N_CHIP - 1)),
                        pltpu.SemaphoreType.DMA((n,))],
    )(*arrs)


HBM_SPEC = pl.BlockSpec(memory_space=pltpu.HBM)
SEM_SPEC = pl.BlockSpec(memory_space=pltpu.SEMAPHORE)
SPLIT_EFFECT = pltpu.SideEffectType.DATAFLOW_SIDE_EFFECTING
SPLIT_SEMS = {"gather": (N_DEV - 1, True), "pair": (N_CHIP, False), "chips": (N_CHIP - 1, True)}


def _split_descriptors(pattern, srcs, lands, sems):
    x, y, c = lax.axis_index("x"), lax.axis_index("y"), lax.axis_index("c")
    nsem, has_local = SPLIT_SEMS[pattern]
    per = 2 * nsem + int(has_local)
    starts, arrivals, local = [], [], []

    def remote(a, k, src, dst, to):
        return pltpu.make_async_remote_copy(src_ref=src, dst_ref=dst, send_sem=sems[a * per + k],
                                            recv_sem=sems[a * per + nsem + k], device_id=to, device_id_type=MESH_ID)

    for a in range(len(srcs)):
        if pattern == "gather":
            me = 4 * x + 2 * y + c
            local.append(pltpu.make_async_copy(srcs[a], lands[a].at[me], sems[a * per + 2 * nsem]))
            for k in range(1, N_DEV):
                px = (1 - x) if (k >> 2) & 1 else x
                py = (1 - y) if (k >> 1) & 1 else y
                pc = (1 - c) if k & 1 else c
                starts.append(remote(a, k - 1, srcs[a], lands[a].at[me], (px, py, pc)))
                arrivals.append(remote(a, k - 1, srcs[a], lands[a].at[4 * px + 2 * py + pc], (px, py, pc)))
        elif pattern == "pair":
            for q in range(N_CHIP):
                cp = remote(a, q, srcs[a].at[2 * q + 1 - c], lands[a].at[q], (x, y, 1 - c))
                starts.append(cp)
                arrivals.append(cp)
        else:
            mine = 2 * x + y
            local.append(pltpu.make_async_copy(srcs[a].at[mine], lands[a].at[mine], sems[a * per + 2 * nsem]))
            for k in range(1, N_CHIP):
                px = (1 - x) if (k >> 1) & 1 else x
                py = (1 - y) if k & 1 else y
                starts.append(remote(a, k - 1, srcs[a].at[2 * px + py], lands[a].at[mine], (px, py, c)))
                arrivals.append(remote(a, k - 1, srcs[a].at[2 * px + py], lands[a].at[2 * px + py], (px, py, c)))
    return starts, arrivals, local


def _split_start(pattern, arrs, name):
    n = len(arrs)
    nsem, has_local = SPLIT_SEMS[pattern]
    if pattern == "gather":
        land_shapes = [(N_DEV,) + a.shape for a in arrs]
    elif pattern == "pair":
        land_shapes = [(N_CHIP,) + a.shape[1:] for a in arrs]
    else:
        land_shapes = [a.shape for a in arrs]
    nsem_out = n * (2 * nsem + int(has_local))

    def body(*refs):
        srcs, lands = refs[:n], refs[n:2 * n]
        sems = refs[2 * n:2 * n + nsem_out]
        token = refs[-1]
        starts, _, local = _split_descriptors(pattern, srcs, lands, sems)
        for cp in local + starts:
            cp.start()
        token[...] = jnp.zeros_like(token)

    out_shape = ([pltpu.SemaphoreType.DMA(())] * nsem_out + [pltpu.HBM(a.shape, a.dtype) for a in arrs]
                 + [pltpu.HBM(s, a.dtype) for s, a in zip(land_shapes, arrs)] + [jax.ShapeDtypeStruct((8, 128), F32)])
    ins = ([pltpu.with_memory_space_constraint(a, pltpu.HBM) for a in arrs]
           + [pltpu.with_memory_space_constraint(lax.empty(s, a.dtype), pltpu.HBM) for s, a in zip(land_shapes, arrs)])
    outs = pl.pallas_call(
        body, name=name, out_shape=out_shape, in_specs=[HBM_SPEC] * (2 * n),
        out_specs=[SEM_SPEC] * nsem_out + [HBM_SPEC] * (2 * n) + [pl.BlockSpec(memory_space=pltpu.VMEM)],
        input_output_aliases={i: nsem_out + i for i in range(2 * n)},
        compiler_params=pltpu.CompilerParams(has_side_effects=SPLIT_EFFECT),
    )(*ins)
    handle = dict(pattern=pattern, n=n, sems=outs[:nsem_out], srcs=outs[nsem_out:nsem_out + n],
                  lands=outs[nsem_out + n:nsem_out + 2 * n])
    return handle, outs[-1]


def _split_wait(handle, after, name):
    pattern, n = handle["pattern"], handle["n"]
    nsem_in = len(handle["sems"])

    def body(*refs):
        srcs, lands = refs[:n], refs[n:2 * n]
        starts, arrivals, local = _split_descriptors(pattern, srcs, lands, refs[2 * n:2 * n + nsem_in])
        for cp in starts:
            cp.wait_send()
        for cp in arrivals:
            cp.wait_recv()
        for cp in local:
            cp.wait()

    srcs, lands = handle["srcs"], handle["lands"]
    outs = pl.pallas_call(
        body, name=name,
        out_shape=[pltpu.HBM(a.shape, a.dtype) for a in srcs] + [pltpu.HBM(a.shape, a.dtype) for a in lands],
        in_specs=[HBM_SPEC] * (2 * n) + [SEM_SPEC] * nsem_in + [ANY_SPEC], out_specs=[HBM_SPEC] * (2 * n),
        input_output_aliases={i: i for i in range(2 * n)},
        compiler_params=pltpu.CompilerParams(has_side_effects=SPLIT_EFFECT),
    )(*srcs, *lands, *handle["sems"], after)
    return outs[:n], outs[n:]


def _pair_add(core, a8, b4, name, tr):
    _, r, c = b4.shape

    def body(core_ref, a_ref, b_ref, o_ref):
        o_ref[...] = (a_ref[...] + b_ref[...]).astype(BF16)

    blk = pl.BlockSpec((1, tr, c), lambda q, i, core_ref: (q, i, 0))
    grid_spec = pltpu.PrefetchScalarGridSpec(
        num_scalar_prefetch=1, grid=(N_CHIP, r // tr),
        in_specs=[pl.BlockSpec((1, tr, c), lambda q, i, core_ref: (2 * q + core_ref[0], i, 0)), blk], out_specs=blk)
    return pl.pallas_call(
        body, name=name, grid_spec=grid_spec, out_shape=jax.ShapeDtypeStruct(b4.shape, BF16),
        compiler_params=_cparams(("arbitrary", "arbitrary")),
    )(core, a8, b4)


def _sum_slots(x, name):
    _, r, c = x.shape

    def body(x_ref, o_ref):
        acc = x_ref[0]
        for j in range(1, N_DEV):
            acc = acc + x_ref[j]
        o_ref[...] = acc

    return pl.pallas_call(body, name=name, out_shape=jax.ShapeDtypeStruct((r, c), F32))(x)


def _ada_fwd(c_all, w_ada):
    nb = c_all.shape[0]
    cols = w_ada.shape[2]

    def body(c_ref, w_ref, act_ref, mod_ref):
        cv = c_ref[...]
        act = cv * _sigmoid(cv)
        act_ref[...] = act
        for l in range(DEPTH):
            mod_ref[l] = jnp.dot(act, w_ref[l], precision=HI, preferred_element_type=F32)

    return pl.pallas_call(
        body, name="ada_fwd",
        out_shape=[jax.ShapeDtypeStruct((nb, D_MODEL), F32), jax.ShapeDtypeStruct((DEPTH, nb, cols), F32)],
        compiler_params=_cparams(),
    )(c_all, w_ada)


def _ada_bwd(c_act, dmod_all, dmod_mine):
    nb = c_act.shape[0]
    cols = dmod_mine.shape[2]

    def body(act_ref, dall_ref, dmine_ref, gw_ref, gb_ref):
        act = act_ref[...]
        for l in range(DEPTH):
            gw_ref[l] = lax.dot_general(act, dmine_ref[l], (((0,), (0,)), ((), ())),
                                        precision=HI, preferred_element_type=F32)
            gb_ref[l:l + 1, :] = jnp.sum(dall_ref[l], axis=0, keepdims=True)

    return pl.pallas_call(
        body, name="ada_bwd",
        out_shape=[jax.ShapeDtypeStruct((DEPTH, D_MODEL, cols), F32),
                   jax.ShapeDtypeStruct((DEPTH, 3 * D_MODEL), F32)],
        compiler_params=_cparams(),
    )(c_act, dmod_all, dmod_mine)


def _ln_in(x, ss, g, w, seq, tm=256, dep=None):
    t = x.shape[0]
    tps = seq // tm

    def body(x_ref, ss_ref, g_ref, w_ref, h_ref, *outs):
        xv = x_ref[...]
        xn = xv * lax.rsqrt(jnp.mean(xv * xv, axis=-1, keepdims=True) + EPS)
        h = xn * g_ref[...] * ss_ref[0, 1:2, :] + ss_ref[0, 0:1, :]
        hb = h.astype(BF16)
        h_ref[...] = hb
        z = jnp.dot(hb, w_ref[...], preferred_element_type=F32)
        for o_ref, (_, off, wd, _) in zip(outs, Z_SEGS):
            o_ref[...] = z[:, off:off + wd].astype(o_ref.dtype)

    row = lambda wd: pl.BlockSpec((tm, wd), lambda i: (i, 0))
    in_specs = [row(D_MODEL), pl.BlockSpec((1, 2, D_MODEL), lambda i: (i // tps, 0, 0)),
                pl.BlockSpec((1, D_MODEL), lambda i: (0, 0)), pl.BlockSpec((D_MODEL, N_PAD), lambda i: (0, 0))]
    body, in_specs, args = _after(dep, body, in_specs, [x, ss, g, w])
    return pl.pallas_call(
        body, name="ln_in", grid=(t // tm,), in_specs=in_specs,
        out_specs=[row(D_MODEL)] + [row(wd) for _, _, wd, _ in Z_SEGS],
        out_shape=[jax.ShapeDtypeStruct((t, D_MODEL), BF16)]
        + [jax.ShapeDtypeStruct((t, wd), dt) for _, _, wd, dt in Z_SEGS],
        compiler_params=_cparams(("arbitrary",)),
    )(*args)


def _ln_in_bwd(dz, w_t, x, ss, g, dxo, seq, tm=256):
    t = x.shape[0]
    tps = seq // tm
    nb = t // seq
    nz = len(Z_SEGS)

    def body(*refs):
        dz_refs = refs[:nz]
        wt_ref, x_ref, ss_ref, g_ref, dxo_ref, dx_ref, dss_ref, dg_ref = refs[nz:]
        i = pl.program_id(0)
        dzc = jnp.concatenate([r[...].astype(BF16) for r in dz_refs], axis=1)
        dh = jnp.dot(dzc, wt_ref[...], preferred_element_type=F32)
        xv = x_ref[...]
        rstd = lax.rsqrt(jnp.mean(xv * xv, axis=-1, keepdims=True) + EPS)
        xn = xv * rstd
        gv = g_ref[...]
        s1 = ss_ref[0, 1:2, :]
        dxg = dh * s1
        dxn = dxg * gv
        dx = rstd * (dxn - xn * jnp.mean(dxn * xn, axis=-1, keepdims=True))
        dx_ref[...] = dxo_ref[...] + dx
        dshift = jnp.sum(dh, axis=0, keepdims=True)
        dscale = jnp.sum(dh * (xn * gv), axis=0, keepdims=True)
        dgp = jnp.sum(dxg * xn, axis=0, keepdims=True)

        @pl.when(i % tps == 0)
        def _():
            dss_ref[0, 0:1, :] = dshift
            dss_ref[0, 1:2, :] = dscale

        @pl.when(i % tps != 0)
        def _():
            dss_ref[0, 0:1, :] += dshift
            dss_ref[0, 1:2, :] += dscale

        @pl.when(i == 0)
        def _():
            dg_ref[...] = dgp

        @pl.when(i != 0)
        def _():
            dg_ref[...] += dgp

    row = lambda wd: pl.BlockSpec((tm, wd), lambda i: (i, 0))
    return pl.pallas_call(
        body, name="ln_in_bwd", grid=(t // tm,),
        in_specs=[row(wd) for _, _, wd, _ in Z_SEGS]
        + [pl.BlockSpec((N_PAD, D_MODEL), lambda i: (0, 0)), row(D_MODEL),
           pl.BlockSpec((1, 2, D_MODEL), lambda i: (i // tps, 0, 0)),
           pl.BlockSpec((1, D_MODEL), lambda i: (0, 0)), row(D_MODEL)],
        out_specs=[row(D_MODEL), pl.BlockSpec((1, 2, D_MODEL), lambda i: (i // tps, 0, 0)),
                   pl.BlockSpec((1, D_MODEL), lambda i: (0, 0))],
        out_shape=[jax.ShapeDtypeStruct((t, D_MODEL), F32), jax.ShapeDtypeStruct((nb, 2, D_MODEL), F32),
                   jax.ShapeDtypeStruct((1, D_MODEL), F32)],
        compiler_params=_cparams(("arbitrary",)),
    )(*dz, w_t, x, ss, g, dxo)


def _matmul_tn(a, bs, name, tm=1024):
    bs = list(bs) if isinstance(bs, (list, tuple)) else [bs]
    t, k = a.shape
    widths = [b.shape[1] for b in bs]
    n = sum(widths)
    tm = min(tm, t)

    def body(a_ref, *refs):
        b_refs, o_ref = refs[:-1], refs[-1]
        i = pl.program_id(0)
        av = a_ref[...].astype(BF16)
        parts = [b_ref[...].astype(BF16) for b_ref in b_refs]
        bv = parts[0] if len(parts) == 1 else jnp.concatenate(parts, axis=1)
        part = lax.dot_general(av, bv, (((0,), (0,)), ((), ())), preferred_element_type=F32)

        @pl.when(i == 0)
        def _():
            o_ref[...] = part

        @pl.when(i != 0)
        def _():
            o_ref[...] += part

    return pl.pallas_call(
        body, name=name, grid=(t // tm,),
        in_specs=[pl.BlockSpec((tm, k), lambda i: (i, 0))] + [pl.BlockSpec((tm, wd), lambda i: (i, 0)) for wd in widths],
        out_specs=pl.BlockSpec((k, n), lambda i: (0, 0)),
        out_shape=jax.ShapeDtypeStruct((k, n), F32),
        compiler_params=_cparams(("arbitrary",)),
    )(a, *bs)


def _rope(blk, cos_t, sin_a, sin_b):
    return blk * cos_t + pltpu.roll(blk, 112, 1) * sin_a + pltpu.roll(blk, 16, 1) * sin_b


def _unrope(d, cos_t, sin_a, sin_b):
    return d * cos_t + pltpu.roll(d * sin_a, 16, 1) + pltpu.roll(d * sin_b, 112, 1)


def _mla_prep(cq, ckv, kpe, gq, gkv, wuq, wk, wv, cos_t, sin_a, sin_b, tm=256):
    t = cq.shape[0]
    qw = A_HEADS * HEAD_PAD

    def body(cq_ref, ckv_ref, kpe_ref, gq_ref, gkv_ref, wuq_ref, wk_ref, wv_ref, c_ref, sa_ref, sb_ref,
             q_ref, k_ref, v_ref, cqn_ref, ckvn_ref):
        ct, sa, sb = c_ref[...], sa_ref[...], sb_ref[...]
        a = cq_ref[...]
        cqn = (a * lax.rsqrt(jnp.mean(a * a, axis=-1, keepdims=True) + EPS) * gq_ref[...]).astype(BF16)
        cqn_ref[...] = cqn
        b = ckv_ref[...]
        ckvn = (b * lax.rsqrt(jnp.mean(b * b, axis=-1, keepdims=True) + EPS) * gkv_ref[...]).astype(BF16)
        ckvn_ref[...] = ckvn
        qlin = jnp.dot(cqn, wuq_ref[...], preferred_element_type=F32)
        klin = jnp.dot(ckvn, wk_ref[...], preferred_element_type=F32)
        v_ref[...] = jnp.dot(ckvn, wv_ref[...], preferred_element_type=F32).astype(BF16)
        kr = _rope(kpe_ref[...], ct, sa, sb)
        for h in range(A_HEADS):
            sl = slice(h * HEAD_PAD, (h + 1) * HEAD_PAD)
            q_ref[:, sl] = _rope(qlin[:, sl], ct, sa, sb).astype(BF16)
            k_ref[:, sl] = (klin[:, sl] + kr).astype(BF16)

    row = lambda wd: pl.BlockSpec((tm, wd), lambda i: (i, 0))
    full = lambda r, c: pl.BlockSpec((r, c), lambda i: (0, 0))
    return pl.pallas_call(
        body, name="mla_prep", grid=(t // tm,),
        in_specs=[row(A_Q_RANK), row(A_KV_RANK), row(128), full(1, A_Q_RANK), full(1, A_KV_RANK),
                  full(A_Q_RANK, qw), full(A_KV_RANK, qw), full(A_KV_RANK, GW), row(128), row(128), row(128)],
        out_specs=[row(qw), row(qw), row(GW), row(A_Q_RANK), row(A_KV_RANK)],
        out_shape=[jax.ShapeDtypeStruct((t, qw), BF16), jax.ShapeDtypeStruct((t, qw), BF16),
                   jax.ShapeDtypeStruct((t, GW), BF16), jax.ShapeDtypeStruct((t, A_Q_RANK), BF16),
                   jax.ShapeDtypeStruct((t, A_KV_RANK), BF16)],
        compiler_params=_cparams(("arbitrary",)),
    )(cq, ckv, kpe, gq, gkv, wuq, wk, wv, cos_t, sin_a, sin_b)


def _mla_prep_bwd(dq, dk, dv, cq, ckv, gq, gkv, wuq_t, wk_t, wv_t, cos_t, sin_a, sin_b, tm=256):
    t = cq.shape[0]
    qw = A_HEADS * HEAD_PAD

    def body(dq_ref, dk_ref, dv_ref, cq_ref, ckv_ref, gq_ref, gkv_ref, wuqt_ref, wkt_ref, wvt_ref,
             c_ref, sa_ref, sb_ref, dcq_ref, dckv_ref, dkpe_ref, dql_ref, dkl_ref, dgq_ref, dgkv_ref):
        i = pl.program_id(0)
        ct, sa, sb = c_ref[...], sa_ref[...], sb_ref[...]
        lane = lax.broadcasted_iota(jnp.int32, (1, HEAD_PAD), 1)
        nope = lane < A_NOPE
        rope = (lane >= A_NOPE) & (lane < A_NOPE + A_ROPE)
        dksum = None
        for h in range(A_HEADS):
            sl = slice(h * HEAD_PAD, (h + 1) * HEAD_PAD)
            dql_ref[:, sl] = _unrope(dq_ref[:, sl], ct, sa, sb).astype(BF16)
            dkh = dk_ref[:, sl]
            dkl_ref[:, sl] = jnp.where(nope, dkh, 0.0).astype(BF16)
            dksum = dkh if dksum is None else dksum + dkh
        dkpe_ref[...] = jnp.where(rope, _unrope(jnp.where(rope, dksum, 0.0), ct, sa, sb), 0.0)
        dcqn = jnp.dot(dql_ref[...], wuqt_ref[...], preferred_element_type=F32)
        dckvn = (jnp.dot(dkl_ref[...], wkt_ref[...], preferred_element_type=F32)
                 + jnp.dot(dv_ref[...].astype(BF16), wvt_ref[...], preferred_element_type=F32))

        def norm_bwd(xv, gv, dy):
            rstd = lax.rsqrt(jnp.mean(xv * xv, axis=-1, keepdims=True) + EPS)
            xn = xv * rstd
            dxn = dy * gv
            dx = rstd * (dxn - xn * jnp.mean(dxn * xn, axis=-1, keepdims=True))
            return dx, jnp.sum(dy * xn, axis=0, keepdims=True)

        dcq, dgq = norm_bwd(cq_ref[...], gq_ref[...], dcqn)
        dckv, dgkv = norm_bwd(ckv_ref[...], gkv_ref[...], dckvn)
        dcq_ref[...] = dcq
        dckv_ref[...] = dckv

        @pl.when(i == 0)
        def _():
            dgq_ref[...] = dgq
            dgkv_ref[...] = dgkv

        @pl.when(i != 0)
        def _():
            dgq_ref[...] += dgq
            dgkv_ref[...] += dgkv

    row = lambda wd: pl.BlockSpec((tm, wd), lambda i: (i, 0))
    full = lambda r, c: pl.BlockSpec((r, c), lambda i: (0, 0))
    return pl.pallas_call(
        body, name="mla_prep_bwd", grid=(t // tm,),
        in_specs=[row(qw), row(qw), row(GW), row(A_Q_RANK), row(A_KV_RANK), full(1, A_Q_RANK), full(1, A_KV_RANK),
                  full(qw, A_Q_RANK), full(qw, A_KV_RANK), full(GW, A_KV_RANK), row(128), row(128), row(128)],
        out_specs=[row(A_Q_RANK), row(A_KV_RANK), row(128), row(qw), row(qw), full(1, A_Q_RANK), full(1, A_KV_RANK)],
        out_shape=[jax.ShapeDtypeStruct((t, A_Q_RANK), F32), jax.ShapeDtypeStruct((t, A_KV_RANK), F32),
                   jax.ShapeDtypeStruct((t, 128), F32), jax.ShapeDtypeStruct((t, qw), BF16),
                   jax.ShapeDtypeStruct((t, qw), BF16), jax.ShapeDtypeStruct((1, A_Q_RANK), F32),
                   jax.ShapeDtypeStruct((1, A_KV_RANK), F32)],
        compiler_params=_cparams(("arbitrary",)),
    )(dq, dk, dv, cq, ckv, gq, gkv, wuq_t, wk_t, wv_t, cos_t, sin_a, sin_b)


def _nt(a, b):
    return lax.dot_general(a, b, (((1,), (1,)), ((), ())), preferred_element_type=F32)


def _tn(a, b):
    return lax.dot_general(a, b, (((0,), (0,)), ((), ())), preferred_element_type=F32)


def _causal_mask(kind, q0, k0, tq, tk):
    qpos = q0 + lax.broadcasted_iota(jnp.int32, (tq, tk), 0)
    kpos = k0 + lax.broadcasted_iota(jnp.int32, (tq, tk), 1)
    if kind == "mla":
        return lax.shift_right_logical(kpos, 6) <= lax.shift_right_logical(qpos, 6)
    return kpos <= qpos


def _attn_fwd(kind, q, k, v, f, seq, scale, tq=512, tk=512):
    t = v.shape[0]
    nb = t // seq
    nq = seq // tq
    hw = 256 if kind == "mla" else 128
    use_f = f is not None
    tq, tk = min(tq, seq), min(tk, seq)
    nq = seq // tq
    assert tk % tq == 0

    def body(*refs):
        if use_f:
            q_ref, k_ref, v_ref, f_ref, o_ref, st_ref = refs
        else:
            q_ref, k_ref, v_ref, o_ref, st_ref = refs
        qi = pl.program_id(2)
        q0 = qi * tq
        lane = lax.broadcasted_iota(jnp.int32, (1, 128), 1)
        half = lane >= 64
        qall = q_ref[...]
        if kind == "mla":
            qhs = [qall[:, 0:128], qall[:, 128:256]]
        else:
            qhs = [jnp.where(half, jnp.zeros_like(qall), qall), jnp.where(half, qall, jnp.zeros_like(qall))]
        nfull = q0 // tk
        kd = pl.multiple_of(nfull * tk, tk)
        diag = _causal_mask(kind, q0 - kd, 0, tq, tk)

        def block(j, k0, state, masked):
            m, l, acc = state
            kh = k_ref[pl.ds(k0, tk), j * 128:(j + 1) * 128] if kind == "mla" else k_ref[pl.ds(k0, tk), :]
            s = _nt(qhs[j], kh) * scale
            if use_f:
                s = s - f_ref[0, 0, j:j + 1, pl.ds(k0, tk)]
            if masked:
                s = jnp.where(diag, s, NEG)
            mn = jnp.maximum(m, jnp.max(s, axis=-1, keepdims=True))
            alpha = jnp.exp(m - mn)
            p = jnp.exp(s - mn)
            l = alpha * l + jnp.sum(p, axis=-1, keepdims=True)
            acc = alpha * acc + jnp.dot(p.astype(BF16), v_ref[pl.ds(k0, tk), :], preferred_element_type=F32)
            return mn, l, acc

        def kstep(kb, carry):
            k0 = pl.multiple_of(kb * tk, tk)
            return block(0, k0, carry[:3], False) + block(1, k0, carry[3:], False)

        init = (jnp.full((tq, 1), NEG, F32), jnp.zeros((tq, 1), F32), jnp.zeros((tq, 128), F32)) * 2
        carry = lax.fori_loop(0, nfull, kstep, init)
        m0, l0, a0 = block(0, kd, carry[:3], True)
        m1, l1, a1 = block(1, kd, carry[3:], True)
        o_ref[...] = jnp.where(half, a1 / l1, a0 / l0)
        st_ref[...] = jnp.where(lane == 0, m0 + jnp.log(l0), jnp.where(lane == 1, m1 + jnp.log(l1), 0.0))

    in_specs = [pl.BlockSpec((tq, hw), lambda b, p, i: (b * nq + i, p)),
                pl.BlockSpec((seq, hw), lambda b, p, i: (b, p)),
                pl.BlockSpec((seq, 128), lambda b, p, i: (b, p))]
    args = [q, k, v]
    if use_f:
        in_specs.append(pl.BlockSpec((1, 1, 8, seq), lambda b, p, i: (b, p, 0, 0)))
        args.append(f)
    oblk = pl.BlockSpec((tq, 128), lambda b, p, i: (b * nq + i, p))
    return pl.pallas_call(
        body, name="attn_fwd_" + kind, grid=(nb, 3, nq), in_specs=in_specs, out_specs=[oblk, oblk],
        out_shape=[jax.ShapeDtypeStruct((t, GW), F32), jax.ShapeDtypeStruct((t, GW), F32)],
        compiler_params=_cparams(("arbitrary", "arbitrary", "arbitrary")),
    )(*args)


def _attn_bwd(kind, q, k, v, f, o, st, do, seq, scale, tq=512, tk=512, dep=None):
    t = v.shape[0]
    nb = t // seq
    tq, tk = min(tq, seq), min(tk, seq)
    nq = seq // tq
    nk = seq // tk
    hw = 256 if kind == "mla" else 128
    use_f = f is not None
    assert tq == tk

    def body(*refs):
        if use_f:
            q_ref, k_ref, v_ref, f_ref, o_ref, st_ref, do_ref, dq_ref, dk_ref, dv_ref, df_ref, dfq_ref = refs
        else:
            q_ref, k_ref, v_ref, o_ref, st_ref, do_ref, dq_ref, dk_ref, dv_ref = refs
        kj = pl.program_id(2)
        k0 = kj * tk
        lane = lax.broadcasted_iota(jnp.int32, (1, 128), 1)
        half = lane >= 64

        @pl.when(kj == 0)
        def _():
            dq_ref[...] = jnp.zeros_like(dq_ref)
            if use_f:
                dfq_ref[...] = jnp.zeros_like(dfq_ref)

        dk_ref[...] = jnp.zeros_like(dk_ref)
        dv_ref[...] = jnp.zeros_like(dv_ref)
        if use_f:
            df_ref[...] = jnp.zeros_like(df_ref)
        vv = v_ref[...]
        diag = _causal_mask(kind, 0, 0, tq, tk)

        def qstep(qi, masked):
            q0 = pl.multiple_of(qi * tq, tq)
            rows = pl.ds(q0, tq)
            dov = do_ref[rows, :]
            dd = dov * o_ref[rows, :]
            stv = st_ref[rows, :]
            for j in range(2):
                hm = half == bool(j)
                delta = jnp.sum(jnp.where(hm, dd, 0.0), axis=-1, keepdims=True)
                lse = stv[:, j:j + 1]
                if kind == "mla":
                    cols = slice(j * 128, (j + 1) * 128)
                    qh = q_ref[rows, cols]
                    kh = k_ref[:, cols]
                else:
                    cols = slice(0, 128)
                    qa = q_ref[rows, :]
                    qh = jnp.where(hm, qa, jnp.zeros_like(qa))
                    kh = k_ref[...]
                s = _nt(qh, kh) * scale
                if use_f:
                    s = s - f_ref[0, 0, j:j + 1, :]
                if masked:
                    s = jnp.where(diag, s, NEG)
                p = jnp.exp(s - lse)
                doh = jnp.where(hm, dov, 0.0).astype(BF16)
                ds = p * (_nt(doh, vv) - delta)
                dsb = (ds * scale).astype(BF16)
                dv_ref[...] += _tn(p.astype(BF16), doh)
                dk_ref[:, cols] += _tn(dsb, qh)
                dqc = jnp.dot(dsb, kh, preferred_element_type=F32)
                if kind != "mla":
                    dqc = jnp.where(hm, dqc, 0.0)
                dq_ref[rows, cols] += dqc
                if use_f:
                    df_ref[0, 0, j:j + 1, :] += -jnp.sum(ds, axis=0, keepdims=True)
                    dfq_ref[rows, :] += jnp.where(lane == j, jnp.sum(ds, axis=-1, keepdims=True), 0.0)

        qstep(kj, True)

        def rest(qi, carry):
            qstep(qi, False)
            return carry

        lax.fori_loop(kj + 1, nq, rest, 0)

    full_q = lambda wd: pl.BlockSpec((seq, wd), lambda b, p, i: (b, p))
    kblk = lambda wd: pl.BlockSpec((tk, wd), lambda b, p, i: (b * nk + i, p))
    in_specs = [full_q(hw), kblk(hw), kblk(128)]
    args = [q, k, v]
    if use_f:
        in_specs.append(pl.BlockSpec((1, 1, 8, tk), lambda b, p, i: (b, p, 0, i)))
        args.append(f)
    in_specs += [full_q(128), full_q(128), full_q(128)]
    args += [o, st, do]
    out_specs = [full_q(hw), kblk(hw), kblk(128)]
    out_shape = [jax.ShapeDtypeStruct((t, 3 * hw), F32), jax.ShapeDtypeStruct((t, 3 * hw), F32),
                 jax.ShapeDtypeStruct((t, GW), F32)]
    if use_f:
        out_specs += [pl.BlockSpec((1, 1, 8, tk), lambda b, p, i: (b, p, 0, i)), full_q(128)]
        out_shape += [jax.ShapeDtypeStruct((nb, 3, 8, seq), F32), jax.ShapeDtypeStruct((t, GW), F32)]
    body, in_specs, args = _after(dep, body, in_specs, args)
    return pl.pallas_call(
        body, name="attn_bwd_" + kind, grid=(nb, 3, nk), in_specs=in_specs, out_specs=out_specs,
        out_shape=out_shape, compiler_params=_cparams(("arbitrary", "arbitrary", "arbitrary")),
    )(*args)


BQ = 256
BWIN = BQ + B_LEFT


def _band_geometry():
    r = lax.broadcasted_iota(jnp.int32, (BQ, BWIN), 0)
    j = lax.broadcasted_iota(jnp.int32, (BQ, BWIN), 1)
    rc = lax.shift_right_logical(r, 6)
    jc = lax.shift_right_logical(j, 6)
    allowed = (jc - 8 <= rc) & (rc <= jc)
    return (r + B_LEFT - j) >= REL_CLIP, allowed, j < r


def _band_onehot(transposed, offset=0):
    shape = (BWIN, GW) if transposed else (GW, BWIN)
    kk = lax.broadcasted_iota(jnp.int32, shape, 1 if transposed else 0)
    x = lax.broadcasted_iota(jnp.int32, shape, 0 if transposed else 1) - offset
    x = jnp.where(x < 0, x + BWIN, x)
    return (kk == jnp.clip(B_LEFT - x, -REL_CLIP, REL_CLIP) + REL_CLIP).astype(F32)


def _band_table(rel_bias8):
    def body(b_ref, o_ref):
        hh = pl.program_id(0)
        u8 = jnp.dot(b_ref[...], _band_onehot(False), precision=HI, preferred_element_type=F32)
        rid = lax.broadcasted_iota(jnp.int32, (8, BWIN), 0)
        row = jnp.sum(jnp.where(rid == hh, u8, 0.0), axis=0, keepdims=True)
        far, allowed, _ = _band_geometry()
        tbl = pltpu.roll(jnp.broadcast_to(row, (BQ, BWIN)), 0, 1, stride=1, stride_axis=0)
        tbl = jnp.where(far, row[:, 0:1], tbl)
        o_ref[0] = jnp.where(allowed, tbl, NEG)

    return pl.pallas_call(
        body, name="band_table", grid=(6,),
        in_specs=[pl.BlockSpec((8, GW), lambda h: (0, 0))],
        out_specs=pl.BlockSpec((1, BQ, BWIN), lambda h: (h, 0, 0)),
        out_shape=jax.ShapeDtypeStruct((6, BQ, BWIN), F32),
        compiler_params=_cparams(("arbitrary",)),
    )(rel_bias8)


def _band_table_bwd(gtab):
    def body(g_ref, o_ref):
        gv = g_ref[0]
        _, _, wrapped = _band_geometry()
        gfar = jnp.sum(jnp.sum(jnp.where(wrapped, gv, 0.0), axis=-1, keepdims=True), axis=0, keepdims=True)
        anti = (lax.broadcasted_iota(jnp.int32, (BQ, BQ), 0) + lax.broadcasted_iota(jnp.int32, (BQ, BQ), 1)
                == BQ - 1).astype(F32)
        grev = jnp.dot(anti, jnp.where(wrapped, 0.0, gv), precision=HI, preferred_element_type=F32)
        near = pltpu.roll(grev, 0, 1, stride=1, stride_axis=0)
        y = jnp.broadcast_to(jnp.sum(near, axis=0, keepdims=True), (8, BWIN))
        gb = jnp.dot(y, _band_onehot(True, BQ - 1), precision=HI, preferred_element_type=F32)
        lane = lax.broadcasted_iota(jnp.int32, (8, GW), 1)
        o_ref[0] = gb + jnp.where(lane == 2 * REL_CLIP, gfar, 0.0)

    return pl.pallas_call(
        body, name="band_table_bwd", grid=(B_HEADS,),
        in_specs=[pl.BlockSpec((1, BQ, BWIN), lambda h: (h, 0, 0))],
        out_specs=pl.BlockSpec((1, 8, GW), lambda h: (h, 0, 0)),
        out_shape=jax.ShapeDtypeStruct((B_HEADS, 8, GW), F32),
        compiler_params=_cparams(("arbitrary",)),
    )(gtab)


def _band_fwd(q, k, v, table, seq, scale):
    t = q.shape[0]
    nb = t // seq
    nq = seq // BQ

    def body(q_ref, k_ref, v_ref, tb_ref, o_ref, st_ref, kpad, vpad):
        qi = pl.program_id(2)
        q0 = pl.multiple_of(qi * BQ, BQ)
        lane = lax.broadcasted_iota(jnp.int32, (1, 128), 1)
        half = lane >= 64

        @pl.when(qi == 0)
        def _():
            kpad[0:B_LEFT, :] = jnp.zeros((B_LEFT, 128), BF16)
            vpad[0:B_LEFT, :] = jnp.zeros((B_LEFT, 128), BF16)
            kpad[B_LEFT:, :] = k_ref[...]
            vpad[B_LEFT:, :] = v_ref[...]

        kw = kpad[pl.ds(q0, BWIN), :]
        vw = vpad[pl.ds(q0, BWIN), :]
        inside = lax.broadcasted_iota(jnp.int32, (BQ, BWIN), 1) >= B_LEFT - q0
        qall = q_ref[...]
        outs, lses = [], []
        for j in range(2):
            qh = jnp.where(half == bool(j), qall, jnp.zeros_like(qall))
            s = jnp.where(inside, _nt(qh, kw) * scale + tb_ref[j], NEG)
            m = jnp.max(s, axis=-1, keepdims=True)
            p = jnp.exp(s - m)
            l = jnp.sum(p, axis=-1, keepdims=True)
            outs.append(jnp.dot(p.astype(BF16), vw, preferred_element_type=F32) / l)
            lses.append(m + jnp.log(l))
        o_ref[...] = jnp.where(half, outs[1], outs[0])
        st_ref[...] = jnp.where(lane == 0, lses[0], jnp.where(lane == 1, lses[1], 0.0))

    qblk = pl.BlockSpec((BQ, 128), lambda b, p, i: (b * nq + i, p))
    full = pl.BlockSpec((seq, 128), lambda b, p, i: (b, p))
    return pl.pallas_call(
        body, name="band_fwd", grid=(nb, 3, nq),
        in_specs=[qblk, full, full, pl.BlockSpec((2, BQ, BWIN), lambda b, p, i: (p, 0, 0))],
        out_specs=[qblk, qblk],
        out_shape=[jax.ShapeDtypeStruct((t, GW), F32), jax.ShapeDtypeStruct((t, GW), F32)],
        scratch_shapes=[pltpu.VMEM((seq + B_LEFT, 128), BF16), pltpu.VMEM((seq + B_LEFT, 128), BF16)],
        compiler_params=_cparams(("arbitrary", "arbitrary", "arbitrary")),
    )(q, k, v, table)


def _band_bwd(q, k, v, table, o, st, do, seq, scale, dep=None):
    t = q.shape[0]
    nb = t // seq
    nq = seq // BQ

    def body(q_ref, k_ref, v_ref, tb_ref, o_ref, st_ref, do_ref, dq_ref, dk_ref, dv_ref, g_ref,
             kpad, vpad, dkpad, dvpad):
        b = pl.program_id(1)
        qi = pl.program_id(2)
        q0 = pl.multiple_of(qi * BQ, BQ)
        lane = lax.broadcasted_iota(jnp.int32, (1, 128), 1)
        half = lane >= 64

        @pl.when(qi == 0)
        def _():
            kpad[0:B_LEFT, :] = jnp.zeros((B_LEFT, 128), BF16)
            vpad[0:B_LEFT, :] = jnp.zeros((B_LEFT, 128), BF16)
            kpad[B_LEFT:, :] = k_ref[...]
            vpad[B_LEFT:, :] = v_ref[...]
            dkpad[...] = jnp.zeros_like(dkpad)
            dvpad[...] = jnp.zeros_like(dvpad)

        @pl.when((qi == 0) & (b == 0))
        def _():
            g_ref[...] = jnp.zeros_like(g_ref)

        win = pl.ds(q0, BWIN)
        kw = kpad[win, :]
        vw = vpad[win, :]
        inside = lax.broadcasted_iota(jnp.int32, (BQ, BWIN), 1) >= B_LEFT - q0
        qall = q_ref[...]
        dov = do_ref[...]
        dd = dov * o_ref[...]
        stv = st_ref[...]
        dq = jnp.zeros((BQ, 128), F32)
        for j in range(2):
            hm = half == bool(j)
            qh = jnp.where(hm, qall, jnp.zeros_like(qall))
            delta = jnp.sum(jnp.where(hm, dd, 0.0), axis=-1, keepdims=True)
            s = jnp.where(inside, _nt(qh, kw) * scale + tb_ref[j], NEG)
            p = jnp.exp(s - stv[:, j:j + 1])
            doh = jnp.where(hm, dov, 0.0).astype(BF16)
            ds = p * (_nt(doh, vw) - delta)
            g_ref[j] += ds
            dsb = (ds * scale).astype(BF16)
            dvpad[win, :] += _tn(p.astype(BF16), doh)
            dkpad[win, :] += _tn(dsb, qh)
            dq = dq + jnp.where(hm, jnp.dot(dsb, kw, preferred_element_type=F32), 0.0)
        dq_ref[...] = dq

        @pl.when(qi == nq - 1)
        def _():
            dk_ref[...] = dkpad[B_LEFT:, :]
            dv_ref[...] = dvpad[B_LEFT:, :]

    qblk = pl.BlockSpec((BQ, 128), lambda p, b, i: (b * nq + i, p))
    full = pl.BlockSpec((seq, 128), lambda p, b, i: (b, p))
    tblk = pl.BlockSpec((2, BQ, BWIN), lambda p, b, i: (p, 0, 0))
    body, in_specs, args = _after(dep, body, [qblk, full, full, tblk, qblk, qblk, qblk], [q, k, v, table, o, st, do])
    return pl.pallas_call(
        body, name="band_bwd", grid=(3, nb, nq),
        in_specs=in_specs,
        out_specs=[qblk, full, full, tblk],
        out_shape=[jax.ShapeDtypeStruct((t, GW), F32), jax.ShapeDtypeStruct((t, GW), F32),
                   jax.ShapeDtypeStruct((t, GW), F32), jax.ShapeDtypeStruct((6, BQ, BWIN), F32)],
        scratch_shapes=[pltpu.VMEM((seq + B_LEFT, 128), BF16), pltpu.VMEM((seq + B_LEFT, 128), BF16),
                        pltpu.VMEM((seq + B_LEFT, 128), F32), pltpu.VMEM((seq + B_LEFT, 128), F32)],
        compiler_params=_cparams(("arbitrary", "arbitrary", "arbitrary")),
    )(*args)


def _fox_prep(cf, fb, seq):
    nb = cf.shape[0] // seq
    nblk = seq // 128

    def body(cf_ref, fb_ref, f_ref):
        x = cf_ref[...] + fb_ref[...]
        lf = jnp.minimum(x, 0.0) - jnp.log1p(jnp.exp(-jnp.abs(x)))
        rows = lf.T[0:8, :]
        upper = (lax.broadcasted_iota(jnp.int32, (128, 128), 0)
                 <= lax.broadcasted_iota(jnp.int32, (128, 128), 1)).astype(F32)
        carry = jnp.zeros((8, 1), F32)
        for blk in range(nblk):
            sl = slice(blk * 128, (blk + 1) * 128)
            cs = jnp.dot(rows[:, sl], upper, precision=HI, preferred_element_type=F32) + carry
            carry = cs[:, 127:128]
            f_ref[0, 0, :, sl] = cs
            f_ref[0, 1, :, sl] = pltpu.roll(cs, 6, 0)
            f_ref[0, 2, :, sl] = pltpu.roll(cs, 4, 0)

    return pl.pallas_call(
        body, name="fox_prep", grid=(nb,),
        in_specs=[pl.BlockSpec((seq, 128), lambda b: (b, 0)), pl.BlockSpec((1, 128), lambda b: (0, 0))],
        out_specs=pl.BlockSpec((1, 3, 8, seq), lambda b: (b, 0, 0, 0)),
        out_shape=jax.ShapeDtypeStruct((nb, 3, 8, seq), F32),
        compiler_params=_cparams(("arbitrary",)),
    )(cf, fb)


def _fox_prep_bwd(df, dfq, cf, fb, seq):
    nb = cf.shape[0] // seq
    nblk = seq // 128

    def body(df_ref, dfq_ref, cf_ref, fb_ref, dcf_ref, dfb_ref, wide):
        b = pl.program_id(0)
        row = lax.broadcasted_iota(jnp.int32, (8, seq), 0)
        dfh = None
        for p in range(3):
            both = df_ref[0, p] + dfq_ref[:, p * 128:(p + 1) * 128].T[0:8, :]
            both = jnp.where(row < 2, both, 0.0)
            if p:
                both = pltpu.roll(both, 2 * p, 0)
            dfh = both if dfh is None else dfh + both
        lower = (lax.broadcasted_iota(jnp.int32, (128, 128), 0)
                 >= lax.broadcasted_iota(jnp.int32, (128, 128), 1)).astype(F32)
        wide[...] = jnp.zeros_like(wide)
        carry = jnp.zeros((8, 1), F32)
        for blk in reversed(range(nblk)):
            sl = slice(blk * 128, (blk + 1) * 128)
            rc = jnp.dot(dfh[:, sl], lower, precision=HI, preferred_element_type=F32) + carry
            carry = rc[:, 0:1]
            wide[0:8, sl] = rc
        dl = wide[...].T
        x = cf_ref[...] + fb_ref[...]
        dcf = dl * (1.0 / (1.0 + jnp.exp(x)))
        dcf_ref[...] = dcf
        part = jnp.sum(dcf, axis=0, keepdims=True)

        @pl.when(b == 0)
        def _():
            dfb_ref[...] = part

        @pl.when(b != 0)
        def _():
            dfb_ref[...] += part

    return pl.pallas_call(
        body, name="fox_prep_bwd", grid=(nb,),
        in_specs=[pl.BlockSpec((1, 3, 8, seq), lambda b: (b, 0, 0, 0)), pl.BlockSpec((seq, GW), lambda b: (b, 0)),
                  pl.BlockSpec((seq, 128), lambda b: (b, 0)), pl.BlockSpec((1, 128), lambda b: (0, 0))],
        out_specs=[pl.BlockSpec((seq, 128), lambda b: (b, 0)), pl.BlockSpec((1, 128), lambda b: (0, 0))],
        out_shape=[jax.ShapeDtypeStruct(cf.shape, F32), jax.ShapeDtypeStruct((1, 128), F32)],
        scratch_shapes=[pltpu.VMEM((128, seq), F32)],
        compiler_params=_cparams(("arbitrary",)),
    )(df, dfq, cf, fb)


def _gate_out(oa, ob, oc, gates, w, x, gate, seq, tm=256):
    t = x.shape[0]
    tps = seq // tm

    def body(oa_ref, ob_ref, oc_ref, g_ref, w_ref, x_ref, gt_ref, xo_ref, y_ref, u_ref):
        for n, o_ref in enumerate((oa_ref, ob_ref, oc_ref)):
            sl = slice(n * GW, (n + 1) * GW)
            gv = g_ref[:, sl]
            u_ref[:, sl] = (o_ref[...] * (gv * _sigmoid(gv))).astype(BF16)
        y = jnp.dot(u_ref[...], w_ref[...], preferred_element_type=F32)
        y_ref[...] = y
        xo_ref[...] = x_ref[...] + gt_ref[0] * y

    row = lambda wd: pl.BlockSpec((tm, wd), lambda i: (i, 0))
    return pl.pallas_call(
        body, name="gate_out", grid=(t // tm,),
        in_specs=[row(GW), row(GW), row(GW), row(U_PAD), pl.BlockSpec((U_PAD, D_MODEL), lambda i: (0, 0)),
                  row(D_MODEL), pl.BlockSpec((1, 1, D_MODEL), lambda i: (i // tps, 0, 0))],
        out_specs=[row(D_MODEL), row(D_MODEL), row(U_PAD)],
        out_shape=[jax.ShapeDtypeStruct((t, D_MODEL), F32), jax.ShapeDtypeStruct((t, D_MODEL), F32),
                   jax.ShapeDtypeStruct((t, U_PAD), BF16)],
        compiler_params=_cparams(("arbitrary",)),
    )(oa, ob, oc, gates, w, x, gate)


def _gate_out_bwd(dxo, y, gate, oa, ob, oc, gates, w_t, seq, tm=256, dep=None):
    t = dxo.shape[0]
    tps = seq // tm
    nb = t // seq

    def body(dxo_ref, y_ref, gt_ref, oa_ref, ob_ref, oc_ref, g_ref, wt_ref,
             dy_ref, doa_ref, dob_ref, doc_ref, dg_ref, dgt_ref):
        i = pl.program_id(0)
        dxo_v = dxo_ref[...]
        dgt = jnp.sum(dxo_v * y_ref[...], axis=0, keepdims=True)
        dyb = (dxo_v * gt_ref[0]).astype(BF16)
        dy_ref[...] = dyb
        du = jnp.dot(dyb, wt_ref[...], preferred_element_type=F32)
        for n, (o_ref, do_ref) in enumerate(((oa_ref, doa_ref), (ob_ref, dob_ref), (oc_ref, doc_ref))):
            sl = slice(n * GW, (n + 1) * GW)
            gv = g_ref[:, sl]
            sg = _sigmoid(gv)
            dun = du[:, sl]
            do_ref[...] = dun * (gv * sg)
            dg_ref[:, sl] = dun * o_ref[...] * (sg * (1.0 + gv * (1.0 - sg)))

        @pl.when(i % tps == 0)
        def _():
            dgt_ref[0] = dgt

        @pl.when(i % tps != 0)
        def _():
            dgt_ref[0] += dgt

    row = lambda wd: pl.BlockSpec((tm, wd), lambda i: (i, 0))
    per_b = pl.BlockSpec((1, 1, D_MODEL), lambda i: (i // tps, 0, 0))
    in_specs = [row(D_MODEL), row(D_MODEL), per_b, row(GW), row(GW), row(GW), row(U_PAD),
                pl.BlockSpec((D_MODEL, U_PAD), lambda i: (0, 0))]
    body, in_specs, args = _after(dep, body, in_specs, [dxo, y, gate, oa, ob, oc, gates, w_t])
    return pl.pallas_call(
        body, name="gate_out_bwd", grid=(t // tm,), in_specs=in_specs,
        out_specs=[row(D_MODEL), row(GW), row(GW), row(GW), row(U_PAD), per_b],
        out_shape=[jax.ShapeDtypeStruct((t, D_MODEL), BF16), jax.ShapeDtypeStruct((t, GW), F32),
                   jax.ShapeDtypeStruct((t, GW), F32), jax.ShapeDtypeStruct((t, GW), F32),
                   jax.ShapeDtypeStruct((t, U_PAD), F32), jax.ShapeDtypeStruct((nb, 1, D_MODEL), F32)],
        compiler_params=_cparams(("arbitrary",)),
    )(*args)


def _final_loss(x, target, g, tm=256):
    t = x.shape[0]

    def body(x_ref, t_ref, g_ref, dx_ref, loss_ref, dg_ref):
        i = pl.program_id(0)
        xv = x_ref[...]
        rstd = lax.rsqrt(jnp.mean(xv * xv, axis=-1, keepdims=True) + EPS)
        xn = xv * rstd
        gv = g_ref[...]
        err = xn * gv - t_ref[...]
        dy = err * (1.0 / D_MODEL)
        dxn = dy * gv
        dx_ref[...] = rstd * (dxn - xn * jnp.mean(dxn * xn, axis=-1, keepdims=True))
        lp = jnp.sum(err * err, axis=0, keepdims=True) * (0.5 / D_MODEL)
        dgp = jnp.sum(dy * xn, axis=0, keepdims=True)

        @pl.when(i == 0)
        def _():
            loss_ref[...] = lp
            dg_ref[...] = dgp

        @pl.when(i != 0)
        def _():
            loss_ref[...] += lp
            dg_ref[...] += dgp

    row = pl.BlockSpec((tm, D_MODEL), lambda i: (i, 0))
    vec = pl.BlockSpec((1, D_MODEL), lambda i: (0, 0))
    return pl.pallas_call(
        body, name="final_loss", grid=(t // tm,),
        in_specs=[row, row, vec], out_specs=[row, vec, vec],
        out_shape=[jax.ShapeDtypeStruct((t, D_MODEL), F32), jax.ShapeDtypeStruct((1, D_MODEL), F32),
                   jax.ShapeDtypeStruct((1, D_MODEL), F32)],
        compiler_params=_cparams(("arbitrary",)),
    )(x, target, g)


def _adamw(w, gslots, m, v, name, tr=None):
    r, c = w.shape
    ns = gslots.shape[0]
    tr = r if tr is None else tr

    def body(w_ref, g_ref, m_ref, v_ref, go_ref, d_ref, mo_ref, vo_ref):
        g = g_ref[0].astype(F32)
        for j in range(1, ns):
            g = g + g_ref[j].astype(F32)
        mn = ADAM_B1 * m_ref[...] + (1.0 - ADAM_B1) * g
        vn = ADAM_B2 * v_ref[...] + (1.0 - ADAM_B2) * jnp.square(g)
        m_hat = mn / (1.0 - ADAM_B1 ** ADAM_STEP)
        v_hat = vn / (1.0 - ADAM_B2 ** ADAM_STEP)
        go_ref[...] = g
        d_ref[...] = -ADAM_LR * (m_hat / (jnp.sqrt(v_hat) + ADAM_EPS) + ADAM_WD * w_ref[...])
        mo_ref[...] = mn
        vo_ref[...] = vn

    blk = pl.BlockSpec((tr, c), lambda i: (i, 0))
    return pl.pallas_call(
        body, name=name, grid=(r // tr,),
        in_specs=[blk, pl.BlockSpec((ns, tr, c), lambda i: (0, i, 0)), blk, blk],
        out_specs=[blk] * 4, out_shape=[jax.ShapeDtypeStruct((r, c), F32)] * 4,
        compiler_params=_cparams(("arbitrary",)),
    )(w, gslots, m, v)


def _rope_tables(positions):
    inv = ROPE_THETA ** (-jnp.arange(0, A_ROPE, 2, dtype=F32) / A_ROPE)
    ang = positions.astype(F32)[:, None] * inv
    cos, sin = jnp.cos(ang), jnp.sin(ang)
    t = positions.shape[0]
    one = jnp.ones((t, 64), F32)
    zero16 = jnp.zeros((t, 16), F32)
    cos_t = jnp.concatenate([one, cos, cos, jnp.ones((t, 32), F32)], axis=1)
    sin_a = jnp.concatenate([jnp.zeros((t, 64), F32), -sin, zero16, jnp.zeros((t, 32), F32)], axis=1)
    sin_b = jnp.concatenate([jnp.zeros((t, 64), F32), zero16, sin, jnp.zeros((t, 32), F32)], axis=1)
    return cos_t, sin_a, sin_b


def _pad_heads(w, real, padded, nheads, axis):
    shp = w.shape[:axis] + (nheads, real) + w.shape[axis + 1:]
    w = w.reshape(shp)
    pad = [(0, 0)] * w.ndim
    pad[axis + 1] = (0, padded - real)
    w = jnp.pad(w, pad)
    return w.reshape(w.shape[:axis] + (nheads * padded,) + w.shape[axis + 2:])


def kernel(x, c, positions, w_ada, b_ada, norm_g, w_in, a_q_norm_g, a_w_uq, a_kv_norm_g, a_w_ukv, b_rel_bias, c_forget_b, w_out, final_g, loss_target, m_w_ada, m_b_ada, m_norm_g, m_w_in, m_a_q_norm_g, m_a_w_uq, m_a_kv_norm_g, m_a_w_ukv, m_b_rel_bias, m_c_forget_b, m_w_out, m_final_g, v_w_ada, v_b_ada, v_norm_g, v_w_in, v_a_q_norm_g, v_a_w_uq, v_a_kv_norm_g, v_a_w_ukv, v_b_rel_bias, v_c_forget_b, v_w_out, v_final_g):
    nb, seq, _ = x.shape
    t = nb * seq
    me = 4 * lax.axis_index("x") + 2 * lax.axis_index("y") + lax.axis_index("c")
    x2 = x.reshape(t, D_MODEL)
    tgt = loss_target.reshape(t, D_MODEL)
    cos_t, sin_a, sin_b = _rope_tables(positions.reshape(t))

    def shards(l):
        return [_pad_runs(w_in[l].astype(BF16), IN_RUNS, N_PAD, 1), w_out[l].astype(BF16),
                a_w_uq[l].astype(BF16), a_w_ukv[l].astype(BF16)]

    def prepare(gi, go, gq, gkv):
        wi = gi.reshape(D_MODEL, N_PAD)
        wo = _pad_runs(go.reshape(D_MODEL, D_MODEL), OUT_RUNS, U_PAD, 0)
        wq = jnp.transpose(gq, (1, 0, 2)).reshape(A_Q_RANK, A_HEADS * (A_NOPE + A_ROPE))
        wq = _pad_heads(wq, A_NOPE + A_ROPE, HEAD_PAD, A_HEADS, 1)
        wkv = jnp.transpose(gkv, (1, 0, 2)).reshape(A_KV_RANK, A_HEADS, 2 * A_NOPE)
        wk = jnp.pad(wkv[:, :, :A_NOPE], ((0, 0), (0, 0), (0, HEAD_PAD - A_NOPE))).reshape(A_KV_RANK, A_HEADS * HEAD_PAD)
        wv = wkv[:, :, A_NOPE:].reshape(A_KV_RANK, GW)
        return dict(w_in=wi, w_in_t=wi.T, w_out=wo, w_out_t=wo.T, wuq=wq, wuq_t=wq.T, wk=wk, wk_t=wk.T,
                    wv=wv, wv_t=wv.T)

    gathered = _gather(shards(0) + [c], "gather_weights0")
    c_all = gathered[-1].reshape(N_DEV * nb, D_MODEL)
    weights = [prepare(*gathered[:4]), None]
    gather1, gather1_token = _split_start("gather", shards(1), "gather_weights1_start")

    c_act, mod_cols = _ada_fwd(c_all, w_ada)
    (mod_g,) = _gather([mod_cols], "gather_mod")
    mod_all = jnp.transpose(mod_g, (1, 2, 0, 3)).reshape(DEPTH, N_DEV * nb, 3 * D_MODEL)
    mod = lax.dynamic_slice_in_dim(mod_all, me * nb, nb, axis=1) + b_ada[:, None, :]

    fb_pad = jnp.pad(c_forget_b, ((0, 0), (0, 128 - C_HEADS)))
    a_scale = (A_NOPE + A_ROPE) ** -0.5
    h_scale = CHUNK ** -0.5

    saved = []
    xl = x2
    for l in range(DEPTH):
        if l == 1:
            weights[1] = prepare(*_split_wait(gather1, xl, "gather_weights1_wait")[1])
        w = weights[l]
        shift, scale, gate = mod[l, :, :D_MODEL], mod[l, :, D_MODEL:2 * D_MODEL], mod[l, :, 2 * D_MODEL:]
        ss = jnp.stack([shift, 1.0 + scale], axis=1)
        gate3 = gate[:, None, :]
        h, cq, ckv, kpe, gates, bq, bk, bv, cq2, ck, cv, cf = _ln_in(
            xl, ss, norm_g[l:l + 1], w["w_in"], seq, dep=gather1_token if l == 0 else None)
        q, k, v, cqn, ckvn = _mla_prep(cq, ckv, kpe, a_q_norm_g[l:l + 1], a_kv_norm_g[l:l + 1],
                                       w["wuq"], w["wk"], w["wv"], cos_t, sin_a, sin_b)
        oa, sta = _attn_fwd("mla", q, k, v, None, seq, a_scale)
        table = _band_table(jnp.pad(b_rel_bias[l], ((0, 8 - B_HEADS), (0, GW - N_REL))))
        ob, stb = _band_fwd(bq, bk, bv, table, seq, h_scale)
        fcum = _fox_prep(cf, fb_pad[l:l + 1], seq)
        oc, stc = _attn_fwd("fox", cq2, ck, cv, fcum, seq, h_scale)
        xn, y, u = _gate_out(oa, ob, oc, gates, w["w_out"], xl, gate3, seq)
        saved.append(dict(x=xl, ss=ss, gate3=gate3, h=h, cq=cq, ckv=ckv, gates=gates, bq=bq, bk=bk, bv=bv,
                          cq2=cq2, ck=ck, cv=cv, cf=cf, q=q, k=k, v=v, cqn=cqn, ckvn=ckvn, oa=oa, sta=sta,
                          table=table, ob=ob, stb=stb, fcum=fcum, oc=oc, stc=stc, y=y, u=u))
        xl = xn

    dx, loss_lanes, g_final = _final_loss(xl, tgt, final_g[None, :])
    loss = lax.psum(jnp.sum(loss_lanes), AXES)

    rows = D_MODEL // N_DEV
    core = lax.axis_index("c").astype(jnp.int32).reshape(1)
    grad_names = ("in_a", "in_b", "out", "uq", "ukv")
    n_seg_a = 4
    dmods, smalls, parts = [None] * DEPTH, [None] * DEPTH, [None] * DEPTH
    pair1 = chips1 = pair1_token = chips1_token = None
    for l in reversed(range(DEPTH)):
        s, w = saved[l], weights[l]
        dy, doa, dob, doc, dgates, dgate = _gate_out_bwd(dx, s["y"], s["gate3"], s["oa"], s["ob"], s["oc"],
                                                         s["gates"], w["w_out_t"], seq, dep=pair1_token)
        g_out = _unpad_runs(_matmul_tn(s["u"], dy, "dw_out"), OUT_RUNS, 0)
        if l == 0:
            own, from_sib = _split_wait(pair1, g_out, "grads1_pair_wait")
            sums = [_pair_add(core, a, r, "grads1_add_" + nm, r.shape[1]) for a, r, nm in zip(own, from_sib, grad_names)]
            chips1, chips1_token = _split_start("chips", sums, "grads1_chips_start")
        dq, dk, dv = _attn_bwd("mla", s["q"], s["k"], s["v"], None, s["oa"], s["sta"], doa, seq, a_scale,
                               dep=chips1_token)
        dbq, dbk, dbv, gtab = _band_bwd(s["bq"], s["bk"], s["bv"], s["table"], s["ob"], s["stb"], dob, seq, h_scale,
                                        dep=chips1_token)
        g_rel = _band_table_bwd(gtab)[:, 0, :N_REL]
        dcq2, dck, dcv, dfc, dfq = _attn_bwd("fox", s["cq2"], s["ck"], s["cv"], s["fcum"], s["oc"], s["stc"], doc,
                                             seq, h_scale, dep=chips1_token)
        dcf, dfb = _fox_prep_bwd(dfc, dfq, s["cf"], fb_pad[l:l + 1], seq)
        dcq, dckv, dkpe, dqlin, dklin, dgq, dgkv = _mla_prep_bwd(
            dq, dk, dv, s["cq"], s["ckv"], a_q_norm_g[l:l + 1], a_kv_norm_g[l:l + 1],
            w["wuq_t"], w["wk_t"], w["wv_t"], cos_t, sin_a, sin_b)
        gq_pad = _matmul_tn(s["cqn"], dqlin, "dw_uq")
        g_uq = gq_pad.reshape(A_Q_RANK, A_HEADS, HEAD_PAD)[:, :, :A_NOPE + A_ROPE].reshape(A_Q_RANK, -1)
        gkv_pad = _matmul_tn(s["ckvn"], [dklin, dv], "dw_ukv")
        gk_pad = gkv_pad[:, :A_HEADS * HEAD_PAD].reshape(A_KV_RANK, A_HEADS, HEAD_PAD)[:, :, :A_NOPE]
        gv_pad = gkv_pad[:, A_HEADS * HEAD_PAD:].reshape(A_KV_RANK, A_HEADS, A_NOPE)
        g_ukv = jnp.concatenate([gk_pad, gv_pad], axis=2).reshape(A_KV_RANK, -1)
        dz = [dcq, dckv, dkpe, dgates, dbq, dbk, dbv, dcq2, dck, dcv, dcf]
        g_in_a = _matmul_tn(s["h"], dz[:n_seg_a], "dw_in_a")
        g_in_b = _matmul_tn(s["h"], dz[n_seg_a:], "dw_in_b")
        dx, dss, dg_norm = _ln_in_bwd(dz, w["w_in_t"], s["x"], s["ss"], norm_g[l:l + 1], dx, seq)
        dmods[l] = jnp.concatenate([dss[:, 0, :], dss[:, 1, :], dgate[:, 0, :]], axis=1)
        smalls[l] = [dg_norm.reshape(-1), dgq.reshape(-1), dgkv.reshape(-1), g_rel.reshape(-1),
                     dfb[0, :C_HEADS]]
        slots = [g_in_a.reshape(N_DEV, rows, -1), g_in_b.reshape(N_DEV, rows, -1), g_out.reshape(N_DEV, rows, D_MODEL),
                 g_uq.reshape(A_Q_RANK, N_DEV, -1).transpose(1, 0, 2), g_ukv.reshape(A_KV_RANK, N_DEV, -1).transpose(1, 0, 2)]
        if l == 1:
            pair1, pair1_token = _split_start("pair", slots, "grads1_pair_start")
        else:
            parts[1] = _split_wait(chips1, dx, "grads1_chips_wait")[1]
            from_sib = _pair_swap(slots, "grads0_pair")
            sums = [_pair_add(core, a, r, "grads0_add_" + nm, r.shape[1]) for a, r, nm in zip(slots, from_sib, grad_names)]
            parts[0] = _chip_a2a(sums, "grads0_chips")
    grad_x = dx.reshape(nb, seq, D_MODEL)

    small = jnp.concatenate([p for l in range(DEPTH) for p in smalls[l]] + [g_final.reshape(-1)])
    n_small = small.shape[0]
    small_rows = -(-n_small // 1024) * 8
    small = jnp.pad(small, (0, small_rows * 128 - n_small)).reshape(small_rows, 128)
    dmod_local = jnp.stack(dmods)
    dmod_g, small_g = _gather([dmod_local, small], "gather_small")
    dmod_all = jnp.transpose(dmod_g, (1, 0, 2, 3)).reshape(DEPTH, N_DEV * nb, 3 * D_MODEL)
    cols = 3 * D_MODEL // N_DEV
    dmod_mine = lax.dynamic_slice_in_dim(dmod_all, me * cols, cols, axis=2)
    g_w_ada, g_b_ada = _ada_bwd(c_act, dmod_all, dmod_mine)
    small_sum = _sum_slots(small_g, "sum_small").reshape(-1)

    p_in = jnp.stack([_unpad_runs(jnp.concatenate(parts[l][0:2], axis=2), IN_RUNS, 2) for l in range(DEPTH)], axis=1)
    p_out, p_uq, p_ukv = (jnp.stack([parts[l][j] for l in range(DEPTH)], axis=1) for j in (2, 3, 4))

    def split_small():
        out, pos = [], 0
        sizes = [D_MODEL, A_Q_RANK, A_KV_RANK, B_HEADS * N_REL, C_HEADS]
        per_layer = []
        for l in range(DEPTH):
            parts = []
            for sz in sizes:
                parts.append(small_sum[pos:pos + sz])
                pos += sz
            per_layer.append(parts)
        for j in range(len(sizes)):
            out.append(jnp.stack([per_layer[l][j] for l in range(DEPTH)]))
        out.append(small_sum[pos:pos + D_MODEL])
        return out

    g_norm, g_qn, g_kvn, g_relb, g_fb, g_fin = split_small()

    def adam(w, g, m, v, name, tr=None):
        shp = w.shape
        w2 = w.reshape(-1, shp[-1]) if w.ndim > 1 else w.reshape(1, -1)
        gs = g.reshape((-1,) + w2.shape) if g.size != w.size else g.reshape((1,) + w2.shape)
        outs = _adamw(w2, gs, m.reshape(w2.shape), v.reshape(w2.shape), name, tr)
        return [o.reshape(shp) for o in outs]

    res = {
        "w_ada": adam(w_ada, g_w_ada, m_w_ada, v_w_ada, "adam_w_ada", 256),
        "b_ada": adam(b_ada, g_b_ada, m_b_ada, v_b_ada, "adam_b_ada"),
        "norm_g": adam(norm_g, g_norm, m_norm_g, v_norm_g, "adam_norm_g"),
        "w_in": adam(w_in, p_in, m_w_in, v_w_in, "adam_w_in", 32),
        "a_q_norm_g": adam(a_q_norm_g, g_qn, m_a_q_norm_g, v_a_q_norm_g, "adam_q_norm"),
        "a_w_uq": adam(a_w_uq, p_uq, m_a_w_uq, v_a_w_uq, "adam_w_uq"),
        "a_kv_norm_g": adam(a_kv_norm_g, g_kvn, m_a_kv_norm_g, v_a_kv_norm_g, "adam_kv_norm"),
        "a_w_ukv": adam(a_w_ukv, p_ukv, m_a_w_ukv, v_a_w_ukv, "adam_w_ukv"),
        "b_rel_bias": adam(b_rel_bias, g_relb.reshape(b_rel_bias.shape), m_b_rel_bias, v_b_rel_bias, "adam_rel_bias"),
        "c_forget_b": adam(c_forget_b, g_fb, m_c_forget_b, v_c_forget_b, "adam_forget_b"),
        "w_out": adam(w_out, p_out, m_w_out, v_w_out, "adam_w_out", 64),
        "final_g": adam(final_g, g_fin, m_final_g, v_final_g, "adam_final_g"),
    }
    names = ["w_ada", "b_ada", "norm_g", "w_in", "a_q_norm_g", "a_w_uq", "a_kv_norm_g", "a_w_ukv", "b_rel_bias",
             "c_forget_b", "w_out", "final_g"]
    outs = [loss, grad_x]
    for j in range(4):
        outs += [res[n][j] for n in names]
    return tuple(outs)
```

```python
import functools

import jax
import jax.numpy as jnp
from jax import lax
from jax.experimental import pallas as pl
from jax.experimental.pallas import tpu as pltpu

F32 = jnp.float32
BF16 = jnp.bfloat16
HI = lax.Precision.HIGHEST

N_DEV = 8
AXES = ("x", "y", "c")
D_MODEL = 1024
DEPTH = 2
CHUNK = 64
EPS = 1e-6
NEG = -1e30
A_HEADS = 6
A_NOPE = 64
A_ROPE = 32
A_Q_RANK = 384
A_KV_RANK = 256
ROPE_THETA = 10000.0
B_HEADS = 5
B_LEFT = 512
REL_CLIP = 128
N_REL = 2 * REL_CLIP + 1
C_HEADS = 5
HEAD_PAD = 128
GW = 384
N_IN = 3621
ADAM_LR = 0.001
ADAM_B1 = 0.9
ADAM_B2 = 0.999
ADAM_EPS = 1e-08
ADAM_WD = 0.01
ADAM_STEP = 10
VMEM_LIMIT = 56 * 1024 * 1024

Z_SEGS = (
    ("cq", 0, 384, F32), ("ckv", 384, 256, F32), ("kpe", 640, 128, F32), ("gates", 768, 1152, F32),
    ("bq", 1920, 384, BF16), ("bk", 2304, 384, BF16), ("bv", 2688, 384, BF16),
    ("cq2", 3072, 384, BF16), ("ck", 3456, 384, BF16), ("cv", 3840, 384, BF16), ("cf", 4224, 128, F32),
)
N_PAD = 4352
IN_RUNS = (
    (0, 384, 0), (384, 256, 384), (640 + 64, 32, 640),
    (768, 384, 672), (768 + 384, 320, 2016), (768 + 768, 320, 3301),
    (1920, 320, 1056), (2304, 320, 1376), (2688, 320, 1696),
    (3072, 320, 2336), (3456, 320, 2656), (3840, 320, 2976), (4224, 5, 3296),
)
OUT_RUNS = ((0, 384, 0), (384, 320, 384), (768, 320, 704))
U_PAD = 1152


def _cparams(sem=None, vmem=VMEM_LIMIT):
    return pltpu.CompilerParams(dimension_semantics=sem, vmem_limit_bytes=vmem)


def _after(dep, body, in_specs, args):
    if dep is None:
        return body, in_specs, args
    n = len(args)

    def ordered(*refs):
        return body(*refs[:n], *refs[n + 1:])

    return ordered, list(in_specs) + [pl.BlockSpec((8, 128), lambda *_: (0, 0))], list(args) + [dep]


def _pad_runs(w, runs, total, axis):
    order = sorted(runs)
    parts, pos = [], 0
    for off, wd, src in order:
        if off > pos:
            shp = list(w.shape)
            shp[axis] = off - pos
            parts.append(jnp.zeros(shp, w.dtype))
        parts.append(lax.slice_in_dim(w, src, src + wd, axis=axis))
        pos = off + wd
    if pos < total:
        shp = list(w.shape)
        shp[axis] = total - pos
        parts.append(jnp.zeros(shp, w.dtype))
    return jnp.concatenate(parts, axis=axis)


def _unpad_runs(w, runs, axis):
    order = sorted(runs, key=lambda r: r[2])
    return jnp.concatenate([lax.slice_in_dim(w, off, off + wd, axis=axis) for off, wd, _ in order], axis=axis)


def _sigmoid(x):
    return 1.0 / (1.0 + jnp.exp(-x))


N_CHIP = 4
ANY_SPEC = pl.BlockSpec(memory_space=pl.ANY)
MESH_ID = pl.DeviceIdType.MESH


def _gather(arrs, name, dep=None):
    n = len(arrs)
    nin = n + (dep is not None)

    def body(*refs):
        ins, outs = refs[:n], refs[nin:nin + n]
        send_sems, recv_sems, local_sems = refs[nin + n:]
        x, y, c = lax.axis_index("x"), lax.axis_index("y"), lax.axis_index("c")
        me, sib = (x, y, c), (x, y, 1 - c)
        chips = [(1 - x, y), (x, 1 - y), (1 - x, 1 - y)]

        def slot(px, py, pc):
            return 4 * px + 2 * py + pc

        def copy(a, k, block, to, src=None):
            dst = outs[a].at[slot(*block)]
            return pltpu.make_async_remote_copy(
                src_ref=dst if src is None else src, dst_ref=dst, send_sem=send_sems.at[a, k],
                recv_sem=recv_sems.at[a, k], device_id=to, device_id_type=MESH_ID)

        local = [pltpu.make_async_copy(ins[a], outs[a].at[slot(*me)], local_sems.at[a]) for a in range(n)]
        first = []
        for a in range(n):
            first.append(copy(a, 0, me, sib, src=ins[a]))
            first += [copy(a, 1 + j, me, (*chip, c), src=ins[a]) for j, chip in enumerate(chips)]
        for cp in local + first:
            cp.start()
        passed = []
        for j, chip in enumerate(chips):
            for a in range(n):
                copy(a, 1 + j, (*chip, c), me).wait_recv()
                fwd = copy(a, 4 + j, (*chip, c), sib)
                fwd.start()
                passed.append(fwd)
        for a in range(n):
            copy(a, 0, sib, me).wait_recv()
            for j, chip in enumerate(chips):
                copy(a, 4 + j, (*chip, 1 - c), me).wait_recv()
        for cp in first + passed:
            cp.wait_send()
        for cp in local:
            cp.wait()

    return pl.pallas_call(
        body, name=name, out_shape=[jax.ShapeDtypeStruct((N_DEV,) + a.shape, a.dtype) for a in arrs],
        in_specs=[ANY_SPEC] * nin, out_specs=[ANY_SPEC] * n,
        scratch_shapes=[pltpu.SemaphoreType.DMA((n, N_DEV - 1)), pltpu.SemaphoreType.DMA((n, N_DEV - 1)),
                        pltpu.SemaphoreType.DMA((n,))],
    )(*arrs, *([] if dep is None else [dep]))


def _pair_swap(arrs, name):
    n = len(arrs)

    def body(*refs):
        ins, outs = refs[:n], refs[n:2 * n]
        send_sems, recv_sems = refs[2 * n:]
        x, y, c = lax.axis_index("x"), lax.axis_index("y"), lax.axis_index("c")
        copies = []
        for a in range(n):
            for q in range(N_CHIP):
                cp = pltpu.make_async_remote_copy(
                    src_ref=ins[a].at[2 * q + 1 - c], dst_ref=outs[a].at[q], send_sem=send_sems.at[a, q],
                    recv_sem=recv_sems.at[a, q], device_id=(x, y, 1 - c), device_id_type=MESH_ID)
                cp.start()
                copies.append(cp)
        for cp in copies:
            cp.wait()

    return pl.pallas_call(
        body, name=name, out_shape=[jax.ShapeDtypeStruct((N_CHIP,) + a.shape[1:], a.dtype) for a in arrs],
        in_specs=[ANY_SPEC] * n, out_specs=[ANY_SPEC] * n,
        scratch_shapes=[pltpu.SemaphoreType.DMA((n, N_CHIP)), pltpu.SemaphoreType.DMA((n, N_CHIP))],
    )(*arrs)


def _chip_a2a(arrs, name):
    n = len(arrs)

    def body(*refs):
        ins, outs = refs[:n], refs[n:2 * n]
        send_sems, recv_sems, local_sems = refs[2 * n:]
        x, y, c = lax.axis_index("x"), lax.axis_index("y"), lax.axis_index("c")
        mine = 2 * x + y
        copies = []
        for a in range(n):
            loc = pltpu.make_async_copy(ins[a].at[mine], outs[a].at[mine], local_sems.at[a])
            loc.start()
            copies.append(loc)
            for k in range(1, N_CHIP):
                px = (1 - x) if (k >> 1) & 1 else x
                py = (1 - y) if k & 1 else y
                cp = pltpu.make_async_remote_copy(
                    src_ref=ins[a].at[2 * px + py], dst_ref=outs[a].at[mine], send_sem=send_sems.at[a, k - 1],
                    recv_sem=recv_sems.at[a, k - 1], device_id=(px, py, c), device_id_type=MESH_ID)
                cp.start()
                copies.append(cp)
        for cp in copies:
            cp.wait()

    return pl.pallas_call(
        body, name=name, out_shape=[jax.ShapeDtypeStruct(a.shape, a.dtype) for a in arrs],
        in_specs=[ANY_SPEC] * n, out_specs=[ANY_SPEC] * n,
        scratch_shapes=[pltpu.SemaphoreType.DMA((n, N_CHIP - 1)), pltpu.SemaphoreType.DMA((n, N_CHIP - 1)),
                        pltpu.SemaphoreType.DMA((n,))],
    )(*arrs)


HBM_SPEC = pl.BlockSpec(memory_space=pltpu.HBM)
SEM_SPEC = pl.BlockSpec(memory_space=pltpu.SEMAPHORE)
SPLIT_EFFECT = pltpu.SideEffectType.DATAFLOW_SIDE_EFFECTING
SPLIT_SEMS = {"gather": (N_DEV - 1, True), "pair": (N_CHIP, False), "chips": (N_CHIP - 1, True)}


def _split_descriptors(pattern, srcs, lands, sems):
    x, y, c = lax.axis_index("x"), lax.axis_index("y"), lax.axis_index("c")
    nsem, has_local = SPLIT_SEMS[pattern]
    per = 2 * nsem + int(has_local)
    starts, arrivals, local = [], [], []

    def remote(a, k, src, dst, to):
        return pltpu.make_async_remote_copy(src_ref=src, dst_ref=dst, send_sem=sems[a * per + k],
                                            recv_sem=sems[a * per + nsem + k], device_id=to, device_id_type=MESH_ID)

    for a in range(len(srcs)):
        if pattern == "gather":
            me = 4 * x + 2 * y + c
            local.append(pltpu.make_async_copy(srcs[a], lands[a].at[me], sems[a * per + 2 * nsem]))
            for k in range(1, N_DEV):
                px = (1 - x) if (k >> 2) & 1 else x
                py = (1 - y) if (k >> 1) & 1 else y
                pc = (1 - c) if k & 1 else c
                starts.append(remote(a, k - 1, srcs[a], lands[a].at[me], (px, py, pc)))
                arrivals.append(remote(a, k - 1, srcs[a], lands[a].at[4 * px + 2 * py + pc], (px, py, pc)))
        elif pattern == "pair":
            for q in range(N_CHIP):
                cp = remote(a, q, srcs[a].at[2 * q + 1 - c], lands[a].at[q], (x, y, 1 - c))
                starts.append(cp)
                arrivals.append(cp)
        else:
            mine = 2 * x + y
            local.append(pltpu.make_async_copy(srcs[a].at[mine], lands[a].at[mine], sems[a * per + 2 * nsem]))
            for k in range(1, N_CHIP):
                px = (1 - x) if (k >> 1) & 1 else x
                py = (1 - y) if k & 1 else y
                starts.append(remote(a, k - 1, srcs[a].at[2 * px + py], lands[a].at[mine], (px, py, c)))
                arrivals.append(remote(a, k - 1, srcs[a].at[2 * px + py], lands[a].at[2 * px + py], (px, py, c)))
    return starts, arrivals, local


def _split_start(pattern, arrs, name, after=None):
    n = len(arrs)
    extra = [] if after is None else [after]
    nsem, has_local = SPLIT_SEMS[pattern]
    if pattern == "gather":
        land_shapes = [(N_DEV,) + a.shape for a in arrs]
    elif pattern == "pair":
        land_shapes = [(N_CHIP,) + a.shape[1:] for a in arrs]
    else:
        land_shapes = [a.shape for a in arrs]
    nsem_out = n * (2 * nsem + int(has_local))

    def body(*refs):
        srcs, lands = refs[:n], refs[n:2 * n]
        first_sem = 2 * n + len(extra)
        sems = refs[first_sem:first_sem + nsem_out]
        token = refs[-1]
        starts, _, local = _split_descriptors(pattern, srcs, lands, sems)
        for cp in local + starts:
            cp.start()
        token[...] = jnp.zeros_like(token)

    out_shape = ([pltpu.SemaphoreType.DMA(())] * nsem_out + [pltpu.HBM(a.shape, a.dtype) for a in arrs]
                 + [pltpu.HBM(s, a.dtype) for s, a in zip(land_shapes, arrs)] + [jax.ShapeDtypeStruct((8, 128), F32)])
    ins = ([pltpu.with_memory_space_constraint(a, pltpu.HBM) for a in arrs]
           + [pltpu.with_memory_space_constraint(lax.empty(s, a.dtype), pltpu.HBM) for s, a in zip(land_shapes, arrs)])
    outs = pl.pallas_call(
        body, name=name, out_shape=out_shape, in_specs=[HBM_SPEC] * (2 * n) + [ANY_SPEC] * len(extra),
        out_specs=[SEM_SPEC] * nsem_out + [HBM_SPEC] * (2 * n) + [pl.BlockSpec(memory_space=pltpu.VMEM)],
        input_output_aliases={i: nsem_out + i for i in range(2 * n)},
        compiler_params=pltpu.CompilerParams(has_side_effects=SPLIT_EFFECT),
    )(*ins, *extra)
    handle = dict(pattern=pattern, n=n, sems=outs[:nsem_out], srcs=outs[nsem_out:nsem_out + n],
                  lands=outs[nsem_out + n:nsem_out + 2 * n])
    return handle, outs[-1]


def _split_wait(handle, after, name):
    pattern, n = handle["pattern"], handle["n"]
    nsem_in = len(handle["sems"])

    def body(*refs):
        srcs, lands = refs[:n], refs[n:2 * n]
        starts, arrivals, local = _split_descriptors(pattern, srcs, lands, refs[2 * n:2 * n + nsem_in])
        for cp in starts:
            cp.wait_send()
        for cp in arrivals:
            cp.wait_recv()
        for cp in local:
            cp.wait()

    srcs, lands = handle["srcs"], handle["lands"]
    outs = pl.pallas_call(
        body, name=name,
        out_shape=[pltpu.HBM(a.shape, a.dtype) for a in srcs] + [pltpu.HBM(a.shape, a.dtype) for a in lands],
        in_specs=[HBM_SPEC] * (2 * n) + [SEM_SPEC] * nsem_in + [ANY_SPEC], out_specs=[HBM_SPEC] * (2 * n),
        input_output_aliases={i: i for i in range(2 * n)},
        compiler_params=pltpu.CompilerParams(has_side_effects=SPLIT_EFFECT),
    )(*srcs, *lands, *handle["sems"], after)
    return outs[:n], outs[n:]


def _pair_add(core, a8, b4, name, tr):
    _, r, c = b4.shape

    def body(core_ref, a_ref, b_ref, o_ref):
        o_ref[...] = (a_ref[...] + b_ref[...]).astype(BF16)

    blk = pl.BlockSpec((1, tr, c), lambda q, i, core_ref: (q, i, 0))
    grid_spec = pltpu.PrefetchScalarGridSpec(
        num_scalar_prefetch=1, grid=(N_CHIP, r // tr),
        in_specs=[pl.BlockSpec((1, tr, c), lambda q, i, core_ref: (2 * q + core_ref[0], i, 0)), blk], out_specs=blk)
    return pl.pallas_call(
        body, name=name, grid_spec=grid_spec, out_shape=jax.ShapeDtypeStruct(b4.shape, BF16),
        compiler_params=_cparams(("arbitrary", "arbitrary")),
    )(core, a8, b4)


def _sum_slots(x, name):
    _, r, c = x.shape

    def body(x_ref, o_ref):
        acc = x_ref[0]
        for j in range(1, N_DEV):
            acc = acc + x_ref[j]
        o_ref[...] = acc

    return pl.pallas_call(body, name=name, out_shape=jax.ShapeDtypeStruct((r, c), F32))(x)


def _ada_fwd(c_all, w_ada, dep):
    nb = c_all.shape[0]
    cols = w_ada.shape[2]

    def body(c_ref, w_ref, dep_ref, act_ref, mod_ref):
        cv = c_ref[...]
        act = cv * _sigmoid(cv)
        act_ref[...] = act
        for l in range(DEPTH):
            mod_ref[l] = jnp.dot(act, w_ref[l], precision=HI, preferred_element_type=F32)

    return pl.pallas_call(
        body, name="ada_fwd",
        out_shape=[jax.ShapeDtypeStruct((nb, D_MODEL), F32), jax.ShapeDtypeStruct((DEPTH, nb, cols), F32)],
        compiler_params=_cparams(),
    )(c_all, w_ada, dep)


def _ada_bwd(c_act, dmod_all, dmod_mine):
    nb = c_act.shape[0]
    cols = dmod_mine.shape[2]

    def body(act_ref, dall_ref, dmine_ref, gw_ref, gb_ref):
        act = act_ref[...]
        for l in range(DEPTH):
            gw_ref[l] = lax.dot_general(act, dmine_ref[l], (((0,), (0,)), ((), ())),
                                        precision=HI, preferred_element_type=F32)
            gb_ref[l:l + 1, :] = jnp.sum(dall_ref[l], axis=0, keepdims=True)

    return pl.pallas_call(
        body, name="ada_bwd",
        out_shape=[jax.ShapeDtypeStruct((DEPTH, D_MODEL, cols), F32),
                   jax.ShapeDtypeStruct((DEPTH, 3 * D_MODEL), F32)],
        compiler_params=_cparams(),
    )(c_act, dmod_all, dmod_mine)


def _ln_in(x, ss, g, w, seq, tm=256, dep=None):
    t = x.shape[0]
    tps = seq // tm

    def body(x_ref, ss_ref, g_ref, w_ref, h_ref, *outs):
        xv = x_ref[...]
        xn = xv * lax.rsqrt(jnp.mean(xv * xv, axis=-1, keepdims=True) + EPS)
        h = xn * g_ref[...] * ss_ref[0, 1:2, :] + ss_ref[0, 0:1, :]
        hb = h.astype(BF16)
        h_ref[...] = hb
        z = jnp.dot(hb, w_ref[...], preferred_element_type=F32)
        for o_ref, (_, off, wd, _) in zip(outs, Z_SEGS):
            o_ref[...] = z[:, off:off + wd].astype(o_ref.dtype)

    row = lambda wd: pl.BlockSpec((tm, wd), lambda i: (i, 0))
    in_specs = [row(D_MODEL), pl.BlockSpec((1, 2, D_MODEL), lambda i: (i // tps, 0, 0)),
                pl.BlockSpec((1, D_MODEL), lambda i: (0, 0)), pl.BlockSpec((D_MODEL, N_PAD), lambda i: (0, 0))]
    body, in_specs, args = _after(dep, body, in_specs, [x, ss, g, w])
    return pl.pallas_call(
        body, name="ln_in", grid=(t // tm,), in_specs=in_specs,
        out_specs=[row(D_MODEL)] + [row(wd) for _, _, wd, _ in Z_SEGS],
        out_shape=[jax.ShapeDtypeStruct((t, D_MODEL), BF16)]
        + [jax.ShapeDtypeStruct((t, wd), dt) for _, _, wd, dt in Z_SEGS],
        compiler_params=_cparams(("arbitrary",)),
    )(*args)


def _ln_in_bwd(dz, w_t, x, ss, g, dxo, seq, tm=256, dep=None):
    t = x.shape[0]
    tps = seq // tm
    nb = t // seq
    nz = len(Z_SEGS)

    def body(*refs):
        dz_refs = refs[:nz]
        wt_ref, x_ref, ss_ref, g_ref, dxo_ref, dx_ref, dss_ref, dg_ref = refs[nz:]
        i = pl.program_id(0)
        dzc = jnp.concatenate([r[...].astype(BF16) for r in dz_refs], axis=1)
        dh = jnp.dot(dzc, wt_ref[...], preferred_element_type=F32)
        xv = x_ref[...]
        rstd = lax.rsqrt(jnp.mean(xv * xv, axis=-1, keepdims=True) + EPS)
        xn = xv * rstd
        gv = g_ref[...]
        s1 = ss_ref[0, 1:2, :]
        dxg = dh * s1
        dxn = dxg * gv
        dx = rstd * (dxn - xn * jnp.mean(dxn * xn, axis=-1, keepdims=True))
        dx_ref[...] = dxo_ref[...] + dx
        dshift = jnp.sum(dh, axis=0, keepdims=True)
        dscale = jnp.sum(dh * (xn * gv), axis=0, keepdims=True)
        dgp = jnp.sum(dxg * xn, axis=0, keepdims=True)

        @pl.when(i % tps == 0)
        def _():
            dss_ref[0, 0:1, :] = dshift
            dss_ref[0, 1:2, :] = dscale

        @pl.when(i % tps != 0)
        def _():
            dss_ref[0, 0:1, :] += dshift
            dss_ref[0, 1:2, :] += dscale

        @pl.when(i == 0)
        def _():
            dg_ref[...] = dgp

        @pl.when(i != 0)
        def _():
            dg_ref[...] += dgp

    row = lambda wd: pl.BlockSpec((tm, wd), lambda i: (i, 0))
    in_specs = ([row(wd) for _, _, wd, _ in Z_SEGS]
                + [pl.BlockSpec((N_PAD, D_MODEL), lambda i: (0, 0)), row(D_MODEL),
                   pl.BlockSpec((1, 2, D_MODEL), lambda i: (i // tps, 0, 0)),
                   pl.BlockSpec((1, D_MODEL), lambda i: (0, 0)), row(D_MODEL)])
    body, in_specs, args = _after(dep, body, in_specs, [*dz, w_t, x, ss, g, dxo])
    return pl.pallas_call(
        body, name="ln_in_bwd", grid=(t // tm,), in_specs=in_specs,
        out_specs=[row(D_MODEL), pl.BlockSpec((1, 2, D_MODEL), lambda i: (i // tps, 0, 0)),
                   pl.BlockSpec((1, D_MODEL), lambda i: (0, 0))],
        out_shape=[jax.ShapeDtypeStruct((t, D_MODEL), F32), jax.ShapeDtypeStruct((nb, 2, D_MODEL), F32),
                   jax.ShapeDtypeStruct((1, D_MODEL), F32)],
        compiler_params=_cparams(("arbitrary",)),
    )(*args)


def _matmul_tn(a, bs, name, tm=1024):
    bs = list(bs) if isinstance(bs, (list, tuple)) else [bs]
    t, k = a.shape
    widths = [b.shape[1] for b in bs]
    n = sum(widths)
    tm = min(tm, t)

    def body(a_ref, *refs):
        b_refs, o_ref = refs[:-1], refs[-1]
        i = pl.program_id(0)
        av = a_ref[...].astype(BF16)
        parts = [b_ref[...].astype(BF16) for b_ref in b_refs]
        bv = parts[0] if len(parts) == 1 else jnp.concatenate(parts, axis=1)
        part = lax.dot_general(av, bv, (((0,), (0,)), ((), ())), preferred_element_type=F32)

        @pl.when(i == 0)
        def _():
            o_ref[...] = part

        @pl.when(i != 0)
        def _():
            o_ref[...] += part

    return pl.pallas_call(
        body, name=name, grid=(t // tm,),
        in_specs=[pl.BlockSpec((tm, k), lambda i: (i, 0))] + [pl.BlockSpec((tm, wd), lambda i: (i, 0)) for wd in widths],
        out_specs=pl.BlockSpec((k, n), lambda i: (0, 0)),
        out_shape=jax.ShapeDtypeStruct((k, n), F32),
        compiler_params=_cparams(("arbitrary",)),
    )(a, *bs)


def _rope(blk, cos_t, sin_a, sin_b):
    return blk * cos_t + pltpu.roll(blk, 112, 1) * sin_a + pltpu.roll(blk, 16, 1) * sin_b


def _unrope(d, cos_t, sin_a, sin_b):
    return d * cos_t + pltpu.roll(d * sin_a, 16, 1) + pltpu.roll(d * sin_b, 112, 1)


def _mla_prep(cq, ckv, kpe, gq, gkv, wuq, wk, wv, cos_t, sin_a, sin_b, tm=256):
    t = cq.shape[0]
    qw = A_HEADS * HEAD_PAD

    def body(cq_ref, ckv_ref, kpe_ref, gq_ref, gkv_ref, wuq_ref, wk_ref, wv_ref, c_ref, sa_ref, sb_ref,
             q_ref, k_ref, v_ref, cqn_ref, ckvn_ref):
        ct, sa, sb = c_ref[...], sa_ref[...], sb_ref[...]
        a = cq_ref[...]
        cqn = (a * lax.rsqrt(jnp.mean(a * a, axis=-1, keepdims=True) + EPS) * gq_ref[...]).astype(BF16)
        cqn_ref[...] = cqn
        b = ckv_ref[...]
        ckvn = (b * lax.rsqrt(jnp.mean(b * b, axis=-1, keepdims=True) + EPS) * gkv_ref[...]).astype(BF16)
        ckvn_ref[...] = ckvn
        qlin = jnp.dot(cqn, wuq_ref[...], preferred_element_type=F32)
        klin = jnp.dot(ckvn, wk_ref[...], preferred_element_type=F32)
        v_ref[...] = jnp.dot(ckvn, wv_ref[...], preferred_element_type=F32).astype(BF16)
        kr = _rope(kpe_ref[...], ct, sa, sb)
        for h in range(A_HEADS):
            sl = slice(h * HEAD_PAD, (h + 1) * HEAD_PAD)
            q_ref[:, sl] = _rope(qlin[:, sl], ct, sa, sb).astype(BF16)
            k_ref[:, sl] = (klin[:, sl] + kr).astype(BF16)

    row = lambda wd: pl.BlockSpec((tm, wd), lambda i: (i, 0))
    full = lambda r, c: pl.BlockSpec((r, c), lambda i: (0, 0))
    return pl.pallas_call(
        body, name="mla_prep", grid=(t // tm,),
        in_specs=[row(A_Q_RANK), row(A_KV_RANK), row(128), full(1, A_Q_RANK), full(1, A_KV_RANK),
                  full(A_Q_RANK, qw), full(A_KV_RANK, qw), full(A_KV_RANK, GW), row(128), row(128), row(128)],
        out_specs=[row(qw), row(qw), row(GW), row(A_Q_RANK), row(A_KV_RANK)],
        out_shape=[jax.ShapeDtypeStruct((t, qw), BF16), jax.ShapeDtypeStruct((t, qw), BF16),
                   jax.ShapeDtypeStruct((t, GW), BF16), jax.ShapeDtypeStruct((t, A_Q_RANK), BF16),
                   jax.ShapeDtypeStruct((t, A_KV_RANK), BF16)],
        compiler_params=_cparams(("arbitrary",)),
    )(cq, ckv, kpe, gq, gkv, wuq, wk, wv, cos_t, sin_a, sin_b)


def _mla_prep_bwd(dq, dk, dv, cq, ckv, gq, gkv, wuq_t, wk_t, wv_t, cos_t, sin_a, sin_b, tm=256):
    t = cq.shape[0]
    qw = A_HEADS * HEAD_PAD

    def body(dq_ref, dk_ref, dv_ref, cq_ref, ckv_ref, gq_ref, gkv_ref, wuqt_ref, wkt_ref, wvt_ref,
             c_ref, sa_ref, sb_ref, dcq_ref, dckv_ref, dkpe_ref, dql_ref, dkl_ref, dgq_ref, dgkv_ref):
        i = pl.program_id(0)
        ct, sa, sb = c_ref[...], sa_ref[...], sb_ref[...]
        lane = lax.broadcasted_iota(jnp.int32, (1, HEAD_PAD), 1)
        nope = lane < A_NOPE
        rope = (lane >= A_NOPE) & (lane < A_NOPE + A_ROPE)
        dksum = None
        for h in range(A_HEADS):
            sl = slice(h * HEAD_PAD, (h + 1) * HEAD_PAD)
            dql_ref[:, sl] = _unrope(dq_ref[:, sl], ct, sa, sb).astype(BF16)
            dkh = dk_ref[:, sl]
            dkl_ref[:, sl] = jnp.where(nope, dkh, 0.0).astype(BF16)
            dksum = dkh if dksum is None else dksum + dkh
        dkpe_ref[...] = jnp.where(rope, _unrope(jnp.where(rope, dksum, 0.0), ct, sa, sb), 0.0)
        dcqn = jnp.dot(dql_ref[...], wuqt_ref[...], preferred_element_type=F32)
        dckvn = (jnp.dot(dkl_ref[...], wkt_ref[...], preferred_element_type=F32)
                 + jnp.dot(dv_ref[...].astype(BF16), wvt_ref[...], preferred_element_type=F32))

        def norm_bwd(xv, gv, dy):
            rstd = lax.rsqrt(jnp.mean(xv * xv, axis=-1, keepdims=True) + EPS)
            xn = xv * rstd
            dxn = dy * gv
            dx = rstd * (dxn - xn * jnp.mean(dxn * xn, axis=-1, keepdims=True))
            return dx, jnp.sum(dy * xn, axis=0, keepdims=True)

        dcq, dgq = norm_bwd(cq_ref[...], gq_ref[...], dcqn)
        dckv, dgkv = norm_bwd(ckv_ref[...], gkv_ref[...], dckvn)
        dcq_ref[...] = dcq
        dckv_ref[...] = dckv

        @pl.when(i == 0)
        def _():
            dgq_ref[...] = dgq
            dgkv_ref[...] = dgkv

        @pl.when(i != 0)
        def _():
            dgq_ref[...] += dgq
            dgkv_ref[...] += dgkv

    row = lambda wd: pl.BlockSpec((tm, wd), lambda i: (i, 0))
    full = lambda r, c: pl.BlockSpec((r, c), lambda i: (0, 0))
    return pl.pallas_call(
        body, name="mla_prep_bwd", grid=(t // tm,),
        in_specs=[row(qw), row(qw), row(GW), row(A_Q_RANK), row(A_KV_RANK), full(1, A_Q_RANK), full(1, A_KV_RANK),
                  full(qw, A_Q_RANK), full(qw, A_KV_RANK), full(GW, A_KV_RANK), row(128), row(128), row(128)],
        out_specs=[row(A_Q_RANK), row(A_KV_RANK), row(128), row(qw), row(qw), full(1, A_Q_RANK), full(1, A_KV_RANK)],
        out_shape=[jax.ShapeDtypeStruct((t, A_Q_RANK), F32), jax.ShapeDtypeStruct((t, A_KV_RANK), F32),
                   jax.ShapeDtypeStruct((t, 128), F32), jax.ShapeDtypeStruct((t, qw), BF16),
                   jax.ShapeDtypeStruct((t, qw), BF16), jax.ShapeDtypeStruct((1, A_Q_RANK), F32),
                   jax.ShapeDtypeStruct((1, A_KV_RANK), F32)],
        compiler_params=_cparams(("arbitrary",)),
    )(dq, dk, dv, cq, ckv, gq, gkv, wuq_t, wk_t, wv_t, cos_t, sin_a, sin_b)


def _nt(a, b):
    return lax.dot_general(a, b, (((1,), (1,)), ((), ())), preferred_element_type=F32)


def _tn(a, b):
    return lax.dot_general(a, b, (((0,), (0,)), ((), ())), preferred_element_type=F32)


def _causal_mask(kind, q0, k0, tq, tk):
    qpos = q0 + lax.broadcasted_iota(jnp.int32, (tq, tk), 0)
    kpos = k0 + lax.broadcasted_iota(jnp.int32, (tq, tk), 1)
    if kind == "mla":
        return lax.shift_right_logical(kpos, 6) <= lax.shift_right_logical(qpos, 6)
    return kpos <= qpos


def _attn_fwd(kind, q, k, v, f, seq, scale, tq=512, tk=512):
    t = v.shape[0]
    nb = t // seq
    nq = seq // tq
    hw = 256 if kind == "mla" else 128
    use_f = f is not None
    tq, tk = min(tq, seq), min(tk, seq)
    nq = seq // tq
    assert tk % tq == 0

    def body(*refs):
        if use_f:
            q_ref, k_ref, v_ref, f_ref, o_ref, st_ref = refs
        else:
            q_ref, k_ref, v_ref, o_ref, st_ref = refs
        qi = pl.program_id(2)
        q0 = qi * tq
        lane = lax.broadcasted_iota(jnp.int32, (1, 128), 1)
        half = lane >= 64
        qall = q_ref[...]
        if kind == "mla":
            qhs = [qall[:, 0:128], qall[:, 128:256]]
        else:
            qhs = [jnp.where(half, jnp.zeros_like(qall), qall), jnp.where(half, qall, jnp.zeros_like(qall))]
        nfull = q0 // tk
        kd = pl.multiple_of(nfull * tk, tk)
        diag = _causal_mask(kind, q0 - kd, 0, tq, tk)

        def block(j, k0, state, masked):
            m, l, acc = state
            kh = k_ref[pl.ds(k0, tk), j * 128:(j + 1) * 128] if kind == "mla" else k_ref[pl.ds(k0, tk), :]
            s = _nt(qhs[j], kh) * scale
            if use_f:
                s = s - f_ref[0, 0, j:j + 1, pl.ds(k0, tk)]
            if masked:
                s = jnp.where(diag, s, NEG)
            mn = jnp.maximum(m, jnp.max(s, axis=-1, keepdims=True))
            alpha = jnp.exp(m - mn)
            p = jnp.exp(s - mn)
            l = alpha * l + jnp.sum(p, axis=-1, keepdims=True)
            acc = alpha * acc + jnp.dot(p.astype(BF16), v_ref[pl.ds(k0, tk), :], preferred_element_type=F32)
            return mn, l, acc

        def kstep(kb, carry):
            k0 = pl.multiple_of(kb * tk, tk)
            return block(0, k0, carry[:3], False) + block(1, k0, carry[3:], False)

        init = (jnp.full((tq, 1), NEG, F32), jnp.zeros((tq, 1), F32), jnp.zeros((tq, 128), F32)) * 2
        carry = lax.fori_loop(0, nfull, kstep, init)
        m0, l0, a0 = block(0, kd, carry[:3], True)
        m1, l1, a1 = block(1, kd, carry[3:], True)
        o_ref[...] = jnp.where(half, a1 / l1, a0 / l0)
        st_ref[...] = jnp.where(lane == 0, m0 + jnp.log(l0), jnp.where(lane == 1, m1 + jnp.log(l1), 0.0))

    in_specs = [pl.BlockSpec((tq, hw), lambda b, p, i: (b * nq + i, p)),
                pl.BlockSpec((seq, hw), lambda b, p, i: (b, p)),
                pl.BlockSpec((seq, 128), lambda b, p, i: (b, p))]
    args = [q, k, v]
    if use_f:
        in_specs.append(pl.BlockSpec((1, 1, 8, seq), lambda b, p, i: (b, p, 0, 0)))
        args.append(f)
    oblk = pl.BlockSpec((tq, 128), lambda b, p, i: (b * nq + i, p))
    return pl.pallas_call(
        body, name="attn_fwd_" + kind, grid=(nb, 3, nq), in_specs=in_specs, out_specs=[oblk, oblk],
        out_shape=[jax.ShapeDtypeStruct((t, GW), F32), jax.ShapeDtypeStruct((t, GW), F32)],
        compiler_params=_cparams(("arbitrary", "arbitrary", "arbitrary")),
    )(*args)


def _attn_bwd(kind, q, k, v, f, o, st, do, seq, scale, tq=512, tk=512, dep=None):
    t = v.shape[0]
    nb = t // seq
    tq, tk = min(tq, seq), min(tk, seq)
    nq = seq // tq
    nk = seq // tk
    hw = 256 if kind == "mla" else 128
    use_f = f is not None
    assert tq == tk

    def body(*refs):
        if use_f:
            q_ref, k_ref, v_ref, f_ref, o_ref, st_ref, do_ref, dq_ref, dk_ref, dv_ref, df_ref, dfq_ref = refs
        else:
            q_ref, k_ref, v_ref, o_ref, st_ref, do_ref, dq_ref, dk_ref, dv_ref = refs
        kj = pl.program_id(2)
        k0 = kj * tk
        lane = lax.broadcasted_iota(jnp.int32, (1, 128), 1)
        half = lane >= 64

        @pl.when(kj == 0)
        def _():
            dq_ref[...] = jnp.zeros_like(dq_ref)
            if use_f:
                dfq_ref[...] = jnp.zeros_like(dfq_ref)

        dk_ref[...] = jnp.zeros_like(dk_ref)
        dv_ref[...] = jnp.zeros_like(dv_ref)
        if use_f:
            df_ref[...] = jnp.zeros_like(df_ref)
        vv = v_ref[...]
        diag = _causal_mask(kind, 0, 0, tq, tk)

        def qstep(qi, masked):
            q0 = pl.multiple_of(qi * tq, tq)
            rows = pl.ds(q0, tq)
            dov = do_ref[rows, :]
            dd = dov * o_ref[rows, :]
            stv = st_ref[rows, :]
            for j in range(2):
                hm = half == bool(j)
                delta = jnp.sum(jnp.where(hm, dd, 0.0), axis=-1, keepdims=True)
                lse = stv[:, j:j + 1]
                if kind == "mla":
                    cols = slice(j * 128, (j + 1) * 128)
                    qh = q_ref[rows, cols]
                    kh = k_ref[:, cols]
                else:
                    cols = slice(0, 128)
                    qa = q_ref[rows, :]
                    qh = jnp.where(hm, qa, jnp.zeros_like(qa))
                    kh = k_ref[...]
                s = _nt(qh, kh) * scale
                if use_f:
                    s = s - f_ref[0, 0, j:j + 1, :]
                if masked:
                    s = jnp.where(diag, s, NEG)
                p = jnp.exp(s - lse)
                doh = jnp.where(hm, dov, 0.0).astype(BF16)
                ds = p * (_nt(doh, vv) - delta)
                dsb = (ds * scale).astype(BF16)
                dv_ref[...] += _tn(p.astype(BF16), doh)
                dk_ref[:, cols] += _tn(dsb, qh)
                dqc = jnp.dot(dsb, kh, preferred_element_type=F32)
                if kind != "mla":
                    dqc = jnp.where(hm, dqc, 0.0)
                dq_ref[rows, cols] += dqc
                if use_f:
                    df_ref[0, 0, j:j + 1, :] += -jnp.sum(ds, axis=0, keepdims=True)
                    dfq_ref[rows, :] += jnp.where(lane == j, jnp.sum(ds, axis=-1, keepdims=True), 0.0)

        qstep(kj, True)

        def rest(qi, carry):
            qstep(qi, False)
            return carry

        lax.fori_loop(kj + 1, nq, rest, 0)

    full_q = lambda wd: pl.BlockSpec((seq, wd), lambda b, p, i: (b, p))
    kblk = lambda wd: pl.BlockSpec((tk, wd), lambda b, p, i: (b * nk + i, p))
    in_specs = [full_q(hw), kblk(hw), kblk(128)]
    args = [q, k, v]
    if use_f:
        in_specs.append(pl.BlockSpec((1, 1, 8, tk), lambda b, p, i: (b, p, 0, i)))
        args.append(f)
    in_specs += [full_q(128), full_q(128), full_q(128)]
    args += [o, st, do]
    out_specs = [full_q(hw), kblk(hw), kblk(128)]
    out_shape = [jax.ShapeDtypeStruct((t, 3 * hw), F32), jax.ShapeDtypeStruct((t, 3 * hw), F32),
                 jax.ShapeDtypeStruct((t, GW), F32)]
    if use_f:
        out_specs += [pl.BlockSpec((1, 1, 8, tk), lambda b, p, i: (b, p, 0, i)), full_q(128)]
        out_shape += [jax.ShapeDtypeStruct((nb, 3, 8, seq), F32), jax.ShapeDtypeStruct((t, GW), F32)]
    body, in_specs, args = _after(dep, body, in_specs, args)
    return pl.pallas_call(
        body, name="attn_bwd_" + kind, grid=(nb, 3, nk), in_specs=in_specs, out_specs=out_specs,
        out_shape=out_shape, compiler_params=_cparams(("arbitrary", "arbitrary", "arbitrary")),
    )(*args)


BQ = 256
BWIN = BQ + B_LEFT


def _band_geometry():
    r = lax.broadcasted_iota(jnp.int32, (BQ, BWIN), 0)
    j = lax.broadcasted_iota(jnp.int32, (BQ, BWIN), 1)
    rc = lax.shift_right_logical(r, 6)
    jc = lax.shift_right_logical(j, 6)
    allowed = (jc - 8 <= rc) & (rc <= jc)
    return (r + B_LEFT - j) >= REL_CLIP, allowed, j < r


def _band_onehot(transposed, offset=0):
    shape = (BWIN, GW) if transposed else (GW, BWIN)
    kk = lax.broadcasted_iota(jnp.int32, shape, 1 if transposed else 0)
    x = lax.broadcasted_iota(jnp.int32, shape, 0 if transposed else 1) - offset
    x = jnp.where(x < 0, x + BWIN, x)
    return (kk == jnp.clip(B_LEFT - x, -REL_CLIP, REL_CLIP) + REL_CLIP).astype(F32)


def _band_table(rel_bias8):
    def body(b_ref, o_ref):
        hh = pl.program_id(0)
        u8 = jnp.dot(b_ref[...], _band_onehot(False), precision=HI, preferred_element_type=F32)
        rid = lax.broadcasted_iota(jnp.int32, (8, BWIN), 0)
        row = jnp.sum(jnp.where(rid == hh, u8, 0.0), axis=0, keepdims=True)
        far, allowed, _ = _band_geometry()
        tbl = pltpu.roll(jnp.broadcast_to(row, (BQ, BWIN)), 0, 1, stride=1, stride_axis=0)
        tbl = jnp.where(far, row[:, 0:1], tbl)
        o_ref[0] = jnp.where(allowed, tbl, NEG)

    return pl.pallas_call(
        body, name="band_table", grid=(6,),
        in_specs=[pl.BlockSpec((8, GW), lambda h: (0, 0))],
        out_specs=pl.BlockSpec((1, BQ, BWIN), lambda h: (h, 0, 0)),
        out_shape=jax.ShapeDtypeStruct((6, BQ, BWIN), F32),
        compiler_params=_cparams(("arbitrary",)),
    )(rel_bias8)


def _band_table_bwd(gtab):
    def body(g_ref, o_ref):
        gv = g_ref[0]
        _, _, wrapped = _band_geometry()
        gfar = jnp.sum(jnp.sum(jnp.where(wrapped, gv, 0.0), axis=-1, keepdims=True), axis=0, keepdims=True)
        anti = (lax.broadcasted_iota(jnp.int32, (BQ, BQ), 0) + lax.broadcasted_iota(jnp.int32, (BQ, BQ), 1)
                == BQ - 1).astype(F32)
        grev = jnp.dot(anti, jnp.where(wrapped, 0.0, gv), precision=HI, preferred_element_type=F32)
        near = pltpu.roll(grev, 0, 1, stride=1, stride_axis=0)
        y = jnp.broadcast_to(jnp.sum(near, axis=0, keepdims=True), (8, BWIN))
        gb = jnp.dot(y, _band_onehot(True, BQ - 1), precision=HI, preferred_element_type=F32)
        lane = lax.broadcasted_iota(jnp.int32, (8, GW), 1)
        o_ref[0] = gb + jnp.where(lane == 2 * REL_CLIP, gfar, 0.0)

    return pl.pallas_call(
        body, name="band_table_bwd", grid=(B_HEADS,),
        in_specs=[pl.BlockSpec((1, BQ, BWIN), lambda h: (h, 0, 0))],
        out_specs=pl.BlockSpec((1, 8, GW), lambda h: (h, 0, 0)),
        out_shape=jax.ShapeDtypeStruct((B_HEADS, 8, GW), F32),
        compiler_params=_cparams(("arbitrary",)),
    )(gtab)


def _band_fwd(q, k, v, table, seq, scale):
    t = q.shape[0]
    nb = t // seq
    nq = seq // BQ

    def body(q_ref, k_ref, v_ref, tb_ref, o_ref, st_ref, kpad, vpad):
        qi = pl.program_id(2)
        q0 = pl.multiple_of(qi * BQ, BQ)
        lane = lax.broadcasted_iota(jnp.int32, (1, 128), 1)
        half = lane >= 64

        @pl.when(qi == 0)
        def _():
            kpad[0:B_LEFT, :] = jnp.zeros((B_LEFT, 128), BF16)
            vpad[0:B_LEFT, :] = jnp.zeros((B_LEFT, 128), BF16)
            kpad[B_LEFT:, :] = k_ref[...]
            vpad[B_LEFT:, :] = v_ref[...]

        kw = kpad[pl.ds(q0, BWIN), :]
        vw = vpad[pl.ds(q0, BWIN), :]
        inside = lax.broadcasted_iota(jnp.int32, (BQ, BWIN), 1) >= B_LEFT - q0
        qall = q_ref[...]
        outs, lses = [], []
        for j in range(2):
            qh = jnp.where(half == bool(j), qall, jnp.zeros_like(qall))
            s = jnp.where(inside, _nt(qh, kw) * scale + tb_ref[j], NEG)
            m = jnp.max(s, axis=-1, keepdims=True)
            p = jnp.exp(s - m)
            l = jnp.sum(p, axis=-1, keepdims=True)
            outs.append(jnp.dot(p.astype(BF16), vw, preferred_element_type=F32) / l)
            lses.append(m + jnp.log(l))
        o_ref[...] = jnp.where(half, outs[1], outs[0])
        st_ref[...] = jnp.where(lane == 0, lses[0], jnp.where(lane == 1, lses[1], 0.0))

    qblk = pl.BlockSpec((BQ, 128), lambda b, p, i: (b * nq + i, p))
    full = pl.BlockSpec((seq, 128), lambda b, p, i: (b, p))
    return pl.pallas_call(
        body, name="band_fwd", grid=(nb, 3, nq),
        in_specs=[qblk, full, full, pl.BlockSpec((2, BQ, BWIN), lambda b, p, i: (p, 0, 0))],
        out_specs=[qblk, qblk],
        out_shape=[jax.ShapeDtypeStruct((t, GW), F32), jax.ShapeDtypeStruct((t, GW), F32)],
        scratch_shapes=[pltpu.VMEM((seq + B_LEFT, 128), BF16), pltpu.VMEM((seq + B_LEFT, 128), BF16)],
        compiler_params=_cparams(("arbitrary", "arbitrary", "arbitrary")),
    )(q, k, v, table)


def _band_bwd(q, k, v, table, o, st, do, seq, scale, dep=None):
    t = q.shape[0]
    nb = t // seq
    nq = seq // BQ

    def body(q_ref, k_ref, v_ref, tb_ref, o_ref, st_ref, do_ref, dq_ref, dk_ref, dv_ref, g_ref,
             kpad, vpad, dkpad, dvpad):
        b = pl.program_id(1)
        qi = pl.program_id(2)
        q0 = pl.multiple_of(qi * BQ, BQ)
        lane = lax.broadcasted_iota(jnp.int32, (1, 128), 1)
        half = lane >= 64

        @pl.when(qi == 0)
        def _():
            kpad[0:B_LEFT, :] = jnp.zeros((B_LEFT, 128), BF16)
            vpad[0:B_LEFT, :] = jnp.zeros((B_LEFT, 128), BF16)
            kpad[B_LEFT:, :] = k_ref[...]
            vpad[B_LEFT:, :] = v_ref[...]
            dkpad[...] = jnp.zeros_like(dkpad)
            dvpad[...] = jnp.zeros_like(dvpad)

        @pl.when((qi == 0) & (b == 0))
        def _():
            g_ref[...] = jnp.zeros_like(g_ref)

        win = pl.ds(q0, BWIN)
        kw = kpad[win, :]
        vw = vpad[win, :]
        inside = lax.broadcasted_iota(jnp.int32, (BQ, BWIN), 1) >= B_LEFT - q0
        qall = q_ref[...]
        dov = do_ref[...]
        dd = dov * o_ref[...]
        stv = st_ref[...]
        dq = jnp.zeros((BQ, 128), F32)
        for j in range(2):
            hm = half == bool(j)
            qh = jnp.where(hm, qall, jnp.zeros_like(qall))
            delta = jnp.sum(jnp.where(hm, dd, 0.0), axis=-1, keepdims=True)
            s = jnp.where(inside, _nt(qh, kw) * scale + tb_ref[j], NEG)
            p = jnp.exp(s - stv[:, j:j + 1])
            doh = jnp.where(hm, dov, 0.0).astype(BF16)
            ds = p * (_nt(doh, vw) - delta)
            g_ref[j] += ds
            dsb = (ds * scale).astype(BF16)
            dvpad[win, :] += _tn(p.astype(BF16), doh)
            dkpad[win, :] += _tn(dsb, qh)
            dq = dq + jnp.where(hm, jnp.dot(dsb, kw, preferred_element_type=F32), 0.0)
        dq_ref[...] = dq

        @pl.when(qi == nq - 1)
        def _():
            dk_ref[...] = dkpad[B_LEFT:, :]
            dv_ref[...] = dvpad[B_LEFT:, :]

    qblk = pl.BlockSpec((BQ, 128), lambda p, b, i: (b * nq + i, p))
    full = pl.BlockSpec((seq, 128), lambda p, b, i: (b, p))
    tblk = pl.BlockSpec((2, BQ, BWIN), lambda p, b, i: (p, 0, 0))
    body, in_specs, args = _after(dep, body, [qblk, full, full, tblk, qblk, qblk, qblk], [q, k, v, table, o, st, do])
    return pl.pallas_call(
        body, name="band_bwd", grid=(3, nb, nq),
        in_specs=in_specs,
        out_specs=[qblk, full, full, tblk],
        out_shape=[jax.ShapeDtypeStruct((t, GW), F32), jax.ShapeDtypeStruct((t, GW), F32),
                   jax.ShapeDtypeStruct((t, GW), F32), jax.ShapeDtypeStruct((6, BQ, BWIN), F32)],
        scratch_shapes=[pltpu.VMEM((seq + B_LEFT, 128), BF16), pltpu.VMEM((seq + B_LEFT, 128), BF16),
                        pltpu.VMEM((seq + B_LEFT, 128), F32), pltpu.VMEM((seq + B_LEFT, 128), F32)],
        compiler_params=_cparams(("arbitrary", "arbitrary", "arbitrary")),
    )(*args)


def _fox_prep(cf, fb, seq):
    nb = cf.shape[0] // seq
    nblk = seq // 128

    def body(cf_ref, fb_ref, f_ref):
        x = cf_ref[...] + fb_ref[...]
        lf = jnp.minimum(x, 0.0) - jnp.log1p(jnp.exp(-jnp.abs(x)))
        rows = lf.T[0:8, :]
        upper = (lax.broadcasted_iota(jnp.int32, (128, 128), 0)
                 <= lax.broadcasted_iota(jnp.int32, (128, 128), 1)).astype(F32)
        carry = jnp.zeros((8, 1), F32)
        for blk in range(nblk):
            sl = slice(blk * 128, (blk + 1) * 128)
            cs = jnp.dot(rows[:, sl], upper, precision=HI, preferred_element_type=F32) + carry
            carry = cs[:, 127:128]
            f_ref[0, 0, :, sl] = cs
            f_ref[0, 1, :, sl] = pltpu.roll(cs, 6, 0)
            f_ref[0, 2, :, sl] = pltpu.roll(cs, 4, 0)

    return pl.pallas_call(
        body, name="fox_prep", grid=(nb,),
        in_specs=[pl.BlockSpec((seq, 128), lambda b: (b, 0)), pl.BlockSpec((1, 128), lambda b: (0, 0))],
        out_specs=pl.BlockSpec((1, 3, 8, seq), lambda b: (b, 0, 0, 0)),
        out_shape=jax.ShapeDtypeStruct((nb, 3, 8, seq), F32),
        compiler_params=_cparams(("arbitrary",)),
    )(cf, fb)


def _fox_prep_bwd(df, dfq, cf, fb, seq):
    nb = cf.shape[0] // seq
    nblk = seq // 128

    def body(df_ref, dfq_ref, cf_ref, fb_ref, dcf_ref, dfb_ref, wide):
        b = pl.program_id(0)
        row = lax.broadcasted_iota(jnp.int32, (8, seq), 0)
        dfh = None
        for p in range(3):
            both = df_ref[0, p] + dfq_ref[:, p * 128:(p + 1) * 128].T[0:8, :]
            both = jnp.where(row < 2, both, 0.0)
            if p:
                both = pltpu.roll(both, 2 * p, 0)
            dfh = both if dfh is None else dfh + both
        lower = (lax.broadcasted_iota(jnp.int32, (128, 128), 0)
                 >= lax.broadcasted_iota(jnp.int32, (128, 128), 1)).astype(F32)
        wide[...] = jnp.zeros_like(wide)
        carry = jnp.zeros((8, 1), F32)
        for blk in reversed(range(nblk)):
            sl = slice(blk * 128, (blk + 1) * 128)
            rc = jnp.dot(dfh[:, sl], lower, precision=HI, preferred_element_type=F32) + carry
            carry = rc[:, 0:1]
            wide[0:8, sl] = rc
        dl = wide[...].T
        x = cf_ref[...] + fb_ref[...]
        dcf = dl * (1.0 / (1.0 + jnp.exp(x)))
        dcf_ref[...] = dcf
        part = jnp.sum(dcf, axis=0, keepdims=True)

        @pl.when(b == 0)
        def _():
            dfb_ref[...] = part

        @pl.when(b != 0)
        def _():
            dfb_ref[...] += part

    return pl.pallas_call(
        body, name="fox_prep_bwd", grid=(nb,),
        in_specs=[pl.BlockSpec((1, 3, 8, seq), lambda b: (b, 0, 0, 0)), pl.BlockSpec((seq, GW), lambda b: (b, 0)),
                  pl.BlockSpec((seq, 128), lambda b: (b, 0)), pl.BlockSpec((1, 128), lambda b: (0, 0))],
        out_specs=[pl.BlockSpec((seq, 128), lambda b: (b, 0)), pl.BlockSpec((1, 128), lambda b: (0, 0))],
        out_shape=[jax.ShapeDtypeStruct(cf.shape, F32), jax.ShapeDtypeStruct((1, 128), F32)],
        scratch_shapes=[pltpu.VMEM((128, seq), F32)],
        compiler_params=_cparams(("arbitrary",)),
    )(df, dfq, cf, fb)


def _gate_out(oa, ob, oc, gates, w, x, gate, seq, tm=256):
    t = x.shape[0]
    tps = seq // tm

    def body(oa_ref, ob_ref, oc_ref, g_ref, w_ref, x_ref, gt_ref, xo_ref, y_ref, u_ref):
        for n, o_ref in enumerate((oa_ref, ob_ref, oc_ref)):
            sl = slice(n * GW, (n + 1) * GW)
            gv = g_ref[:, sl]
            u_ref[:, sl] = (o_ref[...] * (gv * _sigmoid(gv))).astype(BF16)
        y = jnp.dot(u_ref[...], w_ref[...], preferred_element_type=F32)
        y_ref[...] = y
        xo_ref[...] = x_ref[...] + gt_ref[0] * y

    row = lambda wd: pl.BlockSpec((tm, wd), lambda i: (i, 0))
    return pl.pallas_call(
        body, name="gate_out", grid=(t // tm,),
        in_specs=[row(GW), row(GW), row(GW), row(U_PAD), pl.BlockSpec((U_PAD, D_MODEL), lambda i: (0, 0)),
                  row(D_MODEL), pl.BlockSpec((1, 1, D_MODEL), lambda i: (i // tps, 0, 0))],
        out_specs=[row(D_MODEL), row(D_MODEL), row(U_PAD)],
        out_shape=[jax.ShapeDtypeStruct((t, D_MODEL), F32), jax.ShapeDtypeStruct((t, D_MODEL), F32),
                   jax.ShapeDtypeStruct((t, U_PAD), BF16)],
        compiler_params=_cparams(("arbitrary",)),
    )(oa, ob, oc, gates, w, x, gate)


def _gate_out_bwd(dxo, y, gate, oa, ob, oc, gates, w_t, seq, tm=256, dep=None):
    t = dxo.shape[0]
    tps = seq // tm
    nb = t // seq

    def body(dxo_ref, y_ref, gt_ref, oa_ref, ob_ref, oc_ref, g_ref, wt_ref,
             dy_ref, doa_ref, dob_ref, doc_ref, dg_ref, dgt_ref):
        i = pl.program_id(0)
        dxo_v = dxo_ref[...]
        dgt = jnp.sum(dxo_v * y_ref[...], axis=0, keepdims=True)
        dyb = (dxo_v * gt_ref[0]).astype(BF16)
        dy_ref[...] = dyb
        du = jnp.dot(dyb, wt_ref[...], preferred_element_type=F32)
        for n, (o_ref, do_ref) in enumerate(((oa_ref, doa_ref), (ob_ref, dob_ref), (oc_ref, doc_ref))):
            sl = slice(n * GW, (n + 1) * GW)
            gv = g_ref[:, sl]
            sg = _sigmoid(gv)
            dun = du[:, sl]
            do_ref[...] = dun * (gv * sg)
            dg_ref[:, sl] = dun * o_ref[...] * (sg * (1.0 + gv * (1.0 - sg)))

        @pl.when(i % tps == 0)
        def _():
            dgt_ref[0] = dgt

        @pl.when(i % tps != 0)
        def _():
            dgt_ref[0] += dgt

    row = lambda wd: pl.BlockSpec((tm, wd), lambda i: (i, 0))
    per_b = pl.BlockSpec((1, 1, D_MODEL), lambda i: (i // tps, 0, 0))
    in_specs = [row(D_MODEL), row(D_MODEL), per_b, row(GW), row(GW), row(GW), row(U_PAD),
                pl.BlockSpec((D_MODEL, U_PAD), lambda i: (0, 0))]
    body, in_specs, args = _after(dep, body, in_specs, [dxo, y, gate, oa, ob, oc, gates, w_t])
    return pl.pallas_call(
        body, name="gate_out_bwd", grid=(t // tm,), in_specs=in_specs,
        out_specs=[row(D_MODEL), row(GW), row(GW), row(GW), row(U_PAD), per_b],
        out_shape=[jax.ShapeDtypeStruct((t, D_MODEL), BF16), jax.ShapeDtypeStruct((t, GW), F32),
                   jax.ShapeDtypeStruct((t, GW), F32), jax.ShapeDtypeStruct((t, GW), F32),
                   jax.ShapeDtypeStruct((t, U_PAD), F32), jax.ShapeDtypeStruct((nb, 1, D_MODEL), F32)],
        compiler_params=_cparams(("arbitrary",)),
    )(*args)


def _final_loss(x, target, g, tm=256):
    t = x.shape[0]

    def body(x_ref, t_ref, g_ref, dx_ref, loss_ref, dg_ref):
        i = pl.program_id(0)
        xv = x_ref[...]
        rstd = lax.rsqrt(jnp.mean(xv * xv, axis=-1, keepdims=True) + EPS)
        xn = xv * rstd
        gv = g_ref[...]
        err = xn * gv - t_ref[...]
        dy = err * (1.0 / D_MODEL)
        dxn = dy * gv
        dx_ref[...] = rstd * (dxn - xn * jnp.mean(dxn * xn, axis=-1, keepdims=True))
        lp = jnp.sum(err * err, axis=0, keepdims=True) * (0.5 / D_MODEL)
        dgp = jnp.sum(dy * xn, axis=0, keepdims=True)

        @pl.when(i == 0)
        def _():
            loss_ref[...] = lp
            dg_ref[...] = dgp

        @pl.when(i != 0)
        def _():
            loss_ref[...] += lp
            dg_ref[...] += dgp

    row = pl.BlockSpec((tm, D_MODEL), lambda i: (i, 0))
    vec = pl.BlockSpec((1, D_MODEL), lambda i: (0, 0))
    return pl.pallas_call(
        body, name="final_loss", grid=(t // tm,),
        in_specs=[row, row, vec], out_specs=[row, vec, vec],
        out_shape=[jax.ShapeDtypeStruct((t, D_MODEL), F32), jax.ShapeDtypeStruct((1, D_MODEL), F32),
                   jax.ShapeDtypeStruct((1, D_MODEL), F32)],
        compiler_params=_cparams(("arbitrary",)),
    )(x, target, g)


def _adamw(w, gslots, m, v, name, tr=None):
    r, c = w.shape
    ns = gslots.shape[0]
    tr = r if tr is None else tr

    def body(w_ref, g_ref, m_ref, v_ref, go_ref, d_ref, mo_ref, vo_ref):
        g = g_ref[0].astype(F32)
        for j in range(1, ns):
            g = g + g_ref[j].astype(F32)
        mn = ADAM_B1 * m_ref[...] + (1.0 - ADAM_B1) * g
        vn = ADAM_B2 * v_ref[...] + (1.0 - ADAM_B2) * jnp.square(g)
        m_hat = mn / (1.0 - ADAM_B1 ** ADAM_STEP)
        v_hat = vn / (1.0 - ADAM_B2 ** ADAM_STEP)
        go_ref[...] = g
        d_ref[...] = -ADAM_LR * (m_hat / (jnp.sqrt(v_hat) + ADAM_EPS) + ADAM_WD * w_ref[...])
        mo_ref[...] = mn
        vo_ref[...] = vn

    blk = pl.BlockSpec((tr, c), lambda i: (i, 0))
    return pl.pallas_call(
        body, name=name, grid=(r // tr,),
        in_specs=[blk, pl.BlockSpec((ns, tr, c), lambda i: (0, i, 0)), blk, blk],
        out_specs=[blk] * 4, out_shape=[jax.ShapeDtypeStruct((r, c), F32)] * 4,
        compiler_params=_cparams(("arbitrary",)),
    )(w, gslots, m, v)


def _rope_tables(positions):
    inv = ROPE_THETA ** (-jnp.arange(0, A_ROPE, 2, dtype=F32) / A_ROPE)
    ang = positions.astype(F32)[:, None] * inv
    cos, sin = jnp.cos(ang), jnp.sin(ang)
    t = positions.shape[0]
    one = jnp.ones((t, 64), F32)
    zero16 = jnp.zeros((t, 16), F32)
    cos_t = jnp.concatenate([one, cos, cos, jnp.ones((t, 32), F32)], axis=1)
    sin_a = jnp.concatenate([jnp.zeros((t, 64), F32), -sin, zero16, jnp.zeros((t, 32), F32)], axis=1)
    sin_b = jnp.concatenate([jnp.zeros((t, 64), F32), zero16, sin, jnp.zeros((t, 32), F32)], axis=1)
    return cos_t, sin_a, sin_b


def _pad_heads(w, real, padded, nheads, axis):
    shp = w.shape[:axis] + (nheads, real) + w.shape[axis + 1:]
    w = w.reshape(shp)
    pad = [(0, 0)] * w.ndim
    pad[axis + 1] = (0, padded - real)
    w = jnp.pad(w, pad)
    return w.reshape(w.shape[:axis] + (nheads * padded,) + w.shape[axis + 2:])


def kernel(x, c, positions, w_ada, b_ada, norm_g, w_in, a_q_norm_g, a_w_uq, a_kv_norm_g, a_w_ukv, b_rel_bias, c_forget_b, w_out, final_g, loss_target, m_w_ada, m_b_ada, m_norm_g, m_w_in, m_a_q_norm_g, m_a_w_uq, m_a_kv_norm_g, m_a_w_ukv, m_b_rel_bias, m_c_forget_b, m_w_out, m_final_g, v_w_ada, v_b_ada, v_norm_g, v_w_in, v_a_q_norm_g, v_a_w_uq, v_a_kv_norm_g, v_a_w_ukv, v_b_rel_bias, v_c_forget_b, v_w_out, v_final_g):
    nb, seq, _ = x.shape
    t = nb * seq
    me = 4 * lax.axis_index("x") + 2 * lax.axis_index("y") + lax.axis_index("c")
    x2 = x.reshape(t, D_MODEL)
    tgt = loss_target.reshape(t, D_MODEL)
    cos_t, sin_a, sin_b = _rope_tables(positions.reshape(t))

    def shards(l):
        return [_pad_runs(w_in[l].astype(BF16), IN_RUNS, N_PAD, 1), w_out[l].astype(BF16),
                a_w_uq[l].astype(BF16), a_w_ukv[l].astype(BF16)]

    def prepare(gi, go, gq, gkv):
        wi = gi.reshape(D_MODEL, N_PAD)
        wo = _pad_runs(go.reshape(D_MODEL, D_MODEL), OUT_RUNS, U_PAD, 0)
        wq = jnp.transpose(gq, (1, 0, 2)).reshape(A_Q_RANK, A_HEADS * (A_NOPE + A_ROPE))
        wq = _pad_heads(wq, A_NOPE + A_ROPE, HEAD_PAD, A_HEADS, 1)
        wkv = jnp.transpose(gkv, (1, 0, 2)).reshape(A_KV_RANK, A_HEADS, 2 * A_NOPE)
        wk = jnp.pad(wkv[:, :, :A_NOPE], ((0, 0), (0, 0), (0, HEAD_PAD - A_NOPE))).reshape(A_KV_RANK, A_HEADS * HEAD_PAD)
        wv = wkv[:, :, A_NOPE:].reshape(A_KV_RANK, GW)
        return dict(w_in=wi, w_in_t=wi.T, w_out=wo, w_out_t=wo.T, wuq=wq, wuq_t=wq.T, wk=wk, wk_t=wk.T,
                    wv=wv, wv_t=wv.T)

    gathered = _gather(shards(0) + [c], "gather_weights0")
    c_all = gathered[-1].reshape(N_DEV * nb, D_MODEL)
    weights = [prepare(*gathered[:4]), None]
    gather1, gather1_token = _split_start("gather", shards(1), "gather_weights1_start")

    c_act, mod_cols = _ada_fwd(c_all, w_ada, gather1_token)
    (mod_g,) = _gather([mod_cols], "gather_mod")
    mod_all = jnp.transpose(mod_g, (1, 2, 0, 3)).reshape(DEPTH, N_DEV * nb, 3 * D_MODEL)
    mod = lax.dynamic_slice_in_dim(mod_all, me * nb, nb, axis=1) + b_ada[:, None, :]

    fb_pad = jnp.pad(c_forget_b, ((0, 0), (0, 128 - C_HEADS)))
    a_scale = (A_NOPE + A_ROPE) ** -0.5
    h_scale = CHUNK ** -0.5

    saved = []
    xl = x2
    for l in range(DEPTH):
        if l == 1:
            weights[1] = prepare(*_split_wait(gather1, xl, "gather_weights1_wait")[1])
        w = weights[l]
        shift, scale, gate = mod[l, :, :D_MODEL], mod[l, :, D_MODEL:2 * D_MODEL], mod[l, :, 2 * D_MODEL:]
        ss = jnp.stack([shift, 1.0 + scale], axis=1)
        gate3 = gate[:, None, :]
        h, cq, ckv, kpe, gates, bq, bk, bv, cq2, ck, cv, cf = _ln_in(
            xl, ss, norm_g[l:l + 1], w["w_in"], seq, dep=gather1_token if l == 0 else None)
        q, k, v, cqn, ckvn = _mla_prep(cq, ckv, kpe, a_q_norm_g[l:l + 1], a_kv_norm_g[l:l + 1],
                                       w["wuq"], w["wk"], w["wv"], cos_t, sin_a, sin_b)
        oa, sta = _attn_fwd("mla", q, k, v, None, seq, a_scale)
        table = _band_table(jnp.pad(b_rel_bias[l], ((0, 8 - B_HEADS), (0, GW - N_REL))))
        ob, stb = _band_fwd(bq, bk, bv, table, seq, h_scale)
        fcum = _fox_prep(cf, fb_pad[l:l + 1], seq)
        oc, stc = _attn_fwd("fox", cq2, ck, cv, fcum, seq, h_scale)
        xn, y, u = _gate_out(oa, ob, oc, gates, w["w_out"], xl, gate3, seq)
        saved.append(dict(x=xl, ss=ss, gate3=gate3, h=h, cq=cq, ckv=ckv, gates=gates, bq=bq, bk=bk, bv=bv,
                          cq2=cq2, ck=ck, cv=cv, cf=cf, q=q, k=k, v=v, cqn=cqn, ckvn=ckvn, oa=oa, sta=sta,
                          table=table, ob=ob, stb=stb, fcum=fcum, oc=oc, stc=stc, y=y, u=u))
        xl = xn

    dx, loss_lanes, g_final = _final_loss(xl, tgt, final_g[None, :])
    loss = lax.psum(jnp.sum(loss_lanes), AXES)

    rows = D_MODEL // N_DEV
    core = lax.axis_index("c").astype(jnp.int32).reshape(1)
    grad_names = ("in_a", "in_b", "out", "uq", "ukv")
    n_seg_a = 4
    dmods, smalls, parts = [None] * DEPTH, [None] * DEPTH, [None] * DEPTH
    pair1 = chips1 = pair1_token = chips1_token = None
    for l in reversed(range(DEPTH)):
        s, w = saved[l], weights[l]
        dy, doa, dob, doc, dgates, dgate = _gate_out_bwd(dx, s["y"], s["gate3"], s["oa"], s["ob"], s["oc"],
                                                         s["gates"], w["w_out_t"], seq, dep=pair1_token)
        g_out = _unpad_runs(_matmul_tn(s["u"], dy, "dw_out"), OUT_RUNS, 0)
        if l == 0:
            own, from_sib = _split_wait(pair1, g_out, "grads1_pair_wait")
            sums = [_pair_add(core, a, r, "grads1_add_" + nm, r.shape[1]) for a, r, nm in zip(own, from_sib, grad_names)]
            chips1, chips1_token = _split_start("chips", sums, "grads1_chips_start")
        dq, dk, dv = _attn_bwd("mla", s["q"], s["k"], s["v"], None, s["oa"], s["sta"], doa, seq, a_scale,
                               dep=chips1_token)
        dbq, dbk, dbv, gtab = _band_bwd(s["bq"], s["bk"], s["bv"], s["table"], s["ob"], s["stb"], dob, seq, h_scale,
                                        dep=chips1_token)
        g_rel = _band_table_bwd(gtab)[:, 0, :N_REL]
        dcq2, dck, dcv, dfc, dfq = _attn_bwd("fox", s["cq2"], s["ck"], s["cv"], s["fcum"], s["oc"], s["stc"], doc,
                                             seq, h_scale, dep=chips1_token)
        dcf, dfb = _fox_prep_bwd(dfc, dfq, s["cf"], fb_pad[l:l + 1], seq)
        dcq, dckv, dkpe, dqlin, dklin, dgq, dgkv = _mla_prep_bwd(
            dq, dk, dv, s["cq"], s["ckv"], a_q_norm_g[l:l + 1], a_kv_norm_g[l:l + 1],
            w["wuq_t"], w["wk_t"], w["wv_t"], cos_t, sin_a, sin_b)
        gq_pad = _matmul_tn(s["cqn"], dqlin, "dw_uq")
        g_uq = gq_pad.reshape(A_Q_RANK, A_HEADS, HEAD_PAD)[:, :, :A_NOPE + A_ROPE].reshape(A_Q_RANK, -1)
        gkv_pad = _matmul_tn(s["ckvn"], [dklin, dv], "dw_ukv")
        gk_pad = gkv_pad[:, :A_HEADS * HEAD_PAD].reshape(A_KV_RANK, A_HEADS, HEAD_PAD)[:, :, :A_NOPE]
        gv_pad = gkv_pad[:, A_HEADS * HEAD_PAD:].reshape(A_KV_RANK, A_HEADS, A_NOPE)
        g_ukv = jnp.concatenate([gk_pad, gv_pad], axis=2).reshape(A_KV_RANK, -1)
        dz = [dcq, dckv, dkpe, dgates, dbq, dbk, dbv, dcq2, dck, dcv, dcf]
        g_in_a = _matmul_tn(s["h"], dz[:n_seg_a], "dw_in_a")
        g_in_b = _matmul_tn(s["h"], dz[n_seg_a:], "dw_in_b")
        slots = [g_in_a.reshape(N_DEV, rows, -1), g_in_b.reshape(N_DEV, rows, -1), g_out.reshape(N_DEV, rows, D_MODEL),
                 g_uq.reshape(A_Q_RANK, N_DEV, -1).transpose(1, 0, 2), g_ukv.reshape(A_KV_RANK, N_DEV, -1).transpose(1, 0, 2)]
        if l == 1:
            pair1, pair1_token = _split_start("pair", slots, "grads1_pair_start")
            pair0_token = None
        else:
            pair0, pair0_token = _split_start("pair", slots, "grads0_pair_start")
        dx, dss, dg_norm = _ln_in_bwd(dz, w["w_in_t"], s["x"], s["ss"], norm_g[l:l + 1], dx, seq, dep=pair0_token)
        dmods[l] = jnp.concatenate([dss[:, 0, :], dss[:, 1, :], dgate[:, 0, :]], axis=1)
        smalls[l] = [dg_norm.reshape(-1), dgq.reshape(-1), dgkv.reshape(-1), g_rel.reshape(-1),
                     dfb[0, :C_HEADS]]
    grad_x = dx.reshape(nb, seq, D_MODEL)
    parts[1] = _split_wait(chips1, dx, "grads1_chips_wait")[1]
    own, from_sib = _split_wait(pair0, dx, "grads0_pair_wait")
    sums = [_pair_add(core, a, r, "grads0_add_" + nm, r.shape[1]) for a, r, nm in zip(own, from_sib, grad_names)]
    chips0, chips0_token = _split_start("chips", sums, "grads0_chips_start")

    small = jnp.concatenate([p for l in range(DEPTH) for p in smalls[l]] + [g_final.reshape(-1)])
    n_small = small.shape[0]
    small_rows = -(-n_small // 1024) * 8
    small = jnp.pad(small, (0, small_rows * 128 - n_small)).reshape(small_rows, 128)
    dmod_local = jnp.stack(dmods)
    dmod_g, small_g = _gather([dmod_local, small], "gather_small", dep=chips0_token)
    dmod_all = jnp.transpose(dmod_g, (1, 0, 2, 3)).reshape(DEPTH, N_DEV * nb, 3 * D_MODEL)
    cols = 3 * D_MODEL // N_DEV
    dmod_mine = lax.dynamic_slice_in_dim(dmod_all, me * cols, cols, axis=2)
    g_w_ada, g_b_ada = _ada_bwd(c_act, dmod_all, dmod_mine)
    small_sum = _sum_slots(small_g, "sum_small").reshape(-1)

    def split_small():
        out, pos = [], 0
        sizes = [D_MODEL, A_Q_RANK, A_KV_RANK, B_HEADS * N_REL, C_HEADS]
        per_layer = []
        for l in range(DEPTH):
            parts = []
            for sz in sizes:
                parts.append(small_sum[pos:pos + sz])
                pos += sz
            per_layer.append(parts)
        for j in range(len(sizes)):
            out.append(jnp.stack([per_layer[l][j] for l in range(DEPTH)]))
        out.append(small_sum[pos:pos + D_MODEL])
        return out

    g_norm, g_qn, g_kvn, g_relb, g_fb, g_fin = split_small()

    def adam(w, g, m, v, name, tr=None):
        shp = w.shape
        w2 = w.reshape(-1, shp[-1]) if w.ndim > 1 else w.reshape(1, -1)
        gs = g.reshape((-1,) + w2.shape) if g.size != w.size else g.reshape((1,) + w2.shape)
        outs = _adamw(w2, gs, m.reshape(w2.shape), v.reshape(w2.shape), name, tr)
        return [o.reshape(shp) for o in outs]

    res = {
        "w_ada": adam(w_ada, g_w_ada, m_w_ada, v_w_ada, "adam_w_ada", 256),
        "b_ada": adam(b_ada, g_b_ada, m_b_ada, v_b_ada, "adam_b_ada"),
        "norm_g": adam(norm_g, g_norm, m_norm_g, v_norm_g, "adam_norm_g"),
        "a_q_norm_g": adam(a_q_norm_g, g_qn, m_a_q_norm_g, v_a_q_norm_g, "adam_q_norm"),
        "a_kv_norm_g": adam(a_kv_norm_g, g_kvn, m_a_kv_norm_g, v_a_kv_norm_g, "adam_kv_norm"),
        "b_rel_bias": adam(b_rel_bias, g_relb.reshape(b_rel_bias.shape), m_b_rel_bias, v_b_rel_bias, "adam_rel_bias"),
        "c_forget_b": adam(c_forget_b, g_fb, m_c_forget_b, v_c_forget_b, "adam_forget_b"),
        "final_g": adam(final_g, g_fin, m_final_g, v_final_g, "adam_final_g"),
    }
    parts[0] = _split_wait(chips0, res["w_ada"][1], "grads0_chips_wait")[1]
    p_in = jnp.stack([_unpad_runs(jnp.concatenate(parts[l][0:2], axis=2), IN_RUNS, 2) for l in range(DEPTH)], axis=1)
    p_out, p_uq, p_ukv = (jnp.stack([parts[l][j] for l in range(DEPTH)], axis=1) for j in (2, 3, 4))
    res.update({
        "w_in": adam(w_in, p_in, m_w_in, v_w_in, "adam_w_in", 32),
        "a_w_uq": adam(a_w_uq, p_uq, m_a_w_uq, v_a_w_uq, "adam_w_uq"),
        "a_w_ukv": adam(a_w_ukv, p_ukv, m_a_w_ukv, v_a_w_ukv, "adam_w_ukv"),
        "w_out": adam(w_out, p_out, m_w_out, v_w_out, "adam_w_out", 64),
    })
    names = ["w_ada", "b_ada", "norm_g", "w_in", "a_q_norm_g", "a_w_uq", "a_kv_norm_g", "a_w_ukv", "b_rel_bias",
             "c_forget_b", "w_out", "final_g"]
    outs = [loss, grad_x]
    for j in range(4):
        outs += [res[n][j] for n in names]
    return tuple(outs)
```

```python
import functools

import jax
import jax.numpy as jnp
from jax import lax
from jax.experimental import pallas as pl
from jax.experimental.pallas import tpu as pltpu

F32 = jnp.float32
BF16 = jnp.bfloat16
HI = lax.Precision.HIGHEST

N_DEV = 8
AXES = ("x", "y", "c")
D_MODEL = 1024
DEPTH = 2
CHUNK = 64
EPS = 1e-6
NEG = -1e30
A_HEADS = 6
A_NOPE = 64
A_ROPE = 32
A_Q_RANK = 384
A_KV_RANK = 256
ROPE_THETA = 10000.0
B_HEADS = 5
B_LEFT = 512
REL_CLIP = 128
N_REL = 2 * REL_CLIP + 1
C_HEADS = 5
HEAD_PAD = 128
GW = 384
N_IN = 3621
ADAM_LR = 0.001
ADAM_B1 = 0.9
ADAM_B2 = 0.999
ADAM_EPS = 1e-08
ADAM_WD = 0.01
ADAM_STEP = 10
VMEM_LIMIT = 56 * 1024 * 1024

Z_SEGS = (
    ("cq", 0, 384, F32), ("ckv", 384, 256, F32), ("kpe", 640, 128, F32), ("gates", 768, 1152, F32),
    ("bq", 1920, 384, BF16), ("bk", 2304, 384, BF16), ("bv", 2688, 384, BF16),
    ("cq2", 3072, 384, BF16), ("ck", 3456, 384, BF16), ("cv", 3840, 384, BF16), ("cf", 4224, 128, F32),
)
N_PAD = 4352
IN_RUNS = (
    (0, 384, 0), (384, 256, 384), (640 + 64, 32, 640),
    (768, 384, 672), (768 + 384, 320, 2016), (768 + 768, 320, 3301),
    (1920, 320, 1056), (2304, 320, 1376), (2688, 320, 1696),
    (3072, 320, 2336), (3456, 320, 2656), (3840, 320, 2976), (4224, 5, 3296),
)
OUT_RUNS = ((0, 384, 0), (384, 320, 384), (768, 320, 704))
U_PAD = 1152


def _cparams(sem=None, vmem=VMEM_LIMIT):
    return pltpu.CompilerParams(dimension_semantics=sem, vmem_limit_bytes=vmem)


def _after(dep, body, in_specs, args):
    if dep is None:
        return body, in_specs, args
    n = len(args)

    def ordered(*refs):
        return body(*refs[:n], *refs[n + 1:])

    return ordered, list(in_specs) + [pl.BlockSpec((8, 128), lambda *_: (0, 0))], list(args) + [dep]


def _pad_runs(w, runs, total, axis):
    order = sorted(runs)
    parts, pos = [], 0
    for off, wd, src in order:
        if off > pos:
            shp = list(w.shape)
            shp[axis] = off - pos
            parts.append(jnp.zeros(shp, w.dtype))
        parts.append(lax.slice_in_dim(w, src, src + wd, axis=axis))
        pos = off + wd
    if pos < total:
        shp = list(w.shape)
        shp[axis] = total - pos
        parts.append(jnp.zeros(shp, w.dtype))
    return jnp.concatenate(parts, axis=axis)


def _unpad_runs(w, runs, axis):
    order = sorted(runs, key=lambda r: r[2])
    return jnp.concatenate([lax.slice_in_dim(w, off, off + wd, axis=axis) for off, wd, _ in order], axis=axis)


def _sigmoid(x):
    return 1.0 / (1.0 + jnp.exp(-x))


N_CHIP = 4
ANY_SPEC = pl.BlockSpec(memory_space=pl.ANY)
MESH_ID = pl.DeviceIdType.MESH


def _gather(arrs, name, dep=None):
    n = len(arrs)
    nin = n + (dep is not None)

    def body(*refs):
        ins, outs = refs[:n], refs[nin:nin + n]
        send_sems, recv_sems, local_sems = refs[nin + n:]
        x, y, c = lax.axis_index("x"), lax.axis_index("y"), lax.axis_index("c")
        me, sib = (x, y, c), (x, y, 1 - c)
        chips = [(1 - x, y), (x, 1 - y), (1 - x, 1 - y)]

        def slot(px, py, pc):
            return 4 * px + 2 * py + pc

        def copy(a, k, block, to, src=None):
            dst = outs[a].at[slot(*block)]
            return pltpu.make_async_remote_copy(
                src_ref=dst if src is None else src, dst_ref=dst, send_sem=send_sems.at[a, k],
                recv_sem=recv_sems.at[a, k], device_id=to, device_id_type=MESH_ID)

        local = [pltpu.make_async_copy(ins[a], outs[a].at[slot(*me)], local_sems.at[a]) for a in range(n)]
        first = []
        for a in range(n):
            first.append(copy(a, 0, me, sib, src=ins[a]))
            first += [copy(a, 1 + j, me, (*chip, c), src=ins[a]) for j, chip in enumerate(chips)]
        for cp in local + first:
            cp.start()
        passed = []
        for j, chip in enumerate(chips):
            for a in range(n):
                copy(a, 1 + j, (*chip, c), me).wait_recv()
                fwd = copy(a, 4 + j, (*chip, c), sib)
                fwd.start()
                passed.append(fwd)
        for a in range(n):
            copy(a, 0, sib, me).wait_recv()
            for j, chip in enumerate(chips):
                copy(a, 4 + j, (*chip, 1 - c), me).wait_recv()
        for cp in first + passed:
            cp.wait_send()
        for cp in local:
            cp.wait()

    return pl.pallas_call(
        body, name=name, out_shape=[jax.ShapeDtypeStruct((N_DEV,) + a.shape, a.dtype) for a in arrs],
        in_specs=[ANY_SPEC] * nin, out_specs=[ANY_SPEC] * n,
        scratch_shapes=[pltpu.SemaphoreType.DMA((n, N_DEV - 1)), pltpu.SemaphoreType.DMA((n, N_DEV - 1)),
                        pltpu.SemaphoreType.DMA((n,))],
    )(*arrs, *([] if dep is None else [dep]))


def _pair_swap(arrs, name):
    n = len(arrs)

    def body(*refs):
        ins, outs = refs[:n], refs[n:2 * n]
        send_sems, recv_sems = refs[2 * n:]
        x, y, c = lax.axis_index("x"), lax.axis_index("y"), lax.axis_index("c")
        copies = []
        for a in range(n):
            for q in range(N_CHIP):
                cp = pltpu.make_async_remote_copy(
                    src_ref=ins[a].at[2 * q + 1 - c], dst_ref=outs[a].at[q], send_sem=send_sems.at[a, q],
                    recv_sem=recv_sems.at[a, q], device_id=(x, y, 1 - c), device_id_type=MESH_ID)
                cp.start()
                copies.append(cp)
        for cp in copies:
            cp.wait()

    return pl.pallas_call(
        body, name=name, out_shape=[jax.ShapeDtypeStruct((N_CHIP,) + a.shape[1:], a.dtype) for a in arrs],
        in_specs=[ANY_SPEC] * n, out_specs=[ANY_SPEC] * n,
        scratch_shapes=[pltpu.SemaphoreType.DMA((n, N_CHIP)), pltpu.SemaphoreType.DMA((n, N_CHIP))],
    )(*arrs)


def _chip_a2a(arrs, name):
    n = len(arrs)

    def body(*refs):
        ins, outs = refs[:n], refs[n:2 * n]
        send_sems, recv_sems, local_sems = refs[2 * n:]
        x, y, c = lax.axis_index("x"), lax.axis_index("y"), lax.axis_index("c")
        mine = 2 * x + y
        copies = []
        for a in range(n):
            loc = pltpu.make_async_copy(ins[a].at[mine], outs[a].at[mine], local_sems.at[a])
            loc.start()
            copies.append(loc)
            for k in range(1, N_CHIP):
                px = (1 - x) if (k >> 1) & 1 else x
                py = (1 - y) if k & 1 else y
                cp = pltpu.make_async_remote_copy(
                    src_ref=ins[a].at[2 * px + py], dst_ref=outs[a].at[mine], send_sem=send_sems.at[a, k - 1],
                    recv_sem=recv_sems.at[a, k - 1], device_id=(px, py, c), device_id_type=MESH_ID)
                cp.start()
                copies.append(cp)
        for cp in copies:
            cp.wait()

    return pl.pallas_call(
        body, name=name, out_shape=[jax.ShapeDtypeStruct(a.shape, a.dtype) for a in arrs],
        in_specs=[ANY_SPEC] * n, out_specs=[ANY_SPEC] * n,
        scratch_shapes=[pltpu.SemaphoreType.DMA((n, N_CHIP - 1)), pltpu.SemaphoreType.DMA((n, N_CHIP - 1)),
                        pltpu.SemaphoreType.DMA((n,))],
    )(*arrs)


HBM_SPEC = pl.BlockSpec(memory_space=pltpu.HBM)
SEM_SPEC = pl.BlockSpec(memory_space=pltpu.SEMAPHORE)
SPLIT_EFFECT = pltpu.SideEffectType.DATAFLOW_SIDE_EFFECTING
SPLIT_SEMS = {"gather": (N_DEV - 1, True), "pair": (N_CHIP, False), "chips": (N_CHIP - 1, True)}


def _split_descriptors(pattern, srcs, lands, sems):
    x, y, c = lax.axis_index("x"), lax.axis_index("y"), lax.axis_index("c")
    nsem, has_local = SPLIT_SEMS[pattern]
    per = 2 * nsem + int(has_local)
    starts, arrivals, local = [], [], []

    def remote(a, k, src, dst, to):
        return pltpu.make_async_remote_copy(src_ref=src, dst_ref=dst, send_sem=sems[a * per + k],
                                            recv_sem=sems[a * per + nsem + k], device_id=to, device_id_type=MESH_ID)

    for a in range(len(srcs)):
        if pattern == "gather":
            me = 4 * x + 2 * y + c
            local.append(pltpu.make_async_copy(srcs[a], lands[a].at[me], sems[a * per + 2 * nsem]))
            for k in range(1, N_DEV):
                px = (1 - x) if (k >> 2) & 1 else x
                py = (1 - y) if (k >> 1) & 1 else y
                pc = (1 - c) if k & 1 else c
                starts.append(remote(a, k - 1, srcs[a], lands[a].at[me], (px, py, pc)))
                arrivals.append(remote(a, k - 1, srcs[a], lands[a].at[4 * px + 2 * py + pc], (px, py, pc)))
        elif pattern == "pair":
            for q in range(N_CHIP):
                cp = remote(a, q, srcs[a].at[2 * q + 1 - c], lands[a].at[q], (x, y, 1 - c))
                starts.append(cp)
                arrivals.append(cp)
        else:
            mine = 2 * x + y
            local.append(pltpu.make_async_copy(srcs[a].at[mine], lands[a].at[mine], sems[a * per + 2 * nsem]))
            for k in range(1, N_CHIP):
                px = (1 - x) if (k >> 1) & 1 else x
                py = (1 - y) if k & 1 else y
                starts.append(remote(a, k - 1, srcs[a].at[2 * px + py], lands[a].at[mine], (px, py, c)))
                arrivals.append(remote(a, k - 1, srcs[a].at[2 * px + py], lands[a].at[2 * px + py], (px, py, c)))
    return starts, arrivals, local


def _split_start(pattern, arrs, name, after=None):
    n = len(arrs)
    extra = [] if after is None else [after]
    nsem, has_local = SPLIT_SEMS[pattern]
    if pattern == "gather":
        land_shapes = [(N_DEV,) + a.shape for a in arrs]
    elif pattern == "pair":
        land_shapes = [(N_CHIP,) + a.shape[1:] for a in arrs]
    else:
        land_shapes = [a.shape for a in arrs]
    nsem_out = n * (2 * nsem + int(has_local))

    def body(*refs):
        srcs, lands = refs[:n], refs[n:2 * n]
        first_sem = 2 * n + len(extra)
        sems = refs[first_sem:first_sem + nsem_out]
        token = refs[-1]
        starts, _, local = _split_descriptors(pattern, srcs, lands, sems)
        for cp in local + starts:
            cp.start()
        token[...] = jnp.zeros_like(token)

    out_shape = ([pltpu.SemaphoreType.DMA(())] * nsem_out + [pltpu.HBM(a.shape, a.dtype) for a in arrs]
                 + [pltpu.HBM(s, a.dtype) for s, a in zip(land_shapes, arrs)] + [jax.ShapeDtypeStruct((8, 128), F32)])
    ins = ([pltpu.with_memory_space_constraint(a, pltpu.HBM) for a in arrs]
           + [pltpu.with_memory_space_constraint(lax.empty(s, a.dtype), pltpu.HBM) for s, a in zip(land_shapes, arrs)])
    outs = pl.pallas_call(
        body, name=name, out_shape=out_shape, in_specs=[HBM_SPEC] * (2 * n) + [ANY_SPEC] * len(extra),
        out_specs=[SEM_SPEC] * nsem_out + [HBM_SPEC] * (2 * n) + [pl.BlockSpec(memory_space=pltpu.VMEM)],
        input_output_aliases={i: nsem_out + i for i in range(2 * n)},
        compiler_params=pltpu.CompilerParams(has_side_effects=SPLIT_EFFECT),
    )(*ins, *extra)
    handle = dict(pattern=pattern, n=n, sems=outs[:nsem_out], srcs=outs[nsem_out:nsem_out + n],
                  lands=outs[nsem_out + n:nsem_out + 2 * n])
    return handle, outs[-1]


def _split_wait(handle, after, name):
    pattern, n = handle["pattern"], handle["n"]
    nsem_in = len(handle["sems"])

    def body(*refs):
        srcs, lands = refs[:n], refs[n:2 * n]
        starts, arrivals, local = _split_descriptors(pattern, srcs, lands, refs[2 * n:2 * n + nsem_in])
        for cp in starts:
            cp.wait_send()
        for cp in arrivals:
            cp.wait_recv()
        for cp in local:
            cp.wait()

    srcs, lands = handle["srcs"], handle["lands"]
    outs = pl.pallas_call(
        body, name=name,
        out_shape=[pltpu.HBM(a.shape, a.dtype) for a in srcs] + [pltpu.HBM(a.shape, a.dtype) for a in lands],
        in_specs=[HBM_SPEC] * (2 * n) + [SEM_SPEC] * nsem_in + [ANY_SPEC], out_specs=[HBM_SPEC] * (2 * n),
        input_output_aliases={i: i for i in range(2 * n)},
        compiler_params=pltpu.CompilerParams(has_side_effects=SPLIT_EFFECT),
    )(*srcs, *lands, *handle["sems"], after)
    return outs[:n], outs[n:]


def _pair_add(core, a8, b4, name, tr):
    _, r, c = b4.shape

    def body(core_ref, a_ref, b_ref, o_ref):
        o_ref[...] = (a_ref[...] + b_ref[...]).astype(BF16)

    blk = pl.BlockSpec((1, tr, c), lambda q, i, core_ref: (q, i, 0))
    grid_spec = pltpu.PrefetchScalarGridSpec(
        num_scalar_prefetch=1, grid=(N_CHIP, r // tr),
        in_specs=[pl.BlockSpec((1, tr, c), lambda q, i, core_ref: (2 * q + core_ref[0], i, 0)), blk], out_specs=blk)
    return pl.pallas_call(
        body, name=name, grid_spec=grid_spec, out_shape=jax.ShapeDtypeStruct(b4.shape, BF16),
        compiler_params=_cparams(("arbitrary", "arbitrary")),
    )(core, a8, b4)


def _sum_slots(x, name):
    _, r, c = x.shape

    def body(x_ref, o_ref):
        acc = x_ref[0]
        for j in range(1, N_DEV):
            acc = acc + x_ref[j]
        o_ref[...] = acc

    return pl.pallas_call(body, name=name, out_shape=jax.ShapeDtypeStruct((r, c), F32))(x)


def _ada_fwd(c_all, w_ada):
    nb = c_all.shape[0]
    cols = w_ada.shape[2]

    def body(c_ref, w_ref, act_ref, mod_ref):
        cv = c_ref[...]
        act = cv * _sigmoid(cv)
        act_ref[...] = act
        for l in range(DEPTH):
            mod_ref[l] = jnp.dot(act, w_ref[l], precision=HI, preferred_element_type=F32)

    return pl.pallas_call(
        body, name="ada_fwd",
        out_shape=[jax.ShapeDtypeStruct((nb, D_MODEL), F32), jax.ShapeDtypeStruct((DEPTH, nb, cols), F32)],
        compiler_params=_cparams(),
    )(c_all, w_ada)


def _ada_bwd(c_act, dmod_all, dmod_mine, dep):
    nb = c_act.shape[0]
    cols = dmod_mine.shape[2]

    def body(act_ref, dall_ref, dmine_ref, dep_ref, gw_ref, gb_ref):
        act = act_ref[...]
        for l in range(DEPTH):
            gw_ref[l] = lax.dot_general(act, dmine_ref[l], (((0,), (0,)), ((), ())),
                                        precision=HI, preferred_element_type=F32)
            gb_ref[l:l + 1, :] = jnp.sum(dall_ref[l], axis=0, keepdims=True)

    return pl.pallas_call(
        body, name="ada_bwd",
        out_shape=[jax.ShapeDtypeStruct((DEPTH, D_MODEL, cols), F32),
                   jax.ShapeDtypeStruct((DEPTH, 3 * D_MODEL), F32)],
        compiler_params=_cparams(),
    )(c_act, dmod_all, dmod_mine, dep)


def _ln_in(x, ss, g, w, seq, tm=256, dep=None):
    t = x.shape[0]
    tps = seq // tm

    def body(x_ref, ss_ref, g_ref, w_ref, h_ref, *outs):
        xv = x_ref[...]
        xn = xv * lax.rsqrt(jnp.mean(xv * xv, axis=-1, keepdims=True) + EPS)
        h = xn * g_ref[...] * ss_ref[0, 1:2, :] + ss_ref[0, 0:1, :]
        hb = h.astype(BF16)
        h_ref[...] = hb
        z = jnp.dot(hb, w_ref[...], preferred_element_type=F32)
        for o_ref, (_, off, wd, _) in zip(outs, Z_SEGS):
            o_ref[...] = z[:, off:off + wd].astype(o_ref.dtype)

    row = lambda wd: pl.BlockSpec((tm, wd), lambda i: (i, 0))
    in_specs = [row(D_MODEL), pl.BlockSpec((1, 2, D_MODEL), lambda i: (i // tps, 0, 0)),
                pl.BlockSpec((1, D_MODEL), lambda i: (0, 0)), pl.BlockSpec((D_MODEL, N_PAD), lambda i: (0, 0))]
    body, in_specs, args = _after(dep, body, in_specs, [x, ss, g, w])
    return pl.pallas_call(
        body, name="ln_in", grid=(t // tm,), in_specs=in_specs,
        out_specs=[row(D_MODEL)] + [row(wd) for _, _, wd, _ in Z_SEGS],
        out_shape=[jax.ShapeDtypeStruct((t, D_MODEL), BF16)]
        + [jax.ShapeDtypeStruct((t, wd), dt) for _, _, wd, dt in Z_SEGS],
        compiler_params=_cparams(("arbitrary",)),
    )(*args)


def _ln_in_bwd(dz, w_t, x, ss, g, dxo, seq, tm=256, dep=None):
    t = x.shape[0]
    tps = seq // tm
    nb = t // seq
    nz = len(Z_SEGS)

    def body(*refs):
        dz_refs = refs[:nz]
        wt_ref, x_ref, ss_ref, g_ref, dxo_ref, dx_ref, dss_ref, dg_ref = refs[nz:]
        i = pl.program_id(0)
        dzc = jnp.concatenate([r[...].astype(BF16) for r in dz_refs], axis=1)
        dh = jnp.dot(dzc, wt_ref[...], preferred_element_type=F32)
        xv = x_ref[...]
        rstd = lax.rsqrt(jnp.mean(xv * xv, axis=-1, keepdims=True) + EPS)
        xn = xv * rstd
        gv = g_ref[...]
        s1 = ss_ref[0, 1:2, :]
        dxg = dh * s1
        dxn = dxg * gv
        dx = rstd * (dxn - xn * jnp.mean(dxn * xn, axis=-1, keepdims=True))
        dx_ref[...] = dxo_ref[...] + dx
        dshift = jnp.sum(dh, axis=0, keepdims=True)
        dscale = jnp.sum(dh * (xn * gv), axis=0, keepdims=True)
        dgp = jnp.sum(dxg * xn, axis=0, keepdims=True)

        @pl.when(i % tps == 0)
        def _():
            dss_ref[0, 0:1, :] = dshift
            dss_ref[0, 1:2, :] = dscale

        @pl.when(i % tps != 0)
        def _():
            dss_ref[0, 0:1, :] += dshift
            dss_ref[0, 1:2, :] += dscale

        @pl.when(i == 0)
        def _():
            dg_ref[...] = dgp

        @pl.when(i != 0)
        def _():
            dg_ref[...] += dgp

    row = lambda wd: pl.BlockSpec((tm, wd), lambda i: (i, 0))
    in_specs = ([row(wd) for _, _, wd, _ in Z_SEGS]
                + [pl.BlockSpec((N_PAD, D_MODEL), lambda i: (0, 0)), row(D_MODEL),
                   pl.BlockSpec((1, 2, D_MODEL), lambda i: (i // tps, 0, 0)),
                   pl.BlockSpec((1, D_MODEL), lambda i: (0, 0)), row(D_MODEL)])
    body, in_specs, args = _after(dep, body, in_specs, [*dz, w_t, x, ss, g, dxo])
    return pl.pallas_call(
        body, name="ln_in_bwd", grid=(t // tm,), in_specs=in_specs,
        out_specs=[row(D_MODEL), pl.BlockSpec((1, 2, D_MODEL), lambda i: (i // tps, 0, 0)),
                   pl.BlockSpec((1, D_MODEL), lambda i: (0, 0))],
        out_shape=[jax.ShapeDtypeStruct((t, D_MODEL), F32), jax.ShapeDtypeStruct((nb, 2, D_MODEL), F32),
                   jax.ShapeDtypeStruct((1, D_MODEL), F32)],
        compiler_params=_cparams(("arbitrary",)),
    )(*args)


def _matmul_tn(a, bs, name, tm=1024):
    bs = list(bs) if isinstance(bs, (list, tuple)) else [bs]
    t, k = a.shape
    widths = [b.shape[1] for b in bs]
    n = sum(widths)
    tm = min(tm, t)

    def body(a_ref, *refs):
        b_refs, o_ref = refs[:-1], refs[-1]
        i = pl.program_id(0)
        av = a_ref[...].astype(BF16)
        parts = [b_ref[...].astype(BF16) for b_ref in b_refs]
        bv = parts[0] if len(parts) == 1 else jnp.concatenate(parts, axis=1)
        part = lax.dot_general(av, bv, (((0,), (0,)), ((), ())), preferred_element_type=F32)

        @pl.when(i == 0)
        def _():
            o_ref[...] = part

        @pl.when(i != 0)
        def _():
            o_ref[...] += part

    return pl.pallas_call(
        body, name=name, grid=(t // tm,),
        in_specs=[pl.BlockSpec((tm, k), lambda i: (i, 0))] + [pl.BlockSpec((tm, wd), lambda i: (i, 0)) for wd in widths],
        out_specs=pl.BlockSpec((k, n), lambda i: (0, 0)),
        out_shape=jax.ShapeDtypeStruct((k, n), F32),
        compiler_params=_cparams(("arbitrary",)),
    )(a, *bs)


def _rope(blk, cos_t, sin_a, sin_b):
    return blk * cos_t + pltpu.roll(blk, 112, 1) * sin_a + pltpu.roll(blk, 16, 1) * sin_b


def _unrope(d, cos_t, sin_a, sin_b):
    return d * cos_t + pltpu.roll(d * sin_a, 16, 1) + pltpu.roll(d * sin_b, 112, 1)


def _mla_prep(cq, ckv, kpe, gq, gkv, wuq, wk, wv, cos_t, sin_a, sin_b, tm=256):
    t = cq.shape[0]
    qw = A_HEADS * HEAD_PAD

    def body(cq_ref, ckv_ref, kpe_ref, gq_ref, gkv_ref, wuq_ref, wk_ref, wv_ref, c_ref, sa_ref, sb_ref,
             q_ref, k_ref, v_ref, cqn_ref, ckvn_ref):
        ct, sa, sb = c_ref[...], sa_ref[...], sb_ref[...]
        a = cq_ref[...]
        cqn = (a * lax.rsqrt(jnp.mean(a * a, axis=-1, keepdims=True) + EPS) * gq_ref[...]).astype(BF16)
        cqn_ref[...] = cqn
        b = ckv_ref[...]
        ckvn = (b * lax.rsqrt(jnp.mean(b * b, axis=-1, keepdims=True) + EPS) * gkv_ref[...]).astype(BF16)
        ckvn_ref[...] = ckvn
        qlin = jnp.dot(cqn, wuq_ref[...], preferred_element_type=F32)
        klin = jnp.dot(ckvn, wk_ref[...], preferred_element_type=F32)
        v_ref[...] = jnp.dot(ckvn, wv_ref[...], preferred_element_type=F32).astype(BF16)
        kr = _rope(kpe_ref[...], ct, sa, sb)
        for h in range(A_HEADS):
            sl = slice(h * HEAD_PAD, (h + 1) * HEAD_PAD)
            q_ref[:, sl] = _rope(qlin[:, sl], ct, sa, sb).astype(BF16)
            k_ref[:, sl] = (klin[:, sl] + kr).astype(BF16)

    row = lambda wd: pl.BlockSpec((tm, wd), lambda i: (i, 0))
    full = lambda r, c: pl.BlockSpec((r, c), lambda i: (0, 0))
    return pl.pallas_call(
        body, name="mla_prep", grid=(t // tm,),
        in_specs=[row(A_Q_RANK), row(A_KV_RANK), row(128), full(1, A_Q_RANK), full(1, A_KV_RANK),
                  full(A_Q_RANK, qw), full(A_KV_RANK, qw), full(A_KV_RANK, GW), row(128), row(128), row(128)],
        out_specs=[row(qw), row(qw), row(GW), row(A_Q_RANK), row(A_KV_RANK)],
        out_shape=[jax.ShapeDtypeStruct((t, qw), BF16), jax.ShapeDtypeStruct((t, qw), BF16),
                   jax.ShapeDtypeStruct((t, GW), BF16), jax.ShapeDtypeStruct((t, A_Q_RANK), BF16),
                   jax.ShapeDtypeStruct((t, A_KV_RANK), BF16)],
        compiler_params=_cparams(("arbitrary",)),
    )(cq, ckv, kpe, gq, gkv, wuq, wk, wv, cos_t, sin_a, sin_b)


def _mla_prep_bwd(dq, dk, dv, cq, ckv, gq, gkv, wuq_t, wk_t, wv_t, cos_t, sin_a, sin_b, tm=256):
    t = cq.shape[0]
    qw = A_HEADS * HEAD_PAD

    def body(dq_ref, dk_ref, dv_ref, cq_ref, ckv_ref, gq_ref, gkv_ref, wuqt_ref, wkt_ref, wvt_ref,
             c_ref, sa_ref, sb_ref, dcq_ref, dckv_ref, dkpe_ref, dql_ref, dkl_ref, dgq_ref, dgkv_ref):
        i = pl.program_id(0)
        ct, sa, sb = c_ref[...], sa_ref[...], sb_ref[...]
        lane = lax.broadcasted_iota(jnp.int32, (1, HEAD_PAD), 1)
        nope = lane < A_NOPE
        rope = (lane >= A_NOPE) & (lane < A_NOPE + A_ROPE)
        dksum = None
        for h in range(A_HEADS):
            sl = slice(h * HEAD_PAD, (h + 1) * HEAD_PAD)
            dql_ref[:, sl] = _unrope(dq_ref[:, sl], ct, sa, sb).astype(BF16)
            dkh = dk_ref[:, sl]
            dkl_ref[:, sl] = jnp.where(nope, dkh, 0.0).astype(BF16)
            dksum = dkh if dksum is None else dksum + dkh
        dkpe_ref[...] = jnp.where(rope, _unrope(jnp.where(rope, dksum, 0.0), ct, sa, sb), 0.0)
        dcqn = jnp.dot(dql_ref[...], wuqt_ref[...], preferred_element_type=F32)
        dckvn = (jnp.dot(dkl_ref[...], wkt_ref[...], preferred_element_type=F32)
                 + jnp.dot(dv_ref[...].astype(BF16), wvt_ref[...], preferred_element_type=F32))

        def norm_bwd(xv, gv, dy):
            rstd = lax.rsqrt(jnp.mean(xv * xv, axis=-1, keepdims=True) + EPS)
            xn = xv * rstd
            dxn = dy * gv
            dx = rstd * (dxn - xn * jnp.mean(dxn * xn, axis=-1, keepdims=True))
            return dx, jnp.sum(dy * xn, axis=0, keepdims=True)

        dcq, dgq = norm_bwd(cq_ref[...], gq_ref[...], dcqn)
        dckv, dgkv = norm_bwd(ckv_ref[...], gkv_ref[...], dckvn)
        dcq_ref[...] = dcq
        dckv_ref[...] = dckv

        @pl.when(i == 0)
        def _():
            dgq_ref[...] = dgq
            dgkv_ref[...] = dgkv

        @pl.when(i != 0)
        def _():
            dgq_ref[...] += dgq
            dgkv_ref[...] += dgkv

    row = lambda wd: pl.BlockSpec((tm, wd), lambda i: (i, 0))
    full = lambda r, c: pl.BlockSpec((r, c), lambda i: (0, 0))
    return pl.pallas_call(
        body, name="mla_prep_bwd", grid=(t // tm,),
        in_specs=[row(qw), row(qw), row(GW), row(A_Q_RANK), row(A_KV_RANK), full(1, A_Q_RANK), full(1, A_KV_RANK),
                  full(qw, A_Q_RANK), full(qw, A_KV_RANK), full(GW, A_KV_RANK), row(128), row(128), row(128)],
        out_specs=[row(A_Q_RANK), row(A_KV_RANK), row(128), row(qw), row(qw), full(1, A_Q_RANK), full(1, A_KV_RANK)],
        out_shape=[jax.ShapeDtypeStruct((t, A_Q_RANK), F32), jax.ShapeDtypeStruct((t, A_KV_RANK), F32),
                   jax.ShapeDtypeStruct((t, 128), F32), jax.ShapeDtypeStruct((t, qw), BF16),
                   jax.ShapeDtypeStruct((t, qw), BF16), jax.ShapeDtypeStruct((1, A_Q_RANK), F32),
                   jax.ShapeDtypeStruct((1, A_KV_RANK), F32)],
        compiler_params=_cparams(("arbitrary",)),
    )(dq, dk, dv, cq, ckv, gq, gkv, wuq_t, wk_t, wv_t, cos_t, sin_a, sin_b)


def _nt(a, b):
    return lax.dot_general(a, b, (((1,), (1,)), ((), ())), preferred_element_type=F32)


def _tn(a, b):
    return lax.dot_general(a, b, (((0,), (0,)), ((), ())), preferred_element_type=F32)


def _causal_mask(kind, q0, k0, tq, tk):
    qpos = q0 + lax.broadcasted_iota(jnp.int32, (tq, tk), 0)
    kpos = k0 + lax.broadcasted_iota(jnp.int32, (tq, tk), 1)
    if kind == "mla":
        return lax.shift_right_logical(kpos, 6) <= lax.shift_right_logical(qpos, 6)
    return kpos <= qpos


def _attn_fwd(kind, q, k, v, f, seq, scale, tq=512, tk=512):
    t = v.shape[0]
    nb = t // seq
    nq = seq // tq
    hw = 256 if kind == "mla" else 128
    use_f = f is not None
    tq, tk = min(tq, seq), min(tk, seq)
    nq = seq // tq
    assert tk % tq == 0

    def body(*refs):
        if use_f:
            q_ref, k_ref, v_ref, f_ref, o_ref, st_ref = refs
        else:
            q_ref, k_ref, v_ref, o_ref, st_ref = refs
        qi = pl.program_id(2)
        q0 = qi * tq
        lane = lax.broadcasted_iota(jnp.int32, (1, 128), 1)
        half = lane >= 64
        qall = q_ref[...]
        if kind == "mla":
            qhs = [qall[:, 0:128], qall[:, 128:256]]
        else:
            qhs = [jnp.where(half, jnp.zeros_like(qall), qall), jnp.where(half, qall, jnp.zeros_like(qall))]
        nfull = q0 // tk
        kd = pl.multiple_of(nfull * tk, tk)
        diag = _causal_mask(kind, q0 - kd, 0, tq, tk)

        def block(j, k0, state, masked):
            m, l, acc = state
            kh = k_ref[pl.ds(k0, tk), j * 128:(j + 1) * 128] if kind == "mla" else k_ref[pl.ds(k0, tk), :]
            s = _nt(qhs[j], kh) * scale
            if use_f:
                s = s - f_ref[0, 0, j:j + 1, pl.ds(k0, tk)]
            if masked:
                s = jnp.where(diag, s, NEG)
            mn = jnp.maximum(m, jnp.max(s, axis=-1, keepdims=True))
            alpha = jnp.exp(m - mn)
            p = jnp.exp(s - mn)
            l = alpha * l + jnp.sum(p, axis=-1, keepdims=True)
            acc = alpha * acc + jnp.dot(p.astype(BF16), v_ref[pl.ds(k0, tk), :], preferred_element_type=F32)
            return mn, l, acc

        def kstep(kb, carry):
            k0 = pl.multiple_of(kb * tk, tk)
            return block(0, k0, carry[:3], False) + block(1, k0, carry[3:], False)

        init = (jnp.full((tq, 1), NEG, F32), jnp.zeros((tq, 1), F32), jnp.zeros((tq, 128), F32)) * 2
        carry = lax.fori_loop(0, nfull, kstep, init)
        m0, l0, a0 = block(0, kd, carry[:3], True)
        m1, l1, a1 = block(1, kd, carry[3:], True)
        o_ref[...] = jnp.where(half, a1 / l1, a0 / l0)
        st_ref[...] = jnp.where(lane == 0, m0 + jnp.log(l0), jnp.where(lane == 1, m1 + jnp.log(l1), 0.0))

    in_specs = [pl.BlockSpec((tq, hw), lambda b, p, i: (b * nq + i, p)),
                pl.BlockSpec((seq, hw), lambda b, p, i: (b, p)),
                pl.BlockSpec((seq, 128), lambda b, p, i: (b, p))]
    args = [q, k, v]
    if use_f:
        in_specs.append(pl.BlockSpec((1, 1, 8, seq), lambda b, p, i: (b, p, 0, 0)))
        args.append(f)
    oblk = pl.BlockSpec((tq, 128), lambda b, p, i: (b * nq + i, p))
    return pl.pallas_call(
        body, name="attn_fwd_" + kind, grid=(nb, 3, nq), in_specs=in_specs, out_specs=[oblk, oblk],
        out_shape=[jax.ShapeDtypeStruct((t, GW), F32), jax.ShapeDtypeStruct((t, GW), F32)],
        compiler_params=_cparams(("arbitrary", "arbitrary", "arbitrary")),
    )(*args)


def _attn_bwd(kind, q, k, v, f, o, st, do, seq, scale, tq=512, tk=512, dep=None):
    t = v.shape[0]
    nb = t // seq
    tq, tk = min(tq, seq), min(tk, seq)
    nq = seq // tq
    nk = seq // tk
    hw = 256 if kind == "mla" else 128
    use_f = f is not None
    assert tq == tk

    def body(*refs):
        if use_f:
            q_ref, k_ref, v_ref, f_ref, o_ref, st_ref, do_ref, dq_ref, dk_ref, dv_ref, df_ref, dfq_ref = refs
        else:
            q_ref, k_ref, v_ref, o_ref, st_ref, do_ref, dq_ref, dk_ref, dv_ref = refs
        kj = pl.program_id(2)
        k0 = kj * tk
        lane = lax.broadcasted_iota(jnp.int32, (1, 128), 1)
        half = lane >= 64

        @pl.when(kj == 0)
        def _():
            dq_ref[...] = jnp.zeros_like(dq_ref)
            if use_f:
                dfq_ref[...] = jnp.zeros_like(dfq_ref)

        dk_ref[...] = jnp.zeros_like(dk_ref)
        dv_ref[...] = jnp.zeros_like(dv_ref)
        if use_f:
            df_ref[...] = jnp.zeros_like(df_ref)
        vv = v_ref[...]
        diag = _causal_mask(kind, 0, 0, tq, tk)

        def qstep(qi, masked):
            q0 = pl.multiple_of(qi * tq, tq)
            rows = pl.ds(q0, tq)
            dov = do_ref[rows, :]
            dd = dov * o_ref[rows, :]
            stv = st_ref[rows, :]
            for j in range(2):
                hm = half == bool(j)
                delta = jnp.sum(jnp.where(hm, dd, 0.0), axis=-1, keepdims=True)
                lse = stv[:, j:j + 1]
                if kind == "mla":
                    cols = slice(j * 128, (j + 1) * 128)
                    qh = q_ref[rows, cols]
                    kh = k_ref[:, cols]
                else:
                    cols = slice(0, 128)
                    qa = q_ref[rows, :]
                    qh = jnp.where(hm, qa, jnp.zeros_like(qa))
                    kh = k_ref[...]
                s = _nt(qh, kh) * scale
                if use_f:
                    s = s - f_ref[0, 0, j:j + 1, :]
                if masked:
                    s = jnp.where(diag, s, NEG)
                p = jnp.exp(s - lse)
                doh = jnp.where(hm, dov, 0.0).astype(BF16)
                ds = p * (_nt(doh, vv) - delta)
                dsb = (ds * scale).astype(BF16)
                dv_ref[...] += _tn(p.astype(BF16), doh)
                dk_ref[:, cols] += _tn(dsb, qh)
                dqc = jnp.dot(dsb, kh, preferred_element_type=F32)
                if kind != "mla":
                    dqc = jnp.where(hm, dqc, 0.0)
                dq_ref[rows, cols] += dqc
                if use_f:
                    df_ref[0, 0, j:j + 1, :] += -jnp.sum(ds, axis=0, keepdims=True)
                    dfq_ref[rows, :] += jnp.where(lane == j, jnp.sum(ds, axis=-1, keepdims=True), 0.0)

        qstep(kj, True)

        def rest(qi, carry):
            qstep(qi, False)
            return carry

        lax.fori_loop(kj + 1, nq, rest, 0)

    full_q = lambda wd: pl.BlockSpec((seq, wd), lambda b, p, i: (b, p))
    kblk = lambda wd: pl.BlockSpec((tk, wd), lambda b, p, i: (b * nk + i, p))
    in_specs = [full_q(hw), kblk(hw), kblk(128)]
    args = [q, k, v]
    if use_f:
        in_specs.append(pl.BlockSpec((1, 1, 8, tk), lambda b, p, i: (b, p, 0, i)))
        args.append(f)
    in_specs += [full_q(128), full_q(128), full_q(128)]
    args += [o, st, do]
    out_specs = [full_q(hw), kblk(hw), kblk(128)]
    out_shape = [jax.ShapeDtypeStruct((t, 3 * hw), F32), jax.ShapeDtypeStruct((t, 3 * hw), F32),
                 jax.ShapeDtypeStruct((t, GW), F32)]
    if use_f:
        out_specs += [pl.BlockSpec((1, 1, 8, tk), lambda b, p, i: (b, p, 0, i)), full_q(128)]
        out_shape += [jax.ShapeDtypeStruct((nb, 3, 8, seq), F32), jax.ShapeDtypeStruct((t, GW), F32)]
    body, in_specs, args = _after(dep, body, in_specs, args)
    return pl.pallas_call(
        body, name="attn_bwd_" + kind, grid=(nb, 3, nk), in_specs=in_specs, out_specs=out_specs,
        out_shape=out_shape, compiler_params=_cparams(("arbitrary", "arbitrary", "arbitrary")),
    )(*args)


BQ = 256
BWIN = BQ + B_LEFT


def _band_geometry():
    r = lax.broadcasted_iota(jnp.int32, (BQ, BWIN), 0)
    j = lax.broadcasted_iota(jnp.int32, (BQ, BWIN), 1)
    rc = lax.shift_right_logical(r, 6)
    jc = lax.shift_right_logical(j, 6)
    allowed = (jc - 8 <= rc) & (rc <= jc)
    return (r + B_LEFT - j) >= REL_CLIP, allowed, j < r


def _band_onehot(transposed, offset=0):
    shape = (BWIN, GW) if transposed else (GW, BWIN)
    kk = lax.broadcasted_iota(jnp.int32, shape, 1 if transposed else 0)
    x = lax.broadcasted_iota(jnp.int32, shape, 0 if transposed else 1) - offset
    x = jnp.where(x < 0, x + BWIN, x)
    return (kk == jnp.clip(B_LEFT - x, -REL_CLIP, REL_CLIP) + REL_CLIP).astype(F32)


def _band_table(rel_bias8):
    def body(b_ref, o_ref):
        hh = pl.program_id(0)
        u8 = jnp.dot(b_ref[...], _band_onehot(False), precision=HI, preferred_element_type=F32)
        rid = lax.broadcasted_iota(jnp.int32, (8, BWIN), 0)
        row = jnp.sum(jnp.where(rid == hh, u8, 0.0), axis=0, keepdims=True)
        far, allowed, _ = _band_geometry()
        tbl = pltpu.roll(jnp.broadcast_to(row, (BQ, BWIN)), 0, 1, stride=1, stride_axis=0)
        tbl = jnp.where(far, row[:, 0:1], tbl)
        o_ref[0] = jnp.where(allowed, tbl, NEG)

    return pl.pallas_call(
        body, name="band_table", grid=(6,),
        in_specs=[pl.BlockSpec((8, GW), lambda h: (0, 0))],
        out_specs=pl.BlockSpec((1, BQ, BWIN), lambda h: (h, 0, 0)),
        out_shape=jax.ShapeDtypeStruct((6, BQ, BWIN), F32),
        compiler_params=_cparams(("arbitrary",)),
    )(rel_bias8)


def _band_table_bwd(gtab):
    def body(g_ref, o_ref):
        gv = g_ref[0]
        _, _, wrapped = _band_geometry()
        gfar = jnp.sum(jnp.sum(jnp.where(wrapped, gv, 0.0), axis=-1, keepdims=True), axis=0, keepdims=True)
        anti = (lax.broadcasted_iota(jnp.int32, (BQ, BQ), 0) + lax.broadcasted_iota(jnp.int32, (BQ, BQ), 1)
                == BQ - 1).astype(F32)
        grev = jnp.dot(anti, jnp.where(wrapped, 0.0, gv), precision=HI, preferred_element_type=F32)
        near = pltpu.roll(grev, 0, 1, stride=1, stride_axis=0)
        y = jnp.broadcast_to(jnp.sum(near, axis=0, keepdims=True), (8, BWIN))
        gb = jnp.dot(y, _band_onehot(True, BQ - 1), precision=HI, preferred_element_type=F32)
        lane = lax.broadcasted_iota(jnp.int32, (8, GW), 1)
        o_ref[0] = gb + jnp.where(lane == 2 * REL_CLIP, gfar, 0.0)

    return pl.pallas_call(
        body, name="band_table_bwd", grid=(B_HEADS,),
        in_specs=[pl.BlockSpec((1, BQ, BWIN), lambda h: (h, 0, 0))],
        out_specs=pl.BlockSpec((1, 8, GW), lambda h: (h, 0, 0)),
        out_shape=jax.ShapeDtypeStruct((B_HEADS, 8, GW), F32),
        compiler_params=_cparams(("arbitrary",)),
    )(gtab)


def _band_fwd(q, k, v, table, seq, scale):
    t = q.shape[0]
    nb = t // seq
    nq = seq // BQ

    def body(q_ref, k_ref, v_ref, tb_ref, o_ref, st_ref, kpad, vpad):
        qi = pl.program_id(2)
        q0 = pl.multiple_of(qi * BQ, BQ)
        lane = lax.broadcasted_iota(jnp.int32, (1, 128), 1)
        half = lane >= 64

        @pl.when(qi == 0)
        def _():
            kpad[0:B_LEFT, :] = jnp.zeros((B_LEFT, 128), BF16)
            vpad[0:B_LEFT, :] = jnp.zeros((B_LEFT, 128), BF16)
            kpad[B_LEFT:, :] = k_ref[...]
            vpad[B_LEFT:, :] = v_ref[...]

        kw = kpad[pl.ds(q0, BWIN), :]
        vw = vpad[pl.ds(q0, BWIN), :]
        inside = lax.broadcasted_iota(jnp.int32, (BQ, BWIN), 1) >= B_LEFT - q0
        qall = q_ref[...]
        outs, lses = [], []
        for j in range(2):
            qh = jnp.where(half == bool(j), qall, jnp.zeros_like(qall))
            s = jnp.where(inside, _nt(qh, kw) * scale + tb_ref[j], NEG)
            m = jnp.max(s, axis=-1, keepdims=True)
            p = jnp.exp(s - m)
            l = jnp.sum(p, axis=-1, keepdims=True)
            outs.append(jnp.dot(p.astype(BF16), vw, preferred_element_type=F32) / l)
            lses.append(m + jnp.log(l))
        o_ref[...] = jnp.where(half, outs[1], outs[0])
        st_ref[...] = jnp.where(lane == 0, lses[0], jnp.where(lane == 1, lses[1], 0.0))

    qblk = pl.BlockSpec((BQ, 128), lambda b, p, i: (b * nq + i, p))
    full = pl.BlockSpec((seq, 128), lambda b, p, i: (b, p))
    return pl.pallas_call(
        body, name="band_fwd", grid=(nb, 3, nq),
        in_specs=[qblk, full, full, pl.BlockSpec((2, BQ, BWIN), lambda b, p, i: (p, 0, 0))],
        out_specs=[qblk, qblk],
        out_shape=[jax.ShapeDtypeStruct((t, GW), F32), jax.ShapeDtypeStruct((t, GW), F32)],
        scratch_shapes=[pltpu.VMEM((seq + B_LEFT, 128), BF16), pltpu.VMEM((seq + B_LEFT, 128), BF16)],
        compiler_params=_cparams(("arbitrary", "arbitrary", "arbitrary")),
    )(q, k, v, table)


def _band_bwd(q, k, v, table, o, st, do, seq, scale, dep=None):
    t = q.shape[0]
    nb = t // seq
    nq = seq // BQ

    def body(q_ref, k_ref, v_ref, tb_ref, o_ref, st_ref, do_ref, dq_ref, dk_ref, dv_ref, g_ref,
             kpad, vpad, dkpad, dvpad):
        b = pl.program_id(1)
        qi = pl.program_id(2)
        q0 = pl.multiple_of(qi * BQ, BQ)
        lane = lax.broadcasted_iota(jnp.int32, (1, 128), 1)
        half = lane >= 64

        @pl.when(qi == 0)
        def _():
            kpad[0:B_LEFT, :] = jnp.zeros((B_LEFT, 128), BF16)
            vpad[0:B_LEFT, :] = jnp.zeros((B_LEFT, 128), BF16)
            kpad[B_LEFT:, :] = k_ref[...]
            vpad[B_LEFT:, :] = v_ref[...]
            dkpad[...] = jnp.zeros_like(dkpad)
            dvpad[...] = jnp.zeros_like(dvpad)

        @pl.when((qi == 0) & (b == 0))
        def _():
            g_ref[...] = jnp.zeros_like(g_ref)

        win = pl.ds(q0, BWIN)
        kw = kpad[win, :]
        vw = vpad[win, :]
        inside = lax.broadcasted_iota(jnp.int32, (BQ, BWIN), 1) >= B_LEFT - q0
        qall = q_ref[...]
        dov = do_ref[...]
        dd = dov * o_ref[...]
        stv = st_ref[...]
        dq = jnp.zeros((BQ, 128), F32)
        for j in range(2):
            hm = half == bool(j)
            qh = jnp.where(hm, qall, jnp.zeros_like(qall))
            delta = jnp.sum(jnp.where(hm, dd, 0.0), axis=-1, keepdims=True)
            s = jnp.where(inside, _nt(qh, kw) * scale + tb_ref[j], NEG)
            p = jnp.exp(s - stv[:, j:j + 1])
            doh = jnp.where(hm, dov, 0.0).astype(BF16)
            ds = p * (_nt(doh, vw) - delta)
            g_ref[j] += ds
            dsb = (ds * scale).astype(BF16)
            dvpad[win, :] += _tn(p.astype(BF16), doh)
            dkpad[win, :] += _tn(dsb, qh)
            dq = dq + jnp.where(hm, jnp.dot(dsb, kw, preferred_element_type=F32), 0.0)
        dq_ref[...] = dq

        @pl.when(qi == nq - 1)
        def _():
            dk_ref[...] = dkpad[B_LEFT:, :]
            dv_ref[...] = dvpad[B_LEFT:, :]

    qblk = pl.BlockSpec((BQ, 128), lambda p, b, i: (b * nq + i, p))
    full = pl.BlockSpec((seq, 128), lambda p, b, i: (b, p))
    tblk = pl.BlockSpec((2, BQ, BWIN), lambda p, b, i: (p, 0, 0))
    body, in_specs, args = _after(dep, body, [qblk, full, full, tblk, qblk, qblk, qblk], [q, k, v, table, o, st, do])
    return pl.pallas_call(
        body, name="band_bwd", grid=(3, nb, nq),
        in_specs=in_specs,
        out_specs=[qblk, full, full, tblk],
        out_shape=[jax.ShapeDtypeStruct((t, GW), F32), jax.ShapeDtypeStruct((t, GW), F32),
                   jax.ShapeDtypeStruct((t, GW), F32), jax.ShapeDtypeStruct((6, BQ, BWIN), F32)],
        scratch_shapes=[pltpu.VMEM((seq + B_LEFT, 128), BF16), pltpu.VMEM((seq + B_LEFT, 128), BF16),
                        pltpu.VMEM((seq + B_LEFT, 128), F32), pltpu.VMEM((seq + B_LEFT, 128), F32)],
        compiler_params=_cparams(("arbitrary", "arbitrary", "arbitrary")),
    )(*args)


def _fox_prep(cf, fb, seq):
    nb = cf.shape[0] // seq
    nblk = seq // 128

    def body(cf_ref, fb_ref, f_ref):
        x = cf_ref[...] + fb_ref[...]
        lf = jnp.minimum(x, 0.0) - jnp.log1p(jnp.exp(-jnp.abs(x)))
        rows = lf.T[0:8, :]
        upper = (lax.broadcasted_iota(jnp.int32, (128, 128), 0)
                 <= lax.broadcasted_iota(jnp.int32, (128, 128), 1)).astype(F32)
        carry = jnp.zeros((8, 1), F32)
        for blk in range(nblk):
            sl = slice(blk * 128, (blk + 1) * 128)
            cs = jnp.dot(rows[:, sl], upper, precision=HI, preferred_element_type=F32) + carry
            carry = cs[:, 127:128]
            f_ref[0, 0, :, sl] = cs
            f_ref[0, 1, :, sl] = pltpu.roll(cs, 6, 0)
            f_ref[0, 2, :, sl] = pltpu.roll(cs, 4, 0)

    return pl.pallas_call(
        body, name="fox_prep", grid=(nb,),
        in_specs=[pl.BlockSpec((seq, 128), lambda b: (b, 0)), pl.BlockSpec((1, 128), lambda b: (0, 0))],
        out_specs=pl.BlockSpec((1, 3, 8, seq), lambda b: (b, 0, 0, 0)),
        out_shape=jax.ShapeDtypeStruct((nb, 3, 8, seq), F32),
        compiler_params=_cparams(("arbitrary",)),
    )(cf, fb)


def _fox_prep_bwd(df, dfq, cf, fb, seq):
    nb = cf.shape[0] // seq
    nblk = seq // 128

    def body(df_ref, dfq_ref, cf_ref, fb_ref, dcf_ref, dfb_ref, wide):
        b = pl.program_id(0)
        row = lax.broadcasted_iota(jnp.int32, (8, seq), 0)
        dfh = None
        for p in range(3):
            both = df_ref[0, p] + dfq_ref[:, p * 128:(p + 1) * 128].T[0:8, :]
            both = jnp.where(row < 2, both, 0.0)
            if p:
                both = pltpu.roll(both, 2 * p, 0)
            dfh = both if dfh is None else dfh + both
        lower = (lax.broadcasted_iota(jnp.int32, (128, 128), 0)
                 >= lax.broadcasted_iota(jnp.int32, (128, 128), 1)).astype(F32)
        wide[...] = jnp.zeros_like(wide)
        carry = jnp.zeros((8, 1), F32)
        for blk in reversed(range(nblk)):
            sl = slice(blk * 128, (blk + 1) * 128)
            rc = jnp.dot(dfh[:, sl], lower, precision=HI, preferred_element_type=F32) + carry
            carry = rc[:, 0:1]
            wide[0:8, sl] = rc
        dl = wide[...].T
        x = cf_ref[...] + fb_ref[...]
        dcf = dl * (1.0 / (1.0 + jnp.exp(x)))
        dcf_ref[...] = dcf
        part = jnp.sum(dcf, axis=0, keepdims=True)

        @pl.when(b == 0)
        def _():
            dfb_ref[...] = part

        @pl.when(b != 0)
        def _():
            dfb_ref[...] += part

    return pl.pallas_call(
        body, name="fox_prep_bwd", grid=(nb,),
        in_specs=[pl.BlockSpec((1, 3, 8, seq), lambda b: (b, 0, 0, 0)), pl.BlockSpec((seq, GW), lambda b: (b, 0)),
                  pl.BlockSpec((seq, 128), lambda b: (b, 0)), pl.BlockSpec((1, 128), lambda b: (0, 0))],
        out_specs=[pl.BlockSpec((seq, 128), lambda b: (b, 0)), pl.BlockSpec((1, 128), lambda b: (0, 0))],
        out_shape=[jax.ShapeDtypeStruct(cf.shape, F32), jax.ShapeDtypeStruct((1, 128), F32)],
        scratch_shapes=[pltpu.VMEM((128, seq), F32)],
        compiler_params=_cparams(("arbitrary",)),
    )(df, dfq, cf, fb)


def _gate_out(oa, ob, oc, gates, w, x, gate, seq, tm=256):
    t = x.shape[0]
    tps = seq // tm

    def body(oa_ref, ob_ref, oc_ref, g_ref, w_ref, x_ref, gt_ref, xo_ref, y_ref, u_ref):
        for n, o_ref in enumerate((oa_ref, ob_ref, oc_ref)):
            sl = slice(n * GW, (n + 1) * GW)
            gv = g_ref[:, sl]
            u_ref[:, sl] = (o_ref[...] * (gv * _sigmoid(gv))).astype(BF16)
        y = jnp.dot(u_ref[...], w_ref[...], preferred_element_type=F32)
        y_ref[...] = y
        xo_ref[...] = x_ref[...] + gt_ref[0] * y

    row = lambda wd: pl.BlockSpec((tm, wd), lambda i: (i, 0))
    return pl.pallas_call(
        body, name="gate_out", grid=(t // tm,),
        in_specs=[row(GW), row(GW), row(GW), row(U_PAD), pl.BlockSpec((U_PAD, D_MODEL), lambda i: (0, 0)),
                  row(D_MODEL), pl.BlockSpec((1, 1, D_MODEL), lambda i: (i // tps, 0, 0))],
        out_specs=[row(D_MODEL), row(D_MODEL), row(U_PAD)],
        out_shape=[jax.ShapeDtypeStruct((t, D_MODEL), F32), jax.ShapeDtypeStruct((t, D_MODEL), F32),
                   jax.ShapeDtypeStruct((t, U_PAD), BF16)],
        compiler_params=_cparams(("arbitrary",)),
    )(oa, ob, oc, gates, w, x, gate)


def _gate_out_bwd(dxo, y, gate, oa, ob, oc, gates, w_t, seq, tm=256, dep=None):
    t = dxo.shape[0]
    tps = seq // tm
    nb = t // seq

    def body(dxo_ref, y_ref, gt_ref, oa_ref, ob_ref, oc_ref, g_ref, wt_ref,
             dy_ref, doa_ref, dob_ref, doc_ref, dg_ref, dgt_ref):
        i = pl.program_id(0)
        dxo_v = dxo_ref[...]
        dgt = jnp.sum(dxo_v * y_ref[...], axis=0, keepdims=True)
        dyb = (dxo_v * gt_ref[0]).astype(BF16)
        dy_ref[...] = dyb
        du = jnp.dot(dyb, wt_ref[...], preferred_element_type=F32)
        for n, (o_ref, do_ref) in enumerate(((oa_ref, doa_ref), (ob_ref, dob_ref), (oc_ref, doc_ref))):
            sl = slice(n * GW, (n + 1) * GW)
            gv = g_ref[:, sl]
            sg = _sigmoid(gv)
            dun = du[:, sl]
            do_ref[...] = dun * (gv * sg)
            dg_ref[:, sl] = dun * o_ref[...] * (sg * (1.0 + gv * (1.0 - sg)))

        @pl.when(i % tps == 0)
        def _():
            dgt_ref[0] = dgt

        @pl.when(i % tps != 0)
        def _():
            dgt_ref[0] += dgt

    row = lambda wd: pl.BlockSpec((tm, wd), lambda i: (i, 0))
    per_b = pl.BlockSpec((1, 1, D_MODEL), lambda i: (i // tps, 0, 0))
    in_specs = [row(D_MODEL), row(D_MODEL), per_b, row(GW), row(GW), row(GW), row(U_PAD),
                pl.BlockSpec((D_MODEL, U_PAD), lambda i: (0, 0))]
    body, in_specs, args = _after(dep, body, in_specs, [dxo, y, gate, oa, ob, oc, gates, w_t])
    return pl.pallas_call(
        body, name="gate_out_bwd", grid=(t // tm,), in_specs=in_specs,
        out_specs=[row(D_MODEL), row(GW), row(GW), row(GW), row(U_PAD), per_b],
        out_shape=[jax.ShapeDtypeStruct((t, D_MODEL), BF16), jax.ShapeDtypeStruct((t, GW), F32),
                   jax.ShapeDtypeStruct((t, GW), F32), jax.ShapeDtypeStruct((t, GW), F32),
                   jax.ShapeDtypeStruct((t, U_PAD), F32), jax.ShapeDtypeStruct((nb, 1, D_MODEL), F32)],
        compiler_params=_cparams(("arbitrary",)),
    )(*args)


def _final_loss(x, target, g, tm=256):
    t = x.shape[0]

    def body(x_ref, t_ref, g_ref, dx_ref, loss_ref, dg_ref):
        i = pl.program_id(0)
        xv = x_ref[...]
        rstd = lax.rsqrt(jnp.mean(xv * xv, axis=-1, keepdims=True) + EPS)
        xn = xv * rstd
        gv = g_ref[...]
        err = xn * gv - t_ref[...]
        dy = err * (1.0 / D_MODEL)
        dxn = dy * gv
        dx_ref[...] = rstd * (dxn - xn * jnp.mean(dxn * xn, axis=-1, keepdims=True))
        lp = jnp.sum(err * err, axis=0, keepdims=True) * (0.5 / D_MODEL)
        dgp = jnp.sum(dy * xn, axis=0, keepdims=True)

        @pl.when(i == 0)
        def _():
            loss_ref[...] = lp
            dg_ref[...] = dgp

        @pl.when(i != 0)
        def _():
            loss_ref[...] += lp
            dg_ref[...] += dgp

    row = pl.BlockSpec((tm, D_MODEL), lambda i: (i, 0))
    vec = pl.BlockSpec((1, D_MODEL), lambda i: (0, 0))
    return pl.pallas_call(
        body, name="final_loss", grid=(t // tm,),
        in_specs=[row, row, vec], out_specs=[row, vec, vec],
        out_shape=[jax.ShapeDtypeStruct((t, D_MODEL), F32), jax.ShapeDtypeStruct((1, D_MODEL), F32),
                   jax.ShapeDtypeStruct((1, D_MODEL), F32)],
        compiler_params=_cparams(("arbitrary",)),
    )(x, target, g)


def _adamw(w, gslots, m, v, name, tr=None):
    r, c = w.shape
    ns = gslots.shape[0]
    tr = r if tr is None else tr

    def body(w_ref, g_ref, m_ref, v_ref, go_ref, d_ref, mo_ref, vo_ref):
        g = g_ref[0].astype(F32)
        for j in range(1, ns):
            g = g + g_ref[j].astype(F32)
        mn = ADAM_B1 * m_ref[...] + (1.0 - ADAM_B1) * g
        vn = ADAM_B2 * v_ref[...] + (1.0 - ADAM_B2) * jnp.square(g)
        m_hat = mn / (1.0 - ADAM_B1 ** ADAM_STEP)
        v_hat = vn / (1.0 - ADAM_B2 ** ADAM_STEP)
        go_ref[...] = g
        d_ref[...] = -ADAM_LR * (m_hat / (jnp.sqrt(v_hat) + ADAM_EPS) + ADAM_WD * w_ref[...])
        mo_ref[...] = mn
        vo_ref[...] = vn

    blk = pl.BlockSpec((tr, c), lambda i: (i, 0))
    return pl.pallas_call(
        body, name=name, grid=(r // tr,),
        in_specs=[blk, pl.BlockSpec((ns, tr, c), lambda i: (0, i, 0)), blk, blk],
        out_specs=[blk] * 4, out_shape=[jax.ShapeDtypeStruct((r, c), F32)] * 4,
        compiler_params=_cparams(("arbitrary",)),
    )(w, gslots, m, v)


def _rope_tables(positions):
    inv = ROPE_THETA ** (-jnp.arange(0, A_ROPE, 2, dtype=F32) / A_ROPE)
    ang = positions.astype(F32)[:, None] * inv
    cos, sin = jnp.cos(ang), jnp.sin(ang)
    t = positions.shape[0]
    one = jnp.ones((t, 64), F32)
    zero16 = jnp.zeros((t, 16), F32)
    cos_t = jnp.concatenate([one, cos, cos, jnp.ones((t, 32), F32)], axis=1)
    sin_a = jnp.concatenate([jnp.zeros((t, 64), F32), -sin, zero16, jnp.zeros((t, 32), F32)], axis=1)
    sin_b = jnp.concatenate([jnp.zeros((t, 64), F32), zero16, sin, jnp.zeros((t, 32), F32)], axis=1)
    return cos_t, sin_a, sin_b


def _pad_heads(w, real, padded, nheads, axis):
    shp = w.shape[:axis] + (nheads, real) + w.shape[axis + 1:]
    w = w.reshape(shp)
    pad = [(0, 0)] * w.ndim
    pad[axis + 1] = (0, padded - real)
    w = jnp.pad(w, pad)
    return w.reshape(w.shape[:axis] + (nheads * padded,) + w.shape[axis + 2:])


def kernel(x, c, positions, w_ada, b_ada, norm_g, w_in, a_q_norm_g, a_w_uq, a_kv_norm_g, a_w_ukv, b_rel_bias, c_forget_b, w_out, final_g, loss_target, m_w_ada, m_b_ada, m_norm_g, m_w_in, m_a_q_norm_g, m_a_w_uq, m_a_kv_norm_g, m_a_w_ukv, m_b_rel_bias, m_c_forget_b, m_w_out, m_final_g, v_w_ada, v_b_ada, v_norm_g, v_w_in, v_a_q_norm_g, v_a_w_uq, v_a_kv_norm_g, v_a_w_ukv, v_b_rel_bias, v_c_forget_b, v_w_out, v_final_g):
    nb, seq, _ = x.shape
    t = nb * seq
    me = 4 * lax.axis_index("x") + 2 * lax.axis_index("y") + lax.axis_index("c")
    x2 = x.reshape(t, D_MODEL)
    tgt = loss_target.reshape(t, D_MODEL)
    cos_t, sin_a, sin_b = _rope_tables(positions.reshape(t))

    def shards(l):
        return [_pad_runs(w_in[l].astype(BF16), IN_RUNS, N_PAD, 1), w_out[l].astype(BF16),
                a_w_uq[l].astype(BF16), a_w_ukv[l].astype(BF16)]

    def prepare(gi, go, gq, gkv):
        wi = gi.reshape(D_MODEL, N_PAD)
        wo = _pad_runs(go.reshape(D_MODEL, D_MODEL), OUT_RUNS, U_PAD, 0)
        wq = jnp.transpose(gq, (1, 0, 2)).reshape(A_Q_RANK, A_HEADS * (A_NOPE + A_ROPE))
        wq = _pad_heads(wq, A_NOPE + A_ROPE, HEAD_PAD, A_HEADS, 1)
        wkv = jnp.transpose(gkv, (1, 0, 2)).reshape(A_KV_RANK, A_HEADS, 2 * A_NOPE)
        wk = jnp.pad(wkv[:, :, :A_NOPE], ((0, 0), (0, 0), (0, HEAD_PAD - A_NOPE))).reshape(A_KV_RANK, A_HEADS * HEAD_PAD)
        wv = wkv[:, :, A_NOPE:].reshape(A_KV_RANK, GW)
        return dict(w_in=wi, w_in_t=wi.T, w_out=wo, w_out_t=wo.T, wuq=wq, wuq_t=wq.T, wk=wk, wk_t=wk.T,
                    wv=wv, wv_t=wv.T)

    gathered = _gather(shards(0) + [c], "gather_weights0")
    c_all = gathered[-1].reshape(N_DEV * nb, D_MODEL)
    weights = [prepare(*gathered[:4]), None]

    c_act, mod_cols = _ada_fwd(c_all, w_ada)
    (mod_g,) = _gather([mod_cols], "gather_mod")
    gather1, gather1_token = _split_start("gather", shards(1), "gather_weights1_start", after=mod_g)
    mod_all = jnp.transpose(mod_g, (1, 2, 0, 3)).reshape(DEPTH, N_DEV * nb, 3 * D_MODEL)
    mod = lax.dynamic_slice_in_dim(mod_all, me * nb, nb, axis=1) + b_ada[:, None, :]

    fb_pad = jnp.pad(c_forget_b, ((0, 0), (0, 128 - C_HEADS)))
    a_scale = (A_NOPE + A_ROPE) ** -0.5
    h_scale = CHUNK ** -0.5

    saved = []
    xl = x2
    for l in range(DEPTH):
        if l == 1:
            weights[1] = prepare(*_split_wait(gather1, xl, "gather_weights1_wait")[1])
        w = weights[l]
        shift, scale, gate = mod[l, :, :D_MODEL], mod[l, :, D_MODEL:2 * D_MODEL], mod[l, :, 2 * D_MODEL:]
        ss = jnp.stack([shift, 1.0 + scale], axis=1)
        gate3 = gate[:, None, :]
        h, cq, ckv, kpe, gates, bq, bk, bv, cq2, ck, cv, cf = _ln_in(
            xl, ss, norm_g[l:l + 1], w["w_in"], seq, dep=gather1_token if l == 0 else None)
        q, k, v, cqn, ckvn = _mla_prep(cq, ckv, kpe, a_q_norm_g[l:l + 1], a_kv_norm_g[l:l + 1],
                                       w["wuq"], w["wk"], w["wv"], cos_t, sin_a, sin_b)
        oa, sta = _attn_fwd("mla", q, k, v, None, seq, a_scale)
        table = _band_table(jnp.pad(b_rel_bias[l], ((0, 8 - B_HEADS), (0, GW - N_REL))))
        ob, stb = _band_fwd(bq, bk, bv, table, seq, h_scale)
        fcum = _fox_prep(cf, fb_pad[l:l + 1], seq)
        oc, stc = _attn_fwd("fox", cq2, ck, cv, fcum, seq, h_scale)
        xn, y, u = _gate_out(oa, ob, oc, gates, w["w_out"], xl, gate3, seq)
        saved.append(dict(x=xl, ss=ss, gate3=gate3, h=h, cq=cq, ckv=ckv, gates=gates, bq=bq, bk=bk, bv=bv,
                          cq2=cq2, ck=ck, cv=cv, cf=cf, q=q, k=k, v=v, cqn=cqn, ckvn=ckvn, oa=oa, sta=sta,
                          table=table, ob=ob, stb=stb, fcum=fcum, oc=oc, stc=stc, y=y, u=u))
        xl = xn

    dx, loss_lanes, g_final = _final_loss(xl, tgt, final_g[None, :])
    loss = lax.psum(jnp.sum(loss_lanes), AXES)

    rows = D_MODEL // N_DEV
    core = lax.axis_index("c").astype(jnp.int32).reshape(1)
    grad_names = ("in_a", "in_b", "out", "uq", "ukv")
    n_seg_a = 4
    dmods, smalls, parts = [None] * DEPTH, [None] * DEPTH, [None] * DEPTH
    pair1 = chips1 = pair1_token = chips1_token = None
    for l in reversed(range(DEPTH)):
        s, w = saved[l], weights[l]
        dy, doa, dob, doc, dgates, dgate = _gate_out_bwd(dx, s["y"], s["gate3"], s["oa"], s["ob"], s["oc"],
                                                         s["gates"], w["w_out_t"], seq, dep=pair1_token)
        g_out = _unpad_runs(_matmul_tn(s["u"], dy, "dw_out"), OUT_RUNS, 0)
        if l == 0:
            own, from_sib = _split_wait(pair1, g_out, "grads1_pair_wait")
            sums = [_pair_add(core, a, r, "grads1_add_" + nm, r.shape[1]) for a, r, nm in zip(own, from_sib, grad_names)]
            chips1, chips1_token = _split_start("chips", sums, "grads1_chips_start")
        dq, dk, dv = _attn_bwd("mla", s["q"], s["k"], s["v"], None, s["oa"], s["sta"], doa, seq, a_scale,
                               dep=chips1_token)
        dbq, dbk, dbv, gtab = _band_bwd(s["bq"], s["bk"], s["bv"], s["table"], s["ob"], s["stb"], dob, seq, h_scale,
                                        dep=chips1_token)
        g_rel = _band_table_bwd(gtab)[:, 0, :N_REL]
        dcq2, dck, dcv, dfc, dfq = _attn_bwd("fox", s["cq2"], s["ck"], s["cv"], s["fcum"], s["oc"], s["stc"], doc,
                                             seq, h_scale, dep=chips1_token)
        dcf, dfb = _fox_prep_bwd(dfc, dfq, s["cf"], fb_pad[l:l + 1], seq)
        dcq, dckv, dkpe, dqlin, dklin, dgq, dgkv = _mla_prep_bwd(
            dq, dk, dv, s["cq"], s["ckv"], a_q_norm_g[l:l + 1], a_kv_norm_g[l:l + 1],
            w["wuq_t"], w["wk_t"], w["wv_t"], cos_t, sin_a, sin_b)
        gq_pad = _matmul_tn(s["cqn"], dqlin, "dw_uq")
        g_uq = gq_pad.reshape(A_Q_RANK, A_HEADS, HEAD_PAD)[:, :, :A_NOPE + A_ROPE].reshape(A_Q_RANK, -1)
        gkv_pad = _matmul_tn(s["ckvn"], [dklin, dv], "dw_ukv")
        gk_pad = gkv_pad[:, :A_HEADS * HEAD_PAD].reshape(A_KV_RANK, A_HEADS, HEAD_PAD)[:, :, :A_NOPE]
        gv_pad = gkv_pad[:, A_HEADS * HEAD_PAD:].reshape(A_KV_RANK, A_HEADS, A_NOPE)
        g_ukv = jnp.concatenate([gk_pad, gv_pad], axis=2).reshape(A_KV_RANK, -1)
        dz = [dcq, dckv, dkpe, dgates, dbq, dbk, dbv, dcq2, dck, dcv, dcf]
        g_in_a = _matmul_tn(s["h"], dz[:n_seg_a], "dw_in_a")
        g_in_b = _matmul_tn(s["h"], dz[n_seg_a:], "dw_in_b")
        slots = [g_in_a.reshape(N_DEV, rows, -1), g_in_b.reshape(N_DEV, rows, -1), g_out.reshape(N_DEV, rows, D_MODEL),
                 g_uq.reshape(A_Q_RANK, N_DEV, -1).transpose(1, 0, 2), g_ukv.reshape(A_KV_RANK, N_DEV, -1).transpose(1, 0, 2)]
        if l == 1:
            pair1, pair1_token = _split_start("pair", slots, "grads1_pair_start")
            pair0_token = None
        else:
            pair0, pair0_token = _split_start("pair", slots, "grads0_pair_start")
        dx, dss, dg_norm = _ln_in_bwd(dz, w["w_in_t"], s["x"], s["ss"], norm_g[l:l + 1], dx, seq, dep=pair0_token)
        dmods[l] = jnp.concatenate([dss[:, 0, :], dss[:, 1, :], dgate[:, 0, :]], axis=1)
        smalls[l] = [dg_norm.reshape(-1), dgq.reshape(-1), dgkv.reshape(-1), g_rel.reshape(-1),
                     dfb[0, :C_HEADS]]
    grad_x = dx.reshape(nb, seq, D_MODEL)
    parts[1] = _split_wait(chips1, dx, "grads1_chips_wait")[1]
    own, from_sib = _split_wait(pair0, dx, "grads0_pair_wait")

    small = jnp.concatenate([p for l in range(DEPTH) for p in smalls[l]] + [g_final.reshape(-1)])
    n_small = small.shape[0]
    small_rows = -(-n_small // 1024) * 8
    small = jnp.pad(small, (0, small_rows * 128 - n_small)).reshape(small_rows, 128)
    dmod_local = jnp.stack(dmods)
    dmod_g, small_g = _gather([dmod_local, small], "gather_small", dep=own[0])
    sums = [_pair_add(core, a, r, "grads0_add_" + nm, r.shape[1]) for a, r, nm in zip(own, from_sib, grad_names)]
    chips0, chips0_token = _split_start("chips", sums, "grads0_chips_start", after=small_g)
    dmod_all = jnp.transpose(dmod_g, (1, 0, 2, 3)).reshape(DEPTH, N_DEV * nb, 3 * D_MODEL)
    cols = 3 * D_MODEL // N_DEV
    dmod_mine = lax.dynamic_slice_in_dim(dmod_all, me * cols, cols, axis=2)
    g_w_ada, g_b_ada = _ada_bwd(c_act, dmod_all, dmod_mine, chips0_token)
    small_sum = _sum_slots(small_g, "sum_small").reshape(-1)

    def split_small():
        out, pos = [], 0
        sizes = [D_MODEL, A_Q_RANK, A_KV_RANK, B_HEADS * N_REL, C_HEADS]
        per_layer = []
        for l in range(DEPTH):
            parts = []
            for sz in sizes:
                parts.append(small_sum[pos:pos + sz])
                pos += sz
            per_layer.append(parts)
        for j in range(len(sizes)):
            out.append(jnp.stack([per_layer[l][j] for l in range(DEPTH)]))
        out.append(small_sum[pos:pos + D_MODEL])
        return out

    g_norm, g_qn, g_kvn, g_relb, g_fb, g_fin = split_small()

    def adam(w, g, m, v, name, tr=None):
        shp = w.shape
        w2 = w.reshape(-1, shp[-1]) if w.ndim > 1 else w.reshape(1, -1)
        gs = g.reshape((-1,) + w2.shape) if g.size != w.size else g.reshape((1,) + w2.shape)
        outs = _adamw(w2, gs, m.reshape(w2.shape), v.reshape(w2.shape), name, tr)
        return [o.reshape(shp) for o in outs]

    res = {
        "w_ada": adam(w_ada, g_w_ada, m_w_ada, v_w_ada, "adam_w_ada", 256),
        "b_ada": adam(b_ada, g_b_ada, m_b_ada, v_b_ada, "adam_b_ada"),
        "norm_g": adam(norm_g, g_norm, m_norm_g, v_norm_g, "adam_norm_g"),
        "a_q_norm_g": adam(a_q_norm_g, g_qn, m_a_q_norm_g, v_a_q_norm_g, "adam_q_norm"),
        "a_kv_norm_g": adam(a_kv_norm_g, g_kvn, m_a_kv_norm_g, v_a_kv_norm_g, "adam_kv_norm"),
        "b_rel_bias": adam(b_rel_bias, g_relb.reshape(b_rel_bias.shape), m_b_rel_bias, v_b_rel_bias, "adam_rel_bias"),
        "c_forget_b": adam(c_forget_b, g_fb, m_c_forget_b, v_c_forget_b, "adam_forget_b"),
        "final_g": adam(final_g, g_fin, m_final_g, v_final_g, "adam_final_g"),
    }
    parts[0] = _split_wait(chips0, res["w_ada"][1], "grads0_chips_wait")[1]
    p_in = jnp.stack([_unpad_runs(jnp.concatenate(parts[l][0:2], axis=2), IN_RUNS, 2) for l in range(DEPTH)], axis=1)
    p_out, p_uq, p_ukv = (jnp.stack([parts[l][j] for l in range(DEPTH)], axis=1) for j in (2, 3, 4))
    res.update({
        "w_in": adam(w_in, p_in, m_w_in, v_w_in, "adam_w_in", 32),
        "a_w_uq": adam(a_w_uq, p_uq, m_a_w_uq, v_a_w_uq, "adam_w_uq"),
        "a_w_ukv": adam(a_w_ukv, p_ukv, m_a_w_ukv, v_a_w_ukv, "adam_w_ukv"),
        "w_out": adam(w_out, p_out, m_w_out, v_w_out, "adam_w_out", 64),
    })
    names = ["w_ada", "b_ada", "norm_g", "w_in", "a_q_norm_g", "a_w_uq", "a_kv_norm_g", "a_w_ukv", "b_rel_bias",
             "c_forget_b", "w_out", "final_g"]
    outs = [loss, grad_x]
    for j in range(4):
        outs += [res[n][j] for n in names]
    return tuple(outs)
```

```python
import functools

import jax
import jax.numpy as jnp
from jax import lax
from jax.experimental import pallas as pl
from jax.experimental.pallas import tpu as pltpu

F32 = jnp.float32
BF16 = jnp.bfloat16
HI = lax.Precision.HIGHEST

N_DEV = 8
AXES = ("x", "y", "c")
D_MODEL = 1024
DEPTH = 2
CHUNK = 64
EPS = 1e-6
NEG = -1e30
A_HEADS = 6
A_NOPE = 64
A_ROPE = 32
A_Q_RANK = 384
A_KV_RANK = 256
ROPE_THETA = 10000.0
B_HEADS = 5
B_LEFT = 512
REL_CLIP = 128
N_REL = 2 * REL_CLIP + 1
C_HEADS = 5
HEAD_PAD = 128
GW = 384
N_IN = 3621
ADAM_LR = 0.001
ADAM_B1 = 0.9
ADAM_B2 = 0.999
ADAM_EPS = 1e-08
ADAM_WD = 0.01
ADAM_STEP = 10
VMEM_LIMIT = 56 * 1024 * 1024

Z_SEGS = (
    ("cq", 0, 384, F32), ("ckv", 384, 256, F32), ("kpe", 640, 128, F32), ("gates", 768, 1152, F32),
    ("bq", 1920, 384, BF16), ("bk", 2304, 384, BF16), ("bv", 2688, 384, BF16),
    ("cq2", 3072, 384, BF16), ("ck", 3456, 384, BF16), ("cv", 3840, 384, BF16), ("cf", 4224, 128, F32),
)
N_PAD = 4352
IN_RUNS = (
    (0, 384, 0), (384, 256, 384), (640 + 64, 32, 640),
    (768, 384, 672), (768 + 384, 320, 2016), (768 + 768, 320, 3301),
    (1920, 320, 1056), (2304, 320, 1376), (2688, 320, 1696),
    (3072, 320, 2336), (3456, 320, 2656), (3840, 320, 2976), (4224, 5, 3296),
)
OUT_RUNS = ((0, 384, 0), (384, 320, 384), (768, 320, 704))
U_PAD = 1152


def _cparams(sem=None, vmem=VMEM_LIMIT):
    return pltpu.CompilerParams(dimension_semantics=sem, vmem_limit_bytes=vmem)


def _after(dep, body, in_specs, args):
    if dep is None:
        return body, in_specs, args
    n = len(args)

    def ordered(*refs):
        return body(*refs[:n], *refs[n + 1:])

    return ordered, list(in_specs) + [pl.BlockSpec((8, 128), lambda *_: (0, 0))], list(args) + [dep]


def _pad_runs(w, runs, total, axis):
    order = sorted(runs)
    parts, pos = [], 0
    for off, wd, src in order:
        if off > pos:
            shp = list(w.shape)
            shp[axis] = off - pos
            parts.append(jnp.zeros(shp, w.dtype))
        parts.append(lax.slice_in_dim(w, src, src + wd, axis=axis))
        pos = off + wd
    if pos < total:
        shp = list(w.shape)
        shp[axis] = total - pos
        parts.append(jnp.zeros(shp, w.dtype))
    return jnp.concatenate(parts, axis=axis)


def _unpad_runs(w, runs, axis):
    order = sorted(runs, key=lambda r: r[2])
    return jnp.concatenate([lax.slice_in_dim(w, off, off + wd, axis=axis) for off, wd, _ in order], axis=axis)


def _sigmoid(x):
    return 1.0 / (1.0 + jnp.exp(-x))


N_CHIP = 4
ANY_SPEC = pl.BlockSpec(memory_space=pl.ANY)
MESH_ID = pl.DeviceIdType.MESH


def _gather(arrs, name, dep=None):
    n = len(arrs)
    nin = n + (dep is not None)

    def body(*refs):
        ins, outs = refs[:n], refs[nin:nin + n]
        send_sems, recv_sems, local_sems = refs[nin + n:]
        x, y, c = lax.axis_index("x"), lax.axis_index("y"), lax.axis_index("c")
        me, sib = (x, y, c), (x, y, 1 - c)
        chips = [(1 - x, y), (x, 1 - y), (1 - x, 1 - y)]

        def slot(px, py, pc):
            return 4 * px + 2 * py + pc

        def copy(a, k, block, to, src=None):
            dst = outs[a].at[slot(*block)]
            return pltpu.make_async_remote_copy(
                src_ref=dst if src is None else src, dst_ref=dst, send_sem=send_sems.at[a, k],
                recv_sem=recv_sems.at[a, k], device_id=to, device_id_type=MESH_ID)

        local = [pltpu.make_async_copy(ins[a], outs[a].at[slot(*me)], local_sems.at[a]) for a in range(n)]
        first = []
        for a in range(n):
            first.append(copy(a, 0, me, sib, src=ins[a]))
            first += [copy(a, 1 + j, me, (*chip, c), src=ins[a]) for j, chip in enumerate(chips)]
        for cp in local + first:
            cp.start()
        passed = []
        for j, chip in enumerate(chips):
            for a in range(n):
                copy(a, 1 + j, (*chip, c), me).wait_recv()
                fwd = copy(a, 4 + j, (*chip, c), sib)
                fwd.start()
                passed.append(fwd)
        for a in range(n):
            copy(a, 0, sib, me).wait_recv()
            for j, chip in enumerate(chips):
                copy(a, 4 + j, (*chip, 1 - c), me).wait_recv()
        for cp in first + passed:
            cp.wait_send()
        for cp in local:
            cp.wait()

    return pl.pallas_call(
        body, name=name, out_shape=[jax.ShapeDtypeStruct((N_DEV,) + a.shape, a.dtype) for a in arrs],
        in_specs=[ANY_SPEC] * nin, out_specs=[ANY_SPEC] * n,
        scratch_shapes=[pltpu.SemaphoreType.DMA((n, N_DEV - 1)), pltpu.SemaphoreType.DMA((n, N_DEV - 1)),
                        pltpu.SemaphoreType.DMA((n,))],
    )(*arrs, *([] if dep is None else [dep]))


HBM_SPEC = pl.BlockSpec(memory_space=pltpu.HBM)
SEM_SPEC = pl.BlockSpec(memory_space=pltpu.SEMAPHORE)
SPLIT_EFFECT = pltpu.SideEffectType.DATAFLOW_SIDE_EFFECTING
SPLIT_SEMS = {"gather": (N_DEV - 1, True), "pair": (N_CHIP, False), "chips": (N_CHIP - 1, True)}


def _split_descriptors(pattern, srcs, lands, sems):
    x, y, c = lax.axis_index("x"), lax.axis_index("y"), lax.axis_index("c")
    nsem, has_local = SPLIT_SEMS[pattern]
    per = 2 * nsem + int(has_local)
    starts, arrivals, local = [], [], []

    def remote(a, k, src, dst, to):
        return pltpu.make_async_remote_copy(src_ref=src, dst_ref=dst, send_sem=sems[a * per + k],
                                            recv_sem=sems[a * per + nsem + k], device_id=to, device_id_type=MESH_ID)

    for a in range(len(srcs)):
        if pattern == "gather":
            me = 4 * x + 2 * y + c
            local.append(pltpu.make_async_copy(srcs[a], lands[a].at[me], sems[a * per + 2 * nsem]))
            for k in range(1, N_DEV):
                px = (1 - x) if (k >> 2) & 1 else x
                py = (1 - y) if (k >> 1) & 1 else y
                pc = (1 - c) if k & 1 else c
                starts.append(remote(a, k - 1, srcs[a], lands[a].at[me], (px, py, pc)))
                arrivals.append(remote(a, k - 1, srcs[a], lands[a].at[4 * px + 2 * py + pc], (px, py, pc)))
        elif pattern == "pair":
            for q in range(N_CHIP):
                cp = remote(a, q, srcs[a].at[2 * q + 1 - c], lands[a].at[q], (x, y, 1 - c))
                starts.append(cp)
                arrivals.append(cp)
        else:
            mine = 2 * x + y
            local.append(pltpu.make_async_copy(srcs[a].at[mine], lands[a].at[mine], sems[a * per + 2 * nsem]))
            for k in range(1, N_CHIP):
                px = (1 - x) if (k >> 1) & 1 else x
                py = (1 - y) if k & 1 else y
                starts.append(remote(a, k - 1, srcs[a].at[2 * px + py], lands[a].at[mine], (px, py, c)))
                arrivals.append(remote(a, k - 1, srcs[a].at[2 * px + py], lands[a].at[2 * px + py], (px, py, c)))
    return starts, arrivals, local


def _split_start(pattern, arrs, name, after=None):
    n = len(arrs)
    extra = [] if after is None else [after]
    nsem, has_local = SPLIT_SEMS[pattern]
    if pattern == "gather":
        land_shapes = [(N_DEV,) + a.shape for a in arrs]
    elif pattern == "pair":
        land_shapes = [(N_CHIP,) + a.shape[1:] for a in arrs]
    else:
        land_shapes = [a.shape for a in arrs]
    nsem_out = n * (2 * nsem + int(has_local))

    def body(*refs):
        srcs, lands = refs[:n], refs[n:2 * n]
        first_sem = 2 * n + len(extra)
        sems = refs[first_sem:first_sem + nsem_out]
        token = refs[-1]
        starts, _, local = _split_descriptors(pattern, srcs, lands, sems)
        for cp in local + starts:
            cp.start()
        token[...] = jnp.zeros_like(token)

    out_shape = ([pltpu.SemaphoreType.DMA(())] * nsem_out + [pltpu.HBM(a.shape, a.dtype) for a in arrs]
                 + [pltpu.HBM(s, a.dtype) for s, a in zip(land_shapes, arrs)] + [jax.ShapeDtypeStruct((8, 128), F32)])
    ins = ([pltpu.with_memory_space_constraint(a, pltpu.HBM) for a in arrs]
           + [pltpu.with_memory_space_constraint(lax.empty(s, a.dtype), pltpu.HBM) for s, a in zip(land_shapes, arrs)])
    outs = pl.pallas_call(
        body, name=name, out_shape=out_shape, in_specs=[HBM_SPEC] * (2 * n) + [ANY_SPEC] * len(extra),
        out_specs=[SEM_SPEC] * nsem_out + [HBM_SPEC] * (2 * n) + [pl.BlockSpec(memory_space=pltpu.VMEM)],
        input_output_aliases={i: nsem_out + i for i in range(2 * n)},
        compiler_params=pltpu.CompilerParams(has_side_effects=SPLIT_EFFECT),
    )(*ins, *extra)
    handle = dict(pattern=pattern, n=n, sems=outs[:nsem_out], srcs=outs[nsem_out:nsem_out + n],
                  lands=outs[nsem_out + n:nsem_out + 2 * n])
    return handle, outs[-1]


def _split_wait(handle, after, name):
    pattern, n = handle["pattern"], handle["n"]
    nsem_in = len(handle["sems"])

    def body(*refs):
        srcs, lands = refs[:n], refs[n:2 * n]
        starts, arrivals, local = _split_descriptors(pattern, srcs, lands, refs[2 * n:2 * n + nsem_in])
        for cp in starts:
            cp.wait_send()
        for cp in arrivals:
            cp.wait_recv()
        for cp in local:
            cp.wait()

    srcs, lands = handle["srcs"], handle["lands"]
    outs = pl.pallas_call(
        body, name=name,
        out_shape=[pltpu.HBM(a.shape, a.dtype) for a in srcs] + [pltpu.HBM(a.shape, a.dtype) for a in lands],
        in_specs=[HBM_SPEC] * (2 * n) + [SEM_SPEC] * nsem_in + [ANY_SPEC], out_specs=[HBM_SPEC] * (2 * n),
        input_output_aliases={i: i for i in range(2 * n)},
        compiler_params=pltpu.CompilerParams(has_side_effects=SPLIT_EFFECT),
    )(*srcs, *lands, *handle["sems"], after)
    return outs[:n], outs[n:]


def _pair_add(core, a8s, b4s, name):
    n = len(a8s)

    def body(core_ref, *refs):
        for i in range(n):
            refs[2 * n + i][...] = (refs[i][...] + refs[n + i][...]).astype(BF16)

    own = [pl.BlockSpec((1,) + b.shape[1:], lambda q, core_ref: (2 * q + core_ref[0], 0, 0)) for b in b4s]
    slot = [pl.BlockSpec((1,) + b.shape[1:], lambda q, core_ref: (q, 0, 0)) for b in b4s]
    grid_spec = pltpu.PrefetchScalarGridSpec(num_scalar_prefetch=1, grid=(N_CHIP,), in_specs=own + slot, out_specs=slot)
    return pl.pallas_call(
        body, name=name, grid_spec=grid_spec, out_shape=[jax.ShapeDtypeStruct(b.shape, BF16) for b in b4s],
        compiler_params=_cparams(("arbitrary",)),
    )(core, *a8s, *b4s)


def _sum_slots(x, name):
    _, r, c = x.shape

    def body(x_ref, o_ref):
        acc = x_ref[0]
        for j in range(1, N_DEV):
            acc = acc + x_ref[j]
        o_ref[...] = acc

    return pl.pallas_call(body, name=name, out_shape=jax.ShapeDtypeStruct((r, c), F32))(x)


def _ada_fwd(c_all, w_ada):
    nb = c_all.shape[0]
    cols = w_ada.shape[2]

    def body(c_ref, w_ref, act_ref, mod_ref):
        cv = c_ref[...]
        act = cv * _sigmoid(cv)
        act_ref[...] = act
        for l in range(DEPTH):
            mod_ref[l] = jnp.dot(act, w_ref[l], precision=HI, preferred_element_type=F32)

    return pl.pallas_call(
        body, name="ada_fwd",
        out_shape=[jax.ShapeDtypeStruct((nb, D_MODEL), F32), jax.ShapeDtypeStruct((DEPTH, nb, cols), F32)],
        compiler_params=_cparams(),
    )(c_all, w_ada)


def _ada_bwd(c_act, dmod_all, dmod_mine, dep):
    nb = c_act.shape[0]
    cols = dmod_mine.shape[2]

    def body(act_ref, dall_ref, dmine_ref, dep_ref, gw_ref, gb_ref):
        act = act_ref[...]
        for l in range(DEPTH):
            gw_ref[l] = lax.dot_general(act, dmine_ref[l], (((0,), (0,)), ((), ())),
                                        precision=HI, preferred_element_type=F32)
            gb_ref[l:l + 1, :] = jnp.sum(dall_ref[l], axis=0, keepdims=True)

    return pl.pallas_call(
        body, name="ada_bwd",
        out_shape=[jax.ShapeDtypeStruct((DEPTH, D_MODEL, cols), F32),
                   jax.ShapeDtypeStruct((DEPTH, 3 * D_MODEL), F32)],
        compiler_params=_cparams(),
    )(c_act, dmod_all, dmod_mine, dep)


def _ln_in(x, ss, g, w, seq, tm=256, dep=None):
    t = x.shape[0]
    tps = seq // tm

    def body(x_ref, ss_ref, g_ref, w_ref, h_ref, *outs):
        xv = x_ref[...]
        xn = xv * lax.rsqrt(jnp.mean(xv * xv, axis=-1, keepdims=True) + EPS)
        h = xn * g_ref[...] * ss_ref[0, 1:2, :] + ss_ref[0, 0:1, :]
        hb = h.astype(BF16)
        h_ref[...] = hb
        z = jnp.dot(hb, w_ref[...], preferred_element_type=F32)
        for o_ref, (_, off, wd, _) in zip(outs, Z_SEGS):
            o_ref[...] = z[:, off:off + wd].astype(o_ref.dtype)

    row = lambda wd: pl.BlockSpec((tm, wd), lambda i: (i, 0))
    in_specs = [row(D_MODEL), pl.BlockSpec((1, 2, D_MODEL), lambda i: (i // tps, 0, 0)),
                pl.BlockSpec((1, D_MODEL), lambda i: (0, 0)), pl.BlockSpec((D_MODEL, N_PAD), lambda i: (0, 0))]
    body, in_specs, args = _after(dep, body, in_specs, [x, ss, g, w])
    return pl.pallas_call(
        body, name="ln_in", grid=(t // tm,), in_specs=in_specs,
        out_specs=[row(D_MODEL)] + [row(wd) for _, _, wd, _ in Z_SEGS],
        out_shape=[jax.ShapeDtypeStruct((t, D_MODEL), BF16)]
        + [jax.ShapeDtypeStruct((t, wd), dt) for _, _, wd, dt in Z_SEGS],
        compiler_params=_cparams(("arbitrary",)),
    )(*args)


def _ln_in_bwd(dz, w_t, x, ss, g, dxo, seq, tm=256, dep=None):
    t = x.shape[0]
    tps = seq // tm
    nb = t // seq
    nz = len(Z_SEGS)

    def body(*refs):
        dz_refs = refs[:nz]
        wt_ref, x_ref, ss_ref, g_ref, dxo_ref, dx_ref, dss_ref, dg_ref = refs[nz:]
        i = pl.program_id(0)
        dzc = jnp.concatenate([r[...].astype(BF16) for r in dz_refs], axis=1)
        dh = jnp.dot(dzc, wt_ref[...], preferred_element_type=F32)
        xv = x_ref[...]
        rstd = lax.rsqrt(jnp.mean(xv * xv, axis=-1, keepdims=True) + EPS)
        xn = xv * rstd
        gv = g_ref[...]
        s1 = ss_ref[0, 1:2, :]
        dxg = dh * s1
        dxn = dxg * gv
        dx = rstd * (dxn - xn * jnp.mean(dxn * xn, axis=-1, keepdims=True))
        dx_ref[...] = dxo_ref[...] + dx
        dshift = jnp.sum(dh, axis=0, keepdims=True)
        dscale = jnp.sum(dh * (xn * gv), axis=0, keepdims=True)
        dgp = jnp.sum(dxg * xn, axis=0, keepdims=True)

        @pl.when(i % tps == 0)
        def _():
            dss_ref[0, 0:1, :] = dshift
            dss_ref[0, 1:2, :] = dscale

        @pl.when(i % tps != 0)
        def _():
            dss_ref[0, 0:1, :] += dshift
            dss_ref[0, 1:2, :] += dscale

        @pl.when(i == 0)
        def _():
            dg_ref[...] = dgp

        @pl.when(i != 0)
        def _():
            dg_ref[...] += dgp

    row = lambda wd: pl.BlockSpec((tm, wd), lambda i: (i, 0))
    in_specs = ([row(wd) for _, _, wd, _ in Z_SEGS]
                + [pl.BlockSpec((N_PAD, D_MODEL), lambda i: (0, 0)), row(D_MODEL),
                   pl.BlockSpec((1, 2, D_MODEL), lambda i: (i // tps, 0, 0)),
                   pl.BlockSpec((1, D_MODEL), lambda i: (0, 0)), row(D_MODEL)])
    body, in_specs, args = _after(dep, body, in_specs, [*dz, w_t, x, ss, g, dxo])
    return pl.pallas_call(
        body, name="ln_in_bwd", grid=(t // tm,), in_specs=in_specs,
        out_specs=[row(D_MODEL), pl.BlockSpec((1, 2, D_MODEL), lambda i: (i // tps, 0, 0)),
                   pl.BlockSpec((1, D_MODEL), lambda i: (0, 0))],
        out_shape=[jax.ShapeDtypeStruct((t, D_MODEL), F32), jax.ShapeDtypeStruct((nb, 2, D_MODEL), F32),
                   jax.ShapeDtypeStruct((1, D_MODEL), F32)],
        compiler_params=_cparams(("arbitrary",)),
    )(*args)


def _matmul_tn(a, bs, name, tm=1024, dep=None):
    bs = list(bs) if isinstance(bs, (list, tuple)) else [bs]
    t, k = a.shape
    widths = [b.shape[1] for b in bs]
    n = sum(widths)
    tm = min(tm, t)

    def body(a_ref, *refs):
        b_refs, o_ref = refs[:-1], refs[-1]
        i = pl.program_id(0)
        av = a_ref[...].astype(BF16)
        parts = [b_ref[...].astype(BF16) for b_ref in b_refs]
        bv = parts[0] if len(parts) == 1 else jnp.concatenate(parts, axis=1)
        part = lax.dot_general(av, bv, (((0,), (0,)), ((), ())), preferred_element_type=F32)

        @pl.when(i == 0)
        def _():
            o_ref[...] = part

        @pl.when(i != 0)
        def _():
            o_ref[...] += part

    in_specs = [pl.BlockSpec((tm, k), lambda i: (i, 0))] + [pl.BlockSpec((tm, wd), lambda i: (i, 0)) for wd in widths]
    body, in_specs, args = _after(dep, body, in_specs, [a, *bs])
    return pl.pallas_call(
        body, name=name, grid=(t // tm,), in_specs=in_specs,
        out_specs=pl.BlockSpec((k, n), lambda i: (0, 0)),
        out_shape=jax.ShapeDtypeStruct((k, n), F32),
        compiler_params=_cparams(("arbitrary",)),
    )(*args)


def _rope(blk, cos_t, sin_a, sin_b):
    return blk * cos_t + pltpu.roll(blk, 112, 1) * sin_a + pltpu.roll(blk, 16, 1) * sin_b


def _unrope(d, cos_t, sin_a, sin_b):
    return d * cos_t + pltpu.roll(d * sin_a, 16, 1) + pltpu.roll(d * sin_b, 112, 1)


def _mla_prep(cq, ckv, kpe, gq, gkv, wuq, wk, wv, cos_t, sin_a, sin_b, tm=256):
    t = cq.shape[0]
    qw = A_HEADS * HEAD_PAD

    def body(cq_ref, ckv_ref, kpe_ref, gq_ref, gkv_ref, wuq_ref, wk_ref, wv_ref, c_ref, sa_ref, sb_ref,
             q_ref, k_ref, v_ref, cqn_ref, ckvn_ref):
        ct, sa, sb = c_ref[...], sa_ref[...], sb_ref[...]
        a = cq_ref[...]
        cqn = (a * lax.rsqrt(jnp.mean(a * a, axis=-1, keepdims=True) + EPS) * gq_ref[...]).astype(BF16)
        cqn_ref[...] = cqn
        b = ckv_ref[...]
        ckvn = (b * lax.rsqrt(jnp.mean(b * b, axis=-1, keepdims=True) + EPS) * gkv_ref[...]).astype(BF16)
        ckvn_ref[...] = ckvn
        qlin = jnp.dot(cqn, wuq_ref[...], preferred_element_type=F32)
        klin = jnp.dot(ckvn, wk_ref[...], preferred_element_type=F32)
        v_ref[...] = jnp.dot(ckvn, wv_ref[...], preferred_element_type=F32).astype(BF16)
        kr = _rope(kpe_ref[...], ct, sa, sb)
        for h in range(A_HEADS):
            sl = slice(h * HEAD_PAD, (h + 1) * HEAD_PAD)
            q_ref[:, sl] = _rope(qlin[:, sl], ct, sa, sb).astype(BF16)
            k_ref[:, sl] = (klin[:, sl] + kr).astype(BF16)

    row = lambda wd: pl.BlockSpec((tm, wd), lambda i: (i, 0))
    full = lambda r, c: pl.BlockSpec((r, c), lambda i: (0, 0))
    return pl.pallas_call(
        body, name="mla_prep", grid=(t // tm,),
        in_specs=[row(A_Q_RANK), row(A_KV_RANK), row(128), full(1, A_Q_RANK), full(1, A_KV_RANK),
                  full(A_Q_RANK, qw), full(A_KV_RANK, qw), full(A_KV_RANK, GW), row(128), row(128), row(128)],
        out_specs=[row(qw), row(qw), row(GW), row(A_Q_RANK), row(A_KV_RANK)],
        out_shape=[jax.ShapeDtypeStruct((t, qw), BF16), jax.ShapeDtypeStruct((t, qw), BF16),
                   jax.ShapeDtypeStruct((t, GW), BF16), jax.ShapeDtypeStruct((t, A_Q_RANK), BF16),
                   jax.ShapeDtypeStruct((t, A_KV_RANK), BF16)],
        compiler_params=_cparams(("arbitrary",)),
    )(cq, ckv, kpe, gq, gkv, wuq, wk, wv, cos_t, sin_a, sin_b)


def _mla_prep_bwd(dq, dk, dv, cq, ckv, gq, gkv, wuq_t, wk_t, wv_t, cos_t, sin_a, sin_b, tm=256):
    t = cq.shape[0]
    qw = A_HEADS * HEAD_PAD

    def body(dq_ref, dk_ref, dv_ref, cq_ref, ckv_ref, gq_ref, gkv_ref, wuqt_ref, wkt_ref, wvt_ref,
             c_ref, sa_ref, sb_ref, dcq_ref, dckv_ref, dkpe_ref, dql_ref, dkl_ref, dgq_ref, dgkv_ref):
        i = pl.program_id(0)
        ct, sa, sb = c_ref[...], sa_ref[...], sb_ref[...]
        lane = lax.broadcasted_iota(jnp.int32, (1, HEAD_PAD), 1)
        nope = lane < A_NOPE
        rope = (lane >= A_NOPE) & (lane < A_NOPE + A_ROPE)
        dksum = None
        for h in range(A_HEADS):
            sl = slice(h * HEAD_PAD, (h + 1) * HEAD_PAD)
            dql_ref[:, sl] = _unrope(dq_ref[:, sl], ct, sa, sb).astype(BF16)
            dkh = dk_ref[:, sl]
            dkl_ref[:, sl] = jnp.where(nope, dkh, 0.0).astype(BF16)
            dksum = dkh if dksum is None else dksum + dkh
        dkpe_ref[...] = jnp.where(rope, _unrope(jnp.where(rope, dksum, 0.0), ct, sa, sb), 0.0)
        dcqn = jnp.dot(dql_ref[...], wuqt_ref[...], preferred_element_type=F32)
        dckvn = (jnp.dot(dkl_ref[...], wkt_ref[...], preferred_element_type=F32)
                 + jnp.dot(dv_ref[...].astype(BF16), wvt_ref[...], preferred_element_type=F32))

        def norm_bwd(xv, gv, dy):
            rstd = lax.rsqrt(jnp.mean(xv * xv, axis=-1, keepdims=True) + EPS)
            xn = xv * rstd
            dxn = dy * gv
            dx = rstd * (dxn - xn * jnp.mean(dxn * xn, axis=-1, keepdims=True))
            return dx, jnp.sum(dy * xn, axis=0, keepdims=True)

        dcq, dgq = norm_bwd(cq_ref[...], gq_ref[...], dcqn)
        dckv, dgkv = norm_bwd(ckv_ref[...], gkv_ref[...], dckvn)
        dcq_ref[...] = dcq
        dckv_ref[...] = dckv

        @pl.when(i == 0)
        def _():
            dgq_ref[...] = dgq
            dgkv_ref[...] = dgkv

        @pl.when(i != 0)
        def _():
            dgq_ref[...] += dgq
            dgkv_ref[...] += dgkv

    row = lambda wd: pl.BlockSpec((tm, wd), lambda i: (i, 0))
    full = lambda r, c: pl.BlockSpec((r, c), lambda i: (0, 0))
    return pl.pallas_call(
        body, name="mla_prep_bwd", grid=(t // tm,),
        in_specs=[row(qw), row(qw), row(GW), row(A_Q_RANK), row(A_KV_RANK), full(1, A_Q_RANK), full(1, A_KV_RANK),
                  full(qw, A_Q_RANK), full(qw, A_KV_RANK), full(GW, A_KV_RANK), row(128), row(128), row(128)],
        out_specs=[row(A_Q_RANK), row(A_KV_RANK), row(128), row(qw), row(qw), full(1, A_Q_RANK), full(1, A_KV_RANK)],
        out_shape=[jax.ShapeDtypeStruct((t, A_Q_RANK), F32), jax.ShapeDtypeStruct((t, A_KV_RANK), F32),
                   jax.ShapeDtypeStruct((t, 128), F32), jax.ShapeDtypeStruct((t, qw), BF16),
                   jax.ShapeDtypeStruct((t, qw), BF16), jax.ShapeDtypeStruct((1, A_Q_RANK), F32),
                   jax.ShapeDtypeStruct((1, A_KV_RANK), F32)],
        compiler_params=_cparams(("arbitrary",)),
    )(dq, dk, dv, cq, ckv, gq, gkv, wuq_t, wk_t, wv_t, cos_t, sin_a, sin_b)


def _nt(a, b):
    return lax.dot_general(a, b, (((1,), (1,)), ((), ())), preferred_element_type=F32)


def _tn(a, b):
    return lax.dot_general(a, b, (((0,), (0,)), ((), ())), preferred_element_type=F32)


def _causal_mask(kind, q0, k0, tq, tk):
    qpos = q0 + lax.broadcasted_iota(jnp.int32, (tq, tk), 0)
    kpos = k0 + lax.broadcasted_iota(jnp.int32, (tq, tk), 1)
    if kind == "mla":
        return lax.shift_right_logical(kpos, 6) <= lax.shift_right_logical(qpos, 6)
    return kpos <= qpos


def _attn_fwd(kind, q, k, v, f, seq, scale, tq=512, tk=512):
    t = v.shape[0]
    nb = t // seq
    nq = seq // tq
    hw = 256 if kind == "mla" else 128
    use_f = f is not None
    tq, tk = min(tq, seq), min(tk, seq)
    nq = seq // tq
    assert tk % tq == 0

    def body(*refs):
        if use_f:
            q_ref, k_ref, v_ref, f_ref, o_ref, st_ref = refs
        else:
            q_ref, k_ref, v_ref, o_ref, st_ref = refs
        qi = pl.program_id(2)
        q0 = qi * tq
        lane = lax.broadcasted_iota(jnp.int32, (1, 128), 1)
        half = lane >= 64
        qall = q_ref[...]
        if kind == "mla":
            qhs = [qall[:, 0:128], qall[:, 128:256]]
        else:
            qhs = [jnp.where(half, jnp.zeros_like(qall), qall), jnp.where(half, qall, jnp.zeros_like(qall))]
        nfull = q0 // tk
        kd = pl.multiple_of(nfull * tk, tk)
        diag = _causal_mask(kind, q0 - kd, 0, tq, tk)

        def block(j, k0, state, masked):
            m, l, acc = state
            kh = k_ref[pl.ds(k0, tk), j * 128:(j + 1) * 128] if kind == "mla" else k_ref[pl.ds(k0, tk), :]
            s = _nt(qhs[j], kh) * scale
            if use_f:
                s = s - f_ref[0, 0, j:j + 1, pl.ds(k0, tk)]
            if masked:
                s = jnp.where(diag, s, NEG)
            mn = jnp.maximum(m, jnp.max(s, axis=-1, keepdims=True))
            alpha = jnp.exp(m - mn)
            p = jnp.exp(s - mn)
            l = alpha * l + jnp.sum(p, axis=-1, keepdims=True)
            acc = alpha * acc + jnp.dot(p.astype(BF16), v_ref[pl.ds(k0, tk), :], preferred_element_type=F32)
            return mn, l, acc

        def kstep(kb, carry):
            k0 = pl.multiple_of(kb * tk, tk)
            return block(0, k0, carry[:3], False) + block(1, k0, carry[3:], False)

        init = (jnp.full((tq, 1), NEG, F32), jnp.zeros((tq, 1), F32), jnp.zeros((tq, 128), F32)) * 2
        carry = lax.fori_loop(0, nfull, kstep, init)
        m0, l0, a0 = block(0, kd, carry[:3], True)
        m1, l1, a1 = block(1, kd, carry[3:], True)
        o_ref[...] = jnp.where(half, a1 / l1, a0 / l0)
        st_ref[...] = jnp.where(lane == 0, m0 + jnp.log(l0), jnp.where(lane == 1, m1 + jnp.log(l1), 0.0))

    in_specs = [pl.BlockSpec((tq, hw), lambda b, p, i: (b * nq + i, p)),
                pl.BlockSpec((seq, hw), lambda b, p, i: (b, p)),
                pl.BlockSpec((seq, 128), lambda b, p, i: (b, p))]
    args = [q, k, v]
    if use_f:
        in_specs.append(pl.BlockSpec((1, 1, 8, seq), lambda b, p, i: (b, p, 0, 0)))
        args.append(f)
    oblk = pl.BlockSpec((tq, 128), lambda b, p, i: (b * nq + i, p))
    return pl.pallas_call(
        body, name="attn_fwd_" + kind, grid=(nb, 3, nq), in_specs=in_specs, out_specs=[oblk, oblk],
        out_shape=[jax.ShapeDtypeStruct((t, GW), F32), jax.ShapeDtypeStruct((t, GW), F32)],
        compiler_params=_cparams(("arbitrary", "arbitrary", "arbitrary")),
    )(*args)


def _attn_bwd(kind, q, k, v, f, o, st, do, seq, scale, tq=512, tk=512, dep=None):
    t = v.shape[0]
    nb = t // seq
    tq, tk = min(tq, seq), min(tk, seq)
    nq = seq // tq
    nk = seq // tk
    hw = 256 if kind == "mla" else 128
    use_f = f is not None
    assert tq == tk

    def body(*refs):
        if use_f:
            q_ref, k_ref, v_ref, f_ref, o_ref, st_ref, do_ref, dq_ref, dk_ref, dv_ref, df_ref, dfq_ref = refs
        else:
            q_ref, k_ref, v_ref, o_ref, st_ref, do_ref, dq_ref, dk_ref, dv_ref = refs
        kj = pl.program_id(2)
        k0 = kj * tk
        lane = lax.broadcasted_iota(jnp.int32, (1, 128), 1)
        half = lane >= 64

        @pl.when(kj == 0)
        def _():
            dq_ref[...] = jnp.zeros_like(dq_ref)
            if use_f:
                dfq_ref[...] = jnp.zeros_like(dfq_ref)

        dk_ref[...] = jnp.zeros_like(dk_ref)
        dv_ref[...] = jnp.zeros_like(dv_ref)
        if use_f:
            df_ref[...] = jnp.zeros_like(df_ref)
        vv = v_ref[...]
        diag = _causal_mask(kind, 0, 0, tq, tk)

        def qstep(qi, masked):
            q0 = pl.multiple_of(qi * tq, tq)
            rows = pl.ds(q0, tq)
            dov = do_ref[rows, :]
            dd = dov * o_ref[rows, :]
            stv = st_ref[rows, :]
            for j in range(2):
                hm = half == bool(j)
                delta = jnp.sum(jnp.where(hm, dd, 0.0), axis=-1, keepdims=True)
                lse = stv[:, j:j + 1]
                if kind == "mla":
                    cols = slice(j * 128, (j + 1) * 128)
                    qh = q_ref[rows, cols]
                    kh = k_ref[:, cols]
                else:
                    cols = slice(0, 128)
                    qa = q_ref[rows, :]
                    qh = jnp.where(hm, qa, jnp.zeros_like(qa))
                    kh = k_ref[...]
                s = _nt(qh, kh) * scale
                if use_f:
                    s = s - f_ref[0, 0, j:j + 1, :]
                if masked:
                    s = jnp.where(diag, s, NEG)
                p = jnp.exp(s - lse)
                doh = jnp.where(hm, dov, 0.0).astype(BF16)
                ds = p * (_nt(doh, vv) - delta)
                dsb = (ds * scale).astype(BF16)
                dv_ref[...] += _tn(p.astype(BF16), doh)
                dk_ref[:, cols] += _tn(dsb, qh)
                dqc = jnp.dot(dsb, kh, preferred_element_type=F32)
                if kind != "mla":
                    dqc = jnp.where(hm, dqc, 0.0)
                dq_ref[rows, cols] += dqc
                if use_f:
                    df_ref[0, 0, j:j + 1, :] += -jnp.sum(ds, axis=0, keepdims=True)
                    dfq_ref[rows, :] += jnp.where(lane == j, jnp.sum(ds, axis=-1, keepdims=True), 0.0)

        qstep(kj, True)

        def rest(qi, carry):
            qstep(qi, False)
            return carry

        lax.fori_loop(kj + 1, nq, rest, 0)

    full_q = lambda wd: pl.BlockSpec((seq, wd), lambda b, p, i: (b, p))
    kblk = lambda wd: pl.BlockSpec((tk, wd), lambda b, p, i: (b * nk + i, p))
    in_specs = [full_q(hw), kblk(hw), kblk(128)]
    args = [q, k, v]
    if use_f:
        in_specs.append(pl.BlockSpec((1, 1, 8, tk), lambda b, p, i: (b, p, 0, i)))
        args.append(f)
    in_specs += [full_q(128), full_q(128), full_q(128)]
    args += [o, st, do]
    out_specs = [full_q(hw), kblk(hw), kblk(128)]
    out_shape = [jax.ShapeDtypeStruct((t, 3 * hw), F32), jax.ShapeDtypeStruct((t, 3 * hw), F32),
                 jax.ShapeDtypeStruct((t, GW), F32)]
    if use_f:
        out_specs += [pl.BlockSpec((1, 1, 8, tk), lambda b, p, i: (b, p, 0, i)), full_q(128)]
        out_shape += [jax.ShapeDtypeStruct((nb, 3, 8, seq), F32), jax.ShapeDtypeStruct((t, GW), F32)]
    body, in_specs, args = _after(dep, body, in_specs, args)
    return pl.pallas_call(
        body, name="attn_bwd_" + kind, grid=(nb, 3, nk), in_specs=in_specs, out_specs=out_specs,
        out_shape=out_shape, compiler_params=_cparams(("arbitrary", "arbitrary", "arbitrary")),
    )(*args)


BQ = 256
BWIN = BQ + B_LEFT


def _band_geometry():
    r = lax.broadcasted_iota(jnp.int32, (BQ, BWIN), 0)
    j = lax.broadcasted_iota(jnp.int32, (BQ, BWIN), 1)
    rc = lax.shift_right_logical(r, 6)
    jc = lax.shift_right_logical(j, 6)
    allowed = (jc - 8 <= rc) & (rc <= jc)
    return (r + B_LEFT - j) >= REL_CLIP, allowed, j < r


def _band_onehot(transposed, offset=0):
    shape = (BWIN, GW) if transposed else (GW, BWIN)
    kk = lax.broadcasted_iota(jnp.int32, shape, 1 if transposed else 0)
    x = lax.broadcasted_iota(jnp.int32, shape, 0 if transposed else 1) - offset
    x = jnp.where(x < 0, x + BWIN, x)
    return (kk == jnp.clip(B_LEFT - x, -REL_CLIP, REL_CLIP) + REL_CLIP).astype(F32)


def _band_table(rel_bias8):
    def body(b_ref, o_ref):
        hh = pl.program_id(0)
        u8 = jnp.dot(b_ref[...], _band_onehot(False), precision=HI, preferred_element_type=F32)
        rid = lax.broadcasted_iota(jnp.int32, (8, BWIN), 0)
        row = jnp.sum(jnp.where(rid == hh, u8, 0.0), axis=0, keepdims=True)
        far, allowed, _ = _band_geometry()
        tbl = pltpu.roll(jnp.broadcast_to(row, (BQ, BWIN)), 0, 1, stride=1, stride_axis=0)
        tbl = jnp.where(far, row[:, 0:1], tbl)
        o_ref[0] = jnp.where(allowed, tbl, NEG)

    return pl.pallas_call(
        body, name="band_table", grid=(6,),
        in_specs=[pl.BlockSpec((8, GW), lambda h: (0, 0))],
        out_specs=pl.BlockSpec((1, BQ, BWIN), lambda h: (h, 0, 0)),
        out_shape=jax.ShapeDtypeStruct((6, BQ, BWIN), F32),
        compiler_params=_cparams(("arbitrary",)),
    )(rel_bias8)


def _band_table_bwd(gtab):
    def body(g_ref, o_ref):
        gv = g_ref[0]
        _, _, wrapped = _band_geometry()
        gfar = jnp.sum(jnp.sum(jnp.where(wrapped, gv, 0.0), axis=-1, keepdims=True), axis=0, keepdims=True)
        anti = (lax.broadcasted_iota(jnp.int32, (BQ, BQ), 0) + lax.broadcasted_iota(jnp.int32, (BQ, BQ), 1)
                == BQ - 1).astype(F32)
        grev = jnp.dot(anti, jnp.where(wrapped, 0.0, gv), precision=HI, preferred_element_type=F32)
        near = pltpu.roll(grev, 0, 1, stride=1, stride_axis=0)
        y = jnp.broadcast_to(jnp.sum(near, axis=0, keepdims=True), (8, BWIN))
        gb = jnp.dot(y, _band_onehot(True, BQ - 1), precision=HI, preferred_element_type=F32)
        lane = lax.broadcasted_iota(jnp.int32, (8, GW), 1)
        o_ref[0] = gb + jnp.where(lane == 2 * REL_CLIP, gfar, 0.0)

    return pl.pallas_call(
        body, name="band_table_bwd", grid=(B_HEADS,),
        in_specs=[pl.BlockSpec((1, BQ, BWIN), lambda h: (h, 0, 0))],
        out_specs=pl.BlockSpec((1, 8, GW), lambda h: (h, 0, 0)),
        out_shape=jax.ShapeDtypeStruct((B_HEADS, 8, GW), F32),
        compiler_params=_cparams(("arbitrary",)),
    )(gtab)


def _band_fwd(q, k, v, table, seq, scale):
    t = q.shape[0]
    nb = t // seq
    nq = seq // BQ

    def body(q_ref, k_ref, v_ref, tb_ref, o_ref, st_ref, kpad, vpad):
        qi = pl.program_id(2)
        q0 = pl.multiple_of(qi * BQ, BQ)
        lane = lax.broadcasted_iota(jnp.int32, (1, 128), 1)
        half = lane >= 64

        @pl.when(qi == 0)
        def _():
            kpad[0:B_LEFT, :] = jnp.zeros((B_LEFT, 128), BF16)
            vpad[0:B_LEFT, :] = jnp.zeros((B_LEFT, 128), BF16)
            kpad[B_LEFT:, :] = k_ref[...]
            vpad[B_LEFT:, :] = v_ref[...]

        kw = kpad[pl.ds(q0, BWIN), :]
        vw = vpad[pl.ds(q0, BWIN), :]
        inside = lax.broadcasted_iota(jnp.int32, (BQ, BWIN), 1) >= B_LEFT - q0
        qall = q_ref[...]
        outs, lses = [], []
        for j in range(2):
            qh = jnp.where(half == bool(j), qall, jnp.zeros_like(qall))
            s = jnp.where(inside, _nt(qh, kw) * scale + tb_ref[j], NEG)
            m = jnp.max(s, axis=-1, keepdims=True)
            p = jnp.exp(s - m)
            l = jnp.sum(p, axis=-1, keepdims=True)
            outs.append(jnp.dot(p.astype(BF16), vw, preferred_element_type=F32) / l)
            lses.append(m + jnp.log(l))
        o_ref[...] = jnp.where(half, outs[1], outs[0])
        st_ref[...] = jnp.where(lane == 0, lses[0], jnp.where(lane == 1, lses[1], 0.0))

    qblk = pl.BlockSpec((BQ, 128), lambda b, p, i: (b * nq + i, p))
    full = pl.BlockSpec((seq, 128), lambda b, p, i: (b, p))
    return pl.pallas_call(
        body, name="band_fwd", grid=(nb, 3, nq),
        in_specs=[qblk, full, full, pl.BlockSpec((2, BQ, BWIN), lambda b, p, i: (p, 0, 0))],
        out_specs=[qblk, qblk],
        out_shape=[jax.ShapeDtypeStruct((t, GW), F32), jax.ShapeDtypeStruct((t, GW), F32)],
        scratch_shapes=[pltpu.VMEM((seq + B_LEFT, 128), BF16), pltpu.VMEM((seq + B_LEFT, 128), BF16)],
        compiler_params=_cparams(("arbitrary", "arbitrary", "arbitrary")),
    )(q, k, v, table)


def _band_bwd(q, k, v, table, o, st, do, seq, scale, dep=None):
    t = q.shape[0]
    nb = t // seq
    nq = seq // BQ

    def body(q_ref, k_ref, v_ref, tb_ref, o_ref, st_ref, do_ref, dq_ref, dk_ref, dv_ref, g_ref,
             kpad, vpad, dkpad, dvpad):
        b = pl.program_id(1)
        qi = pl.program_id(2)
        q0 = pl.multiple_of(qi * BQ, BQ)
        lane = lax.broadcasted_iota(jnp.int32, (1, 128), 1)
        half = lane >= 64

        @pl.when(qi == 0)
        def _():
            kpad[0:B_LEFT, :] = jnp.zeros((B_LEFT, 128), BF16)
            vpad[0:B_LEFT, :] = jnp.zeros((B_LEFT, 128), BF16)
            kpad[B_LEFT:, :] = k_ref[...]
            vpad[B_LEFT:, :] = v_ref[...]
            dkpad[...] = jnp.zeros_like(dkpad)
            dvpad[...] = jnp.zeros_like(dvpad)

        @pl.when((qi == 0) & (b == 0))
        def _():
            g_ref[...] = jnp.zeros_like(g_ref)

        win = pl.ds(q0, BWIN)
        kw = kpad[win, :]
        vw = vpad[win, :]
        inside = lax.broadcasted_iota(jnp.int32, (BQ, BWIN), 1) >= B_LEFT - q0
        qall = q_ref[...]
        dov = do_ref[...]
        dd = dov * o_ref[...]
        stv = st_ref[...]
        dq = jnp.zeros((BQ, 128), F32)
        for j in range(2):
            hm = half == bool(j)
            qh = jnp.where(hm, qall, jnp.zeros_like(qall))
            delta = jnp.sum(jnp.where(hm, dd, 0.0), axis=-1, keepdims=True)
            s = jnp.where(inside, _nt(qh, kw) * scale + tb_ref[j], NEG)
            p = jnp.exp(s - stv[:, j:j + 1])
            doh = jnp.where(hm, dov, 0.0).astype(BF16)
            ds = p * (_nt(doh, vw) - delta)
            g_ref[j] += ds
            dsb = (ds * scale).astype(BF16)
            dvpad[win, :] += _tn(p.astype(BF16), doh)
            dkpad[win, :] += _tn(dsb, qh)
            dq = dq + jnp.where(hm, jnp.dot(dsb, kw, preferred_element_type=F32), 0.0)
        dq_ref[...] = dq

        @pl.when(qi == nq - 1)
        def _():
            dk_ref[...] = dkpad[B_LEFT:, :]
            dv_ref[...] = dvpad[B_LEFT:, :]

    qblk = pl.BlockSpec((BQ, 128), lambda p, b, i: (b * nq + i, p))
    full = pl.BlockSpec((seq, 128), lambda p, b, i: (b, p))
    tblk = pl.BlockSpec((2, BQ, BWIN), lambda p, b, i: (p, 0, 0))
    body, in_specs, args = _after(dep, body, [qblk, full, full, tblk, qblk, qblk, qblk], [q, k, v, table, o, st, do])
    return pl.pallas_call(
        body, name="band_bwd", grid=(3, nb, nq),
        in_specs=in_specs,
        out_specs=[qblk, full, full, tblk],
        out_shape=[jax.ShapeDtypeStruct((t, GW), F32), jax.ShapeDtypeStruct((t, GW), F32),
                   jax.ShapeDtypeStruct((t, GW), F32), jax.ShapeDtypeStruct((6, BQ, BWIN), F32)],
        scratch_shapes=[pltpu.VMEM((seq + B_LEFT, 128), BF16), pltpu.VMEM((seq + B_LEFT, 128), BF16),
                        pltpu.VMEM((seq + B_LEFT, 128), F32), pltpu.VMEM((seq + B_LEFT, 128), F32)],
        compiler_params=_cparams(("arbitrary", "arbitrary", "arbitrary")),
    )(*args)


def _fox_prep(cf, fb, seq):
    nb = cf.shape[0] // seq
    nblk = seq // 128

    def body(cf_ref, fb_ref, f_ref):
        x = cf_ref[...] + fb_ref[...]
        lf = jnp.minimum(x, 0.0) - jnp.log1p(jnp.exp(-jnp.abs(x)))
        rows = lf.T[0:8, :]
        upper = (lax.broadcasted_iota(jnp.int32, (128, 128), 0)
                 <= lax.broadcasted_iota(jnp.int32, (128, 128), 1)).astype(F32)
        carry = jnp.zeros((8, 1), F32)
        for blk in range(nblk):
            sl = slice(blk * 128, (blk + 1) * 128)
            cs = jnp.dot(rows[:, sl], upper, precision=HI, preferred_element_type=F32) + carry
            carry = cs[:, 127:128]
            f_ref[0, 0, :, sl] = cs
            f_ref[0, 1, :, sl] = pltpu.roll(cs, 6, 0)
            f_ref[0, 2, :, sl] = pltpu.roll(cs, 4, 0)

    return pl.pallas_call(
        body, name="fox_prep", grid=(nb,),
        in_specs=[pl.BlockSpec((seq, 128), lambda b: (b, 0)), pl.BlockSpec((1, 128), lambda b: (0, 0))],
        out_specs=pl.BlockSpec((1, 3, 8, seq), lambda b: (b, 0, 0, 0)),
        out_shape=jax.ShapeDtypeStruct((nb, 3, 8, seq), F32),
        compiler_params=_cparams(("arbitrary",)),
    )(cf, fb)


def _fox_prep_bwd(df, dfq, cf, fb, seq):
    nb = cf.shape[0] // seq
    nblk = seq // 128

    def body(df_ref, dfq_ref, cf_ref, fb_ref, dcf_ref, dfb_ref, wide):
        b = pl.program_id(0)
        row = lax.broadcasted_iota(jnp.int32, (8, seq), 0)
        dfh = None
        for p in range(3):
            both = df_ref[0, p] + dfq_ref[:, p * 128:(p + 1) * 128].T[0:8, :]
            both = jnp.where(row < 2, both, 0.0)
            if p:
                both = pltpu.roll(both, 2 * p, 0)
            dfh = both if dfh is None else dfh + both
        lower = (lax.broadcasted_iota(jnp.int32, (128, 128), 0)
                 >= lax.broadcasted_iota(jnp.int32, (128, 128), 1)).astype(F32)
        wide[...] = jnp.zeros_like(wide)
        carry = jnp.zeros((8, 1), F32)
        for blk in reversed(range(nblk)):
            sl = slice(blk * 128, (blk + 1) * 128)
            rc = jnp.dot(dfh[:, sl], lower, precision=HI, preferred_element_type=F32) + carry
            carry = rc[:, 0:1]
            wide[0:8, sl] = rc
        dl = wide[...].T
        x = cf_ref[...] + fb_ref[...]
        dcf = dl * (1.0 / (1.0 + jnp.exp(x)))
        dcf_ref[...] = dcf
        part = jnp.sum(dcf, axis=0, keepdims=True)

        @pl.when(b == 0)
        def _():
            dfb_ref[...] = part

        @pl.when(b != 0)
        def _():
            dfb_ref[...] += part

    return pl.pallas_call(
        body, name="fox_prep_bwd", grid=(nb,),
        in_specs=[pl.BlockSpec((1, 3, 8, seq), lambda b: (b, 0, 0, 0)), pl.BlockSpec((seq, GW), lambda b: (b, 0)),
                  pl.BlockSpec((seq, 128), lambda b: (b, 0)), pl.BlockSpec((1, 128), lambda b: (0, 0))],
        out_specs=[pl.BlockSpec((seq, 128), lambda b: (b, 0)), pl.BlockSpec((1, 128), lambda b: (0, 0))],
        out_shape=[jax.ShapeDtypeStruct(cf.shape, F32), jax.ShapeDtypeStruct((1, 128), F32)],
        scratch_shapes=[pltpu.VMEM((128, seq), F32)],
        compiler_params=_cparams(("arbitrary",)),
    )(df, dfq, cf, fb)


def _gate_out(oa, ob, oc, gates, w, x, gate, seq, tm=256):
    t = x.shape[0]
    tps = seq // tm

    def body(oa_ref, ob_ref, oc_ref, g_ref, w_ref, x_ref, gt_ref, xo_ref, y_ref, u_ref):
        for n, o_ref in enumerate((oa_ref, ob_ref, oc_ref)):
            sl = slice(n * GW, (n + 1) * GW)
            gv = g_ref[:, sl]
            u_ref[:, sl] = (o_ref[...] * (gv * _sigmoid(gv))).astype(BF16)
        y = jnp.dot(u_ref[...], w_ref[...], preferred_element_type=F32)
        y_ref[...] = y
        xo_ref[...] = x_ref[...] + gt_ref[0] * y

    row = lambda wd: pl.BlockSpec((tm, wd), lambda i: (i, 0))
    return pl.pallas_call(
        body, name="gate_out", grid=(t // tm,),
        in_specs=[row(GW), row(GW), row(GW), row(U_PAD), pl.BlockSpec((U_PAD, D_MODEL), lambda i: (0, 0)),
                  row(D_MODEL), pl.BlockSpec((1, 1, D_MODEL), lambda i: (i // tps, 0, 0))],
        out_specs=[row(D_MODEL), row(D_MODEL), row(U_PAD)],
        out_shape=[jax.ShapeDtypeStruct((t, D_MODEL), F32), jax.ShapeDtypeStruct((t, D_MODEL), F32),
                   jax.ShapeDtypeStruct((t, U_PAD), BF16)],
        compiler_params=_cparams(("arbitrary",)),
    )(oa, ob, oc, gates, w, x, gate)


def _gate_out_bwd(dxo, y, gate, oa, ob, oc, gates, w_t, seq, tm=256, dep=None):
    t = dxo.shape[0]
    tps = seq // tm
    nb = t // seq

    def body(dxo_ref, y_ref, gt_ref, oa_ref, ob_ref, oc_ref, g_ref, wt_ref,
             dy_ref, doa_ref, dob_ref, doc_ref, dg_ref, dgt_ref):
        i = pl.program_id(0)
        dxo_v = dxo_ref[...]
        dgt = jnp.sum(dxo_v * y_ref[...], axis=0, keepdims=True)
        dyb = (dxo_v * gt_ref[0]).astype(BF16)
        dy_ref[...] = dyb
        du = jnp.dot(dyb, wt_ref[...], preferred_element_type=F32)
        for n, (o_ref, do_ref) in enumerate(((oa_ref, doa_ref), (ob_ref, dob_ref), (oc_ref, doc_ref))):
            sl = slice(n * GW, (n + 1) * GW)
            gv = g_ref[:, sl]
            sg = _sigmoid(gv)
            dun = du[:, sl]
            do_ref[...] = dun * (gv * sg)
            dg_ref[:, sl] = dun * o_ref[...] * (sg * (1.0 + gv * (1.0 - sg)))

        @pl.when(i % tps == 0)
        def _():
            dgt_ref[0] = dgt

        @pl.when(i % tps != 0)
        def _():
            dgt_ref[0] += dgt

    row = lambda wd: pl.BlockSpec((tm, wd), lambda i: (i, 0))
    per_b = pl.BlockSpec((1, 1, D_MODEL), lambda i: (i // tps, 0, 0))
    in_specs = [row(D_MODEL), row(D_MODEL), per_b, row(GW), row(GW), row(GW), row(U_PAD),
                pl.BlockSpec((D_MODEL, U_PAD), lambda i: (0, 0))]
    body, in_specs, args = _after(dep, body, in_specs, [dxo, y, gate, oa, ob, oc, gates, w_t])
    return pl.pallas_call(
        body, name="gate_out_bwd", grid=(t // tm,), in_specs=in_specs,
        out_specs=[row(D_MODEL), row(GW), row(GW), row(GW), row(U_PAD), per_b],
        out_shape=[jax.ShapeDtypeStruct((t, D_MODEL), BF16), jax.ShapeDtypeStruct((t, GW), F32),
                   jax.ShapeDtypeStruct((t, GW), F32), jax.ShapeDtypeStruct((t, GW), F32),
                   jax.ShapeDtypeStruct((t, U_PAD), F32), jax.ShapeDtypeStruct((nb, 1, D_MODEL), F32)],
        compiler_params=_cparams(("arbitrary",)),
    )(*args)


def _final_loss(x, target, g, tm=256):
    t = x.shape[0]

    def body(x_ref, t_ref, g_ref, dx_ref, loss_ref, dg_ref):
        i = pl.program_id(0)
        xv = x_ref[...]
        rstd = lax.rsqrt(jnp.mean(xv * xv, axis=-1, keepdims=True) + EPS)
        xn = xv * rstd
        gv = g_ref[...]
        err = xn * gv - t_ref[...]
        dy = err * (1.0 / D_MODEL)
        dxn = dy * gv
        dx_ref[...] = rstd * (dxn - xn * jnp.mean(dxn * xn, axis=-1, keepdims=True))
        lp = jnp.sum(err * err, axis=0, keepdims=True) * (0.5 / D_MODEL)
        dgp = jnp.sum(dy * xn, axis=0, keepdims=True)

        @pl.when(i == 0)
        def _():
            loss_ref[...] = lp
            dg_ref[...] = dgp

        @pl.when(i != 0)
        def _():
            loss_ref[...] += lp
            dg_ref[...] += dgp

    row = pl.BlockSpec((tm, D_MODEL), lambda i: (i, 0))
    vec = pl.BlockSpec((1, D_MODEL), lambda i: (0, 0))
    return pl.pallas_call(
        body, name="final_loss", grid=(t // tm,),
        in_specs=[row, row, vec], out_specs=[row, vec, vec],
        out_shape=[jax.ShapeDtypeStruct((t, D_MODEL), F32), jax.ShapeDtypeStruct((1, D_MODEL), F32),
                   jax.ShapeDtypeStruct((1, D_MODEL), F32)],
        compiler_params=_cparams(("arbitrary",)),
    )(x, target, g)


def _adamw(w, gslots, m, v, name, tr=None):
    r, c = w.shape
    ns = gslots.shape[0]
    tr = r if tr is None else tr

    def body(w_ref, g_ref, m_ref, v_ref, go_ref, d_ref, mo_ref, vo_ref):
        g = g_ref[0].astype(F32)
        for j in range(1, ns):
            g = g + g_ref[j].astype(F32)
        mn = ADAM_B1 * m_ref[...] + (1.0 - ADAM_B1) * g
        vn = ADAM_B2 * v_ref[...] + (1.0 - ADAM_B2) * jnp.square(g)
        m_hat = mn / (1.0 - ADAM_B1 ** ADAM_STEP)
        v_hat = vn / (1.0 - ADAM_B2 ** ADAM_STEP)
        go_ref[...] = g
        d_ref[...] = -ADAM_LR * (m_hat / (jnp.sqrt(v_hat) + ADAM_EPS) + ADAM_WD * w_ref[...])
        mo_ref[...] = mn
        vo_ref[...] = vn

    blk = pl.BlockSpec((tr, c), lambda i: (i, 0))
    return pl.pallas_call(
        body, name=name, grid=(r // tr,),
        in_specs=[blk, pl.BlockSpec((ns, tr, c), lambda i: (0, i, 0)), blk, blk],
        out_specs=[blk] * 4, out_shape=[jax.ShapeDtypeStruct((r, c), F32)] * 4,
        compiler_params=_cparams(("arbitrary",)),
    )(w, gslots, m, v)


def _rope_tables(positions):
    inv = ROPE_THETA ** (-jnp.arange(0, A_ROPE, 2, dtype=F32) / A_ROPE)
    ang = positions.astype(F32)[:, None] * inv
    cos, sin = jnp.cos(ang), jnp.sin(ang)
    t = positions.shape[0]
    one = jnp.ones((t, 64), F32)
    zero16 = jnp.zeros((t, 16), F32)
    cos_t = jnp.concatenate([one, cos, cos, jnp.ones((t, 32), F32)], axis=1)
    sin_a = jnp.concatenate([jnp.zeros((t, 64), F32), -sin, zero16, jnp.zeros((t, 32), F32)], axis=1)
    sin_b = jnp.concatenate([jnp.zeros((t, 64), F32), zero16, sin, jnp.zeros((t, 32), F32)], axis=1)
    return cos_t, sin_a, sin_b


def _pad_heads(w, real, padded, nheads, axis):
    shp = w.shape[:axis] + (nheads, real) + w.shape[axis + 1:]
    w = w.reshape(shp)
    pad = [(0, 0)] * w.ndim
    pad[axis + 1] = (0, padded - real)
    w = jnp.pad(w, pad)
    return w.reshape(w.shape[:axis] + (nheads * padded,) + w.shape[axis + 2:])


def kernel(x, c, positions, w_ada, b_ada, norm_g, w_in, a_q_norm_g, a_w_uq, a_kv_norm_g, a_w_ukv, b_rel_bias, c_forget_b, w_out, final_g, loss_target, m_w_ada, m_b_ada, m_norm_g, m_w_in, m_a_q_norm_g, m_a_w_uq, m_a_kv_norm_g, m_a_w_ukv, m_b_rel_bias, m_c_forget_b, m_w_out, m_final_g, v_w_ada, v_b_ada, v_norm_g, v_w_in, v_a_q_norm_g, v_a_w_uq, v_a_kv_norm_g, v_a_w_ukv, v_b_rel_bias, v_c_forget_b, v_w_out, v_final_g):
    nb, seq, _ = x.shape
    t = nb * seq
    me = 4 * lax.axis_index("x") + 2 * lax.axis_index("y") + lax.axis_index("c")
    x2 = x.reshape(t, D_MODEL)
    tgt = loss_target.reshape(t, D_MODEL)
    cos_t, sin_a, sin_b = _rope_tables(positions.reshape(t))

    def shards(l):
        return [_pad_runs(w_in[l].astype(BF16), IN_RUNS, N_PAD, 1), w_out[l].astype(BF16),
                a_w_uq[l].astype(BF16), a_w_ukv[l].astype(BF16)]

    def prepare(gi, go, gq, gkv):
        wi = gi.reshape(D_MODEL, N_PAD)
        wo = _pad_runs(go.reshape(D_MODEL, D_MODEL), OUT_RUNS, U_PAD, 0)
        wq = jnp.transpose(gq, (1, 0, 2)).reshape(A_Q_RANK, A_HEADS * (A_NOPE + A_ROPE))
        wq = _pad_heads(wq, A_NOPE + A_ROPE, HEAD_PAD, A_HEADS, 1)
        wkv = jnp.transpose(gkv, (1, 0, 2)).reshape(A_KV_RANK, A_HEADS, 2 * A_NOPE)
        wk = jnp.pad(wkv[:, :, :A_NOPE], ((0, 0), (0, 0), (0, HEAD_PAD - A_NOPE))).reshape(A_KV_RANK, A_HEADS * HEAD_PAD)
        wv = wkv[:, :, A_NOPE:].reshape(A_KV_RANK, GW)
        return dict(w_in=wi, w_in_t=wi.T, w_out=wo, w_out_t=wo.T, wuq=wq, wuq_t=wq.T, wk=wk, wk_t=wk.T,
                    wv=wv, wv_t=wv.T)

    gathered = _gather(shards(0) + [c], "gather_weights0")
    c_all = gathered[-1].reshape(N_DEV * nb, D_MODEL)
    weights = [prepare(*gathered[:4]), None]

    c_act, mod_cols = _ada_fwd(c_all, w_ada)
    (mod_g,) = _gather([mod_cols], "gather_mod")
    gather1, gather1_token = _split_start("gather", shards(1), "gather_weights1_start", after=mod_g)
    mod_all = jnp.transpose(mod_g, (1, 2, 0, 3)).reshape(DEPTH, N_DEV * nb, 3 * D_MODEL)
    mod = lax.dynamic_slice_in_dim(mod_all, me * nb, nb, axis=1) + b_ada[:, None, :]

    fb_pad = jnp.pad(c_forget_b, ((0, 0), (0, 128 - C_HEADS)))
    a_scale = (A_NOPE + A_ROPE) ** -0.5
    h_scale = CHUNK ** -0.5

    saved = []
    xl = x2
    for l in range(DEPTH):
        if l == 1:
            weights[1] = prepare(*_split_wait(gather1, xl, "gather_weights1_wait")[1])
        w = weights[l]
        shift, scale, gate = mod[l, :, :D_MODEL], mod[l, :, D_MODEL:2 * D_MODEL], mod[l, :, 2 * D_MODEL:]
        ss = jnp.stack([shift, 1.0 + scale], axis=1)
        gate3 = gate[:, None, :]
        h, cq, ckv, kpe, gates, bq, bk, bv, cq2, ck, cv, cf = _ln_in(
            xl, ss, norm_g[l:l + 1], w["w_in"], seq, dep=gather1_token if l == 0 else None)
        q, k, v, cqn, ckvn = _mla_prep(cq, ckv, kpe, a_q_norm_g[l:l + 1], a_kv_norm_g[l:l + 1],
                                       w["wuq"], w["wk"], w["wv"], cos_t, sin_a, sin_b)
        oa, sta = _attn_fwd("mla", q, k, v, None, seq, a_scale)
        table = _band_table(jnp.pad(b_rel_bias[l], ((0, 8 - B_HEADS), (0, GW - N_REL))))
        ob, stb = _band_fwd(bq, bk, bv, table, seq, h_scale)
        fcum = _fox_prep(cf, fb_pad[l:l + 1], seq)
        oc, stc = _attn_fwd("fox", cq2, ck, cv, fcum, seq, h_scale)
        xn, y, u = _gate_out(oa, ob, oc, gates, w["w_out"], xl, gate3, seq)
        saved.append(dict(x=xl, ss=ss, gate3=gate3, h=h, cq=cq, ckv=ckv, gates=gates, bq=bq, bk=bk, bv=bv,
                          cq2=cq2, ck=ck, cv=cv, cf=cf, q=q, k=k, v=v, cqn=cqn, ckvn=ckvn, oa=oa, sta=sta,
                          table=table, ob=ob, stb=stb, fcum=fcum, oc=oc, stc=stc, y=y, u=u))
        xl = xn

    dx, loss_lanes, g_final = _final_loss(xl, tgt, final_g[None, :])
    loss = lax.psum(jnp.sum(loss_lanes), AXES)

    rows = D_MODEL // N_DEV
    core = lax.axis_index("c").astype(jnp.int32).reshape(1)
    n_seg_a = 4
    dmods, smalls, parts = [None] * DEPTH, [None] * DEPTH, [None] * DEPTH
    pair1 = chips1 = pair1_token = chips1_token = None
    for l in reversed(range(DEPTH)):
        s, w = saved[l], weights[l]
        dy, doa, dob, doc, dgates, dgate = _gate_out_bwd(dx, s["y"], s["gate3"], s["oa"], s["ob"], s["oc"],
                                                         s["gates"], w["w_out_t"], seq, dep=pair1_token)
        g_out = _unpad_runs(_matmul_tn(s["u"], dy, "dw_out"), OUT_RUNS, 0)
        if l == 0:
            own, from_sib = _split_wait(pair1, g_out, "grads1_pair_wait")
            chips1, chips1_token = _split_start("chips", _pair_add(core, own, from_sib, "grads1_add"), "grads1_chips_start")
        dq, dk, dv = _attn_bwd("mla", s["q"], s["k"], s["v"], None, s["oa"], s["sta"], doa, seq, a_scale,
                               dep=chips1_token)
        dbq, dbk, dbv, gtab = _band_bwd(s["bq"], s["bk"], s["bv"], s["table"], s["ob"], s["stb"], dob, seq, h_scale,
                                        dep=chips1_token)
        g_rel = _band_table_bwd(gtab)[:, 0, :N_REL]
        dcq2, dck, dcv, dfc, dfq = _attn_bwd("fox", s["cq2"], s["ck"], s["cv"], s["fcum"], s["oc"], s["stc"], doc,
                                             seq, h_scale, dep=chips1_token)
        dcf, dfb = _fox_prep_bwd(dfc, dfq, s["cf"], fb_pad[l:l + 1], seq)
        dcq, dckv, dkpe, dqlin, dklin, dgq, dgkv = _mla_prep_bwd(
            dq, dk, dv, s["cq"], s["ckv"], a_q_norm_g[l:l + 1], a_kv_norm_g[l:l + 1],
            w["wuq_t"], w["wk_t"], w["wv_t"], cos_t, sin_a, sin_b)
        gq_pad = _matmul_tn(s["cqn"], dqlin, "dw_uq")
        g_uq = gq_pad.reshape(A_Q_RANK, A_HEADS, HEAD_PAD)[:, :, :A_NOPE + A_ROPE].reshape(A_Q_RANK, -1)
        gkv_pad = _matmul_tn(s["ckvn"], [dklin, dv], "dw_ukv")
        gk_pad = gkv_pad[:, :A_HEADS * HEAD_PAD].reshape(A_KV_RANK, A_HEADS, HEAD_PAD)[:, :, :A_NOPE]
        gv_pad = gkv_pad[:, A_HEADS * HEAD_PAD:].reshape(A_KV_RANK, A_HEADS, A_NOPE)
        g_ukv = jnp.concatenate([gk_pad, gv_pad], axis=2).reshape(A_KV_RANK, -1)
        dz = [dcq, dckv, dkpe, dgates, dbq, dbk, dbv, dcq2, dck, dcv, dcf]
        g_in_a = _matmul_tn(s["h"], dz[:n_seg_a], "dw_in_a")
        first = [g_in_a.reshape(N_DEV, rows, -1), g_out.reshape(N_DEV, rows, D_MODEL),
                 g_uq.reshape(A_Q_RANK, N_DEV, -1).transpose(1, 0, 2), g_ukv.reshape(A_KV_RANK, N_DEV, -1).transpose(1, 0, 2)]
        if l == 1:
            g_in_b = _matmul_tn(s["h"], dz[n_seg_a:], "dw_in_b")
            pair1, pair1_token = _split_start("pair", first + [g_in_b.reshape(N_DEV, rows, -1)], "grads1_pair_start")
            tail_token = None
        else:
            pair0a, pair0a_token = _split_start("pair", first, "grads0a_pair_start")
            g_in_b = _matmul_tn(s["h"], dz[n_seg_a:], "dw_in_b", dep=pair0a_token)
            own, from_sib = _split_wait(pair0a, g_in_b, "grads0a_pair_wait")
            sums0a = _pair_add(core, own, from_sib, "grads0a_add")
            pair0b, pair0b_token = _split_start("pair", [g_in_b.reshape(N_DEV, rows, -1)], "grads0b_pair_start",
                                                after=sums0a[0])
            chips0a, tail_token = _split_start("chips", sums0a, "grads0a_chips_start", after=pair0b_token)
        dx, dss, dg_norm = _ln_in_bwd(dz, w["w_in_t"], s["x"], s["ss"], norm_g[l:l + 1], dx, seq, dep=tail_token)
        dmods[l] = jnp.concatenate([dss[:, 0, :], dss[:, 1, :], dgate[:, 0, :]], axis=1)
        smalls[l] = [dg_norm.reshape(-1), dgq.reshape(-1), dgkv.reshape(-1), g_rel.reshape(-1),
                     dfb[0, :C_HEADS]]
    grad_x = dx.reshape(nb, seq, D_MODEL)
    parts[1] = _split_wait(chips1, dx, "grads1_chips_wait")[1]
    parts0a = _split_wait(chips0a, dx, "grads0a_chips_wait")[1]
    own, from_sib = _split_wait(pair0b, dx, "grads0b_pair_wait")

    small = jnp.concatenate([p for l in range(DEPTH) for p in smalls[l]] + [g_final.reshape(-1)])
    n_small = small.shape[0]
    small_rows = -(-n_small // 1024) * 8
    small = jnp.pad(small, (0, small_rows * 128 - n_small)).reshape(small_rows, 128)
    dmod_local = jnp.stack(dmods)
    dmod_g, small_g = _gather([dmod_local, small], "gather_small", dep=parts0a[0])
    chips0, chips0_token = _split_start("chips", _pair_add(core, own, from_sib, "grads0b_add"), "grads0b_chips_start",
                                        after=small_g)
    dmod_all = jnp.transpose(dmod_g, (1, 0, 2, 3)).reshape(DEPTH, N_DEV * nb, 3 * D_MODEL)
    cols = 3 * D_MODEL // N_DEV
    dmod_mine = lax.dynamic_slice_in_dim(dmod_all, me * cols, cols, axis=2)
    g_w_ada, g_b_ada = _ada_bwd(c_act, dmod_all, dmod_mine, chips0_token)
    small_sum = _sum_slots(small_g, "sum_small").reshape(-1)

    def split_small():
        out, pos = [], 0
        sizes = [D_MODEL, A_Q_RANK, A_KV_RANK, B_HEADS * N_REL, C_HEADS]
        per_layer = []
        for l in range(DEPTH):
            parts = []
            for sz in sizes:
                parts.append(small_sum[pos:pos + sz])
                pos += sz
            per_layer.append(parts)
        for j in range(len(sizes)):
            out.append(jnp.stack([per_layer[l][j] for l in range(DEPTH)]))
        out.append(small_sum[pos:pos + D_MODEL])
        return out

    g_norm, g_qn, g_kvn, g_relb, g_fb, g_fin = split_small()

    def adam(w, g, m, v, name, tr=None):
        shp = w.shape
        w2 = w.reshape(-1, shp[-1]) if w.ndim > 1 else w.reshape(1, -1)
        gs = g.reshape((-1,) + w2.shape) if g.size != w.size else g.reshape((1,) + w2.shape)
        outs = _adamw(w2, gs, m.reshape(w2.shape), v.reshape(w2.shape), name, tr)
        return [o.reshape(shp) for o in outs]

    res = {
        "w_ada": adam(w_ada, g_w_ada, m_w_ada, v_w_ada, "adam_w_ada", 256),
        "b_ada": adam(b_ada, g_b_ada, m_b_ada, v_b_ada, "adam_b_ada"),
        "norm_g": adam(norm_g, g_norm, m_norm_g, v_norm_g, "adam_norm_g"),
        "a_q_norm_g": adam(a_q_norm_g, g_qn, m_a_q_norm_g, v_a_q_norm_g, "adam_q_norm"),
        "a_kv_norm_g": adam(a_kv_norm_g, g_kvn, m_a_kv_norm_g, v_a_kv_norm_g, "adam_kv_norm"),
        "b_rel_bias": adam(b_rel_bias, g_relb.reshape(b_rel_bias.shape), m_b_rel_bias, v_b_rel_bias, "adam_rel_bias"),
        "c_forget_b": adam(c_forget_b, g_fb, m_c_forget_b, v_c_forget_b, "adam_forget_b"),
        "final_g": adam(final_g, g_fin, m_final_g, v_final_g, "adam_final_g"),
    }
    parts[0] = list(parts0a) + list(_split_wait(chips0, res["w_ada"][1], "grads0b_chips_wait")[1])
    p_in = jnp.stack([_unpad_runs(jnp.concatenate([parts[l][0], parts[l][4]], axis=2), IN_RUNS, 2)
                      for l in range(DEPTH)], axis=1)
    p_out, p_uq, p_ukv = (jnp.stack([parts[l][j] for l in range(DEPTH)], axis=1) for j in (1, 2, 3))
    res.update({
        "w_in": adam(w_in, p_in, m_w_in, v_w_in, "adam_w_in", 32),
        "a_w_uq": adam(a_w_uq, p_uq, m_a_w_uq, v_a_w_uq, "adam_w_uq"),
        "a_w_ukv": adam(a_w_ukv, p_ukv, m_a_w_ukv, v_a_w_ukv, "adam_w_ukv"),
        "w_out": adam(w_out, p_out, m_w_out, v_w_out, "adam_w_out", 64),
    })
    names = ["w_ada", "b_ada", "norm_g", "w_in", "a_q_norm_g", "a_w_uq", "a_kv_norm_g", "a_w_ukv", "b_rel_bias",
             "c_forget_b", "w_out", "final_g"]
    outs = [loss, grad_x]
    for j in range(4):
        outs += [res[n][j] for n in names]
    return tuple(outs)
```

```python
import functools

import jax
import jax.numpy as jnp
from jax import lax
from jax.experimental import pallas as pl
from jax.experimental.pallas import tpu as pltpu

F32 = jnp.float32
BF16 = jnp.bfloat16
HI = lax.Precision.HIGHEST

N_DEV = 8
AXES = ("x", "y", "c")
D_MODEL = 1024
DEPTH = 2
CHUNK = 64
EPS = 1e-6
NEG = -1e30
A_HEADS = 6
A_NOPE = 64
A_ROPE = 32
A_Q_RANK = 384
A_KV_RANK = 256
ROPE_THETA = 10000.0
B_HEADS = 5
B_LEFT = 512
REL_CLIP = 128
N_REL = 2 * REL_CLIP + 1
C_HEADS = 5
HEAD_PAD = 128
GW = 384
N_IN = 3621
ADAM_LR = 0.001
ADAM_B1 = 0.9
ADAM_B2 = 0.999
ADAM_EPS = 1e-08
ADAM_WD = 0.01
ADAM_STEP = 10
VMEM_LIMIT = 56 * 1024 * 1024

Z_SEGS = (
    ("cq", 0, 384, F32), ("ckv", 384, 256, F32), ("kpe", 640, 128, F32), ("gates", 768, 1152, F32),
    ("bq", 1920, 384, BF16), ("bk", 2304, 384, BF16), ("bv", 2688, 384, BF16),
    ("cq2", 3072, 384, BF16), ("ck", 3456, 384, BF16), ("cv", 3840, 384, BF16), ("cf", 4224, 128, F32),
)
N_PAD = 4352
IN_RUNS = (
    (0, 384, 0), (384, 256, 384), (640 + 64, 32, 640),
    (768, 384, 672), (768 + 384, 320, 2016), (768 + 768, 320, 3301),
    (1920, 320, 1056), (2304, 320, 1376), (2688, 320, 1696),
    (3072, 320, 2336), (3456, 320, 2656), (3840, 320, 2976), (4224, 5, 3296),
)
OUT_RUNS = ((0, 384, 0), (384, 320, 384), (768, 320, 704))
U_PAD = 1152


def _cparams(sem=None, vmem=VMEM_LIMIT):
    return pltpu.CompilerParams(dimension_semantics=sem, vmem_limit_bytes=vmem)


def _after(dep, body, in_specs, args):
    if dep is None:
        return body, in_specs, args
    n = len(args)

    def ordered(*refs):
        return body(*refs[:n], *refs[n + 1:])

    return ordered, list(in_specs) + [pl.BlockSpec((8, 128), lambda *_: (0, 0))], list(args) + [dep]


def _pad_runs(w, runs, total, axis):
    order = sorted(runs)
    parts, pos = [], 0
    for off, wd, src in order:
        if off > pos:
            shp = list(w.shape)
            shp[axis] = off - pos
            parts.append(jnp.zeros(shp, w.dtype))
        parts.append(lax.slice_in_dim(w, src, src + wd, axis=axis))
        pos = off + wd
    if pos < total:
        shp = list(w.shape)
        shp[axis] = total - pos
        parts.append(jnp.zeros(shp, w.dtype))
    return jnp.concatenate(parts, axis=axis)


def _unpad_runs(w, runs, axis):
    order = sorted(runs, key=lambda r: r[2])
    return jnp.concatenate([lax.slice_in_dim(w, off, off + wd, axis=axis) for off, wd, _ in order], axis=axis)


def _sigmoid(x):
    return 1.0 / (1.0 + jnp.exp(-x))


N_CHIP = 4
ANY_SPEC = pl.BlockSpec(memory_space=pl.ANY)
MESH_ID = pl.DeviceIdType.MESH


def _gather(arrs, name, dep=None):
    n = len(arrs)
    nin = n + (dep is not None)

    def body(*refs):
        ins, outs = refs[:n], refs[nin:nin + n]
        send_sems, recv_sems, local_sems = refs[nin + n:]
        x, y, c = lax.axis_index("x"), lax.axis_index("y"), lax.axis_index("c")
        me, sib = (x, y, c), (x, y, 1 - c)
        chips = [(1 - x, y), (x, 1 - y), (1 - x, 1 - y)]

        def slot(px, py, pc):
            return 4 * px + 2 * py + pc

        def copy(a, k, block, to, src=None):
            dst = outs[a].at[slot(*block)]
            return pltpu.make_async_remote_copy(
                src_ref=dst if src is None else src, dst_ref=dst, send_sem=send_sems.at[a, k],
                recv_sem=recv_sems.at[a, k], device_id=to, device_id_type=MESH_ID)

        local = [pltpu.make_async_copy(ins[a], outs[a].at[slot(*me)], local_sems.at[a]) for a in range(n)]
        first = []
        for a in range(n):
            first.append(copy(a, 0, me, sib, src=ins[a]))
            first += [copy(a, 1 + j, me, (*chip, c), src=ins[a]) for j, chip in enumerate(chips)]
        for cp in local + first:
            cp.start()
        passed = []
        for j, chip in enumerate(chips):
            for a in range(n):
                copy(a, 1 + j, (*chip, c), me).wait_recv()
                fwd = copy(a, 4 + j, (*chip, c), sib)
                fwd.start()
                passed.append(fwd)
        for a in range(n):
            copy(a, 0, sib, me).wait_recv()
            for j, chip in enumerate(chips):
                copy(a, 4 + j, (*chip, 1 - c), me).wait_recv()
        for cp in first + passed:
            cp.wait_send()
        for cp in local:
            cp.wait()

    return pl.pallas_call(
        body, name=name, out_shape=[jax.ShapeDtypeStruct((N_DEV,) + a.shape, a.dtype) for a in arrs],
        in_specs=[ANY_SPEC] * nin, out_specs=[ANY_SPEC] * n,
        scratch_shapes=[pltpu.SemaphoreType.DMA((n, N_DEV - 1)), pltpu.SemaphoreType.DMA((n, N_DEV - 1)),
                        pltpu.SemaphoreType.DMA((n,))],
    )(*arrs, *([] if dep is None else [dep]))


HBM_SPEC = pl.BlockSpec(memory_space=pltpu.HBM)
SEM_SPEC = pl.BlockSpec(memory_space=pltpu.SEMAPHORE)
SPLIT_EFFECT = pltpu.SideEffectType.DATAFLOW_SIDE_EFFECTING
SPLIT_SEMS = {"gather": (N_DEV - 1, True), "pair": (N_CHIP, False), "chips": (N_CHIP - 1, True)}


def _split_descriptors(pattern, srcs, lands, sems):
    x, y, c = lax.axis_index("x"), lax.axis_index("y"), lax.axis_index("c")
    nsem, has_local = SPLIT_SEMS[pattern]
    per = 2 * nsem + int(has_local)
    starts, arrivals, local = [], [], []

    def remote(a, k, src, dst, to):
        return pltpu.make_async_remote_copy(src_ref=src, dst_ref=dst, send_sem=sems[a * per + k],
                                            recv_sem=sems[a * per + nsem + k], device_id=to, device_id_type=MESH_ID)

    for a in range(len(srcs)):
        if pattern == "gather":
            me = 4 * x + 2 * y + c
            local.append(pltpu.make_async_copy(srcs[a], lands[a].at[me], sems[a * per + 2 * nsem]))
            for k in range(1, N_DEV):
                px = (1 - x) if (k >> 2) & 1 else x
                py = (1 - y) if (k >> 1) & 1 else y
                pc = (1 - c) if k & 1 else c
                starts.append(remote(a, k - 1, srcs[a], lands[a].at[me], (px, py, pc)))
                arrivals.append(remote(a, k - 1, srcs[a], lands[a].at[4 * px + 2 * py + pc], (px, py, pc)))
        elif pattern == "pair":
            for q in range(N_CHIP):
                cp = remote(a, q, srcs[a].at[2 * q + 1 - c], lands[a].at[q], (x, y, 1 - c))
                starts.append(cp)
                arrivals.append(cp)
        else:
            mine = 2 * x + y
            local.append(pltpu.make_async_copy(srcs[a].at[mine], lands[a].at[mine], sems[a * per + 2 * nsem]))
            for k in range(1, N_CHIP):
                px = (1 - x) if (k >> 1) & 1 else x
                py = (1 - y) if k & 1 else y
                starts.append(remote(a, k - 1, srcs[a].at[2 * px + py], lands[a].at[mine], (px, py, c)))
                arrivals.append(remote(a, k - 1, srcs[a].at[2 * px + py], lands[a].at[2 * px + py], (px, py, c)))
    return starts, arrivals, local


def _split_start(pattern, arrs, name, after=None):
    n = len(arrs)
    extra = [] if after is None else [after]
    nsem, has_local = SPLIT_SEMS[pattern]
    if pattern == "gather":
        land_shapes = [(N_DEV,) + a.shape for a in arrs]
    elif pattern == "pair":
        land_shapes = [(N_CHIP,) + a.shape[1:] for a in arrs]
    else:
        land_shapes = [a.shape for a in arrs]
    nsem_out = n * (2 * nsem + int(has_local))

    def body(*refs):
        srcs, lands = refs[:n], refs[n:2 * n]
        first_sem = 2 * n + len(extra)
        sems = refs[first_sem:first_sem + nsem_out]
        token = refs[-1]
        starts, _, local = _split_descriptors(pattern, srcs, lands, sems)
        for cp in local + starts:
            cp.start()
        token[...] = jnp.zeros_like(token)

    out_shape = ([pltpu.SemaphoreType.DMA(())] * nsem_out + [pltpu.HBM(a.shape, a.dtype) for a in arrs]
                 + [pltpu.HBM(s, a.dtype) for s, a in zip(land_shapes, arrs)] + [jax.ShapeDtypeStruct((8, 128), F32)])
    ins = ([pltpu.with_memory_space_constraint(a, pltpu.HBM) for a in arrs]
           + [pltpu.with_memory_space_constraint(lax.empty(s, a.dtype), pltpu.HBM) for s, a in zip(land_shapes, arrs)])
    outs = pl.pallas_call(
        body, name=name, out_shape=out_shape, in_specs=[HBM_SPEC] * (2 * n) + [ANY_SPEC] * len(extra),
        out_specs=[SEM_SPEC] * nsem_out + [HBM_SPEC] * (2 * n) + [pl.BlockSpec(memory_space=pltpu.VMEM)],
        input_output_aliases={i: nsem_out + i for i in range(2 * n)},
        compiler_params=pltpu.CompilerParams(has_side_effects=SPLIT_EFFECT),
    )(*ins, *extra)
    handle = dict(pattern=pattern, n=n, sems=outs[:nsem_out], srcs=outs[nsem_out:nsem_out + n],
                  lands=outs[nsem_out + n:nsem_out + 2 * n])
    return handle, outs[-1]


def _split_wait(handle, after, name):
    pattern, n = handle["pattern"], handle["n"]
    nsem_in = len(handle["sems"])

    def body(*refs):
        srcs, lands = refs[:n], refs[n:2 * n]
        starts, arrivals, local = _split_descriptors(pattern, srcs, lands, refs[2 * n:2 * n + nsem_in])
        for cp in starts:
            cp.wait_send()
        for cp in arrivals:
            cp.wait_recv()
        for cp in local:
            cp.wait()

    srcs, lands = handle["srcs"], handle["lands"]
    outs = pl.pallas_call(
        body, name=name,
        out_shape=[pltpu.HBM(a.shape, a.dtype) for a in srcs] + [pltpu.HBM(a.shape, a.dtype) for a in lands],
        in_specs=[HBM_SPEC] * (2 * n) + [SEM_SPEC] * nsem_in + [ANY_SPEC], out_specs=[HBM_SPEC] * (2 * n),
        input_output_aliases={i: i for i in range(2 * n)},
        compiler_params=pltpu.CompilerParams(has_side_effects=SPLIT_EFFECT),
    )(*srcs, *lands, *handle["sems"], after)
    return outs[:n], outs[n:]


def _pair_add(core, a8s, b4s, name):
    n = len(a8s)

    def body(core_ref, *refs):
        for i in range(n):
            refs[2 * n + i][...] = (refs[i][...] + refs[n + i][...]).astype(BF16)

    own = [pl.BlockSpec((1,) + b.shape[1:], lambda q, core_ref: (2 * q + core_ref[0], 0, 0)) for b in b4s]
    slot = [pl.BlockSpec((1,) + b.shape[1:], lambda q, core_ref: (q, 0, 0)) for b in b4s]
    grid_spec = pltpu.PrefetchScalarGridSpec(num_scalar_prefetch=1, grid=(N_CHIP,), in_specs=own + slot, out_specs=slot)
    return pl.pallas_call(
        body, name=name, grid_spec=grid_spec, out_shape=[jax.ShapeDtypeStruct(b.shape, BF16) for b in b4s],
        compiler_params=_cparams(("arbitrary",)),
    )(core, *a8s, *b4s)


def _sum_slots(x, name):
    _, r, c = x.shape

    def body(x_ref, o_ref):
        acc = x_ref[0]
        for j in range(1, N_DEV):
            acc = acc + x_ref[j]
        o_ref[...] = acc

    return pl.pallas_call(body, name=name, out_shape=jax.ShapeDtypeStruct((r, c), F32))(x)


def _ada_fwd(c_all, w_ada):
    nb = c_all.shape[0]
    cols = w_ada.shape[2]

    def body(c_ref, w_ref, act_ref, mod_ref):
        cv = c_ref[...]
        act = cv * _sigmoid(cv)
        act_ref[...] = act
        for l in range(DEPTH):
            mod_ref[l] = jnp.dot(act, w_ref[l], precision=HI, preferred_element_type=F32)

    return pl.pallas_call(
        body, name="ada_fwd",
        out_shape=[jax.ShapeDtypeStruct((nb, D_MODEL), F32), jax.ShapeDtypeStruct((DEPTH, nb, cols), F32)],
        compiler_params=_cparams(),
    )(c_all, w_ada)


def _ada_bwd(c_act, dmod_all, dmod_mine, dep):
    nb = c_act.shape[0]
    cols = dmod_mine.shape[2]

    def body(act_ref, dall_ref, dmine_ref, dep_ref, gw_ref, gb_ref):
        act = act_ref[...]
        for l in range(DEPTH):
            gw_ref[l] = lax.dot_general(act, dmine_ref[l], (((0,), (0,)), ((), ())),
                                        precision=HI, preferred_element_type=F32)
            gb_ref[l:l + 1, :] = jnp.sum(dall_ref[l], axis=0, keepdims=True)

    return pl.pallas_call(
        body, name="ada_bwd",
        out_shape=[jax.ShapeDtypeStruct((DEPTH, D_MODEL, cols), F32),
                   jax.ShapeDtypeStruct((DEPTH, 3 * D_MODEL), F32)],
        compiler_params=_cparams(),
    )(c_act, dmod_all, dmod_mine, dep)


def _ln_in(x, ss, g, w, seq, tm=256, dep=None):
    t = x.shape[0]
    tps = seq // tm

    def body(x_ref, ss_ref, g_ref, w_ref, h_ref, *outs):
        xv = x_ref[...]
        xn = xv * lax.rsqrt(jnp.mean(xv * xv, axis=-1, keepdims=True) + EPS)
        h = xn * g_ref[...] * ss_ref[0, 1:2, :] + ss_ref[0, 0:1, :]
        hb = h.astype(BF16)
        h_ref[...] = hb
        z = jnp.dot(hb, w_ref[...], preferred_element_type=F32)
        for o_ref, (_, off, wd, _) in zip(outs, Z_SEGS):
            o_ref[...] = z[:, off:off + wd].astype(o_ref.dtype)

    row = lambda wd: pl.BlockSpec((tm, wd), lambda i: (i, 0))
    in_specs = [row(D_MODEL), pl.BlockSpec((1, 2, D_MODEL), lambda i: (i // tps, 0, 0)),
                pl.BlockSpec((1, D_MODEL), lambda i: (0, 0)), pl.BlockSpec((D_MODEL, N_PAD), lambda i: (0, 0))]
    body, in_specs, args = _after(dep, body, in_specs, [x, ss, g, w])
    return pl.pallas_call(
        body, name="ln_in", grid=(t // tm,), in_specs=in_specs,
        out_specs=[row(D_MODEL)] + [row(wd) for _, _, wd, _ in Z_SEGS],
        out_shape=[jax.ShapeDtypeStruct((t, D_MODEL), BF16)]
        + [jax.ShapeDtypeStruct((t, wd), dt) for _, _, wd, dt in Z_SEGS],
        compiler_params=_cparams(("arbitrary",)),
    )(*args)


def _ln_in_bwd(dz, w_t, x, ss, g, dxo, seq, tm=256, dep=None):
    t = x.shape[0]
    tps = seq // tm
    nb = t // seq
    nz = len(Z_SEGS)

    def body(*refs):
        dz_refs = refs[:nz]
        wt_ref, x_ref, ss_ref, g_ref, dxo_ref, dx_ref, dss_ref, dg_ref = refs[nz:]
        i = pl.program_id(0)
        dzc = jnp.concatenate([r[...].astype(BF16) for r in dz_refs], axis=1)
        dh = jnp.dot(dzc, wt_ref[...], preferred_element_type=F32)
        xv = x_ref[...]
        rstd = lax.rsqrt(jnp.mean(xv * xv, axis=-1, keepdims=True) + EPS)
        xn = xv * rstd
        gv = g_ref[...]
        s1 = ss_ref[0, 1:2, :]
        dxg = dh * s1
        dxn = dxg * gv
        dx = rstd * (dxn - xn * jnp.mean(dxn * xn, axis=-1, keepdims=True))
        dx_ref[...] = dxo_ref[...] + dx
        dshift = jnp.sum(dh, axis=0, keepdims=True)
        dscale = jnp.sum(dh * (xn * gv), axis=0, keepdims=True)
        dgp = jnp.sum(dxg * xn, axis=0, keepdims=True)

        @pl.when(i % tps == 0)
        def _():
            dss_ref[0, 0:1, :] = dshift
            dss_ref[0, 1:2, :] = dscale

        @pl.when(i % tps != 0)
        def _():
            dss_ref[0, 0:1, :] += dshift
            dss_ref[0, 1:2, :] += dscale

        @pl.when(i == 0)
        def _():
            dg_ref[...] = dgp

        @pl.when(i != 0)
        def _():
            dg_ref[...] += dgp

    row = lambda wd: pl.BlockSpec((tm, wd), lambda i: (i, 0))
    in_specs = ([row(wd) for _, _, wd, _ in Z_SEGS]
                + [pl.BlockSpec((N_PAD, D_MODEL), lambda i: (0, 0)), row(D_MODEL),
                   pl.BlockSpec((1, 2, D_MODEL), lambda i: (i // tps, 0, 0)),
                   pl.BlockSpec((1, D_MODEL), lambda i: (0, 0)), row(D_MODEL)])
    body, in_specs, args = _after(dep, body, in_specs, [*dz, w_t, x, ss, g, dxo])
    return pl.pallas_call(
        body, name="ln_in_bwd", grid=(t // tm,), in_specs=in_specs,
        out_specs=[row(D_MODEL), pl.BlockSpec((1, 2, D_MODEL), lambda i: (i // tps, 0, 0)),
                   pl.BlockSpec((1, D_MODEL), lambda i: (0, 0))],
        out_shape=[jax.ShapeDtypeStruct((t, D_MODEL), F32), jax.ShapeDtypeStruct((nb, 2, D_MODEL), F32),
                   jax.ShapeDtypeStruct((1, D_MODEL), F32)],
        compiler_params=_cparams(("arbitrary",)),
    )(*args)


def _matmul_tn(a, bs, name, tm=1024, dep=None):
    bs = list(bs) if isinstance(bs, (list, tuple)) else [bs]
    t, k = a.shape
    widths = [b.shape[1] for b in bs]
    n = sum(widths)
    tm = min(tm, t)

    def body(a_ref, *refs):
        b_refs, o_ref = refs[:-1], refs[-1]
        i = pl.program_id(0)
        av = a_ref[...].astype(BF16)
        parts = [b_ref[...].astype(BF16) for b_ref in b_refs]
        bv = parts[0] if len(parts) == 1 else jnp.concatenate(parts, axis=1)
        part = lax.dot_general(av, bv, (((0,), (0,)), ((), ())), preferred_element_type=F32)

        @pl.when(i == 0)
        def _():
            o_ref[...] = part

        @pl.when(i != 0)
        def _():
            o_ref[...] += part

    in_specs = [pl.BlockSpec((tm, k), lambda i: (i, 0))] + [pl.BlockSpec((tm, wd), lambda i: (i, 0)) for wd in widths]
    body, in_specs, args = _after(dep, body, in_specs, [a, *bs])
    return pl.pallas_call(
        body, name=name, grid=(t // tm,), in_specs=in_specs,
        out_specs=pl.BlockSpec((k, n), lambda i: (0, 0)),
        out_shape=jax.ShapeDtypeStruct((k, n), F32),
        compiler_params=_cparams(("arbitrary",)),
    )(*args)


def _rope(blk, cos_t, sin_a, sin_b):
    return blk * cos_t + pltpu.roll(blk, 112, 1) * sin_a + pltpu.roll(blk, 16, 1) * sin_b


def _unrope(d, cos_t, sin_a, sin_b):
    return d * cos_t + pltpu.roll(d * sin_a, 16, 1) + pltpu.roll(d * sin_b, 112, 1)


def _mla_prep(cq, ckv, kpe, gq, gkv, wuq, wk, wv, cos_t, sin_a, sin_b, tm=256):
    t = cq.shape[0]
    qw = A_HEADS * HEAD_PAD

    def body(cq_ref, ckv_ref, kpe_ref, gq_ref, gkv_ref, wuq_ref, wk_ref, wv_ref, c_ref, sa_ref, sb_ref,
             q_ref, k_ref, v_ref, cqn_ref, ckvn_ref):
        ct, sa, sb = c_ref[...], sa_ref[...], sb_ref[...]
        a = cq_ref[...]
        cqn = (a * lax.rsqrt(jnp.mean(a * a, axis=-1, keepdims=True) + EPS) * gq_ref[...]).astype(BF16)
        cqn_ref[...] = cqn
        b = ckv_ref[...]
        ckvn = (b * lax.rsqrt(jnp.mean(b * b, axis=-1, keepdims=True) + EPS) * gkv_ref[...]).astype(BF16)
        ckvn_ref[...] = ckvn
        qlin = jnp.dot(cqn, wuq_ref[...], preferred_element_type=F32)
        klin = jnp.dot(ckvn, wk_ref[...], preferred_element_type=F32)
        v_ref[...] = jnp.dot(ckvn, wv_ref[...], preferred_element_type=F32).astype(BF16)
        kr = _rope(kpe_ref[...], ct, sa, sb)
        for h in range(A_HEADS):
            sl = slice(h * HEAD_PAD, (h + 1) * HEAD_PAD)
            q_ref[:, sl] = _rope(qlin[:, sl], ct, sa, sb).astype(BF16)
            k_ref[:, sl] = (klin[:, sl] + kr).astype(BF16)

    row = lambda wd: pl.BlockSpec((tm, wd), lambda i: (i, 0))
    full = lambda r, c: pl.BlockSpec((r, c), lambda i: (0, 0))
    return pl.pallas_call(
        body, name="mla_prep", grid=(t // tm,),
        in_specs=[row(A_Q_RANK), row(A_KV_RANK), row(128), full(1, A_Q_RANK), full(1, A_KV_RANK),
                  full(A_Q_RANK, qw), full(A_KV_RANK, qw), full(A_KV_RANK, GW), row(128), row(128), row(128)],
        out_specs=[row(qw), row(qw), row(GW), row(A_Q_RANK), row(A_KV_RANK)],
        out_shape=[jax.ShapeDtypeStruct((t, qw), BF16), jax.ShapeDtypeStruct((t, qw), BF16),
                   jax.ShapeDtypeStruct((t, GW), BF16), jax.ShapeDtypeStruct((t, A_Q_RANK), BF16),
                   jax.ShapeDtypeStruct((t, A_KV_RANK), BF16)],
        compiler_params=_cparams(("arbitrary",)),
    )(cq, ckv, kpe, gq, gkv, wuq, wk, wv, cos_t, sin_a, sin_b)


def _mla_prep_bwd(dq, dk, dv, cq, ckv, gq, gkv, wuq_t, wk_t, wv_t, cos_t, sin_a, sin_b, tm=256):
    t = cq.shape[0]
    qw = A_HEADS * HEAD_PAD

    def body(dq_ref, dk_ref, dv_ref, cq_ref, ckv_ref, gq_ref, gkv_ref, wuqt_ref, wkt_ref, wvt_ref,
             c_ref, sa_ref, sb_ref, dcq_ref, dckv_ref, dkpe_ref, dql_ref, dkl_ref, dgq_ref, dgkv_ref):
        i = pl.program_id(0)
        ct, sa, sb = c_ref[...], sa_ref[...], sb_ref[...]
        lane = lax.broadcasted_iota(jnp.int32, (1, HEAD_PAD), 1)
        nope = lane < A_NOPE
        rope = (lane >= A_NOPE) & (lane < A_NOPE + A_ROPE)
        dksum = None
        for h in range(A_HEADS):
            sl = slice(h * HEAD_PAD, (h + 1) * HEAD_PAD)
            dql_ref[:, sl] = _unrope(dq_ref[:, sl], ct, sa, sb).astype(BF16)
            dkh = dk_ref[:, sl]
            dkl_ref[:, sl] = jnp.where(nope, dkh, 0.0).astype(BF16)
            dksum = dkh if dksum is None else dksum + dkh
        dkpe_ref[...] = jnp.where(rope, _unrope(jnp.where(rope, dksum, 0.0), ct, sa, sb), 0.0).astype(BF16)
        dcqn = jnp.dot(dql_ref[...], wuqt_ref[...], preferred_element_type=F32)
        dckvn = (jnp.dot(dkl_ref[...], wkt_ref[...], preferred_element_type=F32)
                 + jnp.dot(dv_ref[...].astype(BF16), wvt_ref[...], preferred_element_type=F32))

        def norm_bwd(xv, gv, dy):
            rstd = lax.rsqrt(jnp.mean(xv * xv, axis=-1, keepdims=True) + EPS)
            xn = xv * rstd
            dxn = dy * gv
            dx = rstd * (dxn - xn * jnp.mean(dxn * xn, axis=-1, keepdims=True))
            return dx, jnp.sum(dy * xn, axis=0, keepdims=True)

        dcq, dgq = norm_bwd(cq_ref[...], gq_ref[...], dcqn)
        dckv, dgkv = norm_bwd(ckv_ref[...], gkv_ref[...], dckvn)
        dcq_ref[...] = dcq.astype(BF16)
        dckv_ref[...] = dckv.astype(BF16)

        @pl.when(i == 0)
        def _():
            dgq_ref[...] = dgq
            dgkv_ref[...] = dgkv

        @pl.when(i != 0)
        def _():
            dgq_ref[...] += dgq
            dgkv_ref[...] += dgkv

    row = lambda wd: pl.BlockSpec((tm, wd), lambda i: (i, 0))
    full = lambda r, c: pl.BlockSpec((r, c), lambda i: (0, 0))
    return pl.pallas_call(
        body, name="mla_prep_bwd", grid=(t // tm,),
        in_specs=[row(qw), row(qw), row(GW), row(A_Q_RANK), row(A_KV_RANK), full(1, A_Q_RANK), full(1, A_KV_RANK),
                  full(qw, A_Q_RANK), full(qw, A_KV_RANK), full(GW, A_KV_RANK), row(128), row(128), row(128)],
        out_specs=[row(A_Q_RANK), row(A_KV_RANK), row(128), row(qw), row(qw), full(1, A_Q_RANK), full(1, A_KV_RANK)],
        out_shape=[jax.ShapeDtypeStruct((t, A_Q_RANK), BF16), jax.ShapeDtypeStruct((t, A_KV_RANK), BF16),
                   jax.ShapeDtypeStruct((t, 128), BF16), jax.ShapeDtypeStruct((t, qw), BF16),
                   jax.ShapeDtypeStruct((t, qw), BF16), jax.ShapeDtypeStruct((1, A_Q_RANK), F32),
                   jax.ShapeDtypeStruct((1, A_KV_RANK), F32)],
        compiler_params=_cparams(("arbitrary",)),
    )(dq, dk, dv, cq, ckv, gq, gkv, wuq_t, wk_t, wv_t, cos_t, sin_a, sin_b)


def _nt(a, b):
    return lax.dot_general(a, b, (((1,), (1,)), ((), ())), preferred_element_type=F32)


def _tn(a, b):
    return lax.dot_general(a, b, (((0,), (0,)), ((), ())), preferred_element_type=F32)


def _causal_mask(kind, q0, k0, tq, tk):
    qpos = q0 + lax.broadcasted_iota(jnp.int32, (tq, tk), 0)
    kpos = k0 + lax.broadcasted_iota(jnp.int32, (tq, tk), 1)
    if kind == "mla":
        return lax.shift_right_logical(kpos, 6) <= lax.shift_right_logical(qpos, 6)
    return kpos <= qpos


def _attn_fwd(kind, q, k, v, f, seq, scale, tq=512, tk=512):
    t = v.shape[0]
    nb = t // seq
    nq = seq // tq
    hw = 256 if kind == "mla" else 128
    use_f = f is not None
    tq, tk = min(tq, seq), min(tk, seq)
    nq = seq // tq
    assert tk == tq

    def body(*refs):
        if use_f:
            q_ref, k_ref, v_ref, f_ref, o_ref, st_ref = refs
        else:
            q_ref, k_ref, v_ref, o_ref, st_ref = refs
        qi = pl.program_id(2)
        q0 = qi * tq
        lane = lax.broadcasted_iota(jnp.int32, (1, 128), 1)
        half = lane >= 64
        qall = q_ref[...]
        if kind == "mla":
            qhs = [qall[:, 0:128], qall[:, 128:256]]
        else:
            qhs = [jnp.where(half, jnp.zeros_like(qall), qall), jnp.where(half, qall, jnp.zeros_like(qall))]
        kd = pl.multiple_of(q0, tq)
        diag = _causal_mask(kind, 0, 0, tq, tk)

        def block(j, k0, state, masked):
            m, l, acc = state
            kh = k_ref[pl.ds(k0, tk), j * 128:(j + 1) * 128] if kind == "mla" else k_ref[pl.ds(k0, tk), :]
            s = _nt(qhs[j], kh) * scale
            if use_f:
                s = s - f_ref[0, 0, j:j + 1, pl.ds(k0, tk)]
            if masked:
                s = jnp.where(diag, s, NEG)
            mn = jnp.maximum(m, jnp.max(s, axis=-1, keepdims=True))
            alpha = jnp.exp(m - mn)
            p = jnp.exp(s - mn)
            l = alpha * l + jnp.sum(p, axis=-1, keepdims=True)
            acc = alpha * acc + jnp.dot(p.astype(BF16), v_ref[pl.ds(k0, tk), :], preferred_element_type=F32)
            return mn, l, acc

        def kstep(kb, carry):
            k0 = pl.multiple_of(kb * tk, tk)
            return block(0, k0, carry[:3], False) + block(1, k0, carry[3:], False)

        init = (jnp.full((tq, 1), NEG, F32), jnp.zeros((tq, 1), F32), jnp.zeros((tq, 128), F32)) * 2
        carry = lax.fori_loop(0, qi, kstep, init)
        m0, l0, a0 = block(0, kd, carry[:3], True)
        m1, l1, a1 = block(1, kd, carry[3:], True)
        o_ref[...] = jnp.where(half, a1 / l1, a0 / l0)
        st_ref[...] = jnp.where(lane == 0, m0 + jnp.log(l0), jnp.where(lane == 1, m1 + jnp.log(l1), 0.0))

    in_specs = [pl.BlockSpec((tq, hw), lambda b, p, i: (b * nq + i, p)),
                pl.BlockSpec((seq, hw), lambda b, p, i: (b, p)),
                pl.BlockSpec((seq, 128), lambda b, p, i: (b, p))]
    args = [q, k, v]
    if use_f:
        in_specs.append(pl.BlockSpec((1, 1, 8, seq), lambda b, p, i: (b, p, 0, 0)))
        args.append(f)
    oblk = pl.BlockSpec((tq, 128), lambda b, p, i: (b * nq + i, p))
    return pl.pallas_call(
        body, name="attn_fwd_" + kind, grid=(nb, 3, nq), in_specs=in_specs, out_specs=[oblk, oblk],
        out_shape=[jax.ShapeDtypeStruct((t, GW), F32), jax.ShapeDtypeStruct((t, GW), F32)],
        compiler_params=_cparams(("arbitrary", "arbitrary", "arbitrary")),
    )(*args)


def _attn_bwd(kind, q, k, v, f, o, st, do, seq, scale, tq=512, tk=512, dep=None):
    t = v.shape[0]
    nb = t // seq
    tq, tk = min(tq, seq), min(tk, seq)
    nq = seq // tq
    nk = seq // tk
    hw = 256 if kind == "mla" else 128
    use_f = f is not None
    assert tq == tk

    def body(*refs):
        if use_f:
            q_ref, k_ref, v_ref, f_ref, o_ref, st_ref, do_ref, dq_ref, dk_ref, dv_ref, df_ref, dfq_ref = refs
        else:
            q_ref, k_ref, v_ref, o_ref, st_ref, do_ref, dq_ref, dk_ref, dv_ref = refs
        kj = pl.program_id(2)
        k0 = kj * tk
        lane = lax.broadcasted_iota(jnp.int32, (1, 128), 1)
        half = lane >= 64

        @pl.when(kj == 0)
        def _():
            dq_ref[...] = jnp.zeros_like(dq_ref)
            if use_f:
                dfq_ref[...] = jnp.zeros_like(dfq_ref)

        dk_ref[...] = jnp.zeros_like(dk_ref)
        dv_ref[...] = jnp.zeros_like(dv_ref)
        if use_f:
            df_ref[...] = jnp.zeros_like(df_ref)
        vv = v_ref[...]
        diag = _causal_mask(kind, 0, 0, tq, tk)

        def qstep(qi, masked):
            q0 = pl.multiple_of(qi * tq, tq)
            rows = pl.ds(q0, tq)
            dov = do_ref[rows, :]
            dd = dov * o_ref[rows, :]
            stv = st_ref[rows, :]
            for j in range(2):
                hm = half == bool(j)
                delta = jnp.sum(jnp.where(hm, dd, 0.0), axis=-1, keepdims=True)
                lse = stv[:, j:j + 1]
                if kind == "mla":
                    cols = slice(j * 128, (j + 1) * 128)
                    qh = q_ref[rows, cols]
                    kh = k_ref[:, cols]
                else:
                    cols = slice(0, 128)
                    qa = q_ref[rows, :]
                    qh = jnp.where(hm, qa, jnp.zeros_like(qa))
                    kh = k_ref[...]
                s = _nt(qh, kh) * scale
                if use_f:
                    s = s - f_ref[0, 0, j:j + 1, :]
                if masked:
                    s = jnp.where(diag, s, NEG)
                p = jnp.exp(s - lse)
                doh = jnp.where(hm, dov, 0.0).astype(BF16)
                ds = p * (_nt(doh, vv) - delta)
                dsb = (ds * scale).astype(BF16)
                dv_ref[...] += _tn(p.astype(BF16), doh)
                dk_ref[:, cols] += _tn(dsb, qh)
                dqc = jnp.dot(dsb, kh, preferred_element_type=F32)
                if kind != "mla":
                    dqc = jnp.where(hm, dqc, 0.0)
                dq_ref[rows, cols] += dqc
                if use_f:
                    df_ref[0, 0, j:j + 1, :] += -jnp.sum(ds, axis=0, keepdims=True)
                    dfq_ref[rows, :] += jnp.where(lane == j, jnp.sum(ds, axis=-1, keepdims=True), 0.0)

        qstep(kj, True)

        def rest(qi, carry):
            qstep(qi, False)
            return carry

        lax.fori_loop(kj + 1, nq, rest, 0)

    full_q = lambda wd: pl.BlockSpec((seq, wd), lambda b, p, i: (b, p))
    kblk = lambda wd: pl.BlockSpec((tk, wd), lambda b, p, i: (b * nk + i, p))
    in_specs = [full_q(hw), kblk(hw), kblk(128)]
    args = [q, k, v]
    if use_f:
        in_specs.append(pl.BlockSpec((1, 1, 8, tk), lambda b, p, i: (b, p, 0, i)))
        args.append(f)
    in_specs += [full_q(128), full_q(128), full_q(128)]
    args += [o, st, do]
    out_specs = [full_q(hw), kblk(hw), kblk(128)]
    out_shape = [jax.ShapeDtypeStruct((t, 3 * hw), F32), jax.ShapeDtypeStruct((t, 3 * hw), F32),
                 jax.ShapeDtypeStruct((t, GW), F32)]
    if use_f:
        out_specs += [pl.BlockSpec((1, 1, 8, tk), lambda b, p, i: (b, p, 0, i)), full_q(128)]
        out_shape += [jax.ShapeDtypeStruct((nb, 3, 8, seq), F32), jax.ShapeDtypeStruct((t, GW), F32)]
    body, in_specs, args = _after(dep, body, in_specs, args)
    return pl.pallas_call(
        body, name="attn_bwd_" + kind, grid=(nb, 3, nk), in_specs=in_specs, out_specs=out_specs,
        out_shape=out_shape, compiler_params=_cparams(("arbitrary", "arbitrary", "arbitrary")),
    )(*args)


BQ = 256
BWIN = BQ + B_LEFT


def _band_geometry():
    r = lax.broadcasted_iota(jnp.int32, (BQ, BWIN), 0)
    j = lax.broadcasted_iota(jnp.int32, (BQ, BWIN), 1)
    rc = lax.shift_right_logical(r, 6)
    jc = lax.shift_right_logical(j, 6)
    allowed = (jc - 8 <= rc) & (rc <= jc)
    return (r + B_LEFT - j) >= REL_CLIP, allowed, j < r


def _band_onehot(transposed, offset=0):
    shape = (BWIN, GW) if transposed else (GW, BWIN)
    kk = lax.broadcasted_iota(jnp.int32, shape, 1 if transposed else 0)
    x = lax.broadcasted_iota(jnp.int32, shape, 0 if transposed else 1) - offset
    x = jnp.where(x < 0, x + BWIN, x)
    return (kk == jnp.clip(B_LEFT - x, -REL_CLIP, REL_CLIP) + REL_CLIP).astype(F32)


def _band_table(rel_bias8):
    def body(b_ref, o_ref):
        hh = pl.program_id(0)
        u8 = jnp.dot(b_ref[...], _band_onehot(False), precision=HI, preferred_element_type=F32)
        rid = lax.broadcasted_iota(jnp.int32, (8, BWIN), 0)
        row = jnp.sum(jnp.where(rid == hh, u8, 0.0), axis=0, keepdims=True)
        far, allowed, _ = _band_geometry()
        tbl = pltpu.roll(jnp.broadcast_to(row, (BQ, BWIN)), 0, 1, stride=1, stride_axis=0)
        tbl = jnp.where(far, row[:, 0:1], tbl)
        o_ref[0] = jnp.where(allowed, tbl, NEG)

    return pl.pallas_call(
        body, name="band_table", grid=(6,),
        in_specs=[pl.BlockSpec((8, GW), lambda h: (0, 0))],
        out_specs=pl.BlockSpec((1, BQ, BWIN), lambda h: (h, 0, 0)),
        out_shape=jax.ShapeDtypeStruct((6, BQ, BWIN), F32),
        compiler_params=_cparams(("arbitrary",)),
    )(rel_bias8)


def _band_table_bwd(gtab):
    def body(g_ref, o_ref):
        gv = g_ref[0]
        _, _, wrapped = _band_geometry()
        gfar = jnp.sum(jnp.sum(jnp.where(wrapped, gv, 0.0), axis=-1, keepdims=True), axis=0, keepdims=True)
        anti = (lax.broadcasted_iota(jnp.int32, (BQ, BQ), 0) + lax.broadcasted_iota(jnp.int32, (BQ, BQ), 1)
                == BQ - 1).astype(F32)
        grev = jnp.dot(anti, jnp.where(wrapped, 0.0, gv), precision=HI, preferred_element_type=F32)
        near = pltpu.roll(grev, 0, 1, stride=1, stride_axis=0)
        y = jnp.broadcast_to(jnp.sum(near, axis=0, keepdims=True), (8, BWIN))
        gb = jnp.dot(y, _band_onehot(True, BQ - 1), precision=HI, preferred_element_type=F32)
        lane = lax.broadcasted_iota(jnp.int32, (8, GW), 1)
        o_ref[0] = gb + jnp.where(lane == 2 * REL_CLIP, gfar, 0.0)

    return pl.pallas_call(
        body, name="band_table_bwd", grid=(B_HEADS,),
        in_specs=[pl.BlockSpec((1, BQ, BWIN), lambda h: (h, 0, 0))],
        out_specs=pl.BlockSpec((1, 8, GW), lambda h: (h, 0, 0)),
        out_shape=jax.ShapeDtypeStruct((B_HEADS, 8, GW), F32),
        compiler_params=_cparams(("arbitrary",)),
    )(gtab)


def _band_fwd(q, k, v, table, seq, scale):
    t = q.shape[0]
    nb = t // seq
    nq = seq // BQ

    def body(q_ref, k_ref, v_ref, tb_ref, o_ref, st_ref, kpad, vpad):
        qi = pl.program_id(2)
        q0 = pl.multiple_of(qi * BQ, BQ)
        lane = lax.broadcasted_iota(jnp.int32, (1, 128), 1)
        half = lane >= 64

        @pl.when(qi == 0)
        def _():
            kpad[0:B_LEFT, :] = jnp.zeros((B_LEFT, 128), BF16)
            vpad[0:B_LEFT, :] = jnp.zeros((B_LEFT, 128), BF16)
            kpad[B_LEFT:, :] = k_ref[...]
            vpad[B_LEFT:, :] = v_ref[...]

        kw = kpad[pl.ds(q0, BWIN), :]
        vw = vpad[pl.ds(q0, BWIN), :]
        inside = lax.broadcasted_iota(jnp.int32, (BQ, BWIN), 1) >= B_LEFT - q0
        qall = q_ref[...]
        outs, lses = [], []
        for j in range(2):
            qh = jnp.where(half == bool(j), qall, jnp.zeros_like(qall))
            s = jnp.where(inside, _nt(qh, kw) * scale + tb_ref[j], NEG)
            m = jnp.max(s, axis=-1, keepdims=True)
            p = jnp.exp(s - m)
            l = jnp.sum(p, axis=-1, keepdims=True)
            outs.append(jnp.dot(p.astype(BF16), vw, preferred_element_type=F32) / l)
            lses.append(m + jnp.log(l))
        o_ref[...] = jnp.where(half, outs[1], outs[0])
        st_ref[...] = jnp.where(lane == 0, lses[0], jnp.where(lane == 1, lses[1], 0.0))

    qblk = pl.BlockSpec((BQ, 128), lambda b, p, i: (b * nq + i, p))
    full = pl.BlockSpec((seq, 128), lambda b, p, i: (b, p))
    return pl.pallas_call(
        body, name="band_fwd", grid=(nb, 3, nq),
        in_specs=[qblk, full, full, pl.BlockSpec((2, BQ, BWIN), lambda b, p, i: (p, 0, 0))],
        out_specs=[qblk, qblk],
        out_shape=[jax.ShapeDtypeStruct((t, GW), F32), jax.ShapeDtypeStruct((t, GW), F32)],
        scratch_shapes=[pltpu.VMEM((seq + B_LEFT, 128), BF16), pltpu.VMEM((seq + B_LEFT, 128), BF16)],
        compiler_params=_cparams(("arbitrary", "arbitrary", "arbitrary")),
    )(q, k, v, table)


def _band_bwd(q, k, v, table, o, st, do, seq, scale, dep=None):
    t = q.shape[0]
    nb = t // seq
    nq = seq // BQ

    def body(q_ref, k_ref, v_ref, tb_ref, o_ref, st_ref, do_ref, dq_ref, dk_ref, dv_ref, g_ref,
             kpad, vpad, dkpad, dvpad):
        b = pl.program_id(1)
        qi = pl.program_id(2)
        q0 = pl.multiple_of(qi * BQ, BQ)
        lane = lax.broadcasted_iota(jnp.int32, (1, 128), 1)
        half = lane >= 64

        @pl.when(qi == 0)
        def _():
            kpad[0:B_LEFT, :] = jnp.zeros((B_LEFT, 128), BF16)
            vpad[0:B_LEFT, :] = jnp.zeros((B_LEFT, 128), BF16)
            kpad[B_LEFT:, :] = k_ref[...]
            vpad[B_LEFT:, :] = v_ref[...]
            dkpad[...] = jnp.zeros_like(dkpad)
            dvpad[...] = jnp.zeros_like(dvpad)

        @pl.when((qi == 0) & (b == 0))
        def _():
            g_ref[...] = jnp.zeros_like(g_ref)

        win = pl.ds(q0, BWIN)
        kw = kpad[win, :]
        vw = vpad[win, :]
        inside = lax.broadcasted_iota(jnp.int32, (BQ, BWIN), 1) >= B_LEFT - q0
        qall = q_ref[...]
        dov = do_ref[...]
        dd = dov * o_ref[...]
        stv = st_ref[...]
        dq = jnp.zeros((BQ, 128), F32)
        for j in range(2):
            hm = half == bool(j)
            qh = jnp.where(hm, qall, jnp.zeros_like(qall))
            delta = jnp.sum(jnp.where(hm, dd, 0.0), axis=-1, keepdims=True)
            s = jnp.where(inside, _nt(qh, kw) * scale + tb_ref[j], NEG)
            p = jnp.exp(s - stv[:, j:j + 1])
            doh = jnp.where(hm, dov, 0.0).astype(BF16)
            ds = p * (_nt(doh, vw) - delta)
            g_ref[j] += ds
            dsb = (ds * scale).astype(BF16)
            dvpad[win, :] += _tn(p.astype(BF16), doh)
            dkpad[win, :] += _tn(dsb, qh)
            dq = dq + jnp.where(hm, jnp.dot(dsb, kw, preferred_element_type=F32), 0.0)
        dq_ref[...] = dq.astype(BF16)

        @pl.when(qi == nq - 1)
        def _():
            dk_ref[...] = dkpad[B_LEFT:, :].astype(BF16)
            dv_ref[...] = dvpad[B_LEFT:, :].astype(BF16)

    qblk = pl.BlockSpec((BQ, 128), lambda p, b, i: (b * nq + i, p))
    full = pl.BlockSpec((seq, 128), lambda p, b, i: (b, p))
    tblk = pl.BlockSpec((2, BQ, BWIN), lambda p, b, i: (p, 0, 0))
    body, in_specs, args = _after(dep, body, [qblk, full, full, tblk, qblk, qblk, qblk], [q, k, v, table, o, st, do])
    return pl.pallas_call(
        body, name="band_bwd", grid=(3, nb, nq),
        in_specs=in_specs,
        out_specs=[qblk, full, full, tblk],
        out_shape=[jax.ShapeDtypeStruct((t, GW), BF16), jax.ShapeDtypeStruct((t, GW), BF16),
                   jax.ShapeDtypeStruct((t, GW), BF16), jax.ShapeDtypeStruct((6, BQ, BWIN), F32)],
        scratch_shapes=[pltpu.VMEM((seq + B_LEFT, 128), BF16), pltpu.VMEM((seq + B_LEFT, 128), BF16),
                        pltpu.VMEM((seq + B_LEFT, 128), F32), pltpu.VMEM((seq + B_LEFT, 128), F32)],
        compiler_params=_cparams(("arbitrary", "arbitrary", "arbitrary")),
    )(*args)


def _fox_prep(cf, fb, seq):
    nb = cf.shape[0] // seq
    nblk = seq // 128

    def body(cf_ref, fb_ref, f_ref):
        x = cf_ref[...] + fb_ref[...]
        lf = jnp.minimum(x, 0.0) - jnp.log1p(jnp.exp(-jnp.abs(x)))
        rows = lf.T[0:8, :]
        upper = (lax.broadcasted_iota(jnp.int32, (128, 128), 0)
                 <= lax.broadcasted_iota(jnp.int32, (128, 128), 1)).astype(F32)
        carry = jnp.zeros((8, 1), F32)
        for blk in range(nblk):
            sl = slice(blk * 128, (blk + 1) * 128)
            cs = jnp.dot(rows[:, sl], upper, precision=HI, preferred_element_type=F32) + carry
            carry = cs[:, 127:128]
            f_ref[0, 0, :, sl] = cs
            f_ref[0, 1, :, sl] = pltpu.roll(cs, 6, 0)
            f_ref[0, 2, :, sl] = pltpu.roll(cs, 4, 0)

    return pl.pallas_call(
        body, name="fox_prep", grid=(nb,),
        in_specs=[pl.BlockSpec((seq, 128), lambda b: (b, 0)), pl.BlockSpec((1, 128), lambda b: (0, 0))],
        out_specs=pl.BlockSpec((1, 3, 8, seq), lambda b: (b, 0, 0, 0)),
        out_shape=jax.ShapeDtypeStruct((nb, 3, 8, seq), F32),
        compiler_params=_cparams(("arbitrary",)),
    )(cf, fb)


def _fox_prep_bwd(df, dfq, cf, fb, seq):
    nb = cf.shape[0] // seq
    nblk = seq // 128

    def body(df_ref, dfq_ref, cf_ref, fb_ref, dcf_ref, dfb_ref, wide):
        b = pl.program_id(0)
        row = lax.broadcasted_iota(jnp.int32, (8, seq), 0)
        dfh = None
        for p in range(3):
            both = df_ref[0, p] + dfq_ref[:, p * 128:(p + 1) * 128].T[0:8, :]
            both = jnp.where(row < 2, both, 0.0)
            if p:
                both = pltpu.roll(both, 2 * p, 0)
            dfh = both if dfh is None else dfh + both
        lower = (lax.broadcasted_iota(jnp.int32, (128, 128), 0)
                 >= lax.broadcasted_iota(jnp.int32, (128, 128), 1)).astype(F32)
        wide[...] = jnp.zeros_like(wide)
        carry = jnp.zeros((8, 1), F32)
        for blk in reversed(range(nblk)):
            sl = slice(blk * 128, (blk + 1) * 128)
            rc = jnp.dot(dfh[:, sl], lower, precision=HI, preferred_element_type=F32) + carry
            carry = rc[:, 0:1]
            wide[0:8, sl] = rc
        dl = wide[...].T
        x = cf_ref[...] + fb_ref[...]
        dcf = dl * (1.0 / (1.0 + jnp.exp(x)))
        dcf_ref[...] = dcf.astype(BF16)
        part = jnp.sum(dcf, axis=0, keepdims=True)

        @pl.when(b == 0)
        def _():
            dfb_ref[...] = part

        @pl.when(b != 0)
        def _():
            dfb_ref[...] += part

    return pl.pallas_call(
        body, name="fox_prep_bwd", grid=(nb,),
        in_specs=[pl.BlockSpec((1, 3, 8, seq), lambda b: (b, 0, 0, 0)), pl.BlockSpec((seq, GW), lambda b: (b, 0)),
                  pl.BlockSpec((seq, 128), lambda b: (b, 0)), pl.BlockSpec((1, 128), lambda b: (0, 0))],
        out_specs=[pl.BlockSpec((seq, 128), lambda b: (b, 0)), pl.BlockSpec((1, 128), lambda b: (0, 0))],
        out_shape=[jax.ShapeDtypeStruct(cf.shape, BF16), jax.ShapeDtypeStruct((1, 128), F32)],
        scratch_shapes=[pltpu.VMEM((128, seq), F32)],
        compiler_params=_cparams(("arbitrary",)),
    )(df, dfq, cf, fb)


def _gate_out(oa, ob, oc, gates, w, x, gate, seq, tm=256):
    t = x.shape[0]
    tps = seq // tm

    def body(oa_ref, ob_ref, oc_ref, g_ref, w_ref, x_ref, gt_ref, xo_ref, y_ref, u_ref):
        for n, o_ref in enumerate((oa_ref, ob_ref, oc_ref)):
            sl = slice(n * GW, (n + 1) * GW)
            gv = g_ref[:, sl]
            u_ref[:, sl] = (o_ref[...] * (gv * _sigmoid(gv))).astype(BF16)
        y = jnp.dot(u_ref[...], w_ref[...], preferred_element_type=F32)
        y_ref[...] = y
        xo_ref[...] = x_ref[...] + gt_ref[0] * y

    row = lambda wd: pl.BlockSpec((tm, wd), lambda i: (i, 0))
    return pl.pallas_call(
        body, name="gate_out", grid=(t // tm,),
        in_specs=[row(GW), row(GW), row(GW), row(U_PAD), pl.BlockSpec((U_PAD, D_MODEL), lambda i: (0, 0)),
                  row(D_MODEL), pl.BlockSpec((1, 1, D_MODEL), lambda i: (i // tps, 0, 0))],
        out_specs=[row(D_MODEL), row(D_MODEL), row(U_PAD)],
        out_shape=[jax.ShapeDtypeStruct((t, D_MODEL), F32), jax.ShapeDtypeStruct((t, D_MODEL), F32),
                   jax.ShapeDtypeStruct((t, U_PAD), BF16)],
        compiler_params=_cparams(("arbitrary",)),
    )(oa, ob, oc, gates, w, x, gate)


def _gate_out_bwd(dxo, y, gate, oa, ob, oc, gates, w_t, seq, tm=256, dep=None):
    t = dxo.shape[0]
    tps = seq // tm
    nb = t // seq

    def body(dxo_ref, y_ref, gt_ref, oa_ref, ob_ref, oc_ref, g_ref, wt_ref,
             dy_ref, doa_ref, dob_ref, doc_ref, dg_ref, dgt_ref):
        i = pl.program_id(0)
        dxo_v = dxo_ref[...]
        dgt = jnp.sum(dxo_v * y_ref[...], axis=0, keepdims=True)
        dyb = (dxo_v * gt_ref[0]).astype(BF16)
        dy_ref[...] = dyb
        du = jnp.dot(dyb, wt_ref[...], preferred_element_type=F32)
        for n, (o_ref, do_ref) in enumerate(((oa_ref, doa_ref), (ob_ref, dob_ref), (oc_ref, doc_ref))):
            sl = slice(n * GW, (n + 1) * GW)
            gv = g_ref[:, sl]
            sg = _sigmoid(gv)
            dun = du[:, sl]
            do_ref[...] = dun * (gv * sg)
            dg_ref[:, sl] = (dun * o_ref[...] * (sg * (1.0 + gv * (1.0 - sg)))).astype(BF16)

        @pl.when(i % tps == 0)
        def _():
            dgt_ref[0] = dgt

        @pl.when(i % tps != 0)
        def _():
            dgt_ref[0] += dgt

    row = lambda wd: pl.BlockSpec((tm, wd), lambda i: (i, 0))
    per_b = pl.BlockSpec((1, 1, D_MODEL), lambda i: (i // tps, 0, 0))
    in_specs = [row(D_MODEL), row(D_MODEL), per_b, row(GW), row(GW), row(GW), row(U_PAD),
                pl.BlockSpec((D_MODEL, U_PAD), lambda i: (0, 0))]
    body, in_specs, args = _after(dep, body, in_specs, [dxo, y, gate, oa, ob, oc, gates, w_t])
    return pl.pallas_call(
        body, name="gate_out_bwd", grid=(t // tm,), in_specs=in_specs,
        out_specs=[row(D_MODEL), row(GW), row(GW), row(GW), row(U_PAD), per_b],
        out_shape=[jax.ShapeDtypeStruct((t, D_MODEL), BF16), jax.ShapeDtypeStruct((t, GW), F32),
                   jax.ShapeDtypeStruct((t, GW), F32), jax.ShapeDtypeStruct((t, GW), F32),
                   jax.ShapeDtypeStruct((t, U_PAD), BF16), jax.ShapeDtypeStruct((nb, 1, D_MODEL), F32)],
        compiler_params=_cparams(("arbitrary",)),
    )(*args)


def _final_loss(x, target, g, tm=256):
    t = x.shape[0]

    def body(x_ref, t_ref, g_ref, dx_ref, loss_ref, dg_ref):
        i = pl.program_id(0)
        xv = x_ref[...]
        rstd = lax.rsqrt(jnp.mean(xv * xv, axis=-1, keepdims=True) + EPS)
        xn = xv * rstd
        gv = g_ref[...]
        err = xn * gv - t_ref[...]
        dy = err * (1.0 / D_MODEL)
        dxn = dy * gv
        dx_ref[...] = rstd * (dxn - xn * jnp.mean(dxn * xn, axis=-1, keepdims=True))
        lp = jnp.sum(err * err, axis=0, keepdims=True) * (0.5 / D_MODEL)
        dgp = jnp.sum(dy * xn, axis=0, keepdims=True)

        @pl.when(i == 0)
        def _():
            loss_ref[...] = lp
            dg_ref[...] = dgp

        @pl.when(i != 0)
        def _():
            loss_ref[...] += lp
            dg_ref[...] += dgp

    row = pl.BlockSpec((tm, D_MODEL), lambda i: (i, 0))
    vec = pl.BlockSpec((1, D_MODEL), lambda i: (0, 0))
    return pl.pallas_call(
        body, name="final_loss", grid=(t // tm,),
        in_specs=[row, row, vec], out_specs=[row, vec, vec],
        out_shape=[jax.ShapeDtypeStruct((t, D_MODEL), F32), jax.ShapeDtypeStruct((1, D_MODEL), F32),
                   jax.ShapeDtypeStruct((1, D_MODEL), F32)],
        compiler_params=_cparams(("arbitrary",)),
    )(x, target, g)


def _adamw(w, gslots, m, v, name, tr=None):
    nl, r, c = w.shape
    ns = gslots.shape[0]
    tr = r if tr is None else tr

    def body(w_ref, g_ref, m_ref, v_ref, go_ref, d_ref, mo_ref, vo_ref):
        g = g_ref[0].astype(F32)
        for j in range(1, ns):
            g = g + g_ref[j].astype(F32)
        mn = ADAM_B1 * m_ref[...] + (1.0 - ADAM_B1) * g
        vn = ADAM_B2 * v_ref[...] + (1.0 - ADAM_B2) * jnp.square(g)
        m_hat = mn / (1.0 - ADAM_B1 ** ADAM_STEP)
        v_hat = vn / (1.0 - ADAM_B2 ** ADAM_STEP)
        go_ref[...] = g
        d_ref[...] = -ADAM_LR * (m_hat / (jnp.sqrt(v_hat) + ADAM_EPS) + ADAM_WD * w_ref[...])
        mo_ref[...] = mn
        vo_ref[...] = vn

    blk = pl.BlockSpec((1, tr, c), lambda l, i: (l, i, 0))
    return pl.pallas_call(
        body, name=name, grid=(nl, r // tr),
        in_specs=[blk, pl.BlockSpec((ns, 1, tr, c), lambda l, i: (0, l, i, 0)), blk, blk],
        out_specs=[blk] * 4, out_shape=[jax.ShapeDtypeStruct((nl, r, c), F32)] * 4,
        compiler_params=_cparams(("arbitrary", "arbitrary")),
    )(w, gslots, m, v)


def _rope_tables(positions):
    inv = ROPE_THETA ** (-jnp.arange(0, A_ROPE, 2, dtype=F32) / A_ROPE)
    ang = positions.astype(F32)[:, None] * inv
    cos, sin = jnp.cos(ang), jnp.sin(ang)
    t = positions.shape[0]
    one = jnp.ones((t, 64), F32)
    zero16 = jnp.zeros((t, 16), F32)
    cos_t = jnp.concatenate([one, cos, cos, jnp.ones((t, 32), F32)], axis=1)
    sin_a = jnp.concatenate([jnp.zeros((t, 64), F32), -sin, zero16, jnp.zeros((t, 32), F32)], axis=1)
    sin_b = jnp.concatenate([jnp.zeros((t, 64), F32), zero16, sin, jnp.zeros((t, 32), F32)], axis=1)
    return cos_t, sin_a, sin_b


def _pad_heads(w, real, padded, nheads, axis):
    shp = w.shape[:axis] + (nheads, real) + w.shape[axis + 1:]
    w = w.reshape(shp)
    pad = [(0, 0)] * w.ndim
    pad[axis + 1] = (0, padded - real)
    w = jnp.pad(w, pad)
    return w.reshape(w.shape[:axis] + (nheads * padded,) + w.shape[axis + 2:])


def kernel(x, c, positions, w_ada, b_ada, norm_g, w_in, a_q_norm_g, a_w_uq, a_kv_norm_g, a_w_ukv, b_rel_bias, c_forget_b, w_out, final_g, loss_target, m_w_ada, m_b_ada, m_norm_g, m_w_in, m_a_q_norm_g, m_a_w_uq, m_a_kv_norm_g, m_a_w_ukv, m_b_rel_bias, m_c_forget_b, m_w_out, m_final_g, v_w_ada, v_b_ada, v_norm_g, v_w_in, v_a_q_norm_g, v_a_w_uq, v_a_kv_norm_g, v_a_w_ukv, v_b_rel_bias, v_c_forget_b, v_w_out, v_final_g):
    nb, seq, _ = x.shape
    t = nb * seq
    me = 4 * lax.axis_index("x") + 2 * lax.axis_index("y") + lax.axis_index("c")
    x2 = x.reshape(t, D_MODEL)
    tgt = loss_target.reshape(t, D_MODEL)
    cos_t, sin_a, sin_b = _rope_tables(positions.reshape(t))

    def shards(l):
        return [_pad_runs(w_in[l].astype(BF16), IN_RUNS, N_PAD, 1), w_out[l].astype(BF16),
                a_w_uq[l].astype(BF16), a_w_ukv[l].astype(BF16)]

    def prepare(gi, go, gq, gkv):
        wi = gi.reshape(D_MODEL, N_PAD)
        wo = _pad_runs(go.reshape(D_MODEL, D_MODEL), OUT_RUNS, U_PAD, 0)
        wq = jnp.transpose(gq, (1, 0, 2)).reshape(A_Q_RANK, A_HEADS * (A_NOPE + A_ROPE))
        wq = _pad_heads(wq, A_NOPE + A_ROPE, HEAD_PAD, A_HEADS, 1)
        wkv = jnp.transpose(gkv, (1, 0, 2)).reshape(A_KV_RANK, A_HEADS, 2 * A_NOPE)
        wk = jnp.pad(wkv[:, :, :A_NOPE], ((0, 0), (0, 0), (0, HEAD_PAD - A_NOPE))).reshape(A_KV_RANK, A_HEADS * HEAD_PAD)
        wv = wkv[:, :, A_NOPE:].reshape(A_KV_RANK, GW)
        return dict(w_in=wi, w_in_t=wi.T, w_out=wo, w_out_t=wo.T, wuq=wq, wuq_t=wq.T, wk=wk, wk_t=wk.T,
                    wv=wv, wv_t=wv.T)

    gathered = _gather(shards(0) + [c], "gather_weights0")
    c_all = gathered[-1].reshape(N_DEV * nb, D_MODEL)
    weights = [prepare(*gathered[:4]), None]

    c_act, mod_cols = _ada_fwd(c_all, w_ada)
    (mod_g,) = _gather([mod_cols], "gather_mod")
    gather1, gather1_token = _split_start("gather", shards(1), "gather_weights1_start", after=mod_g)
    mod_all = jnp.transpose(mod_g, (1, 2, 0, 3)).reshape(DEPTH, N_DEV * nb, 3 * D_MODEL)
    mod = lax.dynamic_slice_in_dim(mod_all, me * nb, nb, axis=1) + b_ada[:, None, :]

    fb_pad = jnp.pad(c_forget_b, ((0, 0), (0, 128 - C_HEADS)))
    a_scale = (A_NOPE + A_ROPE) ** -0.5
    h_scale = CHUNK ** -0.5

    saved = []
    xl = x2
    for l in range(DEPTH):
        if l == 1:
            weights[1] = prepare(*_split_wait(gather1, xl, "gather_weights1_wait")[1])
        w = weights[l]
        shift, scale, gate = mod[l, :, :D_MODEL], mod[l, :, D_MODEL:2 * D_MODEL], mod[l, :, 2 * D_MODEL:]
        ss = jnp.stack([shift, 1.0 + scale], axis=1)
        gate3 = gate[:, None, :]
        h, cq, ckv, kpe, gates, bq, bk, bv, cq2, ck, cv, cf = _ln_in(
            xl, ss, norm_g[l:l + 1], w["w_in"], seq, dep=gather1_token if l == 0 else None)
        q, k, v, cqn, ckvn = _mla_prep(cq, ckv, kpe, a_q_norm_g[l:l + 1], a_kv_norm_g[l:l + 1],
                                       w["wuq"], w["wk"], w["wv"], cos_t, sin_a, sin_b)
        oa, sta = _attn_fwd("mla", q, k, v, None, seq, a_scale)
        table = _band_table(jnp.pad(b_rel_bias[l], ((0, 8 - B_HEADS), (0, GW - N_REL))))
        ob, stb = _band_fwd(bq, bk, bv, table, seq, h_scale)
        fcum = _fox_prep(cf, fb_pad[l:l + 1], seq)
        oc, stc = _attn_fwd("fox", cq2, ck, cv, fcum, seq, h_scale)
        xn, y, u = _gate_out(oa, ob, oc, gates, w["w_out"], xl, gate3, seq)
        saved.append(dict(x=xl, ss=ss, gate3=gate3, h=h, cq=cq, ckv=ckv, gates=gates, bq=bq, bk=bk, bv=bv,
                          cq2=cq2, ck=ck, cv=cv, cf=cf, q=q, k=k, v=v, cqn=cqn, ckvn=ckvn, oa=oa, sta=sta,
                          table=table, ob=ob, stb=stb, fcum=fcum, oc=oc, stc=stc, y=y, u=u))
        xl = xn

    dx, loss_lanes, g_final = _final_loss(xl, tgt, final_g[None, :])
    loss = lax.psum(jnp.sum(loss_lanes), AXES)

    rows = D_MODEL // N_DEV
    core = lax.axis_index("c").astype(jnp.int32).reshape(1)
    n_seg_a = 4
    dmods, smalls, parts = [None] * DEPTH, [None] * DEPTH, [None] * DEPTH
    pair1 = chips1 = pair1_token = chips1_token = None
    for l in reversed(range(DEPTH)):
        s, w = saved[l], weights[l]
        dy, doa, dob, doc, dgates, dgate = _gate_out_bwd(dx, s["y"], s["gate3"], s["oa"], s["ob"], s["oc"],
                                                         s["gates"], w["w_out_t"], seq, dep=pair1_token)
        g_out = _unpad_runs(_matmul_tn(s["u"], dy, "dw_out"), OUT_RUNS, 0)
        if l == 0:
            own, from_sib = _split_wait(pair1, g_out, "grads1_pair_wait")
            chips1, chips1_token = _split_start("chips", _pair_add(core, own, from_sib, "grads1_add"), "grads1_chips_start")
        dq, dk, dv = _attn_bwd("mla", s["q"], s["k"], s["v"], None, s["oa"], s["sta"], doa, seq, a_scale,
                               dep=chips1_token)
        dbq, dbk, dbv, gtab = _band_bwd(s["bq"], s["bk"], s["bv"], s["table"], s["ob"], s["stb"], dob, seq, h_scale,
                                        dep=chips1_token)
        g_rel = _band_table_bwd(gtab)[:, 0, :N_REL]
        dcq2, dck, dcv, dfc, dfq = _attn_bwd("fox", s["cq2"], s["ck"], s["cv"], s["fcum"], s["oc"], s["stc"], doc,
                                             seq, h_scale, dep=chips1_token)
        dcf, dfb = _fox_prep_bwd(dfc, dfq, s["cf"], fb_pad[l:l + 1], seq)
        dcq, dckv, dkpe, dqlin, dklin, dgq, dgkv = _mla_prep_bwd(
            dq, dk, dv, s["cq"], s["ckv"], a_q_norm_g[l:l + 1], a_kv_norm_g[l:l + 1],
            w["wuq_t"], w["wk_t"], w["wv_t"], cos_t, sin_a, sin_b)
        gq_pad = _matmul_tn(s["cqn"], dqlin, "dw_uq")
        g_uq = gq_pad.reshape(A_Q_RANK, A_HEADS, HEAD_PAD)[:, :, :A_NOPE + A_ROPE].reshape(A_Q_RANK, -1)
        gkv_pad = _matmul_tn(s["ckvn"], [dklin, dv], "dw_ukv")
        gk_pad = gkv_pad[:, :A_HEADS * HEAD_PAD].reshape(A_KV_RANK, A_HEADS, HEAD_PAD)[:, :, :A_NOPE]
        gv_pad = gkv_pad[:, A_HEADS * HEAD_PAD:].reshape(A_KV_RANK, A_HEADS, A_NOPE)
        g_ukv = jnp.concatenate([gk_pad, gv_pad], axis=2).reshape(A_KV_RANK, -1)
        dz = [dcq, dckv, dkpe, dgates, dbq, dbk, dbv, dcq2, dck, dcv, dcf]
        g_in_a = _matmul_tn(s["h"], dz[:n_seg_a], "dw_in_a")
        first = [g_in_a.reshape(N_DEV, rows, -1), g_out.reshape(N_DEV, rows, D_MODEL),
                 g_uq.reshape(A_Q_RANK, N_DEV, -1).transpose(1, 0, 2), g_ukv.reshape(A_KV_RANK, N_DEV, -1).transpose(1, 0, 2)]
        if l == 1:
            g_in_b = _matmul_tn(s["h"], dz[n_seg_a:], "dw_in_b")
            pair1, pair1_token = _split_start("pair", first + [g_in_b.reshape(N_DEV, rows, -1)], "grads1_pair_start")
            tail_token = None
        else:
            pair0a, pair0a_token = _split_start("pair", first, "grads0a_pair_start")
            g_in_b = _matmul_tn(s["h"], dz[n_seg_a:], "dw_in_b", dep=pair0a_token)
            own, from_sib = _split_wait(pair0a, g_in_b, "grads0a_pair_wait")
            sums0a = _pair_add(core, own, from_sib, "grads0a_add")
            pair0b, pair0b_token = _split_start("pair", [g_in_b.reshape(N_DEV, rows, -1)], "grads0b_pair_start",
                                                after=sums0a[0])
            chips0a, tail_token = _split_start("chips", sums0a, "grads0a_chips_start", after=pair0b_token)
        dx, dss, dg_norm = _ln_in_bwd(dz, w["w_in_t"], s["x"], s["ss"], norm_g[l:l + 1], dx, seq, dep=tail_token)
        dmods[l] = jnp.concatenate([dss[:, 0, :], dss[:, 1, :], dgate[:, 0, :]], axis=1)
        smalls[l] = [dg_norm.reshape(-1), dgq.reshape(-1), dgkv.reshape(-1), g_rel.reshape(-1),
                     dfb[0, :C_HEADS]]
    grad_x = dx.reshape(nb, seq, D_MODEL)
    parts[1] = _split_wait(chips1, dx, "grads1_chips_wait")[1]
    parts0a = _split_wait(chips0a, dx, "grads0a_chips_wait")[1]
    own, from_sib = _split_wait(pair0b, dx, "grads0b_pair_wait")

    small = jnp.concatenate([p for l in range(DEPTH) for p in smalls[l]] + [g_final.reshape(-1)])
    n_small = small.shape[0]
    small_rows = -(-n_small // 1024) * 8
    small = jnp.pad(small, (0, small_rows * 128 - n_small)).reshape(small_rows, 128)
    dmod_local = jnp.stack(dmods)
    dmod_g, small_g = _gather([dmod_local, small], "gather_small", dep=parts0a[0])
    chips0, chips0_token = _split_start("chips", _pair_add(core, own, from_sib, "grads0b_add"), "grads0b_chips_start",
                                        after=small_g)
    dmod_all = jnp.transpose(dmod_g, (1, 0, 2, 3)).reshape(DEPTH, N_DEV * nb, 3 * D_MODEL)
    cols = 3 * D_MODEL // N_DEV
    dmod_mine = lax.dynamic_slice_in_dim(dmod_all, me * cols, cols, axis=2)
    g_w_ada, g_b_ada = _ada_bwd(c_act, dmod_all, dmod_mine, chips0_token)
    small_sum = _sum_slots(small_g, "sum_small").reshape(-1)

    def split_small():
        out, pos = [], 0
        sizes = [D_MODEL, A_Q_RANK, A_KV_RANK, B_HEADS * N_REL, C_HEADS]
        per_layer = []
        for l in range(DEPTH):
            parts = []
            for sz in sizes:
                parts.append(small_sum[pos:pos + sz])
                pos += sz
            per_layer.append(parts)
        for j in range(len(sizes)):
            out.append(jnp.stack([per_layer[l][j] for l in range(DEPTH)]))
        out.append(small_sum[pos:pos + D_MODEL])
        return out

    g_norm, g_qn, g_kvn, g_relb, g_fb, g_fin = split_small()

    def adam(w, g, m, v, name, tr=None):
        shp = w.shape
        w3 = w.reshape((1,) * (3 - w.ndim) + shp)
        outs = _adamw(w3, g.reshape((-1,) + w3.shape), m.reshape(w3.shape), v.reshape(w3.shape), name, tr)
        return [o.reshape(shp) for o in outs]

    res = {
        "w_ada": adam(w_ada, g_w_ada, m_w_ada, v_w_ada, "adam_w_ada", 256),
        "b_ada": adam(b_ada, g_b_ada, m_b_ada, v_b_ada, "adam_b_ada"),
        "norm_g": adam(norm_g, g_norm, m_norm_g, v_norm_g, "adam_norm_g"),
        "a_q_norm_g": adam(a_q_norm_g, g_qn, m_a_q_norm_g, v_a_q_norm_g, "adam_q_norm"),
        "a_kv_norm_g": adam(a_kv_norm_g, g_kvn, m_a_kv_norm_g, v_a_kv_norm_g, "adam_kv_norm"),
        "b_rel_bias": adam(b_rel_bias, g_relb.reshape(b_rel_bias.shape), m_b_rel_bias, v_b_rel_bias, "adam_rel_bias"),
        "c_forget_b": adam(c_forget_b, g_fb, m_c_forget_b, v_c_forget_b, "adam_forget_b"),
        "final_g": adam(final_g, g_fin, m_final_g, v_final_g, "adam_final_g"),
    }
    parts[0] = list(parts0a) + list(_split_wait(chips0, res["w_ada"][1], "grads0b_chips_wait")[1])
    p_in = jnp.stack([_unpad_runs(jnp.concatenate([parts[l][0], parts[l][4]], axis=2), IN_RUNS, 2)
                      for l in range(DEPTH)], axis=1)
    p_out, p_uq, p_ukv = (jnp.stack([parts[l][j] for l in range(DEPTH)], axis=1) for j in (1, 2, 3))
    res.update({
        "w_in": adam(w_in, p_in, m_w_in, v_w_in, "adam_w_in", 32),
        "a_w_uq": adam(a_w_uq, p_uq, m_a_w_uq, v_a_w_uq, "adam_w_uq"),
        "a_w_ukv": adam(a_w_ukv, p_ukv, m_a_w_ukv, v_a_w_ukv, "adam_w_ukv"),
        "w_out": adam(w_out, p_out, m_w_out, v_w_out, "adam_w_out", 64),
    })
    names = ["w_ada", "b_ada", "norm_g", "w_in", "a_q_norm_g", "a_w_uq", "a_kv_norm_g", "a_w_ukv", "b_rel_bias",
             "c_forget_b", "w_out", "final_g"]
    outs = [loss, grad_x]
    for j in range(4):
        outs += [res[n][j] for n in names]
    return tuple(outs)
```

```python
import functools

import jax
import jax.numpy as jnp
from jax import lax
from jax.experimental import pallas as pl
from jax.experimental.pallas import tpu as pltpu

F32 = jnp.float32
BF16 = jnp.bfloat16
HI = lax.Precision.HIGHEST

N_DEV = 8
AXES = ("x", "y", "c")
D_MODEL = 1024
DEPTH = 2
CHUNK = 64
EPS = 1e-6
NEG = -1e30
A_HEADS = 6
A_NOPE = 64
A_ROPE = 32
A_Q_RANK = 384
A_KV_RANK = 256
ROPE_THETA = 10000.0
B_HEADS = 5
B_LEFT = 512
REL_CLIP = 128
N_REL = 2 * REL_CLIP + 1
C_HEADS = 5
HEAD_PAD = 128
GW = 384
N_IN = 3621
ADAM_LR = 0.001
ADAM_B1 = 0.9
ADAM_B2 = 0.999
ADAM_EPS = 1e-08
ADAM_WD = 0.01
ADAM_STEP = 10
VMEM_LIMIT = 56 * 1024 * 1024

Z_SEGS = (
    ("cq", 0, 384, F32), ("ckv", 384, 256, F32), ("kpe", 640, 128, F32), ("gates", 768, 1152, F32),
    ("bq", 1920, 384, BF16), ("bk", 2304, 384, BF16), ("bv", 2688, 384, BF16),
    ("cq2", 3072, 384, BF16), ("ck", 3456, 384, BF16), ("cv", 3840, 384, BF16), ("cf", 4224, 128, F32),
)
N_PAD = 4352
IN_RUNS = (
    (0, 384, 0), (384, 256, 384), (640 + 64, 32, 640),
    (768, 384, 672), (768 + 384, 320, 2016), (768 + 768, 320, 3301),
    (1920, 320, 1056), (2304, 320, 1376), (2688, 320, 1696),
    (3072, 320, 2336), (3456, 320, 2656), (3840, 320, 2976), (4224, 5, 3296),
)
OUT_RUNS = ((0, 384, 0), (384, 320, 384), (768, 320, 704))
U_PAD = 1152


def _cparams(sem=None, vmem=VMEM_LIMIT):
    return pltpu.CompilerParams(dimension_semantics=sem, vmem_limit_bytes=vmem)


def _after(dep, body, in_specs, args):
    if dep is None:
        return body, in_specs, args
    n = len(args)

    def ordered(*refs):
        return body(*refs[:n], *refs[n + 1:])

    return ordered, list(in_specs) + [pl.BlockSpec((8, 128), lambda *_: (0, 0))], list(args) + [dep]


def _pad_runs(w, runs, total, axis):
    order = sorted(runs)
    parts, pos = [], 0
    for off, wd, src in order:
        if off > pos:
            shp = list(w.shape)
            shp[axis] = off - pos
            parts.append(jnp.zeros(shp, w.dtype))
        parts.append(lax.slice_in_dim(w, src, src + wd, axis=axis))
        pos = off + wd
    if pos < total:
        shp = list(w.shape)
        shp[axis] = total - pos
        parts.append(jnp.zeros(shp, w.dtype))
    return jnp.concatenate(parts, axis=axis)


def _unpad_runs(w, runs, axis):
    order = sorted(runs, key=lambda r: r[2])
    return jnp.concatenate([lax.slice_in_dim(w, off, off + wd, axis=axis) for off, wd, _ in order], axis=axis)


def _sigmoid(x):
    return 1.0 / (1.0 + jnp.exp(-x))


N_CHIP = 4
ANY_SPEC = pl.BlockSpec(memory_space=pl.ANY)
MESH_ID = pl.DeviceIdType.MESH


def _gather(arrs, name, dep=None):
    n = len(arrs)
    nin = n + (dep is not None)

    def body(*refs):
        ins, outs = refs[:n], refs[nin:nin + n]
        send_sems, recv_sems, local_sems = refs[nin + n:]
        x, y, c = lax.axis_index("x"), lax.axis_index("y"), lax.axis_index("c")
        me, sib = (x, y, c), (x, y, 1 - c)
        chips = [(1 - x, y), (x, 1 - y), (1 - x, 1 - y)]

        def slot(px, py, pc):
            return 4 * px + 2 * py + pc

        def copy(a, k, block, to, src=None):
            dst = outs[a].at[slot(*block)]
            return pltpu.make_async_remote_copy(
                src_ref=dst if src is None else src, dst_ref=dst, send_sem=send_sems.at[a, k],
                recv_sem=recv_sems.at[a, k], device_id=to, device_id_type=MESH_ID)

        local = [pltpu.make_async_copy(ins[a], outs[a].at[slot(*me)], local_sems.at[a]) for a in range(n)]
        first = []
        for a in range(n):
            first.append(copy(a, 0, me, sib, src=ins[a]))
            first += [copy(a, 1 + j, me, (*chip, c), src=ins[a]) for j, chip in enumerate(chips)]
        for cp in local + first:
            cp.start()
        passed = []
        for j, chip in enumerate(chips):
            for a in range(n):
                copy(a, 1 + j, (*chip, c), me).wait_recv()
                fwd = copy(a, 4 + j, (*chip, c), sib)
                fwd.start()
                passed.append(fwd)
        for a in range(n):
            copy(a, 0, sib, me).wait_recv()
            for j, chip in enumerate(chips):
                copy(a, 4 + j, (*chip, 1 - c), me).wait_recv()
        for cp in first + passed:
            cp.wait_send()
        for cp in local:
            cp.wait()

    return pl.pallas_call(
        body, name=name, out_shape=[jax.ShapeDtypeStruct((N_DEV,) + a.shape, a.dtype) for a in arrs],
        in_specs=[ANY_SPEC] * nin, out_specs=[ANY_SPEC] * n,
        scratch_shapes=[pltpu.SemaphoreType.DMA((n, N_DEV - 1)), pltpu.SemaphoreType.DMA((n, N_DEV - 1)),
                        pltpu.SemaphoreType.DMA((n,))],
    )(*arrs, *([] if dep is None else [dep]))


HBM_SPEC = pl.BlockSpec(memory_space=pltpu.HBM)
SEM_SPEC = pl.BlockSpec(memory_space=pltpu.SEMAPHORE)
SPLIT_EFFECT = pltpu.SideEffectType.DATAFLOW_SIDE_EFFECTING
SPLIT_SEMS = {"gather": (N_DEV - 1, True), "pair": (N_CHIP, False), "chips": (N_CHIP - 1, True)}


def _split_descriptors(pattern, srcs, lands, sems):
    x, y, c = lax.axis_index("x"), lax.axis_index("y"), lax.axis_index("c")
    nsem, has_local = SPLIT_SEMS[pattern]
    per = 2 * nsem + int(has_local)
    starts, arrivals, local = [], [], []

    def remote(a, k, src, dst, to):
        return pltpu.make_async_remote_copy(src_ref=src, dst_ref=dst, send_sem=sems[a * per + k],
                                            recv_sem=sems[a * per + nsem + k], device_id=to, device_id_type=MESH_ID)

    for a in range(len(srcs)):
        if pattern == "gather":
            me = 4 * x + 2 * y + c
            local.append(pltpu.make_async_copy(srcs[a], lands[a].at[me], sems[a * per + 2 * nsem]))
            for k in range(1, N_DEV):
                px = (1 - x) if (k >> 2) & 1 else x
                py = (1 - y) if (k >> 1) & 1 else y
                pc = (1 - c) if k & 1 else c
                starts.append(remote(a, k - 1, srcs[a], lands[a].at[me], (px, py, pc)))
                arrivals.append(remote(a, k - 1, srcs[a], lands[a].at[4 * px + 2 * py + pc], (px, py, pc)))
        elif pattern == "pair":
            for q in range(N_CHIP):
                cp = remote(a, q, srcs[a].at[2 * q + 1 - c], lands[a].at[q], (x, y, 1 - c))
                starts.append(cp)
                arrivals.append(cp)
        else:
            mine = 2 * x + y
            local.append(pltpu.make_async_copy(srcs[a].at[mine], lands[a].at[mine], sems[a * per + 2 * nsem]))
            for k in range(1, N_CHIP):
                px = (1 - x) if (k >> 1) & 1 else x
                py = (1 - y) if k & 1 else y
                starts.append(remote(a, k - 1, srcs[a].at[2 * px + py], lands[a].at[mine], (px, py, c)))
                arrivals.append(remote(a, k - 1, srcs[a].at[2 * px + py], lands[a].at[2 * px + py], (px, py, c)))
    return starts, arrivals, local


def _split_start(pattern, arrs, name, after=None):
    n = len(arrs)
    extra = [] if after is None else [after]
    nsem, has_local = SPLIT_SEMS[pattern]
    if pattern == "gather":
        land_shapes = [(N_DEV,) + a.shape for a in arrs]
    elif pattern == "pair":
        land_shapes = [(N_CHIP,) + a.shape[1:] for a in arrs]
    else:
        land_shapes = [a.shape for a in arrs]
    nsem_out = n * (2 * nsem + int(has_local))

    def body(*refs):
        srcs, lands = refs[:n], refs[n:2 * n]
        first_sem = 2 * n + len(extra)
        sems = refs[first_sem:first_sem + nsem_out]
        token = refs[-1]
        starts, _, local = _split_descriptors(pattern, srcs, lands, sems)
        for cp in local + starts:
            cp.start()
        token[...] = jnp.zeros_like(token)

    out_shape = ([pltpu.SemaphoreType.DMA(())] * nsem_out + [pltpu.HBM(a.shape, a.dtype) for a in arrs]
                 + [pltpu.HBM(s, a.dtype) for s, a in zip(land_shapes, arrs)] + [jax.ShapeDtypeStruct((8, 128), F32)])
    ins = ([pltpu.with_memory_space_constraint(a, pltpu.HBM) for a in arrs]
           + [pltpu.with_memory_space_constraint(lax.empty(s, a.dtype), pltpu.HBM) for s, a in zip(land_shapes, arrs)])
    outs = pl.pallas_call(
        body, name=name, out_shape=out_shape, in_specs=[HBM_SPEC] * (2 * n) + [ANY_SPEC] * len(extra),
        out_specs=[SEM_SPEC] * nsem_out + [HBM_SPEC] * (2 * n) + [pl.BlockSpec(memory_space=pltpu.VMEM)],
        input_output_aliases={i: nsem_out + i for i in range(2 * n)},
        compiler_params=pltpu.CompilerParams(has_side_effects=SPLIT_EFFECT),
    )(*ins, *extra)
    handle = dict(pattern=pattern, n=n, sems=outs[:nsem_out], srcs=outs[nsem_out:nsem_out + n],
                  lands=outs[nsem_out + n:nsem_out + 2 * n])
    return handle, outs[-1]


def _split_wait(handle, after, name):
    pattern, n = handle["pattern"], handle["n"]
    nsem_in = len(handle["sems"])

    def body(*refs):
        srcs, lands = refs[:n], refs[n:2 * n]
        starts, arrivals, local = _split_descriptors(pattern, srcs, lands, refs[2 * n:2 * n + nsem_in])
        for cp in starts:
            cp.wait_send()
        for cp in arrivals:
            cp.wait_recv()
        for cp in local:
            cp.wait()

    srcs, lands = handle["srcs"], handle["lands"]
    outs = pl.pallas_call(
        body, name=name,
        out_shape=[pltpu.HBM(a.shape, a.dtype) for a in srcs] + [pltpu.HBM(a.shape, a.dtype) for a in lands],
        in_specs=[HBM_SPEC] * (2 * n) + [SEM_SPEC] * nsem_in + [ANY_SPEC], out_specs=[HBM_SPEC] * (2 * n),
        input_output_aliases={i: i for i in range(2 * n)},
        compiler_params=pltpu.CompilerParams(has_side_effects=SPLIT_EFFECT),
    )(*srcs, *lands, *handle["sems"], after)
    return outs[:n], outs[n:]


def _pair_add(core, a8s, b4s, name):
    n = len(a8s)

    def body(core_ref, *refs):
        for i in range(n):
            refs[2 * n + i][...] = (refs[i][...] + refs[n + i][...]).astype(BF16)

    own = [pl.BlockSpec((1,) + b.shape[1:], lambda q, core_ref: (2 * q + core_ref[0], 0, 0)) for b in b4s]
    slot = [pl.BlockSpec((1,) + b.shape[1:], lambda q, core_ref: (q, 0, 0)) for b in b4s]
    grid_spec = pltpu.PrefetchScalarGridSpec(num_scalar_prefetch=1, grid=(N_CHIP,), in_specs=own + slot, out_specs=slot)
    return pl.pallas_call(
        body, name=name, grid_spec=grid_spec, out_shape=[jax.ShapeDtypeStruct(b.shape, BF16) for b in b4s],
        compiler_params=_cparams(("arbitrary",)),
    )(core, *a8s, *b4s)


def _sum_slots(x, name):
    _, r, c = x.shape

    def body(x_ref, o_ref):
        acc = x_ref[0]
        for j in range(1, N_DEV):
            acc = acc + x_ref[j]
        o_ref[...] = acc

    return pl.pallas_call(body, name=name, out_shape=jax.ShapeDtypeStruct((r, c), F32))(x)


def _ada_fwd(c_all, w_ada):
    nb = c_all.shape[0]
    cols = w_ada.shape[2]

    def body(c_ref, w_ref, act_ref, mod_ref):
        cv = c_ref[...]
        act = cv * _sigmoid(cv)
        act_ref[...] = act
        for l in range(DEPTH):
            mod_ref[l] = jnp.dot(act, w_ref[l], precision=HI, preferred_element_type=F32)

    return pl.pallas_call(
        body, name="ada_fwd",
        out_shape=[jax.ShapeDtypeStruct((nb, D_MODEL), F32), jax.ShapeDtypeStruct((DEPTH, nb, cols), F32)],
        compiler_params=_cparams(),
    )(c_all, w_ada)


def _ada_bwd(c_act, dmod_all, dmod_mine, dep):
    nb = c_act.shape[0]
    cols = dmod_mine.shape[2]

    def body(act_ref, dall_ref, dmine_ref, dep_ref, gw_ref, gb_ref):
        act = act_ref[...]
        for l in range(DEPTH):
            gw_ref[l] = lax.dot_general(act, dmine_ref[l], (((0,), (0,)), ((), ())),
                                        precision=HI, preferred_element_type=F32)
            gb_ref[l:l + 1, :] = jnp.sum(dall_ref[l], axis=0, keepdims=True)

    return pl.pallas_call(
        body, name="ada_bwd",
        out_shape=[jax.ShapeDtypeStruct((DEPTH, D_MODEL, cols), F32),
                   jax.ShapeDtypeStruct((DEPTH, 3 * D_MODEL), F32)],
        compiler_params=_cparams(),
    )(c_act, dmod_all, dmod_mine, dep)


def _ln_in(x, ss, g, w, seq, tm=256, dep=None):
    t = x.shape[0]
    tps = seq // tm

    def body(x_ref, ss_ref, g_ref, w_ref, h_ref, *outs):
        xv = x_ref[...]
        xn = xv * lax.rsqrt(jnp.mean(xv * xv, axis=-1, keepdims=True) + EPS)
        h = xn * g_ref[...] * ss_ref[0, 1:2, :] + ss_ref[0, 0:1, :]
        hb = h.astype(BF16)
        h_ref[...] = hb
        z = jnp.dot(hb, w_ref[...], preferred_element_type=F32)
        for o_ref, (_, off, wd, _) in zip(outs, Z_SEGS):
            o_ref[...] = z[:, off:off + wd].astype(o_ref.dtype)

    row = lambda wd: pl.BlockSpec((tm, wd), lambda i: (i, 0))
    in_specs = [row(D_MODEL), pl.BlockSpec((1, 2, D_MODEL), lambda i: (i // tps, 0, 0)),
                pl.BlockSpec((1, D_MODEL), lambda i: (0, 0)), pl.BlockSpec((D_MODEL, N_PAD), lambda i: (0, 0))]
    body, in_specs, args = _after(dep, body, in_specs, [x, ss, g, w])
    return pl.pallas_call(
        body, name="ln_in", grid=(t // tm,), in_specs=in_specs,
        out_specs=[row(D_MODEL)] + [row(wd) for _, _, wd, _ in Z_SEGS],
        out_shape=[jax.ShapeDtypeStruct((t, D_MODEL), BF16)]
        + [jax.ShapeDtypeStruct((t, wd), dt) for _, _, wd, dt in Z_SEGS],
        compiler_params=_cparams(("arbitrary",)),
    )(*args)


def _ln_in_bwd(dz, w_t, x, ss, g, dxo, seq, tm=256, dep=None):
    t = x.shape[0]
    tps = seq // tm
    nb = t // seq
    nz = len(Z_SEGS)

    def body(*refs):
        dz_refs = refs[:nz]
        wt_ref, x_ref, ss_ref, g_ref, dxo_ref, dx_ref, dss_ref, dg_ref = refs[nz:]
        i = pl.program_id(0)
        dzc = jnp.concatenate([r[...].astype(BF16) for r in dz_refs], axis=1)
        dh = jnp.dot(dzc, wt_ref[...], preferred_element_type=F32)
        xv = x_ref[...]
        rstd = lax.rsqrt(jnp.mean(xv * xv, axis=-1, keepdims=True) + EPS)
        xn = xv * rstd
        gv = g_ref[...]
        s1 = ss_ref[0, 1:2, :]
        dxg = dh * s1
        dxn = dxg * gv
        dx = rstd * (dxn - xn * jnp.mean(dxn * xn, axis=-1, keepdims=True))
        dx_ref[...] = dxo_ref[...] + dx
        dshift = jnp.sum(dh, axis=0, keepdims=True)
        dscale = jnp.sum(dh * (xn * gv), axis=0, keepdims=True)
        dgp = jnp.sum(dxg * xn, axis=0, keepdims=True)

        @pl.when(i % tps == 0)
        def _():
            dss_ref[0, 0:1, :] = dshift
            dss_ref[0, 1:2, :] = dscale

        @pl.when(i % tps != 0)
        def _():
            dss_ref[0, 0:1, :] += dshift
            dss_ref[0, 1:2, :] += dscale

        @pl.when(i == 0)
        def _():
            dg_ref[...] = dgp

        @pl.when(i != 0)
        def _():
            dg_ref[...] += dgp

    row = lambda wd: pl.BlockSpec((tm, wd), lambda i: (i, 0))
    in_specs = ([row(wd) for _, _, wd, _ in Z_SEGS]
                + [pl.BlockSpec((N_PAD, D_MODEL), lambda i: (0, 0)), row(D_MODEL),
                   pl.BlockSpec((1, 2, D_MODEL), lambda i: (i // tps, 0, 0)),
                   pl.BlockSpec((1, D_MODEL), lambda i: (0, 0)), row(D_MODEL)])
    body, in_specs, args = _after(dep, body, in_specs, [*dz, w_t, x, ss, g, dxo])
    return pl.pallas_call(
        body, name="ln_in_bwd", grid=(t // tm,), in_specs=in_specs,
        out_specs=[row(D_MODEL), pl.BlockSpec((1, 2, D_MODEL), lambda i: (i // tps, 0, 0)),
                   pl.BlockSpec((1, D_MODEL), lambda i: (0, 0))],
        out_shape=[jax.ShapeDtypeStruct((t, D_MODEL), F32), jax.ShapeDtypeStruct((nb, 2, D_MODEL), F32),
                   jax.ShapeDtypeStruct((1, D_MODEL), F32)],
        compiler_params=_cparams(("arbitrary",)),
    )(*args)


def _matmul_tn(a, bs, name, tm=1024, dep=None):
    bs = list(bs) if isinstance(bs, (list, tuple)) else [bs]
    t, k = a.shape
    widths = [b.shape[1] for b in bs]
    n = sum(widths)
    tm = min(tm, t)

    def body(a_ref, *refs):
        b_refs, o_ref = refs[:-1], refs[-1]
        i = pl.program_id(0)
        av = a_ref[...].astype(BF16)
        parts = [b_ref[...].astype(BF16) for b_ref in b_refs]
        bv = parts[0] if len(parts) == 1 else jnp.concatenate(parts, axis=1)
        part = lax.dot_general(av, bv, (((0,), (0,)), ((), ())), preferred_element_type=F32)

        @pl.when(i == 0)
        def _():
            o_ref[...] = part

        @pl.when(i != 0)
        def _():
            o_ref[...] += part

    in_specs = [pl.BlockSpec((tm, k), lambda i: (i, 0))] + [pl.BlockSpec((tm, wd), lambda i: (i, 0)) for wd in widths]
    body, in_specs, args = _after(dep, body, in_specs, [a, *bs])
    return pl.pallas_call(
        body, name=name, grid=(t // tm,), in_specs=in_specs,
        out_specs=pl.BlockSpec((k, n), lambda i: (0, 0)),
        out_shape=jax.ShapeDtypeStruct((k, n), F32),
        compiler_params=_cparams(("arbitrary",)),
    )(*args)


def _rope(blk, cos_t, sin_a, sin_b):
    return blk * cos_t + pltpu.roll(blk, 112, 1) * sin_a + pltpu.roll(blk, 16, 1) * sin_b


def _unrope(d, cos_t, sin_a, sin_b):
    return d * cos_t + pltpu.roll(d * sin_a, 16, 1) + pltpu.roll(d * sin_b, 112, 1)


def _mla_prep(cq, ckv, kpe, gq, gkv, wuq, wk, wv, cos_t, sin_a, sin_b, tm=256):
    t = cq.shape[0]
    qw = A_HEADS * HEAD_PAD

    def body(cq_ref, ckv_ref, kpe_ref, gq_ref, gkv_ref, wuq_ref, wk_ref, wv_ref, c_ref, sa_ref, sb_ref,
             q_ref, k_ref, v_ref, cqn_ref, ckvn_ref):
        ct, sa, sb = c_ref[...], sa_ref[...], sb_ref[...]
        a = cq_ref[...]
        cqn = (a * lax.rsqrt(jnp.mean(a * a, axis=-1, keepdims=True) + EPS) * gq_ref[...]).astype(BF16)
        cqn_ref[...] = cqn
        b = ckv_ref[...]
        ckvn = (b * lax.rsqrt(jnp.mean(b * b, axis=-1, keepdims=True) + EPS) * gkv_ref[...]).astype(BF16)
        ckvn_ref[...] = ckvn
        qlin = jnp.dot(cqn, wuq_ref[...], preferred_element_type=F32)
        klin = jnp.dot(ckvn, wk_ref[...], preferred_element_type=F32)
        v_ref[...] = jnp.dot(ckvn, wv_ref[...], preferred_element_type=F32).astype(BF16)
        kr = _rope(kpe_ref[...], ct, sa, sb)
        for h in range(A_HEADS):
            sl = slice(h * HEAD_PAD, (h + 1) * HEAD_PAD)
            q_ref[:, sl] = _rope(qlin[:, sl], ct, sa, sb).astype(BF16)
            k_ref[:, sl] = (klin[:, sl] + kr).astype(BF16)

    row = lambda wd: pl.BlockSpec((tm, wd), lambda i: (i, 0))
    full = lambda r, c: pl.BlockSpec((r, c), lambda i: (0, 0))
    return pl.pallas_call(
        body, name="mla_prep", grid=(t // tm,),
        in_specs=[row(A_Q_RANK), row(A_KV_RANK), row(128), full(1, A_Q_RANK), full(1, A_KV_RANK),
                  full(A_Q_RANK, qw), full(A_KV_RANK, qw), full(A_KV_RANK, GW), row(128), row(128), row(128)],
        out_specs=[row(qw), row(qw), row(GW), row(A_Q_RANK), row(A_KV_RANK)],
        out_shape=[jax.ShapeDtypeStruct((t, qw), BF16), jax.ShapeDtypeStruct((t, qw), BF16),
                   jax.ShapeDtypeStruct((t, GW), BF16), jax.ShapeDtypeStruct((t, A_Q_RANK), BF16),
                   jax.ShapeDtypeStruct((t, A_KV_RANK), BF16)],
        compiler_params=_cparams(("arbitrary",)),
    )(cq, ckv, kpe, gq, gkv, wuq, wk, wv, cos_t, sin_a, sin_b)


def _mla_prep_bwd(dq, dk, dv, cq, ckv, gq, gkv, wuq_t, wk_t, wv_t, cos_t, sin_a, sin_b, tm=256):
    t = cq.shape[0]
    qw = A_HEADS * HEAD_PAD

    def body(dq_ref, dk_ref, dv_ref, cq_ref, ckv_ref, gq_ref, gkv_ref, wuqt_ref, wkt_ref, wvt_ref,
             c_ref, sa_ref, sb_ref, dcq_ref, dckv_ref, dkpe_ref, dql_ref, dkl_ref, dgq_ref, dgkv_ref):
        i = pl.program_id(0)
        ct, sa, sb = c_ref[...], sa_ref[...], sb_ref[...]
        lane = lax.broadcasted_iota(jnp.int32, (1, HEAD_PAD), 1)
        nope = lane < A_NOPE
        rope = (lane >= A_NOPE) & (lane < A_NOPE + A_ROPE)
        dksum = None
        for h in range(A_HEADS):
            sl = slice(h * HEAD_PAD, (h + 1) * HEAD_PAD)
            dql_ref[:, sl] = _unrope(dq_ref[:, sl], ct, sa, sb).astype(BF16)
            dkh = dk_ref[:, sl]
            dkl_ref[:, sl] = jnp.where(nope, dkh, 0.0).astype(BF16)
            dksum = dkh if dksum is None else dksum + dkh
        dkpe_ref[...] = jnp.where(rope, _unrope(jnp.where(rope, dksum, 0.0), ct, sa, sb), 0.0).astype(BF16)
        dcqn = jnp.dot(dql_ref[...], wuqt_ref[...], preferred_element_type=F32)
        dckvn = (jnp.dot(dkl_ref[...], wkt_ref[...], preferred_element_type=F32)
                 + jnp.dot(dv_ref[...].astype(BF16), wvt_ref[...], preferred_element_type=F32))

        def norm_bwd(xv, gv, dy):
            rstd = lax.rsqrt(jnp.mean(xv * xv, axis=-1, keepdims=True) + EPS)
            xn = xv * rstd
            dxn = dy * gv
            dx = rstd * (dxn - xn * jnp.mean(dxn * xn, axis=-1, keepdims=True))
            return dx, jnp.sum(dy * xn, axis=0, keepdims=True)

        dcq, dgq = norm_bwd(cq_ref[...], gq_ref[...], dcqn)
        dckv, dgkv = norm_bwd(ckv_ref[...], gkv_ref[...], dckvn)
        dcq_ref[...] = dcq.astype(BF16)
        dckv_ref[...] = dckv.astype(BF16)

        @pl.when(i == 0)
        def _():
            dgq_ref[...] = dgq
            dgkv_ref[...] = dgkv

        @pl.when(i != 0)
        def _():
            dgq_ref[...] += dgq
            dgkv_ref[...] += dgkv

    row = lambda wd: pl.BlockSpec((tm, wd), lambda i: (i, 0))
    full = lambda r, c: pl.BlockSpec((r, c), lambda i: (0, 0))
    return pl.pallas_call(
        body, name="mla_prep_bwd", grid=(t // tm,),
        in_specs=[row(qw), row(qw), row(GW), row(A_Q_RANK), row(A_KV_RANK), full(1, A_Q_RANK), full(1, A_KV_RANK),
                  full(qw, A_Q_RANK), full(qw, A_KV_RANK), full(GW, A_KV_RANK), row(128), row(128), row(128)],
        out_specs=[row(A_Q_RANK), row(A_KV_RANK), row(128), row(qw), row(qw), full(1, A_Q_RANK), full(1, A_KV_RANK)],
        out_shape=[jax.ShapeDtypeStruct((t, A_Q_RANK), BF16), jax.ShapeDtypeStruct((t, A_KV_RANK), BF16),
                   jax.ShapeDtypeStruct((t, 128), BF16), jax.ShapeDtypeStruct((t, qw), BF16),
                   jax.ShapeDtypeStruct((t, qw), BF16), jax.ShapeDtypeStruct((1, A_Q_RANK), F32),
                   jax.ShapeDtypeStruct((1, A_KV_RANK), F32)],
        compiler_params=_cparams(("arbitrary",)),
    )(dq, dk, dv, cq, ckv, gq, gkv, wuq_t, wk_t, wv_t, cos_t, sin_a, sin_b)


def _nt(a, b):
    return lax.dot_general(a, b, (((1,), (1,)), ((), ())), preferred_element_type=F32)


def _tn(a, b):
    return lax.dot_general(a, b, (((0,), (0,)), ((), ())), preferred_element_type=F32)


def _causal_mask(kind, q0, k0, tq, tk):
    qpos = q0 + lax.broadcasted_iota(jnp.int32, (tq, tk), 0)
    kpos = k0 + lax.broadcasted_iota(jnp.int32, (tq, tk), 1)
    if kind == "mla":
        return lax.shift_right_logical(kpos, 6) <= lax.shift_right_logical(qpos, 6)
    return kpos <= qpos


def _attn_fwd(kind, q, k, v, f, seq, scale, tq=512, tk=512):
    t = v.shape[0]
    nb = t // seq
    nq = seq // tq
    hw = 256 if kind == "mla" else 128
    n_heads = A_HEADS if kind == "mla" else C_HEADS
    use_f = f is not None
    tq, tk = min(tq, seq), min(tk, seq)
    nq = seq // tq
    assert tk == tq

    def body(*refs):
        if use_f:
            q_ref, k_ref, v_ref, f_ref, o_ref, st_ref = refs
        else:
            q_ref, k_ref, v_ref, o_ref, st_ref = refs
        qi = pl.program_id(2)
        q0 = qi * tq
        lane = lax.broadcasted_iota(jnp.int32, (1, 128), 1)
        half = lane >= 64
        qall = q_ref[...]
        if kind == "mla":
            qhs = [qall[:, 0:128], qall[:, 128:256]]
        else:
            qhs = [jnp.where(half, jnp.zeros_like(qall), qall), jnp.where(half, qall, jnp.zeros_like(qall))]
        kd = pl.multiple_of(q0, tq)
        diag = _causal_mask(kind, 0, 0, tq, tk)

        def block(j, k0, state, masked):
            m, l, acc = state
            kh = k_ref[pl.ds(k0, tk), j * 128:(j + 1) * 128] if kind == "mla" else k_ref[pl.ds(k0, tk), :]
            s = _nt(qhs[j], kh) * scale
            if use_f:
                s = s - f_ref[0, 0, j:j + 1, pl.ds(k0, tk)]
            if masked:
                s = jnp.where(diag, s, NEG)
            mn = jnp.maximum(m, jnp.max(s, axis=-1, keepdims=True))
            alpha = jnp.exp(m - mn)
            p = jnp.exp(s - mn)
            l = alpha * l + jnp.sum(p, axis=-1, keepdims=True)
            acc = alpha * acc + jnp.dot(p.astype(BF16), v_ref[pl.ds(k0, tk), :], preferred_element_type=F32)
            return mn, l, acc

        def run(heads):
            def kstep(kb, carry):
                k0 = pl.multiple_of(kb * tk, tk)
                out = ()
                for n, j in enumerate(heads):
                    out += block(j, k0, carry[3 * n:3 * n + 3], False)
                return out

            init = (jnp.full((tq, 1), NEG, F32), jnp.zeros((tq, 1), F32), jnp.zeros((tq, 128), F32)) * len(heads)
            carry = lax.fori_loop(0, qi, kstep, init)
            o, st = jnp.zeros((tq, 128), F32), jnp.zeros((tq, 128), F32)
            for n, j in enumerate(heads):
                m, l, acc = block(j, kd, carry[3 * n:3 * n + 3], True)
                o = jnp.where(half == bool(j), acc / l, o)
                st = jnp.where(lane == j, m + jnp.log(l), st)
            o_ref[...] = o
            st_ref[...] = st

        if n_heads % 2 == 0:
            run((0, 1))
        else:
            last = pl.program_id(1) == n_heads // 2
            pl.when(jnp.logical_not(last))(lambda: run((0, 1)))
            pl.when(last)(lambda: run((0,)))

    in_specs = [pl.BlockSpec((tq, hw), lambda b, p, i: (b * nq + i, p)),
                pl.BlockSpec((seq, hw), lambda b, p, i: (b, p)),
                pl.BlockSpec((seq, 128), lambda b, p, i: (b, p))]
    args = [q, k, v]
    if use_f:
        in_specs.append(pl.BlockSpec((1, 1, 8, seq), lambda b, p, i: (b, p, 0, 0)))
        args.append(f)
    oblk = pl.BlockSpec((tq, 128), lambda b, p, i: (b * nq + i, p))
    return pl.pallas_call(
        body, name="attn_fwd_" + kind, grid=(nb, 3, nq), in_specs=in_specs, out_specs=[oblk, oblk],
        out_shape=[jax.ShapeDtypeStruct((t, GW), F32), jax.ShapeDtypeStruct((t, GW), F32)],
        compiler_params=_cparams(("arbitrary", "arbitrary", "arbitrary")),
    )(*args)


def _attn_bwd(kind, q, k, v, f, o, st, do, seq, scale, tq=512, tk=512, dep=None):
    t = v.shape[0]
    nb = t // seq
    tq, tk = min(tq, seq), min(tk, seq)
    nq = seq // tq
    nk = seq // tk
    hw = 256 if kind == "mla" else 128
    n_heads = A_HEADS if kind == "mla" else C_HEADS
    use_f = f is not None
    assert tq == tk

    def body(*refs):
        if use_f:
            q_ref, k_ref, v_ref, f_ref, o_ref, st_ref, do_ref, dq_ref, dk_ref, dv_ref, df_ref, dfq_ref = refs
        else:
            q_ref, k_ref, v_ref, o_ref, st_ref, do_ref, dq_ref, dk_ref, dv_ref = refs
        kj = pl.program_id(2)
        k0 = kj * tk
        lane = lax.broadcasted_iota(jnp.int32, (1, 128), 1)
        half = lane >= 64

        @pl.when(kj == 0)
        def _():
            dq_ref[...] = jnp.zeros_like(dq_ref)
            if use_f:
                dfq_ref[...] = jnp.zeros_like(dfq_ref)

        dk_ref[...] = jnp.zeros_like(dk_ref)
        dv_ref[...] = jnp.zeros_like(dv_ref)
        if use_f:
            df_ref[...] = jnp.zeros_like(df_ref)
        vv = v_ref[...]
        diag = _causal_mask(kind, 0, 0, tq, tk)

        def qstep(qi, masked):
            q0 = pl.multiple_of(qi * tq, tq)
            rows = pl.ds(q0, tq)
            dov = do_ref[rows, :]
            dd = dov * o_ref[rows, :]
            stv = st_ref[rows, :]

            def one_head(j):
                hm = half == bool(j)
                delta = jnp.sum(jnp.where(hm, dd, 0.0), axis=-1, keepdims=True)
                lse = stv[:, j:j + 1]
                if kind == "mla":
                    cols = slice(j * 128, (j + 1) * 128)
                    qh = q_ref[rows, cols]
                    kh = k_ref[:, cols]
                else:
                    cols = slice(0, 128)
                    qa = q_ref[rows, :]
                    qh = jnp.where(hm, qa, jnp.zeros_like(qa))
                    kh = k_ref[...]
                s = _nt(qh, kh) * scale
                if use_f:
                    s = s - f_ref[0, 0, j:j + 1, :]
                if masked:
                    s = jnp.where(diag, s, NEG)
                p = jnp.exp(s - lse)
                doh = jnp.where(hm, dov, 0.0).astype(BF16)
                ds = p * (_nt(doh, vv) - delta)
                dsb = (ds * scale).astype(BF16)
                dv_ref[...] += _tn(p.astype(BF16), doh)
                dk_ref[:, cols] += _tn(dsb, qh)
                dqc = jnp.dot(dsb, kh, preferred_element_type=F32)
                if kind != "mla":
                    dqc = jnp.where(hm, dqc, 0.0)
                dq_ref[rows, cols] += dqc
                if use_f:
                    df_ref[0, 0, j:j + 1, :] += -jnp.sum(ds, axis=0, keepdims=True)
                    dfq_ref[rows, :] += jnp.where(lane == j, jnp.sum(ds, axis=-1, keepdims=True), 0.0)

            def both():
                one_head(0)
                one_head(1)

            if n_heads % 2 == 0:
                both()
            else:
                last = pl.program_id(1) == n_heads // 2
                pl.when(jnp.logical_not(last))(both)
                pl.when(last)(lambda: one_head(0))

        qstep(kj, True)

        def rest(qi, carry):
            qstep(qi, False)
            return carry

        lax.fori_loop(kj + 1, nq, rest, 0)

    full_q = lambda wd: pl.BlockSpec((seq, wd), lambda b, p, i: (b, p))
    kblk = lambda wd: pl.BlockSpec((tk, wd), lambda b, p, i: (b * nk + i, p))
    in_specs = [full_q(hw), kblk(hw), kblk(128)]
    args = [q, k, v]
    if use_f:
        in_specs.append(pl.BlockSpec((1, 1, 8, tk), lambda b, p, i: (b, p, 0, i)))
        args.append(f)
    in_specs += [full_q(128), full_q(128), full_q(128)]
    args += [o, st, do]
    out_specs = [full_q(hw), kblk(hw), kblk(128)]
    out_shape = [jax.ShapeDtypeStruct((t, 3 * hw), F32), jax.ShapeDtypeStruct((t, 3 * hw), F32),
                 jax.ShapeDtypeStruct((t, GW), F32)]
    if use_f:
        out_specs += [pl.BlockSpec((1, 1, 8, tk), lambda b, p, i: (b, p, 0, i)), full_q(128)]
        out_shape += [jax.ShapeDtypeStruct((nb, 3, 8, seq), F32), jax.ShapeDtypeStruct((t, GW), F32)]
    body, in_specs, args = _after(dep, body, in_specs, args)
    return pl.pallas_call(
        body, name="attn_bwd_" + kind, grid=(nb, 3, nk), in_specs=in_specs, out_specs=out_specs,
        out_shape=out_shape, compiler_params=_cparams(("arbitrary", "arbitrary", "arbitrary")),
    )(*args)


BQ = 256
BWIN = BQ + B_LEFT


def _band_geometry():
    r = lax.broadcasted_iota(jnp.int32, (BQ, BWIN), 0)
    j = lax.broadcasted_iota(jnp.int32, (BQ, BWIN), 1)
    rc = lax.shift_right_logical(r, 6)
    jc = lax.shift_right_logical(j, 6)
    allowed = (jc - 8 <= rc) & (rc <= jc)
    return (r + B_LEFT - j) >= REL_CLIP, allowed, j < r


def _band_onehot(transposed, offset=0):
    shape = (BWIN, GW) if transposed else (GW, BWIN)
    kk = lax.broadcasted_iota(jnp.int32, shape, 1 if transposed else 0)
    x = lax.broadcasted_iota(jnp.int32, shape, 0 if transposed else 1) - offset
    x = jnp.where(x < 0, x + BWIN, x)
    return (kk == jnp.clip(B_LEFT - x, -REL_CLIP, REL_CLIP) + REL_CLIP).astype(F32)


def _band_table(rel_bias8):
    def body(b_ref, o_ref):
        hh = pl.program_id(0)
        u8 = jnp.dot(b_ref[...], _band_onehot(False), precision=HI, preferred_element_type=F32)
        rid = lax.broadcasted_iota(jnp.int32, (8, BWIN), 0)
        row = jnp.sum(jnp.where(rid == hh, u8, 0.0), axis=0, keepdims=True)
        far, allowed, _ = _band_geometry()
        tbl = pltpu.roll(jnp.broadcast_to(row, (BQ, BWIN)), 0, 1, stride=1, stride_axis=0)
        tbl = jnp.where(far, row[:, 0:1], tbl)
        o_ref[0] = jnp.where(allowed, tbl, NEG)

    return pl.pallas_call(
        body, name="band_table", grid=(6,),
        in_specs=[pl.BlockSpec((8, GW), lambda h: (0, 0))],
        out_specs=pl.BlockSpec((1, BQ, BWIN), lambda h: (h, 0, 0)),
        out_shape=jax.ShapeDtypeStruct((6, BQ, BWIN), F32),
        compiler_params=_cparams(("arbitrary",)),
    )(rel_bias8)


def _band_table_bwd(gtab):
    def body(g_ref, o_ref):
        gv = g_ref[0]
        _, _, wrapped = _band_geometry()
        gfar = jnp.sum(jnp.sum(jnp.where(wrapped, gv, 0.0), axis=-1, keepdims=True), axis=0, keepdims=True)
        anti = (lax.broadcasted_iota(jnp.int32, (BQ, BQ), 0) + lax.broadcasted_iota(jnp.int32, (BQ, BQ), 1)
                == BQ - 1).astype(F32)
        grev = jnp.dot(anti, jnp.where(wrapped, 0.0, gv), precision=HI, preferred_element_type=F32)
        near = pltpu.roll(grev, 0, 1, stride=1, stride_axis=0)
        y = jnp.broadcast_to(jnp.sum(near, axis=0, keepdims=True), (8, BWIN))
        gb = jnp.dot(y, _band_onehot(True, BQ - 1), precision=HI, preferred_element_type=F32)
        lane = lax.broadcasted_iota(jnp.int32, (8, GW), 1)
        o_ref[0] = gb + jnp.where(lane == 2 * REL_CLIP, gfar, 0.0)

    return pl.pallas_call(
        body, name="band_table_bwd", grid=(B_HEADS,),
        in_specs=[pl.BlockSpec((1, BQ, BWIN), lambda h: (h, 0, 0))],
        out_specs=pl.BlockSpec((1, 8, GW), lambda h: (h, 0, 0)),
        out_shape=jax.ShapeDtypeStruct((B_HEADS, 8, GW), F32),
        compiler_params=_cparams(("arbitrary",)),
    )(gtab)


def _band_fwd(q, k, v, table, seq, scale):
    t = q.shape[0]
    nb = t // seq
    nq = seq // BQ

    def body(q_ref, k_ref, v_ref, tb_ref, o_ref, st_ref, kpad, vpad):
        qi = pl.program_id(2)
        q0 = pl.multiple_of(qi * BQ, BQ)
        lane = lax.broadcasted_iota(jnp.int32, (1, 128), 1)
        half = lane >= 64

        @pl.when(qi == 0)
        def _():
            kpad[0:B_LEFT, :] = jnp.zeros((B_LEFT, 128), BF16)
            vpad[0:B_LEFT, :] = jnp.zeros((B_LEFT, 128), BF16)
            kpad[B_LEFT:, :] = k_ref[...]
            vpad[B_LEFT:, :] = v_ref[...]

        kw = kpad[pl.ds(q0, BWIN), :]
        vw = vpad[pl.ds(q0, BWIN), :]
        inside = lax.broadcasted_iota(jnp.int32, (BQ, BWIN), 1) >= B_LEFT - q0
        qall = q_ref[...]

        def run(heads):
            o, st = jnp.zeros((BQ, 128), F32), jnp.zeros((BQ, 128), F32)
            for j in heads:
                qh = jnp.where(half == bool(j), qall, jnp.zeros_like(qall))
                s = jnp.where(inside, _nt(qh, kw) * scale + tb_ref[j], NEG)
                m = jnp.max(s, axis=-1, keepdims=True)
                p = jnp.exp(s - m)
                l = jnp.sum(p, axis=-1, keepdims=True)
                o = jnp.where(half == bool(j), jnp.dot(p.astype(BF16), vw, preferred_element_type=F32) / l, o)
                st = jnp.where(lane == j, m + jnp.log(l), st)
            o_ref[...] = o
            st_ref[...] = st

        last = pl.program_id(1) == B_HEADS // 2
        pl.when(jnp.logical_not(last))(lambda: run((0, 1)))
        pl.when(last)(lambda: run((0,)))

    qblk = pl.BlockSpec((BQ, 128), lambda b, p, i: (b * nq + i, p))
    full = pl.BlockSpec((seq, 128), lambda b, p, i: (b, p))
    return pl.pallas_call(
        body, name="band_fwd", grid=(nb, 3, nq),
        in_specs=[qblk, full, full, pl.BlockSpec((2, BQ, BWIN), lambda b, p, i: (p, 0, 0))],
        out_specs=[qblk, qblk],
        out_shape=[jax.ShapeDtypeStruct((t, GW), F32), jax.ShapeDtypeStruct((t, GW), F32)],
        scratch_shapes=[pltpu.VMEM((seq + B_LEFT, 128), BF16), pltpu.VMEM((seq + B_LEFT, 128), BF16)],
        compiler_params=_cparams(("arbitrary", "arbitrary", "arbitrary")),
    )(q, k, v, table)


def _band_bwd(q, k, v, table, o, st, do, seq, scale, dep=None):
    t = q.shape[0]
    nb = t // seq
    nq = seq // BQ

    def body(q_ref, k_ref, v_ref, tb_ref, o_ref, st_ref, do_ref, dq_ref, dk_ref, dv_ref, g_ref,
             kpad, vpad, dkpad, dvpad):
        b = pl.program_id(1)
        qi = pl.program_id(2)
        q0 = pl.multiple_of(qi * BQ, BQ)
        lane = lax.broadcasted_iota(jnp.int32, (1, 128), 1)
        half = lane >= 64

        @pl.when(qi == 0)
        def _():
            kpad[0:B_LEFT, :] = jnp.zeros((B_LEFT, 128), BF16)
            vpad[0:B_LEFT, :] = jnp.zeros((B_LEFT, 128), BF16)
            kpad[B_LEFT:, :] = k_ref[...]
            vpad[B_LEFT:, :] = v_ref[...]
            dkpad[...] = jnp.zeros_like(dkpad)
            dvpad[...] = jnp.zeros_like(dvpad)

        @pl.when((qi == 0) & (b == 0))
        def _():
            g_ref[...] = jnp.zeros_like(g_ref)

        win = pl.ds(q0, BWIN)
        kw = kpad[win, :]
        vw = vpad[win, :]
        inside = lax.broadcasted_iota(jnp.int32, (BQ, BWIN), 1) >= B_LEFT - q0
        qall = q_ref[...]
        dov = do_ref[...]
        dd = dov * o_ref[...]
        stv = st_ref[...]

        def run(heads):
            dq = jnp.zeros((BQ, 128), F32)
            for j in heads:
                hm = half == bool(j)
                qh = jnp.where(hm, qall, jnp.zeros_like(qall))
                delta = jnp.sum(jnp.where(hm, dd, 0.0), axis=-1, keepdims=True)
                s = jnp.where(inside, _nt(qh, kw) * scale + tb_ref[j], NEG)
                p = jnp.exp(s - stv[:, j:j + 1])
                doh = jnp.where(hm, dov, 0.0).astype(BF16)
                ds = p * (_nt(doh, vw) - delta)
                g_ref[j] += ds
                dsb = (ds * scale).astype(BF16)
                dvpad[win, :] += _tn(p.astype(BF16), doh)
                dkpad[win, :] += _tn(dsb, qh)
                dq = dq + jnp.where(hm, jnp.dot(dsb, kw, preferred_element_type=F32), 0.0)
            dq_ref[...] = dq.astype(BF16)

        last = pl.program_id(0) == B_HEADS // 2
        pl.when(jnp.logical_not(last))(lambda: run((0, 1)))
        pl.when(last)(lambda: run((0,)))

        @pl.when(qi == nq - 1)
        def _():
            dk_ref[...] = dkpad[B_LEFT:, :].astype(BF16)
            dv_ref[...] = dvpad[B_LEFT:, :].astype(BF16)

    qblk = pl.BlockSpec((BQ, 128), lambda p, b, i: (b * nq + i, p))
    full = pl.BlockSpec((seq, 128), lambda p, b, i: (b, p))
    tblk = pl.BlockSpec((2, BQ, BWIN), lambda p, b, i: (p, 0, 0))
    body, in_specs, args = _after(dep, body, [qblk, full, full, tblk, qblk, qblk, qblk], [q, k, v, table, o, st, do])
    return pl.pallas_call(
        body, name="band_bwd", grid=(3, nb, nq),
        in_specs=in_specs,
        out_specs=[qblk, full, full, tblk],
        out_shape=[jax.ShapeDtypeStruct((t, GW), BF16), jax.ShapeDtypeStruct((t, GW), BF16),
                   jax.ShapeDtypeStruct((t, GW), BF16), jax.ShapeDtypeStruct((6, BQ, BWIN), F32)],
        scratch_shapes=[pltpu.VMEM((seq + B_LEFT, 128), BF16), pltpu.VMEM((seq + B_LEFT, 128), BF16),
                        pltpu.VMEM((seq + B_LEFT, 128), F32), pltpu.VMEM((seq + B_LEFT, 128), F32)],
        compiler_params=_cparams(("arbitrary", "arbitrary", "arbitrary")),
    )(*args)


def _fox_prep(cf, fb, seq):
    nb = cf.shape[0] // seq
    nblk = seq // 128

    def body(cf_ref, fb_ref, f_ref):
        x = cf_ref[...] + fb_ref[...]
        lf = jnp.minimum(x, 0.0) - jnp.log1p(jnp.exp(-jnp.abs(x)))
        rows = lf.T[0:8, :]
        upper = (lax.broadcasted_iota(jnp.int32, (128, 128), 0)
                 <= lax.broadcasted_iota(jnp.int32, (128, 128), 1)).astype(F32)
        carry = jnp.zeros((8, 1), F32)
        for blk in range(nblk):
            sl = slice(blk * 128, (blk + 1) * 128)
            cs = jnp.dot(rows[:, sl], upper, precision=HI, preferred_element_type=F32) + carry
            carry = cs[:, 127:128]
            f_ref[0, 0, :, sl] = cs
            f_ref[0, 1, :, sl] = pltpu.roll(cs, 6, 0)
            f_ref[0, 2, :, sl] = pltpu.roll(cs, 4, 0)

    return pl.pallas_call(
        body, name="fox_prep", grid=(nb,),
        in_specs=[pl.BlockSpec((seq, 128), lambda b: (b, 0)), pl.BlockSpec((1, 128), lambda b: (0, 0))],
        out_specs=pl.BlockSpec((1, 3, 8, seq), lambda b: (b, 0, 0, 0)),
        out_shape=jax.ShapeDtypeStruct((nb, 3, 8, seq), F32),
        compiler_params=_cparams(("arbitrary",)),
    )(cf, fb)


def _fox_prep_bwd(df, dfq, cf, fb, seq):
    nb = cf.shape[0] // seq
    nblk = seq // 128

    def body(df_ref, dfq_ref, cf_ref, fb_ref, dcf_ref, dfb_ref, wide):
        b = pl.program_id(0)
        row = lax.broadcasted_iota(jnp.int32, (8, seq), 0)
        dfh = None
        for p in range(3):
            both = df_ref[0, p] + dfq_ref[:, p * 128:(p + 1) * 128].T[0:8, :]
            both = jnp.where(row < 2, both, 0.0)
            if p:
                both = pltpu.roll(both, 2 * p, 0)
            dfh = both if dfh is None else dfh + both
        lower = (lax.broadcasted_iota(jnp.int32, (128, 128), 0)
                 >= lax.broadcasted_iota(jnp.int32, (128, 128), 1)).astype(F32)
        wide[...] = jnp.zeros_like(wide)
        carry = jnp.zeros((8, 1), F32)
        for blk in reversed(range(nblk)):
            sl = slice(blk * 128, (blk + 1) * 128)
            rc = jnp.dot(dfh[:, sl], lower, precision=HI, preferred_element_type=F32) + carry
            carry = rc[:, 0:1]
            wide[0:8, sl] = rc
        dl = wide[...].T
        x = cf_ref[...] + fb_ref[...]
        dcf = dl * (1.0 / (1.0 + jnp.exp(x)))
        dcf_ref[...] = dcf.astype(BF16)
        part = jnp.sum(dcf, axis=0, keepdims=True)

        @pl.when(b == 0)
        def _():
            dfb_ref[...] = part

        @pl.when(b != 0)
        def _():
            dfb_ref[...] += part

    return pl.pallas_call(
        body, name="fox_prep_bwd", grid=(nb,),
        in_specs=[pl.BlockSpec((1, 3, 8, seq), lambda b: (b, 0, 0, 0)), pl.BlockSpec((seq, GW), lambda b: (b, 0)),
                  pl.BlockSpec((seq, 128), lambda b: (b, 0)), pl.BlockSpec((1, 128), lambda b: (0, 0))],
        out_specs=[pl.BlockSpec((seq, 128), lambda b: (b, 0)), pl.BlockSpec((1, 128), lambda b: (0, 0))],
        out_shape=[jax.ShapeDtypeStruct(cf.shape, BF16), jax.ShapeDtypeStruct((1, 128), F32)],
        scratch_shapes=[pltpu.VMEM((128, seq), F32)],
        compiler_params=_cparams(("arbitrary",)),
    )(df, dfq, cf, fb)


def _gate_out(oa, ob, oc, gates, w, x, gate, seq, tm=256):
    t = x.shape[0]
    tps = seq // tm

    def body(oa_ref, ob_ref, oc_ref, g_ref, w_ref, x_ref, gt_ref, xo_ref, y_ref, u_ref):
        for n, o_ref in enumerate((oa_ref, ob_ref, oc_ref)):
            sl = slice(n * GW, (n + 1) * GW)
            gv = g_ref[:, sl]
            u_ref[:, sl] = (o_ref[...] * (gv * _sigmoid(gv))).astype(BF16)
        y = jnp.dot(u_ref[...], w_ref[...], preferred_element_type=F32)
        y_ref[...] = y
        xo_ref[...] = x_ref[...] + gt_ref[0] * y

    row = lambda wd: pl.BlockSpec((tm, wd), lambda i: (i, 0))
    return pl.pallas_call(
        body, name="gate_out", grid=(t // tm,),
        in_specs=[row(GW), row(GW), row(GW), row(U_PAD), pl.BlockSpec((U_PAD, D_MODEL), lambda i: (0, 0)),
                  row(D_MODEL), pl.BlockSpec((1, 1, D_MODEL), lambda i: (i // tps, 0, 0))],
        out_specs=[row(D_MODEL), row(D_MODEL), row(U_PAD)],
        out_shape=[jax.ShapeDtypeStruct((t, D_MODEL), F32), jax.ShapeDtypeStruct((t, D_MODEL), F32),
                   jax.ShapeDtypeStruct((t, U_PAD), BF16)],
        compiler_params=_cparams(("arbitrary",)),
    )(oa, ob, oc, gates, w, x, gate)


def _gate_out_bwd(dxo, y, gate, oa, ob, oc, gates, w_t, seq, tm=256, dep=None):
    t = dxo.shape[0]
    tps = seq // tm
    nb = t // seq

    def body(dxo_ref, y_ref, gt_ref, oa_ref, ob_ref, oc_ref, g_ref, wt_ref,
             dy_ref, doa_ref, dob_ref, doc_ref, dg_ref, dgt_ref):
        i = pl.program_id(0)
        dxo_v = dxo_ref[...]
        dgt = jnp.sum(dxo_v * y_ref[...], axis=0, keepdims=True)
        dyb = (dxo_v * gt_ref[0]).astype(BF16)
        dy_ref[...] = dyb
        du = jnp.dot(dyb, wt_ref[...], preferred_element_type=F32)
        for n, (o_ref, do_ref) in enumerate(((oa_ref, doa_ref), (ob_ref, dob_ref), (oc_ref, doc_ref))):
            sl = slice(n * GW, (n + 1) * GW)
            gv = g_ref[:, sl]
            sg = _sigmoid(gv)
            dun = du[:, sl]
            do_ref[...] = dun * (gv * sg)
            dg_ref[:, sl] = (dun * o_ref[...] * (sg * (1.0 + gv * (1.0 - sg)))).astype(BF16)

        @pl.when(i % tps == 0)
        def _():
            dgt_ref[0] = dgt

        @pl.when(i % tps != 0)
        def _():
            dgt_ref[0] += dgt

    row = lambda wd: pl.BlockSpec((tm, wd), lambda i: (i, 0))
    per_b = pl.BlockSpec((1, 1, D_MODEL), lambda i: (i // tps, 0, 0))
    in_specs = [row(D_MODEL), row(D_MODEL), per_b, row(GW), row(GW), row(GW), row(U_PAD),
                pl.BlockSpec((D_MODEL, U_PAD), lambda i: (0, 0))]
    body, in_specs, args = _after(dep, body, in_specs, [dxo, y, gate, oa, ob, oc, gates, w_t])
    return pl.pallas_call(
        body, name="gate_out_bwd", grid=(t // tm,), in_specs=in_specs,
        out_specs=[row(D_MODEL), row(GW), row(GW), row(GW), row(U_PAD), per_b],
        out_shape=[jax.ShapeDtypeStruct((t, D_MODEL), BF16), jax.ShapeDtypeStruct((t, GW), F32),
                   jax.ShapeDtypeStruct((t, GW), F32), jax.ShapeDtypeStruct((t, GW), F32),
                   jax.ShapeDtypeStruct((t, U_PAD), BF16), jax.ShapeDtypeStruct((nb, 1, D_MODEL), F32)],
        compiler_params=_cparams(("arbitrary",)),
    )(*args)


def _final_loss(x, target, g, tm=256):
    t = x.shape[0]

    def body(x_ref, t_ref, g_ref, dx_ref, loss_ref, dg_ref):
        i = pl.program_id(0)
        xv = x_ref[...]
        rstd = lax.rsqrt(jnp.mean(xv * xv, axis=-1, keepdims=True) + EPS)
        xn = xv * rstd
        gv = g_ref[...]
        err = xn * gv - t_ref[...]
        dy = err * (1.0 / D_MODEL)
        dxn = dy * gv
        dx_ref[...] = rstd * (dxn - xn * jnp.mean(dxn * xn, axis=-1, keepdims=True))
        lp = jnp.sum(err * err, axis=0, keepdims=True) * (0.5 / D_MODEL)
        dgp = jnp.sum(dy * xn, axis=0, keepdims=True)

        @pl.when(i == 0)
        def _():
            loss_ref[...] = lp
            dg_ref[...] = dgp

        @pl.when(i != 0)
        def _():
            loss_ref[...] += lp
            dg_ref[...] += dgp

    row = pl.BlockSpec((tm, D_MODEL), lambda i: (i, 0))
    vec = pl.BlockSpec((1, D_MODEL), lambda i: (0, 0))
    return pl.pallas_call(
        body, name="final_loss", grid=(t // tm,),
        in_specs=[row, row, vec], out_specs=[row, vec, vec],
        out_shape=[jax.ShapeDtypeStruct((t, D_MODEL), F32), jax.ShapeDtypeStruct((1, D_MODEL), F32),
                   jax.ShapeDtypeStruct((1, D_MODEL), F32)],
        compiler_params=_cparams(("arbitrary",)),
    )(x, target, g)


def _adamw(w, gslots, m, v, name, tr=None):
    nl, r, c = w.shape
    ns = gslots.shape[0]
    tr = r if tr is None else tr

    def body(w_ref, g_ref, m_ref, v_ref, go_ref, d_ref, mo_ref, vo_ref):
        g = g_ref[0].astype(F32)
        for j in range(1, ns):
            g = g + g_ref[j].astype(F32)
        mn = ADAM_B1 * m_ref[...] + (1.0 - ADAM_B1) * g
        vn = ADAM_B2 * v_ref[...] + (1.0 - ADAM_B2) * jnp.square(g)
        m_hat = mn / (1.0 - ADAM_B1 ** ADAM_STEP)
        v_hat = vn / (1.0 - ADAM_B2 ** ADAM_STEP)
        go_ref[...] = g
        d_ref[...] = -ADAM_LR * (m_hat / (jnp.sqrt(v_hat) + ADAM_EPS) + ADAM_WD * w_ref[...])
        mo_ref[...] = mn
        vo_ref[...] = vn

    blk = pl.BlockSpec((1, tr, c), lambda l, i: (l, i, 0))
    return pl.pallas_call(
        body, name=name, grid=(nl, r // tr),
        in_specs=[blk, pl.BlockSpec((ns, 1, tr, c), lambda l, i: (0, l, i, 0)), blk, blk],
        out_specs=[blk] * 4, out_shape=[jax.ShapeDtypeStruct((nl, r, c), F32)] * 4,
        compiler_params=_cparams(("arbitrary", "arbitrary")),
    )(w, gslots, m, v)


def _rope_tables(positions):
    inv = ROPE_THETA ** (-jnp.arange(0, A_ROPE, 2, dtype=F32) / A_ROPE)
    ang = positions.astype(F32)[:, None] * inv
    cos, sin = jnp.cos(ang), jnp.sin(ang)
    t = positions.shape[0]
    one = jnp.ones((t, 64), F32)
    zero16 = jnp.zeros((t, 16), F32)
    cos_t = jnp.concatenate([one, cos, cos, jnp.ones((t, 32), F32)], axis=1)
    sin_a = jnp.concatenate([jnp.zeros((t, 64), F32), -sin, zero16, jnp.zeros((t, 32), F32)], axis=1)
    sin_b = jnp.concatenate([jnp.zeros((t, 64), F32), zero16, sin, jnp.zeros((t, 32), F32)], axis=1)
    return cos_t, sin_a, sin_b


def _pad_heads(w, real, padded, nheads, axis):
    shp = w.shape[:axis] + (nheads, real) + w.shape[axis + 1:]
    w = w.reshape(shp)
    pad = [(0, 0)] * w.ndim
    pad[axis + 1] = (0, padded - real)
    w = jnp.pad(w, pad)
    return w.reshape(w.shape[:axis] + (nheads * padded,) + w.shape[axis + 2:])


def kernel(x, c, positions, w_ada, b_ada, norm_g, w_in, a_q_norm_g, a_w_uq, a_kv_norm_g, a_w_ukv, b_rel_bias, c_forget_b, w_out, final_g, loss_target, m_w_ada, m_b_ada, m_norm_g, m_w_in, m_a_q_norm_g, m_a_w_uq, m_a_kv_norm_g, m_a_w_ukv, m_b_rel_bias, m_c_forget_b, m_w_out, m_final_g, v_w_ada, v_b_ada, v_norm_g, v_w_in, v_a_q_norm_g, v_a_w_uq, v_a_kv_norm_g, v_a_w_ukv, v_b_rel_bias, v_c_forget_b, v_w_out, v_final_g):
    nb, seq, _ = x.shape
    t = nb * seq
    me = 4 * lax.axis_index("x") + 2 * lax.axis_index("y") + lax.axis_index("c")
    x2 = x.reshape(t, D_MODEL)
    tgt = loss_target.reshape(t, D_MODEL)
    cos_t, sin_a, sin_b = _rope_tables(positions.reshape(t))

    def shards(l):
        return [_pad_runs(w_in[l].astype(BF16), IN_RUNS, N_PAD, 1), w_out[l].astype(BF16),
                a_w_uq[l].astype(BF16), a_w_ukv[l].astype(BF16)]

    def prepare(gi, go, gq, gkv):
        wi = gi.reshape(D_MODEL, N_PAD)
        wo = _pad_runs(go.reshape(D_MODEL, D_MODEL), OUT_RUNS, U_PAD, 0)
        wq = jnp.transpose(gq, (1, 0, 2)).reshape(A_Q_RANK, A_HEADS * (A_NOPE + A_ROPE))
        wq = _pad_heads(wq, A_NOPE + A_ROPE, HEAD_PAD, A_HEADS, 1)
        wkv = jnp.transpose(gkv, (1, 0, 2)).reshape(A_KV_RANK, A_HEADS, 2 * A_NOPE)
        wk = jnp.pad(wkv[:, :, :A_NOPE], ((0, 0), (0, 0), (0, HEAD_PAD - A_NOPE))).reshape(A_KV_RANK, A_HEADS * HEAD_PAD)
        wv = wkv[:, :, A_NOPE:].reshape(A_KV_RANK, GW)
        return dict(w_in=wi, w_in_t=wi.T, w_out=wo, w_out_t=wo.T, wuq=wq, wuq_t=wq.T, wk=wk, wk_t=wk.T,
                    wv=wv, wv_t=wv.T)

    gathered = _gather(shards(0) + [c], "gather_weights0")
    c_all = gathered[-1].reshape(N_DEV * nb, D_MODEL)
    weights = [prepare(*gathered[:4]), None]

    c_act, mod_cols = _ada_fwd(c_all, w_ada)
    (mod_g,) = _gather([mod_cols], "gather_mod")
    gather1, gather1_token = _split_start("gather", shards(1), "gather_weights1_start", after=mod_g)
    mod_all = jnp.transpose(mod_g, (1, 2, 0, 3)).reshape(DEPTH, N_DEV * nb, 3 * D_MODEL)
    mod = lax.dynamic_slice_in_dim(mod_all, me * nb, nb, axis=1) + b_ada[:, None, :]

    fb_pad = jnp.pad(c_forget_b, ((0, 0), (0, 128 - C_HEADS)))
    a_scale = (A_NOPE + A_ROPE) ** -0.5
    h_scale = CHUNK ** -0.5

    saved = []
    xl = x2
    for l in range(DEPTH):
        if l == 1:
            weights[1] = prepare(*_split_wait(gather1, xl, "gather_weights1_wait")[1])
        w = weights[l]
        shift, scale, gate = mod[l, :, :D_MODEL], mod[l, :, D_MODEL:2 * D_MODEL], mod[l, :, 2 * D_MODEL:]
        ss = jnp.stack([shift, 1.0 + scale], axis=1)
        gate3 = gate[:, None, :]
        h, cq, ckv, kpe, gates, bq, bk, bv, cq2, ck, cv, cf = _ln_in(
            xl, ss, norm_g[l:l + 1], w["w_in"], seq, dep=gather1_token if l == 0 else None)
        q, k, v, cqn, ckvn = _mla_prep(cq, ckv, kpe, a_q_norm_g[l:l + 1], a_kv_norm_g[l:l + 1],
                                       w["wuq"], w["wk"], w["wv"], cos_t, sin_a, sin_b)
        oa, sta = _attn_fwd("mla", q, k, v, None, seq, a_scale)
        table = _band_table(jnp.pad(b_rel_bias[l], ((0, 8 - B_HEADS), (0, GW - N_REL))))
        ob, stb = _band_fwd(bq, bk, bv, table, seq, h_scale)
        fcum = _fox_prep(cf, fb_pad[l:l + 1], seq)
        oc, stc = _attn_fwd("fox", cq2, ck, cv, fcum, seq, h_scale)
        xn, y, u = _gate_out(oa, ob, oc, gates, w["w_out"], xl, gate3, seq)
        saved.append(dict(x=xl, ss=ss, gate3=gate3, h=h, cq=cq, ckv=ckv, gates=gates, bq=bq, bk=bk, bv=bv,
                          cq2=cq2, ck=ck, cv=cv, cf=cf, q=q, k=k, v=v, cqn=cqn, ckvn=ckvn, oa=oa, sta=sta,
                          table=table, ob=ob, stb=stb, fcum=fcum, oc=oc, stc=stc, y=y, u=u))
        xl = xn

    dx, loss_lanes, g_final = _final_loss(xl, tgt, final_g[None, :])
    loss = lax.psum(jnp.sum(loss_lanes), AXES)

    rows = D_MODEL // N_DEV
    core = lax.axis_index("c").astype(jnp.int32).reshape(1)
    n_seg_a = 4
    dmods, smalls, parts = [None] * DEPTH, [None] * DEPTH, [None] * DEPTH
    pair1 = chips1 = pair1_token = chips1_token = None
    for l in reversed(range(DEPTH)):
        s, w = saved[l], weights[l]
        dy, doa, dob, doc, dgates, dgate = _gate_out_bwd(dx, s["y"], s["gate3"], s["oa"], s["ob"], s["oc"],
                                                         s["gates"], w["w_out_t"], seq, dep=pair1_token)
        g_out = _unpad_runs(_matmul_tn(s["u"], dy, "dw_out"), OUT_RUNS, 0)
        if l == 0:
            own, from_sib = _split_wait(pair1, g_out, "grads1_pair_wait")
            chips1, chips1_token = _split_start("chips", _pair_add(core, own, from_sib, "grads1_add"), "grads1_chips_start")
        dq, dk, dv = _attn_bwd("mla", s["q"], s["k"], s["v"], None, s["oa"], s["sta"], doa, seq, a_scale,
                               dep=chips1_token)
        dbq, dbk, dbv, gtab = _band_bwd(s["bq"], s["bk"], s["bv"], s["table"], s["ob"], s["stb"], dob, seq, h_scale,
                                        dep=chips1_token)
        g_rel = _band_table_bwd(gtab)[:, 0, :N_REL]
        dcq2, dck, dcv, dfc, dfq = _attn_bwd("fox", s["cq2"], s["ck"], s["cv"], s["fcum"], s["oc"], s["stc"], doc,
                                             seq, h_scale, dep=chips1_token)
        dcf, dfb = _fox_prep_bwd(dfc, dfq, s["cf"], fb_pad[l:l + 1], seq)
        dcq, dckv, dkpe, dqlin, dklin, dgq, dgkv = _mla_prep_bwd(
            dq, dk, dv, s["cq"], s["ckv"], a_q_norm_g[l:l + 1], a_kv_norm_g[l:l + 1],
            w["wuq_t"], w["wk_t"], w["wv_t"], cos_t, sin_a, sin_b)
        gq_pad = _matmul_tn(s["cqn"], dqlin, "dw_uq")
        g_uq = gq_pad.reshape(A_Q_RANK, A_HEADS, HEAD_PAD)[:, :, :A_NOPE + A_ROPE].reshape(A_Q_RANK, -1)
        gkv_pad = _matmul_tn(s["ckvn"], [dklin, dv], "dw_ukv")
        gk_pad = gkv_pad[:, :A_HEADS * HEAD_PAD].reshape(A_KV_RANK, A_HEADS, HEAD_PAD)[:, :, :A_NOPE]
        gv_pad = gkv_pad[:, A_HEADS * HEAD_PAD:].reshape(A_KV_RANK, A_HEADS, A_NOPE)
        g_ukv = jnp.concatenate([gk_pad, gv_pad], axis=2).reshape(A_KV_RANK, -1)
        dz = [dcq, dckv, dkpe, dgates, dbq, dbk, dbv, dcq2, dck, dcv, dcf]
        g_in_a = _matmul_tn(s["h"], dz[:n_seg_a], "dw_in_a")
        first = [g_in_a.reshape(N_DEV, rows, -1), g_out.reshape(N_DEV, rows, D_MODEL),
                 g_uq.reshape(A_Q_RANK, N_DEV, -1).transpose(1, 0, 2), g_ukv.reshape(A_KV_RANK, N_DEV, -1).transpose(1, 0, 2)]
        if l == 1:
            g_in_b = _matmul_tn(s["h"], dz[n_seg_a:], "dw_in_b")
            pair1, pair1_token = _split_start("pair", first + [g_in_b.reshape(N_DEV, rows, -1)], "grads1_pair_start")
            tail_token = None
        else:
            pair0a, pair0a_token = _split_start("pair", first, "grads0a_pair_start")
            g_in_b = _matmul_tn(s["h"], dz[n_seg_a:], "dw_in_b", dep=pair0a_token)
            own, from_sib = _split_wait(pair0a, g_in_b, "grads0a_pair_wait")
            sums0a = _pair_add(core, own, from_sib, "grads0a_add")
            pair0b, pair0b_token = _split_start("pair", [g_in_b.reshape(N_DEV, rows, -1)], "grads0b_pair_start",
                                                after=sums0a[0])
            chips0a, tail_token = _split_start("chips", sums0a, "grads0a_chips_start", after=pair0b_token)
        dx, dss, dg_norm = _ln_in_bwd(dz, w["w_in_t"], s["x"], s["ss"], norm_g[l:l + 1], dx, seq, dep=tail_token)
        dmods[l] = jnp.concatenate([dss[:, 0, :], dss[:, 1, :], dgate[:, 0, :]], axis=1)
        smalls[l] = [dg_norm.reshape(-1), dgq.reshape(-1), dgkv.reshape(-1), g_rel.reshape(-1),
                     dfb[0, :C_HEADS]]
    grad_x = dx.reshape(nb, seq, D_MODEL)
    parts[1] = _split_wait(chips1, dx, "grads1_chips_wait")[1]
    parts0a = _split_wait(chips0a, dx, "grads0a_chips_wait")[1]
    own, from_sib = _split_wait(pair0b, dx, "grads0b_pair_wait")

    small = jnp.concatenate([p for l in range(DEPTH) for p in smalls[l]] + [g_final.reshape(-1)])
    n_small = small.shape[0]
    small_rows = -(-n_small // 1024) * 8
    small = jnp.pad(small, (0, small_rows * 128 - n_small)).reshape(small_rows, 128)
    dmod_local = jnp.stack(dmods)
    dmod_g, small_g = _gather([dmod_local, small], "gather_small", dep=parts0a[0])
    chips0, chips0_token = _split_start("chips", _pair_add(core, own, from_sib, "grads0b_add"), "grads0b_chips_start",
                                        after=small_g)
    dmod_all = jnp.transpose(dmod_g, (1, 0, 2, 3)).reshape(DEPTH, N_DEV * nb, 3 * D_MODEL)
    cols = 3 * D_MODEL // N_DEV
    dmod_mine = lax.dynamic_slice_in_dim(dmod_all, me * cols, cols, axis=2)
    g_w_ada, g_b_ada = _ada_bwd(c_act, dmod_all, dmod_mine, chips0_token)
    small_sum = _sum_slots(small_g, "sum_small").reshape(-1)

    def split_small():
        out, pos = [], 0
        sizes = [D_MODEL, A_Q_RANK, A_KV_RANK, B_HEADS * N_REL, C_HEADS]
        per_layer = []
        for l in range(DEPTH):
            parts = []
            for sz in sizes:
                parts.append(small_sum[pos:pos + sz])
                pos += sz
            per_layer.append(parts)
        for j in range(len(sizes)):
            out.append(jnp.stack([per_layer[l][j] for l in range(DEPTH)]))
        out.append(small_sum[pos:pos + D_MODEL])
        return out

    g_norm, g_qn, g_kvn, g_relb, g_fb, g_fin = split_small()

    def adam(w, g, m, v, name, tr=None):
        shp = w.shape
        w3 = w.reshape((1,) * (3 - w.ndim) + shp)
        outs = _adamw(w3, g.reshape((-1,) + w3.shape), m.reshape(w3.shape), v.reshape(w3.shape), name, tr)
        return [o.reshape(shp) for o in outs]

    res = {
        "w_ada": adam(w_ada, g_w_ada, m_w_ada, v_w_ada, "adam_w_ada", 256),
        "b_ada": adam(b_ada, g_b_ada, m_b_ada, v_b_ada, "adam_b_ada"),
        "norm_g": adam(norm_g, g_norm, m_norm_g, v_norm_g, "adam_norm_g"),
        "a_q_norm_g": adam(a_q_norm_g, g_qn, m_a_q_norm_g, v_a_q_norm_g, "adam_q_norm"),
        "a_kv_norm_g": adam(a_kv_norm_g, g_kvn, m_a_kv_norm_g, v_a_kv_norm_g, "adam_kv_norm"),
        "b_rel_bias": adam(b_rel_bias, g_relb.reshape(b_rel_bias.shape), m_b_rel_bias, v_b_rel_bias, "adam_rel_bias"),
        "c_forget_b": adam(c_forget_b, g_fb, m_c_forget_b, v_c_forget_b, "adam_forget_b"),
        "final_g": adam(final_g, g_fin, m_final_g, v_final_g, "adam_final_g"),
    }
    parts[0] = list(parts0a) + list(_split_wait(chips0, res["w_ada"][1], "grads0b_chips_wait")[1])
    p_in = jnp.stack([_unpad_runs(jnp.concatenate([parts[l][0], parts[l][4]], axis=2), IN_RUNS, 2)
                      for l in range(DEPTH)], axis=1)
    p_out, p_uq, p_ukv = (jnp.stack([parts[l][j] for l in range(DEPTH)], axis=1) for j in (1, 2, 3))
    res.update({
        "w_in": adam(w_in, p_in, m_w_in, v_w_in, "adam_w_in", 32),
        "a_w_uq": adam(a_w_uq, p_uq, m_a_w_uq, v_a_w_uq, "adam_w_uq"),
        "a_w_ukv": adam(a_w_ukv, p_ukv, m_a_w_ukv, v_a_w_ukv, "adam_w_ukv"),
        "w_out": adam(w_out, p_out, m_w_out, v_w_out, "adam_w_out", 64),
    })
    names = ["w_ada", "b_ada", "norm_g", "w_in", "a_q_norm_g", "a_w_uq", "a_kv_norm_g", "a_w_ukv", "b_rel_bias",
             "c_forget_b", "w_out", "final_g"]
    outs = [loss, grad_x]
    for j in range(4):
        outs += [res[n][j] for n in names]
    return tuple(outs)
```

```python
import functools
import math

import jax
import jax.numpy as jnp
from jax import lax
from jax.experimental import pallas as pl
from jax.experimental.pallas import tpu as pltpu

F32 = jnp.float32
BF16 = jnp.bfloat16
HI = lax.Precision.HIGHEST

N_DEV = 8
AXES = ("x", "y", "c")
D_MODEL = 1024
DEPTH = 2
CHUNK = 64
EPS = 1e-6
NEG = -1e30
A_HEADS = 6
A_NOPE = 64
A_ROPE = 32
A_Q_RANK = 384
A_KV_RANK = 256
ROPE_THETA = 10000.0
B_HEADS = 5
B_LEFT = 512
REL_CLIP = 128
N_REL = 2 * REL_CLIP + 1
C_HEADS = 5
HEAD_PAD = 128
GW = 384
N_IN = 3621
ADAM_LR = 0.001
ADAM_B1 = 0.9
ADAM_B2 = 0.999
ADAM_EPS = 1e-08
ADAM_WD = 0.01
ADAM_STEP = 10
VMEM_LIMIT = 56 * 1024 * 1024
ROW_TILE = 512

Z_SEGS = (
    ("cq", 0, 384, F32), ("ckv", 384, 256, F32), ("kpe", 640, 128, F32), ("gates", 768, 1152, BF16),
    ("bq", 1920, 384, BF16), ("bk", 2304, 384, BF16), ("bv", 2688, 384, BF16),
    ("cq2", 3072, 384, BF16), ("ck", 3456, 384, BF16), ("cv", 3840, 384, BF16), ("cf", 4224, 128, F32),
)
N_PAD = 4352
IN_RUNS = (
    (0, 384, 0), (384, 256, 384), (640 + 64, 32, 640),
    (768, 384, 672), (768 + 384, 320, 2016), (768 + 768, 320, 3301),
    (1920, 320, 1056), (2304, 320, 1376), (2688, 320, 1696),
    (3072, 320, 2336), (3456, 320, 2656), (3840, 320, 2976), (4224, 5, 3296),
)
OUT_RUNS = ((0, 384, 0), (384, 320, 384), (768, 320, 704))
U_PAD = 1152


def _cparams(sem=None, vmem=VMEM_LIMIT):
    return pltpu.CompilerParams(dimension_semantics=sem, vmem_limit_bytes=vmem)


def _after(dep, body, in_specs, args):
    if dep is None:
        return body, in_specs, args
    n = len(args)

    def ordered(*refs):
        return body(*refs[:n], *refs[n + 1:])

    return ordered, list(in_specs) + [pl.BlockSpec((8, 128), lambda *_: (0, 0))], list(args) + [dep]


def _pad_runs(w, runs, total, axis):
    order = sorted(runs)
    parts, pos = [], 0
    for off, wd, src in order:
        if off > pos:
            shp = list(w.shape)
            shp[axis] = off - pos
            parts.append(jnp.zeros(shp, w.dtype))
        parts.append(lax.slice_in_dim(w, src, src + wd, axis=axis))
        pos = off + wd
    if pos < total:
        shp = list(w.shape)
        shp[axis] = total - pos
        parts.append(jnp.zeros(shp, w.dtype))
    return jnp.concatenate(parts, axis=axis)


def _unpad_runs(w, runs, axis):
    order = sorted(runs, key=lambda r: r[2])
    return jnp.concatenate([lax.slice_in_dim(w, off, off + wd, axis=axis) for off, wd, _ in order], axis=axis)


def _sigmoid(x):
    return 1.0 / (1.0 + jnp.exp(-x))


N_CHIP = 4
ANY_SPEC = pl.BlockSpec(memory_space=pl.ANY)
MESH_ID = pl.DeviceIdType.MESH


def _gather(arrs, name, dep=None):
    n = len(arrs)
    nin = n + (dep is not None)

    def body(*refs):
        ins, outs = refs[:n], refs[nin:nin + n]
        send_sems, recv_sems, local_sems = refs[nin + n:]
        x, y, c = lax.axis_index("x"), lax.axis_index("y"), lax.axis_index("c")
        me, sib = (x, y, c), (x, y, 1 - c)
        chips = [(1 - x, y), (x, 1 - y), (1 - x, 1 - y)]

        def slot(px, py, pc):
            return 4 * px + 2 * py + pc

        def copy(a, k, block, to, src=None):
            dst = outs[a].at[slot(*block)]
            return pltpu.make_async_remote_copy(
                src_ref=dst if src is None else src, dst_ref=dst, send_sem=send_sems.at[a, k],
                recv_sem=recv_sems.at[a, k], device_id=to, device_id_type=MESH_ID)

        local = [pltpu.make_async_copy(ins[a], outs[a].at[slot(*me)], local_sems.at[a]) for a in range(n)]
        first = []
        for a in range(n):
            first.append(copy(a, 0, me, sib, src=ins[a]))
            first += [copy(a, 1 + j, me, (*chip, c), src=ins[a]) for j, chip in enumerate(chips)]
        for cp in local + first:
            cp.start()
        passed = []
        for j, chip in enumerate(chips):
            for a in range(n):
                copy(a, 1 + j, (*chip, c), me).wait_recv()
                fwd = copy(a, 4 + j, (*chip, c), sib)
                fwd.start()
                passed.append(fwd)
        for a in range(n):
            copy(a, 0, sib, me).wait_recv()
            for j, chip in enumerate(chips):
                copy(a, 4 + j, (*chip, 1 - c), me).wait_recv()
        for cp in first + passed:
            cp.wait_send()
        for cp in local:
            cp.wait()

    return pl.pallas_call(
        body, name=name, out_shape=[jax.ShapeDtypeStruct((N_DEV,) + a.shape, a.dtype) for a in arrs],
        in_specs=[ANY_SPEC] * nin, out_specs=[ANY_SPEC] * n,
        scratch_shapes=[pltpu.SemaphoreType.DMA((n, N_DEV - 1)), pltpu.SemaphoreType.DMA((n, N_DEV - 1)),
                        pltpu.SemaphoreType.DMA((n,))],
    )(*arrs, *([] if dep is None else [dep]))


HBM_SPEC = pl.BlockSpec(memory_space=pltpu.HBM)
SEM_SPEC = pl.BlockSpec(memory_space=pltpu.SEMAPHORE)
SPLIT_EFFECT = pltpu.SideEffectType.DATAFLOW_SIDE_EFFECTING
SPLIT_SEMS = {"gather": (N_DEV - 1, True), "pair": (N_CHIP, False), "chips": (N_CHIP - 1, True)}


def _split_descriptors(pattern, srcs, lands, sems):
    x, y, c = lax.axis_index("x"), lax.axis_index("y"), lax.axis_index("c")
    nsem, has_local = SPLIT_SEMS[pattern]
    per = 2 * nsem + int(has_local)
    starts, arrivals, local = [], [], []

    def remote(a, k, src, dst, to):
        return pltpu.make_async_remote_copy(src_ref=src, dst_ref=dst, send_sem=sems[a * per + k],
                                            recv_sem=sems[a * per + nsem + k], device_id=to, device_id_type=MESH_ID)

    for a in range(len(srcs)):
        if pattern == "gather":
            me = 4 * x + 2 * y + c
            local.append(pltpu.make_async_copy(srcs[a], lands[a].at[me], sems[a * per + 2 * nsem]))
            for k in range(1, N_DEV):
                px = (1 - x) if (k >> 2) & 1 else x
                py = (1 - y) if (k >> 1) & 1 else y
                pc = (1 - c) if k & 1 else c
                starts.append(remote(a, k - 1, srcs[a], lands[a].at[me], (px, py, pc)))
                arrivals.append(remote(a, k - 1, srcs[a], lands[a].at[4 * px + 2 * py + pc], (px, py, pc)))
        elif pattern == "pair":
            for q in range(N_CHIP):
                cp = remote(a, q, srcs[a].at[2 * q + 1 - c], lands[a].at[q], (x, y, 1 - c))
                starts.append(cp)
                arrivals.append(cp)
        else:
            mine = 2 * x + y
            local.append(pltpu.make_async_copy(srcs[a].at[mine], lands[a].at[mine], sems[a * per + 2 * nsem]))
            for k in range(1, N_CHIP):
                px = (1 - x) if (k >> 1) & 1 else x
                py = (1 - y) if k & 1 else y
                starts.append(remote(a, k - 1, srcs[a].at[2 * px + py], lands[a].at[mine], (px, py, c)))
                arrivals.append(remote(a, k - 1, srcs[a].at[2 * px + py], lands[a].at[2 * px + py], (px, py, c)))
    return starts, arrivals, local


def _split_start(pattern, arrs, name, after=None):
    n = len(arrs)
    extra = [] if after is None else [after]
    nsem, has_local = SPLIT_SEMS[pattern]
    if pattern == "gather":
        land_shapes = [(N_DEV,) + a.shape for a in arrs]
    elif pattern == "pair":
        land_shapes = [(N_CHIP,) + a.shape[1:] for a in arrs]
    else:
        land_shapes = [a.shape for a in arrs]
    nsem_out = n * (2 * nsem + int(has_local))

    def body(*refs):
        srcs, lands = refs[:n], refs[n:2 * n]
        first_sem = 2 * n + len(extra)
        sems = refs[first_sem:first_sem + nsem_out]
        token = refs[-1]
        starts, _, local = _split_descriptors(pattern, srcs, lands, sems)
        for cp in local + starts:
            cp.start()
        token[...] = jnp.zeros_like(token)

    out_shape = ([pltpu.SemaphoreType.DMA(())] * nsem_out + [pltpu.HBM(a.shape, a.dtype) for a in arrs]
                 + [pltpu.HBM(s, a.dtype) for s, a in zip(land_shapes, arrs)] + [jax.ShapeDtypeStruct((8, 128), F32)])
    ins = ([pltpu.with_memory_space_constraint(a, pltpu.HBM) for a in arrs]
           + [pltpu.with_memory_space_constraint(lax.empty(s, a.dtype), pltpu.HBM) for s, a in zip(land_shapes, arrs)])
    outs = pl.pallas_call(
        body, name=name, out_shape=out_shape, in_specs=[HBM_SPEC] * (2 * n) + [ANY_SPEC] * len(extra),
        out_specs=[SEM_SPEC] * nsem_out + [HBM_SPEC] * (2 * n) + [pl.BlockSpec(memory_space=pltpu.VMEM)],
        input_output_aliases={i: nsem_out + i for i in range(2 * n)},
        compiler_params=pltpu.CompilerParams(has_side_effects=SPLIT_EFFECT),
    )(*ins, *extra)
    handle = dict(pattern=pattern, n=n, sems=outs[:nsem_out], srcs=outs[nsem_out:nsem_out + n],
                  lands=outs[nsem_out + n:nsem_out + 2 * n])
    return handle, outs[-1]


def _split_wait(handle, after, name):
    pattern, n = handle["pattern"], handle["n"]
    nsem_in = len(handle["sems"])

    def body(*refs):
        srcs, lands = refs[:n], refs[n:2 * n]
        starts, arrivals, local = _split_descriptors(pattern, srcs, lands, refs[2 * n:2 * n + nsem_in])
        for cp in starts:
            cp.wait_send()
        for cp in arrivals:
            cp.wait_recv()
        for cp in local:
            cp.wait()

    srcs, lands = handle["srcs"], handle["lands"]
    outs = pl.pallas_call(
        body, name=name,
        out_shape=[pltpu.HBM(a.shape, a.dtype) for a in srcs] + [pltpu.HBM(a.shape, a.dtype) for a in lands],
        in_specs=[HBM_SPEC] * (2 * n) + [SEM_SPEC] * nsem_in + [ANY_SPEC], out_specs=[HBM_SPEC] * (2 * n),
        input_output_aliases={i: i for i in range(2 * n)},
        compiler_params=pltpu.CompilerParams(has_side_effects=SPLIT_EFFECT),
    )(*srcs, *lands, *handle["sems"], after)
    return outs[:n], outs[n:]


def _pair_add(core, a8s, b4s, name):
    n = len(a8s)

    def body(core_ref, *refs):
        for i in range(n):
            refs[2 * n + i][...] = (refs[i][...] + refs[n + i][...]).astype(BF16)

    own = [pl.BlockSpec((1,) + b.shape[1:], lambda q, core_ref: (2 * q + core_ref[0], 0, 0)) for b in b4s]
    slot = [pl.BlockSpec((1,) + b.shape[1:], lambda q, core_ref: (q, 0, 0)) for b in b4s]
    grid_spec = pltpu.PrefetchScalarGridSpec(num_scalar_prefetch=1, grid=(N_CHIP,), in_specs=own + slot, out_specs=slot)
    return pl.pallas_call(
        body, name=name, grid_spec=grid_spec, out_shape=[jax.ShapeDtypeStruct(b.shape, BF16) for b in b4s],
        compiler_params=_cparams(("arbitrary",)),
    )(core, *a8s, *b4s)


def _sum_slots(x, name):
    _, r, c = x.shape

    def body(x_ref, o_ref):
        acc = x_ref[0]
        for j in range(1, N_DEV):
            acc = acc + x_ref[j]
        o_ref[...] = acc

    return pl.pallas_call(body, name=name, out_shape=jax.ShapeDtypeStruct((r, c), F32))(x)


def _ada_fwd(c_all, w_ada):
    nb = c_all.shape[0]
    cols = w_ada.shape[2]

    def body(c_ref, w_ref, act_ref, mod_ref):
        cv = c_ref[...]
        act = cv * _sigmoid(cv)
        act_ref[...] = act
        for l in range(DEPTH):
            mod_ref[l] = jnp.dot(act, w_ref[l], precision=HI, preferred_element_type=F32)

    return pl.pallas_call(
        body, name="ada_fwd",
        out_shape=[jax.ShapeDtypeStruct((nb, D_MODEL), F32), jax.ShapeDtypeStruct((DEPTH, nb, cols), F32)],
        compiler_params=_cparams(),
    )(c_all, w_ada)


def _ada_bwd(c_act, dmod_all, dmod_mine, dep):
    nb = c_act.shape[0]
    cols = dmod_mine.shape[2]

    def body(act_ref, dall_ref, dmine_ref, dep_ref, gw_ref, gb_ref):
        act = act_ref[...]
        for l in range(DEPTH):
            gw_ref[l] = lax.dot_general(act, dmine_ref[l], (((0,), (0,)), ((), ())),
                                        precision=HI, preferred_element_type=F32)
            gb_ref[l:l + 1, :] = jnp.sum(dall_ref[l], axis=0, keepdims=True)

    return pl.pallas_call(
        body, name="ada_bwd",
        out_shape=[jax.ShapeDtypeStruct((DEPTH, D_MODEL, cols), F32),
                   jax.ShapeDtypeStruct((DEPTH, 3 * D_MODEL), F32)],
        compiler_params=_cparams(),
    )(c_act, dmod_all, dmod_mine, dep)


def _ln_in(x, ss, g, w, seq, tm=ROW_TILE, dep=None):
    t = x.shape[0]
    tm = min(tm, seq)
    tps = seq // tm

    def body(x_ref, ss_ref, g_ref, w_ref, h_ref, *outs):
        xv = x_ref[...]
        xn = xv * lax.rsqrt(jnp.mean(xv * xv, axis=-1, keepdims=True) + EPS)
        h = xn * g_ref[...] * ss_ref[0, 1:2, :] + ss_ref[0, 0:1, :]
        hb = h.astype(BF16)
        h_ref[...] = hb
        z = jnp.dot(hb, w_ref[...], preferred_element_type=F32)
        for o_ref, (_, off, wd, _) in zip(outs, Z_SEGS):
            o_ref[...] = z[:, off:off + wd].astype(o_ref.dtype)

    row = lambda wd: pl.BlockSpec((tm, wd), lambda i: (i, 0))
    in_specs = [row(D_MODEL), pl.BlockSpec((1, 2, D_MODEL), lambda i: (i // tps, 0, 0)),
                pl.BlockSpec((1, D_MODEL), lambda i: (0, 0)), pl.BlockSpec((D_MODEL, N_PAD), lambda i: (0, 0))]
    body, in_specs, args = _after(dep, body, in_specs, [x, ss, g, w])
    return pl.pallas_call(
        body, name="ln_in", grid=(t // tm,), in_specs=in_specs,
        out_specs=[row(D_MODEL)] + [row(wd) for _, _, wd, _ in Z_SEGS],
        out_shape=[jax.ShapeDtypeStruct((t, D_MODEL), BF16)]
        + [jax.ShapeDtypeStruct((t, wd), dt) for _, _, wd, dt in Z_SEGS],
        compiler_params=_cparams(("arbitrary",)),
    )(*args)


def _ln_in_bwd(dz, w_t, x, ss, g, dxo, seq, tm=ROW_TILE, dep=None):
    t = x.shape[0]
    tm = min(tm, seq)
    tps = seq // tm
    nb = t // seq
    nz = len(Z_SEGS)

    def body(*refs):
        dz_refs = refs[:nz]
        wt_ref, x_ref, ss_ref, g_ref, dxo_ref, dx_ref, dss_ref, dg_ref = refs[nz:]
        i = pl.program_id(0)
        dzc = jnp.concatenate([r[...].astype(BF16) for r in dz_refs], axis=1)
        dh = jnp.dot(dzc, wt_ref[...], preferred_element_type=F32)
        xv = x_ref[...]
        rstd = lax.rsqrt(jnp.mean(xv * xv, axis=-1, keepdims=True) + EPS)
        xn = xv * rstd
        gv = g_ref[...]
        s1 = ss_ref[0, 1:2, :]
        dxg = dh * s1
        dxn = dxg * gv
        dx = rstd * (dxn - xn * jnp.mean(dxn * xn, axis=-1, keepdims=True))
        dx_ref[...] = dxo_ref[...] + dx
        dshift = jnp.sum(dh, axis=0, keepdims=True)
        dscale = jnp.sum(dh * (xn * gv), axis=0, keepdims=True)
        dgp = jnp.sum(dxg * xn, axis=0, keepdims=True)

        @pl.when(i % tps == 0)
        def _():
            dss_ref[0, 0:1, :] = dshift
            dss_ref[0, 1:2, :] = dscale

        @pl.when(i % tps != 0)
        def _():
            dss_ref[0, 0:1, :] += dshift
            dss_ref[0, 1:2, :] += dscale

        @pl.when(i == 0)
        def _():
            dg_ref[...] = dgp

        @pl.when(i != 0)
        def _():
            dg_ref[...] += dgp

    row = lambda wd: pl.BlockSpec((tm, wd), lambda i: (i, 0))
    in_specs = ([row(wd) for _, _, wd, _ in Z_SEGS]
                + [pl.BlockSpec((N_PAD, D_MODEL), lambda i: (0, 0)), row(D_MODEL),
                   pl.BlockSpec((1, 2, D_MODEL), lambda i: (i // tps, 0, 0)),
                   pl.BlockSpec((1, D_MODEL), lambda i: (0, 0)), row(D_MODEL)])
    body, in_specs, args = _after(dep, body, in_specs, [*dz, w_t, x, ss, g, dxo])
    return pl.pallas_call(
        body, name="ln_in_bwd", grid=(t // tm,), in_specs=in_specs,
        out_specs=[row(D_MODEL), pl.BlockSpec((1, 2, D_MODEL), lambda i: (i // tps, 0, 0)),
                   pl.BlockSpec((1, D_MODEL), lambda i: (0, 0))],
        out_shape=[jax.ShapeDtypeStruct((t, D_MODEL), F32), jax.ShapeDtypeStruct((nb, 2, D_MODEL), F32),
                   jax.ShapeDtypeStruct((1, D_MODEL), F32)],
        compiler_params=_cparams(("arbitrary",)),
    )(*args)


def _matmul_tn(a, bs, name, tm=1024, dep=None):
    bs = list(bs) if isinstance(bs, (list, tuple)) else [bs]
    t, k = a.shape
    widths = [b.shape[1] for b in bs]
    n = sum(widths)
    tm = min(tm, t)

    def body(a_ref, *refs):
        b_refs, o_ref = refs[:-1], refs[-1]
        i = pl.program_id(0)
        av = a_ref[...].astype(BF16)
        parts = [b_ref[...].astype(BF16) for b_ref in b_refs]
        bv = parts[0] if len(parts) == 1 else jnp.concatenate(parts, axis=1)
        part = lax.dot_general(av, bv, (((0,), (0,)), ((), ())), preferred_element_type=F32)

        @pl.when(i == 0)
        def _():
            o_ref[...] = part

        @pl.when(i != 0)
        def _():
            o_ref[...] += part

    in_specs = [pl.BlockSpec((tm, k), lambda i: (i, 0))] + [pl.BlockSpec((tm, wd), lambda i: (i, 0)) for wd in widths]
    body, in_specs, args = _after(dep, body, in_specs, [a, *bs])
    return pl.pallas_call(
        body, name=name, grid=(t // tm,), in_specs=in_specs,
        out_specs=pl.BlockSpec((k, n), lambda i: (0, 0)),
        out_shape=jax.ShapeDtypeStruct((k, n), F32),
        compiler_params=_cparams(("arbitrary",)),
    )(*args)


def _rope(blk, cos_t, sin_a, sin_b):
    return blk * cos_t + pltpu.roll(blk, 112, 1) * sin_a + pltpu.roll(blk, 16, 1) * sin_b


def _unrope(d, cos_t, sin_a, sin_b):
    return d * cos_t + pltpu.roll(d * sin_a, 16, 1) + pltpu.roll(d * sin_b, 112, 1)


def _mla_prep(cq, ckv, kpe, gq, gkv, wuq, wk, wv, cos_t, sin_a, sin_b, tm=ROW_TILE):
    t = cq.shape[0]
    tm = min(tm, t)
    qw = A_HEADS * HEAD_PAD

    def body(cq_ref, ckv_ref, kpe_ref, gq_ref, gkv_ref, wuq_ref, wk_ref, wv_ref, c_ref, sa_ref, sb_ref,
             q_ref, k_ref, v_ref, cqn_ref, ckvn_ref):
        ct, sa, sb = c_ref[...], sa_ref[...], sb_ref[...]
        a = cq_ref[...]
        cqn = (a * lax.rsqrt(jnp.mean(a * a, axis=-1, keepdims=True) + EPS) * gq_ref[...]).astype(BF16)
        cqn_ref[...] = cqn
        b = ckv_ref[...]
        ckvn = (b * lax.rsqrt(jnp.mean(b * b, axis=-1, keepdims=True) + EPS) * gkv_ref[...]).astype(BF16)
        ckvn_ref[...] = ckvn
        qlin = jnp.dot(cqn, wuq_ref[...], preferred_element_type=F32)
        klin = jnp.dot(ckvn, wk_ref[...], preferred_element_type=F32)
        v_ref[...] = jnp.dot(ckvn, wv_ref[...], preferred_element_type=F32).astype(BF16)
        kr = _rope(kpe_ref[...], ct, sa, sb)
        for h in range(A_HEADS):
            sl = slice(h * HEAD_PAD, (h + 1) * HEAD_PAD)
            q_ref[:, sl] = _rope(qlin[:, sl], ct, sa, sb).astype(BF16)
            k_ref[:, sl] = (klin[:, sl] + kr).astype(BF16)

    row = lambda wd: pl.BlockSpec((tm, wd), lambda i: (i, 0))
    full = lambda r, c: pl.BlockSpec((r, c), lambda i: (0, 0))
    return pl.pallas_call(
        body, name="mla_prep", grid=(t // tm,),
        in_specs=[row(A_Q_RANK), row(A_KV_RANK), row(128), full(1, A_Q_RANK), full(1, A_KV_RANK),
                  full(A_Q_RANK, qw), full(A_KV_RANK, qw), full(A_KV_RANK, GW), row(128), row(128), row(128)],
        out_specs=[row(qw), row(qw), row(GW), row(A_Q_RANK), row(A_KV_RANK)],
        out_shape=[jax.ShapeDtypeStruct((t, qw), BF16), jax.ShapeDtypeStruct((t, qw), BF16),
                   jax.ShapeDtypeStruct((t, GW), BF16), jax.ShapeDtypeStruct((t, A_Q_RANK), BF16),
                   jax.ShapeDtypeStruct((t, A_KV_RANK), BF16)],
        compiler_params=_cparams(("arbitrary",)),
    )(cq, ckv, kpe, gq, gkv, wuq, wk, wv, cos_t, sin_a, sin_b)


def _mla_prep_bwd(dq, dk, dv, cq, ckv, gq, gkv, wuq_t, wk_t, wv_t, cos_t, sin_a, sin_b, tm=ROW_TILE):
    t = cq.shape[0]
    tm = min(tm, t)
    qw = A_HEADS * HEAD_PAD

    def body(dq_ref, dk_ref, dv_ref, cq_ref, ckv_ref, gq_ref, gkv_ref, wuqt_ref, wkt_ref, wvt_ref,
             c_ref, sa_ref, sb_ref, dcq_ref, dckv_ref, dkpe_ref, dql_ref, dkl_ref, dgq_ref, dgkv_ref):
        i = pl.program_id(0)
        ct, sa, sb = c_ref[...], sa_ref[...], sb_ref[...]
        lane = lax.broadcasted_iota(jnp.int32, (1, HEAD_PAD), 1)
        nope = lane < A_NOPE
        rope = (lane >= A_NOPE) & (lane < A_NOPE + A_ROPE)
        dksum = None
        for h in range(A_HEADS):
            sl = slice(h * HEAD_PAD, (h + 1) * HEAD_PAD)
            dql_ref[:, sl] = _unrope(dq_ref[:, sl], ct, sa, sb).astype(BF16)
            dkh = dk_ref[:, sl]
            dkl_ref[:, sl] = jnp.where(nope, dkh, 0.0).astype(BF16)
            dksum = dkh if dksum is None else dksum + dkh
        dkpe_ref[...] = jnp.where(rope, _unrope(jnp.where(rope, dksum, 0.0), ct, sa, sb), 0.0).astype(BF16)
        dcqn = jnp.dot(dql_ref[...], wuqt_ref[...], preferred_element_type=F32)
        dckvn = (jnp.dot(dkl_ref[...], wkt_ref[...], preferred_element_type=F32)
                 + jnp.dot(dv_ref[...].astype(BF16), wvt_ref[...], preferred_element_type=F32))

        def norm_bwd(xv, gv, dy):
            rstd = lax.rsqrt(jnp.mean(xv * xv, axis=-1, keepdims=True) + EPS)
            xn = xv * rstd
            dxn = dy * gv
            dx = rstd * (dxn - xn * jnp.mean(dxn * xn, axis=-1, keepdims=True))
            return dx, jnp.sum(dy * xn, axis=0, keepdims=True)

        dcq, dgq = norm_bwd(cq_ref[...], gq_ref[...], dcqn)
        dckv, dgkv = norm_bwd(ckv_ref[...], gkv_ref[...], dckvn)
        dcq_ref[...] = dcq.astype(BF16)
        dckv_ref[...] = dckv.astype(BF16)

        @pl.when(i == 0)
        def _():
            dgq_ref[...] = dgq
            dgkv_ref[...] = dgkv

        @pl.when(i != 0)
        def _():
            dgq_ref[...] += dgq
            dgkv_ref[...] += dgkv

    row = lambda wd: pl.BlockSpec((tm, wd), lambda i: (i, 0))
    full = lambda r, c: pl.BlockSpec((r, c), lambda i: (0, 0))
    return pl.pallas_call(
        body, name="mla_prep_bwd", grid=(t // tm,),
        in_specs=[row(qw), row(qw), row(GW), row(A_Q_RANK), row(A_KV_RANK), full(1, A_Q_RANK), full(1, A_KV_RANK),
                  full(qw, A_Q_RANK), full(qw, A_KV_RANK), full(GW, A_KV_RANK), row(128), row(128), row(128)],
        out_specs=[row(A_Q_RANK), row(A_KV_RANK), row(128), row(qw), row(qw), full(1, A_Q_RANK), full(1, A_KV_RANK)],
        out_shape=[jax.ShapeDtypeStruct((t, A_Q_RANK), BF16), jax.ShapeDtypeStruct((t, A_KV_RANK), BF16),
                   jax.ShapeDtypeStruct((t, 128), BF16), jax.ShapeDtypeStruct((t, qw), BF16),
                   jax.ShapeDtypeStruct((t, qw), BF16), jax.ShapeDtypeStruct((1, A_Q_RANK), F32),
                   jax.ShapeDtypeStruct((1, A_KV_RANK), F32)],
        compiler_params=_cparams(("arbitrary",)),
    )(dq, dk, dv, cq, ckv, gq, gkv, wuq_t, wk_t, wv_t, cos_t, sin_a, sin_b)


def _nt(a, b):
    return lax.dot_general(a, b, (((1,), (1,)), ((), ())), preferred_element_type=F32)


def _tn(a, b):
    return lax.dot_general(a, b, (((0,), (0,)), ((), ())), preferred_element_type=F32)


def _causal_mask(kind, q0, k0, tq, tk):
    qpos = q0 + lax.broadcasted_iota(jnp.int32, (tq, tk), 0)
    kpos = k0 + lax.broadcasted_iota(jnp.int32, (tq, tk), 1)
    if kind == "mla":
        return lax.shift_right_logical(kpos, 6) <= lax.shift_right_logical(qpos, 6)
    return kpos <= qpos


def _attn_fwd(kind, q, k, v, f, seq, scale, tq=512, tk=512):
    t = v.shape[0]
    nb = t // seq
    nq = seq // tq
    hw = 256 if kind == "mla" else 128
    n_heads = A_HEADS if kind == "mla" else C_HEADS
    use_f = f is not None
    tq, tk = min(tq, seq), min(tk, seq)
    nq = seq // tq
    assert tk == tq

    def body(*refs):
        if use_f:
            q_ref, k_ref, v_ref, f_ref, o_ref, st_ref = refs
        else:
            q_ref, k_ref, v_ref, o_ref, st_ref = refs
        qi = pl.program_id(2)
        q0 = qi * tq
        lane = lax.broadcasted_iota(jnp.int32, (1, 128), 1)
        half = lane >= 64
        qall = q_ref[...]
        if kind == "mla":
            qhs = [qall[:, 0:128], qall[:, 128:256]]
            post = scale
        else:
            assert math.frexp(scale)[0] == 0.5
            qall = qall * jnp.asarray(scale, BF16)
            qhs = [jnp.where(half, jnp.zeros_like(qall), qall), jnp.where(half, qall, jnp.zeros_like(qall))]
            post = None
        kd = pl.multiple_of(q0, tq)
        diag = _causal_mask(kind, 0, 0, tq, tk)

        def block(j, k0, state, masked):
            m, l, acc = state
            kh = k_ref[pl.ds(k0, tk), j * 128:(j + 1) * 128] if kind == "mla" else k_ref[pl.ds(k0, tk), :]
            s = _nt(qhs[j], kh)
            if post is not None:
                s = s * post
            if use_f:
                s = s - f_ref[0, 0, j:j + 1, pl.ds(k0, tk)]
            if masked:
                s = jnp.where(diag, s, NEG)
            mn = jnp.maximum(m, jnp.max(s, axis=-1, keepdims=True))
            alpha = jnp.exp(m - mn)
            p = jnp.exp(s - mn)
            l = alpha * l + jnp.sum(p, axis=-1, keepdims=True)
            acc = alpha * acc + jnp.dot(p.astype(BF16), v_ref[pl.ds(k0, tk), :], preferred_element_type=F32)
            return mn, l, acc

        def run(heads):
            def kstep(kb, carry):
                k0 = pl.multiple_of(kb * tk, tk)
                out = ()
                for n, j in enumerate(heads):
                    out += block(j, k0, carry[3 * n:3 * n + 3], False)
                return out

            init = (jnp.full((tq, 1), NEG, F32), jnp.zeros((tq, 1), F32), jnp.zeros((tq, 128), F32)) * len(heads)
            carry = lax.fori_loop(0, qi, kstep, init)
            o, st = jnp.zeros((tq, 128), F32), jnp.zeros((tq, 128), F32)
            for n, j in enumerate(heads):
                m, l, acc = block(j, kd, carry[3 * n:3 * n + 3], True)
                o = jnp.where(half == bool(j), acc / l, o)
                st = jnp.where(lane == j, m + jnp.log(l), st)
            o_ref[...] = o
            st_ref[...] = st

        if n_heads % 2 == 0:
            run((0, 1))
        else:
            last = pl.program_id(1) == n_heads // 2
            pl.when(jnp.logical_not(last))(lambda: run((0, 1)))
            pl.when(last)(lambda: run((0,)))

    in_specs = [pl.BlockSpec((tq, hw), lambda b, p, i: (b * nq + i, p)),
                pl.BlockSpec((seq, hw), lambda b, p, i: (b, p)),
                pl.BlockSpec((seq, 128), lambda b, p, i: (b, p))]
    args = [q, k, v]
    if use_f:
        in_specs.append(pl.BlockSpec((1, 1, 8, seq), lambda b, p, i: (b, p, 0, 0)))
        args.append(f)
    oblk = pl.BlockSpec((tq, 128), lambda b, p, i: (b * nq + i, p))
    return pl.pallas_call(
        body, name="attn_fwd_" + kind, grid=(nb, 3, nq), in_specs=in_specs, out_specs=[oblk, oblk],
        out_shape=[jax.ShapeDtypeStruct((t, GW), F32), jax.ShapeDtypeStruct((t, GW), F32)],
        compiler_params=_cparams(("arbitrary", "arbitrary", "arbitrary")),
    )(*args)


def _attn_bwd(kind, q, k, v, f, o, st, do, seq, scale, tq=512, tk=512, dep=None):
    t = v.shape[0]
    nb = t // seq
    tq, tk = min(tq, seq), min(tk, seq)
    nq = seq // tq
    nk = seq // tk
    hw = 256 if kind == "mla" else 128
    n_heads = A_HEADS if kind == "mla" else C_HEADS
    use_f = f is not None
    assert tq == tk

    def body(*refs):
        if use_f:
            q_ref, k_ref, v_ref, f_ref, o_ref, st_ref, do_ref, dq_ref, dk_ref, dv_ref, df_ref, dfq_ref = refs
        else:
            q_ref, k_ref, v_ref, o_ref, st_ref, do_ref, dq_ref, dk_ref, dv_ref = refs
        kj = pl.program_id(2)
        k0 = kj * tk
        lane = lax.broadcasted_iota(jnp.int32, (1, 128), 1)
        half = lane >= 64

        @pl.when(kj == 0)
        def _():
            dq_ref[...] = jnp.zeros_like(dq_ref)
            if use_f:
                dfq_ref[...] = jnp.zeros_like(dfq_ref)

        dk_ref[...] = jnp.zeros_like(dk_ref)
        dv_ref[...] = jnp.zeros_like(dv_ref)
        if use_f:
            df_ref[...] = jnp.zeros_like(df_ref)
        vv = v_ref[...]
        diag = _causal_mask(kind, 0, 0, tq, tk)

        def qstep(qi, masked):
            q0 = pl.multiple_of(qi * tq, tq)
            rows = pl.ds(q0, tq)
            dov = do_ref[rows, :]
            dd = dov * o_ref[rows, :]
            stv = st_ref[rows, :]

            def one_head(j):
                hm = half == bool(j)
                delta = jnp.sum(jnp.where(hm, dd, 0.0), axis=-1, keepdims=True)
                lse = stv[:, j:j + 1]
                if kind == "mla":
                    cols = slice(j * 128, (j + 1) * 128)
                    qh = q_ref[rows, cols]
                    kh = k_ref[:, cols]
                else:
                    cols = slice(0, 128)
                    qa = q_ref[rows, :]
                    qh = jnp.where(hm, qa, jnp.zeros_like(qa))
                    kh = k_ref[...]
                s = _nt(qh, kh) * scale
                if use_f:
                    s = s - f_ref[0, 0, j:j + 1, :]
                if masked:
                    s = jnp.where(diag, s, NEG)
                p = jnp.exp(s - lse)
                doh = jnp.where(hm, dov, 0.0).astype(BF16)
                ds = p * (_nt(doh, vv) - delta)
                dsb = (ds * scale).astype(BF16)
                dv_ref[...] += _tn(p.astype(BF16), doh)
                dk_ref[:, cols] += _tn(dsb, qh)
                dqc = jnp.dot(dsb, kh, preferred_element_type=F32)
                if kind != "mla":
                    dqc = jnp.where(hm, dqc, 0.0)
                dq_ref[rows, cols] += dqc
                if use_f:
                    df_ref[0, 0, j:j + 1, :] += -jnp.sum(ds, axis=0, keepdims=True)
                    dfq_ref[rows, :] += jnp.where(lane == j, jnp.sum(ds, axis=-1, keepdims=True), 0.0)

            def both():
                one_head(0)
                one_head(1)

            if n_heads % 2 == 0:
                both()
            else:
                last = pl.program_id(1) == n_heads // 2
                pl.when(jnp.logical_not(last))(both)
                pl.when(last)(lambda: one_head(0))

        qstep(kj, True)

        def rest(qi, carry):
            qstep(qi, False)
            return carry

        lax.fori_loop(kj + 1, nq, rest, 0)

    full_q = lambda wd: pl.BlockSpec((seq, wd), lambda b, p, i: (b, p))
    kblk = lambda wd: pl.BlockSpec((tk, wd), lambda b, p, i: (b * nk + i, p))
    in_specs = [full_q(hw), kblk(hw), kblk(128)]
    args = [q, k, v]
    if use_f:
        in_specs.append(pl.BlockSpec((1, 1, 8, tk), lambda b, p, i: (b, p, 0, i)))
        args.append(f)
    in_specs += [full_q(128), full_q(128), full_q(128)]
    args += [o, st, do]
    out_specs = [full_q(hw), kblk(hw), kblk(128)]
    out_shape = [jax.ShapeDtypeStruct((t, 3 * hw), F32), jax.ShapeDtypeStruct((t, 3 * hw), F32),
                 jax.ShapeDtypeStruct((t, GW), F32)]
    if use_f:
        out_specs += [pl.BlockSpec((1, 1, 8, tk), lambda b, p, i: (b, p, 0, i)), full_q(128)]
        out_shape += [jax.ShapeDtypeStruct((nb, 3, 8, seq), F32), jax.ShapeDtypeStruct((t, GW), F32)]
    body, in_specs, args = _after(dep, body, in_specs, args)
    return pl.pallas_call(
        body, name="attn_bwd_" + kind, grid=(nb, 3, nk), in_specs=in_specs, out_specs=out_specs,
        out_shape=out_shape, compiler_params=_cparams(("arbitrary", "arbitrary", "arbitrary")),
    )(*args)


BQ = 256
BWIN = BQ + B_LEFT


def _band_geometry():
    r = lax.broadcasted_iota(jnp.int32, (BQ, BWIN), 0)
    j = lax.broadcasted_iota(jnp.int32, (BQ, BWIN), 1)
    rc = lax.shift_right_logical(r, 6)
    jc = lax.shift_right_logical(j, 6)
    allowed = (jc - 8 <= rc) & (rc <= jc)
    return (r + B_LEFT - j) >= REL_CLIP, allowed, j < r


def _band_onehot(transposed, offset=0):
    shape = (BWIN, GW) if transposed else (GW, BWIN)
    kk = lax.broadcasted_iota(jnp.int32, shape, 1 if transposed else 0)
    x = lax.broadcasted_iota(jnp.int32, shape, 0 if transposed else 1) - offset
    x = jnp.where(x < 0, x + BWIN, x)
    return (kk == jnp.clip(B_LEFT - x, -REL_CLIP, REL_CLIP) + REL_CLIP).astype(F32)


def _band_table(rel_bias8):
    def body(b_ref, o_ref):
        hh = pl.program_id(0)
        u8 = jnp.dot(b_ref[...], _band_onehot(False), precision=HI, preferred_element_type=F32)
        rid = lax.broadcasted_iota(jnp.int32, (8, BWIN), 0)
        row = jnp.sum(jnp.where(rid == hh, u8, 0.0), axis=0, keepdims=True)
        far, allowed, _ = _band_geometry()
        tbl = pltpu.roll(jnp.broadcast_to(row, (BQ, BWIN)), 0, 1, stride=1, stride_axis=0)
        tbl = jnp.where(far, row[:, 0:1], tbl)
        o_ref[0] = jnp.where(allowed, tbl, NEG)

    return pl.pallas_call(
        body, name="band_table", grid=(6,),
        in_specs=[pl.BlockSpec((8, GW), lambda h: (0, 0))],
        out_specs=pl.BlockSpec((1, BQ, BWIN), lambda h: (h, 0, 0)),
        out_shape=jax.ShapeDtypeStruct((6, BQ, BWIN), F32),
        compiler_params=_cparams(("arbitrary",)),
    )(rel_bias8)


def _band_table_bwd(gtab):
    def body(g_ref, o_ref):
        gv = g_ref[0]
        _, _, wrapped = _band_geometry()
        gfar = jnp.sum(jnp.sum(jnp.where(wrapped, gv, 0.0), axis=-1, keepdims=True), axis=0, keepdims=True)
        anti = (lax.broadcasted_iota(jnp.int32, (BQ, BQ), 0) + lax.broadcasted_iota(jnp.int32, (BQ, BQ), 1)
                == BQ - 1).astype(F32)
        grev = jnp.dot(anti, jnp.where(wrapped, 0.0, gv), precision=HI, preferred_element_type=F32)
        near = pltpu.roll(grev, 0, 1, stride=1, stride_axis=0)
        y = jnp.broadcast_to(jnp.sum(near, axis=0, keepdims=True), (8, BWIN))
        gb = jnp.dot(y, _band_onehot(True, BQ - 1), precision=HI, preferred_element_type=F32)
        lane = lax.broadcasted_iota(jnp.int32, (8, GW), 1)
        o_ref[0] = gb + jnp.where(lane == 2 * REL_CLIP, gfar, 0.0)

    return pl.pallas_call(
        body, name="band_table_bwd", grid=(B_HEADS,),
        in_specs=[pl.BlockSpec((1, BQ, BWIN), lambda h: (h, 0, 0))],
        out_specs=pl.BlockSpec((1, 8, GW), lambda h: (h, 0, 0)),
        out_shape=jax.ShapeDtypeStruct((B_HEADS, 8, GW), F32),
        compiler_params=_cparams(("arbitrary",)),
    )(gtab)


def _band_fwd(q, k, v, table, seq, scale):
    t = q.shape[0]
    nb = t // seq
    nq = seq // BQ

    def body(q_ref, k_ref, v_ref, tb_ref, o_ref, st_ref, kpad, vpad):
        qi = pl.program_id(2)
        q0 = pl.multiple_of(qi * BQ, BQ)
        lane = lax.broadcasted_iota(jnp.int32, (1, 128), 1)
        half = lane >= 64

        @pl.when(qi == 0)
        def _():
            kpad[0:B_LEFT, :] = jnp.zeros((B_LEFT, 128), BF16)
            vpad[0:B_LEFT, :] = jnp.zeros((B_LEFT, 128), BF16)
            kpad[B_LEFT:, :] = k_ref[...]
            vpad[B_LEFT:, :] = v_ref[...]

        kw = kpad[pl.ds(q0, BWIN), :]
        vw = vpad[pl.ds(q0, BWIN), :]
        inside = lax.broadcasted_iota(jnp.int32, (BQ, BWIN), 1) >= B_LEFT - q0
        assert math.frexp(scale)[0] == 0.5
        qall = q_ref[...] * jnp.asarray(scale, BF16)

        def run(heads):
            o, st = jnp.zeros((BQ, 128), F32), jnp.zeros((BQ, 128), F32)
            for j in heads:
                qh = jnp.where(half == bool(j), qall, jnp.zeros_like(qall))
                s = jnp.where(inside, _nt(qh, kw) + tb_ref[j], NEG)
                m = jnp.max(s, axis=-1, keepdims=True)
                p = jnp.exp(s - m)
                l = jnp.sum(p, axis=-1, keepdims=True)
                o = jnp.where(half == bool(j), jnp.dot(p.astype(BF16), vw, preferred_element_type=F32) / l, o)
                st = jnp.where(lane == j, m + jnp.log(l), st)
            o_ref[...] = o
            st_ref[...] = st

        last = pl.program_id(1) == B_HEADS // 2
        pl.when(jnp.logical_not(last))(lambda: run((0, 1)))
        pl.when(last)(lambda: run((0,)))

    qblk = pl.BlockSpec((BQ, 128), lambda b, p, i: (b * nq + i, p))
    full = pl.BlockSpec((seq, 128), lambda b, p, i: (b, p))
    return pl.pallas_call(
        body, name="band_fwd", grid=(nb, 3, nq),
        in_specs=[qblk, full, full, pl.BlockSpec((2, BQ, BWIN), lambda b, p, i: (p, 0, 0))],
        out_specs=[qblk, qblk],
        out_shape=[jax.ShapeDtypeStruct((t, GW), F32), jax.ShapeDtypeStruct((t, GW), F32)],
        scratch_shapes=[pltpu.VMEM((seq + B_LEFT, 128), BF16), pltpu.VMEM((seq + B_LEFT, 128), BF16)],
        compiler_params=_cparams(("arbitrary", "arbitrary", "arbitrary")),
    )(q, k, v, table)


def _band_bwd(q, k, v, table, o, st, do, seq, scale, dep=None):
    t = q.shape[0]
    nb = t // seq
    nq = seq // BQ

    def body(q_ref, k_ref, v_ref, tb_ref, o_ref, st_ref, do_ref, dq_ref, dk_ref, dv_ref, g_ref,
             kpad, vpad, dkpad, dvpad):
        b = pl.program_id(1)
        qi = pl.program_id(2)
        q0 = pl.multiple_of(qi * BQ, BQ)
        lane = lax.broadcasted_iota(jnp.int32, (1, 128), 1)
        half = lane >= 64

        @pl.when(qi == 0)
        def _():
            kpad[0:B_LEFT, :] = jnp.zeros((B_LEFT, 128), BF16)
            vpad[0:B_LEFT, :] = jnp.zeros((B_LEFT, 128), BF16)
            kpad[B_LEFT:, :] = k_ref[...]
            vpad[B_LEFT:, :] = v_ref[...]
            dkpad[...] = jnp.zeros_like(dkpad)
            dvpad[...] = jnp.zeros_like(dvpad)

        @pl.when((qi == 0) & (b == 0))
        def _():
            g_ref[...] = jnp.zeros_like(g_ref)

        win = pl.ds(q0, BWIN)
        kw = kpad[win, :]
        vw = vpad[win, :]
        inside = lax.broadcasted_iota(jnp.int32, (BQ, BWIN), 1) >= B_LEFT - q0
        qall = q_ref[...]
        dov = do_ref[...]
        dd = dov * o_ref[...]
        stv = st_ref[...]

        def run(heads):
            dq = jnp.zeros((BQ, 128), F32)
            for j in heads:
                hm = half == bool(j)
                qh = jnp.where(hm, qall, jnp.zeros_like(qall))
                delta = jnp.sum(jnp.where(hm, dd, 0.0), axis=-1, keepdims=True)
                s = jnp.where(inside, _nt(qh, kw) * scale + tb_ref[j], NEG)
                p = jnp.exp(s - stv[:, j:j + 1])
                doh = jnp.where(hm, dov, 0.0).astype(BF16)
                ds = p * (_nt(doh, vw) - delta)
                g_ref[j] += ds
                dsb = (ds * scale).astype(BF16)
                dvpad[win, :] += _tn(p.astype(BF16), doh)
                dkpad[win, :] += _tn(dsb, qh)
                dq = dq + jnp.where(hm, jnp.dot(dsb, kw, preferred_element_type=F32), 0.0)
            dq_ref[...] = dq.astype(BF16)

        last = pl.program_id(0) == B_HEADS // 2
        pl.when(jnp.logical_not(last))(lambda: run((0, 1)))
        pl.when(last)(lambda: run((0,)))

        @pl.when(qi == nq - 1)
        def _():
            dk_ref[...] = dkpad[B_LEFT:, :].astype(BF16)
            dv_ref[...] = dvpad[B_LEFT:, :].astype(BF16)

    qblk = pl.BlockSpec((BQ, 128), lambda p, b, i: (b * nq + i, p))
    full = pl.BlockSpec((seq, 128), lambda p, b, i: (b, p))
    tblk = pl.BlockSpec((2, BQ, BWIN), lambda p, b, i: (p, 0, 0))
    body, in_specs, args = _after(dep, body, [qblk, full, full, tblk, qblk, qblk, qblk], [q, k, v, table, o, st, do])
    return pl.pallas_call(
        body, name="band_bwd", grid=(3, nb, nq),
        in_specs=in_specs,
        out_specs=[qblk, full, full, tblk],
        out_shape=[jax.ShapeDtypeStruct((t, GW), BF16), jax.ShapeDtypeStruct((t, GW), BF16),
                   jax.ShapeDtypeStruct((t, GW), BF16), jax.ShapeDtypeStruct((6, BQ, BWIN), F32)],
        scratch_shapes=[pltpu.VMEM((seq + B_LEFT, 128), BF16), pltpu.VMEM((seq + B_LEFT, 128), BF16),
                        pltpu.VMEM((seq + B_LEFT, 128), F32), pltpu.VMEM((seq + B_LEFT, 128), F32)],
        compiler_params=_cparams(("arbitrary", "arbitrary", "arbitrary")),
    )(*args)


def _fox_prep(cf, fb, seq):
    nb = cf.shape[0] // seq
    nblk = seq // 128

    def body(cf_ref, fb_ref, f_ref):
        x = cf_ref[...] + fb_ref[...]
        lf = jnp.minimum(x, 0.0) - jnp.log1p(jnp.exp(-jnp.abs(x)))
        rows = lf.T[0:8, :]
        upper = (lax.broadcasted_iota(jnp.int32, (128, 128), 0)
                 <= lax.broadcasted_iota(jnp.int32, (128, 128), 1)).astype(F32)
        carry = jnp.zeros((8, 1), F32)
        for blk in range(nblk):
            sl = slice(blk * 128, (blk + 1) * 128)
            cs = jnp.dot(rows[:, sl], upper, precision=HI, preferred_element_type=F32) + carry
            carry = cs[:, 127:128]
            f_ref[0, 0, :, sl] = cs
            f_ref[0, 1, :, sl] = pltpu.roll(cs, 6, 0)
            f_ref[0, 2, :, sl] = pltpu.roll(cs, 4, 0)

    return pl.pallas_call(
        body, name="fox_prep", grid=(nb,),
        in_specs=[pl.BlockSpec((seq, 128), lambda b: (b, 0)), pl.BlockSpec((1, 128), lambda b: (0, 0))],
        out_specs=pl.BlockSpec((1, 3, 8, seq), lambda b: (b, 0, 0, 0)),
        out_shape=jax.ShapeDtypeStruct((nb, 3, 8, seq), F32),
        compiler_params=_cparams(("arbitrary",)),
    )(cf, fb)


def _fox_prep_bwd(df, dfq, cf, fb, seq):
    nb = cf.shape[0] // seq
    nblk = seq // 128

    def body(df_ref, dfq_ref, cf_ref, fb_ref, dcf_ref, dfb_ref, wide):
        b = pl.program_id(0)
        row = lax.broadcasted_iota(jnp.int32, (8, seq), 0)
        dfh = None
        for p in range(3):
            both = df_ref[0, p] + dfq_ref[:, p * 128:(p + 1) * 128].T[0:8, :]
            both = jnp.where(row < 2, both, 0.0)
            if p:
                both = pltpu.roll(both, 2 * p, 0)
            dfh = both if dfh is None else dfh + both
        lower = (lax.broadcasted_iota(jnp.int32, (128, 128), 0)
                 >= lax.broadcasted_iota(jnp.int32, (128, 128), 1)).astype(F32)
        wide[...] = jnp.zeros_like(wide)
        carry = jnp.zeros((8, 1), F32)
        for blk in reversed(range(nblk)):
            sl = slice(blk * 128, (blk + 1) * 128)
            rc = jnp.dot(dfh[:, sl], lower, precision=HI, preferred_element_type=F32) + carry
            carry = rc[:, 0:1]
            wide[0:8, sl] = rc
        dl = wide[...].T
        x = cf_ref[...] + fb_ref[...]
        dcf = dl * (1.0 / (1.0 + jnp.exp(x)))
        dcf_ref[...] = dcf.astype(BF16)
        part = jnp.sum(dcf, axis=0, keepdims=True)

        @pl.when(b == 0)
        def _():
            dfb_ref[...] = part

        @pl.when(b != 0)
        def _():
            dfb_ref[...] += part

    return pl.pallas_call(
        body, name="fox_prep_bwd", grid=(nb,),
        in_specs=[pl.BlockSpec((1, 3, 8, seq), lambda b: (b, 0, 0, 0)), pl.BlockSpec((seq, GW), lambda b: (b, 0)),
                  pl.BlockSpec((seq, 128), lambda b: (b, 0)), pl.BlockSpec((1, 128), lambda b: (0, 0))],
        out_specs=[pl.BlockSpec((seq, 128), lambda b: (b, 0)), pl.BlockSpec((1, 128), lambda b: (0, 0))],
        out_shape=[jax.ShapeDtypeStruct(cf.shape, BF16), jax.ShapeDtypeStruct((1, 128), F32)],
        scratch_shapes=[pltpu.VMEM((128, seq), F32)],
        compiler_params=_cparams(("arbitrary",)),
    )(df, dfq, cf, fb)


def _gate_out(oa, ob, oc, gates, w, x, gate, seq, tm=ROW_TILE):
    t = x.shape[0]
    tm = min(tm, seq)
    tps = seq // tm

    def body(oa_ref, ob_ref, oc_ref, g_ref, w_ref, x_ref, gt_ref, xo_ref, y_ref, u_ref):
        for n, o_ref in enumerate((oa_ref, ob_ref, oc_ref)):
            sl = slice(n * GW, (n + 1) * GW)
            gv = g_ref[:, sl].astype(F32)
            u_ref[:, sl] = (o_ref[...] * (gv * _sigmoid(gv))).astype(BF16)
        y = jnp.dot(u_ref[...], w_ref[...], preferred_element_type=F32)
        y_ref[...] = y.astype(BF16)
        xo_ref[...] = x_ref[...] + gt_ref[0] * y

    row = lambda wd: pl.BlockSpec((tm, wd), lambda i: (i, 0))
    return pl.pallas_call(
        body, name="gate_out", grid=(t // tm,),
        in_specs=[row(GW), row(GW), row(GW), row(U_PAD), pl.BlockSpec((U_PAD, D_MODEL), lambda i: (0, 0)),
                  row(D_MODEL), pl.BlockSpec((1, 1, D_MODEL), lambda i: (i // tps, 0, 0))],
        out_specs=[row(D_MODEL), row(D_MODEL), row(U_PAD)],
        out_shape=[jax.ShapeDtypeStruct((t, D_MODEL), F32), jax.ShapeDtypeStruct((t, D_MODEL), BF16),
                   jax.ShapeDtypeStruct((t, U_PAD), BF16)],
        compiler_params=_cparams(("arbitrary",)),
    )(oa, ob, oc, gates, w, x, gate)


def _gate_out_bwd(dxo, y, gate, oa, ob, oc, gates, w_t, seq, tm=ROW_TILE, dep=None):
    t = dxo.shape[0]
    tm = min(tm, seq)
    tps = seq // tm
    nb = t // seq

    def body(dxo_ref, y_ref, gt_ref, oa_ref, ob_ref, oc_ref, g_ref, wt_ref,
             dy_ref, doa_ref, dob_ref, doc_ref, dg_ref, dgt_ref):
        i = pl.program_id(0)
        dxo_v = dxo_ref[...]
        dgt = jnp.sum(dxo_v * y_ref[...].astype(F32), axis=0, keepdims=True)
        dyb = (dxo_v * gt_ref[0]).astype(BF16)
        dy_ref[...] = dyb
        du = jnp.dot(dyb, wt_ref[...], preferred_element_type=F32)
        for n, (o_ref, do_ref) in enumerate(((oa_ref, doa_ref), (ob_ref, dob_ref), (oc_ref, doc_ref))):
            sl = slice(n * GW, (n + 1) * GW)
            gv = g_ref[:, sl].astype(F32)
            sg = _sigmoid(gv)
            dun = du[:, sl]
            do_ref[...] = dun * (gv * sg)
            dg_ref[:, sl] = (dun * o_ref[...] * (sg * (1.0 + gv * (1.0 - sg)))).astype(BF16)

        @pl.when(i % tps == 0)
        def _():
            dgt_ref[0] = dgt

        @pl.when(i % tps != 0)
        def _():
            dgt_ref[0] += dgt

    row = lambda wd: pl.BlockSpec((tm, wd), lambda i: (i, 0))
    per_b = pl.BlockSpec((1, 1, D_MODEL), lambda i: (i // tps, 0, 0))
    in_specs = [row(D_MODEL), row(D_MODEL), per_b, row(GW), row(GW), row(GW), row(U_PAD),
                pl.BlockSpec((D_MODEL, U_PAD), lambda i: (0, 0))]
    body, in_specs, args = _after(dep, body, in_specs, [dxo, y, gate, oa, ob, oc, gates, w_t])
    return pl.pallas_call(
        body, name="gate_out_bwd", grid=(t // tm,), in_specs=in_specs,
        out_specs=[row(D_MODEL), row(GW), row(GW), row(GW), row(U_PAD), per_b],
        out_shape=[jax.ShapeDtypeStruct((t, D_MODEL), BF16), jax.ShapeDtypeStruct((t, GW), F32),
                   jax.ShapeDtypeStruct((t, GW), F32), jax.ShapeDtypeStruct((t, GW), F32),
                   jax.ShapeDtypeStruct((t, U_PAD), BF16), jax.ShapeDtypeStruct((nb, 1, D_MODEL), F32)],
        compiler_params=_cparams(("arbitrary",)),
    )(*args)


def _final_loss(x, target, g, tm=ROW_TILE):
    t = x.shape[0]
    tm = min(tm, t)

    def body(x_ref, t_ref, g_ref, dx_ref, loss_ref, dg_ref):
        i = pl.program_id(0)
        xv = x_ref[...]
        rstd = lax.rsqrt(jnp.mean(xv * xv, axis=-1, keepdims=True) + EPS)
        xn = xv * rstd
        gv = g_ref[...]
        err = xn * gv - t_ref[...]
        dy = err * (1.0 / D_MODEL)
        dxn = dy * gv
        dx_ref[...] = rstd * (dxn - xn * jnp.mean(dxn * xn, axis=-1, keepdims=True))
        lp = jnp.sum(err * err, axis=0, keepdims=True) * (0.5 / D_MODEL)
        dgp = jnp.sum(dy * xn, axis=0, keepdims=True)

        @pl.when(i == 0)
        def _():
            loss_ref[...] = lp
            dg_ref[...] = dgp

        @pl.when(i != 0)
        def _():
            loss_ref[...] += lp
            dg_ref[...] += dgp

    row = pl.BlockSpec((tm, D_MODEL), lambda i: (i, 0))
    vec = pl.BlockSpec((1, D_MODEL), lambda i: (0, 0))
    return pl.pallas_call(
        body, name="final_loss", grid=(t // tm,),
        in_specs=[row, row, vec], out_specs=[row, vec, vec],
        out_shape=[jax.ShapeDtypeStruct((t, D_MODEL), F32), jax.ShapeDtypeStruct((1, D_MODEL), F32),
                   jax.ShapeDtypeStruct((1, D_MODEL), F32)],
        compiler_params=_cparams(("arbitrary",)),
    )(x, target, g)


def _adamw(w, gslots, m, v, name, tr=None):
    nl, r, c = w.shape
    ns = gslots.shape[0]
    tr = r if tr is None else tr

    def body(w_ref, g_ref, m_ref, v_ref, go_ref, d_ref, mo_ref, vo_ref):
        g = g_ref[0].astype(F32)
        for j in range(1, ns):
            g = g + g_ref[j].astype(F32)
        mn = ADAM_B1 * m_ref[...] + (1.0 - ADAM_B1) * g
        vn = ADAM_B2 * v_ref[...] + (1.0 - ADAM_B2) * jnp.square(g)
        m_hat = mn / (1.0 - ADAM_B1 ** ADAM_STEP)
        v_hat = vn / (1.0 - ADAM_B2 ** ADAM_STEP)
        go_ref[...] = g
        d_ref[...] = -ADAM_LR * (m_hat / (jnp.sqrt(v_hat) + ADAM_EPS) + ADAM_WD * w_ref[...])
        mo_ref[...] = mn
        vo_ref[...] = vn

    blk = pl.BlockSpec((1, tr, c), lambda l, i: (l, i, 0))
    return pl.pallas_call(
        body, name=name, grid=(nl, r // tr),
        in_specs=[blk, pl.BlockSpec((ns, 1, tr, c), lambda l, i: (0, l, i, 0)), blk, blk],
        out_specs=[blk] * 4, out_shape=[jax.ShapeDtypeStruct((nl, r, c), F32)] * 4,
        compiler_params=_cparams(("arbitrary", "arbitrary")),
    )(w, gslots, m, v)


def _rope_tables(positions):
    inv = ROPE_THETA ** (-jnp.arange(0, A_ROPE, 2, dtype=F32) / A_ROPE)
    ang = positions.astype(F32)[:, None] * inv
    cos, sin = jnp.cos(ang), jnp.sin(ang)
    t = positions.shape[0]
    one = jnp.ones((t, 64), F32)
    zero16 = jnp.zeros((t, 16), F32)
    cos_t = jnp.concatenate([one, cos, cos, jnp.ones((t, 32), F32)], axis=1)
    sin_a = jnp.concatenate([jnp.zeros((t, 64), F32), -sin, zero16, jnp.zeros((t, 32), F32)], axis=1)
    sin_b = jnp.concatenate([jnp.zeros((t, 64), F32), zero16, sin, jnp.zeros((t, 32), F32)], axis=1)
    return cos_t, sin_a, sin_b


def _pad_heads(w, real, padded, nheads, axis):
    shp = w.shape[:axis] + (nheads, real) + w.shape[axis + 1:]
    w = w.reshape(shp)
    pad = [(0, 0)] * w.ndim
    pad[axis + 1] = (0, padded - real)
    w = jnp.pad(w, pad)
    return w.reshape(w.shape[:axis] + (nheads * padded,) + w.shape[axis + 2:])


def kernel(x, c, positions, w_ada, b_ada, norm_g, w_in, a_q_norm_g, a_w_uq, a_kv_norm_g, a_w_ukv, b_rel_bias, c_forget_b, w_out, final_g, loss_target, m_w_ada, m_b_ada, m_norm_g, m_w_in, m_a_q_norm_g, m_a_w_uq, m_a_kv_norm_g, m_a_w_ukv, m_b_rel_bias, m_c_forget_b, m_w_out, m_final_g, v_w_ada, v_b_ada, v_norm_g, v_w_in, v_a_q_norm_g, v_a_w_uq, v_a_kv_norm_g, v_a_w_ukv, v_b_rel_bias, v_c_forget_b, v_w_out, v_final_g):
    nb, seq, _ = x.shape
    t = nb * seq
    me = 4 * lax.axis_index("x") + 2 * lax.axis_index("y") + lax.axis_index("c")
    x2 = x.reshape(t, D_MODEL)
    tgt = loss_target.reshape(t, D_MODEL)
    cos_t, sin_a, sin_b = _rope_tables(positions.reshape(t))

    def shards(l):
        return [_pad_runs(w_in[l].astype(BF16), IN_RUNS, N_PAD, 1), w_out[l].astype(BF16),
                a_w_uq[l].astype(BF16), a_w_ukv[l].astype(BF16)]

    def prepare(gi, go, gq, gkv):
        wi = gi.reshape(D_MODEL, N_PAD)
        wo = _pad_runs(go.reshape(D_MODEL, D_MODEL), OUT_RUNS, U_PAD, 0)
        wq = jnp.transpose(gq, (1, 0, 2)).reshape(A_Q_RANK, A_HEADS * (A_NOPE + A_ROPE))
        wq = _pad_heads(wq, A_NOPE + A_ROPE, HEAD_PAD, A_HEADS, 1)
        wkv = jnp.transpose(gkv, (1, 0, 2)).reshape(A_KV_RANK, A_HEADS, 2 * A_NOPE)
        wk = jnp.pad(wkv[:, :, :A_NOPE], ((0, 0), (0, 0), (0, HEAD_PAD - A_NOPE))).reshape(A_KV_RANK, A_HEADS * HEAD_PAD)
        wv = wkv[:, :, A_NOPE:].reshape(A_KV_RANK, GW)
        return dict(w_in=wi, w_in_t=wi.T, w_out=wo, w_out_t=wo.T, wuq=wq, wuq_t=wq.T, wk=wk, wk_t=wk.T,
                    wv=wv, wv_t=wv.T)

    gathered = _gather(shards(0) + [c], "gather_weights0")
    c_all = gathered[-1].reshape(N_DEV * nb, D_MODEL)
    weights = [prepare(*gathered[:4]), None]

    c_act, mod_cols = _ada_fwd(c_all, w_ada)
    (mod_g,) = _gather([mod_cols], "gather_mod")
    gather1, gather1_token = _split_start("gather", shards(1), "gather_weights1_start", after=mod_g)
    mod_all = jnp.transpose(mod_g, (1, 2, 0, 3)).reshape(DEPTH, N_DEV * nb, 3 * D_MODEL)
    mod = lax.dynamic_slice_in_dim(mod_all, me * nb, nb, axis=1) + b_ada[:, None, :]

    fb_pad = jnp.pad(c_forget_b, ((0, 0), (0, 128 - C_HEADS)))
    a_scale = (A_NOPE + A_ROPE) ** -0.5
    h_scale = CHUNK ** -0.5

    saved = []
    xl = x2
    for l in range(DEPTH):
        if l == 1:
            weights[1] = prepare(*_split_wait(gather1, xl, "gather_weights1_wait")[1])
        w = weights[l]
        shift, scale, gate = mod[l, :, :D_MODEL], mod[l, :, D_MODEL:2 * D_MODEL], mod[l, :, 2 * D_MODEL:]
        ss = jnp.stack([shift, 1.0 + scale], axis=1)
        gate3 = gate[:, None, :]
        h, cq, ckv, kpe, gates, bq, bk, bv, cq2, ck, cv, cf = _ln_in(
            xl, ss, norm_g[l:l + 1], w["w_in"], seq, dep=gather1_token if l == 0 else None)
        q, k, v, cqn, ckvn = _mla_prep(cq, ckv, kpe, a_q_norm_g[l:l + 1], a_kv_norm_g[l:l + 1],
                                       w["wuq"], w["wk"], w["wv"], cos_t, sin_a, sin_b)
        oa, sta = _attn_fwd("mla", q, k, v, None, seq, a_scale)
        table = _band_table(jnp.pad(b_rel_bias[l], ((0, 8 - B_HEADS), (0, GW - N_REL))))
        ob, stb = _band_fwd(bq, bk, bv, table, seq, h_scale)
        fcum = _fox_prep(cf, fb_pad[l:l + 1], seq)
        oc, stc = _attn_fwd("fox", cq2, ck, cv, fcum, seq, h_scale)
        xn, y, u = _gate_out(oa, ob, oc, gates, w["w_out"], xl, gate3, seq)
        saved.append(dict(x=xl, ss=ss, gate3=gate3, h=h, cq=cq, ckv=ckv, gates=gates, bq=bq, bk=bk, bv=bv,
                          cq2=cq2, ck=ck, cv=cv, cf=cf, q=q, k=k, v=v, cqn=cqn, ckvn=ckvn, oa=oa, sta=sta,
                          table=table, ob=ob, stb=stb, fcum=fcum, oc=oc, stc=stc, y=y, u=u))
        xl = xn

    dx, loss_lanes, g_final = _final_loss(xl, tgt, final_g[None, :])
    loss = lax.psum(jnp.sum(loss_lanes), AXES)

    rows = D_MODEL // N_DEV
    core = lax.axis_index("c").astype(jnp.int32).reshape(1)
    n_seg_a = 4
    dmods, smalls, parts = [None] * DEPTH, [None] * DEPTH, [None] * DEPTH
    pair1 = chips1 = pair1_token = chips1_token = None
    for l in reversed(range(DEPTH)):
        s, w = saved[l], weights[l]
        dy, doa, dob, doc, dgates, dgate = _gate_out_bwd(dx, s["y"], s["gate3"], s["oa"], s["ob"], s["oc"],
                                                         s["gates"], w["w_out_t"], seq, dep=pair1_token)
        g_out = _unpad_runs(_matmul_tn(s["u"], dy, "dw_out"), OUT_RUNS, 0)
        if l == 0:
            own, from_sib = _split_wait(pair1, g_out, "grads1_pair_wait")
            chips1, chips1_token = _split_start("chips", _pair_add(core, own, from_sib, "grads1_add"), "grads1_chips_start")
        dq, dk, dv = _attn_bwd("mla", s["q"], s["k"], s["v"], None, s["oa"], s["sta"], doa, seq, a_scale,
                               dep=chips1_token)
        dbq, dbk, dbv, gtab = _band_bwd(s["bq"], s["bk"], s["bv"], s["table"], s["ob"], s["stb"], dob, seq, h_scale,
                                        dep=chips1_token)
        g_rel = _band_table_bwd(gtab)[:, 0, :N_REL]
        dcq2, dck, dcv, dfc, dfq = _attn_bwd("fox", s["cq2"], s["ck"], s["cv"], s["fcum"], s["oc"], s["stc"], doc,
                                             seq, h_scale, dep=chips1_token)
        dcf, dfb = _fox_prep_bwd(dfc, dfq, s["cf"], fb_pad[l:l + 1], seq)
        dcq, dckv, dkpe, dqlin, dklin, dgq, dgkv = _mla_prep_bwd(
            dq, dk, dv, s["cq"], s["ckv"], a_q_norm_g[l:l + 1], a_kv_norm_g[l:l + 1],
            w["wuq_t"], w["wk_t"], w["wv_t"], cos_t, sin_a, sin_b)
        gq_pad = _matmul_tn(s["cqn"], dqlin, "dw_uq")
        g_uq = gq_pad.reshape(A_Q_RANK, A_HEADS, HEAD_PAD)[:, :, :A_NOPE + A_ROPE].reshape(A_Q_RANK, -1)
        gkv_pad = _matmul_tn(s["ckvn"], [dklin, dv], "dw_ukv")
        gk_pad = gkv_pad[:, :A_HEADS * HEAD_PAD].reshape(A_KV_RANK, A_HEADS, HEAD_PAD)[:, :, :A_NOPE]
        gv_pad = gkv_pad[:, A_HEADS * HEAD_PAD:].reshape(A_KV_RANK, A_HEADS, A_NOPE)
        g_ukv = jnp.concatenate([gk_pad, gv_pad], axis=2).reshape(A_KV_RANK, -1)
        dz = [dcq, dckv, dkpe, dgates, dbq, dbk, dbv, dcq2, dck, dcv, dcf]
        g_in_a = _matmul_tn(s["h"], dz[:n_seg_a], "dw_in_a")
        first = [g_in_a.reshape(N_DEV, rows, -1), g_out.reshape(N_DEV, rows, D_MODEL),
                 g_uq.reshape(A_Q_RANK, N_DEV, -1).transpose(1, 0, 2), g_ukv.reshape(A_KV_RANK, N_DEV, -1).transpose(1, 0, 2)]
        if l == 1:
            g_in_b = _matmul_tn(s["h"], dz[n_seg_a:], "dw_in_b")
            pair1, pair1_token = _split_start("pair", first + [g_in_b.reshape(N_DEV, rows, -1)], "grads1_pair_start")
            tail_token = None
        else:
            pair0a, pair0a_token = _split_start("pair", first, "grads0a_pair_start")
            g_in_b = _matmul_tn(s["h"], dz[n_seg_a:], "dw_in_b", dep=pair0a_token)
            own, from_sib = _split_wait(pair0a, g_in_b, "grads0a_pair_wait")
            sums0a = _pair_add(core, own, from_sib, "grads0a_add")
            pair0b, pair0b_token = _split_start("pair", [g_in_b.reshape(N_DEV, rows, -1)], "grads0b_pair_start",
                                                after=sums0a[0])
            chips0a, tail_token = _split_start("chips", sums0a, "grads0a_chips_start", after=pair0b_token)
        dx, dss, dg_norm = _ln_in_bwd(dz, w["w_in_t"], s["x"], s["ss"], norm_g[l:l + 1], dx, seq, dep=tail_token)
        dmods[l] = jnp.concatenate([dss[:, 0, :], dss[:, 1, :], dgate[:, 0, :]], axis=1)
        smalls[l] = [dg_norm.reshape(-1), dgq.reshape(-1), dgkv.reshape(-1), g_rel.reshape(-1),
                     dfb[0, :C_HEADS]]
    grad_x = dx.reshape(nb, seq, D_MODEL)
    parts[1] = _split_wait(chips1, dx, "grads1_chips_wait")[1]
    parts0a = _split_wait(chips0a, dx, "grads0a_chips_wait")[1]
    own, from_sib = _split_wait(pair0b, dx, "grads0b_pair_wait")

    small = jnp.concatenate([p for l in range(DEPTH) for p in smalls[l]] + [g_final.reshape(-1)])
    n_small = small.shape[0]
    small_rows = -(-n_small // 1024) * 8
    small = jnp.pad(small, (0, small_rows * 128 - n_small)).reshape(small_rows, 128)
    dmod_local = jnp.stack(dmods)
    dmod_g, small_g = _gather([dmod_local, small], "gather_small", dep=parts0a[0])
    chips0, chips0_token = _split_start("chips", _pair_add(core, own, from_sib, "grads0b_add"), "grads0b_chips_start",
                                        after=small_g)
    dmod_all = jnp.transpose(dmod_g, (1, 0, 2, 3)).reshape(DEPTH, N_DEV * nb, 3 * D_MODEL)
    cols = 3 * D_MODEL // N_DEV
    dmod_mine = lax.dynamic_slice_in_dim(dmod_all, me * cols, cols, axis=2)
    g_w_ada, g_b_ada = _ada_bwd(c_act, dmod_all, dmod_mine, chips0_token)
    small_sum = _sum_slots(small_g, "sum_small").reshape(-1)

    def split_small():
        out, pos = [], 0
        sizes = [D_MODEL, A_Q_RANK, A_KV_RANK, B_HEADS * N_REL, C_HEADS]
        per_layer = []
        for l in range(DEPTH):
            parts = []
            for sz in sizes:
                parts.append(small_sum[pos:pos + sz])
                pos += sz
            per_layer.append(parts)
        for j in range(len(sizes)):
            out.append(jnp.stack([per_layer[l][j] for l in range(DEPTH)]))
        out.append(small_sum[pos:pos + D_MODEL])
        return out

    g_norm, g_qn, g_kvn, g_relb, g_fb, g_fin = split_small()

    def adam(w, g, m, v, name, tr=None):
        shp = w.shape
        w3 = w.reshape((1,) * (3 - w.ndim) + shp)
        outs = _adamw(w3, g.reshape((-1,) + w3.shape), m.reshape(w3.shape), v.reshape(w3.shape), name, tr)
        return [o.reshape(shp) for o in outs]

    res = {
        "w_ada": adam(w_ada, g_w_ada, m_w_ada, v_w_ada, "adam_w_ada", 256),
        "b_ada": adam(b_ada, g_b_ada, m_b_ada, v_b_ada, "adam_b_ada"),
        "norm_g": adam(norm_g, g_norm, m_norm_g, v_norm_g, "adam_norm_g"),
        "a_q_norm_g": adam(a_q_norm_g, g_qn, m_a_q_norm_g, v_a_q_norm_g, "adam_q_norm"),
        "a_kv_norm_g": adam(a_kv_norm_g, g_kvn, m_a_kv_norm_g, v_a_kv_norm_g, "adam_kv_norm"),
        "b_rel_bias": adam(b_rel_bias, g_relb.reshape(b_rel_bias.shape), m_b_rel_bias, v_b_rel_bias, "adam_rel_bias"),
        "c_forget_b": adam(c_forget_b, g_fb, m_c_forget_b, v_c_forget_b, "adam_forget_b"),
        "final_g": adam(final_g, g_fin, m_final_g, v_final_g, "adam_final_g"),
    }
    parts[0] = list(parts0a) + list(_split_wait(chips0, res["w_ada"][1], "grads0b_chips_wait")[1])
    p_in = jnp.stack([_unpad_runs(jnp.concatenate([parts[l][0], parts[l][4]], axis=2), IN_RUNS, 2)
                      for l in range(DEPTH)], axis=1)
    p_out, p_uq, p_ukv = (jnp.stack([parts[l][j] for l in range(DEPTH)], axis=1) for j in (1, 2, 3))
    res.update({
        "w_in": adam(w_in, p_in, m_w_in, v_w_in, "adam_w_in", 32),
        "a_w_uq": adam(a_w_uq, p_uq, m_a_w_uq, v_a_w_uq, "adam_w_uq"),
        "a_w_ukv": adam(a_w_ukv, p_ukv, m_a_w_ukv, v_a_w_ukv, "adam_w_ukv"),
        "w_out": adam(w_out, p_out, m_w_out, v_w_out, "adam_w_out", 64),
    })
    names = ["w_ada", "b_ada", "norm_g", "w_in", "a_q_norm_g", "a_w_uq", "a_kv_norm_g", "a_w_ukv", "b_rel_bias",
             "c_forget_b", "w_out", "final_g"]
    outs = [loss, grad_x]
    for j in range(4):
        outs += [res[n][j] for n in names]
    return tuple(outs)
```

```python
import functools
import math

import jax
import jax.numpy as jnp
from jax import lax
from jax.experimental import pallas as pl
from jax.experimental.pallas import tpu as pltpu

F32 = jnp.float32
BF16 = jnp.bfloat16
HI = lax.Precision.HIGHEST

N_DEV = 8
AXES = ("x", "y", "c")
D_MODEL = 1024
DEPTH = 2
CHUNK = 64
EPS = 1e-6
NEG = -1e30
A_HEADS = 6
A_NOPE = 64
A_ROPE = 32
A_Q_RANK = 384
A_KV_RANK = 256
ROPE_THETA = 10000.0
B_HEADS = 5
B_LEFT = 512
REL_CLIP = 128
N_REL = 2 * REL_CLIP + 1
C_HEADS = 5
HEAD_PAD = 128
GW = 384
N_IN = 3621
ADAM_LR = 0.001
ADAM_B1 = 0.9
ADAM_B2 = 0.999
ADAM_EPS = 1e-08
ADAM_WD = 0.01
ADAM_STEP = 10
VMEM_LIMIT = 56 * 1024 * 1024
ROW_TILE = 512

Z_SEGS = (
    ("cq", 0, 384, F32), ("ckv", 384, 256, F32), ("kpe", 640, 128, F32), ("gates", 768, 1152, BF16),
    ("bq", 1920, 384, BF16), ("bk", 2304, 384, BF16), ("bv", 2688, 384, BF16),
    ("cq2", 3072, 384, BF16), ("ck", 3456, 384, BF16), ("cv", 3840, 384, BF16), ("cf", 4224, 128, F32),
)
N_PAD = 4352
IN_RUNS = (
    (0, 384, 0), (384, 256, 384), (640 + 64, 32, 640),
    (768, 384, 672), (768 + 384, 320, 2016), (768 + 768, 320, 3301),
    (1920, 320, 1056), (2304, 320, 1376), (2688, 320, 1696),
    (3072, 320, 2336), (3456, 320, 2656), (3840, 320, 2976), (4224, 5, 3296),
)
OUT_RUNS = ((0, 384, 0), (384, 320, 384), (768, 320, 704))
U_PAD = 1152


def _cparams(sem=None, vmem=VMEM_LIMIT):
    return pltpu.CompilerParams(dimension_semantics=sem, vmem_limit_bytes=vmem)


def _after(dep, body, in_specs, args):
    if dep is None:
        return body, in_specs, args
    n = len(args)

    def ordered(*refs):
        return body(*refs[:n], *refs[n + 1:])

    return ordered, list(in_specs) + [pl.BlockSpec((8, 128), lambda *_: (0, 0))], list(args) + [dep]


def _pad_runs(w, runs, total, axis):
    order = sorted(runs)
    parts, pos = [], 0
    for off, wd, src in order:
        if off > pos:
            shp = list(w.shape)
            shp[axis] = off - pos
            parts.append(jnp.zeros(shp, w.dtype))
        parts.append(lax.slice_in_dim(w, src, src + wd, axis=axis))
        pos = off + wd
    if pos < total:
        shp = list(w.shape)
        shp[axis] = total - pos
        parts.append(jnp.zeros(shp, w.dtype))
    return jnp.concatenate(parts, axis=axis)


def _unpad_runs(w, runs, axis):
    order = sorted(runs, key=lambda r: r[2])
    return jnp.concatenate([lax.slice_in_dim(w, off, off + wd, axis=axis) for off, wd, _ in order], axis=axis)


def _sigmoid(x):
    return 1.0 / (1.0 + jnp.exp(-x))


N_CHIP = 4
ANY_SPEC = pl.BlockSpec(memory_space=pl.ANY)
MESH_ID = pl.DeviceIdType.MESH


def _gather(arrs, name, dep=None):
    n = len(arrs)
    nin = n + (dep is not None)

    def body(*refs):
        ins, outs = refs[:n], refs[nin:nin + n]
        send_sems, recv_sems, local_sems = refs[nin + n:]
        x, y, c = lax.axis_index("x"), lax.axis_index("y"), lax.axis_index("c")
        me, sib = (x, y, c), (x, y, 1 - c)
        chips = [(1 - x, y), (x, 1 - y), (1 - x, 1 - y)]

        def slot(px, py, pc):
            return 4 * px + 2 * py + pc

        def copy(a, k, block, to, src=None):
            dst = outs[a].at[slot(*block)]
            return pltpu.make_async_remote_copy(
                src_ref=dst if src is None else src, dst_ref=dst, send_sem=send_sems.at[a, k],
                recv_sem=recv_sems.at[a, k], device_id=to, device_id_type=MESH_ID)

        local = [pltpu.make_async_copy(ins[a], outs[a].at[slot(*me)], local_sems.at[a]) for a in range(n)]
        first = []
        for a in range(n):
            first.append(copy(a, 0, me, sib, src=ins[a]))
            first += [copy(a, 1 + j, me, (*chip, c), src=ins[a]) for j, chip in enumerate(chips)]
        for cp in local + first:
            cp.start()
        passed = []
        for j, chip in enumerate(chips):
            for a in range(n):
                copy(a, 1 + j, (*chip, c), me).wait_recv()
                fwd = copy(a, 4 + j, (*chip, c), sib)
                fwd.start()
                passed.append(fwd)
        for a in range(n):
            copy(a, 0, sib, me).wait_recv()
            for j, chip in enumerate(chips):
                copy(a, 4 + j, (*chip, 1 - c), me).wait_recv()
        for cp in first + passed:
            cp.wait_send()
        for cp in local:
            cp.wait()

    return pl.pallas_call(
        body, name=name, out_shape=[jax.ShapeDtypeStruct((N_DEV,) + a.shape, a.dtype) for a in arrs],
        in_specs=[ANY_SPEC] * nin, out_specs=[ANY_SPEC] * n,
        scratch_shapes=[pltpu.SemaphoreType.DMA((n, N_DEV - 1)), pltpu.SemaphoreType.DMA((n, N_DEV - 1)),
                        pltpu.SemaphoreType.DMA((n,))],
    )(*arrs, *([] if dep is None else [dep]))


HBM_SPEC = pl.BlockSpec(memory_space=pltpu.HBM)
SEM_SPEC = pl.BlockSpec(memory_space=pltpu.SEMAPHORE)
SPLIT_EFFECT = pltpu.SideEffectType.DATAFLOW_SIDE_EFFECTING
SPLIT_SEMS = {"gather": (N_DEV - 1, True), "pair": (N_CHIP, False), "chips": (N_CHIP - 1, True)}


def _split_descriptors(pattern, srcs, lands, sems):
    x, y, c = lax.axis_index("x"), lax.axis_index("y"), lax.axis_index("c")
    nsem, has_local = SPLIT_SEMS[pattern]
    per = 2 * nsem + int(has_local)
    starts, arrivals, local = [], [], []

    def remote(a, k, src, dst, to):
        return pltpu.make_async_remote_copy(src_ref=src, dst_ref=dst, send_sem=sems[a * per + k],
                                            recv_sem=sems[a * per + nsem + k], device_id=to, device_id_type=MESH_ID)

    for a in range(len(srcs)):
        if pattern == "gather":
            me = 4 * x + 2 * y + c
            local.append(pltpu.make_async_copy(srcs[a], lands[a].at[me], sems[a * per + 2 * nsem]))
            for k in range(1, N_DEV):
                px = (1 - x) if (k >> 2) & 1 else x
                py = (1 - y) if (k >> 1) & 1 else y
                pc = (1 - c) if k & 1 else c
                starts.append(remote(a, k - 1, srcs[a], lands[a].at[me], (px, py, pc)))
                arrivals.append(remote(a, k - 1, srcs[a], lands[a].at[4 * px + 2 * py + pc], (px, py, pc)))
        elif pattern == "pair":
            for q in range(N_CHIP):
                cp = remote(a, q, srcs[a].at[2 * q + 1 - c], lands[a].at[q], (x, y, 1 - c))
                starts.append(cp)
                arrivals.append(cp)
        else:
            mine = 2 * x + y
            local.append(pltpu.make_async_copy(srcs[a].at[mine], lands[a].at[mine], sems[a * per + 2 * nsem]))
            for k in range(1, N_CHIP):
                px = (1 - x) if (k >> 1) & 1 else x
                py = (1 - y) if k & 1 else y
                starts.append(remote(a, k - 1, srcs[a].at[2 * px + py], lands[a].at[mine], (px, py, c)))
                arrivals.append(remote(a, k - 1, srcs[a].at[2 * px + py], lands[a].at[2 * px + py], (px, py, c)))
    return starts, arrivals, local


def _split_start(pattern, arrs, name, after=None):
    n = len(arrs)
    extra = [] if after is None else [after]
    nsem, has_local = SPLIT_SEMS[pattern]
    if pattern == "gather":
        land_shapes = [(N_DEV,) + a.shape for a in arrs]
    elif pattern == "pair":
        land_shapes = [(N_CHIP,) + a.shape[1:] for a in arrs]
    else:
        land_shapes = [a.shape for a in arrs]
    nsem_out = n * (2 * nsem + int(has_local))

    def body(*refs):
        srcs, lands = refs[:n], refs[n:2 * n]
        first_sem = 2 * n + len(extra)
        sems = refs[first_sem:first_sem + nsem_out]
        token = refs[-1]
        starts, _, local = _split_descriptors(pattern, srcs, lands, sems)
        for cp in local + starts:
            cp.start()
        token[...] = jnp.zeros_like(token)

    out_shape = ([pltpu.SemaphoreType.DMA(())] * nsem_out + [pltpu.HBM(a.shape, a.dtype) for a in arrs]
                 + [pltpu.HBM(s, a.dtype) for s, a in zip(land_shapes, arrs)] + [jax.ShapeDtypeStruct((8, 128), F32)])
    ins = ([pltpu.with_memory_space_constraint(a, pltpu.HBM) for a in arrs]
           + [pltpu.with_memory_space_constraint(lax.empty(s, a.dtype), pltpu.HBM) for s, a in zip(land_shapes, arrs)])
    outs = pl.pallas_call(
        body, name=name, out_shape=out_shape, in_specs=[HBM_SPEC] * (2 * n) + [ANY_SPEC] * len(extra),
        out_specs=[SEM_SPEC] * nsem_out + [HBM_SPEC] * (2 * n) + [pl.BlockSpec(memory_space=pltpu.VMEM)],
        input_output_aliases={i: nsem_out + i for i in range(2 * n)},
        compiler_params=pltpu.CompilerParams(has_side_effects=SPLIT_EFFECT),
    )(*ins, *extra)
    handle = dict(pattern=pattern, n=n, sems=outs[:nsem_out], srcs=outs[nsem_out:nsem_out + n],
                  lands=outs[nsem_out + n:nsem_out + 2 * n])
    return handle, outs[-1]


def _split_wait(handle, after, name):
    pattern, n = handle["pattern"], handle["n"]
    nsem_in = len(handle["sems"])

    def body(*refs):
        srcs, lands = refs[:n], refs[n:2 * n]
        starts, arrivals, local = _split_descriptors(pattern, srcs, lands, refs[2 * n:2 * n + nsem_in])
        for cp in starts:
            cp.wait_send()
        for cp in arrivals:
            cp.wait_recv()
        for cp in local:
            cp.wait()

    srcs, lands = handle["srcs"], handle["lands"]
    outs = pl.pallas_call(
        body, name=name,
        out_shape=[pltpu.HBM(a.shape, a.dtype) for a in srcs] + [pltpu.HBM(a.shape, a.dtype) for a in lands],
        in_specs=[HBM_SPEC] * (2 * n) + [SEM_SPEC] * nsem_in + [ANY_SPEC], out_specs=[HBM_SPEC] * (2 * n),
        input_output_aliases={i: i for i in range(2 * n)},
        compiler_params=pltpu.CompilerParams(has_side_effects=SPLIT_EFFECT),
    )(*srcs, *lands, *handle["sems"], after)
    return outs[:n], outs[n:]


def _pair_add(core, a8s, b4s, name):
    n = len(a8s)

    def body(core_ref, *refs):
        for i in range(n):
            refs[2 * n + i][...] = (refs[i][...] + refs[n + i][...]).astype(BF16)

    own = [pl.BlockSpec((1,) + b.shape[1:], lambda q, core_ref: (2 * q + core_ref[0], 0, 0)) for b in b4s]
    slot = [pl.BlockSpec((1,) + b.shape[1:], lambda q, core_ref: (q, 0, 0)) for b in b4s]
    grid_spec = pltpu.PrefetchScalarGridSpec(num_scalar_prefetch=1, grid=(N_CHIP,), in_specs=own + slot, out_specs=slot)
    return pl.pallas_call(
        body, name=name, grid_spec=grid_spec, out_shape=[jax.ShapeDtypeStruct(b.shape, BF16) for b in b4s],
        compiler_params=_cparams(("arbitrary",)),
    )(core, *a8s, *b4s)


def _sum_slots(x, name):
    _, r, c = x.shape

    def body(x_ref, o_ref):
        acc = x_ref[0]
        for j in range(1, N_DEV):
            acc = acc + x_ref[j]
        o_ref[...] = acc

    return pl.pallas_call(body, name=name, out_shape=jax.ShapeDtypeStruct((r, c), F32))(x)


def _ada_fwd(c_all, w_ada):
    nb = c_all.shape[0]
    cols = w_ada.shape[2]

    def body(c_ref, w_ref, act_ref, mod_ref):
        cv = c_ref[...]
        act = cv * _sigmoid(cv)
        act_ref[...] = act
        for l in range(DEPTH):
            mod_ref[l] = jnp.dot(act, w_ref[l], precision=HI, preferred_element_type=F32)

    return pl.pallas_call(
        body, name="ada_fwd",
        out_shape=[jax.ShapeDtypeStruct((nb, D_MODEL), F32), jax.ShapeDtypeStruct((DEPTH, nb, cols), F32)],
        compiler_params=_cparams(),
    )(c_all, w_ada)


def _ada_bwd(c_act, dmod_all, dmod_mine, dep):
    nb = c_act.shape[0]
    cols = dmod_mine.shape[2]

    def body(act_ref, dall_ref, dmine_ref, dep_ref, gw_ref, gb_ref):
        act = act_ref[...]
        for l in range(DEPTH):
            gw_ref[l] = lax.dot_general(act, dmine_ref[l], (((0,), (0,)), ((), ())),
                                        precision=HI, preferred_element_type=F32)
            gb_ref[l:l + 1, :] = jnp.sum(dall_ref[l], axis=0, keepdims=True)

    return pl.pallas_call(
        body, name="ada_bwd",
        out_shape=[jax.ShapeDtypeStruct((DEPTH, D_MODEL, cols), F32),
                   jax.ShapeDtypeStruct((DEPTH, 3 * D_MODEL), F32)],
        compiler_params=_cparams(),
    )(c_act, dmod_all, dmod_mine, dep)


def _ln_in(x, ss, g, w, seq, tm=ROW_TILE, dep=None):
    t = x.shape[0]
    tm = min(tm, seq)
    tps = seq // tm

    def body(x_ref, ss_ref, g_ref, w_ref, h_ref, *outs):
        xv = x_ref[...]
        xn = xv * lax.rsqrt(jnp.mean(xv * xv, axis=-1, keepdims=True) + EPS)
        h = xn * g_ref[...] * ss_ref[0, 1:2, :] + ss_ref[0, 0:1, :]
        hb = h.astype(BF16)
        h_ref[...] = hb
        z = jnp.dot(hb, w_ref[...], preferred_element_type=F32)
        for o_ref, (_, off, wd, _) in zip(outs, Z_SEGS):
            o_ref[...] = z[:, off:off + wd].astype(o_ref.dtype)

    row = lambda wd: pl.BlockSpec((tm, wd), lambda i: (i, 0))
    in_specs = [row(D_MODEL), pl.BlockSpec((1, 2, D_MODEL), lambda i: (i // tps, 0, 0)),
                pl.BlockSpec((1, D_MODEL), lambda i: (0, 0)), pl.BlockSpec((D_MODEL, N_PAD), lambda i: (0, 0))]
    body, in_specs, args = _after(dep, body, in_specs, [x, ss, g, w])
    return pl.pallas_call(
        body, name="ln_in", grid=(t // tm,), in_specs=in_specs,
        out_specs=[row(D_MODEL)] + [row(wd) for _, _, wd, _ in Z_SEGS],
        out_shape=[jax.ShapeDtypeStruct((t, D_MODEL), BF16)]
        + [jax.ShapeDtypeStruct((t, wd), dt) for _, _, wd, dt in Z_SEGS],
        compiler_params=_cparams(("arbitrary",)),
    )(*args)


def _ln_in_bwd(dz, w_t, x, ss, g, dxo, seq, tm=ROW_TILE, dep=None):
    t = x.shape[0]
    tm = min(tm, seq)
    tps = seq // tm
    nb = t // seq
    nz = len(Z_SEGS)

    def body(*refs):
        dz_refs = refs[:nz]
        wt_ref, x_ref, ss_ref, g_ref, dxo_ref, dx_ref, dss_ref, dg_ref = refs[nz:]
        i = pl.program_id(0)
        dzc = jnp.concatenate([r[...].astype(BF16) for r in dz_refs], axis=1)
        dh = _nt(dzc, wt_ref[...])
        xv = x_ref[...]
        rstd = lax.rsqrt(jnp.mean(xv * xv, axis=-1, keepdims=True) + EPS)
        xn = xv * rstd
        gv = g_ref[...]
        s1 = ss_ref[0, 1:2, :]
        dxg = dh * s1
        dxn = dxg * gv
        dx = rstd * (dxn - xn * jnp.mean(dxn * xn, axis=-1, keepdims=True))
        dx_ref[...] = dxo_ref[...] + dx
        dshift = jnp.sum(dh, axis=0, keepdims=True)
        dscale = jnp.sum(dh * (xn * gv), axis=0, keepdims=True)
        dgp = jnp.sum(dxg * xn, axis=0, keepdims=True)

        @pl.when(i % tps == 0)
        def _():
            dss_ref[0, 0:1, :] = dshift
            dss_ref[0, 1:2, :] = dscale

        @pl.when(i % tps != 0)
        def _():
            dss_ref[0, 0:1, :] += dshift
            dss_ref[0, 1:2, :] += dscale

        @pl.when(i == 0)
        def _():
            dg_ref[...] = dgp

        @pl.when(i != 0)
        def _():
            dg_ref[...] += dgp

    row = lambda wd: pl.BlockSpec((tm, wd), lambda i: (i, 0))
    in_specs = ([row(wd) for _, _, wd, _ in Z_SEGS]
                + [pl.BlockSpec((D_MODEL, N_PAD), lambda i: (0, 0)), row(D_MODEL),
                   pl.BlockSpec((1, 2, D_MODEL), lambda i: (i // tps, 0, 0)),
                   pl.BlockSpec((1, D_MODEL), lambda i: (0, 0)), row(D_MODEL)])
    body, in_specs, args = _after(dep, body, in_specs, [*dz, w_t, x, ss, g, dxo])
    return pl.pallas_call(
        body, name="ln_in_bwd", grid=(t // tm,), in_specs=in_specs,
        out_specs=[row(D_MODEL), pl.BlockSpec((1, 2, D_MODEL), lambda i: (i // tps, 0, 0)),
                   pl.BlockSpec((1, D_MODEL), lambda i: (0, 0))],
        out_shape=[jax.ShapeDtypeStruct((t, D_MODEL), F32), jax.ShapeDtypeStruct((nb, 2, D_MODEL), F32),
                   jax.ShapeDtypeStruct((1, D_MODEL), F32)],
        compiler_params=_cparams(("arbitrary",)),
    )(*args)


def _matmul_tn(a, bs, name, tm=1024, dep=None):
    bs = list(bs) if isinstance(bs, (list, tuple)) else [bs]
    t, k = a.shape
    widths = [b.shape[1] for b in bs]
    n = sum(widths)
    tm = min(tm, t)

    def body(a_ref, *refs):
        b_refs, o_ref = refs[:-1], refs[-1]
        i = pl.program_id(0)
        av = a_ref[...].astype(BF16)
        parts = [b_ref[...].astype(BF16) for b_ref in b_refs]
        bv = parts[0] if len(parts) == 1 else jnp.concatenate(parts, axis=1)
        part = lax.dot_general(av, bv, (((0,), (0,)), ((), ())), preferred_element_type=F32)

        @pl.when(i == 0)
        def _():
            o_ref[...] = part

        @pl.when(i != 0)
        def _():
            o_ref[...] += part

    in_specs = [pl.BlockSpec((tm, k), lambda i: (i, 0))] + [pl.BlockSpec((tm, wd), lambda i: (i, 0)) for wd in widths]
    body, in_specs, args = _after(dep, body, in_specs, [a, *bs])
    return pl.pallas_call(
        body, name=name, grid=(t // tm,), in_specs=in_specs,
        out_specs=pl.BlockSpec((k, n), lambda i: (0, 0)),
        out_shape=jax.ShapeDtypeStruct((k, n), F32),
        compiler_params=_cparams(("arbitrary",)),
    )(*args)


def _rope(blk, cos_t, sin_a, sin_b):
    return blk * cos_t + pltpu.roll(blk, 112, 1) * sin_a + pltpu.roll(blk, 16, 1) * sin_b


def _unrope(d, cos_t, sin_a, sin_b):
    return d * cos_t + pltpu.roll(d * sin_a, 16, 1) + pltpu.roll(d * sin_b, 112, 1)


def _mla_prep(cq, ckv, kpe, gq, gkv, wuq, wk, wv, cos_t, sin_a, sin_b, tm=ROW_TILE, dep=None):
    t = cq.shape[0]
    tm = min(tm, t)
    qw = A_HEADS * HEAD_PAD

    def body(cq_ref, ckv_ref, kpe_ref, gq_ref, gkv_ref, wuq_ref, wk_ref, wv_ref, c_ref, sa_ref, sb_ref,
             q_ref, k_ref, v_ref, cqn_ref, ckvn_ref):
        ct, sa, sb = c_ref[...], sa_ref[...], sb_ref[...]
        a = cq_ref[...]
        cqn = (a * lax.rsqrt(jnp.mean(a * a, axis=-1, keepdims=True) + EPS) * gq_ref[...]).astype(BF16)
        cqn_ref[...] = cqn
        b = ckv_ref[...]
        ckvn = (b * lax.rsqrt(jnp.mean(b * b, axis=-1, keepdims=True) + EPS) * gkv_ref[...]).astype(BF16)
        ckvn_ref[...] = ckvn
        qlin = jnp.dot(cqn, wuq_ref[...], preferred_element_type=F32)
        klin = jnp.dot(ckvn, wk_ref[...], preferred_element_type=F32)
        v_ref[...] = jnp.dot(ckvn, wv_ref[...], preferred_element_type=F32).astype(BF16)
        kr = _rope(kpe_ref[...], ct, sa, sb)
        for h in range(A_HEADS):
            sl = slice(h * HEAD_PAD, (h + 1) * HEAD_PAD)
            q_ref[:, sl] = _rope(qlin[:, sl], ct, sa, sb).astype(BF16)
            k_ref[:, sl] = (klin[:, sl] + kr).astype(BF16)

    row = lambda wd: pl.BlockSpec((tm, wd), lambda i: (i, 0))
    full = lambda r, c: pl.BlockSpec((r, c), lambda i: (0, 0))
    in_specs = [row(A_Q_RANK), row(A_KV_RANK), row(128), full(1, A_Q_RANK), full(1, A_KV_RANK),
                full(A_Q_RANK, qw), full(A_KV_RANK, qw), full(A_KV_RANK, GW), row(128), row(128), row(128)]
    body, in_specs, args = _after(dep, body, in_specs, [cq, ckv, kpe, gq, gkv, wuq, wk, wv, cos_t, sin_a, sin_b])
    return pl.pallas_call(
        body, name="mla_prep", grid=(t // tm,), in_specs=in_specs,
        out_specs=[row(qw), row(qw), row(GW), row(A_Q_RANK), row(A_KV_RANK)],
        out_shape=[jax.ShapeDtypeStruct((t, qw), BF16), jax.ShapeDtypeStruct((t, qw), BF16),
                   jax.ShapeDtypeStruct((t, GW), BF16), jax.ShapeDtypeStruct((t, A_Q_RANK), BF16),
                   jax.ShapeDtypeStruct((t, A_KV_RANK), BF16)],
        compiler_params=_cparams(("arbitrary",)),
    )(*args)


def _mla_prep_bwd(dq, dk, dv, cq, ckv, gq, gkv, wuq_t, wk_t, wv_t, cos_t, sin_a, sin_b, tm=ROW_TILE):
    t = cq.shape[0]
    tm = min(tm, t)
    qw = A_HEADS * HEAD_PAD

    def body(dq_ref, dk_ref, dv_ref, cq_ref, ckv_ref, gq_ref, gkv_ref, wuqt_ref, wkt_ref, wvt_ref,
             c_ref, sa_ref, sb_ref, dcq_ref, dckv_ref, dkpe_ref, dql_ref, dkl_ref, dgq_ref, dgkv_ref):
        i = pl.program_id(0)
        ct, sa, sb = c_ref[...], sa_ref[...], sb_ref[...]
        lane = lax.broadcasted_iota(jnp.int32, (1, HEAD_PAD), 1)
        nope = lane < A_NOPE
        rope = (lane >= A_NOPE) & (lane < A_NOPE + A_ROPE)
        dksum = None
        for h in range(A_HEADS):
            sl = slice(h * HEAD_PAD, (h + 1) * HEAD_PAD)
            dql_ref[:, sl] = _unrope(dq_ref[:, sl], ct, sa, sb).astype(BF16)
            dkh = dk_ref[:, sl]
            dkl_ref[:, sl] = jnp.where(nope, dkh, 0.0).astype(BF16)
            dksum = dkh if dksum is None else dksum + dkh
        dkpe_ref[...] = jnp.where(rope, _unrope(jnp.where(rope, dksum, 0.0), ct, sa, sb), 0.0).astype(BF16)
        dcqn = jnp.dot(dql_ref[...], wuqt_ref[...], preferred_element_type=F32)
        dckvn = (jnp.dot(dkl_ref[...], wkt_ref[...], preferred_element_type=F32)
                 + jnp.dot(dv_ref[...].astype(BF16), wvt_ref[...], preferred_element_type=F32))

        def norm_bwd(xv, gv, dy):
            rstd = lax.rsqrt(jnp.mean(xv * xv, axis=-1, keepdims=True) + EPS)
            xn = xv * rstd
            dxn = dy * gv
            dx = rstd * (dxn - xn * jnp.mean(dxn * xn, axis=-1, keepdims=True))
            return dx, jnp.sum(dy * xn, axis=0, keepdims=True)

        dcq, dgq = norm_bwd(cq_ref[...], gq_ref[...], dcqn)
        dckv, dgkv = norm_bwd(ckv_ref[...], gkv_ref[...], dckvn)
        dcq_ref[...] = dcq.astype(BF16)
        dckv_ref[...] = dckv.astype(BF16)

        @pl.when(i == 0)
        def _():
            dgq_ref[...] = dgq
            dgkv_ref[...] = dgkv

        @pl.when(i != 0)
        def _():
            dgq_ref[...] += dgq
            dgkv_ref[...] += dgkv

    row = lambda wd: pl.BlockSpec((tm, wd), lambda i: (i, 0))
    full = lambda r, c: pl.BlockSpec((r, c), lambda i: (0, 0))
    return pl.pallas_call(
        body, name="mla_prep_bwd", grid=(t // tm,),
        in_specs=[row(qw), row(qw), row(GW), row(A_Q_RANK), row(A_KV_RANK), full(1, A_Q_RANK), full(1, A_KV_RANK),
                  full(qw, A_Q_RANK), full(qw, A_KV_RANK), full(GW, A_KV_RANK), row(128), row(128), row(128)],
        out_specs=[row(A_Q_RANK), row(A_KV_RANK), row(128), row(qw), row(qw), full(1, A_Q_RANK), full(1, A_KV_RANK)],
        out_shape=[jax.ShapeDtypeStruct((t, A_Q_RANK), BF16), jax.ShapeDtypeStruct((t, A_KV_RANK), BF16),
                   jax.ShapeDtypeStruct((t, 128), BF16), jax.ShapeDtypeStruct((t, qw), BF16),
                   jax.ShapeDtypeStruct((t, qw), BF16), jax.ShapeDtypeStruct((1, A_Q_RANK), F32),
                   jax.ShapeDtypeStruct((1, A_KV_RANK), F32)],
        compiler_params=_cparams(("arbitrary",)),
    )(dq, dk, dv, cq, ckv, gq, gkv, wuq_t, wk_t, wv_t, cos_t, sin_a, sin_b)


def _nt(a, b):
    return lax.dot_general(a, b, (((1,), (1,)), ((), ())), preferred_element_type=F32)


def _tn(a, b):
    return lax.dot_general(a, b, (((0,), (0,)), ((), ())), preferred_element_type=F32)


def _causal_mask(kind, q0, k0, tq, tk):
    qpos = q0 + lax.broadcasted_iota(jnp.int32, (tq, tk), 0)
    kpos = k0 + lax.broadcasted_iota(jnp.int32, (tq, tk), 1)
    if kind == "mla":
        return lax.shift_right_logical(kpos, 6) <= lax.shift_right_logical(qpos, 6)
    return kpos <= qpos


def _attn_fwd(kind, q, k, v, f, seq, scale, tq=512, tk=512):
    t = v.shape[0]
    nb = t // seq
    nq = seq // tq
    hw = 256 if kind == "mla" else 128
    n_heads = A_HEADS if kind == "mla" else C_HEADS
    use_f = f is not None
    tq, tk = min(tq, seq), min(tk, seq)
    nq = seq // tq
    assert tk == tq

    def body(*refs):
        if use_f:
            q_ref, k_ref, v_ref, f_ref, o_ref, st_ref = refs
        else:
            q_ref, k_ref, v_ref, o_ref, st_ref = refs
        qi = pl.program_id(2)
        q0 = qi * tq
        lane = lax.broadcasted_iota(jnp.int32, (1, 128), 1)
        half = lane >= 64
        qall = q_ref[...]
        if kind == "mla":
            qhs = [qall[:, 0:128], qall[:, 128:256]]
            post = scale
        else:
            assert math.frexp(scale)[0] == 0.5
            qall = qall * jnp.asarray(scale, BF16)
            qhs = [jnp.where(half, jnp.zeros_like(qall), qall), jnp.where(half, qall, jnp.zeros_like(qall))]
            post = None
        kd = pl.multiple_of(q0, tq)
        diag = _causal_mask(kind, 0, 0, tq, tk)

        def block(j, k0, state, masked):
            m, l, acc = state
            kh = k_ref[pl.ds(k0, tk), j * 128:(j + 1) * 128] if kind == "mla" else k_ref[pl.ds(k0, tk), :]
            s = _nt(qhs[j], kh)
            if post is not None:
                s = s * post
            if use_f:
                s = s - f_ref[0, 0, j:j + 1, pl.ds(k0, tk)]
            if masked:
                s = jnp.where(diag, s, NEG)
            mn = jnp.maximum(m, jnp.max(s, axis=-1, keepdims=True))
            alpha = jnp.exp(m - mn)
            p = jnp.exp(s - mn)
            l = alpha * l + jnp.sum(p, axis=-1, keepdims=True)
            acc = alpha * acc + jnp.dot(p.astype(BF16), v_ref[pl.ds(k0, tk), :], preferred_element_type=F32)
            return mn, l, acc

        def run(heads):
            def kstep(kb, carry):
                k0 = pl.multiple_of(kb * tk, tk)
                out = ()
                for n, j in enumerate(heads):
                    out += block(j, k0, carry[3 * n:3 * n + 3], False)
                return out

            init = (jnp.full((tq, 1), NEG, F32), jnp.zeros((tq, 1), F32), jnp.zeros((tq, 128), F32)) * len(heads)
            carry = lax.fori_loop(0, qi, kstep, init)
            o, st = jnp.zeros((tq, 128), F32), jnp.zeros((tq, 128), F32)
            for n, j in enumerate(heads):
                m, l, acc = block(j, kd, carry[3 * n:3 * n + 3], True)
                o = jnp.where(half == bool(j), acc / l, o)
                st = jnp.where(lane == j, m + jnp.log(l), st)
            o_ref[...] = o
            st_ref[...] = st

        if n_heads % 2 == 0:
            run((0, 1))
        else:
            last = pl.program_id(1) == n_heads // 2
            pl.when(jnp.logical_not(last))(lambda: run((0, 1)))
            pl.when(last)(lambda: run((0,)))

    in_specs = [pl.BlockSpec((tq, hw), lambda b, p, i: (b * nq + i, p)),
                pl.BlockSpec((seq, hw), lambda b, p, i: (b, p)),
                pl.BlockSpec((seq, 128), lambda b, p, i: (b, p))]
    args = [q, k, v]
    if use_f:
        in_specs.append(pl.BlockSpec((1, 1, 8, seq), lambda b, p, i: (b, p, 0, 0)))
        args.append(f)
    oblk = pl.BlockSpec((tq, 128), lambda b, p, i: (b * nq + i, p))
    return pl.pallas_call(
        body, name="attn_fwd_" + kind, grid=(nb, 3, nq), in_specs=in_specs, out_specs=[oblk, oblk],
        out_shape=[jax.ShapeDtypeStruct((t, GW), F32), jax.ShapeDtypeStruct((t, GW), F32)],
        compiler_params=_cparams(("arbitrary", "arbitrary", "arbitrary")),
    )(*args)


def _attn_bwd(kind, q, k, v, f, o, st, do, seq, scale, tq=512, tk=512, dep=None):
    t = v.shape[0]
    nb = t // seq
    tq, tk = min(tq, seq), min(tk, seq)
    nq = seq // tq
    nk = seq // tk
    hw = 256 if kind == "mla" else 128
    n_heads = A_HEADS if kind == "mla" else C_HEADS
    use_f = f is not None
    assert tq == tk

    def body(*refs):
        if use_f:
            q_ref, k_ref, v_ref, f_ref, o_ref, st_ref, do_ref, dq_ref, dk_ref, dv_ref, df_ref, dfq_ref = refs
        else:
            q_ref, k_ref, v_ref, o_ref, st_ref, do_ref, dq_ref, dk_ref, dv_ref = refs
        kj = pl.program_id(2)
        k0 = kj * tk
        lane = lax.broadcasted_iota(jnp.int32, (1, 128), 1)
        half = lane >= 64

        @pl.when(kj == 0)
        def _():
            dq_ref[...] = jnp.zeros_like(dq_ref)
            if use_f:
                dfq_ref[...] = jnp.zeros_like(dfq_ref)

        dk_ref[...] = jnp.zeros_like(dk_ref)
        dv_ref[...] = jnp.zeros_like(dv_ref)
        if use_f:
            df_ref[...] = jnp.zeros_like(df_ref)
        vv = v_ref[...]
        diag = _causal_mask(kind, 0, 0, tq, tk)

        def qstep(qi, masked):
            q0 = pl.multiple_of(qi * tq, tq)
            rows = pl.ds(q0, tq)
            dov = do_ref[rows, :]
            dd = dov * o_ref[rows, :]
            stv = st_ref[rows, :]

            def one_head(j):
                hm = half == bool(j)
                delta = jnp.sum(jnp.where(hm, dd, 0.0), axis=-1, keepdims=True)
                lse = stv[:, j:j + 1]
                if kind == "mla":
                    cols = slice(j * 128, (j + 1) * 128)
                    qh = q_ref[rows, cols]
                    kh = k_ref[:, cols]
                else:
                    cols = slice(0, 128)
                    qa = q_ref[rows, :]
                    qh = jnp.where(hm, qa, jnp.zeros_like(qa))
                    kh = k_ref[...]
                s = _nt(qh, kh) * scale
                if use_f:
                    s = s - f_ref[0, 0, j:j + 1, :]
                if masked:
                    s = jnp.where(diag, s, NEG)
                p = jnp.exp(s - lse)
                doh = jnp.where(hm, dov, 0.0).astype(BF16)
                ds = p * (_nt(doh, vv) - delta)
                dsb = (ds * scale).astype(BF16)
                dv_ref[...] += _tn(p.astype(BF16), doh)
                dk_ref[:, cols] += _tn(dsb, qh)
                dqc = jnp.dot(dsb, kh, preferred_element_type=F32)
                if kind != "mla":
                    dqc = jnp.where(hm, dqc, 0.0)
                dq_ref[rows, cols] += dqc
                if use_f:
                    df_ref[0, 0, j:j + 1, :] += -jnp.sum(ds, axis=0, keepdims=True)
                    dfq_ref[rows, :] += jnp.where(lane == j, jnp.sum(ds, axis=-1, keepdims=True), 0.0)

            def both():
                one_head(0)
                one_head(1)

            if n_heads % 2 == 0:
                both()
            else:
                last = pl.program_id(1) == n_heads // 2
                pl.when(jnp.logical_not(last))(both)
                pl.when(last)(lambda: one_head(0))

        qstep(kj, True)

        def rest(qi, carry):
            qstep(qi, False)
            return carry

        lax.fori_loop(kj + 1, nq, rest, 0)

    full_q = lambda wd: pl.BlockSpec((seq, wd), lambda b, p, i: (b, p))
    kblk = lambda wd: pl.BlockSpec((tk, wd), lambda b, p, i: (b * nk + i, p))
    in_specs = [full_q(hw), kblk(hw), kblk(128)]
    args = [q, k, v]
    if use_f:
        in_specs.append(pl.BlockSpec((1, 1, 8, tk), lambda b, p, i: (b, p, 0, i)))
        args.append(f)
    in_specs += [full_q(128), full_q(128), full_q(128)]
    args += [o, st, do]
    out_specs = [full_q(hw), kblk(hw), kblk(128)]
    out_shape = [jax.ShapeDtypeStruct((t, 3 * hw), F32), jax.ShapeDtypeStruct((t, 3 * hw), F32),
                 jax.ShapeDtypeStruct((t, GW), F32)]
    if use_f:
        out_specs += [pl.BlockSpec((1, 1, 8, tk), lambda b, p, i: (b, p, 0, i)), full_q(128)]
        out_shape += [jax.ShapeDtypeStruct((nb, 3, 8, seq), F32), jax.ShapeDtypeStruct((t, GW), F32)]
    body, in_specs, args = _after(dep, body, in_specs, args)
    return pl.pallas_call(
        body, name="attn_bwd_" + kind, grid=(nb, 3, nk), in_specs=in_specs, out_specs=out_specs,
        out_shape=out_shape, compiler_params=_cparams(("arbitrary", "arbitrary", "arbitrary")),
    )(*args)


BQ = 256
BWIN = BQ + B_LEFT


def _band_geometry():
    r = lax.broadcasted_iota(jnp.int32, (BQ, BWIN), 0)
    j = lax.broadcasted_iota(jnp.int32, (BQ, BWIN), 1)
    rc = lax.shift_right_logical(r, 6)
    jc = lax.shift_right_logical(j, 6)
    allowed = (jc - 8 <= rc) & (rc <= jc)
    return (r + B_LEFT - j) >= REL_CLIP, allowed, j < r


def _band_onehot(transposed, offset=0):
    shape = (BWIN, GW) if transposed else (GW, BWIN)
    kk = lax.broadcasted_iota(jnp.int32, shape, 1 if transposed else 0)
    x = lax.broadcasted_iota(jnp.int32, shape, 0 if transposed else 1) - offset
    x = jnp.where(x < 0, x + BWIN, x)
    return (kk == jnp.clip(B_LEFT - x, -REL_CLIP, REL_CLIP) + REL_CLIP).astype(F32)


def _band_table(rel_bias8):
    def body(b_ref, o_ref):
        hh = pl.program_id(0)
        u8 = jnp.dot(b_ref[...], _band_onehot(False), precision=HI, preferred_element_type=F32)
        rid = lax.broadcasted_iota(jnp.int32, (8, BWIN), 0)
        row = jnp.sum(jnp.where(rid == hh, u8, 0.0), axis=0, keepdims=True)
        far, allowed, _ = _band_geometry()
        tbl = pltpu.roll(jnp.broadcast_to(row, (BQ, BWIN)), 0, 1, stride=1, stride_axis=0)
        tbl = jnp.where(far, row[:, 0:1], tbl)
        o_ref[0] = jnp.where(allowed, tbl, NEG)

    return pl.pallas_call(
        body, name="band_table", grid=(6,),
        in_specs=[pl.BlockSpec((8, GW), lambda h: (0, 0))],
        out_specs=pl.BlockSpec((1, BQ, BWIN), lambda h: (h, 0, 0)),
        out_shape=jax.ShapeDtypeStruct((6, BQ, BWIN), F32),
        compiler_params=_cparams(("arbitrary",)),
    )(rel_bias8)


def _band_table_bwd(gtab):
    def body(g_ref, o_ref):
        gv = g_ref[0]
        _, _, wrapped = _band_geometry()
        gfar = jnp.sum(jnp.sum(jnp.where(wrapped, gv, 0.0), axis=-1, keepdims=True), axis=0, keepdims=True)
        anti = (lax.broadcasted_iota(jnp.int32, (BQ, BQ), 0) + lax.broadcasted_iota(jnp.int32, (BQ, BQ), 1)
                == BQ - 1).astype(F32)
        grev = jnp.dot(anti, jnp.where(wrapped, 0.0, gv), precision=HI, preferred_element_type=F32)
        near = pltpu.roll(grev, 0, 1, stride=1, stride_axis=0)
        y = jnp.broadcast_to(jnp.sum(near, axis=0, keepdims=True), (8, BWIN))
        gb = jnp.dot(y, _band_onehot(True, BQ - 1), precision=HI, preferred_element_type=F32)
        lane = lax.broadcasted_iota(jnp.int32, (8, GW), 1)
        o_ref[0] = gb + jnp.where(lane == 2 * REL_CLIP, gfar, 0.0)

    return pl.pallas_call(
        body, name="band_table_bwd", grid=(B_HEADS,),
        in_specs=[pl.BlockSpec((1, BQ, BWIN), lambda h: (h, 0, 0))],
        out_specs=pl.BlockSpec((1, 8, GW), lambda h: (h, 0, 0)),
        out_shape=jax.ShapeDtypeStruct((B_HEADS, 8, GW), F32),
        compiler_params=_cparams(("arbitrary",)),
    )(gtab)


def _band_fwd(q, k, v, table, seq, scale):
    t = q.shape[0]
    nb = t // seq
    nq = seq // BQ

    def body(q_ref, k_ref, v_ref, tb_ref, o_ref, st_ref, kpad, vpad):
        qi = pl.program_id(2)
        q0 = pl.multiple_of(qi * BQ, BQ)
        lane = lax.broadcasted_iota(jnp.int32, (1, 128), 1)
        half = lane >= 64

        @pl.when(qi == 0)
        def _():
            kpad[0:B_LEFT, :] = jnp.zeros((B_LEFT, 128), BF16)
            vpad[0:B_LEFT, :] = jnp.zeros((B_LEFT, 128), BF16)
            kpad[B_LEFT:, :] = k_ref[...]
            vpad[B_LEFT:, :] = v_ref[...]

        kw = kpad[pl.ds(q0, BWIN), :]
        vw = vpad[pl.ds(q0, BWIN), :]
        inside = lax.broadcasted_iota(jnp.int32, (BQ, BWIN), 1) >= B_LEFT - q0
        assert math.frexp(scale)[0] == 0.5
        qall = q_ref[...] * jnp.asarray(scale, BF16)

        def run(heads):
            o, st = jnp.zeros((BQ, 128), F32), jnp.zeros((BQ, 128), F32)
            for j in heads:
                qh = jnp.where(half == bool(j), qall, jnp.zeros_like(qall))
                s = jnp.where(inside, _nt(qh, kw) + tb_ref[j], NEG)
                m = jnp.max(s, axis=-1, keepdims=True)
                p = jnp.exp(s - m)
                l = jnp.sum(p, axis=-1, keepdims=True)
                o = jnp.where(half == bool(j), jnp.dot(p.astype(BF16), vw, preferred_element_type=F32) / l, o)
                st = jnp.where(lane == j, m + jnp.log(l), st)
            o_ref[...] = o
            st_ref[...] = st

        last = pl.program_id(1) == B_HEADS // 2
        pl.when(jnp.logical_not(last))(lambda: run((0, 1)))
        pl.when(last)(lambda: run((0,)))

    qblk = pl.BlockSpec((BQ, 128), lambda b, p, i: (b * nq + i, p))
    full = pl.BlockSpec((seq, 128), lambda b, p, i: (b, p))
    return pl.pallas_call(
        body, name="band_fwd", grid=(nb, 3, nq),
        in_specs=[qblk, full, full, pl.BlockSpec((2, BQ, BWIN), lambda b, p, i: (p, 0, 0))],
        out_specs=[qblk, qblk],
        out_shape=[jax.ShapeDtypeStruct((t, GW), F32), jax.ShapeDtypeStruct((t, GW), F32)],
        scratch_shapes=[pltpu.VMEM((seq + B_LEFT, 128), BF16), pltpu.VMEM((seq + B_LEFT, 128), BF16)],
        compiler_params=_cparams(("arbitrary", "arbitrary", "arbitrary")),
    )(q, k, v, table)


def _band_bwd(q, k, v, table, o, st, do, seq, scale, dep=None):
    t = q.shape[0]
    nb = t // seq
    nq = seq // BQ

    def body(q_ref, k_ref, v_ref, tb_ref, o_ref, st_ref, do_ref, dq_ref, dk_ref, dv_ref, g_ref,
             kpad, vpad, dkpad, dvpad):
        b = pl.program_id(1)
        qi = pl.program_id(2)
        q0 = pl.multiple_of(qi * BQ, BQ)
        lane = lax.broadcasted_iota(jnp.int32, (1, 128), 1)
        half = lane >= 64

        @pl.when(qi == 0)
        def _():
            kpad[0:B_LEFT, :] = jnp.zeros((B_LEFT, 128), BF16)
            vpad[0:B_LEFT, :] = jnp.zeros((B_LEFT, 128), BF16)
            kpad[B_LEFT:, :] = k_ref[...]
            vpad[B_LEFT:, :] = v_ref[...]
            dkpad[...] = jnp.zeros_like(dkpad)
            dvpad[...] = jnp.zeros_like(dvpad)

        @pl.when((qi == 0) & (b == 0))
        def _():
            g_ref[...] = jnp.zeros_like(g_ref)

        win = pl.ds(q0, BWIN)
        kw = kpad[win, :]
        vw = vpad[win, :]
        inside = lax.broadcasted_iota(jnp.int32, (BQ, BWIN), 1) >= B_LEFT - q0
        qall = q_ref[...]
        dov = do_ref[...]
        dd = dov * o_ref[...]
        stv = st_ref[...]

        def run(heads):
            dq = jnp.zeros((BQ, 128), F32)
            for j in heads:
                hm = half == bool(j)
                qh = jnp.where(hm, qall, jnp.zeros_like(qall))
                delta = jnp.sum(jnp.where(hm, dd, 0.0), axis=-1, keepdims=True)
                s = jnp.where(inside, _nt(qh, kw) * scale + tb_ref[j], NEG)
                p = jnp.exp(s - stv[:, j:j + 1])
                doh = jnp.where(hm, dov, 0.0).astype(BF16)
                ds = p * (_nt(doh, vw) - delta)
                g_ref[j] += ds
                dsb = (ds * scale).astype(BF16)
                dvpad[win, :] += _tn(p.astype(BF16), doh)
                dkpad[win, :] += _tn(dsb, qh)
                dq = dq + jnp.where(hm, jnp.dot(dsb, kw, preferred_element_type=F32), 0.0)
            dq_ref[...] = dq.astype(BF16)

        last = pl.program_id(0) == B_HEADS // 2
        pl.when(jnp.logical_not(last))(lambda: run((0, 1)))
        pl.when(last)(lambda: run((0,)))

        @pl.when(qi == nq - 1)
        def _():
            dk_ref[...] = dkpad[B_LEFT:, :].astype(BF16)
            dv_ref[...] = dvpad[B_LEFT:, :].astype(BF16)

    qblk = pl.BlockSpec((BQ, 128), lambda p, b, i: (b * nq + i, p))
    full = pl.BlockSpec((seq, 128), lambda p, b, i: (b, p))
    tblk = pl.BlockSpec((2, BQ, BWIN), lambda p, b, i: (p, 0, 0))
    body, in_specs, args = _after(dep, body, [qblk, full, full, tblk, qblk, qblk, qblk], [q, k, v, table, o, st, do])
    return pl.pallas_call(
        body, name="band_bwd", grid=(3, nb, nq),
        in_specs=in_specs,
        out_specs=[qblk, full, full, tblk],
        out_shape=[jax.ShapeDtypeStruct((t, GW), BF16), jax.ShapeDtypeStruct((t, GW), BF16),
                   jax.ShapeDtypeStruct((t, GW), BF16), jax.ShapeDtypeStruct((6, BQ, BWIN), F32)],
        scratch_shapes=[pltpu.VMEM((seq + B_LEFT, 128), BF16), pltpu.VMEM((seq + B_LEFT, 128), BF16),
                        pltpu.VMEM((seq + B_LEFT, 128), F32), pltpu.VMEM((seq + B_LEFT, 128), F32)],
        compiler_params=_cparams(("arbitrary", "arbitrary", "arbitrary")),
    )(*args)


def _fox_prep(cf, fb, seq):
    nb = cf.shape[0] // seq
    nblk = seq // 128

    def body(cf_ref, fb_ref, f_ref):
        x = cf_ref[...] + fb_ref[...]
        lf = jnp.minimum(x, 0.0) - jnp.log1p(jnp.exp(-jnp.abs(x)))
        rows = lf.T[0:8, :]
        upper = (lax.broadcasted_iota(jnp.int32, (128, 128), 0)
                 <= lax.broadcasted_iota(jnp.int32, (128, 128), 1)).astype(F32)
        carry = jnp.zeros((8, 1), F32)
        for blk in range(nblk):
            sl = slice(blk * 128, (blk + 1) * 128)
            cs = jnp.dot(rows[:, sl], upper, precision=HI, preferred_element_type=F32) + carry
            carry = cs[:, 127:128]
            f_ref[0, 0, :, sl] = cs
            f_ref[0, 1, :, sl] = pltpu.roll(cs, 6, 0)
            f_ref[0, 2, :, sl] = pltpu.roll(cs, 4, 0)

    return pl.pallas_call(
        body, name="fox_prep", grid=(nb,),
        in_specs=[pl.BlockSpec((seq, 128), lambda b: (b, 0)), pl.BlockSpec((1, 128), lambda b: (0, 0))],
        out_specs=pl.BlockSpec((1, 3, 8, seq), lambda b: (b, 0, 0, 0)),
        out_shape=jax.ShapeDtypeStruct((nb, 3, 8, seq), F32),
        compiler_params=_cparams(("arbitrary",)),
    )(cf, fb)


def _fox_prep_bwd(df, dfq, cf, fb, seq):
    nb = cf.shape[0] // seq
    nblk = seq // 128

    def body(df_ref, dfq_ref, cf_ref, fb_ref, dcf_ref, dfb_ref, wide):
        b = pl.program_id(0)
        row = lax.broadcasted_iota(jnp.int32, (8, seq), 0)
        dfh = None
        for p in range(3):
            both = df_ref[0, p] + dfq_ref[:, p * 128:(p + 1) * 128].T[0:8, :]
            both = jnp.where(row < 2, both, 0.0)
            if p:
                both = pltpu.roll(both, 2 * p, 0)
            dfh = both if dfh is None else dfh + both
        lower = (lax.broadcasted_iota(jnp.int32, (128, 128), 0)
                 >= lax.broadcasted_iota(jnp.int32, (128, 128), 1)).astype(F32)
        wide[...] = jnp.zeros_like(wide)
        carry = jnp.zeros((8, 1), F32)
        for blk in reversed(range(nblk)):
            sl = slice(blk * 128, (blk + 1) * 128)
            rc = jnp.dot(dfh[:, sl], lower, precision=HI, preferred_element_type=F32) + carry
            carry = rc[:, 0:1]
            wide[0:8, sl] = rc
        dl = wide[...].T
        x = cf_ref[...] + fb_ref[...]
        dcf = dl * (1.0 / (1.0 + jnp.exp(x)))
        dcf_ref[...] = dcf.astype(BF16)
        part = jnp.sum(dcf, axis=0, keepdims=True)

        @pl.when(b == 0)
        def _():
            dfb_ref[...] = part

        @pl.when(b != 0)
        def _():
            dfb_ref[...] += part

    return pl.pallas_call(
        body, name="fox_prep_bwd", grid=(nb,),
        in_specs=[pl.BlockSpec((1, 3, 8, seq), lambda b: (b, 0, 0, 0)), pl.BlockSpec((seq, GW), lambda b: (b, 0)),
                  pl.BlockSpec((seq, 128), lambda b: (b, 0)), pl.BlockSpec((1, 128), lambda b: (0, 0))],
        out_specs=[pl.BlockSpec((seq, 128), lambda b: (b, 0)), pl.BlockSpec((1, 128), lambda b: (0, 0))],
        out_shape=[jax.ShapeDtypeStruct(cf.shape, BF16), jax.ShapeDtypeStruct((1, 128), F32)],
        scratch_shapes=[pltpu.VMEM((128, seq), F32)],
        compiler_params=_cparams(("arbitrary",)),
    )(df, dfq, cf, fb)


def _gate_out(oa, ob, oc, gates, w, x, gate, seq, tm=ROW_TILE):
    t = x.shape[0]
    tm = min(tm, seq)
    tps = seq // tm

    def body(oa_ref, ob_ref, oc_ref, g_ref, w_ref, x_ref, gt_ref, xo_ref, y_ref, u_ref):
        for n, o_ref in enumerate((oa_ref, ob_ref, oc_ref)):
            sl = slice(n * GW, (n + 1) * GW)
            gv = g_ref[:, sl].astype(F32)
            u_ref[:, sl] = (o_ref[...] * (gv * _sigmoid(gv))).astype(BF16)
        y = jnp.dot(u_ref[...], w_ref[...], preferred_element_type=F32)
        y_ref[...] = y.astype(BF16)
        xo_ref[...] = x_ref[...] + gt_ref[0] * y

    row = lambda wd: pl.BlockSpec((tm, wd), lambda i: (i, 0))
    return pl.pallas_call(
        body, name="gate_out", grid=(t // tm,),
        in_specs=[row(GW), row(GW), row(GW), row(U_PAD), pl.BlockSpec((U_PAD, D_MODEL), lambda i: (0, 0)),
                  row(D_MODEL), pl.BlockSpec((1, 1, D_MODEL), lambda i: (i // tps, 0, 0))],
        out_specs=[row(D_MODEL), row(D_MODEL), row(U_PAD)],
        out_shape=[jax.ShapeDtypeStruct((t, D_MODEL), F32), jax.ShapeDtypeStruct((t, D_MODEL), BF16),
                   jax.ShapeDtypeStruct((t, U_PAD), BF16)],
        compiler_params=_cparams(("arbitrary",)),
    )(oa, ob, oc, gates, w, x, gate)


def _gate_out_bwd(dxo, y, gate, oa, ob, oc, gates, w_t, seq, tm=ROW_TILE, dep=None):
    t = dxo.shape[0]
    tm = min(tm, seq)
    tps = seq // tm
    nb = t // seq

    def body(dxo_ref, y_ref, gt_ref, oa_ref, ob_ref, oc_ref, g_ref, wt_ref,
             dy_ref, doa_ref, dob_ref, doc_ref, dg_ref, dgt_ref):
        i = pl.program_id(0)
        dxo_v = dxo_ref[...]
        dgt = jnp.sum(dxo_v * y_ref[...].astype(F32), axis=0, keepdims=True)
        dyb = (dxo_v * gt_ref[0]).astype(BF16)
        dy_ref[...] = dyb
        du = _nt(dyb, wt_ref[...])
        for n, (o_ref, do_ref) in enumerate(((oa_ref, doa_ref), (ob_ref, dob_ref), (oc_ref, doc_ref))):
            sl = slice(n * GW, (n + 1) * GW)
            gv = g_ref[:, sl].astype(F32)
            sg = _sigmoid(gv)
            dun = du[:, sl]
            do_ref[...] = dun * (gv * sg)
            dg_ref[:, sl] = (dun * o_ref[...] * (sg * (1.0 + gv * (1.0 - sg)))).astype(BF16)

        @pl.when(i % tps == 0)
        def _():
            dgt_ref[0] = dgt

        @pl.when(i % tps != 0)
        def _():
            dgt_ref[0] += dgt

    row = lambda wd: pl.BlockSpec((tm, wd), lambda i: (i, 0))
    per_b = pl.BlockSpec((1, 1, D_MODEL), lambda i: (i // tps, 0, 0))
    in_specs = [row(D_MODEL), row(D_MODEL), per_b, row(GW), row(GW), row(GW), row(U_PAD),
                pl.BlockSpec((U_PAD, D_MODEL), lambda i: (0, 0))]
    body, in_specs, args = _after(dep, body, in_specs, [dxo, y, gate, oa, ob, oc, gates, w_t])
    return pl.pallas_call(
        body, name="gate_out_bwd", grid=(t // tm,), in_specs=in_specs,
        out_specs=[row(D_MODEL), row(GW), row(GW), row(GW), row(U_PAD), per_b],
        out_shape=[jax.ShapeDtypeStruct((t, D_MODEL), BF16), jax.ShapeDtypeStruct((t, GW), F32),
                   jax.ShapeDtypeStruct((t, GW), F32), jax.ShapeDtypeStruct((t, GW), F32),
                   jax.ShapeDtypeStruct((t, U_PAD), BF16), jax.ShapeDtypeStruct((nb, 1, D_MODEL), F32)],
        compiler_params=_cparams(("arbitrary",)),
    )(*args)


def _final_loss(x, target, g, tm=ROW_TILE):
    t = x.shape[0]
    tm = min(tm, t)

    def body(x_ref, t_ref, g_ref, dx_ref, loss_ref, dg_ref):
        i = pl.program_id(0)
        xv = x_ref[...]
        rstd = lax.rsqrt(jnp.mean(xv * xv, axis=-1, keepdims=True) + EPS)
        xn = xv * rstd
        gv = g_ref[...]
        err = xn * gv - t_ref[...]
        dy = err * (1.0 / D_MODEL)
        dxn = dy * gv
        dx_ref[...] = rstd * (dxn - xn * jnp.mean(dxn * xn, axis=-1, keepdims=True))
        lp = jnp.sum(err * err, axis=0, keepdims=True) * (0.5 / D_MODEL)
        dgp = jnp.sum(dy * xn, axis=0, keepdims=True)

        @pl.when(i == 0)
        def _():
            loss_ref[...] = lp
            dg_ref[...] = dgp

        @pl.when(i != 0)
        def _():
            loss_ref[...] += lp
            dg_ref[...] += dgp

    row = pl.BlockSpec((tm, D_MODEL), lambda i: (i, 0))
    vec = pl.BlockSpec((1, D_MODEL), lambda i: (0, 0))
    return pl.pallas_call(
        body, name="final_loss", grid=(t // tm,),
        in_specs=[row, row, vec], out_specs=[row, vec, vec],
        out_shape=[jax.ShapeDtypeStruct((t, D_MODEL), F32), jax.ShapeDtypeStruct((1, D_MODEL), F32),
                   jax.ShapeDtypeStruct((1, D_MODEL), F32)],
        compiler_params=_cparams(("arbitrary",)),
    )(x, target, g)


def _adamw(w, gslots, m, v, name, tr=None):
    nl, r, c = w.shape
    ns = gslots.shape[0]
    tr = r if tr is None else tr

    def body(w_ref, g_ref, m_ref, v_ref, go_ref, d_ref, mo_ref, vo_ref):
        g = g_ref[0].astype(F32)
        for j in range(1, ns):
            g = g + g_ref[j].astype(F32)
        mn = ADAM_B1 * m_ref[...] + (1.0 - ADAM_B1) * g
        vn = ADAM_B2 * v_ref[...] + (1.0 - ADAM_B2) * jnp.square(g)
        m_hat = mn / (1.0 - ADAM_B1 ** ADAM_STEP)
        v_hat = vn / (1.0 - ADAM_B2 ** ADAM_STEP)
        go_ref[...] = g
        d_ref[...] = -ADAM_LR * (m_hat / (jnp.sqrt(v_hat) + ADAM_EPS) + ADAM_WD * w_ref[...])
        mo_ref[...] = mn
        vo_ref[...] = vn

    blk = pl.BlockSpec((1, tr, c), lambda l, i: (l, i, 0))
    return pl.pallas_call(
        body, name=name, grid=(nl, r // tr),
        in_specs=[blk, pl.BlockSpec((ns, 1, tr, c), lambda l, i: (0, l, i, 0)), blk, blk],
        out_specs=[blk] * 4, out_shape=[jax.ShapeDtypeStruct((nl, r, c), F32)] * 4,
        compiler_params=_cparams(("arbitrary", "arbitrary")),
    )(w, gslots, m, v)


def _rope_tables(positions):
    inv = ROPE_THETA ** (-jnp.arange(0, A_ROPE, 2, dtype=F32) / A_ROPE)
    ang = positions.astype(F32)[:, None] * inv
    cos, sin = jnp.cos(ang), jnp.sin(ang)
    t = positions.shape[0]
    one = jnp.ones((t, 64), F32)
    zero16 = jnp.zeros((t, 16), F32)
    cos_t = jnp.concatenate([one, cos, cos, jnp.ones((t, 32), F32)], axis=1)
    sin_a = jnp.concatenate([jnp.zeros((t, 64), F32), -sin, zero16, jnp.zeros((t, 32), F32)], axis=1)
    sin_b = jnp.concatenate([jnp.zeros((t, 64), F32), zero16, sin, jnp.zeros((t, 32), F32)], axis=1)
    return cos_t, sin_a, sin_b


def _pad_heads(w, real, padded, nheads, axis):
    shp = w.shape[:axis] + (nheads, real) + w.shape[axis + 1:]
    w = w.reshape(shp)
    pad = [(0, 0)] * w.ndim
    pad[axis + 1] = (0, padded - real)
    w = jnp.pad(w, pad)
    return w.reshape(w.shape[:axis] + (nheads * padded,) + w.shape[axis + 2:])


def kernel(x, c, positions, w_ada, b_ada, norm_g, w_in, a_q_norm_g, a_w_uq, a_kv_norm_g, a_w_ukv, b_rel_bias, c_forget_b, w_out, final_g, loss_target, m_w_ada, m_b_ada, m_norm_g, m_w_in, m_a_q_norm_g, m_a_w_uq, m_a_kv_norm_g, m_a_w_ukv, m_b_rel_bias, m_c_forget_b, m_w_out, m_final_g, v_w_ada, v_b_ada, v_norm_g, v_w_in, v_a_q_norm_g, v_a_w_uq, v_a_kv_norm_g, v_a_w_ukv, v_b_rel_bias, v_c_forget_b, v_w_out, v_final_g):
    nb, seq, _ = x.shape
    t = nb * seq
    me = 4 * lax.axis_index("x") + 2 * lax.axis_index("y") + lax.axis_index("c")
    x2 = x.reshape(t, D_MODEL)
    tgt = loss_target.reshape(t, D_MODEL)
    cos_t, sin_a, sin_b = _rope_tables(positions.reshape(t))

    def shards(l):
        return [_pad_runs(w_in[l].astype(BF16), IN_RUNS, N_PAD, 1), w_out[l].astype(BF16),
                a_w_uq[l].astype(BF16), a_w_ukv[l].astype(BF16)]

    def prepare(gi, go, gq, gkv):
        return dict(w_in=gi.reshape(D_MODEL, N_PAD), **prepare_rest(go, gq, gkv))

    def prepare_rest(go, gq, gkv):
        wo = _pad_runs(go.reshape(D_MODEL, D_MODEL), OUT_RUNS, U_PAD, 0)
        wq = jnp.transpose(gq, (1, 0, 2)).reshape(A_Q_RANK, A_HEADS * (A_NOPE + A_ROPE))
        wq = _pad_heads(wq, A_NOPE + A_ROPE, HEAD_PAD, A_HEADS, 1)
        wkv = jnp.transpose(gkv, (1, 0, 2)).reshape(A_KV_RANK, A_HEADS, 2 * A_NOPE)
        wk = jnp.pad(wkv[:, :, :A_NOPE], ((0, 0), (0, 0), (0, HEAD_PAD - A_NOPE))).reshape(A_KV_RANK, A_HEADS * HEAD_PAD)
        wv = wkv[:, :, A_NOPE:].reshape(A_KV_RANK, GW)
        return dict(w_out=wo, wuq=wq, wuq_t=wq.T, wk=wk, wk_t=wk.T, wv=wv, wv_t=wv.T)

    shards0 = shards(0)
    w_in0_g, c_g = _gather([shards0[0], c], "gather_w_in0")
    c_all = c_g.reshape(N_DEV * nb, D_MODEL)
    weights = [dict(w_in=w_in0_g.reshape(D_MODEL, N_PAD)), None]

    c_act, mod_cols = _ada_fwd(c_all, w_ada)
    (mod_g,) = _gather([mod_cols], "gather_mod")
    rest0, rest0_token = _split_start("gather", shards0[1:], "gather_rest0_start", after=mod_g)
    mod_all = jnp.transpose(mod_g, (1, 2, 0, 3)).reshape(DEPTH, N_DEV * nb, 3 * D_MODEL)
    mod = lax.dynamic_slice_in_dim(mod_all, me * nb, nb, axis=1) + b_ada[:, None, :]

    fb_pad = jnp.pad(c_forget_b, ((0, 0), (0, 128 - C_HEADS)))
    a_scale = (A_NOPE + A_ROPE) ** -0.5
    h_scale = CHUNK ** -0.5

    saved = []
    xl = x2
    for l in range(DEPTH):
        if l == 1:
            weights[1] = prepare(*_split_wait(gather1, xl, "gather_weights1_wait")[1])
        w = weights[l]
        shift, scale, gate = mod[l, :, :D_MODEL], mod[l, :, D_MODEL:2 * D_MODEL], mod[l, :, 2 * D_MODEL:]
        ss = jnp.stack([shift, 1.0 + scale], axis=1)
        gate3 = gate[:, None, :]
        h, cq, ckv, kpe, gates, bq, bk, bv, cq2, ck, cv, cf = _ln_in(
            xl, ss, norm_g[l:l + 1], w["w_in"], seq, dep=rest0_token if l == 0 else None)
        gather1_token = None
        if l == 0:
            w.update(prepare_rest(*_split_wait(rest0, h, "gather_rest0_wait")[1]))
            gather1, gather1_token = _split_start("gather", shards(1), "gather_weights1_start", after=w["w_out"])
        q, k, v, cqn, ckvn = _mla_prep(cq, ckv, kpe, a_q_norm_g[l:l + 1], a_kv_norm_g[l:l + 1],
                                       w["wuq"], w["wk"], w["wv"], cos_t, sin_a, sin_b, dep=gather1_token)
        oa, sta = _attn_fwd("mla", q, k, v, None, seq, a_scale)
        table = _band_table(jnp.pad(b_rel_bias[l], ((0, 8 - B_HEADS), (0, GW - N_REL))))
        ob, stb = _band_fwd(bq, bk, bv, table, seq, h_scale)
        fcum = _fox_prep(cf, fb_pad[l:l + 1], seq)
        oc, stc = _attn_fwd("fox", cq2, ck, cv, fcum, seq, h_scale)
        xn, y, u = _gate_out(oa, ob, oc, gates, w["w_out"], xl, gate3, seq)
        saved.append(dict(x=xl, ss=ss, gate3=gate3, h=h, cq=cq, ckv=ckv, gates=gates, bq=bq, bk=bk, bv=bv,
                          cq2=cq2, ck=ck, cv=cv, cf=cf, q=q, k=k, v=v, cqn=cqn, ckvn=ckvn, oa=oa, sta=sta,
                          table=table, ob=ob, stb=stb, fcum=fcum, oc=oc, stc=stc, y=y, u=u))
        xl = xn

    dx, loss_lanes, g_final = _final_loss(xl, tgt, final_g[None, :])
    loss = lax.psum(jnp.sum(loss_lanes), AXES)

    rows = D_MODEL // N_DEV
    core = lax.axis_index("c").astype(jnp.int32).reshape(1)
    n_seg_a = 4
    dmods, smalls, parts = [None] * DEPTH, [None] * DEPTH, [None] * DEPTH
    pair1 = chips1 = pair1_token = chips1_token = None
    for l in reversed(range(DEPTH)):
        s, w = saved[l], weights[l]
        dy, doa, dob, doc, dgates, dgate = _gate_out_bwd(dx, s["y"], s["gate3"], s["oa"], s["ob"], s["oc"],
                                                         s["gates"], w["w_out"], seq, dep=pair1_token)
        g_out = _unpad_runs(_matmul_tn(s["u"], dy, "dw_out"), OUT_RUNS, 0)
        if l == 0:
            own, from_sib = _split_wait(pair1, g_out, "grads1_pair_wait")
            chips1, chips1_token = _split_start("chips", _pair_add(core, own, from_sib, "grads1_add"), "grads1_chips_start")
        dq, dk, dv = _attn_bwd("mla", s["q"], s["k"], s["v"], None, s["oa"], s["sta"], doa, seq, a_scale,
                               dep=chips1_token)
        dbq, dbk, dbv, gtab = _band_bwd(s["bq"], s["bk"], s["bv"], s["table"], s["ob"], s["stb"], dob, seq, h_scale,
                                        dep=chips1_token)
        g_rel = _band_table_bwd(gtab)[:, 0, :N_REL]
        dcq2, dck, dcv, dfc, dfq = _attn_bwd("fox", s["cq2"], s["ck"], s["cv"], s["fcum"], s["oc"], s["stc"], doc,
                                             seq, h_scale, dep=chips1_token)
        dcf, dfb = _fox_prep_bwd(dfc, dfq, s["cf"], fb_pad[l:l + 1], seq)
        dcq, dckv, dkpe, dqlin, dklin, dgq, dgkv = _mla_prep_bwd(
            dq, dk, dv, s["cq"], s["ckv"], a_q_norm_g[l:l + 1], a_kv_norm_g[l:l + 1],
            w["wuq_t"], w["wk_t"], w["wv_t"], cos_t, sin_a, sin_b)
        gq_pad = _matmul_tn(s["cqn"], dqlin, "dw_uq")
        g_uq = gq_pad.reshape(A_Q_RANK, A_HEADS, HEAD_PAD)[:, :, :A_NOPE + A_ROPE].reshape(A_Q_RANK, -1)
        gkv_pad = _matmul_tn(s["ckvn"], [dklin, dv], "dw_ukv")
        gk_pad = gkv_pad[:, :A_HEADS * HEAD_PAD].reshape(A_KV_RANK, A_HEADS, HEAD_PAD)[:, :, :A_NOPE]
        gv_pad = gkv_pad[:, A_HEADS * HEAD_PAD:].reshape(A_KV_RANK, A_HEADS, A_NOPE)
        g_ukv = jnp.concatenate([gk_pad, gv_pad], axis=2).reshape(A_KV_RANK, -1)
        dz = [dcq, dckv, dkpe, dgates, dbq, dbk, dbv, dcq2, dck, dcv, dcf]
        g_in_a = _matmul_tn(s["h"], dz[:n_seg_a], "dw_in_a")
        first = [g_in_a.reshape(N_DEV, rows, -1), g_out.reshape(N_DEV, rows, D_MODEL),
                 g_uq.reshape(A_Q_RANK, N_DEV, -1).transpose(1, 0, 2), g_ukv.reshape(A_KV_RANK, N_DEV, -1).transpose(1, 0, 2)]
        if l == 1:
            g_in_b = _matmul_tn(s["h"], dz[n_seg_a:], "dw_in_b")
            pair1, pair1_token = _split_start("pair", first + [g_in_b.reshape(N_DEV, rows, -1)], "grads1_pair_start")
            tail_token = None
        else:
            pair0a, pair0a_token = _split_start("pair", first, "grads0a_pair_start")
            g_in_b = _matmul_tn(s["h"], dz[n_seg_a:], "dw_in_b", dep=pair0a_token)
            own, from_sib = _split_wait(pair0a, g_in_b, "grads0a_pair_wait")
            sums0a = _pair_add(core, own, from_sib, "grads0a_add")
            pair0b, pair0b_token = _split_start("pair", [g_in_b.reshape(N_DEV, rows, -1)], "grads0b_pair_start",
                                                after=sums0a[0])
            chips0a, tail_token = _split_start("chips", sums0a, "grads0a_chips_start", after=pair0b_token)
        dx, dss, dg_norm = _ln_in_bwd(dz, w["w_in"], s["x"], s["ss"], norm_g[l:l + 1], dx, seq, dep=tail_token)
        dmods[l] = jnp.concatenate([dss[:, 0, :], dss[:, 1, :], dgate[:, 0, :]], axis=1)
        smalls[l] = [dg_norm.reshape(-1), dgq.reshape(-1), dgkv.reshape(-1), g_rel.reshape(-1),
                     dfb[0, :C_HEADS]]
    grad_x = dx.reshape(nb, seq, D_MODEL)
    parts[1] = _split_wait(chips1, dx, "grads1_chips_wait")[1]
    parts0a = _split_wait(chips0a, dx, "grads0a_chips_wait")[1]
    own, from_sib = _split_wait(pair0b, dx, "grads0b_pair_wait")

    small = jnp.concatenate([p for l in range(DEPTH) for p in smalls[l]] + [g_final.reshape(-1)])
    n_small = small.shape[0]
    small_rows = -(-n_small // 1024) * 8
    small = jnp.pad(small, (0, small_rows * 128 - n_small)).reshape(small_rows, 128)
    dmod_local = jnp.stack(dmods)
    dmod_g, small_g = _gather([dmod_local, small], "gather_small", dep=parts0a[0])
    chips0, chips0_token = _split_start("chips", _pair_add(core, own, from_sib, "grads0b_add"), "grads0b_chips_start",
                                        after=small_g)
    dmod_all = jnp.transpose(dmod_g, (1, 0, 2, 3)).reshape(DEPTH, N_DEV * nb, 3 * D_MODEL)
    cols = 3 * D_MODEL // N_DEV
    dmod_mine = lax.dynamic_slice_in_dim(dmod_all, me * cols, cols, axis=2)
    g_w_ada, g_b_ada = _ada_bwd(c_act, dmod_all, dmod_mine, chips0_token)
    small_sum = _sum_slots(small_g, "sum_small").reshape(-1)

    def split_small():
        out, pos = [], 0
        sizes = [D_MODEL, A_Q_RANK, A_KV_RANK, B_HEADS * N_REL, C_HEADS]
        per_layer = []
        for l in range(DEPTH):
            parts = []
            for sz in sizes:
                parts.append(small_sum[pos:pos + sz])
                pos += sz
            per_layer.append(parts)
        for j in range(len(sizes)):
            out.append(jnp.stack([per_layer[l][j] for l in range(DEPTH)]))
        out.append(small_sum[pos:pos + D_MODEL])
        return out

    g_norm, g_qn, g_kvn, g_relb, g_fb, g_fin = split_small()

    def adam(w, g, m, v, name, tr=None):
        shp = w.shape
        w3 = w.reshape((1,) * (3 - w.ndim) + shp)
        outs = _adamw(w3, g.reshape((-1,) + w3.shape), m.reshape(w3.shape), v.reshape(w3.shape), name, tr)
        return [o.reshape(shp) for o in outs]

    res = {
        "w_ada": adam(w_ada, g_w_ada, m_w_ada, v_w_ada, "adam_w_ada", 256),
        "b_ada": adam(b_ada, g_b_ada, m_b_ada, v_b_ada, "adam_b_ada"),
        "norm_g": adam(norm_g, g_norm, m_norm_g, v_norm_g, "adam_norm_g"),
        "a_q_norm_g": adam(a_q_norm_g, g_qn, m_a_q_norm_g, v_a_q_norm_g, "adam_q_norm"),
        "a_kv_norm_g": adam(a_kv_norm_g, g_kvn, m_a_kv_norm_g, v_a_kv_norm_g, "adam_kv_norm"),
        "b_rel_bias": adam(b_rel_bias, g_relb.reshape(b_rel_bias.shape), m_b_rel_bias, v_b_rel_bias, "adam_rel_bias"),
        "c_forget_b": adam(c_forget_b, g_fb, m_c_forget_b, v_c_forget_b, "adam_forget_b"),
        "final_g": adam(final_g, g_fin, m_final_g, v_final_g, "adam_final_g"),
    }
    parts[0] = list(parts0a) + list(_split_wait(chips0, res["w_ada"][1], "grads0b_chips_wait")[1])
    p_in = jnp.stack([_unpad_runs(jnp.concatenate([parts[l][0], parts[l][4]], axis=2), IN_RUNS, 2)
                      for l in range(DEPTH)], axis=1)
    p_out, p_uq, p_ukv = (jnp.stack([parts[l][j] for l in range(DEPTH)], axis=1) for j in (1, 2, 3))
    res.update({
        "w_in": adam(w_in, p_in, m_w_in, v_w_in, "adam_w_in", 32),
        "a_w_uq": adam(a_w_uq, p_uq, m_a_w_uq, v_a_w_uq, "adam_w_uq"),
        "a_w_ukv": adam(a_w_ukv, p_ukv, m_a_w_ukv, v_a_w_ukv, "adam_w_ukv"),
        "w_out": adam(w_out, p_out, m_w_out, v_w_out, "adam_w_out", 64),
    })
    names = ["w_ada", "b_ada", "norm_g", "w_in", "a_q_norm_g", "a_w_uq", "a_kv_norm_g", "a_w_ukv", "b_rel_bias",
             "c_forget_b", "w_out", "final_g"]
    outs = [loss, grad_x]
    for j in range(4):
        outs += [res[n][j] for n in names]
    return tuple(outs)
```

```python
import math

import jax
import jax.numpy as jnp
from jax import lax
from jax.experimental import pallas as pl
from jax.experimental.pallas import tpu as pltpu

F32 = jnp.float32
BF16 = jnp.bfloat16
HI = lax.Precision.HIGHEST

N_DEV = 8
AXES = ("x", "y", "c")
D_MODEL = 1024
DEPTH = 2
CHUNK = 64
EPS = 1e-6
NEG = -1e30
A_HEADS = 6
A_NOPE = 64
A_ROPE = 32
A_Q_RANK = 384
A_KV_RANK = 256
ROPE_THETA = 10000.0
B_HEADS = 5
B_LEFT = 512
REL_CLIP = 128
N_REL = 2 * REL_CLIP + 1
C_HEADS = 5
HEAD_PAD = 128
GW = 384
N_IN = 3621
ADAM_LR = 0.001
ADAM_B1 = 0.9
ADAM_B2 = 0.999
ADAM_EPS = 1e-08
ADAM_WD = 0.01
ADAM_STEP = 10
VMEM_LIMIT = 56 * 1024 * 1024
ROW_TILE = 512

Z_SEGS = (
    ("cq", 0, 384, F32), ("ckv", 384, 256, F32), ("kpe", 640, 128, F32), ("gates", 768, 1152, BF16),
    ("bq", 1920, 384, BF16), ("bk", 2304, 384, BF16), ("bv", 2688, 384, BF16),
    ("cq2", 3072, 384, BF16), ("ck", 3456, 384, BF16), ("cv", 3840, 384, BF16), ("cf", 4224, 128, F32),
)
N_PAD = 4352
IN_RUNS = (
    (0, 384, 0), (384, 256, 384), (640 + 64, 32, 640),
    (768, 384, 672), (768 + 384, 320, 2016), (768 + 768, 320, 3301),
    (1920, 320, 1056), (2304, 320, 1376), (2688, 320, 1696),
    (3072, 320, 2336), (3456, 320, 2656), (3840, 320, 2976), (4224, 5, 3296),
)
OUT_RUNS = ((0, 384, 0), (384, 320, 384), (768, 320, 704))
U_PAD = 1152


def _cparams(sem=None, vmem=VMEM_LIMIT):
    return pltpu.CompilerParams(dimension_semantics=sem, vmem_limit_bytes=vmem)


def _after(dep, body, in_specs, args):
    if dep is None:
        return body, in_specs, args
    n = len(args)

    def ordered(*refs):
        return body(*refs[:n], *refs[n + 1:])

    return ordered, list(in_specs) + [pl.BlockSpec((8, 128), lambda *_: (0, 0))], list(args) + [dep]


def _pad_runs(w, runs, total, axis):
    order = sorted(runs)
    parts, pos = [], 0
    for off, wd, src in order:
        if off > pos:
            shp = list(w.shape)
            shp[axis] = off - pos
            parts.append(jnp.zeros(shp, w.dtype))
        parts.append(lax.slice_in_dim(w, src, src + wd, axis=axis))
        pos = off + wd
    if pos < total:
        shp = list(w.shape)
        shp[axis] = total - pos
        parts.append(jnp.zeros(shp, w.dtype))
    return jnp.concatenate(parts, axis=axis)


def _unpad_runs(w, runs, axis):
    order = sorted(runs, key=lambda r: r[2])
    return jnp.concatenate([lax.slice_in_dim(w, off, off + wd, axis=axis) for off, wd, _ in order], axis=axis)


def _sigmoid(x):
    return 1.0 / (1.0 + jnp.exp(-x))


N_CHIP = 4
ANY_SPEC = pl.BlockSpec(memory_space=pl.ANY)
MESH_ID = pl.DeviceIdType.MESH


def _gather(arrs, name, dep=None):
    n = len(arrs)
    nin = n + (dep is not None)

    def body(*refs):
        ins, outs = refs[:n], refs[nin:nin + n]
        send_sems, recv_sems, local_sems = refs[nin + n:]
        x, y, c = lax.axis_index("x"), lax.axis_index("y"), lax.axis_index("c")
        me, sib = (x, y, c), (x, y, 1 - c)
        chips = [(1 - x, y), (x, 1 - y), (1 - x, 1 - y)]

        def slot(px, py, pc):
            return 4 * px + 2 * py + pc

        def copy(a, k, block, to, src=None):
            dst = outs[a].at[slot(*block)]
            return pltpu.make_async_remote_copy(
                src_ref=dst if src is None else src, dst_ref=dst, send_sem=send_sems.at[a, k],
                recv_sem=recv_sems.at[a, k], device_id=to, device_id_type=MESH_ID)

        local = [pltpu.make_async_copy(ins[a], outs[a].at[slot(*me)], local_sems.at[a]) for a in range(n)]
        first = []
        for a in range(n):
            first.append(copy(a, 0, me, sib, src=ins[a]))
            first += [copy(a, 1 + j, me, (*chip, c), src=ins[a]) for j, chip in enumerate(chips)]
        for cp in local + first:
            cp.start()
        passed = []
        for j, chip in enumerate(chips):
            for a in range(n):
                copy(a, 1 + j, (*chip, c), me).wait_recv()
                fwd = copy(a, 4 + j, (*chip, c), sib)
                fwd.start()
                passed.append(fwd)
        for a in range(n):
            copy(a, 0, sib, me).wait_recv()
            for j, chip in enumerate(chips):
                copy(a, 4 + j, (*chip, 1 - c), me).wait_recv()
        for cp in first + passed:
            cp.wait_send()
        for cp in local:
            cp.wait()

    return pl.pallas_call(
        body, name=name, out_shape=[jax.ShapeDtypeStruct((N_DEV,) + a.shape, a.dtype) for a in arrs],
        in_specs=[ANY_SPEC] * nin, out_specs=[ANY_SPEC] * n,
        scratch_shapes=[pltpu.SemaphoreType.DMA((n, N_DEV - 1)), pltpu.SemaphoreType.DMA((n, N_DEV - 1)),
                        pltpu.SemaphoreType.DMA((n,))],
    )(*arrs, *([] if dep is None else [dep]))


HBM_SPEC = pl.BlockSpec(memory_space=pltpu.HBM)
SEM_SPEC = pl.BlockSpec(memory_space=pltpu.SEMAPHORE)
SPLIT_EFFECT = pltpu.SideEffectType.DATAFLOW_SIDE_EFFECTING
SPLIT_SEMS = {"gather": (N_DEV - 1, True), "pair": (N_CHIP, False), "chips": (N_CHIP - 1, True)}


def _split_descriptors(pattern, srcs, lands, sems):
    x, y, c = lax.axis_index("x"), lax.axis_index("y"), lax.axis_index("c")
    nsem, has_local = SPLIT_SEMS[pattern]
    per = 2 * nsem + int(has_local)
    starts, arrivals, local = [], [], []

    def remote(a, k, src, dst, to):
        return pltpu.make_async_remote_copy(src_ref=src, dst_ref=dst, send_sem=sems[a * per + k],
                                            recv_sem=sems[a * per + nsem + k], device_id=to, device_id_type=MESH_ID)

    for a in range(len(srcs)):
        if pattern == "gather":
            me = 4 * x + 2 * y + c
            local.append(pltpu.make_async_copy(srcs[a], lands[a].at[me], sems[a * per + 2 * nsem]))
            for k in range(1, N_DEV):
                px = (1 - x) if (k >> 2) & 1 else x
                py = (1 - y) if (k >> 1) & 1 else y
                pc = (1 - c) if k & 1 else c
                starts.append(remote(a, k - 1, srcs[a], lands[a].at[me], (px, py, pc)))
                arrivals.append(remote(a, k - 1, srcs[a], lands[a].at[4 * px + 2 * py + pc], (px, py, pc)))
        elif pattern == "pair":
            for q in range(N_CHIP):
                cp = remote(a, q, srcs[a].at[2 * q + 1 - c], lands[a].at[q], (x, y, 1 - c))
                starts.append(cp)
                arrivals.append(cp)
        else:
            mine = 2 * x + y
            local.append(pltpu.make_async_copy(srcs[a].at[mine], lands[a].at[mine], sems[a * per + 2 * nsem]))
            for k in range(1, N_CHIP):
                px = (1 - x) if (k >> 1) & 1 else x
                py = (1 - y) if k & 1 else y
                starts.append(remote(a, k - 1, srcs[a].at[2 * px + py], lands[a].at[mine], (px, py, c)))
                arrivals.append(remote(a, k - 1, srcs[a].at[2 * px + py], lands[a].at[2 * px + py], (px, py, c)))
    return starts, arrivals, local


def _split_start(pattern, arrs, name, after=None):
    n = len(arrs)
    extra = [] if after is None else [after]
    nsem, has_local = SPLIT_SEMS[pattern]
    if pattern == "gather":
        land_shapes = [(N_DEV,) + a.shape for a in arrs]
    elif pattern == "pair":
        land_shapes = [(N_CHIP,) + a.shape[1:] for a in arrs]
    else:
        land_shapes = [a.shape for a in arrs]
    nsem_out = n * (2 * nsem + int(has_local))

    def body(*refs):
        srcs, lands = refs[:n], refs[n:2 * n]
        first_sem = 2 * n + len(extra)
        sems = refs[first_sem:first_sem + nsem_out]
        token = refs[-1]
        starts, _, local = _split_descriptors(pattern, srcs, lands, sems)
        for cp in local + starts:
            cp.start()
        token[...] = jnp.zeros_like(token)

    out_shape = ([pltpu.SemaphoreType.DMA(())] * nsem_out + [pltpu.HBM(a.shape, a.dtype) for a in arrs]
                 + [pltpu.HBM(s, a.dtype) for s, a in zip(land_shapes, arrs)] + [jax.ShapeDtypeStruct((8, 128), F32)])
    ins = ([pltpu.with_memory_space_constraint(a, pltpu.HBM) for a in arrs]
           + [pltpu.with_memory_space_constraint(lax.empty(s, a.dtype), pltpu.HBM) for s, a in zip(land_shapes, arrs)])
    outs = pl.pallas_call(
        body, name=name, out_shape=out_shape, in_specs=[HBM_SPEC] * (2 * n) + [ANY_SPEC] * len(extra),
        out_specs=[SEM_SPEC] * nsem_out + [HBM_SPEC] * (2 * n) + [pl.BlockSpec(memory_space=pltpu.VMEM)],
        input_output_aliases={i: nsem_out + i for i in range(2 * n)},
        compiler_params=pltpu.CompilerParams(has_side_effects=SPLIT_EFFECT),
    )(*ins, *extra)
    handle = dict(pattern=pattern, n=n, sems=outs[:nsem_out], srcs=outs[nsem_out:nsem_out + n],
                  lands=outs[nsem_out + n:nsem_out + 2 * n])
    return handle, outs[-1]


def _split_wait(handle, after, name):
    pattern, n = handle["pattern"], handle["n"]
    nsem_in = len(handle["sems"])

    def body(*refs):
        srcs, lands = refs[:n], refs[n:2 * n]
        starts, arrivals, local = _split_descriptors(pattern, srcs, lands, refs[2 * n:2 * n + nsem_in])
        for cp in starts:
            cp.wait_send()
        for cp in arrivals:
            cp.wait_recv()
        for cp in local:
            cp.wait()

    srcs, lands = handle["srcs"], handle["lands"]
    outs = pl.pallas_call(
        body, name=name,
        out_shape=[pltpu.HBM(a.shape, a.dtype) for a in srcs] + [pltpu.HBM(a.shape, a.dtype) for a in lands],
        in_specs=[HBM_SPEC] * (2 * n) + [SEM_SPEC] * nsem_in + [ANY_SPEC], out_specs=[HBM_SPEC] * (2 * n),
        input_output_aliases={i: i for i in range(2 * n)},
        compiler_params=pltpu.CompilerParams(has_side_effects=SPLIT_EFFECT),
    )(*srcs, *lands, *handle["sems"], after)
    return outs[:n], outs[n:]


def _pair_add(core, a8s, b4s, name):
    n = len(a8s)

    def body(core_ref, *refs):
        for i in range(n):
            refs[2 * n + i][...] = (refs[i][...] + refs[n + i][...]).astype(BF16)

    own = [pl.BlockSpec((1,) + b.shape[1:], lambda q, core_ref: (2 * q + core_ref[0], 0, 0)) for b in b4s]
    slot = [pl.BlockSpec((1,) + b.shape[1:], lambda q, core_ref: (q, 0, 0)) for b in b4s]
    grid_spec = pltpu.PrefetchScalarGridSpec(num_scalar_prefetch=1, grid=(N_CHIP,), in_specs=own + slot, out_specs=slot)
    return pl.pallas_call(
        body, name=name, grid_spec=grid_spec, out_shape=[jax.ShapeDtypeStruct(b.shape, BF16) for b in b4s],
        compiler_params=_cparams(("arbitrary",)),
    )(core, *a8s, *b4s)


def _sum_slots(x, name):
    _, r, c = x.shape

    def body(x_ref, o_ref):
        acc = x_ref[0]
        for j in range(1, N_DEV):
            acc = acc + x_ref[j]
        o_ref[...] = acc

    return pl.pallas_call(body, name=name, out_shape=jax.ShapeDtypeStruct((r, c), F32))(x)


def _ada_fwd(c_all, w_ada):
    nb = c_all.shape[0]
    cols = w_ada.shape[2]

    def body(c_ref, w_ref, act_ref, mod_ref):
        cv = c_ref[...]
        act = cv * _sigmoid(cv)
        act_ref[...] = act
        for l in range(DEPTH):
            mod_ref[l] = jnp.dot(act, w_ref[l], precision=HI, preferred_element_type=F32)

    return pl.pallas_call(
        body, name="ada_fwd",
        out_shape=[jax.ShapeDtypeStruct((nb, D_MODEL), F32), jax.ShapeDtypeStruct((DEPTH, nb, cols), F32)],
        compiler_params=_cparams(),
    )(c_all, w_ada)


def _ada_bwd(c_act, dmod_all, dmod_mine, dep):
    nb = c_act.shape[0]
    cols = dmod_mine.shape[2]

    def body(act_ref, dall_ref, dmine_ref, dep_ref, gw_ref, gb_ref):
        act = act_ref[...]
        for l in range(DEPTH):
            gw_ref[l] = lax.dot_general(act, dmine_ref[l], (((0,), (0,)), ((), ())),
                                        precision=HI, preferred_element_type=F32)
            gb_ref[l:l + 1, :] = jnp.sum(dall_ref[l], axis=0, keepdims=True)

    return pl.pallas_call(
        body, name="ada_bwd",
        out_shape=[jax.ShapeDtypeStruct((DEPTH, D_MODEL, cols), F32),
                   jax.ShapeDtypeStruct((DEPTH, 3 * D_MODEL), F32)],
        compiler_params=_cparams(),
    )(c_act, dmod_all, dmod_mine, dep)


def _ln_in(x, ss, g, w, seq, tm=ROW_TILE, dep=None):
    t = x.shape[0]
    tm = min(tm, seq)
    tps = seq // tm

    def body(x_ref, ss_ref, g_ref, w_ref, h_ref, *outs):
        xv = x_ref[...]
        xn = xv * lax.rsqrt(jnp.mean(xv * xv, axis=-1, keepdims=True) + EPS)
        h = xn * g_ref[...] * ss_ref[0, 1:2, :] + ss_ref[0, 0:1, :]
        hb = h.astype(BF16)
        h_ref[...] = hb
        z = jnp.dot(hb, w_ref[...], preferred_element_type=F32)
        for o_ref, (_, off, wd, _) in zip(outs, Z_SEGS):
            o_ref[...] = z[:, off:off + wd].astype(o_ref.dtype)

    row = lambda wd: pl.BlockSpec((tm, wd), lambda i: (i, 0))
    in_specs = [row(D_MODEL), pl.BlockSpec((1, 2, D_MODEL), lambda i: (i // tps, 0, 0)),
                pl.BlockSpec((1, D_MODEL), lambda i: (0, 0)), pl.BlockSpec((D_MODEL, N_PAD), lambda i: (0, 0))]
    body, in_specs, args = _after(dep, body, in_specs, [x, ss, g, w])
    return pl.pallas_call(
        body, name="ln_in", grid=(t // tm,), in_specs=in_specs,
        out_specs=[row(D_MODEL)] + [row(wd) for _, _, wd, _ in Z_SEGS],
        out_shape=[jax.ShapeDtypeStruct((t, D_MODEL), BF16)]
        + [jax.ShapeDtypeStruct((t, wd), dt) for _, _, wd, dt in Z_SEGS],
        compiler_params=_cparams(("arbitrary",)),
    )(*args)


def _ln_in_bwd(dz, w_t, x, ss, g, dxo, seq, tm=ROW_TILE, dep=None):
    t = x.shape[0]
    tm = min(tm, seq)
    tps = seq // tm
    nb = t // seq
    nz = len(Z_SEGS)

    def body(*refs):
        dz_refs = refs[:nz]
        wt_ref, x_ref, ss_ref, g_ref, dxo_ref, dx_ref, dss_ref, dg_ref = refs[nz:]
        i = pl.program_id(0)
        dzc = jnp.concatenate([r[...].astype(BF16) for r in dz_refs], axis=1)
        dh = _nt(dzc, wt_ref[...])
        xv = x_ref[...]
        rstd = lax.rsqrt(jnp.mean(xv * xv, axis=-1, keepdims=True) + EPS)
        xn = xv * rstd
        gv = g_ref[...]
        s1 = ss_ref[0, 1:2, :]
        dxg = dh * s1
        dxn = dxg * gv
        dx = rstd * (dxn - xn * jnp.mean(dxn * xn, axis=-1, keepdims=True))
        dx_ref[...] = dxo_ref[...] + dx
        dshift = jnp.sum(dh, axis=0, keepdims=True)
        dscale = jnp.sum(dh * (xn * gv), axis=0, keepdims=True)
        dgp = jnp.sum(dxg * xn, axis=0, keepdims=True)

        @pl.when(i % tps == 0)
        def _():
            dss_ref[0, 0:1, :] = dshift
            dss_ref[0, 1:2, :] = dscale

        @pl.when(i % tps != 0)
        def _():
            dss_ref[0, 0:1, :] += dshift
            dss_ref[0, 1:2, :] += dscale

        @pl.when(i == 0)
        def _():
            dg_ref[...] = dgp

        @pl.when(i != 0)
        def _():
            dg_ref[...] += dgp

    row = lambda wd: pl.BlockSpec((tm, wd), lambda i: (i, 0))
    in_specs = ([row(wd) for _, _, wd, _ in Z_SEGS]
                + [pl.BlockSpec((D_MODEL, N_PAD), lambda i: (0, 0)), row(D_MODEL),
                   pl.BlockSpec((1, 2, D_MODEL), lambda i: (i // tps, 0, 0)),
                   pl.BlockSpec((1, D_MODEL), lambda i: (0, 0)), row(D_MODEL)])
    body, in_specs, args = _after(dep, body, in_specs, [*dz, w_t, x, ss, g, dxo])
    return pl.pallas_call(
        body, name="ln_in_bwd", grid=(t // tm,), in_specs=in_specs,
        out_specs=[row(D_MODEL), pl.BlockSpec((1, 2, D_MODEL), lambda i: (i // tps, 0, 0)),
                   pl.BlockSpec((1, D_MODEL), lambda i: (0, 0))],
        out_shape=[jax.ShapeDtypeStruct((t, D_MODEL), F32), jax.ShapeDtypeStruct((nb, 2, D_MODEL), F32),
                   jax.ShapeDtypeStruct((1, D_MODEL), F32)],
        compiler_params=_cparams(("arbitrary",)),
    )(*args)


def _matmul_tn(a, bs, name, tm=1024, dep=None):
    bs = list(bs) if isinstance(bs, (list, tuple)) else [bs]
    t, k = a.shape
    widths = [b.shape[1] for b in bs]
    n = sum(widths)
    tm = min(tm, t)

    def body(a_ref, *refs):
        b_refs, o_ref = refs[:-1], refs[-1]
        i = pl.program_id(0)
        av = a_ref[...].astype(BF16)
        parts = [b_ref[...].astype(BF16) for b_ref in b_refs]
        bv = parts[0] if len(parts) == 1 else jnp.concatenate(parts, axis=1)
        part = lax.dot_general(av, bv, (((0,), (0,)), ((), ())), preferred_element_type=F32)

        @pl.when(i == 0)
        def _():
            o_ref[...] = part

        @pl.when(i != 0)
        def _():
            o_ref[...] += part

    in_specs = [pl.BlockSpec((tm, k), lambda i: (i, 0))] + [pl.BlockSpec((tm, wd), lambda i: (i, 0)) for wd in widths]
    body, in_specs, args = _after(dep, body, in_specs, [a, *bs])
    return pl.pallas_call(
        body, name=name, grid=(t // tm,), in_specs=in_specs,
        out_specs=pl.BlockSpec((k, n), lambda i: (0, 0)),
        out_shape=jax.ShapeDtypeStruct((k, n), F32),
        compiler_params=_cparams(("arbitrary",)),
    )(*args)


def _rope(blk, cos_t, sin_a, sin_b):
    return blk * cos_t + pltpu.roll(blk, 112, 1) * sin_a + pltpu.roll(blk, 16, 1) * sin_b


def _unrope(d, cos_t, sin_a, sin_b):
    return d * cos_t + pltpu.roll(d * sin_a, 16, 1) + pltpu.roll(d * sin_b, 112, 1)


def _mla_prep(cq, ckv, kpe, gq, gkv, wuq, wk, wv, cos_t, sin_a, sin_b, tm=ROW_TILE, dep=None):
    t = cq.shape[0]
    tm = min(tm, t)
    qw = A_HEADS * HEAD_PAD

    def body(cq_ref, ckv_ref, kpe_ref, gq_ref, gkv_ref, wuq_ref, wk_ref, wv_ref, c_ref, sa_ref, sb_ref,
             q_ref, k_ref, v_ref, cqn_ref, ckvn_ref):
        ct, sa, sb = c_ref[...], sa_ref[...], sb_ref[...]
        a = cq_ref[...]
        cqn = (a * lax.rsqrt(jnp.mean(a * a, axis=-1, keepdims=True) + EPS) * gq_ref[...]).astype(BF16)
        cqn_ref[...] = cqn
        b = ckv_ref[...]
        ckvn = (b * lax.rsqrt(jnp.mean(b * b, axis=-1, keepdims=True) + EPS) * gkv_ref[...]).astype(BF16)
        ckvn_ref[...] = ckvn
        qlin = jnp.dot(cqn, wuq_ref[...], preferred_element_type=F32)
        klin = jnp.dot(ckvn, wk_ref[...], preferred_element_type=F32)
        v_ref[...] = jnp.dot(ckvn, wv_ref[...], preferred_element_type=F32).astype(BF16)
        kr = _rope(kpe_ref[...], ct, sa, sb)
        for h in range(A_HEADS):
            sl = slice(h * HEAD_PAD, (h + 1) * HEAD_PAD)
            q_ref[:, sl] = _rope(qlin[:, sl], ct, sa, sb).astype(BF16)
            k_ref[:, sl] = (klin[:, sl] + kr).astype(BF16)

    row = lambda wd: pl.BlockSpec((tm, wd), lambda i: (i, 0))
    full = lambda r, c: pl.BlockSpec((r, c), lambda i: (0, 0))
    in_specs = [row(A_Q_RANK), row(A_KV_RANK), row(128), full(1, A_Q_RANK), full(1, A_KV_RANK),
                full(A_Q_RANK, qw), full(A_KV_RANK, qw), full(A_KV_RANK, GW), row(128), row(128), row(128)]
    body, in_specs, args = _after(dep, body, in_specs, [cq, ckv, kpe, gq, gkv, wuq, wk, wv, cos_t, sin_a, sin_b])
    return pl.pallas_call(
        body, name="mla_prep", grid=(t // tm,), in_specs=in_specs,
        out_specs=[row(qw), row(qw), row(GW), row(A_Q_RANK), row(A_KV_RANK)],
        out_shape=[jax.ShapeDtypeStruct((t, qw), BF16), jax.ShapeDtypeStruct((t, qw), BF16),
                   jax.ShapeDtypeStruct((t, GW), BF16), jax.ShapeDtypeStruct((t, A_Q_RANK), BF16),
                   jax.ShapeDtypeStruct((t, A_KV_RANK), BF16)],
        compiler_params=_cparams(("arbitrary",)),
    )(*args)


def _mla_prep_bwd(dq, dk, dv, cq, ckv, gq, gkv, wuq_t, wk_t, wv_t, cos_t, sin_a, sin_b, tm=ROW_TILE):
    t = cq.shape[0]
    tm = min(tm, t)
    qw = A_HEADS * HEAD_PAD

    def body(dq_ref, dk_ref, dv_ref, cq_ref, ckv_ref, gq_ref, gkv_ref, wuqt_ref, wkt_ref, wvt_ref,
             c_ref, sa_ref, sb_ref, dcq_ref, dckv_ref, dkpe_ref, dql_ref, dkl_ref, dgq_ref, dgkv_ref):
        i = pl.program_id(0)
        ct, sa, sb = c_ref[...], sa_ref[...], sb_ref[...]
        lane = lax.broadcasted_iota(jnp.int32, (1, HEAD_PAD), 1)
        nope = lane < A_NOPE
        rope = (lane >= A_NOPE) & (lane < A_NOPE + A_ROPE)
        dksum = None
        for h in range(A_HEADS):
            sl = slice(h * HEAD_PAD, (h + 1) * HEAD_PAD)
            dql_ref[:, sl] = _unrope(dq_ref[:, sl], ct, sa, sb).astype(BF16)
            dkh = dk_ref[:, sl]
            dkl_ref[:, sl] = jnp.where(nope, dkh, 0.0).astype(BF16)
            dksum = dkh if dksum is None else dksum + dkh
        dkpe_ref[...] = jnp.where(rope, _unrope(jnp.where(rope, dksum, 0.0), ct, sa, sb), 0.0).astype(BF16)
        dcqn = jnp.dot(dql_ref[...], wuqt_ref[...], preferred_element_type=F32)
        dckvn = (jnp.dot(dkl_ref[...], wkt_ref[...], preferred_element_type=F32)
                 + jnp.dot(dv_ref[...].astype(BF16), wvt_ref[...], preferred_element_type=F32))

        def norm_bwd(xv, gv, dy):
            rstd = lax.rsqrt(jnp.mean(xv * xv, axis=-1, keepdims=True) + EPS)
            xn = xv * rstd
            dxn = dy * gv
            dx = rstd * (dxn - xn * jnp.mean(dxn * xn, axis=-1, keepdims=True))
            return dx, jnp.sum(dy * xn, axis=0, keepdims=True)

        dcq, dgq = norm_bwd(cq_ref[...], gq_ref[...], dcqn)
        dckv, dgkv = norm_bwd(ckv_ref[...], gkv_ref[...], dckvn)
        dcq_ref[...] = dcq.astype(BF16)
        dckv_ref[...] = dckv.astype(BF16)

        @pl.when(i == 0)
        def _():
            dgq_ref[...] = dgq
            dgkv_ref[...] = dgkv

        @pl.when(i != 0)
        def _():
            dgq_ref[...] += dgq
            dgkv_ref[...] += dgkv

    row = lambda wd: pl.BlockSpec((tm, wd), lambda i: (i, 0))
    full = lambda r, c: pl.BlockSpec((r, c), lambda i: (0, 0))
    return pl.pallas_call(
        body, name="mla_prep_bwd", grid=(t // tm,),
        in_specs=[row(qw), row(qw), row(GW), row(A_Q_RANK), row(A_KV_RANK), full(1, A_Q_RANK), full(1, A_KV_RANK),
                  full(qw, A_Q_RANK), full(qw, A_KV_RANK), full(GW, A_KV_RANK), row(128), row(128), row(128)],
        out_specs=[row(A_Q_RANK), row(A_KV_RANK), row(128), row(qw), row(qw), full(1, A_Q_RANK), full(1, A_KV_RANK)],
        out_shape=[jax.ShapeDtypeStruct((t, A_Q_RANK), BF16), jax.ShapeDtypeStruct((t, A_KV_RANK), BF16),
                   jax.ShapeDtypeStruct((t, 128), BF16), jax.ShapeDtypeStruct((t, qw), BF16),
                   jax.ShapeDtypeStruct((t, qw), BF16), jax.ShapeDtypeStruct((1, A_Q_RANK), F32),
                   jax.ShapeDtypeStruct((1, A_KV_RANK), F32)],
        compiler_params=_cparams(("arbitrary",)),
    )(dq, dk, dv, cq, ckv, gq, gkv, wuq_t, wk_t, wv_t, cos_t, sin_a, sin_b)


def _nt(a, b):
    return lax.dot_general(a, b, (((1,), (1,)), ((), ())), preferred_element_type=F32)


def _tn(a, b):
    return lax.dot_general(a, b, (((0,), (0,)), ((), ())), preferred_element_type=F32)


def _causal_mask(kind, q0, k0, tq, tk):
    qpos = q0 + lax.broadcasted_iota(jnp.int32, (tq, tk), 0)
    kpos = k0 + lax.broadcasted_iota(jnp.int32, (tq, tk), 1)
    if kind == "mla":
        return lax.shift_right_logical(kpos, 6) <= lax.shift_right_logical(qpos, 6)
    return kpos <= qpos


def _attn_fwd(kind, q, k, v, f, seq, scale, tq=512, tk=512):
    t = v.shape[0]
    nb = t // seq
    nq = seq // tq
    hw = 256 if kind == "mla" else 128
    n_heads = A_HEADS if kind == "mla" else C_HEADS
    use_f = f is not None
    tq, tk = min(tq, seq), min(tk, seq)
    nq = seq // tq
    assert tk == tq

    def body(*refs):
        if use_f:
            q_ref, k_ref, v_ref, f_ref, o_ref, st_ref = refs
        else:
            q_ref, k_ref, v_ref, o_ref, st_ref = refs
        qi = pl.program_id(2)
        q0 = qi * tq
        lane = lax.broadcasted_iota(jnp.int32, (1, 128), 1)
        half = lane >= 64
        qall = q_ref[...]
        if kind == "mla":
            qhs = [qall[:, 0:128], qall[:, 128:256]]
            post = scale * math.log2(math.e)
        else:
            assert math.frexp(scale)[0] == 0.5
            qall = qall * jnp.asarray(scale, BF16)
            qhs = [jnp.where(half, jnp.zeros_like(qall), qall), jnp.where(half, qall, jnp.zeros_like(qall))]
            post = None
        kd = pl.multiple_of(q0, tq)
        diag = _causal_mask(kind, 0, 0, tq, tk)

        def block(j, k0, state, masked):
            m, l, acc = state
            kh = k_ref[pl.ds(k0, tk), j * 128:(j + 1) * 128] if kind == "mla" else k_ref[pl.ds(k0, tk), :]
            s = _nt(qhs[j], kh)
            if post is not None:
                s = s * post
            if use_f:
                s = s - f_ref[0, 0, j:j + 1, pl.ds(k0, tk)]
            if masked:
                s = jnp.where(diag, s, NEG)
            mn = jnp.maximum(m, jnp.max(s, axis=-1, keepdims=True))
            alpha = jnp.exp2(m - mn) if post is not None else jnp.exp(m - mn)
            p = jnp.exp2(s - mn) if post is not None else jnp.exp(s - mn)
            l = alpha * l + jnp.sum(p, axis=-1, keepdims=True)
            acc = alpha * acc + jnp.dot(p.astype(BF16), v_ref[pl.ds(k0, tk), :], preferred_element_type=F32)
            return mn, l, acc

        def run(heads):
            def kstep(kb, carry):
                k0 = pl.multiple_of(kb * tk, tk)
                out = ()
                for n, j in enumerate(heads):
                    out += block(j, k0, carry[3 * n:3 * n + 3], False)
                return out

            init = (jnp.full((tq, 1), NEG, F32), jnp.zeros((tq, 1), F32), jnp.zeros((tq, 128), F32)) * len(heads)
            carry = lax.fori_loop(0, qi, kstep, init)
            o, st = jnp.zeros((tq, 128), F32), jnp.zeros((tq, 128), F32)
            for n, j in enumerate(heads):
                m, l, acc = block(j, kd, carry[3 * n:3 * n + 3], True)
                o = jnp.where(half == bool(j), acc / l, o)
                if post is not None:
                    m = m * math.log(2.0)
                st = jnp.where(lane == j, m + jnp.log(l), st)
            o_ref[...] = o
            st_ref[...] = st

        if n_heads % 2 == 0:
            run((0, 1))
        else:
            last = pl.program_id(1) == n_heads // 2
            pl.when(jnp.logical_not(last))(lambda: run((0, 1)))
            pl.when(last)(lambda: run((0,)))

    in_specs = [pl.BlockSpec((tq, hw), lambda b, p, i: (b * nq + i, p)),
                pl.BlockSpec((seq, hw), lambda b, p, i: (b, p)),
                pl.BlockSpec((seq, 128), lambda b, p, i: (b, p))]
    args = [q, k, v]
    if use_f:
        in_specs.append(pl.BlockSpec((1, 1, 8, seq), lambda b, p, i: (b, p, 0, 0)))
        args.append(f)
    oblk = pl.BlockSpec((tq, 128), lambda b, p, i: (b * nq + i, p))
    return pl.pallas_call(
        body, name="attn_fwd_" + kind, grid=(nb, 3, nq), in_specs=in_specs, out_specs=[oblk, oblk],
        out_shape=[jax.ShapeDtypeStruct((t, GW), F32), jax.ShapeDtypeStruct((t, GW), F32)],
        compiler_params=_cparams(("arbitrary", "arbitrary", "arbitrary")),
    )(*args)


def _attn_bwd(kind, q, k, v, f, o, st, do, seq, scale, tq=512, tk=512, dep=None):
    t = v.shape[0]
    nb = t // seq
    tq, tk = min(tq, seq), min(tk, seq)
    nq = seq // tq
    nk = seq // tk
    hw = 256 if kind == "mla" else 128
    n_heads = A_HEADS if kind == "mla" else C_HEADS
    use_f = f is not None
    assert tq == tk

    def body(*refs):
        if use_f:
            q_ref, k_ref, v_ref, f_ref, o_ref, st_ref, do_ref, dq_ref, dk_ref, dv_ref, df_ref, dfq_ref = refs
        else:
            q_ref, k_ref, v_ref, o_ref, st_ref, do_ref, dq_ref, dk_ref, dv_ref = refs
        kj = pl.program_id(2)
        lane = lax.broadcasted_iota(jnp.int32, (1, 128), 1)
        half = lane >= 64

        @pl.when(kj == 0)
        def _():
            dq_ref[...] = jnp.zeros_like(dq_ref)
            if use_f:
                dfq_ref[...] = jnp.zeros_like(dfq_ref)

        dk_ref[...] = jnp.zeros_like(dk_ref)
        dv_ref[...] = jnp.zeros_like(dv_ref)
        if use_f:
            df_ref[...] = jnp.zeros_like(df_ref)
        vv = v_ref[...]
        diag = _causal_mask(kind, 0, 0, tq, tk)

        def qstep(qi, masked):
            q0 = pl.multiple_of(qi * tq, tq)
            rows = pl.ds(q0, tq)
            dov = do_ref[rows, :]
            dd = dov * o_ref[rows, :]
            stv = st_ref[rows, :]

            def one_head(j):
                hm = half == bool(j)
                delta = jnp.sum(jnp.where(hm, dd, 0.0), axis=-1, keepdims=True)
                lse = stv[:, j:j + 1]
                if kind == "mla":
                    cols = slice(j * 128, (j + 1) * 128)
                    qh = q_ref[rows, cols]
                    kh = k_ref[:, cols]
                else:
                    cols = slice(0, 128)
                    qa = q_ref[rows, :]
                    qh = jnp.where(hm, qa, jnp.zeros_like(qa))
                    kh = k_ref[...]
                s = _nt(qh, kh) * scale
                if use_f:
                    s = s - f_ref[0, 0, j:j + 1, :]
                if masked:
                    s = jnp.where(diag, s, NEG)
                p = jnp.exp(s - lse)
                doh = jnp.where(hm, dov, 0.0).astype(BF16)
                ds = p * (_nt(doh, vv) - delta)
                dsb = (ds * scale).astype(BF16)
                dv_ref[...] += _tn(p.astype(BF16), doh)
                dk_ref[:, cols] += _tn(dsb, qh)
                dqc = jnp.dot(dsb, kh, preferred_element_type=F32)
                if kind != "mla":
                    dqc = jnp.where(hm, dqc, 0.0)
                dq_ref[rows, cols] += dqc
                if use_f:
                    df_ref[0, 0, j:j + 1, :] += -jnp.sum(ds, axis=0, keepdims=True)
                    dfq_ref[rows, :] += jnp.where(lane == j, jnp.sum(ds, axis=-1, keepdims=True), 0.0)

            def both():
                one_head(0)
                one_head(1)

            if n_heads % 2 == 0:
                both()
            else:
                last = pl.program_id(1) == n_heads // 2
                pl.when(jnp.logical_not(last))(both)
                pl.when(last)(lambda: one_head(0))

        qstep(kj, True)

        def rest(qi, carry):
            qstep(qi, False)
            return carry

        lax.fori_loop(kj + 1, nq, rest, 0)

    full_q = lambda wd: pl.BlockSpec((seq, wd), lambda b, p, i: (b, p))
    kblk = lambda wd: pl.BlockSpec((tk, wd), lambda b, p, i: (b * nk + i, p))
    in_specs = [full_q(hw), kblk(hw), kblk(128)]
    args = [q, k, v]
    if use_f:
        in_specs.append(pl.BlockSpec((1, 1, 8, tk), lambda b, p, i: (b, p, 0, i)))
        args.append(f)
    in_specs += [full_q(128), full_q(128), full_q(128)]
    args += [o, st, do]
    out_specs = [full_q(hw), kblk(hw), kblk(128)]
    out_shape = [jax.ShapeDtypeStruct((t, 3 * hw), F32), jax.ShapeDtypeStruct((t, 3 * hw), F32),
                 jax.ShapeDtypeStruct((t, GW), F32)]
    if use_f:
        out_specs += [pl.BlockSpec((1, 1, 8, tk), lambda b, p, i: (b, p, 0, i)), full_q(128)]
        out_shape += [jax.ShapeDtypeStruct((nb, 3, 8, seq), F32), jax.ShapeDtypeStruct((t, GW), F32)]
    body, in_specs, args = _after(dep, body, in_specs, args)
    return pl.pallas_call(
        body, name="attn_bwd_" + kind, grid=(nb, 3, nk), in_specs=in_specs, out_specs=out_specs,
        out_shape=out_shape, compiler_params=_cparams(("arbitrary", "arbitrary", "arbitrary")),
    )(*args)


BQ = 256
BWIN = BQ + B_LEFT


def _band_geometry():
    r = lax.broadcasted_iota(jnp.int32, (BQ, BWIN), 0)
    j = lax.broadcasted_iota(jnp.int32, (BQ, BWIN), 1)
    rc = lax.shift_right_logical(r, 6)
    jc = lax.shift_right_logical(j, 6)
    allowed = (jc - 8 <= rc) & (rc <= jc)
    return (r + B_LEFT - j) >= REL_CLIP, allowed, j < r


def _band_onehot(transposed, offset=0):
    shape = (BWIN, GW) if transposed else (GW, BWIN)
    kk = lax.broadcasted_iota(jnp.int32, shape, 1 if transposed else 0)
    x = lax.broadcasted_iota(jnp.int32, shape, 0 if transposed else 1) - offset
    x = jnp.where(x < 0, x + BWIN, x)
    return (kk == jnp.clip(B_LEFT - x, -REL_CLIP, REL_CLIP) + REL_CLIP).astype(F32)


def _band_table(rel_bias8):
    def body(b_ref, o_ref):
        hh = pl.program_id(0)
        u8 = jnp.dot(b_ref[...], _band_onehot(False), precision=HI, preferred_element_type=F32)
        rid = lax.broadcasted_iota(jnp.int32, (8, BWIN), 0)
        row = jnp.sum(jnp.where(rid == hh, u8, 0.0), axis=0, keepdims=True)
        far, allowed, _ = _band_geometry()
        tbl = pltpu.roll(jnp.broadcast_to(row, (BQ, BWIN)), 0, 1, stride=1, stride_axis=0)
        tbl = jnp.where(far, row[:, 0:1], tbl)
        o_ref[0] = jnp.where(allowed, tbl, NEG)

    return pl.pallas_call(
        body, name="band_table", grid=(6,),
        in_specs=[pl.BlockSpec((8, GW), lambda h: (0, 0))],
        out_specs=pl.BlockSpec((1, BQ, BWIN), lambda h: (h, 0, 0)),
        out_shape=jax.ShapeDtypeStruct((6, BQ, BWIN), F32),
        compiler_params=_cparams(("arbitrary",)),
    )(rel_bias8)


def _band_table_bwd(gtab):
    def body(g_ref, o_ref):
        gv = g_ref[0]
        _, _, wrapped = _band_geometry()
        gfar = jnp.sum(jnp.sum(jnp.where(wrapped, gv, 0.0), axis=-1, keepdims=True), axis=0, keepdims=True)
        anti = (lax.broadcasted_iota(jnp.int32, (BQ, BQ), 0) + lax.broadcasted_iota(jnp.int32, (BQ, BQ), 1)
                == BQ - 1).astype(F32)
        grev = jnp.dot(anti, jnp.where(wrapped, 0.0, gv), precision=HI, preferred_element_type=F32)
        near = pltpu.roll(grev, 0, 1, stride=1, stride_axis=0)
        y = jnp.broadcast_to(jnp.sum(near, axis=0, keepdims=True), (8, BWIN))
        gb = jnp.dot(y, _band_onehot(True, BQ - 1), precision=HI, preferred_element_type=F32)
        lane = lax.broadcasted_iota(jnp.int32, (8, GW), 1)
        o_ref[0] = gb + jnp.where(lane == 2 * REL_CLIP, gfar, 0.0)

    return pl.pallas_call(
        body, name="band_table_bwd", grid=(B_HEADS,),
        in_specs=[pl.BlockSpec((1, BQ, BWIN), lambda h: (h, 0, 0))],
        out_specs=pl.BlockSpec((1, 8, GW), lambda h: (h, 0, 0)),
        out_shape=jax.ShapeDtypeStruct((B_HEADS, 8, GW), F32),
        compiler_params=_cparams(("arbitrary",)),
    )(gtab)


def _band_fwd(q, k, v, table, seq, scale):
    t = q.shape[0]
    nb = t // seq
    nq = seq // BQ

    def body(q_ref, k_ref, v_ref, tb_ref, o_ref, st_ref, kpad, vpad):
        qi = pl.program_id(2)
        q0 = pl.multiple_of(qi * BQ, BQ)
        lane = lax.broadcasted_iota(jnp.int32, (1, 128), 1)
        half = lane >= 64

        @pl.when(qi == 0)
        def _():
            kpad[0:B_LEFT, :] = jnp.zeros((B_LEFT, 128), BF16)
            vpad[0:B_LEFT, :] = jnp.zeros((B_LEFT, 128), BF16)
            kpad[B_LEFT:, :] = k_ref[...]
            vpad[B_LEFT:, :] = v_ref[...]

        kw = kpad[pl.ds(q0, BWIN), :]
        vw = vpad[pl.ds(q0, BWIN), :]
        inside = lax.broadcasted_iota(jnp.int32, (BQ, BWIN), 1) >= B_LEFT - q0
        assert math.frexp(scale)[0] == 0.5
        qall = q_ref[...] * jnp.asarray(scale, BF16)

        def run(heads):
            o, st = jnp.zeros((BQ, 128), F32), jnp.zeros((BQ, 128), F32)
            for j in heads:
                qh = jnp.where(half == bool(j), qall, jnp.zeros_like(qall))
                s = jnp.where(inside, _nt(qh, kw) + tb_ref[j], NEG)
                m = jnp.max(s, axis=-1, keepdims=True)
                p = jnp.exp(s - m)
                l = jnp.sum(p, axis=-1, keepdims=True)
                o = jnp.where(half == bool(j), jnp.dot(p.astype(BF16), vw, preferred_element_type=F32) / l, o)
                st = jnp.where(lane == j, m + jnp.log(l), st)
            o_ref[...] = o
            st_ref[...] = st

        last = pl.program_id(1) == B_HEADS // 2
        pl.when(jnp.logical_not(last))(lambda: run((0, 1)))
        pl.when(last)(lambda: run((0,)))

    qblk = pl.BlockSpec((BQ, 128), lambda b, p, i: (b * nq + i, p))
    full = pl.BlockSpec((seq, 128), lambda b, p, i: (b, p))
    return pl.pallas_call(
        body, name="band_fwd", grid=(nb, 3, nq),
        in_specs=[qblk, full, full, pl.BlockSpec((2, BQ, BWIN), lambda b, p, i: (p, 0, 0))],
        out_specs=[qblk, qblk],
        out_shape=[jax.ShapeDtypeStruct((t, GW), F32), jax.ShapeDtypeStruct((t, GW), F32)],
        scratch_shapes=[pltpu.VMEM((seq + B_LEFT, 128), BF16), pltpu.VMEM((seq + B_LEFT, 128), BF16)],
        compiler_params=_cparams(("arbitrary", "arbitrary", "arbitrary")),
    )(q, k, v, table)


def _band_bwd(q, k, v, table, o, st, do, seq, scale, dep=None):
    t = q.shape[0]
    nb = t // seq
    nq = seq // BQ

    def body(q_ref, k_ref, v_ref, tb_ref, o_ref, st_ref, do_ref, dq_ref, dk_ref, dv_ref, g_ref,
             kpad, vpad, dkpad, dvpad):
        b = pl.program_id(1)
        qi = pl.program_id(2)
        q0 = pl.multiple_of(qi * BQ, BQ)
        lane = lax.broadcasted_iota(jnp.int32, (1, 128), 1)
        half = lane >= 64

        @pl.when(qi == 0)
        def _():
            kpad[0:B_LEFT, :] = jnp.zeros((B_LEFT, 128), BF16)
            vpad[0:B_LEFT, :] = jnp.zeros((B_LEFT, 128), BF16)
            kpad[B_LEFT:, :] = k_ref[...]
            vpad[B_LEFT:, :] = v_ref[...]
            dkpad[...] = jnp.zeros_like(dkpad)
            dvpad[...] = jnp.zeros_like(dvpad)

        @pl.when((qi == 0) & (b == 0))
        def _():
            g_ref[...] = jnp.zeros_like(g_ref)

        win = pl.ds(q0, BWIN)
        kw = kpad[win, :]
        vw = vpad[win, :]
        inside = lax.broadcasted_iota(jnp.int32, (BQ, BWIN), 1) >= B_LEFT - q0
        qall = q_ref[...]
        dov = do_ref[...]
        dd = dov * o_ref[...]
        stv = st_ref[...]

        def run(heads):
            dq = jnp.zeros((BQ, 128), F32)
            for j in heads:
                hm = half == bool(j)
                qh = jnp.where(hm, qall, jnp.zeros_like(qall))
                delta = jnp.sum(jnp.where(hm, dd, 0.0), axis=-1, keepdims=True)
                s = jnp.where(inside, _nt(qh, kw) * scale + tb_ref[j], NEG)
                p = jnp.exp(s - stv[:, j:j + 1])
                doh = jnp.where(hm, dov, 0.0).astype(BF16)
                ds = p * (_nt(doh, vw) - delta)
                g_ref[j] += ds
                dsb = (ds * scale).astype(BF16)
                dvpad[win, :] += _tn(p.astype(BF16), doh)
                dkpad[win, :] += _tn(dsb, qh)
                dq = dq + jnp.where(hm, jnp.dot(dsb, kw, preferred_element_type=F32), 0.0)
            dq_ref[...] = dq.astype(BF16)

        last = pl.program_id(0) == B_HEADS // 2
        pl.when(jnp.logical_not(last))(lambda: run((0, 1)))
        pl.when(last)(lambda: run((0,)))

        @pl.when(qi == nq - 1)
        def _():
            dk_ref[...] = dkpad[B_LEFT:, :].astype(BF16)
            dv_ref[...] = dvpad[B_LEFT:, :].astype(BF16)

    qblk = pl.BlockSpec((BQ, 128), lambda p, b, i: (b * nq + i, p))
    full = pl.BlockSpec((seq, 128), lambda p, b, i: (b, p))
    tblk = pl.BlockSpec((2, BQ, BWIN), lambda p, b, i: (p, 0, 0))
    body, in_specs, args = _after(dep, body, [qblk, full, full, tblk, qblk, qblk, qblk], [q, k, v, table, o, st, do])
    return pl.pallas_call(
        body, name="band_bwd", grid=(3, nb, nq),
        in_specs=in_specs,
        out_specs=[qblk, full, full, tblk],
        out_shape=[jax.ShapeDtypeStruct((t, GW), BF16), jax.ShapeDtypeStruct((t, GW), BF16),
                   jax.ShapeDtypeStruct((t, GW), BF16), jax.ShapeDtypeStruct((6, BQ, BWIN), F32)],
        scratch_shapes=[pltpu.VMEM((seq + B_LEFT, 128), BF16), pltpu.VMEM((seq + B_LEFT, 128), BF16),
                        pltpu.VMEM((seq + B_LEFT, 128), F32), pltpu.VMEM((seq + B_LEFT, 128), F32)],
        compiler_params=_cparams(("arbitrary", "arbitrary", "arbitrary")),
    )(*args)


def _fox_prep(cf, fb, seq):
    nb = cf.shape[0] // seq
    nblk = seq // 128

    def body(cf_ref, fb_ref, f_ref):
        x = cf_ref[...] + fb_ref[...]
        lf = jnp.minimum(x, 0.0) - jnp.log1p(jnp.exp(-jnp.abs(x)))
        rows = lf.T[0:8, :]
        upper = (lax.broadcasted_iota(jnp.int32, (128, 128), 0)
                 <= lax.broadcasted_iota(jnp.int32, (128, 128), 1)).astype(F32)
        carry = jnp.zeros((8, 1), F32)
        for blk in range(nblk):
            sl = slice(blk * 128, (blk + 1) * 128)
            cs = jnp.dot(rows[:, sl], upper, precision=HI, preferred_element_type=F32) + carry
            carry = cs[:, 127:128]
            f_ref[0, 0, :, sl] = cs
            f_ref[0, 1, :, sl] = pltpu.roll(cs, 6, 0)
            f_ref[0, 2, :, sl] = pltpu.roll(cs, 4, 0)

    return pl.pallas_call(
        body, name="fox_prep", grid=(nb,),
        in_specs=[pl.BlockSpec((seq, 128), lambda b: (b, 0)), pl.BlockSpec((1, 128), lambda b: (0, 0))],
        out_specs=pl.BlockSpec((1, 3, 8, seq), lambda b: (b, 0, 0, 0)),
        out_shape=jax.ShapeDtypeStruct((nb, 3, 8, seq), F32),
        compiler_params=_cparams(("arbitrary",)),
    )(cf, fb)


def _fox_prep_bwd(df, dfq, cf, fb, seq):
    nb = cf.shape[0] // seq
    nblk = seq // 128

    def body(df_ref, dfq_ref, cf_ref, fb_ref, dcf_ref, dfb_ref, wide):
        b = pl.program_id(0)
        row = lax.broadcasted_iota(jnp.int32, (8, seq), 0)
        dfh = None
        for p in range(3):
            both = df_ref[0, p] + dfq_ref[:, p * 128:(p + 1) * 128].T[0:8, :]
            both = jnp.where(row < 2, both, 0.0)
            if p:
                both = pltpu.roll(both, 2 * p, 0)
            dfh = both if dfh is None else dfh + both
        lower = (lax.broadcasted_iota(jnp.int32, (128, 128), 0)
                 >= lax.broadcasted_iota(jnp.int32, (128, 128), 1)).astype(F32)
        wide[...] = jnp.zeros_like(wide)
        carry = jnp.zeros((8, 1), F32)
        for blk in reversed(range(nblk)):
            sl = slice(blk * 128, (blk + 1) * 128)
            rc = jnp.dot(dfh[:, sl], lower, precision=HI, preferred_element_type=F32) + carry
            carry = rc[:, 0:1]
            wide[0:8, sl] = rc
        dl = wide[...].T
        x = cf_ref[...] + fb_ref[...]
        dcf = dl * (1.0 / (1.0 + jnp.exp(x)))
        dcf_ref[...] = dcf.astype(BF16)
        part = jnp.sum(dcf, axis=0, keepdims=True)

        @pl.when(b == 0)
        def _():
            dfb_ref[...] = part

        @pl.when(b != 0)
        def _():
            dfb_ref[...] += part

    return pl.pallas_call(
        body, name="fox_prep_bwd", grid=(nb,),
        in_specs=[pl.BlockSpec((1, 3, 8, seq), lambda b: (b, 0, 0, 0)), pl.BlockSpec((seq, GW), lambda b: (b, 0)),
                  pl.BlockSpec((seq, 128), lambda b: (b, 0)), pl.BlockSpec((1, 128), lambda b: (0, 0))],
        out_specs=[pl.BlockSpec((seq, 128), lambda b: (b, 0)), pl.BlockSpec((1, 128), lambda b: (0, 0))],
        out_shape=[jax.ShapeDtypeStruct(cf.shape, BF16), jax.ShapeDtypeStruct((1, 128), F32)],
        scratch_shapes=[pltpu.VMEM((128, seq), F32)],
        compiler_params=_cparams(("arbitrary",)),
    )(df, dfq, cf, fb)


def _gate_out(oa, ob, oc, gates, w, x, gate, seq, tm=ROW_TILE):
    t = x.shape[0]
    tm = min(tm, seq)
    tps = seq // tm

    def body(oa_ref, ob_ref, oc_ref, g_ref, w_ref, x_ref, gt_ref, xo_ref, y_ref, u_ref):
        for n, o_ref in enumerate((oa_ref, ob_ref, oc_ref)):
            sl = slice(n * GW, (n + 1) * GW)
            gv = g_ref[:, sl].astype(F32)
            u_ref[:, sl] = (o_ref[...] * (gv * _sigmoid(gv))).astype(BF16)
        y = jnp.dot(u_ref[...], w_ref[...], preferred_element_type=F32)
        y_ref[...] = y.astype(BF16)
        xo_ref[...] = x_ref[...] + gt_ref[0] * y

    row = lambda wd: pl.BlockSpec((tm, wd), lambda i: (i, 0))
    return pl.pallas_call(
        body, name="gate_out", grid=(t // tm,),
        in_specs=[row(GW), row(GW), row(GW), row(U_PAD), pl.BlockSpec((U_PAD, D_MODEL), lambda i: (0, 0)),
                  row(D_MODEL), pl.BlockSpec((1, 1, D_MODEL), lambda i: (i // tps, 0, 0))],
        out_specs=[row(D_MODEL), row(D_MODEL), row(U_PAD)],
        out_shape=[jax.ShapeDtypeStruct((t, D_MODEL), F32), jax.ShapeDtypeStruct((t, D_MODEL), BF16),
                   jax.ShapeDtypeStruct((t, U_PAD), BF16)],
        compiler_params=_cparams(("arbitrary",)),
    )(oa, ob, oc, gates, w, x, gate)


def _gate_out_bwd(dxo, y, gate, oa, ob, oc, gates, w_t, seq, tm=ROW_TILE, dep=None):
    t = dxo.shape[0]
    tm = min(tm, seq)
    tps = seq // tm
    nb = t // seq

    def body(dxo_ref, y_ref, gt_ref, oa_ref, ob_ref, oc_ref, g_ref, wt_ref,
             dy_ref, doa_ref, dob_ref, doc_ref, dg_ref, dgt_ref):
        i = pl.program_id(0)
        dxo_v = dxo_ref[...]
        dgt = jnp.sum(dxo_v * y_ref[...].astype(F32), axis=0, keepdims=True)
        dyb = (dxo_v * gt_ref[0]).astype(BF16)
        dy_ref[...] = dyb
        du = _nt(dyb, wt_ref[...])
        for n, (o_ref, do_ref) in enumerate(((oa_ref, doa_ref), (ob_ref, dob_ref), (oc_ref, doc_ref))):
            sl = slice(n * GW, (n + 1) * GW)
            gv = g_ref[:, sl].astype(F32)
            sg = _sigmoid(gv)
            dun = du[:, sl]
            do_ref[...] = dun * (gv * sg)
            dg_ref[:, sl] = (dun * o_ref[...] * (sg * (1.0 + gv * (1.0 - sg)))).astype(BF16)

        @pl.when(i % tps == 0)
        def _():
            dgt_ref[0] = dgt

        @pl.when(i % tps != 0)
        def _():
            dgt_ref[0] += dgt

    row = lambda wd: pl.BlockSpec((tm, wd), lambda i: (i, 0))
    per_b = pl.BlockSpec((1, 1, D_MODEL), lambda i: (i // tps, 0, 0))
    in_specs = [row(D_MODEL), row(D_MODEL), per_b, row(GW), row(GW), row(GW), row(U_PAD),
                pl.BlockSpec((U_PAD, D_MODEL), lambda i: (0, 0))]
    body, in_specs, args = _after(dep, body, in_specs, [dxo, y, gate, oa, ob, oc, gates, w_t])
    return pl.pallas_call(
        body, name="gate_out_bwd", grid=(t // tm,), in_specs=in_specs,
        out_specs=[row(D_MODEL), row(GW), row(GW), row(GW), row(U_PAD), per_b],
        out_shape=[jax.ShapeDtypeStruct((t, D_MODEL), BF16), jax.ShapeDtypeStruct((t, GW), F32),
                   jax.ShapeDtypeStruct((t, GW), F32), jax.ShapeDtypeStruct((t, GW), F32),
                   jax.ShapeDtypeStruct((t, U_PAD), BF16), jax.ShapeDtypeStruct((nb, 1, D_MODEL), F32)],
        compiler_params=_cparams(("arbitrary",)),
    )(*args)


def _final_loss(x, target, g, tm=ROW_TILE):
    t = x.shape[0]
    tm = min(tm, t)

    def body(x_ref, t_ref, g_ref, dx_ref, loss_ref, dg_ref):
        i = pl.program_id(0)
        xv = x_ref[...]
        rstd = lax.rsqrt(jnp.mean(xv * xv, axis=-1, keepdims=True) + EPS)
        xn = xv * rstd
        gv = g_ref[...]
        err = xn * gv - t_ref[...]
        dy = err * (1.0 / D_MODEL)
        dxn = dy * gv
        dx_ref[...] = rstd * (dxn - xn * jnp.mean(dxn * xn, axis=-1, keepdims=True))
        lp = jnp.sum(err * err, axis=0, keepdims=True) * (0.5 / D_MODEL)
        dgp = jnp.sum(dy * xn, axis=0, keepdims=True)

        @pl.when(i == 0)
        def _():
            loss_ref[...] = lp
            dg_ref[...] = dgp

        @pl.when(i != 0)
        def _():
            loss_ref[...] += lp
            dg_ref[...] += dgp

    row = pl.BlockSpec((tm, D_MODEL), lambda i: (i, 0))
    vec = pl.BlockSpec((1, D_MODEL), lambda i: (0, 0))
    return pl.pallas_call(
        body, name="final_loss", grid=(t // tm,),
        in_specs=[row, row, vec], out_specs=[row, vec, vec],
        out_shape=[jax.ShapeDtypeStruct((t, D_MODEL), F32), jax.ShapeDtypeStruct((1, D_MODEL), F32),
                   jax.ShapeDtypeStruct((1, D_MODEL), F32)],
        compiler_params=_cparams(("arbitrary",)),
    )(x, target, g)


def _adamw(w, gslots, m, v, name, tr=None):
    nl, r, c = w.shape
    ns = gslots.shape[0]
    tr = r if tr is None else tr

    def body(w_ref, g_ref, m_ref, v_ref, go_ref, d_ref, mo_ref, vo_ref):
        g = g_ref[0].astype(F32)
        for j in range(1, ns):
            g = g + g_ref[j].astype(F32)
        mn = ADAM_B1 * m_ref[...] + (1.0 - ADAM_B1) * g
        vn = ADAM_B2 * v_ref[...] + (1.0 - ADAM_B2) * jnp.square(g)
        m_hat = mn / (1.0 - ADAM_B1 ** ADAM_STEP)
        v_hat = vn / (1.0 - ADAM_B2 ** ADAM_STEP)
        go_ref[...] = g
        d_ref[...] = -ADAM_LR * (m_hat / (jnp.sqrt(v_hat) + ADAM_EPS) + ADAM_WD * w_ref[...])
        mo_ref[...] = mn
        vo_ref[...] = vn

    blk = pl.BlockSpec((1, tr, c), lambda l, i: (l, i, 0))
    return pl.pallas_call(
        body, name=name, grid=(nl, r // tr),
        in_specs=[blk, pl.BlockSpec((ns, 1, tr, c), lambda l, i: (0, l, i, 0)), blk, blk],
        out_specs=[blk] * 4, out_shape=[jax.ShapeDtypeStruct((nl, r, c), F32)] * 4,
        compiler_params=_cparams(("arbitrary", "arbitrary")),
    )(w, gslots, m, v)


def _rope_tables(positions):
    inv = ROPE_THETA ** (-jnp.arange(0, A_ROPE, 2, dtype=F32) / A_ROPE)
    ang = positions.astype(F32)[:, None] * inv
    cos, sin = jnp.cos(ang), jnp.sin(ang)
    t = positions.shape[0]
    one = jnp.ones((t, 64), F32)
    zero16 = jnp.zeros((t, 16), F32)
    cos_t = jnp.concatenate([one, cos, cos, jnp.ones((t, 32), F32)], axis=1)
    sin_a = jnp.concatenate([jnp.zeros((t, 64), F32), -sin, zero16, jnp.zeros((t, 32), F32)], axis=1)
    sin_b = jnp.concatenate([jnp.zeros((t, 64), F32), zero16, sin, jnp.zeros((t, 32), F32)], axis=1)
    return cos_t, sin_a, sin_b


def _pad_heads(w, real, padded, nheads, axis):
    shp = w.shape[:axis] + (nheads, real) + w.shape[axis + 1:]
    w = w.reshape(shp)
    pad = [(0, 0)] * w.ndim
    pad[axis + 1] = (0, padded - real)
    w = jnp.pad(w, pad)
    return w.reshape(w.shape[:axis] + (nheads * padded,) + w.shape[axis + 2:])


def kernel(x, c, positions, w_ada, b_ada, norm_g, w_in, a_q_norm_g, a_w_uq, a_kv_norm_g, a_w_ukv, b_rel_bias, c_forget_b, w_out, final_g, loss_target, m_w_ada, m_b_ada, m_norm_g, m_w_in, m_a_q_norm_g, m_a_w_uq, m_a_kv_norm_g, m_a_w_ukv, m_b_rel_bias, m_c_forget_b, m_w_out, m_final_g, v_w_ada, v_b_ada, v_norm_g, v_w_in, v_a_q_norm_g, v_a_w_uq, v_a_kv_norm_g, v_a_w_ukv, v_b_rel_bias, v_c_forget_b, v_w_out, v_final_g):
    nb, seq, _ = x.shape
    t = nb * seq
    me = 4 * lax.axis_index("x") + 2 * lax.axis_index("y") + lax.axis_index("c")
    x2 = x.reshape(t, D_MODEL)
    tgt = loss_target.reshape(t, D_MODEL)
    cos_t, sin_a, sin_b = _rope_tables(positions.reshape(t))

    def shards(l):
        return [_pad_runs(w_in[l].astype(BF16), IN_RUNS, N_PAD, 1), w_out[l].astype(BF16),
                a_w_uq[l].astype(BF16), a_w_ukv[l].astype(BF16)]

    def prepare(gi, go, gq, gkv):
        return dict(w_in=gi.reshape(D_MODEL, N_PAD), **prepare_rest(go, gq, gkv))

    def prepare_rest(go, gq, gkv):
        wo = _pad_runs(go.reshape(D_MODEL, D_MODEL), OUT_RUNS, U_PAD, 0)
        wq = jnp.transpose(gq, (1, 0, 2)).reshape(A_Q_RANK, A_HEADS * (A_NOPE + A_ROPE))
        wq = _pad_heads(wq, A_NOPE + A_ROPE, HEAD_PAD, A_HEADS, 1)
        wkv = jnp.transpose(gkv, (1, 0, 2)).reshape(A_KV_RANK, A_HEADS, 2 * A_NOPE)
        wk = jnp.pad(wkv[:, :, :A_NOPE], ((0, 0), (0, 0), (0, HEAD_PAD - A_NOPE))).reshape(A_KV_RANK, A_HEADS * HEAD_PAD)
        wv = wkv[:, :, A_NOPE:].reshape(A_KV_RANK, GW)
        return dict(w_out=wo, wuq=wq, wuq_t=wq.T, wk=wk, wk_t=wk.T, wv=wv, wv_t=wv.T)

    shards0 = shards(0)
    w_in0_g, c_g = _gather([shards0[0], c], "gather_w_in0")
    c_all = c_g.reshape(N_DEV * nb, D_MODEL)
    weights = [dict(w_in=w_in0_g.reshape(D_MODEL, N_PAD)), None]

    c_act, mod_cols = _ada_fwd(c_all, w_ada)
    (mod_g,) = _gather([mod_cols], "gather_mod")
    rest0, rest0_token = _split_start("gather", shards0[1:], "gather_rest0_start", after=mod_g)
    mod_all = jnp.transpose(mod_g, (1, 2, 0, 3)).reshape(DEPTH, N_DEV * nb, 3 * D_MODEL)
    mod = lax.dynamic_slice_in_dim(mod_all, me * nb, nb, axis=1) + b_ada[:, None, :]

    fb_pad = jnp.pad(c_forget_b, ((0, 0), (0, 128 - C_HEADS)))
    a_scale = (A_NOPE + A_ROPE) ** -0.5
    h_scale = CHUNK ** -0.5

    saved = []
    xl = x2
    for l in range(DEPTH):
        if l == 1:
            weights[1] = prepare(*_split_wait(gather1, xl, "gather_weights1_wait")[1])
        w = weights[l]
        shift, scale, gate = mod[l, :, :D_MODEL], mod[l, :, D_MODEL:2 * D_MODEL], mod[l, :, 2 * D_MODEL:]
        ss = jnp.stack([shift, 1.0 + scale], axis=1)
        gate3 = gate[:, None, :]
        h, cq, ckv, kpe, gates, bq, bk, bv, cq2, ck, cv, cf = _ln_in(
            xl, ss, norm_g[l:l + 1], w["w_in"], seq, dep=rest0_token if l == 0 else None)
        gather1_token = None
        if l == 0:
            w.update(prepare_rest(*_split_wait(rest0, h, "gather_rest0_wait")[1]))
            gather1, gather1_token = _split_start("gather", shards(1), "gather_weights1_start", after=w["w_out"])
        q, k, v, cqn, ckvn = _mla_prep(cq, ckv, kpe, a_q_norm_g[l:l + 1], a_kv_norm_g[l:l + 1],
                                       w["wuq"], w["wk"], w["wv"], cos_t, sin_a, sin_b, dep=gather1_token)
        oa, sta = _attn_fwd("mla", q, k, v, None, seq, a_scale)
        table = _band_table(jnp.pad(b_rel_bias[l], ((0, 8 - B_HEADS), (0, GW - N_REL))))
        ob, stb = _band_fwd(bq, bk, bv, table, seq, h_scale)
        fcum = _fox_prep(cf, fb_pad[l:l + 1], seq)
        oc, stc = _attn_fwd("fox", cq2, ck, cv, fcum, seq, h_scale)
        xn, y, u = _gate_out(oa, ob, oc, gates, w["w_out"], xl, gate3, seq)
        saved.append(dict(x=xl, ss=ss, gate3=gate3, h=h, cq=cq, ckv=ckv, gates=gates, bq=bq, bk=bk, bv=bv,
                          cq2=cq2, ck=ck, cv=cv, cf=cf, q=q, k=k, v=v, cqn=cqn, ckvn=ckvn, oa=oa, sta=sta,
                          table=table, ob=ob, stb=stb, fcum=fcum, oc=oc, stc=stc, y=y, u=u))
        xl = xn

    dx, loss_lanes, g_final = _final_loss(xl, tgt, final_g[None, :])
    loss = lax.psum(jnp.sum(loss_lanes), AXES)

    rows = D_MODEL // N_DEV
    core = lax.axis_index("c").astype(jnp.int32).reshape(1)
    n_seg_a = 4
    dmods, smalls, parts = [None] * DEPTH, [None] * DEPTH, [None] * DEPTH
    pair1 = chips1 = pair1_token = chips1_token = None
    for l in reversed(range(DEPTH)):
        s, w = saved[l], weights[l]
        dy, doa, dob, doc, dgates, dgate = _gate_out_bwd(dx, s["y"], s["gate3"], s["oa"], s["ob"], s["oc"],
                                                         s["gates"], w["w_out"], seq, dep=pair1_token)
        g_out = _unpad_runs(_matmul_tn(s["u"], dy, "dw_out"), OUT_RUNS, 0)
        if l == 0:
            own, from_sib = _split_wait(pair1, g_out, "grads1_pair_wait")
            chips1, chips1_token = _split_start("chips", _pair_add(core, own, from_sib, "grads1_add"), "grads1_chips_start")
        dq, dk, dv = _attn_bwd("mla", s["q"], s["k"], s["v"], None, s["oa"], s["sta"], doa, seq, a_scale,
                               dep=chips1_token)
        dbq, dbk, dbv, gtab = _band_bwd(s["bq"], s["bk"], s["bv"], s["table"], s["ob"], s["stb"], dob, seq, h_scale,
                                        dep=chips1_token)
        g_rel = _band_table_bwd(gtab)[:, 0, :N_REL]
        dcq2, dck, dcv, dfc, dfq = _attn_bwd("fox", s["cq2"], s["ck"], s["cv"], s["fcum"], s["oc"], s["stc"], doc,
                                             seq, h_scale, dep=chips1_token)
        dcf, dfb = _fox_prep_bwd(dfc, dfq, s["cf"], fb_pad[l:l + 1], seq)
        dcq, dckv, dkpe, dqlin, dklin, dgq, dgkv = _mla_prep_bwd(
            dq, dk, dv, s["cq"], s["ckv"], a_q_norm_g[l:l + 1], a_kv_norm_g[l:l + 1],
            w["wuq_t"], w["wk_t"], w["wv_t"], cos_t, sin_a, sin_b)
        gq_pad = _matmul_tn(s["cqn"], dqlin, "dw_uq")
        g_uq = gq_pad.reshape(A_Q_RANK, A_HEADS, HEAD_PAD)[:, :, :A_NOPE + A_ROPE].reshape(A_Q_RANK, -1)
        gkv_pad = _matmul_tn(s["ckvn"], [dklin, dv], "dw_ukv")
        gk_pad = gkv_pad[:, :A_HEADS * HEAD_PAD].reshape(A_KV_RANK, A_HEADS, HEAD_PAD)[:, :, :A_NOPE]
        gv_pad = gkv_pad[:, A_HEADS * HEAD_PAD:].reshape(A_KV_RANK, A_HEADS, A_NOPE)
        g_ukv = jnp.concatenate([gk_pad, gv_pad], axis=2).reshape(A_KV_RANK, -1)
        dz = [dcq, dckv, dkpe, dgates, dbq, dbk, dbv, dcq2, dck, dcv, dcf]
        g_in_a = _matmul_tn(s["h"], dz[:n_seg_a], "dw_in_a")
        first = [g_in_a.reshape(N_DEV, rows, -1), g_out.reshape(N_DEV, rows, D_MODEL),
                 g_uq.reshape(A_Q_RANK, N_DEV, -1).transpose(1, 0, 2), g_ukv.reshape(A_KV_RANK, N_DEV, -1).transpose(1, 0, 2)]
        if l == 1:
            g_in_b = _matmul_tn(s["h"], dz[n_seg_a:], "dw_in_b")
            pair1, pair1_token = _split_start("pair", first + [g_in_b.reshape(N_DEV, rows, -1)], "grads1_pair_start")
            tail_token = None
        else:
            pair0a, pair0a_token = _split_start("pair", first, "grads0a_pair_start")
            g_in_b = _matmul_tn(s["h"], dz[n_seg_a:], "dw_in_b", dep=pair0a_token)
            own, from_sib = _split_wait(pair0a, g_in_b, "grads0a_pair_wait")
            sums0a = _pair_add(core, own, from_sib, "grads0a_add")
            pair0b, pair0b_token = _split_start("pair", [g_in_b.reshape(N_DEV, rows, -1)], "grads0b_pair_start",
                                                after=sums0a[0])
            chips0a, tail_token = _split_start("chips", sums0a, "grads0a_chips_start", after=pair0b_token)
        dx, dss, dg_norm = _ln_in_bwd(dz, w["w_in"], s["x"], s["ss"], norm_g[l:l + 1], dx, seq, dep=tail_token)
        dmods[l] = jnp.concatenate([dss[:, 0, :], dss[:, 1, :], dgate[:, 0, :]], axis=1)
        smalls[l] = [dg_norm.reshape(-1), dgq.reshape(-1), dgkv.reshape(-1), g_rel.reshape(-1),
                     dfb[0, :C_HEADS]]
    grad_x = dx.reshape(nb, seq, D_MODEL)
    parts[1] = _split_wait(chips1, dx, "grads1_chips_wait")[1]
    parts0a = _split_wait(chips0a, dx, "grads0a_chips_wait")[1]
    own, from_sib = _split_wait(pair0b, dx, "grads0b_pair_wait")

    small = jnp.concatenate([p for l in range(DEPTH) for p in smalls[l]] + [g_final.reshape(-1)])
    n_small = small.shape[0]
    small_rows = -(-n_small // 1024) * 8
    small = jnp.pad(small, (0, small_rows * 128 - n_small)).reshape(small_rows, 128)
    dmod_local = jnp.stack(dmods)
    dmod_g, small_g = _gather([dmod_local, small], "gather_small", dep=parts0a[0])
    chips0, chips0_token = _split_start("chips", _pair_add(core, own, from_sib, "grads0b_add"), "grads0b_chips_start",
                                        after=small_g)
    dmod_all = jnp.transpose(dmod_g, (1, 0, 2, 3)).reshape(DEPTH, N_DEV * nb, 3 * D_MODEL)
    cols = 3 * D_MODEL // N_DEV
    dmod_mine = lax.dynamic_slice_in_dim(dmod_all, me * cols, cols, axis=2)
    g_w_ada, g_b_ada = _ada_bwd(c_act, dmod_all, dmod_mine, chips0_token)
    small_sum = _sum_slots(small_g, "sum_small").reshape(-1)

    def split_small():
        out, pos = [], 0
        sizes = [D_MODEL, A_Q_RANK, A_KV_RANK, B_HEADS * N_REL, C_HEADS]
        per_layer = []
        for l in range(DEPTH):
            parts = []
            for sz in sizes:
                parts.append(small_sum[pos:pos + sz])
                pos += sz
            per_layer.append(parts)
        for j in range(len(sizes)):
            out.append(jnp.stack([per_layer[l][j] for l in range(DEPTH)]))
        out.append(small_sum[pos:pos + D_MODEL])
        return out

    g_norm, g_qn, g_kvn, g_relb, g_fb, g_fin = split_small()

    def adam(w, g, m, v, name, tr=None):
        shp = w.shape
        w3 = w.reshape((1,) * (3 - w.ndim) + shp)
        outs = _adamw(w3, g.reshape((-1,) + w3.shape), m.reshape(w3.shape), v.reshape(w3.shape), name, tr)
        return [o.reshape(shp) for o in outs]

    res = {
        "w_ada": adam(w_ada, g_w_ada, m_w_ada, v_w_ada, "adam_w_ada", 256),
        "b_ada": adam(b_ada, g_b_ada, m_b_ada, v_b_ada, "adam_b_ada"),
        "norm_g": adam(norm_g, g_norm, m_norm_g, v_norm_g, "adam_norm_g"),
        "a_q_norm_g": adam(a_q_norm_g, g_qn, m_a_q_norm_g, v_a_q_norm_g, "adam_q_norm"),
        "a_kv_norm_g": adam(a_kv_norm_g, g_kvn, m_a_kv_norm_g, v_a_kv_norm_g, "adam_kv_norm"),
        "b_rel_bias": adam(b_rel_bias, g_relb.reshape(b_rel_bias.shape), m_b_rel_bias, v_b_rel_bias, "adam_rel_bias"),
        "c_forget_b": adam(c_forget_b, g_fb, m_c_forget_b, v_c_forget_b, "adam_forget_b"),
        "final_g": adam(final_g, g_fin, m_final_g, v_final_g, "adam_final_g"),
    }
    parts[0] = list(parts0a) + list(_split_wait(chips0, res["w_ada"][1], "grads0b_chips_wait")[1])
    p_in = jnp.stack([_unpad_runs(jnp.concatenate([parts[l][0], parts[l][4]], axis=2), IN_RUNS, 2)
                      for l in range(DEPTH)], axis=1)
    p_out, p_uq, p_ukv = (jnp.stack([parts[l][j] for l in range(DEPTH)], axis=1) for j in (1, 2, 3))
    res.update({
        "w_in": adam(w_in, p_in, m_w_in, v_w_in, "adam_w_in", 64),
        "a_w_uq": adam(a_w_uq, p_uq, m_a_w_uq, v_a_w_uq, "adam_w_uq"),
        "a_w_ukv": adam(a_w_ukv, p_ukv, m_a_w_ukv, v_a_w_ukv, "adam_w_ukv"),
        "w_out": adam(w_out, p_out, m_w_out, v_w_out, "adam_w_out", 64),
    })
    names = ["w_ada", "b_ada", "norm_g", "w_in", "a_q_norm_g", "a_w_uq", "a_kv_norm_g", "a_w_ukv", "b_rel_bias",
             "c_forget_b", "w_out", "final_g"]
    outs = [loss, grad_x]
    for j in range(4):
        outs += [res[n][j] for n in names]
    return tuple(outs)
```

```python
import math

import jax
import jax.numpy as jnp
from jax import lax
from jax.experimental import pallas as pl
from jax.experimental.pallas import tpu as pltpu

F32 = jnp.float32
BF16 = jnp.bfloat16
HI = lax.Precision.HIGHEST

N_DEV = 8
AXES = ("x", "y", "c")
D_MODEL = 1024
DEPTH = 2
CHUNK = 64
EPS = 1e-6
NEG = -1e30
A_HEADS = 6
A_NOPE = 64
A_ROPE = 32
A_Q_RANK = 384
A_KV_RANK = 256
ROPE_THETA = 10000.0
B_HEADS = 5
B_LEFT = 512
REL_CLIP = 128
N_REL = 2 * REL_CLIP + 1
C_HEADS = 5
HEAD_PAD = 128
GW = 384
N_IN = 3621
ADAM_LR = 0.001
ADAM_B1 = 0.9
ADAM_B2 = 0.999
ADAM_EPS = 1e-08
ADAM_WD = 0.01
ADAM_STEP = 10
VMEM_LIMIT = 56 * 1024 * 1024
ROW_TILE = 512

Z_SEGS = (
    ("cq", 0, 384, F32), ("ckv", 384, 256, F32), ("kpe", 640, 128, F32), ("gates", 768, 1152, BF16),
    ("bq", 1920, 384, BF16), ("bk", 2304, 384, BF16), ("bv", 2688, 384, BF16),
    ("cq2", 3072, 384, BF16), ("ck", 3456, 384, BF16), ("cv", 3840, 384, BF16), ("cf", 4224, 128, F32),
)
N_PAD = 4352
IN_RUNS = (
    (0, 384, 0), (384, 256, 384), (640 + 64, 32, 640),
    (768, 384, 672), (768 + 384, 320, 2016), (768 + 768, 320, 3301),
    (1920, 320, 1056), (2304, 320, 1376), (2688, 320, 1696),
    (3072, 320, 2336), (3456, 320, 2656), (3840, 320, 2976), (4224, 5, 3296),
)
OUT_RUNS = ((0, 384, 0), (384, 320, 384), (768, 320, 704))
U_PAD = 1152


def _cparams(sem=None, vmem=VMEM_LIMIT):
    return pltpu.CompilerParams(dimension_semantics=sem, vmem_limit_bytes=vmem)


def _after(dep, body, in_specs, args):
    if dep is None:
        return body, in_specs, args
    n = len(args)

    def ordered(*refs):
        return body(*refs[:n], *refs[n + 1:])

    return ordered, list(in_specs) + [pl.BlockSpec((8, 128), lambda *_: (0, 0))], list(args) + [dep]


def _pad_runs(w, runs, total, axis):
    order = sorted(runs)
    parts, pos = [], 0
    for off, wd, src in order:
        if off > pos:
            shp = list(w.shape)
            shp[axis] = off - pos
            parts.append(jnp.zeros(shp, w.dtype))
        parts.append(lax.slice_in_dim(w, src, src + wd, axis=axis))
        pos = off + wd
    if pos < total:
        shp = list(w.shape)
        shp[axis] = total - pos
        parts.append(jnp.zeros(shp, w.dtype))
    return jnp.concatenate(parts, axis=axis)


def _unpad_runs(w, runs, axis):
    order = sorted(runs, key=lambda r: r[2])
    return jnp.concatenate([lax.slice_in_dim(w, off, off + wd, axis=axis) for off, wd, _ in order], axis=axis)


def _sigmoid(x):
    return 1.0 / (1.0 + jnp.exp(-x))


N_CHIP = 4
ANY_SPEC = pl.BlockSpec(memory_space=pl.ANY)
MESH_ID = pl.DeviceIdType.MESH


def _gather(arrs, name, dep=None):
    n = len(arrs)
    nin = n + (dep is not None)

    def body(*refs):
        ins, outs = refs[:n], refs[nin:nin + n]
        send_sems, recv_sems, local_sems = refs[nin + n:]
        x, y, c = lax.axis_index("x"), lax.axis_index("y"), lax.axis_index("c")
        me, sib = (x, y, c), (x, y, 1 - c)
        chips = [(1 - x, y), (x, 1 - y), (1 - x, 1 - y)]

        def slot(px, py, pc):
            return 4 * px + 2 * py + pc

        def copy(a, k, block, to, src=None):
            dst = outs[a].at[slot(*block)]
            return pltpu.make_async_remote_copy(
                src_ref=dst if src is None else src, dst_ref=dst, send_sem=send_sems.at[a, k],
                recv_sem=recv_sems.at[a, k], device_id=to, device_id_type=MESH_ID)

        local = [pltpu.make_async_copy(ins[a], outs[a].at[slot(*me)], local_sems.at[a]) for a in range(n)]
        first = []
        for a in range(n):
            first.append(copy(a, 0, me, sib, src=ins[a]))
            first += [copy(a, 1 + j, me, (*chip, c), src=ins[a]) for j, chip in enumerate(chips)]
        for cp in local + first:
            cp.start()
        passed = []
        for j, chip in enumerate(chips):
            for a in range(n):
                copy(a, 1 + j, (*chip, c), me).wait_recv()
                fwd = copy(a, 4 + j, (*chip, c), sib)
                fwd.start()
                passed.append(fwd)
        for a in range(n):
            copy(a, 0, sib, me).wait_recv()
            for j, chip in enumerate(chips):
                copy(a, 4 + j, (*chip, 1 - c), me).wait_recv()
        for cp in first + passed:
            cp.wait_send()
        for cp in local:
            cp.wait()

    return pl.pallas_call(
        body, name=name, out_shape=[jax.ShapeDtypeStruct((N_DEV,) + a.shape, a.dtype) for a in arrs],
        in_specs=[ANY_SPEC] * nin, out_specs=[ANY_SPEC] * n,
        scratch_shapes=[pltpu.SemaphoreType.DMA((n, N_DEV - 1)), pltpu.SemaphoreType.DMA((n, N_DEV - 1)),
                        pltpu.SemaphoreType.DMA((n,))],
    )(*arrs, *([] if dep is None else [dep]))


HBM_SPEC = pl.BlockSpec(memory_space=pltpu.HBM)
SEM_SPEC = pl.BlockSpec(memory_space=pltpu.SEMAPHORE)
SPLIT_EFFECT = pltpu.SideEffectType.DATAFLOW_SIDE_EFFECTING
SPLIT_SEMS = {"gather": (N_DEV - 1, True), "pair": (N_CHIP, False), "chips": (N_CHIP - 1, True)}


def _split_descriptors(pattern, srcs, lands, sems):
    x, y, c = lax.axis_index("x"), lax.axis_index("y"), lax.axis_index("c")
    nsem, has_local = SPLIT_SEMS[pattern]
    per = 2 * nsem + int(has_local)
    starts, arrivals, local = [], [], []

    def remote(a, k, src, dst, to):
        return pltpu.make_async_remote_copy(src_ref=src, dst_ref=dst, send_sem=sems[a * per + k],
                                            recv_sem=sems[a * per + nsem + k], device_id=to, device_id_type=MESH_ID)

    for a in range(len(srcs)):
        if pattern == "gather":
            me = 4 * x + 2 * y + c
            local.append(pltpu.make_async_copy(srcs[a], lands[a].at[me], sems[a * per + 2 * nsem]))
            for k in range(1, N_DEV):
                px = (1 - x) if (k >> 2) & 1 else x
                py = (1 - y) if (k >> 1) & 1 else y
                pc = (1 - c) if k & 1 else c
                starts.append(remote(a, k - 1, srcs[a], lands[a].at[me], (px, py, pc)))
                arrivals.append(remote(a, k - 1, srcs[a], lands[a].at[4 * px + 2 * py + pc], (px, py, pc)))
        elif pattern == "pair":
            for q in range(N_CHIP):
                cp = remote(a, q, srcs[a].at[2 * q + 1 - c], lands[a].at[q], (x, y, 1 - c))
                starts.append(cp)
                arrivals.append(cp)
        else:
            mine = 2 * x + y
            local.append(pltpu.make_async_copy(srcs[a].at[mine], lands[a].at[mine], sems[a * per + 2 * nsem]))
            for k in range(1, N_CHIP):
                px = (1 - x) if (k >> 1) & 1 else x
                py = (1 - y) if k & 1 else y
                starts.append(remote(a, k - 1, srcs[a].at[2 * px + py], lands[a].at[mine], (px, py, c)))
                arrivals.append(remote(a, k - 1, srcs[a].at[2 * px + py], lands[a].at[2 * px + py], (px, py, c)))
    return starts, arrivals, local


def _split_start(pattern, arrs, name, after=None):
    n = len(arrs)
    extra = [] if after is None else [after]
    nsem, has_local = SPLIT_SEMS[pattern]
    if pattern == "gather":
        land_shapes = [(N_DEV,) + a.shape for a in arrs]
    elif pattern == "pair":
        land_shapes = [(N_CHIP,) + a.shape[1:] for a in arrs]
    else:
        land_shapes = [a.shape for a in arrs]
    nsem_out = n * (2 * nsem + int(has_local))

    def body(*refs):
        srcs, lands = refs[:n], refs[n:2 * n]
        first_sem = 2 * n + len(extra)
        sems = refs[first_sem:first_sem + nsem_out]
        token = refs[-1]
        starts, _, local = _split_descriptors(pattern, srcs, lands, sems)
        for cp in local + starts:
            cp.start()
        token[...] = jnp.zeros_like(token)

    out_shape = ([pltpu.SemaphoreType.DMA(())] * nsem_out + [pltpu.HBM(a.shape, a.dtype) for a in arrs]
                 + [pltpu.HBM(s, a.dtype) for s, a in zip(land_shapes, arrs)] + [jax.ShapeDtypeStruct((8, 128), F32)])
    ins = ([pltpu.with_memory_space_constraint(a, pltpu.HBM) for a in arrs]
           + [pltpu.with_memory_space_constraint(lax.empty(s, a.dtype), pltpu.HBM) for s, a in zip(land_shapes, arrs)])
    outs = pl.pallas_call(
        body, name=name, out_shape=out_shape, in_specs=[HBM_SPEC] * (2 * n) + [ANY_SPEC] * len(extra),
        out_specs=[SEM_SPEC] * nsem_out + [HBM_SPEC] * (2 * n) + [pl.BlockSpec(memory_space=pltpu.VMEM)],
        input_output_aliases={i: nsem_out + i for i in range(2 * n)},
        compiler_params=pltpu.CompilerParams(has_side_effects=SPLIT_EFFECT),
    )(*ins, *extra)
    handle = dict(pattern=pattern, n=n, sems=outs[:nsem_out], srcs=outs[nsem_out:nsem_out + n],
                  lands=outs[nsem_out + n:nsem_out + 2 * n])
    return handle, outs[-1]


def _split_wait(handle, after, name):
    pattern, n = handle["pattern"], handle["n"]
    nsem_in = len(handle["sems"])

    def body(*refs):
        srcs, lands = refs[:n], refs[n:2 * n]
        starts, arrivals, local = _split_descriptors(pattern, srcs, lands, refs[2 * n:2 * n + nsem_in])
        for cp in starts:
            cp.wait_send()
        for cp in arrivals:
            cp.wait_recv()
        for cp in local:
            cp.wait()

    srcs, lands = handle["srcs"], handle["lands"]
    outs = pl.pallas_call(
        body, name=name,
        out_shape=[pltpu.HBM(a.shape, a.dtype) for a in srcs] + [pltpu.HBM(a.shape, a.dtype) for a in lands],
        in_specs=[HBM_SPEC] * (2 * n) + [SEM_SPEC] * nsem_in + [ANY_SPEC], out_specs=[HBM_SPEC] * (2 * n),
        input_output_aliases={i: i for i in range(2 * n)},
        compiler_params=pltpu.CompilerParams(has_side_effects=SPLIT_EFFECT),
    )(*srcs, *lands, *handle["sems"], after)
    return outs[:n], outs[n:]


def _pair_add(core, a8s, b4s, name):
    n = len(a8s)

    def body(core_ref, *refs):
        for i in range(n):
            refs[2 * n + i][...] = (refs[i][...] + refs[n + i][...]).astype(BF16)

    own = [pl.BlockSpec((1,) + b.shape[1:], lambda q, core_ref: (2 * q + core_ref[0], 0, 0)) for b in b4s]
    slot = [pl.BlockSpec((1,) + b.shape[1:], lambda q, core_ref: (q, 0, 0)) for b in b4s]
    grid_spec = pltpu.PrefetchScalarGridSpec(num_scalar_prefetch=1, grid=(N_CHIP,), in_specs=own + slot, out_specs=slot)
    return pl.pallas_call(
        body, name=name, grid_spec=grid_spec, out_shape=[jax.ShapeDtypeStruct(b.shape, BF16) for b in b4s],
        compiler_params=_cparams(("arbitrary",)),
    )(core, *a8s, *b4s)


def _sum_slots(x, name):
    _, r, c = x.shape

    def body(x_ref, o_ref):
        acc = x_ref[0]
        for j in range(1, N_DEV):
            acc = acc + x_ref[j]
        o_ref[...] = acc

    return pl.pallas_call(body, name=name, out_shape=jax.ShapeDtypeStruct((r, c), F32))(x)


def _ada_fwd(c_all, w_ada):
    nb = c_all.shape[0]
    cols = w_ada.shape[2]

    def body(c_ref, w_ref, act_ref, mod_ref):
        cv = c_ref[...]
        act = cv * _sigmoid(cv)
        act_ref[...] = act
        for l in range(DEPTH):
            mod_ref[l] = jnp.dot(act, w_ref[l], precision=HI, preferred_element_type=F32)

    return pl.pallas_call(
        body, name="ada_fwd",
        out_shape=[jax.ShapeDtypeStruct((nb, D_MODEL), F32), jax.ShapeDtypeStruct((DEPTH, nb, cols), F32)],
        compiler_params=_cparams(),
    )(c_all, w_ada)


def _ada_bwd(c_act, dmod_all, dmod_mine, dep):
    nb = c_act.shape[0]
    cols = dmod_mine.shape[2]

    def body(act_ref, dall_ref, dmine_ref, dep_ref, gw_ref, gb_ref):
        act = act_ref[...]
        for l in range(DEPTH):
            gw_ref[l] = lax.dot_general(act, dmine_ref[l], (((0,), (0,)), ((), ())),
                                        precision=HI, preferred_element_type=F32)
            gb_ref[l:l + 1, :] = jnp.sum(dall_ref[l], axis=0, keepdims=True)

    return pl.pallas_call(
        body, name="ada_bwd",
        out_shape=[jax.ShapeDtypeStruct((DEPTH, D_MODEL, cols), F32),
                   jax.ShapeDtypeStruct((DEPTH, 3 * D_MODEL), F32)],
        compiler_params=_cparams(),
    )(c_act, dmod_all, dmod_mine, dep)


def _ln_in(x, ss, g, w, seq, tm=ROW_TILE, dep=None):
    t = x.shape[0]
    tm = min(tm, seq)
    tps = seq // tm

    def body(x_ref, ss_ref, g_ref, w_ref, h_ref, *outs):
        xv = x_ref[...]
        xn = xv * lax.rsqrt(jnp.mean(xv * xv, axis=-1, keepdims=True) + EPS)
        h = xn * g_ref[...] * ss_ref[0, 1:2, :] + ss_ref[0, 0:1, :]
        hb = h.astype(BF16)
        h_ref[...] = hb
        z = jnp.dot(hb, w_ref[...], preferred_element_type=F32)
        for o_ref, (_, off, wd, _) in zip(outs, Z_SEGS):
            o_ref[...] = z[:, off:off + wd].astype(o_ref.dtype)

    row = lambda wd: pl.BlockSpec((tm, wd), lambda i: (i, 0))
    in_specs = [row(D_MODEL), pl.BlockSpec((1, 2, D_MODEL), lambda i: (i // tps, 0, 0)),
                pl.BlockSpec((1, D_MODEL), lambda i: (0, 0)), pl.BlockSpec((D_MODEL, N_PAD), lambda i: (0, 0))]
    body, in_specs, args = _after(dep, body, in_specs, [x, ss, g, w])
    return pl.pallas_call(
        body, name="ln_in", grid=(t // tm,), in_specs=in_specs,
        out_specs=[row(D_MODEL)] + [row(wd) for _, _, wd, _ in Z_SEGS],
        out_shape=[jax.ShapeDtypeStruct((t, D_MODEL), BF16)]
        + [jax.ShapeDtypeStruct((t, wd), dt) for _, _, wd, dt in Z_SEGS],
        compiler_params=_cparams(("arbitrary",)),
    )(*args)


def _ln_in_bwd(dz, w_t, x, ss, g, dxo, seq, tm=ROW_TILE, dep=None):
    t = x.shape[0]
    tm = min(tm, seq)
    tps = seq // tm
    nb = t // seq
    nz = len(Z_SEGS)

    def body(*refs):
        dz_refs = refs[:nz]
        wt_ref, x_ref, ss_ref, g_ref, dxo_ref, dx_ref, dss_ref, dg_ref = refs[nz:]
        i = pl.program_id(0)
        dzc = jnp.concatenate([r[...].astype(BF16) for r in dz_refs], axis=1)
        dh = _nt(dzc, wt_ref[...])
        xv = x_ref[...]
        rstd = lax.rsqrt(jnp.mean(xv * xv, axis=-1, keepdims=True) + EPS)
        xn = xv * rstd
        gv = g_ref[...]
        s1 = ss_ref[0, 1:2, :]
        dxg = dh * s1
        dxn = dxg * gv
        dx = rstd * (dxn - xn * jnp.mean(dxn * xn, axis=-1, keepdims=True))
        dx_ref[...] = dxo_ref[...] + dx
        dshift = jnp.sum(dh, axis=0, keepdims=True)
        dscale = jnp.sum(dh * (xn * gv), axis=0, keepdims=True)
        dgp = jnp.sum(dxg * xn, axis=0, keepdims=True)

        @pl.when(i % tps == 0)
        def _():
            dss_ref[0, 0:1, :] = dshift
            dss_ref[0, 1:2, :] = dscale

        @pl.when(i % tps != 0)
        def _():
            dss_ref[0, 0:1, :] += dshift
            dss_ref[0, 1:2, :] += dscale

        @pl.when(i == 0)
        def _():
            dg_ref[...] = dgp

        @pl.when(i != 0)
        def _():
            dg_ref[...] += dgp

    row = lambda wd: pl.BlockSpec((tm, wd), lambda i: (i, 0))
    in_specs = ([row(wd) for _, _, wd, _ in Z_SEGS]
                + [pl.BlockSpec((D_MODEL, N_PAD), lambda i: (0, 0)), row(D_MODEL),
                   pl.BlockSpec((1, 2, D_MODEL), lambda i: (i // tps, 0, 0)),
                   pl.BlockSpec((1, D_MODEL), lambda i: (0, 0)), row(D_MODEL)])
    body, in_specs, args = _after(dep, body, in_specs, [*dz, w_t, x, ss, g, dxo])
    return pl.pallas_call(
        body, name="ln_in_bwd", grid=(t // tm,), in_specs=in_specs,
        out_specs=[row(D_MODEL), pl.BlockSpec((1, 2, D_MODEL), lambda i: (i // tps, 0, 0)),
                   pl.BlockSpec((1, D_MODEL), lambda i: (0, 0))],
        out_shape=[jax.ShapeDtypeStruct((t, D_MODEL), F32), jax.ShapeDtypeStruct((nb, 2, D_MODEL), F32),
                   jax.ShapeDtypeStruct((1, D_MODEL), F32)],
        compiler_params=_cparams(("arbitrary",)),
    )(*args)


def _matmul_tn(a, bs, name, tm=2048, dep=None):
    bs = list(bs) if isinstance(bs, (list, tuple)) else [bs]
    t, k = a.shape
    widths = [b.shape[1] for b in bs]
    n = sum(widths)
    tm = min(tm, t)

    def body(a_ref, *refs):
        b_refs, o_ref = refs[:-1], refs[-1]
        i = pl.program_id(0)
        av = a_ref[...].astype(BF16)
        parts = [b_ref[...].astype(BF16) for b_ref in b_refs]
        bv = parts[0] if len(parts) == 1 else jnp.concatenate(parts, axis=1)
        part = lax.dot_general(av, bv, (((0,), (0,)), ((), ())), preferred_element_type=F32)

        @pl.when(i == 0)
        def _():
            o_ref[...] = part

        @pl.when(i != 0)
        def _():
            o_ref[...] += part

    in_specs = [pl.BlockSpec((tm, k), lambda i: (i, 0))] + [pl.BlockSpec((tm, wd), lambda i: (i, 0)) for wd in widths]
    body, in_specs, args = _after(dep, body, in_specs, [a, *bs])
    return pl.pallas_call(
        body, name=name, grid=(t // tm,), in_specs=in_specs,
        out_specs=pl.BlockSpec((k, n), lambda i: (0, 0)),
        out_shape=jax.ShapeDtypeStruct((k, n), F32),
        compiler_params=_cparams(("arbitrary",)),
    )(*args)


def _rope(blk, cos_t, sin_a, sin_b):
    return blk * cos_t + pltpu.roll(blk, 112, 1) * sin_a + pltpu.roll(blk, 16, 1) * sin_b


def _unrope(d, cos_t, sin_a, sin_b):
    return d * cos_t + pltpu.roll(d * sin_a, 16, 1) + pltpu.roll(d * sin_b, 112, 1)


def _mla_prep(cq, ckv, kpe, gq, gkv, wuq, wk, wv, cos_t, sin_a, sin_b, tm=ROW_TILE, dep=None):
    t = cq.shape[0]
    tm = min(tm, t)
    qw = A_HEADS * HEAD_PAD

    def body(cq_ref, ckv_ref, kpe_ref, gq_ref, gkv_ref, wuq_ref, wk_ref, wv_ref, c_ref, sa_ref, sb_ref,
             q_ref, k_ref, v_ref, cqn_ref, ckvn_ref):
        ct, sa, sb = c_ref[...], sa_ref[...], sb_ref[...]
        a = cq_ref[...]
        cqn = (a * lax.rsqrt(jnp.mean(a * a, axis=-1, keepdims=True) + EPS) * gq_ref[...]).astype(BF16)
        cqn_ref[...] = cqn
        b = ckv_ref[...]
        ckvn = (b * lax.rsqrt(jnp.mean(b * b, axis=-1, keepdims=True) + EPS) * gkv_ref[...]).astype(BF16)
        ckvn_ref[...] = ckvn
        qlin = jnp.dot(cqn, wuq_ref[...], preferred_element_type=F32)
        klin = jnp.dot(ckvn, wk_ref[...], preferred_element_type=F32)
        v_ref[...] = jnp.dot(ckvn, wv_ref[...], preferred_element_type=F32).astype(BF16)
        kr = _rope(kpe_ref[...], ct, sa, sb)
        for h in range(A_HEADS):
            sl = slice(h * HEAD_PAD, (h + 1) * HEAD_PAD)
            q_ref[:, sl] = _rope(qlin[:, sl], ct, sa, sb).astype(BF16)
            k_ref[:, sl] = (klin[:, sl] + kr).astype(BF16)

    row = lambda wd: pl.BlockSpec((tm, wd), lambda i: (i, 0))
    full = lambda r, c: pl.BlockSpec((r, c), lambda i: (0, 0))
    in_specs = [row(A_Q_RANK), row(A_KV_RANK), row(128), full(1, A_Q_RANK), full(1, A_KV_RANK),
                full(A_Q_RANK, qw), full(A_KV_RANK, qw), full(A_KV_RANK, GW), row(128), row(128), row(128)]
    body, in_specs, args = _after(dep, body, in_specs, [cq, ckv, kpe, gq, gkv, wuq, wk, wv, cos_t, sin_a, sin_b])
    return pl.pallas_call(
        body, name="mla_prep", grid=(t // tm,), in_specs=in_specs,
        out_specs=[row(qw), row(qw), row(GW), row(A_Q_RANK), row(A_KV_RANK)],
        out_shape=[jax.ShapeDtypeStruct((t, qw), BF16), jax.ShapeDtypeStruct((t, qw), BF16),
                   jax.ShapeDtypeStruct((t, GW), BF16), jax.ShapeDtypeStruct((t, A_Q_RANK), BF16),
                   jax.ShapeDtypeStruct((t, A_KV_RANK), BF16)],
        compiler_params=_cparams(("arbitrary",)),
    )(*args)


def _mla_prep_bwd(dq, dk, dv, cq, ckv, gq, gkv, wuq_t, wk_t, wv_t, cos_t, sin_a, sin_b, tm=ROW_TILE):
    t = cq.shape[0]
    tm = min(tm, t)
    qw = A_HEADS * HEAD_PAD

    def body(dq_ref, dk_ref, dv_ref, cq_ref, ckv_ref, gq_ref, gkv_ref, wuqt_ref, wkt_ref, wvt_ref,
             c_ref, sa_ref, sb_ref, dcq_ref, dckv_ref, dkpe_ref, dql_ref, dkl_ref, dgq_ref, dgkv_ref):
        i = pl.program_id(0)
        ct, sa, sb = c_ref[...], sa_ref[...], sb_ref[...]
        lane = lax.broadcasted_iota(jnp.int32, (1, HEAD_PAD), 1)
        nope = lane < A_NOPE
        rope = (lane >= A_NOPE) & (lane < A_NOPE + A_ROPE)
        dksum = None
        for h in range(A_HEADS):
            sl = slice(h * HEAD_PAD, (h + 1) * HEAD_PAD)
            dql_ref[:, sl] = _unrope(dq_ref[:, sl], ct, sa, sb).astype(BF16)
            dkh = dk_ref[:, sl]
            dkl_ref[:, sl] = jnp.where(nope, dkh, 0.0).astype(BF16)
            dksum = dkh if dksum is None else dksum + dkh
        dkpe_ref[...] = jnp.where(rope, _unrope(jnp.where(rope, dksum, 0.0), ct, sa, sb), 0.0).astype(BF16)
        dcqn = jnp.dot(dql_ref[...], wuqt_ref[...], preferred_element_type=F32)
        dckvn = (jnp.dot(dkl_ref[...], wkt_ref[...], preferred_element_type=F32)
                 + jnp.dot(dv_ref[...].astype(BF16), wvt_ref[...], preferred_element_type=F32))

        def norm_bwd(xv, gv, dy):
            rstd = lax.rsqrt(jnp.mean(xv * xv, axis=-1, keepdims=True) + EPS)
            xn = xv * rstd
            dxn = dy * gv
            dx = rstd * (dxn - xn * jnp.mean(dxn * xn, axis=-1, keepdims=True))
            return dx, jnp.sum(dy * xn, axis=0, keepdims=True)

        dcq, dgq = norm_bwd(cq_ref[...], gq_ref[...], dcqn)
        dckv, dgkv = norm_bwd(ckv_ref[...], gkv_ref[...], dckvn)
        dcq_ref[...] = dcq.astype(BF16)
        dckv_ref[...] = dckv.astype(BF16)

        @pl.when(i == 0)
        def _():
            dgq_ref[...] = dgq
            dgkv_ref[...] = dgkv

        @pl.when(i != 0)
        def _():
            dgq_ref[...] += dgq
            dgkv_ref[...] += dgkv

    row = lambda wd: pl.BlockSpec((tm, wd), lambda i: (i, 0))
    full = lambda r, c: pl.BlockSpec((r, c), lambda i: (0, 0))
    return pl.pallas_call(
        body, name="mla_prep_bwd", grid=(t // tm,),
        in_specs=[row(qw), row(qw), row(GW), row(A_Q_RANK), row(A_KV_RANK), full(1, A_Q_RANK), full(1, A_KV_RANK),
                  full(qw, A_Q_RANK), full(qw, A_KV_RANK), full(GW, A_KV_RANK), row(128), row(128), row(128)],
        out_specs=[row(A_Q_RANK), row(A_KV_RANK), row(128), row(qw), row(qw), full(1, A_Q_RANK), full(1, A_KV_RANK)],
        out_shape=[jax.ShapeDtypeStruct((t, A_Q_RANK), BF16), jax.ShapeDtypeStruct((t, A_KV_RANK), BF16),
                   jax.ShapeDtypeStruct((t, 128), BF16), jax.ShapeDtypeStruct((t, qw), BF16),
                   jax.ShapeDtypeStruct((t, qw), BF16), jax.ShapeDtypeStruct((1, A_Q_RANK), F32),
                   jax.ShapeDtypeStruct((1, A_KV_RANK), F32)],
        compiler_params=_cparams(("arbitrary",)),
    )(dq, dk, dv, cq, ckv, gq, gkv, wuq_t, wk_t, wv_t, cos_t, sin_a, sin_b)


def _nt(a, b):
    return lax.dot_general(a, b, (((1,), (1,)), ((), ())), preferred_element_type=F32)


def _tn(a, b):
    return lax.dot_general(a, b, (((0,), (0,)), ((), ())), preferred_element_type=F32)


def _causal_mask(kind, q0, k0, tq, tk):
    qpos = q0 + lax.broadcasted_iota(jnp.int32, (tq, tk), 0)
    kpos = k0 + lax.broadcasted_iota(jnp.int32, (tq, tk), 1)
    if kind == "mla":
        return lax.shift_right_logical(kpos, 6) <= lax.shift_right_logical(qpos, 6)
    return kpos <= qpos


def _attn_fwd(kind, q, k, v, f, seq, scale, tq=512, tk=512):
    t = v.shape[0]
    nb = t // seq
    nq = seq // tq
    hw = 256 if kind == "mla" else 128
    n_heads = A_HEADS if kind == "mla" else C_HEADS
    use_f = f is not None
    tq, tk = min(tq, seq), min(tk, seq)
    nq = seq // tq
    assert tk == tq

    def body(*refs):
        if use_f:
            q_ref, k_ref, v_ref, f_ref, o_ref, st_ref = refs
        else:
            q_ref, k_ref, v_ref, o_ref, st_ref = refs
        qi = pl.program_id(2)
        q0 = qi * tq
        lane = lax.broadcasted_iota(jnp.int32, (1, 128), 1)
        half = lane >= 64
        qall = q_ref[...]
        if kind == "mla":
            qhs = [qall[:, 0:128], qall[:, 128:256]]
            post = scale * math.log2(math.e)
        else:
            assert math.frexp(scale)[0] == 0.5
            qall = qall * jnp.asarray(scale, BF16)
            qhs = [jnp.where(half, jnp.zeros_like(qall), qall), jnp.where(half, qall, jnp.zeros_like(qall))]
            post = None
        kd = pl.multiple_of(q0, tq)
        diag = _causal_mask(kind, 0, 0, tq, tk)

        def block(j, k0, state, masked):
            m, l, acc = state
            kh = k_ref[pl.ds(k0, tk), j * 128:(j + 1) * 128] if kind == "mla" else k_ref[pl.ds(k0, tk), :]
            s = _nt(qhs[j], kh)
            if post is not None:
                s = s * post
            if use_f:
                s = s - f_ref[0, 0, j:j + 1, pl.ds(k0, tk)]
            if masked:
                s = jnp.where(diag, s, NEG)
            mn = jnp.maximum(m, jnp.max(s, axis=-1, keepdims=True))
            alpha = jnp.exp2(m - mn) if post is not None else jnp.exp(m - mn)
            p = jnp.exp2(s - mn) if post is not None else jnp.exp(s - mn)
            l = alpha * l + jnp.sum(p, axis=-1, keepdims=True)
            acc = alpha * acc + jnp.dot(p.astype(BF16), v_ref[pl.ds(k0, tk), :], preferred_element_type=F32)
            return mn, l, acc

        def run(heads):
            def kstep(kb, carry):
                k0 = pl.multiple_of(kb * tk, tk)
                out = ()
                for n, j in enumerate(heads):
                    out += block(j, k0, carry[3 * n:3 * n + 3], False)
                return out

            init = (jnp.full((tq, 1), NEG, F32), jnp.zeros((tq, 1), F32), jnp.zeros((tq, 128), F32)) * len(heads)
            carry = lax.fori_loop(0, qi, kstep, init)
            o, st = jnp.zeros((tq, 128), F32), jnp.zeros((tq, 128), F32)
            for n, j in enumerate(heads):
                m, l, acc = block(j, kd, carry[3 * n:3 * n + 3], True)
                o = jnp.where(half == bool(j), acc / l, o)
                if post is not None:
                    m = m * math.log(2.0)
                st = jnp.where(lane == j, m + jnp.log(l), st)
            o_ref[...] = o
            st_ref[...] = st

        if n_heads % 2 == 0:
            run((0, 1))
        else:
            last = pl.program_id(1) == n_heads // 2
            pl.when(jnp.logical_not(last))(lambda: run((0, 1)))
            pl.when(last)(lambda: run((0,)))

    in_specs = [pl.BlockSpec((tq, hw), lambda b, p, i: (b * nq + i, p)),
                pl.BlockSpec((seq, hw), lambda b, p, i: (b, p)),
                pl.BlockSpec((seq, 128), lambda b, p, i: (b, p))]
    args = [q, k, v]
    if use_f:
        in_specs.append(pl.BlockSpec((1, 1, 8, seq), lambda b, p, i: (b, p, 0, 0)))
        args.append(f)
    oblk = pl.BlockSpec((tq, 128), lambda b, p, i: (b * nq + i, p))
    return pl.pallas_call(
        body, name="attn_fwd_" + kind, grid=(nb, 3, nq), in_specs=in_specs, out_specs=[oblk, oblk],
        out_shape=[jax.ShapeDtypeStruct((t, GW), F32), jax.ShapeDtypeStruct((t, GW), F32)],
        compiler_params=_cparams(("arbitrary", "arbitrary", "arbitrary")),
    )(*args)


def _attn_bwd(kind, q, k, v, f, o, st, do, seq, scale, tq=512, tk=512, dep=None):
    t = v.shape[0]
    nb = t // seq
    tq, tk = min(tq, seq), min(tk, seq)
    nq = seq // tq
    nk = seq // tk
    hw = 256 if kind == "mla" else 128
    n_heads = A_HEADS if kind == "mla" else C_HEADS
    use_f = f is not None
    assert tq == tk

    def body(*refs):
        if use_f:
            (q_ref, k_ref, v_ref, f_ref, o_ref, st_ref, do_ref, dq_out, dk_out, dv_out, df_ref, dfq_ref,
             dq_ref, dk_ref, dv_ref) = refs
        else:
            q_ref, k_ref, v_ref, o_ref, st_ref, do_ref, dq_ref, dk_ref, dv_ref = refs
        kj = pl.program_id(2)
        lane = lax.broadcasted_iota(jnp.int32, (1, 128), 1)
        half = lane >= 64

        @pl.when(kj == 0)
        def _():
            dq_ref[...] = jnp.zeros_like(dq_ref)
            if use_f:
                dfq_ref[...] = jnp.zeros_like(dfq_ref)

        dk_ref[...] = jnp.zeros_like(dk_ref)
        dv_ref[...] = jnp.zeros_like(dv_ref)
        if use_f:
            df_ref[...] = jnp.zeros_like(df_ref)
        vv = v_ref[...]
        diag = _causal_mask(kind, 0, 0, tq, tk)

        def qstep(qi, masked):
            q0 = pl.multiple_of(qi * tq, tq)
            rows = pl.ds(q0, tq)
            dov = do_ref[rows, :]
            dd = dov * o_ref[rows, :]
            stv = st_ref[rows, :]

            def one_head(j):
                hm = half == bool(j)
                delta = jnp.sum(jnp.where(hm, dd, 0.0), axis=-1, keepdims=True)
                lse = stv[:, j:j + 1]
                if kind == "mla":
                    cols = slice(j * 128, (j + 1) * 128)
                    qh = q_ref[rows, cols]
                    kh = k_ref[:, cols]
                else:
                    cols = slice(0, 128)
                    qa = q_ref[rows, :]
                    qh = jnp.where(hm, qa, jnp.zeros_like(qa))
                    kh = k_ref[...]
                s = _nt(qh, kh) * scale
                if use_f:
                    s = s - f_ref[0, 0, j:j + 1, :]
                if masked:
                    s = jnp.where(diag, s, NEG)
                p = jnp.exp(s - lse)
                doh = jnp.where(hm, dov, 0.0).astype(BF16)
                ds = p * (_nt(doh, vv) - delta)
                dsb = (ds * scale).astype(BF16)
                dv_ref[...] += _tn(p.astype(BF16), doh)
                dk_ref[:, cols] += _tn(dsb, qh)
                dqc = jnp.dot(dsb, kh, preferred_element_type=F32)
                if kind != "mla":
                    dqc = jnp.where(hm, dqc, 0.0)
                dq_ref[rows, cols] += dqc
                if use_f:
                    df_ref[0, 0, j:j + 1, :] += -jnp.sum(ds, axis=0, keepdims=True)
                    dfq_ref[rows, :] += jnp.where(lane == j, jnp.sum(ds, axis=-1, keepdims=True), 0.0)

            def both():
                one_head(0)
                one_head(1)

            if n_heads % 2 == 0:
                both()
            else:
                last = pl.program_id(1) == n_heads // 2
                pl.when(jnp.logical_not(last))(both)
                pl.when(last)(lambda: one_head(0))

        qstep(kj, True)

        def rest(qi, carry):
            qstep(qi, False)
            return carry

        lax.fori_loop(kj + 1, nq, rest, 0)
        if use_f:
            dk_out[...] = dk_ref[...].astype(BF16)
            dv_out[...] = dv_ref[...].astype(BF16)

            @pl.when(kj == nk - 1)
            def _():
                dq_out[...] = dq_ref[...].astype(BF16)

    full_q = lambda wd: pl.BlockSpec((seq, wd), lambda b, p, i: (b, p))
    kblk = lambda wd: pl.BlockSpec((tk, wd), lambda b, p, i: (b * nk + i, p))
    in_specs = [full_q(hw), kblk(hw), kblk(128)]
    args = [q, k, v]
    if use_f:
        in_specs.append(pl.BlockSpec((1, 1, 8, tk), lambda b, p, i: (b, p, 0, i)))
        args.append(f)
    in_specs += [full_q(128), full_q(128), full_q(128)]
    args += [o, st, do]
    out_specs = [full_q(hw), kblk(hw), kblk(128)]
    gdt = BF16 if use_f else F32
    out_shape = [jax.ShapeDtypeStruct((t, 3 * hw), gdt), jax.ShapeDtypeStruct((t, 3 * hw), gdt),
                 jax.ShapeDtypeStruct((t, GW), gdt)]
    scratch = []
    if use_f:
        out_specs += [pl.BlockSpec((1, 1, 8, tk), lambda b, p, i: (b, p, 0, i)), full_q(128)]
        out_shape += [jax.ShapeDtypeStruct((nb, 3, 8, seq), F32), jax.ShapeDtypeStruct((t, GW), F32)]
        scratch = [pltpu.VMEM((seq, hw), F32), pltpu.VMEM((tk, hw), F32), pltpu.VMEM((tk, 128), F32)]
    body, in_specs, args = _after(dep, body, in_specs, args)
    return pl.pallas_call(
        body, name="attn_bwd_" + kind, grid=(nb, 3, nk), in_specs=in_specs, out_specs=out_specs,
        out_shape=out_shape, scratch_shapes=scratch,
        compiler_params=_cparams(("arbitrary", "arbitrary", "arbitrary")),
    )(*args)


BQ = 256
BWIN = BQ + B_LEFT


def _band_geometry():
    r = lax.broadcasted_iota(jnp.int32, (BQ, BWIN), 0)
    j = lax.broadcasted_iota(jnp.int32, (BQ, BWIN), 1)
    rc = lax.shift_right_logical(r, 6)
    jc = lax.shift_right_logical(j, 6)
    allowed = (jc - 8 <= rc) & (rc <= jc)
    return (r + B_LEFT - j) >= REL_CLIP, allowed, j < r


def _band_onehot(transposed, offset=0):
    shape = (BWIN, GW) if transposed else (GW, BWIN)
    kk = lax.broadcasted_iota(jnp.int32, shape, 1 if transposed else 0)
    x = lax.broadcasted_iota(jnp.int32, shape, 0 if transposed else 1) - offset
    x = jnp.where(x < 0, x + BWIN, x)
    return (kk == jnp.clip(B_LEFT - x, -REL_CLIP, REL_CLIP) + REL_CLIP).astype(F32)


def _band_table(rel_bias8):
    def body(b_ref, o_ref):
        hh = pl.program_id(0)
        u8 = jnp.dot(b_ref[...], _band_onehot(False), precision=HI, preferred_element_type=F32)
        rid = lax.broadcasted_iota(jnp.int32, (8, BWIN), 0)
        row = jnp.sum(jnp.where(rid == hh, u8, 0.0), axis=0, keepdims=True)
        far, allowed, _ = _band_geometry()
        tbl = pltpu.roll(jnp.broadcast_to(row, (BQ, BWIN)), 0, 1, stride=1, stride_axis=0)
        tbl = jnp.where(far, row[:, 0:1], tbl)
        o_ref[0] = jnp.where(allowed, tbl, NEG)

    return pl.pallas_call(
        body, name="band_table", grid=(6,),
        in_specs=[pl.BlockSpec((8, GW), lambda h: (0, 0))],
        out_specs=pl.BlockSpec((1, BQ, BWIN), lambda h: (h, 0, 0)),
        out_shape=jax.ShapeDtypeStruct((6, BQ, BWIN), F32),
        compiler_params=_cparams(("arbitrary",)),
    )(rel_bias8)


def _band_table_bwd(gtab):
    def body(g_ref, o_ref):
        gv = g_ref[0]
        _, _, wrapped = _band_geometry()
        gfar = jnp.sum(jnp.sum(jnp.where(wrapped, gv, 0.0), axis=-1, keepdims=True), axis=0, keepdims=True)
        anti = (lax.broadcasted_iota(jnp.int32, (BQ, BQ), 0) + lax.broadcasted_iota(jnp.int32, (BQ, BQ), 1)
                == BQ - 1).astype(F32)
        grev = jnp.dot(anti, jnp.where(wrapped, 0.0, gv), precision=HI, preferred_element_type=F32)
        near = pltpu.roll(grev, 0, 1, stride=1, stride_axis=0)
        y = jnp.broadcast_to(jnp.sum(near, axis=0, keepdims=True), (8, BWIN))
        gb = jnp.dot(y, _band_onehot(True, BQ - 1), precision=HI, preferred_element_type=F32)
        lane = lax.broadcasted_iota(jnp.int32, (8, GW), 1)
        o_ref[0] = gb + jnp.where(lane == 2 * REL_CLIP, gfar, 0.0)

    return pl.pallas_call(
        body, name="band_table_bwd", grid=(B_HEADS,),
        in_specs=[pl.BlockSpec((1, BQ, BWIN), lambda h: (h, 0, 0))],
        out_specs=pl.BlockSpec((1, 8, GW), lambda h: (h, 0, 0)),
        out_shape=jax.ShapeDtypeStruct((B_HEADS, 8, GW), F32),
        compiler_params=_cparams(("arbitrary",)),
    )(gtab)


def _band_fwd(q, k, v, table, seq, scale):
    t = q.shape[0]
    nb = t // seq
    nq = seq // BQ

    def body(q_ref, k_ref, v_ref, tb_ref, o_ref, st_ref, kpad, vpad):
        qi = pl.program_id(2)
        q0 = pl.multiple_of(qi * BQ, BQ)
        lane = lax.broadcasted_iota(jnp.int32, (1, 128), 1)
        half = lane >= 64

        @pl.when(qi == 0)
        def _():
            kpad[0:B_LEFT, :] = jnp.zeros((B_LEFT, 128), BF16)
            vpad[0:B_LEFT, :] = jnp.zeros((B_LEFT, 128), BF16)
            kpad[B_LEFT:, :] = k_ref[...]
            vpad[B_LEFT:, :] = v_ref[...]

        kw = kpad[pl.ds(q0, BWIN), :]
        vw = vpad[pl.ds(q0, BWIN), :]
        inside = lax.broadcasted_iota(jnp.int32, (BQ, BWIN), 1) >= B_LEFT - q0
        assert math.frexp(scale)[0] == 0.5
        qall = q_ref[...] * jnp.asarray(scale, BF16)

        def run(heads):
            o, st = jnp.zeros((BQ, 128), F32), jnp.zeros((BQ, 128), F32)
            for j in heads:
                qh = jnp.where(half == bool(j), qall, jnp.zeros_like(qall))
                s = jnp.where(inside, _nt(qh, kw) + tb_ref[j], NEG)
                m = jnp.max(s, axis=-1, keepdims=True)
                p = jnp.exp(s - m)
                l = jnp.sum(p, axis=-1, keepdims=True)
                o = jnp.where(half == bool(j), jnp.dot(p.astype(BF16), vw, preferred_element_type=F32) / l, o)
                st = jnp.where(lane == j, m + jnp.log(l), st)
            o_ref[...] = o
            st_ref[...] = st

        last = pl.program_id(1) == B_HEADS // 2
        pl.when(jnp.logical_not(last))(lambda: run((0, 1)))
        pl.when(last)(lambda: run((0,)))

    qblk = pl.BlockSpec((BQ, 128), lambda b, p, i: (b * nq + i, p))
    full = pl.BlockSpec((seq, 128), lambda b, p, i: (b, p))
    return pl.pallas_call(
        body, name="band_fwd", grid=(nb, 3, nq),
        in_specs=[qblk, full, full, pl.BlockSpec((2, BQ, BWIN), lambda b, p, i: (p, 0, 0))],
        out_specs=[qblk, qblk],
        out_shape=[jax.ShapeDtypeStruct((t, GW), F32), jax.ShapeDtypeStruct((t, GW), F32)],
        scratch_shapes=[pltpu.VMEM((seq + B_LEFT, 128), BF16), pltpu.VMEM((seq + B_LEFT, 128), BF16)],
        compiler_params=_cparams(("arbitrary", "arbitrary", "arbitrary")),
    )(q, k, v, table)


def _band_bwd(q, k, v, table, o, st, do, seq, scale, dep=None):
    t = q.shape[0]
    nb = t // seq
    nq = seq // BQ

    def body(q_ref, k_ref, v_ref, tb_ref, o_ref, st_ref, do_ref, dq_ref, dk_ref, dv_ref, g_ref,
             kpad, vpad, dkpad, dvpad):
        b = pl.program_id(1)
        qi = pl.program_id(2)
        q0 = pl.multiple_of(qi * BQ, BQ)
        lane = lax.broadcasted_iota(jnp.int32, (1, 128), 1)
        half = lane >= 64

        @pl.when(qi == 0)
        def _():
            kpad[0:B_LEFT, :] = jnp.zeros((B_LEFT, 128), BF16)
            vpad[0:B_LEFT, :] = jnp.zeros((B_LEFT, 128), BF16)
            kpad[B_LEFT:, :] = k_ref[...]
            vpad[B_LEFT:, :] = v_ref[...]
            dkpad[...] = jnp.zeros_like(dkpad)
            dvpad[...] = jnp.zeros_like(dvpad)

        @pl.when((qi == 0) & (b == 0))
        def _():
            g_ref[...] = jnp.zeros_like(g_ref)

        win = pl.ds(q0, BWIN)
        kw = kpad[win, :]
        vw = vpad[win, :]
        inside = lax.broadcasted_iota(jnp.int32, (BQ, BWIN), 1) >= B_LEFT - q0
        qall = q_ref[...]
        dov = do_ref[...]
        dd = dov * o_ref[...]
        stv = st_ref[...]

        def run(heads):
            dq = jnp.zeros((BQ, 128), F32)
            for j in heads:
                hm = half == bool(j)
                qh = jnp.where(hm, qall, jnp.zeros_like(qall))
                delta = jnp.sum(jnp.where(hm, dd, 0.0), axis=-1, keepdims=True)
                s = jnp.where(inside, _nt(qh, kw) * scale + tb_ref[j], NEG)
                p = jnp.exp(s - stv[:, j:j + 1])
                doh = jnp.where(hm, dov, 0.0).astype(BF16)
                ds = p * (_nt(doh, vw) - delta)
                g_ref[j] += ds
                dsb = (ds * scale).astype(BF16)
                dvpad[win, :] += _tn(p.astype(BF16), doh)
                dkpad[win, :] += _tn(dsb, qh)
                dq = dq + jnp.where(hm, jnp.dot(dsb, kw, preferred_element_type=F32), 0.0)
            dq_ref[...] = dq.astype(BF16)

        last = pl.program_id(0) == B_HEADS // 2
        pl.when(jnp.logical_not(last))(lambda: run((0, 1)))
        pl.when(last)(lambda: run((0,)))

        @pl.when(qi == nq - 1)
        def _():
            dk_ref[...] = dkpad[B_LEFT:, :].astype(BF16)
            dv_ref[...] = dvpad[B_LEFT:, :].astype(BF16)

    qblk = pl.BlockSpec((BQ, 128), lambda p, b, i: (b * nq + i, p))
    full = pl.BlockSpec((seq, 128), lambda p, b, i: (b, p))
    tblk = pl.BlockSpec((2, BQ, BWIN), lambda p, b, i: (p, 0, 0))
    body, in_specs, args = _after(dep, body, [qblk, full, full, tblk, qblk, qblk, qblk], [q, k, v, table, o, st, do])
    return pl.pallas_call(
        body, name="band_bwd", grid=(3, nb, nq),
        in_specs=in_specs,
        out_specs=[qblk, full, full, tblk],
        out_shape=[jax.ShapeDtypeStruct((t, GW), BF16), jax.ShapeDtypeStruct((t, GW), BF16),
                   jax.ShapeDtypeStruct((t, GW), BF16), jax.ShapeDtypeStruct((6, BQ, BWIN), F32)],
        scratch_shapes=[pltpu.VMEM((seq + B_LEFT, 128), BF16), pltpu.VMEM((seq + B_LEFT, 128), BF16),
                        pltpu.VMEM((seq + B_LEFT, 128), F32), pltpu.VMEM((seq + B_LEFT, 128), F32)],
        compiler_params=_cparams(("arbitrary", "arbitrary", "arbitrary")),
    )(*args)


def _fox_prep(cf, fb, seq):
    nb = cf.shape[0] // seq
    nblk = seq // 128

    def body(cf_ref, fb_ref, f_ref):
        x = cf_ref[...] + fb_ref[...]
        lf = jnp.minimum(x, 0.0) - jnp.log1p(jnp.exp(-jnp.abs(x)))
        rows = lf.T[0:8, :]
        upper = (lax.broadcasted_iota(jnp.int32, (128, 128), 0)
                 <= lax.broadcasted_iota(jnp.int32, (128, 128), 1)).astype(F32)
        carry = jnp.zeros((8, 1), F32)
        for blk in range(nblk):
            sl = slice(blk * 128, (blk + 1) * 128)
            cs = jnp.dot(rows[:, sl], upper, precision=HI, preferred_element_type=F32) + carry
            carry = cs[:, 127:128]
            f_ref[0, 0, :, sl] = cs
            f_ref[0, 1, :, sl] = pltpu.roll(cs, 6, 0)
            f_ref[0, 2, :, sl] = pltpu.roll(cs, 4, 0)

    return pl.pallas_call(
        body, name="fox_prep", grid=(nb,),
        in_specs=[pl.BlockSpec((seq, 128), lambda b: (b, 0)), pl.BlockSpec((1, 128), lambda b: (0, 0))],
        out_specs=pl.BlockSpec((1, 3, 8, seq), lambda b: (b, 0, 0, 0)),
        out_shape=jax.ShapeDtypeStruct((nb, 3, 8, seq), F32),
        compiler_params=_cparams(("arbitrary",)),
    )(cf, fb)


def _fox_prep_bwd(df, dfq, cf, fb, seq):
    nb = cf.shape[0] // seq
    nblk = seq // 128

    def body(df_ref, dfq_ref, cf_ref, fb_ref, dcf_ref, dfb_ref, wide):
        b = pl.program_id(0)
        row = lax.broadcasted_iota(jnp.int32, (8, seq), 0)
        dfh = None
        for p in range(3):
            both = df_ref[0, p] + dfq_ref[:, p * 128:(p + 1) * 128].T[0:8, :]
            both = jnp.where(row < 2, both, 0.0)
            if p:
                both = pltpu.roll(both, 2 * p, 0)
            dfh = both if dfh is None else dfh + both
        lower = (lax.broadcasted_iota(jnp.int32, (128, 128), 0)
                 >= lax.broadcasted_iota(jnp.int32, (128, 128), 1)).astype(F32)
        wide[...] = jnp.zeros_like(wide)
        carry = jnp.zeros((8, 1), F32)
        for blk in reversed(range(nblk)):
            sl = slice(blk * 128, (blk + 1) * 128)
            rc = jnp.dot(dfh[:, sl], lower, precision=HI, preferred_element_type=F32) + carry
            carry = rc[:, 0:1]
            wide[0:8, sl] = rc
        dl = wide[...].T
        x = cf_ref[...] + fb_ref[...]
        dcf = dl * (1.0 / (1.0 + jnp.exp(x)))
        dcf_ref[...] = dcf.astype(BF16)
        part = jnp.sum(dcf, axis=0, keepdims=True)

        @pl.when(b == 0)
        def _():
            dfb_ref[...] = part

        @pl.when(b != 0)
        def _():
            dfb_ref[...] += part

    return pl.pallas_call(
        body, name="fox_prep_bwd", grid=(nb,),
        in_specs=[pl.BlockSpec((1, 3, 8, seq), lambda b: (b, 0, 0, 0)), pl.BlockSpec((seq, GW), lambda b: (b, 0)),
                  pl.BlockSpec((seq, 128), lambda b: (b, 0)), pl.BlockSpec((1, 128), lambda b: (0, 0))],
        out_specs=[pl.BlockSpec((seq, 128), lambda b: (b, 0)), pl.BlockSpec((1, 128), lambda b: (0, 0))],
        out_shape=[jax.ShapeDtypeStruct(cf.shape, BF16), jax.ShapeDtypeStruct((1, 128), F32)],
        scratch_shapes=[pltpu.VMEM((128, seq), F32)],
        compiler_params=_cparams(("arbitrary",)),
    )(df, dfq, cf, fb)


def _gate_out(oa, ob, oc, gates, w, x, gate, seq, tm=ROW_TILE):
    t = x.shape[0]
    tm = min(tm, seq)
    tps = seq // tm

    def body(oa_ref, ob_ref, oc_ref, g_ref, w_ref, x_ref, gt_ref, xo_ref, y_ref, u_ref):
        for n, o_ref in enumerate((oa_ref, ob_ref, oc_ref)):
            sl = slice(n * GW, (n + 1) * GW)
            gv = g_ref[:, sl].astype(F32)
            u_ref[:, sl] = (o_ref[...] * (gv * _sigmoid(gv))).astype(BF16)
        y = jnp.dot(u_ref[...], w_ref[...], preferred_element_type=F32)
        y_ref[...] = y.astype(BF16)
        xo_ref[...] = x_ref[...] + gt_ref[0] * y

    row = lambda wd: pl.BlockSpec((tm, wd), lambda i: (i, 0))
    return pl.pallas_call(
        body, name="gate_out", grid=(t // tm,),
        in_specs=[row(GW), row(GW), row(GW), row(U_PAD), pl.BlockSpec((U_PAD, D_MODEL), lambda i: (0, 0)),
                  row(D_MODEL), pl.BlockSpec((1, 1, D_MODEL), lambda i: (i // tps, 0, 0))],
        out_specs=[row(D_MODEL), row(D_MODEL), row(U_PAD)],
        out_shape=[jax.ShapeDtypeStruct((t, D_MODEL), F32), jax.ShapeDtypeStruct((t, D_MODEL), BF16),
                   jax.ShapeDtypeStruct((t, U_PAD), BF16)],
        compiler_params=_cparams(("arbitrary",)),
    )(oa, ob, oc, gates, w, x, gate)


def _gate_out_bwd(dxo, y, gate, oa, ob, oc, gates, w_t, seq, tm=ROW_TILE, dep=None):
    t = dxo.shape[0]
    tm = min(tm, seq)
    tps = seq // tm
    nb = t // seq

    def body(dxo_ref, y_ref, gt_ref, oa_ref, ob_ref, oc_ref, g_ref, wt_ref,
             dy_ref, doa_ref, dob_ref, doc_ref, dg_ref, dgt_ref):
        i = pl.program_id(0)
        dxo_v = dxo_ref[...]
        dgt = jnp.sum(dxo_v * y_ref[...].astype(F32), axis=0, keepdims=True)
        dyb = (dxo_v * gt_ref[0]).astype(BF16)
        dy_ref[...] = dyb
        du = _nt(dyb, wt_ref[...])
        for n, (o_ref, do_ref) in enumerate(((oa_ref, doa_ref), (ob_ref, dob_ref), (oc_ref, doc_ref))):
            sl = slice(n * GW, (n + 1) * GW)
            gv = g_ref[:, sl].astype(F32)
            sg = _sigmoid(gv)
            dun = du[:, sl]
            do_ref[...] = dun * (gv * sg)
            dg_ref[:, sl] = (dun * o_ref[...] * (sg * (1.0 + gv * (1.0 - sg)))).astype(BF16)

        @pl.when(i % tps == 0)
        def _():
            dgt_ref[0] = dgt

        @pl.when(i % tps != 0)
        def _():
            dgt_ref[0] += dgt

    row = lambda wd: pl.BlockSpec((tm, wd), lambda i: (i, 0))
    per_b = pl.BlockSpec((1, 1, D_MODEL), lambda i: (i // tps, 0, 0))
    in_specs = [row(D_MODEL), row(D_MODEL), per_b, row(GW), row(GW), row(GW), row(U_PAD),
                pl.BlockSpec((U_PAD, D_MODEL), lambda i: (0, 0))]
    body, in_specs, args = _after(dep, body, in_specs, [dxo, y, gate, oa, ob, oc, gates, w_t])
    return pl.pallas_call(
        body, name="gate_out_bwd", grid=(t // tm,), in_specs=in_specs,
        out_specs=[row(D_MODEL), row(GW), row(GW), row(GW), row(U_PAD), per_b],
        out_shape=[jax.ShapeDtypeStruct((t, D_MODEL), BF16), jax.ShapeDtypeStruct((t, GW), F32),
                   jax.ShapeDtypeStruct((t, GW), F32), jax.ShapeDtypeStruct((t, GW), F32),
                   jax.ShapeDtypeStruct((t, U_PAD), BF16), jax.ShapeDtypeStruct((nb, 1, D_MODEL), F32)],
        compiler_params=_cparams(("arbitrary",)),
    )(*args)


def _final_loss(x, target, g, tm=ROW_TILE):
    t = x.shape[0]
    tm = min(tm, t)

    def body(x_ref, t_ref, g_ref, dx_ref, loss_ref, dg_ref):
        i = pl.program_id(0)
        xv = x_ref[...]
        rstd = lax.rsqrt(jnp.mean(xv * xv, axis=-1, keepdims=True) + EPS)
        xn = xv * rstd
        gv = g_ref[...]
        err = xn * gv - t_ref[...]
        dy = err * (1.0 / D_MODEL)
        dxn = dy * gv
        dx_ref[...] = rstd * (dxn - xn * jnp.mean(dxn * xn, axis=-1, keepdims=True))
        lp = jnp.sum(err * err, axis=0, keepdims=True) * (0.5 / D_MODEL)
        dgp = jnp.sum(dy * xn, axis=0, keepdims=True)

        @pl.when(i == 0)
        def _():
            loss_ref[...] = lp
            dg_ref[...] = dgp

        @pl.when(i != 0)
        def _():
            loss_ref[...] += lp
            dg_ref[...] += dgp

    row = pl.BlockSpec((tm, D_MODEL), lambda i: (i, 0))
    vec = pl.BlockSpec((1, D_MODEL), lambda i: (0, 0))
    return pl.pallas_call(
        body, name="final_loss", grid=(t // tm,),
        in_specs=[row, row, vec], out_specs=[row, vec, vec],
        out_shape=[jax.ShapeDtypeStruct((t, D_MODEL), F32), jax.ShapeDtypeStruct((1, D_MODEL), F32),
                   jax.ShapeDtypeStruct((1, D_MODEL), F32)],
        compiler_params=_cparams(("arbitrary",)),
    )(x, target, g)


def _adamw(w, gslots, m, v, name, tr=None):
    nl, r, c = w.shape
    ns = gslots.shape[0]
    tr = r if tr is None else tr

    def body(w_ref, g_ref, m_ref, v_ref, go_ref, d_ref, mo_ref, vo_ref):
        g = g_ref[0].astype(F32)
        for j in range(1, ns):
            g = g + g_ref[j].astype(F32)
        mn = ADAM_B1 * m_ref[...] + (1.0 - ADAM_B1) * g
        vn = ADAM_B2 * v_ref[...] + (1.0 - ADAM_B2) * jnp.square(g)
        m_hat = mn / (1.0 - ADAM_B1 ** ADAM_STEP)
        v_hat = vn / (1.0 - ADAM_B2 ** ADAM_STEP)
        go_ref[...] = g
        d_ref[...] = -ADAM_LR * (m_hat / (jnp.sqrt(v_hat) + ADAM_EPS) + ADAM_WD * w_ref[...])
        mo_ref[...] = mn
        vo_ref[...] = vn

    blk = pl.BlockSpec((1, tr, c), lambda l, i: (l, i, 0))
    return pl.pallas_call(
        body, name=name, grid=(nl, r // tr),
        in_specs=[blk, pl.BlockSpec((ns, 1, tr, c), lambda l, i: (0, l, i, 0)), blk, blk],
        out_specs=[blk] * 4, out_shape=[jax.ShapeDtypeStruct((nl, r, c), F32)] * 4,
        compiler_params=_cparams(("arbitrary", "arbitrary")),
    )(w, gslots, m, v)


def _rope_tables(positions):
    inv = ROPE_THETA ** (-jnp.arange(0, A_ROPE, 2, dtype=F32) / A_ROPE)
    ang = positions.astype(F32)[:, None] * inv
    cos, sin = jnp.cos(ang), jnp.sin(ang)
    t = positions.shape[0]
    one = jnp.ones((t, 64), F32)
    zero16 = jnp.zeros((t, 16), F32)
    cos_t = jnp.concatenate([one, cos, cos, jnp.ones((t, 32), F32)], axis=1)
    sin_a = jnp.concatenate([jnp.zeros((t, 64), F32), -sin, zero16, jnp.zeros((t, 32), F32)], axis=1)
    sin_b = jnp.concatenate([jnp.zeros((t, 64), F32), zero16, sin, jnp.zeros((t, 32), F32)], axis=1)
    return cos_t, sin_a, sin_b


def _pad_heads(w, real, padded, nheads, axis):
    shp = w.shape[:axis] + (nheads, real) + w.shape[axis + 1:]
    w = w.reshape(shp)
    pad = [(0, 0)] * w.ndim
    pad[axis + 1] = (0, padded - real)
    w = jnp.pad(w, pad)
    return w.reshape(w.shape[:axis] + (nheads * padded,) + w.shape[axis + 2:])


def kernel(x, c, positions, w_ada, b_ada, norm_g, w_in, a_q_norm_g, a_w_uq, a_kv_norm_g, a_w_ukv, b_rel_bias, c_forget_b, w_out, final_g, loss_target, m_w_ada, m_b_ada, m_norm_g, m_w_in, m_a_q_norm_g, m_a_w_uq, m_a_kv_norm_g, m_a_w_ukv, m_b_rel_bias, m_c_forget_b, m_w_out, m_final_g, v_w_ada, v_b_ada, v_norm_g, v_w_in, v_a_q_norm_g, v_a_w_uq, v_a_kv_norm_g, v_a_w_ukv, v_b_rel_bias, v_c_forget_b, v_w_out, v_final_g):
    nb, seq, _ = x.shape
    t = nb * seq
    me = 4 * lax.axis_index("x") + 2 * lax.axis_index("y") + lax.axis_index("c")
    x2 = x.reshape(t, D_MODEL)
    tgt = loss_target.reshape(t, D_MODEL)
    cos_t, sin_a, sin_b = _rope_tables(positions.reshape(t))

    def shards(l):
        return [_pad_runs(w_in[l].astype(BF16), IN_RUNS, N_PAD, 1), w_out[l].astype(BF16),
                a_w_uq[l].astype(BF16), a_w_ukv[l].astype(BF16)]

    def prepare(gi, go, gq, gkv):
        return dict(w_in=gi.reshape(D_MODEL, N_PAD), **prepare_rest(go, gq, gkv))

    def prepare_rest(go, gq, gkv):
        wo = _pad_runs(go.reshape(D_MODEL, D_MODEL), OUT_RUNS, U_PAD, 0)
        wq = jnp.transpose(gq, (1, 0, 2)).reshape(A_Q_RANK, A_HEADS * (A_NOPE + A_ROPE))
        wq = _pad_heads(wq, A_NOPE + A_ROPE, HEAD_PAD, A_HEADS, 1)
        wkv = jnp.transpose(gkv, (1, 0, 2)).reshape(A_KV_RANK, A_HEADS, 2 * A_NOPE)
        wk = jnp.pad(wkv[:, :, :A_NOPE], ((0, 0), (0, 0), (0, HEAD_PAD - A_NOPE))).reshape(A_KV_RANK, A_HEADS * HEAD_PAD)
        wv = wkv[:, :, A_NOPE:].reshape(A_KV_RANK, GW)
        return dict(w_out=wo, wuq=wq, wuq_t=wq.T, wk=wk, wk_t=wk.T, wv=wv, wv_t=wv.T)

    shards0 = shards(0)
    w_in0_g, c_g = _gather([shards0[0], c], "gather_w_in0")
    c_all = c_g.reshape(N_DEV * nb, D_MODEL)
    weights = [dict(w_in=w_in0_g.reshape(D_MODEL, N_PAD)), None]

    c_act, mod_cols = _ada_fwd(c_all, w_ada)
    (mod_g,) = _gather([mod_cols], "gather_mod")
    rest0, rest0_token = _split_start("gather", shards0[1:], "gather_rest0_start", after=mod_g)
    mod_all = jnp.transpose(mod_g, (1, 2, 0, 3)).reshape(DEPTH, N_DEV * nb, 3 * D_MODEL)
    mod = lax.dynamic_slice_in_dim(mod_all, me * nb, nb, axis=1) + b_ada[:, None, :]

    fb_pad = jnp.pad(c_forget_b, ((0, 0), (0, 128 - C_HEADS)))
    a_scale = (A_NOPE + A_ROPE) ** -0.5
    h_scale = CHUNK ** -0.5

    saved = []
    xl = x2
    for l in range(DEPTH):
        if l == 1:
            weights[1] = prepare(*_split_wait(gather1, xl, "gather_weights1_wait")[1])
        w = weights[l]
        shift, scale, gate = mod[l, :, :D_MODEL], mod[l, :, D_MODEL:2 * D_MODEL], mod[l, :, 2 * D_MODEL:]
        ss = jnp.stack([shift, 1.0 + scale], axis=1)
        gate3 = gate[:, None, :]
        h, cq, ckv, kpe, gates, bq, bk, bv, cq2, ck, cv, cf = _ln_in(
            xl, ss, norm_g[l:l + 1], w["w_in"], seq, dep=rest0_token if l == 0 else None)
        gather1_token = None
        if l == 0:
            w.update(prepare_rest(*_split_wait(rest0, h, "gather_rest0_wait")[1]))
            gather1, gather1_token = _split_start("gather", shards(1), "gather_weights1_start", after=w["w_out"])
        q, k, v, cqn, ckvn = _mla_prep(cq, ckv, kpe, a_q_norm_g[l:l + 1], a_kv_norm_g[l:l + 1],
                                       w["wuq"], w["wk"], w["wv"], cos_t, sin_a, sin_b, dep=gather1_token)
        oa, sta = _attn_fwd("mla", q, k, v, None, seq, a_scale)
        table = _band_table(jnp.pad(b_rel_bias[l], ((0, 8 - B_HEADS), (0, GW - N_REL))))
        ob, stb = _band_fwd(bq, bk, bv, table, seq, h_scale)
        fcum = _fox_prep(cf, fb_pad[l:l + 1], seq)
        oc, stc = _attn_fwd("fox", cq2, ck, cv, fcum, seq, h_scale)
        xn, y, u = _gate_out(oa, ob, oc, gates, w["w_out"], xl, gate3, seq)
        saved.append(dict(x=xl, ss=ss, gate3=gate3, h=h, cq=cq, ckv=ckv, gates=gates, bq=bq, bk=bk, bv=bv,
                          cq2=cq2, ck=ck, cv=cv, cf=cf, q=q, k=k, v=v, cqn=cqn, ckvn=ckvn, oa=oa, sta=sta,
                          table=table, ob=ob, stb=stb, fcum=fcum, oc=oc, stc=stc, y=y, u=u))
        xl = xn

    dx, loss_lanes, g_final = _final_loss(xl, tgt, final_g[None, :])
    loss = lax.psum(jnp.sum(loss_lanes), AXES)

    rows = D_MODEL // N_DEV
    core = lax.axis_index("c").astype(jnp.int32).reshape(1)
    n_seg_a = 4
    dmods, smalls, parts = [None] * DEPTH, [None] * DEPTH, [None] * DEPTH
    pair1 = chips1 = pair1_token = chips1_token = None
    for l in reversed(range(DEPTH)):
        s, w = saved[l], weights[l]
        dy, doa, dob, doc, dgates, dgate = _gate_out_bwd(dx, s["y"], s["gate3"], s["oa"], s["ob"], s["oc"],
                                                         s["gates"], w["w_out"], seq, dep=pair1_token)
        g_out = _unpad_runs(_matmul_tn(s["u"], dy, "dw_out"), OUT_RUNS, 0)
        if l == 0:
            own, from_sib = _split_wait(pair1, g_out, "grads1_pair_wait")
            chips1, chips1_token = _split_start("chips", _pair_add(core, own, from_sib, "grads1_add"), "grads1_chips_start")
        dq, dk, dv = _attn_bwd("mla", s["q"], s["k"], s["v"], None, s["oa"], s["sta"], doa, seq, a_scale,
                               dep=chips1_token)
        dbq, dbk, dbv, gtab = _band_bwd(s["bq"], s["bk"], s["bv"], s["table"], s["ob"], s["stb"], dob, seq, h_scale,
                                        dep=chips1_token)
        g_rel = _band_table_bwd(gtab)[:, 0, :N_REL]
        dcq2, dck, dcv, dfc, dfq = _attn_bwd("fox", s["cq2"], s["ck"], s["cv"], s["fcum"], s["oc"], s["stc"], doc,
                                             seq, h_scale, dep=chips1_token)
        dcf, dfb = _fox_prep_bwd(dfc, dfq, s["cf"], fb_pad[l:l + 1], seq)
        dcq, dckv, dkpe, dqlin, dklin, dgq, dgkv = _mla_prep_bwd(
            dq, dk, dv, s["cq"], s["ckv"], a_q_norm_g[l:l + 1], a_kv_norm_g[l:l + 1],
            w["wuq_t"], w["wk_t"], w["wv_t"], cos_t, sin_a, sin_b)
        gq_pad = _matmul_tn(s["cqn"], dqlin, "dw_uq")
        g_uq = gq_pad.reshape(A_Q_RANK, A_HEADS, HEAD_PAD)[:, :, :A_NOPE + A_ROPE].reshape(A_Q_RANK, -1)
        gkv_pad = _matmul_tn(s["ckvn"], [dklin, dv], "dw_ukv")
        gk_pad = gkv_pad[:, :A_HEADS * HEAD_PAD].reshape(A_KV_RANK, A_HEADS, HEAD_PAD)[:, :, :A_NOPE]
        gv_pad = gkv_pad[:, A_HEADS * HEAD_PAD:].reshape(A_KV_RANK, A_HEADS, A_NOPE)
        g_ukv = jnp.concatenate([gk_pad, gv_pad], axis=2).reshape(A_KV_RANK, -1)
        dz = [dcq, dckv, dkpe, dgates, dbq, dbk, dbv, dcq2, dck, dcv, dcf]
        g_in_a = _matmul_tn(s["h"], dz[:n_seg_a], "dw_in_a")
        first = [g_in_a.reshape(N_DEV, rows, -1), g_out.reshape(N_DEV, rows, D_MODEL),
                 g_uq.reshape(A_Q_RANK, N_DEV, -1).transpose(1, 0, 2), g_ukv.reshape(A_KV_RANK, N_DEV, -1).transpose(1, 0, 2)]
        if l == 1:
            g_in_b = _matmul_tn(s["h"], dz[n_seg_a:], "dw_in_b")
            pair1, pair1_token = _split_start("pair", first + [g_in_b.reshape(N_DEV, rows, -1)], "grads1_pair_start")
            tail_token = None
        else:
            pair0a, pair0a_token = _split_start("pair", first, "grads0a_pair_start")
            g_in_b = _matmul_tn(s["h"], dz[n_seg_a:], "dw_in_b", dep=pair0a_token)
            own, from_sib = _split_wait(pair0a, g_in_b, "grads0a_pair_wait")
            sums0a = _pair_add(core, own, from_sib, "grads0a_add")
            pair0b, pair0b_token = _split_start("pair", [g_in_b.reshape(N_DEV, rows, -1)], "grads0b_pair_start",
                                                after=sums0a[0])
            chips0a, tail_token = _split_start("chips", sums0a, "grads0a_chips_start", after=pair0b_token)
        dx, dss, dg_norm = _ln_in_bwd(dz, w["w_in"], s["x"], s["ss"], norm_g[l:l + 1], dx, seq, dep=tail_token)
        dmods[l] = jnp.concatenate([dss[:, 0, :], dss[:, 1, :], dgate[:, 0, :]], axis=1)
        smalls[l] = [dg_norm.reshape(-1), dgq.reshape(-1), dgkv.reshape(-1), g_rel.reshape(-1),
                     dfb[0, :C_HEADS]]
    grad_x = dx.reshape(nb, seq, D_MODEL)
    parts[1] = _split_wait(chips1, dx, "grads1_chips_wait")[1]
    parts0a = _split_wait(chips0a, dx, "grads0a_chips_wait")[1]
    own, from_sib = _split_wait(pair0b, dx, "grads0b_pair_wait")

    small = jnp.concatenate([p for l in range(DEPTH) for p in smalls[l]] + [g_final.reshape(-1)])
    n_small = small.shape[0]
    small_rows = -(-n_small // 1024) * 8
    small = jnp.pad(small, (0, small_rows * 128 - n_small)).reshape(small_rows, 128)
    dmod_local = jnp.stack(dmods)
    dmod_g, small_g = _gather([dmod_local, small], "gather_small", dep=parts0a[0])
    chips0, chips0_token = _split_start("chips", _pair_add(core, own, from_sib, "grads0b_add"), "grads0b_chips_start",
                                        after=small_g)
    dmod_all = jnp.transpose(dmod_g, (1, 0, 2, 3)).reshape(DEPTH, N_DEV * nb, 3 * D_MODEL)
    cols = 3 * D_MODEL // N_DEV
    dmod_mine = lax.dynamic_slice_in_dim(dmod_all, me * cols, cols, axis=2)
    g_w_ada, g_b_ada = _ada_bwd(c_act, dmod_all, dmod_mine, chips0_token)
    small_sum = _sum_slots(small_g, "sum_small").reshape(-1)

    def split_small():
        out, pos = [], 0
        sizes = [D_MODEL, A_Q_RANK, A_KV_RANK, B_HEADS * N_REL, C_HEADS]
        per_layer = []
        for l in range(DEPTH):
            parts = []
            for sz in sizes:
                parts.append(small_sum[pos:pos + sz])
                pos += sz
            per_layer.append(parts)
        for j in range(len(sizes)):
            out.append(jnp.stack([per_layer[l][j] for l in range(DEPTH)]))
        out.append(small_sum[pos:pos + D_MODEL])
        return out

    g_norm, g_qn, g_kvn, g_relb, g_fb, g_fin = split_small()

    def adam(w, g, m, v, name, tr=None):
        shp = w.shape
        w3 = w.reshape((1,) * (3 - w.ndim) + shp)
        outs = _adamw(w3, g.reshape((-1,) + w3.shape), m.reshape(w3.shape), v.reshape(w3.shape), name, tr)
        return [o.reshape(shp) for o in outs]

    res = {
        "w_ada": adam(w_ada, g_w_ada, m_w_ada, v_w_ada, "adam_w_ada", 256),
        "b_ada": adam(b_ada, g_b_ada, m_b_ada, v_b_ada, "adam_b_ada"),
        "norm_g": adam(norm_g, g_norm, m_norm_g, v_norm_g, "adam_norm_g"),
        "a_q_norm_g": adam(a_q_norm_g, g_qn, m_a_q_norm_g, v_a_q_norm_g, "adam_q_norm"),
        "a_kv_norm_g": adam(a_kv_norm_g, g_kvn, m_a_kv_norm_g, v_a_kv_norm_g, "adam_kv_norm"),
        "b_rel_bias": adam(b_rel_bias, g_relb.reshape(b_rel_bias.shape), m_b_rel_bias, v_b_rel_bias, "adam_rel_bias"),
        "c_forget_b": adam(c_forget_b, g_fb, m_c_forget_b, v_c_forget_b, "adam_forget_b"),
        "final_g": adam(final_g, g_fin, m_final_g, v_final_g, "adam_final_g"),
    }
    parts[0] = list(parts0a) + list(_split_wait(chips0, res["w_ada"][1], "grads0b_chips_wait")[1])
    p_in = jnp.stack([_unpad_runs(jnp.concatenate([parts[l][0], parts[l][4]], axis=2), IN_RUNS, 2)
                      for l in range(DEPTH)], axis=1)
    p_out, p_uq, p_ukv = (jnp.stack([parts[l][j] for l in range(DEPTH)], axis=1) for j in (1, 2, 3))
    res.update({
        "w_in": adam(w_in, p_in, m_w_in, v_w_in, "adam_w_in", 64),
        "a_w_uq": adam(a_w_uq, p_uq, m_a_w_uq, v_a_w_uq, "adam_w_uq"),
        "a_w_ukv": adam(a_w_ukv, p_ukv, m_a_w_ukv, v_a_w_ukv, "adam_w_ukv"),
        "w_out": adam(w_out, p_out, m_w_out, v_w_out, "adam_w_out", 64),
    })
    names = ["w_ada", "b_ada", "norm_g", "w_in", "a_q_norm_g", "a_w_uq", "a_kv_norm_g", "a_w_ukv", "b_rel_bias",
             "c_forget_b", "w_out", "final_g"]
    outs = [loss, grad_x]
    for j in range(4):
        outs += [res[n][j] for n in names]
    return tuple(outs)
```

```python
import math

import jax
import jax.numpy as jnp
from jax import lax
from jax.experimental import pallas as pl
from jax.experimental.pallas import tpu as pltpu

F32 = jnp.float32
BF16 = jnp.bfloat16
HI = lax.Precision.HIGHEST

N_DEV = 8
AXES = ("x", "y", "c")
D_MODEL = 1024
DEPTH = 2
CHUNK = 64
EPS = 1e-6
NEG = -1e30
A_HEADS = 6
A_NOPE = 64
A_ROPE = 32
A_Q_RANK = 384
A_KV_RANK = 256
ROPE_THETA = 10000.0
B_HEADS = 5
B_LEFT = 512
REL_CLIP = 128
N_REL = 2 * REL_CLIP + 1
C_HEADS = 5
HEAD_PAD = 128
GW = 384
N_IN = 3621
ADAM_LR = 0.001
ADAM_B1 = 0.9
ADAM_B2 = 0.999
ADAM_EPS = 1e-08
ADAM_WD = 0.01
ADAM_STEP = 10
VMEM_LIMIT = 56 * 1024 * 1024
ROW_TILE = 512

Z_SEGS = (
    ("cq", 0, 384, F32), ("ckv", 384, 256, F32), ("kpe", 640, 128, F32), ("gates", 768, 1152, BF16),
    ("bq", 1920, 384, BF16), ("bk", 2304, 384, BF16), ("bv", 2688, 384, BF16),
    ("cq2", 3072, 384, BF16), ("ck", 3456, 384, BF16), ("cv", 3840, 384, BF16), ("cf", 4224, 128, F32),
)
N_PAD = 4352
IN_RUNS = (
    (0, 384, 0), (384, 256, 384), (640 + 64, 32, 640),
    (768, 384, 672), (768 + 384, 320, 2016), (768 + 768, 320, 3301),
    (1920, 320, 1056), (2304, 320, 1376), (2688, 320, 1696),
    (3072, 320, 2336), (3456, 320, 2656), (3840, 320, 2976), (4224, 5, 3296),
)
OUT_RUNS = ((0, 384, 0), (384, 320, 384), (768, 320, 704))
U_PAD = 1152


def _cparams(sem=None, vmem=VMEM_LIMIT):
    return pltpu.CompilerParams(dimension_semantics=sem, vmem_limit_bytes=vmem)


def _after(dep, body, in_specs, args):
    if dep is None:
        return body, in_specs, args
    n = len(args)

    def ordered(*refs):
        return body(*refs[:n], *refs[n + 1:])

    return ordered, list(in_specs) + [pl.BlockSpec((8, 128), lambda *_: (0, 0))], list(args) + [dep]


def _pad_runs(w, runs, total, axis):
    order = sorted(runs)
    parts, pos = [], 0
    for off, wd, src in order:
        if off > pos:
            shp = list(w.shape)
            shp[axis] = off - pos
            parts.append(jnp.zeros(shp, w.dtype))
        parts.append(lax.slice_in_dim(w, src, src + wd, axis=axis))
        pos = off + wd
    if pos < total:
        shp = list(w.shape)
        shp[axis] = total - pos
        parts.append(jnp.zeros(shp, w.dtype))
    return jnp.concatenate(parts, axis=axis)


def _unpad_runs(w, runs, axis):
    order = sorted(runs, key=lambda r: r[2])
    return jnp.concatenate([lax.slice_in_dim(w, off, off + wd, axis=axis) for off, wd, _ in order], axis=axis)


def _sigmoid(x):
    return 1.0 / (1.0 + jnp.exp(-x))


N_CHIP = 4
ANY_SPEC = pl.BlockSpec(memory_space=pl.ANY)
MESH_ID = pl.DeviceIdType.MESH


def _gather(arrs, name, dep=None):
    n = len(arrs)
    nin = n + (dep is not None)

    def body(*refs):
        ins, outs = refs[:n], refs[nin:nin + n]
        send_sems, recv_sems, local_sems = refs[nin + n:]
        x, y, c = lax.axis_index("x"), lax.axis_index("y"), lax.axis_index("c")
        me, sib = (x, y, c), (x, y, 1 - c)
        chips = [(1 - x, y), (x, 1 - y), (1 - x, 1 - y)]

        def slot(px, py, pc):
            return 4 * px + 2 * py + pc

        def copy(a, k, block, to, src=None):
            dst = outs[a].at[slot(*block)]
            return pltpu.make_async_remote_copy(
                src_ref=dst if src is None else src, dst_ref=dst, send_sem=send_sems.at[a, k],
                recv_sem=recv_sems.at[a, k], device_id=to, device_id_type=MESH_ID)

        local = [pltpu.make_async_copy(ins[a], outs[a].at[slot(*me)], local_sems.at[a]) for a in range(n)]
        first = []
        for a in range(n):
            first.append(copy(a, 0, me, sib, src=ins[a]))
            first += [copy(a, 1 + j, me, (*chip, c), src=ins[a]) for j, chip in enumerate(chips)]
        for cp in local + first:
            cp.start()
        passed = []
        for j, chip in enumerate(chips):
            for a in range(n):
                copy(a, 1 + j, (*chip, c), me).wait_recv()
                fwd = copy(a, 4 + j, (*chip, c), sib)
                fwd.start()
                passed.append(fwd)
        for a in range(n):
            copy(a, 0, sib, me).wait_recv()
            for j, chip in enumerate(chips):
                copy(a, 4 + j, (*chip, 1 - c), me).wait_recv()
        for cp in first + passed:
            cp.wait_send()
        for cp in local:
            cp.wait()

    return pl.pallas_call(
        body, name=name, out_shape=[jax.ShapeDtypeStruct((N_DEV,) + a.shape, a.dtype) for a in arrs],
        in_specs=[ANY_SPEC] * nin, out_specs=[ANY_SPEC] * n,
        scratch_shapes=[pltpu.SemaphoreType.DMA((n, N_DEV - 1)), pltpu.SemaphoreType.DMA((n, N_DEV - 1)),
                        pltpu.SemaphoreType.DMA((n,))],
    )(*arrs, *([] if dep is None else [dep]))


HBM_SPEC = pl.BlockSpec(memory_space=pltpu.HBM)
SEM_SPEC = pl.BlockSpec(memory_space=pltpu.SEMAPHORE)
SPLIT_EFFECT = pltpu.SideEffectType.DATAFLOW_SIDE_EFFECTING
SPLIT_SEMS = {"gather": (N_DEV - 1, True), "pair": (N_CHIP, False), "chips": (N_CHIP - 1, True)}


def _split_descriptors(pattern, srcs, lands, sems):
    x, y, c = lax.axis_index("x"), lax.axis_index("y"), lax.axis_index("c")
    nsem, has_local = SPLIT_SEMS[pattern]
    per = 2 * nsem + int(has_local)
    starts, arrivals, local = [], [], []

    def remote(a, k, src, dst, to):
        return pltpu.make_async_remote_copy(src_ref=src, dst_ref=dst, send_sem=sems[a * per + k],
                                            recv_sem=sems[a * per + nsem + k], device_id=to, device_id_type=MESH_ID)

    for a in range(len(srcs)):
        if pattern == "gather":
            me = 4 * x + 2 * y + c
            local.append(pltpu.make_async_copy(srcs[a], lands[a].at[me], sems[a * per + 2 * nsem]))
            for k in range(1, N_DEV):
                px = (1 - x) if (k >> 2) & 1 else x
                py = (1 - y) if (k >> 1) & 1 else y
                pc = (1 - c) if k & 1 else c
                starts.append(remote(a, k - 1, srcs[a], lands[a].at[me], (px, py, pc)))
                arrivals.append(remote(a, k - 1, srcs[a], lands[a].at[4 * px + 2 * py + pc], (px, py, pc)))
        elif pattern == "pair":
            for q in range(N_CHIP):
                cp = remote(a, q, srcs[a].at[2 * q + 1 - c], lands[a].at[q], (x, y, 1 - c))
                starts.append(cp)
                arrivals.append(cp)
        else:
            mine = 2 * x + y
            local.append(pltpu.make_async_copy(srcs[a].at[mine], lands[a].at[mine], sems[a * per + 2 * nsem]))
            for k in range(1, N_CHIP):
                px = (1 - x) if (k >> 1) & 1 else x
                py = (1 - y) if k & 1 else y
                starts.append(remote(a, k - 1, srcs[a].at[2 * px + py], lands[a].at[mine], (px, py, c)))
                arrivals.append(remote(a, k - 1, srcs[a].at[2 * px + py], lands[a].at[2 * px + py], (px, py, c)))
    return starts, arrivals, local


def _split_start(pattern, arrs, name, after=None):
    n = len(arrs)
    extra = [] if after is None else [after]
    nsem, has_local = SPLIT_SEMS[pattern]
    if pattern == "gather":
        land_shapes = [(N_DEV,) + a.shape for a in arrs]
    elif pattern == "pair":
        land_shapes = [(N_CHIP,) + a.shape[1:] for a in arrs]
    else:
        land_shapes = [a.shape for a in arrs]
    nsem_out = n * (2 * nsem + int(has_local))

    def body(*refs):
        srcs, lands = refs[:n], refs[n:2 * n]
        first_sem = 2 * n + len(extra)
        sems = refs[first_sem:first_sem + nsem_out]
        token = refs[-1]
        starts, _, local = _split_descriptors(pattern, srcs, lands, sems)
        for cp in local + starts:
            cp.start()
        token[...] = jnp.zeros_like(token)

    out_shape = ([pltpu.SemaphoreType.DMA(())] * nsem_out + [pltpu.HBM(a.shape, a.dtype) for a in arrs]
                 + [pltpu.HBM(s, a.dtype) for s, a in zip(land_shapes, arrs)] + [jax.ShapeDtypeStruct((8, 128), F32)])
    ins = ([pltpu.with_memory_space_constraint(a, pltpu.HBM) for a in arrs]
           + [pltpu.with_memory_space_constraint(lax.empty(s, a.dtype), pltpu.HBM) for s, a in zip(land_shapes, arrs)])
    outs = pl.pallas_call(
        body, name=name, out_shape=out_shape, in_specs=[HBM_SPEC] * (2 * n) + [ANY_SPEC] * len(extra),
        out_specs=[SEM_SPEC] * nsem_out + [HBM_SPEC] * (2 * n) + [pl.BlockSpec(memory_space=pltpu.VMEM)],
        input_output_aliases={i: nsem_out + i for i in range(2 * n)},
        compiler_params=pltpu.CompilerParams(has_side_effects=SPLIT_EFFECT),
    )(*ins, *extra)
    handle = dict(pattern=pattern, n=n, sems=outs[:nsem_out], srcs=outs[nsem_out:nsem_out + n],
                  lands=outs[nsem_out + n:nsem_out + 2 * n])
    return handle, outs[-1]


def _split_wait(handle, after, name):
    pattern, n = handle["pattern"], handle["n"]
    nsem_in = len(handle["sems"])

    def body(*refs):
        srcs, lands = refs[:n], refs[n:2 * n]
        starts, arrivals, local = _split_descriptors(pattern, srcs, lands, refs[2 * n:2 * n + nsem_in])
        for cp in starts:
            cp.wait_send()
        for cp in arrivals:
            cp.wait_recv()
        for cp in local:
            cp.wait()

    srcs, lands = handle["srcs"], handle["lands"]
    outs = pl.pallas_call(
        body, name=name,
        out_shape=[pltpu.HBM(a.shape, a.dtype) for a in srcs] + [pltpu.HBM(a.shape, a.dtype) for a in lands],
        in_specs=[HBM_SPEC] * (2 * n) + [SEM_SPEC] * nsem_in + [ANY_SPEC], out_specs=[HBM_SPEC] * (2 * n),
        input_output_aliases={i: i for i in range(2 * n)},
        compiler_params=pltpu.CompilerParams(has_side_effects=SPLIT_EFFECT),
    )(*srcs, *lands, *handle["sems"], after)
    return outs[:n], outs[n:]


def _pair_add(core, a8s, b4s, name):
    n = len(a8s)

    def body(core_ref, *refs):
        for i in range(n):
            refs[2 * n + i][...] = (refs[i][...] + refs[n + i][...]).astype(BF16)

    own = [pl.BlockSpec((1,) + b.shape[1:], lambda q, core_ref: (2 * q + core_ref[0], 0, 0)) for b in b4s]
    slot = [pl.BlockSpec((1,) + b.shape[1:], lambda q, core_ref: (q, 0, 0)) for b in b4s]
    grid_spec = pltpu.PrefetchScalarGridSpec(num_scalar_prefetch=1, grid=(N_CHIP,), in_specs=own + slot, out_specs=slot)
    return pl.pallas_call(
        body, name=name, grid_spec=grid_spec, out_shape=[jax.ShapeDtypeStruct(b.shape, BF16) for b in b4s],
        compiler_params=_cparams(("arbitrary",)),
    )(core, *a8s, *b4s)


def _sum_slots(x, name):
    _, r, c = x.shape

    def body(x_ref, o_ref):
        acc = x_ref[0]
        for j in range(1, N_DEV):
            acc = acc + x_ref[j]
        o_ref[...] = acc

    return pl.pallas_call(body, name=name, out_shape=jax.ShapeDtypeStruct((r, c), F32))(x)


def _ada_fwd(c_all, w_ada):
    nb = c_all.shape[0]
    cols = w_ada.shape[2]

    def body(c_ref, w_ref, act_ref, mod_ref):
        cv = c_ref[...]
        act = cv * _sigmoid(cv)
        act_ref[...] = act
        for l in range(DEPTH):
            mod_ref[l] = jnp.dot(act, w_ref[l], precision=HI, preferred_element_type=F32)

    return pl.pallas_call(
        body, name="ada_fwd",
        out_shape=[jax.ShapeDtypeStruct((nb, D_MODEL), F32), jax.ShapeDtypeStruct((DEPTH, nb, cols), F32)],
        compiler_params=_cparams(),
    )(c_all, w_ada)


def _ada_bwd(c_act, dmod_all, dmod_mine, dep):
    nb = c_act.shape[0]
    cols = dmod_mine.shape[2]

    def body(act_ref, dall_ref, dmine_ref, dep_ref, gw_ref, gb_ref):
        act = act_ref[...]
        for l in range(DEPTH):
            gw_ref[l] = lax.dot_general(act, dmine_ref[l], (((0,), (0,)), ((), ())),
                                        precision=HI, preferred_element_type=F32)
            gb_ref[l:l + 1, :] = jnp.sum(dall_ref[l], axis=0, keepdims=True)

    return pl.pallas_call(
        body, name="ada_bwd",
        out_shape=[jax.ShapeDtypeStruct((DEPTH, D_MODEL, cols), F32),
                   jax.ShapeDtypeStruct((DEPTH, 3 * D_MODEL), F32)],
        compiler_params=_cparams(),
    )(c_act, dmod_all, dmod_mine, dep)


def _ln_in(x, ss, g, w, seq, tm=ROW_TILE, dep=None):
    t = x.shape[0]
    tm = min(tm, seq)
    tps = seq // tm

    def body(x_ref, ss_ref, g_ref, w_ref, h_ref, *outs):
        xv = x_ref[...]
        xn = xv * lax.rsqrt(jnp.mean(xv * xv, axis=-1, keepdims=True) + EPS)
        h = xn * g_ref[...] * ss_ref[0, 1:2, :] + ss_ref[0, 0:1, :]
        hb = h.astype(BF16)
        h_ref[...] = hb
        z = jnp.dot(hb, w_ref[...], preferred_element_type=F32)
        for o_ref, (_, off, wd, _) in zip(outs, Z_SEGS):
            o_ref[...] = z[:, off:off + wd].astype(o_ref.dtype)

    row = lambda wd: pl.BlockSpec((tm, wd), lambda i: (i, 0))
    in_specs = [row(D_MODEL), pl.BlockSpec((1, 2, D_MODEL), lambda i: (i // tps, 0, 0)),
                pl.BlockSpec((1, D_MODEL), lambda i: (0, 0)), pl.BlockSpec((D_MODEL, N_PAD), lambda i: (0, 0))]
    body, in_specs, args = _after(dep, body, in_specs, [x, ss, g, w])
    return pl.pallas_call(
        body, name="ln_in", grid=(t // tm,), in_specs=in_specs,
        out_specs=[row(D_MODEL)] + [row(wd) for _, _, wd, _ in Z_SEGS],
        out_shape=[jax.ShapeDtypeStruct((t, D_MODEL), BF16)]
        + [jax.ShapeDtypeStruct((t, wd), dt) for _, _, wd, dt in Z_SEGS],
        compiler_params=_cparams(("arbitrary",)),
    )(*args)


def _ln_in_bwd(dz, w_t, x, ss, g, dxo, seq, tm=ROW_TILE, dep=None):
    t = x.shape[0]
    tm = min(tm, seq)
    tps = seq // tm
    nb = t // seq
    nz = len(Z_SEGS)

    def body(*refs):
        dz_refs = refs[:nz]
        wt_ref, x_ref, ss_ref, g_ref, dxo_ref, dx_ref, dss_ref, dg_ref = refs[nz:]
        i = pl.program_id(0)
        dzc = jnp.concatenate([r[...].astype(BF16) for r in dz_refs], axis=1)
        dh = _nt(dzc, wt_ref[...])
        xv = x_ref[...]
        rstd = lax.rsqrt(jnp.mean(xv * xv, axis=-1, keepdims=True) + EPS)
        xn = xv * rstd
        gv = g_ref[...]
        s1 = ss_ref[0, 1:2, :]
        dxg = dh * s1
        dxn = dxg * gv
        dx = rstd * (dxn - xn * jnp.mean(dxn * xn, axis=-1, keepdims=True))
        dx_ref[...] = dxo_ref[...] + dx
        dshift = jnp.sum(dh, axis=0, keepdims=True)
        dscale = jnp.sum(dh * (xn * gv), axis=0, keepdims=True)
        dgp = jnp.sum(dxg * xn, axis=0, keepdims=True)

        @pl.when(i % tps == 0)
        def _():
            dss_ref[0, 0:1, :] = dshift
            dss_ref[0, 1:2, :] = dscale

        @pl.when(i % tps != 0)
        def _():
            dss_ref[0, 0:1, :] += dshift
            dss_ref[0, 1:2, :] += dscale

        @pl.when(i == 0)
        def _():
            dg_ref[...] = dgp

        @pl.when(i != 0)
        def _():
            dg_ref[...] += dgp

    row = lambda wd: pl.BlockSpec((tm, wd), lambda i: (i, 0))
    in_specs = ([row(wd) for _, _, wd, _ in Z_SEGS]
                + [pl.BlockSpec((D_MODEL, N_PAD), lambda i: (0, 0)), row(D_MODEL),
                   pl.BlockSpec((1, 2, D_MODEL), lambda i: (i // tps, 0, 0)),
                   pl.BlockSpec((1, D_MODEL), lambda i: (0, 0)), row(D_MODEL)])
    body, in_specs, args = _after(dep, body, in_specs, [*dz, w_t, x, ss, g, dxo])
    return pl.pallas_call(
        body, name="ln_in_bwd", grid=(t // tm,), in_specs=in_specs,
        out_specs=[row(D_MODEL), pl.BlockSpec((1, 2, D_MODEL), lambda i: (i // tps, 0, 0)),
                   pl.BlockSpec((1, D_MODEL), lambda i: (0, 0))],
        out_shape=[jax.ShapeDtypeStruct((t, D_MODEL), F32), jax.ShapeDtypeStruct((nb, 2, D_MODEL), F32),
                   jax.ShapeDtypeStruct((1, D_MODEL), F32)],
        compiler_params=_cparams(("arbitrary",)),
    )(*args)


def _matmul_tn(a, bs, name, tm=2048, dep=None):
    bs = list(bs) if isinstance(bs, (list, tuple)) else [bs]
    t, k = a.shape
    widths = [b.shape[1] for b in bs]
    n = sum(widths)
    tm = min(tm, t)

    def body(a_ref, *refs):
        b_refs, o_ref = refs[:-1], refs[-1]
        i = pl.program_id(0)
        av = a_ref[...].astype(BF16)
        parts = [b_ref[...].astype(BF16) for b_ref in b_refs]
        bv = parts[0] if len(parts) == 1 else jnp.concatenate(parts, axis=1)
        part = lax.dot_general(av, bv, (((0,), (0,)), ((), ())), preferred_element_type=F32)

        @pl.when(i == 0)
        def _():
            o_ref[...] = part

        @pl.when(i != 0)
        def _():
            o_ref[...] += part

    in_specs = [pl.BlockSpec((tm, k), lambda i: (i, 0))] + [pl.BlockSpec((tm, wd), lambda i: (i, 0)) for wd in widths]
    body, in_specs, args = _after(dep, body, in_specs, [a, *bs])
    return pl.pallas_call(
        body, name=name, grid=(t // tm,), in_specs=in_specs,
        out_specs=pl.BlockSpec((k, n), lambda i: (0, 0)),
        out_shape=jax.ShapeDtypeStruct((k, n), F32),
        compiler_params=_cparams(("arbitrary",)),
    )(*args)


def _rope(blk, cos_t, sin_a, sin_b):
    return blk * cos_t + pltpu.roll(blk, 112, 1) * sin_a + pltpu.roll(blk, 16, 1) * sin_b


def _unrope(d, cos_t, sin_a, sin_b):
    return d * cos_t + pltpu.roll(d * sin_a, 16, 1) + pltpu.roll(d * sin_b, 112, 1)


def _mla_prep(cq, ckv, kpe, gq, gkv, wuq, wk, wv, cos_t, sin_a, sin_b, tm=ROW_TILE, dep=None):
    t = cq.shape[0]
    tm = min(tm, t)
    qw = A_HEADS * HEAD_PAD

    def body(cq_ref, ckv_ref, kpe_ref, gq_ref, gkv_ref, wuq_ref, wk_ref, wv_ref, c_ref, sa_ref, sb_ref,
             q_ref, k_ref, v_ref, cqn_ref, ckvn_ref):
        ct, sa, sb = c_ref[...], sa_ref[...], sb_ref[...]
        a = cq_ref[...]
        cqn = (a * lax.rsqrt(jnp.mean(a * a, axis=-1, keepdims=True) + EPS) * gq_ref[...]).astype(BF16)
        cqn_ref[...] = cqn
        b = ckv_ref[...]
        ckvn = (b * lax.rsqrt(jnp.mean(b * b, axis=-1, keepdims=True) + EPS) * gkv_ref[...]).astype(BF16)
        ckvn_ref[...] = ckvn
        qlin = jnp.dot(cqn, wuq_ref[...], preferred_element_type=F32)
        klin = jnp.dot(ckvn, wk_ref[...], preferred_element_type=F32)
        v_ref[...] = jnp.dot(ckvn, wv_ref[...], preferred_element_type=F32).astype(BF16)
        kr = _rope(kpe_ref[...], ct, sa, sb)
        for h in range(A_HEADS):
            sl = slice(h * HEAD_PAD, (h + 1) * HEAD_PAD)
            q_ref[:, sl] = _rope(qlin[:, sl], ct, sa, sb).astype(BF16)
            k_ref[:, sl] = (klin[:, sl] + kr).astype(BF16)

    row = lambda wd: pl.BlockSpec((tm, wd), lambda i: (i, 0))
    full = lambda r, c: pl.BlockSpec((r, c), lambda i: (0, 0))
    in_specs = [row(A_Q_RANK), row(A_KV_RANK), row(128), full(1, A_Q_RANK), full(1, A_KV_RANK),
                full(A_Q_RANK, qw), full(A_KV_RANK, qw), full(A_KV_RANK, GW), row(128), row(128), row(128)]
    body, in_specs, args = _after(dep, body, in_specs, [cq, ckv, kpe, gq, gkv, wuq, wk, wv, cos_t, sin_a, sin_b])
    return pl.pallas_call(
        body, name="mla_prep", grid=(t // tm,), in_specs=in_specs,
        out_specs=[row(qw), row(qw), row(GW), row(A_Q_RANK), row(A_KV_RANK)],
        out_shape=[jax.ShapeDtypeStruct((t, qw), BF16), jax.ShapeDtypeStruct((t, qw), BF16),
                   jax.ShapeDtypeStruct((t, GW), BF16), jax.ShapeDtypeStruct((t, A_Q_RANK), BF16),
                   jax.ShapeDtypeStruct((t, A_KV_RANK), BF16)],
        compiler_params=_cparams(("arbitrary",)),
    )(*args)


def _mla_prep_bwd(dq, dk, dv, cq, ckv, gq, gkv, wuq_t, wk_t, wv_t, cos_t, sin_a, sin_b, tm=ROW_TILE):
    t = cq.shape[0]
    tm = min(tm, t)
    qw = A_HEADS * HEAD_PAD

    def body(dq_ref, dk_ref, dv_ref, cq_ref, ckv_ref, gq_ref, gkv_ref, wuqt_ref, wkt_ref, wvt_ref,
             c_ref, sa_ref, sb_ref, dcq_ref, dckv_ref, dkpe_ref, dql_ref, dkl_ref, dgq_ref, dgkv_ref):
        i = pl.program_id(0)
        ct, sa, sb = c_ref[...], sa_ref[...], sb_ref[...]
        lane = lax.broadcasted_iota(jnp.int32, (1, HEAD_PAD), 1)
        nope = lane < A_NOPE
        rope = (lane >= A_NOPE) & (lane < A_NOPE + A_ROPE)
        dksum = None
        for h in range(A_HEADS):
            sl = slice(h * HEAD_PAD, (h + 1) * HEAD_PAD)
            dql_ref[:, sl] = _unrope(dq_ref[:, sl].astype(F32), ct, sa, sb).astype(BF16)
            dkh = dk_ref[:, sl].astype(F32)
            dkl_ref[:, sl] = jnp.where(nope, dkh, 0.0).astype(BF16)
            dksum = dkh if dksum is None else dksum + dkh
        dkpe_ref[...] = jnp.where(rope, _unrope(jnp.where(rope, dksum, 0.0), ct, sa, sb), 0.0).astype(BF16)
        dcqn = jnp.dot(dql_ref[...], wuqt_ref[...], preferred_element_type=F32)
        dckvn = (jnp.dot(dkl_ref[...], wkt_ref[...], preferred_element_type=F32)
                 + jnp.dot(dv_ref[...].astype(BF16), wvt_ref[...], preferred_element_type=F32))

        def norm_bwd(xv, gv, dy):
            rstd = lax.rsqrt(jnp.mean(xv * xv, axis=-1, keepdims=True) + EPS)
            xn = xv * rstd
            dxn = dy * gv
            dx = rstd * (dxn - xn * jnp.mean(dxn * xn, axis=-1, keepdims=True))
            return dx, jnp.sum(dy * xn, axis=0, keepdims=True)

        dcq, dgq = norm_bwd(cq_ref[...], gq_ref[...], dcqn)
        dckv, dgkv = norm_bwd(ckv_ref[...], gkv_ref[...], dckvn)
        dcq_ref[...] = dcq.astype(BF16)
        dckv_ref[...] = dckv.astype(BF16)

        @pl.when(i == 0)
        def _():
            dgq_ref[...] = dgq
            dgkv_ref[...] = dgkv

        @pl.when(i != 0)
        def _():
            dgq_ref[...] += dgq
            dgkv_ref[...] += dgkv

    row = lambda wd: pl.BlockSpec((tm, wd), lambda i: (i, 0))
    full = lambda r, c: pl.BlockSpec((r, c), lambda i: (0, 0))
    return pl.pallas_call(
        body, name="mla_prep_bwd", grid=(t // tm,),
        in_specs=[row(qw), row(qw), row(GW), row(A_Q_RANK), row(A_KV_RANK), full(1, A_Q_RANK), full(1, A_KV_RANK),
                  full(qw, A_Q_RANK), full(qw, A_KV_RANK), full(GW, A_KV_RANK), row(128), row(128), row(128)],
        out_specs=[row(A_Q_RANK), row(A_KV_RANK), row(128), row(qw), row(qw), full(1, A_Q_RANK), full(1, A_KV_RANK)],
        out_shape=[jax.ShapeDtypeStruct((t, A_Q_RANK), BF16), jax.ShapeDtypeStruct((t, A_KV_RANK), BF16),
                   jax.ShapeDtypeStruct((t, 128), BF16), jax.ShapeDtypeStruct((t, qw), BF16),
                   jax.ShapeDtypeStruct((t, qw), BF16), jax.ShapeDtypeStruct((1, A_Q_RANK), F32),
                   jax.ShapeDtypeStruct((1, A_KV_RANK), F32)],
        compiler_params=_cparams(("arbitrary",)),
    )(dq, dk, dv, cq, ckv, gq, gkv, wuq_t, wk_t, wv_t, cos_t, sin_a, sin_b)


def _nt(a, b):
    return lax.dot_general(a, b, (((1,), (1,)), ((), ())), preferred_element_type=F32)


def _tn(a, b):
    return lax.dot_general(a, b, (((0,), (0,)), ((), ())), preferred_element_type=F32)


def _causal_mask(kind, q0, k0, tq, tk):
    qpos = q0 + lax.broadcasted_iota(jnp.int32, (tq, tk), 0)
    kpos = k0 + lax.broadcasted_iota(jnp.int32, (tq, tk), 1)
    if kind == "mla":
        return lax.shift_right_logical(kpos, 6) <= lax.shift_right_logical(qpos, 6)
    return kpos <= qpos


def _attn_fwd(kind, q, k, v, f, seq, scale, tq=512, tk=512):
    t = v.shape[0]
    nb = t // seq
    nq = seq // tq
    hw = 256 if kind == "mla" else 128
    n_heads = A_HEADS if kind == "mla" else C_HEADS
    use_f = f is not None
    tq, tk = min(tq, seq), min(tk, seq)
    nq = seq // tq
    assert tk == tq

    def body(*refs):
        if use_f:
            q_ref, k_ref, v_ref, f_ref, o_ref, st_ref = refs
        else:
            q_ref, k_ref, v_ref, o_ref, st_ref = refs
        qi = pl.program_id(2)
        q0 = qi * tq
        lane = lax.broadcasted_iota(jnp.int32, (1, 128), 1)
        half = lane >= 64
        qall = q_ref[...]
        if kind == "mla":
            qhs = [qall[:, 0:128], qall[:, 128:256]]
            post = scale * math.log2(math.e)
        else:
            assert math.frexp(scale)[0] == 0.5
            qall = qall * jnp.asarray(scale, BF16)
            qhs = [jnp.where(half, jnp.zeros_like(qall), qall), jnp.where(half, qall, jnp.zeros_like(qall))]
            post = None
        kd = pl.multiple_of(q0, tq)
        diag = _causal_mask(kind, 0, 0, tq, tk)

        def block(j, k0, state, masked):
            m, l, acc = state
            kh = k_ref[pl.ds(k0, tk), j * 128:(j + 1) * 128] if kind == "mla" else k_ref[pl.ds(k0, tk), :]
            s = _nt(qhs[j], kh)
            if post is not None:
                s = s * post
            if use_f:
                s = s - f_ref[0, 0, j:j + 1, pl.ds(k0, tk)]
            if masked:
                s = jnp.where(diag, s, NEG)
            mn = jnp.maximum(m, jnp.max(s, axis=-1, keepdims=True))
            alpha = jnp.exp2(m - mn) if post is not None else jnp.exp(m - mn)
            p = jnp.exp2(s - mn) if post is not None else jnp.exp(s - mn)
            l = alpha * l + jnp.sum(p, axis=-1, keepdims=True)
            acc = alpha * acc + jnp.dot(p.astype(BF16), v_ref[pl.ds(k0, tk), :], preferred_element_type=F32)
            return mn, l, acc

        def run(heads):
            def kstep(kb, carry):
                k0 = pl.multiple_of(kb * tk, tk)
                out = ()
                for n, j in enumerate(heads):
                    out += block(j, k0, carry[3 * n:3 * n + 3], False)
                return out

            init = (jnp.full((tq, 1), NEG, F32), jnp.zeros((tq, 1), F32), jnp.zeros((tq, 128), F32)) * len(heads)
            carry = lax.fori_loop(0, qi, kstep, init)
            o, st = jnp.zeros((tq, 128), F32), jnp.zeros((tq, 128), F32)
            for n, j in enumerate(heads):
                m, l, acc = block(j, kd, carry[3 * n:3 * n + 3], True)
                o = jnp.where(half == bool(j), acc / l, o)
                if post is not None:
                    m = m * math.log(2.0)
                st = jnp.where(lane == j, m + jnp.log(l), st)
            o_ref[...] = o
            st_ref[...] = st

        if n_heads % 2 == 0:
            run((0, 1))
        else:
            last = pl.program_id(1) == n_heads // 2
            pl.when(jnp.logical_not(last))(lambda: run((0, 1)))
            pl.when(last)(lambda: run((0,)))

    in_specs = [pl.BlockSpec((tq, hw), lambda b, p, i: (b * nq + i, p)),
                pl.BlockSpec((seq, hw), lambda b, p, i: (b, p)),
                pl.BlockSpec((seq, 128), lambda b, p, i: (b, p))]
    args = [q, k, v]
    if use_f:
        in_specs.append(pl.BlockSpec((1, 1, 8, seq), lambda b, p, i: (b, p, 0, 0)))
        args.append(f)
    oblk = pl.BlockSpec((tq, 128), lambda b, p, i: (b * nq + i, p))
    return pl.pallas_call(
        body, name="attn_fwd_" + kind, grid=(nb, 3, nq), in_specs=in_specs, out_specs=[oblk, oblk],
        out_shape=[jax.ShapeDtypeStruct((t, GW), F32), jax.ShapeDtypeStruct((t, GW), F32)],
        compiler_params=_cparams(("arbitrary", "arbitrary", "arbitrary")),
    )(*args)


def _attn_bwd(kind, q, k, v, f, o, st, do, seq, scale, tq=512, tk=512, dep=None):
    t = v.shape[0]
    nb = t // seq
    tq, tk = min(tq, seq), min(tk, seq)
    nq = seq // tq
    nk = seq // tk
    hw = 256 if kind == "mla" else 128
    n_heads = A_HEADS if kind == "mla" else C_HEADS
    use_f = f is not None
    assert tq == tk

    def body(*refs):
        if use_f:
            (q_ref, k_ref, v_ref, f_ref, o_ref, st_ref, do_ref, dq_out, dk_out, dv_out, df_ref, dfq_ref,
             dq_ref, dk_ref, dv_ref) = refs
        else:
            q_ref, k_ref, v_ref, o_ref, st_ref, do_ref, dq_out, dk_out, dv_out, dq_ref, dk_ref, dv_ref = refs
        kj = pl.program_id(2)
        lane = lax.broadcasted_iota(jnp.int32, (1, 128), 1)
        half = lane >= 64

        @pl.when(kj == 0)
        def _():
            dq_ref[...] = jnp.zeros_like(dq_ref)
            if use_f:
                dfq_ref[...] = jnp.zeros_like(dfq_ref)

        dk_ref[...] = jnp.zeros_like(dk_ref)
        dv_ref[...] = jnp.zeros_like(dv_ref)
        if use_f:
            df_ref[...] = jnp.zeros_like(df_ref)
        vv = v_ref[...]
        diag = _causal_mask(kind, 0, 0, tq, tk)

        def qstep(qi, masked):
            q0 = pl.multiple_of(qi * tq, tq)
            rows = pl.ds(q0, tq)
            dov = do_ref[rows, :]
            dd = dov * o_ref[rows, :]
            stv = st_ref[rows, :]

            def one_head(j):
                hm = half == bool(j)
                delta = jnp.sum(jnp.where(hm, dd, 0.0), axis=-1, keepdims=True)
                lse = stv[:, j:j + 1]
                if kind == "mla":
                    cols = slice(j * 128, (j + 1) * 128)
                    qh = q_ref[rows, cols]
                    kh = k_ref[:, cols]
                else:
                    cols = slice(0, 128)
                    qa = q_ref[rows, :]
                    qh = jnp.where(hm, qa, jnp.zeros_like(qa))
                    kh = k_ref[...]
                s = _nt(qh, kh) * scale
                if use_f:
                    s = s - f_ref[0, 0, j:j + 1, :]
                if masked:
                    s = jnp.where(diag, s, NEG)
                p = jnp.exp(s - lse)
                doh = jnp.where(hm, dov, 0.0).astype(BF16)
                ds = p * (_nt(doh, vv) - delta)
                dsb = (ds * scale).astype(BF16)
                dv_ref[...] += _tn(p.astype(BF16), doh)
                dk_ref[:, cols] += _tn(dsb, qh)
                dqc = jnp.dot(dsb, kh, preferred_element_type=F32)
                if kind != "mla":
                    dqc = jnp.where(hm, dqc, 0.0)
                dq_ref[rows, cols] += dqc
                if use_f:
                    df_ref[0, 0, j:j + 1, :] += -jnp.sum(ds, axis=0, keepdims=True)
                    dfq_ref[rows, :] += jnp.where(lane == j, jnp.sum(ds, axis=-1, keepdims=True), 0.0)

            def both():
                one_head(0)
                one_head(1)

            if n_heads % 2 == 0:
                both()
            else:
                last = pl.program_id(1) == n_heads // 2
                pl.when(jnp.logical_not(last))(both)
                pl.when(last)(lambda: one_head(0))

        qstep(kj, True)

        def rest(qi, carry):
            qstep(qi, False)
            return carry

        lax.fori_loop(kj + 1, nq, rest, 0)
        dk_out[...] = dk_ref[...].astype(BF16)
        dv_out[...] = dv_ref[...].astype(BF16)

        @pl.when(kj == nk - 1)
        def _():
            dq_out[...] = dq_ref[...].astype(BF16)

    full_q = lambda wd: pl.BlockSpec((seq, wd), lambda b, p, i: (b, p))
    kblk = lambda wd: pl.BlockSpec((tk, wd), lambda b, p, i: (b * nk + i, p))
    in_specs = [full_q(hw), kblk(hw), kblk(128)]
    args = [q, k, v]
    if use_f:
        in_specs.append(pl.BlockSpec((1, 1, 8, tk), lambda b, p, i: (b, p, 0, i)))
        args.append(f)
    in_specs += [full_q(128), full_q(128), full_q(128)]
    args += [o, st, do]
    out_specs = [full_q(hw), kblk(hw), kblk(128)]
    out_shape = [jax.ShapeDtypeStruct((t, 3 * hw), BF16), jax.ShapeDtypeStruct((t, 3 * hw), BF16),
                 jax.ShapeDtypeStruct((t, GW), BF16)]
    scratch = [pltpu.VMEM((seq, hw), F32), pltpu.VMEM((tk, hw), F32), pltpu.VMEM((tk, 128), F32)]
    if use_f:
        out_specs += [pl.BlockSpec((1, 1, 8, tk), lambda b, p, i: (b, p, 0, i)), full_q(128)]
        out_shape += [jax.ShapeDtypeStruct((nb, 3, 8, seq), F32), jax.ShapeDtypeStruct((t, GW), F32)]
    body, in_specs, args = _after(dep, body, in_specs, args)
    return pl.pallas_call(
        body, name="attn_bwd_" + kind, grid=(nb, 3, nk), in_specs=in_specs, out_specs=out_specs,
        out_shape=out_shape, scratch_shapes=scratch,
        compiler_params=_cparams(("arbitrary", "arbitrary", "arbitrary")),
    )(*args)


BQ = 256
BWIN = BQ + B_LEFT


def _band_geometry():
    r = lax.broadcasted_iota(jnp.int32, (BQ, BWIN), 0)
    j = lax.broadcasted_iota(jnp.int32, (BQ, BWIN), 1)
    rc = lax.shift_right_logical(r, 6)
    jc = lax.shift_right_logical(j, 6)
    allowed = (jc - 8 <= rc) & (rc <= jc)
    return (r + B_LEFT - j) >= REL_CLIP, allowed, j < r


def _band_onehot(transposed, offset=0):
    shape = (BWIN, GW) if transposed else (GW, BWIN)
    kk = lax.broadcasted_iota(jnp.int32, shape, 1 if transposed else 0)
    x = lax.broadcasted_iota(jnp.int32, shape, 0 if transposed else 1) - offset
    x = jnp.where(x < 0, x + BWIN, x)
    return (kk == jnp.clip(B_LEFT - x, -REL_CLIP, REL_CLIP) + REL_CLIP).astype(F32)


def _band_table(rel_bias8):
    def body(b_ref, o_ref):
        hh = pl.program_id(0)
        u8 = jnp.dot(b_ref[...], _band_onehot(False), precision=HI, preferred_element_type=F32)
        rid = lax.broadcasted_iota(jnp.int32, (8, BWIN), 0)
        row = jnp.sum(jnp.where(rid == hh, u8, 0.0), axis=0, keepdims=True)
        far, allowed, _ = _band_geometry()
        tbl = pltpu.roll(jnp.broadcast_to(row, (BQ, BWIN)), 0, 1, stride=1, stride_axis=0)
        tbl = jnp.where(far, row[:, 0:1], tbl)
        o_ref[0] = jnp.where(allowed, tbl, NEG)

    return pl.pallas_call(
        body, name="band_table", grid=(6,),
        in_specs=[pl.BlockSpec((8, GW), lambda h: (0, 0))],
        out_specs=pl.BlockSpec((1, BQ, BWIN), lambda h: (h, 0, 0)),
        out_shape=jax.ShapeDtypeStruct((6, BQ, BWIN), F32),
        compiler_params=_cparams(("arbitrary",)),
    )(rel_bias8)


def _band_table_bwd(gtab):
    def body(g_ref, o_ref):
        gv = g_ref[0]
        _, _, wrapped = _band_geometry()
        gfar = jnp.sum(jnp.sum(jnp.where(wrapped, gv, 0.0), axis=-1, keepdims=True), axis=0, keepdims=True)
        anti = (lax.broadcasted_iota(jnp.int32, (BQ, BQ), 0) + lax.broadcasted_iota(jnp.int32, (BQ, BQ), 1)
                == BQ - 1).astype(F32)
        grev = jnp.dot(anti, jnp.where(wrapped, 0.0, gv), precision=HI, preferred_element_type=F32)
        near = pltpu.roll(grev, 0, 1, stride=1, stride_axis=0)
        y = jnp.broadcast_to(jnp.sum(near, axis=0, keepdims=True), (8, BWIN))
        gb = jnp.dot(y, _band_onehot(True, BQ - 1), precision=HI, preferred_element_type=F32)
        lane = lax.broadcasted_iota(jnp.int32, (8, GW), 1)
        o_ref[0] = gb + jnp.where(lane == 2 * REL_CLIP, gfar, 0.0)

    return pl.pallas_call(
        body, name="band_table_bwd", grid=(B_HEADS,),
        in_specs=[pl.BlockSpec((1, BQ, BWIN), lambda h: (h, 0, 0))],
        out_specs=pl.BlockSpec((1, 8, GW), lambda h: (h, 0, 0)),
        out_shape=jax.ShapeDtypeStruct((B_HEADS, 8, GW), F32),
        compiler_params=_cparams(("arbitrary",)),
    )(gtab)


def _band_fwd(q, k, v, table, seq, scale):
    t = q.shape[0]
    nb = t // seq
    nq = seq // BQ

    def body(q_ref, k_ref, v_ref, tb_ref, o_ref, st_ref, kpad, vpad):
        qi = pl.program_id(2)
        q0 = pl.multiple_of(qi * BQ, BQ)
        lane = lax.broadcasted_iota(jnp.int32, (1, 128), 1)
        half = lane >= 64

        @pl.when(qi == 0)
        def _():
            kpad[0:B_LEFT, :] = jnp.zeros((B_LEFT, 128), BF16)
            vpad[0:B_LEFT, :] = jnp.zeros((B_LEFT, 128), BF16)
            kpad[B_LEFT:, :] = k_ref[...]
            vpad[B_LEFT:, :] = v_ref[...]

        kw = kpad[pl.ds(q0, BWIN), :]
        vw = vpad[pl.ds(q0, BWIN), :]
        inside = lax.broadcasted_iota(jnp.int32, (BQ, BWIN), 1) >= B_LEFT - q0
        assert math.frexp(scale)[0] == 0.5
        qall = q_ref[...] * jnp.asarray(scale, BF16)

        def run(heads):
            o, st = jnp.zeros((BQ, 128), F32), jnp.zeros((BQ, 128), F32)
            for j in heads:
                qh = jnp.where(half == bool(j), qall, jnp.zeros_like(qall))
                s = jnp.where(inside, _nt(qh, kw) + tb_ref[j], NEG)
                m = jnp.max(s, axis=-1, keepdims=True)
                p = jnp.exp(s - m)
                l = jnp.sum(p, axis=-1, keepdims=True)
                o = jnp.where(half == bool(j), jnp.dot(p.astype(BF16), vw, preferred_element_type=F32) / l, o)
                st = jnp.where(lane == j, m + jnp.log(l), st)
            o_ref[...] = o
            st_ref[...] = st

        last = pl.program_id(1) == B_HEADS // 2
        pl.when(jnp.logical_not(last))(lambda: run((0, 1)))
        pl.when(last)(lambda: run((0,)))

    qblk = pl.BlockSpec((BQ, 128), lambda b, p, i: (b * nq + i, p))
    full = pl.BlockSpec((seq, 128), lambda b, p, i: (b, p))
    return pl.pallas_call(
        body, name="band_fwd", grid=(nb, 3, nq),
        in_specs=[qblk, full, full, pl.BlockSpec((2, BQ, BWIN), lambda b, p, i: (p, 0, 0))],
        out_specs=[qblk, qblk],
        out_shape=[jax.ShapeDtypeStruct((t, GW), F32), jax.ShapeDtypeStruct((t, GW), F32)],
        scratch_shapes=[pltpu.VMEM((seq + B_LEFT, 128), BF16), pltpu.VMEM((seq + B_LEFT, 128), BF16)],
        compiler_params=_cparams(("arbitrary", "arbitrary", "arbitrary")),
    )(q, k, v, table)


def _band_bwd(q, k, v, table, o, st, do, seq, scale, dep=None):
    t = q.shape[0]
    nb = t // seq
    nq = seq // BQ

    def body(q_ref, k_ref, v_ref, tb_ref, o_ref, st_ref, do_ref, dq_ref, dk_ref, dv_ref, g_ref,
             kpad, vpad, dkpad, dvpad):
        b = pl.program_id(1)
        qi = pl.program_id(2)
        q0 = pl.multiple_of(qi * BQ, BQ)
        lane = lax.broadcasted_iota(jnp.int32, (1, 128), 1)
        half = lane >= 64

        @pl.when(qi == 0)
        def _():
            kpad[0:B_LEFT, :] = jnp.zeros((B_LEFT, 128), BF16)
            vpad[0:B_LEFT, :] = jnp.zeros((B_LEFT, 128), BF16)
            kpad[B_LEFT:, :] = k_ref[...]
            vpad[B_LEFT:, :] = v_ref[...]
            dkpad[...] = jnp.zeros_like(dkpad)
            dvpad[...] = jnp.zeros_like(dvpad)

        @pl.when((qi == 0) & (b == 0))
        def _():
            g_ref[...] = jnp.zeros_like(g_ref)

        win = pl.ds(q0, BWIN)
        kw = kpad[win, :]
        vw = vpad[win, :]
        inside = lax.broadcasted_iota(jnp.int32, (BQ, BWIN), 1) >= B_LEFT - q0
        qall = q_ref[...]
        dov = do_ref[...]
        dd = dov * o_ref[...]
        stv = st_ref[...]

        def run(heads):
            dq = jnp.zeros((BQ, 128), F32)
            for j in heads:
                hm = half == bool(j)
                qh = jnp.where(hm, qall, jnp.zeros_like(qall))
                delta = jnp.sum(jnp.where(hm, dd, 0.0), axis=-1, keepdims=True)
                s = jnp.where(inside, _nt(qh, kw) * scale + tb_ref[j], NEG)
                p = jnp.exp(s - stv[:, j:j + 1])
                doh = jnp.where(hm, dov, 0.0).astype(BF16)
                ds = p * (_nt(doh, vw) - delta)
                g_ref[j] += ds
                dsb = (ds * scale).astype(BF16)
                dvpad[win, :] += _tn(p.astype(BF16), doh)
                dkpad[win, :] += _tn(dsb, qh)
                dq = dq + jnp.where(hm, jnp.dot(dsb, kw, preferred_element_type=F32), 0.0)
            dq_ref[...] = dq.astype(BF16)

        last = pl.program_id(0) == B_HEADS // 2
        pl.when(jnp.logical_not(last))(lambda: run((0, 1)))
        pl.when(last)(lambda: run((0,)))

        @pl.when(qi == nq - 1)
        def _():
            dk_ref[...] = dkpad[B_LEFT:, :].astype(BF16)
            dv_ref[...] = dvpad[B_LEFT:, :].astype(BF16)

    qblk = pl.BlockSpec((BQ, 128), lambda p, b, i: (b * nq + i, p))
    full = pl.BlockSpec((seq, 128), lambda p, b, i: (b, p))
    tblk = pl.BlockSpec((2, BQ, BWIN), lambda p, b, i: (p, 0, 0))
    body, in_specs, args = _after(dep, body, [qblk, full, full, tblk, qblk, qblk, qblk], [q, k, v, table, o, st, do])
    return pl.pallas_call(
        body, name="band_bwd", grid=(3, nb, nq),
        in_specs=in_specs,
        out_specs=[qblk, full, full, tblk],
        out_shape=[jax.ShapeDtypeStruct((t, GW), BF16), jax.ShapeDtypeStruct((t, GW), BF16),
                   jax.ShapeDtypeStruct((t, GW), BF16), jax.ShapeDtypeStruct((6, BQ, BWIN), F32)],
        scratch_shapes=[pltpu.VMEM((seq + B_LEFT, 128), BF16), pltpu.VMEM((seq + B_LEFT, 128), BF16),
                        pltpu.VMEM((seq + B_LEFT, 128), F32), pltpu.VMEM((seq + B_LEFT, 128), F32)],
        compiler_params=_cparams(("arbitrary", "arbitrary", "arbitrary")),
    )(*args)


def _fox_prep(cf, fb, seq):
    nb = cf.shape[0] // seq
    nblk = seq // 128

    def body(cf_ref, fb_ref, f_ref):
        x = cf_ref[...] + fb_ref[...]
        lf = jnp.minimum(x, 0.0) - jnp.log1p(jnp.exp(-jnp.abs(x)))
        rows = lf.T[0:8, :]
        upper = (lax.broadcasted_iota(jnp.int32, (128, 128), 0)
                 <= lax.broadcasted_iota(jnp.int32, (128, 128), 1)).astype(F32)
        carry = jnp.zeros((8, 1), F32)
        for blk in range(nblk):
            sl = slice(blk * 128, (blk + 1) * 128)
            cs = jnp.dot(rows[:, sl], upper, precision=HI, preferred_element_type=F32) + carry
            carry = cs[:, 127:128]
            f_ref[0, 0, :, sl] = cs
            f_ref[0, 1, :, sl] = pltpu.roll(cs, 6, 0)
            f_ref[0, 2, :, sl] = pltpu.roll(cs, 4, 0)

    return pl.pallas_call(
        body, name="fox_prep", grid=(nb,),
        in_specs=[pl.BlockSpec((seq, 128), lambda b: (b, 0)), pl.BlockSpec((1, 128), lambda b: (0, 0))],
        out_specs=pl.BlockSpec((1, 3, 8, seq), lambda b: (b, 0, 0, 0)),
        out_shape=jax.ShapeDtypeStruct((nb, 3, 8, seq), F32),
        compiler_params=_cparams(("arbitrary",)),
    )(cf, fb)


def _fox_prep_bwd(df, dfq, cf, fb, seq):
    nb = cf.shape[0] // seq
    nblk = seq // 128

    def body(df_ref, dfq_ref, cf_ref, fb_ref, dcf_ref, dfb_ref, wide):
        b = pl.program_id(0)
        row = lax.broadcasted_iota(jnp.int32, (8, seq), 0)
        dfh = None
        for p in range(3):
            both = df_ref[0, p] + dfq_ref[:, p * 128:(p + 1) * 128].T[0:8, :]
            both = jnp.where(row < 2, both, 0.0)
            if p:
                both = pltpu.roll(both, 2 * p, 0)
            dfh = both if dfh is None else dfh + both
        lower = (lax.broadcasted_iota(jnp.int32, (128, 128), 0)
                 >= lax.broadcasted_iota(jnp.int32, (128, 128), 1)).astype(F32)
        wide[...] = jnp.zeros_like(wide)
        carry = jnp.zeros((8, 1), F32)
        for blk in reversed(range(nblk)):
            sl = slice(blk * 128, (blk + 1) * 128)
            rc = jnp.dot(dfh[:, sl], lower, precision=HI, preferred_element_type=F32) + carry
            carry = rc[:, 0:1]
            wide[0:8, sl] = rc
        dl = wide[...].T
        x = cf_ref[...] + fb_ref[...]
        dcf = dl * (1.0 / (1.0 + jnp.exp(x)))
        dcf_ref[...] = dcf.astype(BF16)
        part = jnp.sum(dcf, axis=0, keepdims=True)

        @pl.when(b == 0)
        def _():
            dfb_ref[...] = part

        @pl.when(b != 0)
        def _():
            dfb_ref[...] += part

    return pl.pallas_call(
        body, name="fox_prep_bwd", grid=(nb,),
        in_specs=[pl.BlockSpec((1, 3, 8, seq), lambda b: (b, 0, 0, 0)), pl.BlockSpec((seq, GW), lambda b: (b, 0)),
                  pl.BlockSpec((seq, 128), lambda b: (b, 0)), pl.BlockSpec((1, 128), lambda b: (0, 0))],
        out_specs=[pl.BlockSpec((seq, 128), lambda b: (b, 0)), pl.BlockSpec((1, 128), lambda b: (0, 0))],
        out_shape=[jax.ShapeDtypeStruct(cf.shape, BF16), jax.ShapeDtypeStruct((1, 128), F32)],
        scratch_shapes=[pltpu.VMEM((128, seq), F32)],
        compiler_params=_cparams(("arbitrary",)),
    )(df, dfq, cf, fb)


def _gate_out(oa, ob, oc, gates, w, x, gate, seq, tm=ROW_TILE):
    t = x.shape[0]
    tm = min(tm, seq)
    tps = seq // tm

    def body(oa_ref, ob_ref, oc_ref, g_ref, w_ref, x_ref, gt_ref, xo_ref, y_ref, u_ref):
        for n, o_ref in enumerate((oa_ref, ob_ref, oc_ref)):
            sl = slice(n * GW, (n + 1) * GW)
            gv = g_ref[:, sl].astype(F32)
            u_ref[:, sl] = (o_ref[...] * (gv * _sigmoid(gv))).astype(BF16)
        y = jnp.dot(u_ref[...], w_ref[...], preferred_element_type=F32)
        y_ref[...] = y.astype(BF16)
        xo_ref[...] = x_ref[...] + gt_ref[0] * y

    row = lambda wd: pl.BlockSpec((tm, wd), lambda i: (i, 0))
    return pl.pallas_call(
        body, name="gate_out", grid=(t // tm,),
        in_specs=[row(GW), row(GW), row(GW), row(U_PAD), pl.BlockSpec((U_PAD, D_MODEL), lambda i: (0, 0)),
                  row(D_MODEL), pl.BlockSpec((1, 1, D_MODEL), lambda i: (i // tps, 0, 0))],
        out_specs=[row(D_MODEL), row(D_MODEL), row(U_PAD)],
        out_shape=[jax.ShapeDtypeStruct((t, D_MODEL), F32), jax.ShapeDtypeStruct((t, D_MODEL), BF16),
                   jax.ShapeDtypeStruct((t, U_PAD), BF16)],
        compiler_params=_cparams(("arbitrary",)),
    )(oa, ob, oc, gates, w, x, gate)


def _gate_out_bwd(dxo, y, gate, oa, ob, oc, gates, w_t, seq, tm=ROW_TILE, dep=None):
    t = dxo.shape[0]
    tm = min(tm, seq)
    tps = seq // tm
    nb = t // seq

    def body(dxo_ref, y_ref, gt_ref, oa_ref, ob_ref, oc_ref, g_ref, wt_ref,
             dy_ref, doa_ref, dob_ref, doc_ref, dg_ref, dgt_ref):
        i = pl.program_id(0)
        dxo_v = dxo_ref[...]
        dgt = jnp.sum(dxo_v * y_ref[...].astype(F32), axis=0, keepdims=True)
        dyb = (dxo_v * gt_ref[0]).astype(BF16)
        dy_ref[...] = dyb
        du = _nt(dyb, wt_ref[...])
        for n, (o_ref, do_ref) in enumerate(((oa_ref, doa_ref), (ob_ref, dob_ref), (oc_ref, doc_ref))):
            sl = slice(n * GW, (n + 1) * GW)
            gv = g_ref[:, sl].astype(F32)
            sg = _sigmoid(gv)
            dun = du[:, sl]
            do_ref[...] = dun * (gv * sg)
            dg_ref[:, sl] = (dun * o_ref[...] * (sg * (1.0 + gv * (1.0 - sg)))).astype(BF16)

        @pl.when(i % tps == 0)
        def _():
            dgt_ref[0] = dgt

        @pl.when(i % tps != 0)
        def _():
            dgt_ref[0] += dgt

    row = lambda wd: pl.BlockSpec((tm, wd), lambda i: (i, 0))
    per_b = pl.BlockSpec((1, 1, D_MODEL), lambda i: (i // tps, 0, 0))
    in_specs = [row(D_MODEL), row(D_MODEL), per_b, row(GW), row(GW), row(GW), row(U_PAD),
                pl.BlockSpec((U_PAD, D_MODEL), lambda i: (0, 0))]
    body, in_specs, args = _after(dep, body, in_specs, [dxo, y, gate, oa, ob, oc, gates, w_t])
    return pl.pallas_call(
        body, name="gate_out_bwd", grid=(t // tm,), in_specs=in_specs,
        out_specs=[row(D_MODEL), row(GW), row(GW), row(GW), row(U_PAD), per_b],
        out_shape=[jax.ShapeDtypeStruct((t, D_MODEL), BF16), jax.ShapeDtypeStruct((t, GW), F32),
                   jax.ShapeDtypeStruct((t, GW), F32), jax.ShapeDtypeStruct((t, GW), F32),
                   jax.ShapeDtypeStruct((t, U_PAD), BF16), jax.ShapeDtypeStruct((nb, 1, D_MODEL), F32)],
        compiler_params=_cparams(("arbitrary",)),
    )(*args)


def _final_loss(x, target, g, tm=ROW_TILE):
    t = x.shape[0]
    tm = min(tm, t)

    def body(x_ref, t_ref, g_ref, dx_ref, loss_ref, dg_ref):
        i = pl.program_id(0)
        xv = x_ref[...]
        rstd = lax.rsqrt(jnp.mean(xv * xv, axis=-1, keepdims=True) + EPS)
        xn = xv * rstd
        gv = g_ref[...]
        err = xn * gv - t_ref[...]
        dy = err * (1.0 / D_MODEL)
        dxn = dy * gv
        dx_ref[...] = rstd * (dxn - xn * jnp.mean(dxn * xn, axis=-1, keepdims=True))
        lp = jnp.sum(err * err, axis=0, keepdims=True) * (0.5 / D_MODEL)
        dgp = jnp.sum(dy * xn, axis=0, keepdims=True)

        @pl.when(i == 0)
        def _():
            loss_ref[...] = lp
            dg_ref[...] = dgp

        @pl.when(i != 0)
        def _():
            loss_ref[...] += lp
            dg_ref[...] += dgp

    row = pl.BlockSpec((tm, D_MODEL), lambda i: (i, 0))
    vec = pl.BlockSpec((1, D_MODEL), lambda i: (0, 0))
    return pl.pallas_call(
        body, name="final_loss", grid=(t // tm,),
        in_specs=[row, row, vec], out_specs=[row, vec, vec],
        out_shape=[jax.ShapeDtypeStruct((t, D_MODEL), F32), jax.ShapeDtypeStruct((1, D_MODEL), F32),
                   jax.ShapeDtypeStruct((1, D_MODEL), F32)],
        compiler_params=_cparams(("arbitrary",)),
    )(x, target, g)


def _adamw(w, gslots, m, v, name, tr=None):
    nl, r, c = w.shape
    ns = gslots.shape[0]
    tr = r if tr is None else tr

    def body(w_ref, g_ref, m_ref, v_ref, go_ref, d_ref, mo_ref, vo_ref):
        g = g_ref[0].astype(F32)
        for j in range(1, ns):
            g = g + g_ref[j].astype(F32)
        mn = ADAM_B1 * m_ref[...] + (1.0 - ADAM_B1) * g
        vn = ADAM_B2 * v_ref[...] + (1.0 - ADAM_B2) * jnp.square(g)
        m_hat = mn / (1.0 - ADAM_B1 ** ADAM_STEP)
        v_hat = vn / (1.0 - ADAM_B2 ** ADAM_STEP)
        go_ref[...] = g
        d_ref[...] = -ADAM_LR * (m_hat / (jnp.sqrt(v_hat) + ADAM_EPS) + ADAM_WD * w_ref[...])
        mo_ref[...] = mn
        vo_ref[...] = vn

    blk = pl.BlockSpec((1, tr, c), lambda l, i: (l, i, 0))
    return pl.pallas_call(
        body, name=name, grid=(nl, r // tr),
        in_specs=[blk, pl.BlockSpec((ns, 1, tr, c), lambda l, i: (0, l, i, 0)), blk, blk],
        out_specs=[blk] * 4, out_shape=[jax.ShapeDtypeStruct((nl, r, c), F32)] * 4,
        compiler_params=_cparams(("arbitrary", "arbitrary")),
    )(w, gslots, m, v)


def _rope_tables(positions):
    inv = ROPE_THETA ** (-jnp.arange(0, A_ROPE, 2, dtype=F32) / A_ROPE)
    ang = positions.astype(F32)[:, None] * inv
    cos, sin = jnp.cos(ang), jnp.sin(ang)
    t = positions.shape[0]
    one = jnp.ones((t, 64), F32)
    zero16 = jnp.zeros((t, 16), F32)
    cos_t = jnp.concatenate([one, cos, cos, jnp.ones((t, 32), F32)], axis=1)
    sin_a = jnp.concatenate([jnp.zeros((t, 64), F32), -sin, zero16, jnp.zeros((t, 32), F32)], axis=1)
    sin_b = jnp.concatenate([jnp.zeros((t, 64), F32), zero16, sin, jnp.zeros((t, 32), F32)], axis=1)
    return cos_t, sin_a, sin_b


def _pad_heads(w, real, padded, nheads, axis):
    shp = w.shape[:axis] + (nheads, real) + w.shape[axis + 1:]
    w = w.reshape(shp)
    pad = [(0, 0)] * w.ndim
    pad[axis + 1] = (0, padded - real)
    w = jnp.pad(w, pad)
    return w.reshape(w.shape[:axis] + (nheads * padded,) + w.shape[axis + 2:])


def kernel(x, c, positions, w_ada, b_ada, norm_g, w_in, a_q_norm_g, a_w_uq, a_kv_norm_g, a_w_ukv, b_rel_bias, c_forget_b, w_out, final_g, loss_target, m_w_ada, m_b_ada, m_norm_g, m_w_in, m_a_q_norm_g, m_a_w_uq, m_a_kv_norm_g, m_a_w_ukv, m_b_rel_bias, m_c_forget_b, m_w_out, m_final_g, v_w_ada, v_b_ada, v_norm_g, v_w_in, v_a_q_norm_g, v_a_w_uq, v_a_kv_norm_g, v_a_w_ukv, v_b_rel_bias, v_c_forget_b, v_w_out, v_final_g):
    nb, seq, _ = x.shape
    t = nb * seq
    me = 4 * lax.axis_index("x") + 2 * lax.axis_index("y") + lax.axis_index("c")
    x2 = x.reshape(t, D_MODEL)
    tgt = loss_target.reshape(t, D_MODEL)
    cos_t, sin_a, sin_b = _rope_tables(positions.reshape(t))

    def shards(l):
        return [_pad_runs(w_in[l].astype(BF16), IN_RUNS, N_PAD, 1), w_out[l].astype(BF16),
                a_w_uq[l].astype(BF16), a_w_ukv[l].astype(BF16)]

    def prepare(gi, go, gq, gkv):
        return dict(w_in=gi.reshape(D_MODEL, N_PAD), **prepare_rest(go, gq, gkv))

    def prepare_rest(go, gq, gkv):
        wo = _pad_runs(go.reshape(D_MODEL, D_MODEL), OUT_RUNS, U_PAD, 0)
        wq = jnp.transpose(gq, (1, 0, 2)).reshape(A_Q_RANK, A_HEADS * (A_NOPE + A_ROPE))
        wq = _pad_heads(wq, A_NOPE + A_ROPE, HEAD_PAD, A_HEADS, 1)
        wkv = jnp.transpose(gkv, (1, 0, 2)).reshape(A_KV_RANK, A_HEADS, 2 * A_NOPE)
        wk = jnp.pad(wkv[:, :, :A_NOPE], ((0, 0), (0, 0), (0, HEAD_PAD - A_NOPE))).reshape(A_KV_RANK, A_HEADS * HEAD_PAD)
        wv = wkv[:, :, A_NOPE:].reshape(A_KV_RANK, GW)
        return dict(w_out=wo, wuq=wq, wuq_t=wq.T, wk=wk, wk_t=wk.T, wv=wv, wv_t=wv.T)

    shards0 = shards(0)
    w_in0_g, c_g = _gather([shards0[0], c], "gather_w_in0")
    c_all = c_g.reshape(N_DEV * nb, D_MODEL)
    weights = [dict(w_in=w_in0_g.reshape(D_MODEL, N_PAD)), None]

    c_act, mod_cols = _ada_fwd(c_all, w_ada)
    (mod_g,) = _gather([mod_cols], "gather_mod")
    rest0, rest0_token = _split_start("gather", shards0[1:], "gather_rest0_start", after=mod_g)
    mod_all = jnp.transpose(mod_g, (1, 2, 0, 3)).reshape(DEPTH, N_DEV * nb, 3 * D_MODEL)
    mod = lax.dynamic_slice_in_dim(mod_all, me * nb, nb, axis=1) + b_ada[:, None, :]

    fb_pad = jnp.pad(c_forget_b, ((0, 0), (0, 128 - C_HEADS)))
    a_scale = (A_NOPE + A_ROPE) ** -0.5
    h_scale = CHUNK ** -0.5

    saved = []
    xl = x2
    for l in range(DEPTH):
        if l == 1:
            weights[1] = prepare(*_split_wait(gather1, xl, "gather_weights1_wait")[1])
        w = weights[l]
        shift, scale, gate = mod[l, :, :D_MODEL], mod[l, :, D_MODEL:2 * D_MODEL], mod[l, :, 2 * D_MODEL:]
        ss = jnp.stack([shift, 1.0 + scale], axis=1)
        gate3 = gate[:, None, :]
        h, cq, ckv, kpe, gates, bq, bk, bv, cq2, ck, cv, cf = _ln_in(
            xl, ss, norm_g[l:l + 1], w["w_in"], seq, dep=rest0_token if l == 0 else None)
        gather1_token = None
        if l == 0:
            w.update(prepare_rest(*_split_wait(rest0, h, "gather_rest0_wait")[1]))
            gather1, gather1_token = _split_start("gather", shards(1), "gather_weights1_start", after=w["w_out"])
        q, k, v, cqn, ckvn = _mla_prep(cq, ckv, kpe, a_q_norm_g[l:l + 1], a_kv_norm_g[l:l + 1],
                                       w["wuq"], w["wk"], w["wv"], cos_t, sin_a, sin_b, dep=gather1_token)
        oa, sta = _attn_fwd("mla", q, k, v, None, seq, a_scale)
        table = _band_table(jnp.pad(b_rel_bias[l], ((0, 8 - B_HEADS), (0, GW - N_REL))))
        ob, stb = _band_fwd(bq, bk, bv, table, seq, h_scale)
        fcum = _fox_prep(cf, fb_pad[l:l + 1], seq)
        oc, stc = _attn_fwd("fox", cq2, ck, cv, fcum, seq, h_scale)
        xn, y, u = _gate_out(oa, ob, oc, gates, w["w_out"], xl, gate3, seq)
        saved.append(dict(x=xl, ss=ss, gate3=gate3, h=h, cq=cq, ckv=ckv, gates=gates, bq=bq, bk=bk, bv=bv,
                          cq2=cq2, ck=ck, cv=cv, cf=cf, q=q, k=k, v=v, cqn=cqn, ckvn=ckvn, oa=oa, sta=sta,
                          table=table, ob=ob, stb=stb, fcum=fcum, oc=oc, stc=stc, y=y, u=u))
        xl = xn

    dx, loss_lanes, g_final = _final_loss(xl, tgt, final_g[None, :])
    loss = lax.psum(jnp.sum(loss_lanes), AXES)

    rows = D_MODEL // N_DEV
    core = lax.axis_index("c").astype(jnp.int32).reshape(1)
    n_seg_a = 4
    dmods, smalls, parts = [None] * DEPTH, [None] * DEPTH, [None] * DEPTH
    pair1 = chips1 = pair1_token = chips1_token = None
    for l in reversed(range(DEPTH)):
        s, w = saved[l], weights[l]
        dy, doa, dob, doc, dgates, dgate = _gate_out_bwd(dx, s["y"], s["gate3"], s["oa"], s["ob"], s["oc"],
                                                         s["gates"], w["w_out"], seq, dep=pair1_token)
        g_out = _unpad_runs(_matmul_tn(s["u"], dy, "dw_out"), OUT_RUNS, 0)
        if l == 0:
            own, from_sib = _split_wait(pair1, g_out, "grads1_pair_wait")
            chips1, chips1_token = _split_start("chips", _pair_add(core, own, from_sib, "grads1_add"), "grads1_chips_start")
        dq, dk, dv = _attn_bwd("mla", s["q"], s["k"], s["v"], None, s["oa"], s["sta"], doa, seq, a_scale,
                               dep=chips1_token)
        dbq, dbk, dbv, gtab = _band_bwd(s["bq"], s["bk"], s["bv"], s["table"], s["ob"], s["stb"], dob, seq, h_scale,
                                        dep=chips1_token)
        g_rel = _band_table_bwd(gtab)[:, 0, :N_REL]
        dcq2, dck, dcv, dfc, dfq = _attn_bwd("fox", s["cq2"], s["ck"], s["cv"], s["fcum"], s["oc"], s["stc"], doc,
                                             seq, h_scale, dep=chips1_token)
        dcf, dfb = _fox_prep_bwd(dfc, dfq, s["cf"], fb_pad[l:l + 1], seq)
        dcq, dckv, dkpe, dqlin, dklin, dgq, dgkv = _mla_prep_bwd(
            dq, dk, dv, s["cq"], s["ckv"], a_q_norm_g[l:l + 1], a_kv_norm_g[l:l + 1],
            w["wuq_t"], w["wk_t"], w["wv_t"], cos_t, sin_a, sin_b)
        gq_pad = _matmul_tn(s["cqn"], dqlin, "dw_uq")
        g_uq = gq_pad.reshape(A_Q_RANK, A_HEADS, HEAD_PAD)[:, :, :A_NOPE + A_ROPE].reshape(A_Q_RANK, -1)
        gkv_pad = _matmul_tn(s["ckvn"], [dklin, dv], "dw_ukv")
        gk_pad = gkv_pad[:, :A_HEADS * HEAD_PAD].reshape(A_KV_RANK, A_HEADS, HEAD_PAD)[:, :, :A_NOPE]
        gv_pad = gkv_pad[:, A_HEADS * HEAD_PAD:].reshape(A_KV_RANK, A_HEADS, A_NOPE)
        g_ukv = jnp.concatenate([gk_pad, gv_pad], axis=2).reshape(A_KV_RANK, -1)
        dz = [dcq, dckv, dkpe, dgates, dbq, dbk, dbv, dcq2, dck, dcv, dcf]
        g_in_a = _matmul_tn(s["h"], dz[:n_seg_a], "dw_in_a")
        first = [g_in_a.reshape(N_DEV, rows, -1), g_out.reshape(N_DEV, rows, D_MODEL),
                 g_uq.reshape(A_Q_RANK, N_DEV, -1).transpose(1, 0, 2), g_ukv.reshape(A_KV_RANK, N_DEV, -1).transpose(1, 0, 2)]
        if l == 1:
            g_in_b = _matmul_tn(s["h"], dz[n_seg_a:], "dw_in_b")
            pair1, pair1_token = _split_start("pair", first + [g_in_b.reshape(N_DEV, rows, -1)], "grads1_pair_start")
            tail_token = None
        else:
            pair0a, pair0a_token = _split_start("pair", first, "grads0a_pair_start")
            g_in_b = _matmul_tn(s["h"], dz[n_seg_a:], "dw_in_b", dep=pair0a_token)
            own, from_sib = _split_wait(pair0a, g_in_b, "grads0a_pair_wait")
            sums0a = _pair_add(core, own, from_sib, "grads0a_add")
            pair0b, pair0b_token = _split_start("pair", [g_in_b.reshape(N_DEV, rows, -1)], "grads0b_pair_start",
                                                after=sums0a[0])
            chips0a, tail_token = _split_start("chips", sums0a, "grads0a_chips_start", after=pair0b_token)
        dx, dss, dg_norm = _ln_in_bwd(dz, w["w_in"], s["x"], s["ss"], norm_g[l:l + 1], dx, seq, dep=tail_token)
        dmods[l] = jnp.concatenate([dss[:, 0, :], dss[:, 1, :], dgate[:, 0, :]], axis=1)
        smalls[l] = [dg_norm.reshape(-1), dgq.reshape(-1), dgkv.reshape(-1), g_rel.reshape(-1),
                     dfb[0, :C_HEADS]]
    grad_x = dx.reshape(nb, seq, D_MODEL)
    parts[1] = _split_wait(chips1, dx, "grads1_chips_wait")[1]
    parts0a = _split_wait(chips0a, dx, "grads0a_chips_wait")[1]
    own, from_sib = _split_wait(pair0b, dx, "grads0b_pair_wait")

    small = jnp.concatenate([p for l in range(DEPTH) for p in smalls[l]] + [g_final.reshape(-1)])
    n_small = small.shape[0]
    small_rows = -(-n_small // 1024) * 8
    small = jnp.pad(small, (0, small_rows * 128 - n_small)).reshape(small_rows, 128)
    dmod_local = jnp.stack(dmods)
    dmod_g, small_g = _gather([dmod_local, small], "gather_small", dep=parts0a[0])
    chips0, chips0_token = _split_start("chips", _pair_add(core, own, from_sib, "grads0b_add"), "grads0b_chips_start",
                                        after=small_g)
    dmod_all = jnp.transpose(dmod_g, (1, 0, 2, 3)).reshape(DEPTH, N_DEV * nb, 3 * D_MODEL)
    cols = 3 * D_MODEL // N_DEV
    dmod_mine = lax.dynamic_slice_in_dim(dmod_all, me * cols, cols, axis=2)
    g_w_ada, g_b_ada = _ada_bwd(c_act, dmod_all, dmod_mine, chips0_token)
    small_sum = _sum_slots(small_g, "sum_small").reshape(-1)

    def split_small():
        out, pos = [], 0
        sizes = [D_MODEL, A_Q_RANK, A_KV_RANK, B_HEADS * N_REL, C_HEADS]
        per_layer = []
        for l in range(DEPTH):
            parts = []
            for sz in sizes:
                parts.append(small_sum[pos:pos + sz])
                pos += sz
            per_layer.append(parts)
        for j in range(len(sizes)):
            out.append(jnp.stack([per_layer[l][j] for l in range(DEPTH)]))
        out.append(small_sum[pos:pos + D_MODEL])
        return out

    g_norm, g_qn, g_kvn, g_relb, g_fb, g_fin = split_small()

    def adam(w, g, m, v, name, tr=None):
        shp = w.shape
        w3 = w.reshape((1,) * (3 - w.ndim) + shp)
        outs = _adamw(w3, g.reshape((-1,) + w3.shape), m.reshape(w3.shape), v.reshape(w3.shape), name, tr)
        return [o.reshape(shp) for o in outs]

    res = {
        "w_ada": adam(w_ada, g_w_ada, m_w_ada, v_w_ada, "adam_w_ada", 256),
        "b_ada": adam(b_ada, g_b_ada, m_b_ada, v_b_ada, "adam_b_ada"),
        "norm_g": adam(norm_g, g_norm, m_norm_g, v_norm_g, "adam_norm_g"),
        "a_q_norm_g": adam(a_q_norm_g, g_qn, m_a_q_norm_g, v_a_q_norm_g, "adam_q_norm"),
        "a_kv_norm_g": adam(a_kv_norm_g, g_kvn, m_a_kv_norm_g, v_a_kv_norm_g, "adam_kv_norm"),
        "b_rel_bias": adam(b_rel_bias, g_relb.reshape(b_rel_bias.shape), m_b_rel_bias, v_b_rel_bias, "adam_rel_bias"),
        "c_forget_b": adam(c_forget_b, g_fb, m_c_forget_b, v_c_forget_b, "adam_forget_b"),
        "final_g": adam(final_g, g_fin, m_final_g, v_final_g, "adam_final_g"),
    }
    parts[0] = list(parts0a) + list(_split_wait(chips0, res["w_ada"][1], "grads0b_chips_wait")[1])
    p_in = jnp.stack([_unpad_runs(jnp.concatenate([parts[l][0], parts[l][4]], axis=2), IN_RUNS, 2)
                      for l in range(DEPTH)], axis=1)
    p_out, p_uq, p_ukv = (jnp.stack([parts[l][j] for l in range(DEPTH)], axis=1) for j in (1, 2, 3))
    res.update({
        "w_in": adam(w_in, p_in, m_w_in, v_w_in, "adam_w_in", 64),
        "a_w_uq": adam(a_w_uq, p_uq, m_a_w_uq, v_a_w_uq, "adam_w_uq"),
        "a_w_ukv": adam(a_w_ukv, p_ukv, m_a_w_ukv, v_a_w_ukv, "adam_w_ukv"),
        "w_out": adam(w_out, p_out, m_w_out, v_w_out, "adam_w_out", 64),
    })
    names = ["w_ada", "b_ada", "norm_g", "w_in", "a_q_norm_g", "a_w_uq", "a_kv_norm_g", "a_w_ukv", "b_rel_bias",
             "c_forget_b", "w_out", "final_g"]
    outs = [loss, grad_x]
    for j in range(4):
        outs += [res[n][j] for n in names]
    return tuple(outs)
```

```python
import math

import jax
import jax.numpy as jnp
from jax import lax
from jax.experimental import pallas as pl
from jax.experimental.pallas import tpu as pltpu

F32 = jnp.float32
BF16 = jnp.bfloat16
HI = lax.Precision.HIGHEST

N_DEV = 8
AXES = ("x", "y", "c")
D_MODEL = 1024
DEPTH = 2
CHUNK = 64
EPS = 1e-6
NEG = -1e30
A_HEADS = 6
A_NOPE = 64
A_ROPE = 32
A_Q_RANK = 384
A_KV_RANK = 256
ROPE_THETA = 10000.0
B_HEADS = 5
B_LEFT = 512
REL_CLIP = 128
N_REL = 2 * REL_CLIP + 1
C_HEADS = 5
HEAD_PAD = 128
GW = 384
N_IN = 3621
ADAM_LR = 0.001
ADAM_B1 = 0.9
ADAM_B2 = 0.999
ADAM_EPS = 1e-08
ADAM_WD = 0.01
ADAM_STEP = 10
VMEM_LIMIT = 56 * 1024 * 1024
ROW_TILE = 512

Z_SEGS = (
    ("cq", 0, 384, F32), ("ckv", 384, 256, F32), ("kpe", 640, 128, F32), ("gates", 768, 1152, BF16),
    ("bq", 1920, 384, BF16), ("bk", 2304, 384, BF16), ("bv", 2688, 384, BF16),
    ("cq2", 3072, 384, BF16), ("ck", 3456, 384, BF16), ("cv", 3840, 384, BF16), ("cf", 4224, 128, F32),
)
N_PAD = 4352
IN_RUNS = (
    (0, 384, 0), (384, 256, 384), (640 + 64, 32, 640),
    (768, 384, 672), (768 + 384, 320, 2016), (768 + 768, 320, 3301),
    (1920, 320, 1056), (2304, 320, 1376), (2688, 320, 1696),
    (3072, 320, 2336), (3456, 320, 2656), (3840, 320, 2976), (4224, 5, 3296),
)
OUT_RUNS = ((0, 384, 0), (384, 320, 384), (768, 320, 704))
U_PAD = 1152


def _cparams(sem=None, vmem=VMEM_LIMIT):
    return pltpu.CompilerParams(dimension_semantics=sem, vmem_limit_bytes=vmem)


def _after(dep, body, in_specs, args):
    if dep is None:
        return body, in_specs, args
    n = len(args)

    def ordered(*refs):
        return body(*refs[:n], *refs[n + 1:])

    return ordered, list(in_specs) + [pl.BlockSpec((8, 128), lambda *_: (0, 0))], list(args) + [dep]


def _pad_runs(w, runs, total, axis):
    order = sorted(runs)
    parts, pos = [], 0
    for off, wd, src in order:
        if off > pos:
            shp = list(w.shape)
            shp[axis] = off - pos
            parts.append(jnp.zeros(shp, w.dtype))
        parts.append(lax.slice_in_dim(w, src, src + wd, axis=axis))
        pos = off + wd
    if pos < total:
        shp = list(w.shape)
        shp[axis] = total - pos
        parts.append(jnp.zeros(shp, w.dtype))
    return jnp.concatenate(parts, axis=axis)


def _unpad_runs(w, runs, axis):
    order = sorted(runs, key=lambda r: r[2])
    return jnp.concatenate([lax.slice_in_dim(w, off, off + wd, axis=axis) for off, wd, _ in order], axis=axis)


def _sigmoid(x):
    return 1.0 / (1.0 + jnp.exp(-x))


N_CHIP = 4
ANY_SPEC = pl.BlockSpec(memory_space=pl.ANY)
MESH_ID = pl.DeviceIdType.MESH


def _gather(arrs, name, dep=None):
    n = len(arrs)
    nin = n + (dep is not None)

    def body(*refs):
        ins, outs = refs[:n], refs[nin:nin + n]
        send_sems, recv_sems, local_sems = refs[nin + n:]
        x, y, c = lax.axis_index("x"), lax.axis_index("y"), lax.axis_index("c")
        me, sib = (x, y, c), (x, y, 1 - c)
        chips = [(1 - x, y), (x, 1 - y), (1 - x, 1 - y)]

        def slot(px, py, pc):
            return 4 * px + 2 * py + pc

        def copy(a, k, block, to, src=None):
            dst = outs[a].at[slot(*block)]
            return pltpu.make_async_remote_copy(
                src_ref=dst if src is None else src, dst_ref=dst, send_sem=send_sems.at[a, k],
                recv_sem=recv_sems.at[a, k], device_id=to, device_id_type=MESH_ID)

        local = [pltpu.make_async_copy(ins[a], outs[a].at[slot(*me)], local_sems.at[a]) for a in range(n)]
        first = []
        for a in range(n):
            first.append(copy(a, 0, me, sib, src=ins[a]))
            first += [copy(a, 1 + j, me, (*chip, c), src=ins[a]) for j, chip in enumerate(chips)]
        for cp in local + first:
            cp.start()
        passed = []
        for j, chip in enumerate(chips):
            for a in range(n):
                copy(a, 1 + j, (*chip, c), me).wait_recv()
                fwd = copy(a, 4 + j, (*chip, c), sib)
                fwd.start()
                passed.append(fwd)
        for a in range(n):
            copy(a, 0, sib, me).wait_recv()
            for j, chip in enumerate(chips):
                copy(a, 4 + j, (*chip, 1 - c), me).wait_recv()
        for cp in first + passed:
            cp.wait_send()
        for cp in local:
            cp.wait()

    return pl.pallas_call(
        body, name=name, out_shape=[jax.ShapeDtypeStruct((N_DEV,) + a.shape, a.dtype) for a in arrs],
        in_specs=[ANY_SPEC] * nin, out_specs=[ANY_SPEC] * n,
        scratch_shapes=[pltpu.SemaphoreType.DMA((n, N_DEV - 1)), pltpu.SemaphoreType.DMA((n, N_DEV - 1)),
                        pltpu.SemaphoreType.DMA((n,))],
    )(*arrs, *([] if dep is None else [dep]))


HBM_SPEC = pl.BlockSpec(memory_space=pltpu.HBM)
SEM_SPEC = pl.BlockSpec(memory_space=pltpu.SEMAPHORE)
SPLIT_EFFECT = pltpu.SideEffectType.DATAFLOW_SIDE_EFFECTING
SPLIT_SEMS = {"gather": (N_DEV - 1, True), "pair": (N_CHIP, False), "chips": (N_CHIP - 1, True)}


def _split_descriptors(pattern, srcs, lands, sems):
    x, y, c = lax.axis_index("x"), lax.axis_index("y"), lax.axis_index("c")
    nsem, has_local = SPLIT_SEMS[pattern]
    per = 2 * nsem + int(has_local)
    starts, arrivals, local = [], [], []

    def remote(a, k, src, dst, to):
        return pltpu.make_async_remote_copy(src_ref=src, dst_ref=dst, send_sem=sems[a * per + k],
                                            recv_sem=sems[a * per + nsem + k], device_id=to, device_id_type=MESH_ID)

    for a in range(len(srcs)):
        if pattern == "gather":
            me = 4 * x + 2 * y + c
            local.append(pltpu.make_async_copy(srcs[a], lands[a].at[me], sems[a * per + 2 * nsem]))
            for k in range(1, N_DEV):
                px = (1 - x) if (k >> 2) & 1 else x
                py = (1 - y) if (k >> 1) & 1 else y
                pc = (1 - c) if k & 1 else c
                starts.append(remote(a, k - 1, srcs[a], lands[a].at[me], (px, py, pc)))
                arrivals.append(remote(a, k - 1, srcs[a], lands[a].at[4 * px + 2 * py + pc], (px, py, pc)))
        elif pattern == "pair":
            for q in range(N_CHIP):
                cp = remote(a, q, srcs[a].at[2 * q + 1 - c], lands[a].at[q], (x, y, 1 - c))
                starts.append(cp)
                arrivals.append(cp)
        else:
            mine = 2 * x + y
            local.append(pltpu.make_async_copy(srcs[a].at[mine], lands[a].at[mine], sems[a * per + 2 * nsem]))
            for k in range(1, N_CHIP):
                px = (1 - x) if (k >> 1) & 1 else x
                py = (1 - y) if k & 1 else y
                starts.append(remote(a, k - 1, srcs[a].at[2 * px + py], lands[a].at[mine], (px, py, c)))
                arrivals.append(remote(a, k - 1, srcs[a].at[2 * px + py], lands[a].at[2 * px + py], (px, py, c)))
    return starts, arrivals, local


def _split_start(pattern, arrs, name, after=None):
    n = len(arrs)
    extra = [] if after is None else [after]
    nsem, has_local = SPLIT_SEMS[pattern]
    if pattern == "gather":
        land_shapes = [(N_DEV,) + a.shape for a in arrs]
    elif pattern == "pair":
        land_shapes = [(N_CHIP,) + a.shape[1:] for a in arrs]
    else:
        land_shapes = [a.shape for a in arrs]
    nsem_out = n * (2 * nsem + int(has_local))

    def body(*refs):
        srcs, lands = refs[:n], refs[n:2 * n]
        first_sem = 2 * n + len(extra)
        sems = refs[first_sem:first_sem + nsem_out]
        token = refs[-1]
        starts, _, local = _split_descriptors(pattern, srcs, lands, sems)
        for cp in local + starts:
            cp.start()
        token[...] = jnp.zeros_like(token)

    out_shape = ([pltpu.SemaphoreType.DMA(())] * nsem_out + [pltpu.HBM(a.shape, a.dtype) for a in arrs]
                 + [pltpu.HBM(s, a.dtype) for s, a in zip(land_shapes, arrs)] + [jax.ShapeDtypeStruct((8, 128), F32)])
    ins = ([pltpu.with_memory_space_constraint(a, pltpu.HBM) for a in arrs]
           + [pltpu.with_memory_space_constraint(lax.empty(s, a.dtype), pltpu.HBM) for s, a in zip(land_shapes, arrs)])
    outs = pl.pallas_call(
        body, name=name, out_shape=out_shape, in_specs=[HBM_SPEC] * (2 * n) + [ANY_SPEC] * len(extra),
        out_specs=[SEM_SPEC] * nsem_out + [HBM_SPEC] * (2 * n) + [pl.BlockSpec(memory_space=pltpu.VMEM)],
        input_output_aliases={i: nsem_out + i for i in range(2 * n)},
        compiler_params=pltpu.CompilerParams(has_side_effects=SPLIT_EFFECT),
    )(*ins, *extra)
    handle = dict(pattern=pattern, n=n, sems=outs[:nsem_out], srcs=outs[nsem_out:nsem_out + n],
                  lands=outs[nsem_out + n:nsem_out + 2 * n])
    return handle, outs[-1]


def _split_wait(handle, after, name):
    pattern, n = handle["pattern"], handle["n"]
    nsem_in = len(handle["sems"])

    def body(*refs):
        srcs, lands = refs[:n], refs[n:2 * n]
        starts, arrivals, local = _split_descriptors(pattern, srcs, lands, refs[2 * n:2 * n + nsem_in])
        for cp in starts:
            cp.wait_send()
        for cp in arrivals:
            cp.wait_recv()
        for cp in local:
            cp.wait()

    srcs, lands = handle["srcs"], handle["lands"]
    outs = pl.pallas_call(
        body, name=name,
        out_shape=[pltpu.HBM(a.shape, a.dtype) for a in srcs] + [pltpu.HBM(a.shape, a.dtype) for a in lands],
        in_specs=[HBM_SPEC] * (2 * n) + [SEM_SPEC] * nsem_in + [ANY_SPEC], out_specs=[HBM_SPEC] * (2 * n),
        input_output_aliases={i: i for i in range(2 * n)},
        compiler_params=pltpu.CompilerParams(has_side_effects=SPLIT_EFFECT),
    )(*srcs, *lands, *handle["sems"], after)
    return outs[:n], outs[n:]


def _pair_add(core, a8s, b4s, name):
    n = len(a8s)

    def body(core_ref, *refs):
        for i in range(n):
            refs[2 * n + i][...] = (refs[i][...] + refs[n + i][...]).astype(BF16)

    own = [pl.BlockSpec((1,) + b.shape[1:], lambda q, core_ref: (2 * q + core_ref[0], 0, 0)) for b in b4s]
    slot = [pl.BlockSpec((1,) + b.shape[1:], lambda q, core_ref: (q, 0, 0)) for b in b4s]
    grid_spec = pltpu.PrefetchScalarGridSpec(num_scalar_prefetch=1, grid=(N_CHIP,), in_specs=own + slot, out_specs=slot)
    return pl.pallas_call(
        body, name=name, grid_spec=grid_spec, out_shape=[jax.ShapeDtypeStruct(b.shape, BF16) for b in b4s],
        compiler_params=_cparams(("arbitrary",)),
    )(core, *a8s, *b4s)


def _sum_slots(x, name):
    _, r, c = x.shape

    def body(x_ref, o_ref):
        acc = x_ref[0]
        for j in range(1, N_DEV):
            acc = acc + x_ref[j]
        o_ref[...] = acc

    return pl.pallas_call(body, name=name, out_shape=jax.ShapeDtypeStruct((r, c), F32))(x)


def _ada_fwd(c_all, w_ada):
    nb = c_all.shape[0]
    cols = w_ada.shape[2]

    def body(c_ref, w_ref, act_ref, mod_ref):
        cv = c_ref[...]
        act = cv * _sigmoid(cv)
        act_ref[...] = act
        for l in range(DEPTH):
            mod_ref[l] = jnp.dot(act, w_ref[l], precision=HI, preferred_element_type=F32)

    return pl.pallas_call(
        body, name="ada_fwd",
        out_shape=[jax.ShapeDtypeStruct((nb, D_MODEL), F32), jax.ShapeDtypeStruct((DEPTH, nb, cols), F32)],
        compiler_params=_cparams(),
    )(c_all, w_ada)


def _ada_bwd(c_act, dmod_all, dmod_mine, dep):
    nb = c_act.shape[0]
    cols = dmod_mine.shape[2]

    def body(act_ref, dall_ref, dmine_ref, dep_ref, gw_ref, gb_ref):
        act = act_ref[...]
        for l in range(DEPTH):
            gw_ref[l] = lax.dot_general(act, dmine_ref[l], (((0,), (0,)), ((), ())),
                                        precision=HI, preferred_element_type=F32)
            gb_ref[l:l + 1, :] = jnp.sum(dall_ref[l], axis=0, keepdims=True)

    return pl.pallas_call(
        body, name="ada_bwd",
        out_shape=[jax.ShapeDtypeStruct((DEPTH, D_MODEL, cols), F32),
                   jax.ShapeDtypeStruct((DEPTH, 3 * D_MODEL), F32)],
        compiler_params=_cparams(),
    )(c_act, dmod_all, dmod_mine, dep)


def _ln_in(x, ss, g, w, seq, tm=ROW_TILE, dep=None):
    t = x.shape[0]
    tm = min(tm, seq)
    tps = seq // tm

    def body(x_ref, ss_ref, g_ref, w_ref, h_ref, *outs):
        xv = x_ref[...]
        xn = xv * lax.rsqrt(jnp.mean(xv * xv, axis=-1, keepdims=True) + EPS)
        h = xn * g_ref[...] * ss_ref[0, 1:2, :] + ss_ref[0, 0:1, :]
        hb = h.astype(BF16)
        h_ref[...] = hb
        z = jnp.dot(hb, w_ref[...], preferred_element_type=F32)
        for o_ref, (_, off, wd, _) in zip(outs, Z_SEGS):
            o_ref[...] = z[:, off:off + wd].astype(o_ref.dtype)

    row = lambda wd: pl.BlockSpec((tm, wd), lambda i: (i, 0))
    in_specs = [row(D_MODEL), pl.BlockSpec((1, 2, D_MODEL), lambda i: (i // tps, 0, 0)),
                pl.BlockSpec((1, D_MODEL), lambda i: (0, 0)), pl.BlockSpec((D_MODEL, N_PAD), lambda i: (0, 0))]
    body, in_specs, args = _after(dep, body, in_specs, [x, ss, g, w])
    return pl.pallas_call(
        body, name="ln_in", grid=(t // tm,), in_specs=in_specs,
        out_specs=[row(D_MODEL)] + [row(wd) for _, _, wd, _ in Z_SEGS],
        out_shape=[jax.ShapeDtypeStruct((t, D_MODEL), BF16)]
        + [jax.ShapeDtypeStruct((t, wd), dt) for _, _, wd, dt in Z_SEGS],
        compiler_params=_cparams(("arbitrary",)),
    )(*args)


def _ln_in_bwd(dz, w_t, x, ss, g, dxo, seq, tm=ROW_TILE, dep=None):
    t = x.shape[0]
    tm = min(tm, seq)
    tps = seq // tm
    nb = t // seq
    nz = len(Z_SEGS)

    def body(*refs):
        dz_refs = refs[:nz]
        wt_ref, x_ref, ss_ref, g_ref, dxo_ref, dx_ref, dss_ref, dg_ref = refs[nz:]
        i = pl.program_id(0)
        dzc = jnp.concatenate([r[...].astype(BF16) for r in dz_refs], axis=1)
        dh = _nt(dzc, wt_ref[...])
        xv = x_ref[...]
        rstd = lax.rsqrt(jnp.mean(xv * xv, axis=-1, keepdims=True) + EPS)
        xn = xv * rstd
        gv = g_ref[...]
        s1 = ss_ref[0, 1:2, :]
        dxg = dh * s1
        dxn = dxg * gv
        dx = rstd * (dxn - xn * jnp.mean(dxn * xn, axis=-1, keepdims=True))
        dx_ref[...] = dxo_ref[...] + dx
        dshift = jnp.sum(dh, axis=0, keepdims=True)
        dscale = jnp.sum(dh * (xn * gv), axis=0, keepdims=True)
        dgp = jnp.sum(dxg * xn, axis=0, keepdims=True)

        @pl.when(i % tps == 0)
        def _():
            dss_ref[0, 0:1, :] = dshift
            dss_ref[0, 1:2, :] = dscale

        @pl.when(i % tps != 0)
        def _():
            dss_ref[0, 0:1, :] += dshift
            dss_ref[0, 1:2, :] += dscale

        @pl.when(i == 0)
        def _():
            dg_ref[...] = dgp

        @pl.when(i != 0)
        def _():
            dg_ref[...] += dgp

    row = lambda wd: pl.BlockSpec((tm, wd), lambda i: (i, 0))
    in_specs = ([row(wd) for _, _, wd, _ in Z_SEGS]
                + [pl.BlockSpec((D_MODEL, N_PAD), lambda i: (0, 0)), row(D_MODEL),
                   pl.BlockSpec((1, 2, D_MODEL), lambda i: (i // tps, 0, 0)),
                   pl.BlockSpec((1, D_MODEL), lambda i: (0, 0)), row(D_MODEL)])
    body, in_specs, args = _after(dep, body, in_specs, [*dz, w_t, x, ss, g, dxo])
    return pl.pallas_call(
        body, name="ln_in_bwd", grid=(t // tm,), in_specs=in_specs,
        out_specs=[row(D_MODEL), pl.BlockSpec((1, 2, D_MODEL), lambda i: (i // tps, 0, 0)),
                   pl.BlockSpec((1, D_MODEL), lambda i: (0, 0))],
        out_shape=[jax.ShapeDtypeStruct((t, D_MODEL), F32), jax.ShapeDtypeStruct((nb, 2, D_MODEL), F32),
                   jax.ShapeDtypeStruct((1, D_MODEL), F32)],
        compiler_params=_cparams(("arbitrary",)),
    )(*args)


def _matmul_tn(a, bs, name, tm=2048, dep=None):
    bs = list(bs) if isinstance(bs, (list, tuple)) else [bs]
    t, k = a.shape
    widths = [b.shape[1] for b in bs]
    n = sum(widths)
    tm = min(tm, t)

    def body(a_ref, *refs):
        b_refs, o_ref = refs[:-1], refs[-1]
        i = pl.program_id(0)
        av = a_ref[...].astype(BF16)
        parts = [b_ref[...].astype(BF16) for b_ref in b_refs]
        bv = parts[0] if len(parts) == 1 else jnp.concatenate(parts, axis=1)
        part = lax.dot_general(av, bv, (((0,), (0,)), ((), ())), preferred_element_type=F32)

        @pl.when(i == 0)
        def _():
            o_ref[...] = part

        @pl.when(i != 0)
        def _():
            o_ref[...] += part

    in_specs = [pl.BlockSpec((tm, k), lambda i: (i, 0))] + [pl.BlockSpec((tm, wd), lambda i: (i, 0)) for wd in widths]
    body, in_specs, args = _after(dep, body, in_specs, [a, *bs])
    return pl.pallas_call(
        body, name=name, grid=(t // tm,), in_specs=in_specs,
        out_specs=pl.BlockSpec((k, n), lambda i: (0, 0)),
        out_shape=jax.ShapeDtypeStruct((k, n), F32),
        compiler_params=_cparams(("arbitrary",)),
    )(*args)


def _rope(blk, cos_t, sin_a, sin_b):
    return blk * cos_t + pltpu.roll(blk, 112, 1) * sin_a + pltpu.roll(blk, 16, 1) * sin_b


def _unrope(d, cos_t, sin_a, sin_b):
    return d * cos_t + pltpu.roll(d * sin_a, 16, 1) + pltpu.roll(d * sin_b, 112, 1)


def _mla_prep(cq, ckv, kpe, gq, gkv, wuq, wk, wv, cos_t, sin_a, sin_b, tm=ROW_TILE, dep=None):
    t = cq.shape[0]
    tm = min(tm, t)
    qw = A_HEADS * HEAD_PAD

    def body(cq_ref, ckv_ref, kpe_ref, gq_ref, gkv_ref, wuq_ref, wk_ref, wv_ref, c_ref, sa_ref, sb_ref,
             q_ref, k_ref, v_ref, cqn_ref, ckvn_ref):
        ct, sa, sb = c_ref[...], sa_ref[...], sb_ref[...]
        a = cq_ref[...]
        cqn = (a * lax.rsqrt(jnp.mean(a * a, axis=-1, keepdims=True) + EPS) * gq_ref[...]).astype(BF16)
        cqn_ref[...] = cqn
        b = ckv_ref[...]
        ckvn = (b * lax.rsqrt(jnp.mean(b * b, axis=-1, keepdims=True) + EPS) * gkv_ref[...]).astype(BF16)
        ckvn_ref[...] = ckvn
        qlin = jnp.dot(cqn, wuq_ref[...], preferred_element_type=F32)
        klin = jnp.dot(ckvn, wk_ref[...], preferred_element_type=F32)
        v_ref[...] = jnp.dot(ckvn, wv_ref[...], preferred_element_type=F32).astype(BF16)
        kr = _rope(kpe_ref[...], ct, sa, sb)
        for h in range(A_HEADS):
            sl = slice(h * HEAD_PAD, (h + 1) * HEAD_PAD)
            q_ref[:, sl] = _rope(qlin[:, sl], ct, sa, sb).astype(BF16)
            k_ref[:, sl] = (klin[:, sl] + kr).astype(BF16)

    row = lambda wd: pl.BlockSpec((tm, wd), lambda i: (i, 0))
    full = lambda r, c: pl.BlockSpec((r, c), lambda i: (0, 0))
    in_specs = [row(A_Q_RANK), row(A_KV_RANK), row(128), full(1, A_Q_RANK), full(1, A_KV_RANK),
                full(A_Q_RANK, qw), full(A_KV_RANK, qw), full(A_KV_RANK, GW), row(128), row(128), row(128)]
    body, in_specs, args = _after(dep, body, in_specs, [cq, ckv, kpe, gq, gkv, wuq, wk, wv, cos_t, sin_a, sin_b])
    return pl.pallas_call(
        body, name="mla_prep", grid=(t // tm,), in_specs=in_specs,
        out_specs=[row(qw), row(qw), row(GW), row(A_Q_RANK), row(A_KV_RANK)],
        out_shape=[jax.ShapeDtypeStruct((t, qw), BF16), jax.ShapeDtypeStruct((t, qw), BF16),
                   jax.ShapeDtypeStruct((t, GW), BF16), jax.ShapeDtypeStruct((t, A_Q_RANK), BF16),
                   jax.ShapeDtypeStruct((t, A_KV_RANK), BF16)],
        compiler_params=_cparams(("arbitrary",)),
    )(*args)


def _mla_prep_bwd(dq, dk, dv, cq, ckv, gq, gkv, wuq_t, wk_t, wv_t, cos_t, sin_a, sin_b, tm=ROW_TILE):
    t = cq.shape[0]
    tm = min(tm, t)
    qw = A_HEADS * HEAD_PAD

    def body(dq_ref, dk_ref, dv_ref, cq_ref, ckv_ref, gq_ref, gkv_ref, wuqt_ref, wkt_ref, wvt_ref,
             c_ref, sa_ref, sb_ref, dcq_ref, dckv_ref, dkpe_ref, dql_ref, dkl_ref, dgq_ref, dgkv_ref):
        i = pl.program_id(0)
        ct, sa, sb = c_ref[...], sa_ref[...], sb_ref[...]
        lane = lax.broadcasted_iota(jnp.int32, (1, HEAD_PAD), 1)
        nope = lane < A_NOPE
        rope = (lane >= A_NOPE) & (lane < A_NOPE + A_ROPE)
        dksum = None
        for h in range(A_HEADS):
            sl = slice(h * HEAD_PAD, (h + 1) * HEAD_PAD)
            dql_ref[:, sl] = _unrope(dq_ref[:, sl].astype(F32), ct, sa, sb).astype(BF16)
            dkh = dk_ref[:, sl].astype(F32)
            dkl_ref[:, sl] = jnp.where(nope, dkh, 0.0).astype(BF16)
            dksum = dkh if dksum is None else dksum + dkh
        dkpe_ref[...] = jnp.where(rope, _unrope(jnp.where(rope, dksum, 0.0), ct, sa, sb), 0.0).astype(BF16)
        dcqn = jnp.dot(dql_ref[...], wuqt_ref[...], preferred_element_type=F32)
        dckvn = (jnp.dot(dkl_ref[...], wkt_ref[...], preferred_element_type=F32)
                 + jnp.dot(dv_ref[...].astype(BF16), wvt_ref[...], preferred_element_type=F32))

        def norm_bwd(xv, gv, dy):
            rstd = lax.rsqrt(jnp.mean(xv * xv, axis=-1, keepdims=True) + EPS)
            xn = xv * rstd
            dxn = dy * gv
            dx = rstd * (dxn - xn * jnp.mean(dxn * xn, axis=-1, keepdims=True))
            return dx, jnp.sum(dy * xn, axis=0, keepdims=True)

        dcq, dgq = norm_bwd(cq_ref[...], gq_ref[...], dcqn)
        dckv, dgkv = norm_bwd(ckv_ref[...], gkv_ref[...], dckvn)
        dcq_ref[...] = dcq.astype(BF16)
        dckv_ref[...] = dckv.astype(BF16)

        @pl.when(i == 0)
        def _():
            dgq_ref[...] = dgq
            dgkv_ref[...] = dgkv

        @pl.when(i != 0)
        def _():
            dgq_ref[...] += dgq
            dgkv_ref[...] += dgkv

    row = lambda wd: pl.BlockSpec((tm, wd), lambda i: (i, 0))
    full = lambda r, c: pl.BlockSpec((r, c), lambda i: (0, 0))
    return pl.pallas_call(
        body, name="mla_prep_bwd", grid=(t // tm,),
        in_specs=[row(qw), row(qw), row(GW), row(A_Q_RANK), row(A_KV_RANK), full(1, A_Q_RANK), full(1, A_KV_RANK),
                  full(qw, A_Q_RANK), full(qw, A_KV_RANK), full(GW, A_KV_RANK), row(128), row(128), row(128)],
        out_specs=[row(A_Q_RANK), row(A_KV_RANK), row(128), row(qw), row(qw), full(1, A_Q_RANK), full(1, A_KV_RANK)],
        out_shape=[jax.ShapeDtypeStruct((t, A_Q_RANK), BF16), jax.ShapeDtypeStruct((t, A_KV_RANK), BF16),
                   jax.ShapeDtypeStruct((t, 128), BF16), jax.ShapeDtypeStruct((t, qw), BF16),
                   jax.ShapeDtypeStruct((t, qw), BF16), jax.ShapeDtypeStruct((1, A_Q_RANK), F32),
                   jax.ShapeDtypeStruct((1, A_KV_RANK), F32)],
        compiler_params=_cparams(("arbitrary",)),
    )(dq, dk, dv, cq, ckv, gq, gkv, wuq_t, wk_t, wv_t, cos_t, sin_a, sin_b)


def _nt(a, b):
    return lax.dot_general(a, b, (((1,), (1,)), ((), ())), preferred_element_type=F32)


def _tn(a, b):
    return lax.dot_general(a, b, (((0,), (0,)), ((), ())), preferred_element_type=F32)


def _causal_mask(kind, q0, k0, tq, tk):
    qpos = q0 + lax.broadcasted_iota(jnp.int32, (tq, tk), 0)
    kpos = k0 + lax.broadcasted_iota(jnp.int32, (tq, tk), 1)
    if kind == "mla":
        return lax.shift_right_logical(kpos, 6) <= lax.shift_right_logical(qpos, 6)
    return kpos <= qpos


def _attn_fwd(kind, q, k, v, f, seq, scale, tq=512, tk=512):
    t = v.shape[0]
    nb = t // seq
    nq = seq // tq
    hw = 256 if kind == "mla" else 128
    n_heads = A_HEADS if kind == "mla" else C_HEADS
    use_f = f is not None
    tq, tk = min(tq, seq), min(tk, seq)
    nq = seq // tq
    assert tk == tq

    def body(*refs):
        if use_f:
            q_ref, k_ref, v_ref, f_ref, o_ref, st_ref = refs
        else:
            q_ref, k_ref, v_ref, o_ref, st_ref = refs
        qi = pl.program_id(2)
        q0 = qi * tq
        lane = lax.broadcasted_iota(jnp.int32, (1, 128), 1)
        half = lane >= 64
        qall = q_ref[...]
        if kind == "mla":
            qhs = [qall[:, 0:128], qall[:, 128:256]]
            post = scale * math.log2(math.e)
        else:
            assert math.frexp(scale)[0] == 0.5
            qall = qall * jnp.asarray(scale, BF16)
            qhs = [jnp.where(half, jnp.zeros_like(qall), qall), jnp.where(half, qall, jnp.zeros_like(qall))]
            post = None
        kd = pl.multiple_of(q0, tq)
        diag = _causal_mask(kind, 0, 0, tq, tk)

        def block(j, k0, state, masked):
            m, l, acc = state
            kh = k_ref[pl.ds(k0, tk), j * 128:(j + 1) * 128] if kind == "mla" else k_ref[pl.ds(k0, tk), :]
            s = _nt(qhs[j], kh)
            if post is not None:
                s = s * post
            if use_f:
                s = s - f_ref[0, 0, j:j + 1, pl.ds(k0, tk)]
            if masked:
                s = jnp.where(diag, s, NEG)
            mn = jnp.maximum(m, jnp.max(s, axis=-1, keepdims=True))
            alpha = jnp.exp2(m - mn) if post is not None else jnp.exp(m - mn)
            p = jnp.exp2(s - mn) if post is not None else jnp.exp(s - mn)
            l = alpha * l + jnp.sum(p, axis=-1, keepdims=True)
            acc = alpha * acc + jnp.dot(p.astype(BF16), v_ref[pl.ds(k0, tk), :], preferred_element_type=F32)
            return mn, l, acc

        def run(heads):
            def kstep(kb, carry):
                k0 = pl.multiple_of(kb * tk, tk)
                out = ()
                for n, j in enumerate(heads):
                    out += block(j, k0, carry[3 * n:3 * n + 3], False)
                return out

            init = (jnp.full((tq, 1), NEG, F32), jnp.zeros((tq, 1), F32), jnp.zeros((tq, 128), F32)) * len(heads)
            carry = lax.fori_loop(0, qi, kstep, init)
            o, st = jnp.zeros((tq, 128), F32), jnp.zeros((tq, 128), F32)
            for n, j in enumerate(heads):
                m, l, acc = block(j, kd, carry[3 * n:3 * n + 3], True)
                o = jnp.where(half == bool(j), acc / l, o)
                if post is not None:
                    m = m * math.log(2.0)
                st = jnp.where(lane == j, m + jnp.log(l), st)
            o_ref[...] = o
            st_ref[...] = st

        if n_heads % 2 == 0:
            run((0, 1))
        else:
            last = pl.program_id(1) == n_heads // 2
            pl.when(jnp.logical_not(last))(lambda: run((0, 1)))
            pl.when(last)(lambda: run((0,)))

    in_specs = [pl.BlockSpec((tq, hw), lambda b, p, i: (b * nq + i, p)),
                pl.BlockSpec((seq, hw), lambda b, p, i: (b, p)),
                pl.BlockSpec((seq, 128), lambda b, p, i: (b, p))]
    args = [q, k, v]
    if use_f:
        in_specs.append(pl.BlockSpec((1, 1, 8, seq), lambda b, p, i: (b, p, 0, 0)))
        args.append(f)
    oblk = pl.BlockSpec((tq, 128), lambda b, p, i: (b * nq + i, p))
    return pl.pallas_call(
        body, name="attn_fwd_" + kind, grid=(nb, 3, nq), in_specs=in_specs, out_specs=[oblk, oblk],
        out_shape=[jax.ShapeDtypeStruct((t, GW), F32), jax.ShapeDtypeStruct((t, GW), F32)],
        compiler_params=_cparams(("arbitrary", "arbitrary", "arbitrary")),
    )(*args)


def _attn_bwd(kind, q, k, v, f, o, st, do, seq, scale, tq=512, tk=512, dep=None):
    t = v.shape[0]
    nb = t // seq
    tq, tk = min(tq, seq), min(tk, seq)
    nq = seq // tq
    nk = seq // tk
    hw = 256 if kind == "mla" else 128
    n_heads = A_HEADS if kind == "mla" else C_HEADS
    use_f = f is not None
    assert tq == tk

    def body(*refs):
        if use_f:
            (q_ref, k_ref, v_ref, f_ref, o_ref, st_ref, do_ref, dq_out, dk_out, dv_out, df_ref, dfq_ref,
             dq_ref, dk_ref, dv_ref) = refs
        else:
            q_ref, k_ref, v_ref, o_ref, st_ref, do_ref, dq_out, dk_out, dv_out, dq_ref, dk_ref, dv_ref = refs
        kj = pl.program_id(2)
        lane = lax.broadcasted_iota(jnp.int32, (1, 128), 1)
        half = lane >= 64

        @pl.when(kj == 0)
        def _():
            dq_ref[...] = jnp.zeros_like(dq_ref)
            if use_f:
                dfq_ref[...] = jnp.zeros_like(dfq_ref)

        dk_ref[...] = jnp.zeros_like(dk_ref)
        dv_ref[...] = jnp.zeros_like(dv_ref)
        if use_f:
            df_ref[...] = jnp.zeros_like(df_ref)
        vv = v_ref[...]
        diag = _causal_mask(kind, 0, 0, tq, tk)

        def qstep(qi, masked):
            q0 = pl.multiple_of(qi * tq, tq)
            rows = pl.ds(q0, tq)
            dov = do_ref[rows, :]
            dd = dov.astype(F32) * o_ref[rows, :]
            stv = st_ref[rows, :]

            def one_head(j):
                hm = half == bool(j)
                delta = jnp.sum(jnp.where(hm, dd, 0.0), axis=-1, keepdims=True)
                lse = stv[:, j:j + 1]
                if kind == "mla":
                    cols = slice(j * 128, (j + 1) * 128)
                    qh = q_ref[rows, cols]
                    kh = k_ref[:, cols]
                else:
                    cols = slice(0, 128)
                    qa = q_ref[rows, :]
                    qh = jnp.where(hm, qa, jnp.zeros_like(qa))
                    kh = k_ref[...]
                s = _nt(qh, kh) * scale
                if use_f:
                    s = s - f_ref[0, 0, j:j + 1, :]
                if masked:
                    s = jnp.where(diag, s, NEG)
                p = jnp.exp(s - lse)
                doh = jnp.where(hm, dov, jnp.zeros_like(dov))
                ds = p * (_nt(doh, vv) - delta)
                dsb = (ds * scale).astype(BF16)
                dv_ref[...] += _tn(p.astype(BF16), doh)
                dk_ref[:, cols] += _tn(dsb, qh)
                dqc = jnp.dot(dsb, kh, preferred_element_type=F32)
                if kind != "mla":
                    dqc = jnp.where(hm, dqc, 0.0)
                dq_ref[rows, cols] += dqc
                if use_f:
                    df_ref[0, 0, j:j + 1, :] += -jnp.sum(ds, axis=0, keepdims=True)
                    dfq_ref[rows, :] += jnp.where(lane == j, jnp.sum(ds, axis=-1, keepdims=True), 0.0)

            def both():
                one_head(0)
                one_head(1)

            if n_heads % 2 == 0:
                both()
            else:
                last = pl.program_id(1) == n_heads // 2
                pl.when(jnp.logical_not(last))(both)
                pl.when(last)(lambda: one_head(0))

        qstep(kj, True)

        def rest(qi, carry):
            qstep(qi, False)
            return carry

        lax.fori_loop(kj + 1, nq, rest, 0)
        dk_out[...] = dk_ref[...].astype(BF16)
        dv_out[...] = dv_ref[...].astype(BF16)

        @pl.when(kj == nk - 1)
        def _():
            dq_out[...] = dq_ref[...].astype(BF16)

    full_q = lambda wd: pl.BlockSpec((seq, wd), lambda b, p, i: (b, p))
    kblk = lambda wd: pl.BlockSpec((tk, wd), lambda b, p, i: (b * nk + i, p))
    in_specs = [full_q(hw), kblk(hw), kblk(128)]
    args = [q, k, v]
    if use_f:
        in_specs.append(pl.BlockSpec((1, 1, 8, tk), lambda b, p, i: (b, p, 0, i)))
        args.append(f)
    in_specs += [full_q(128), full_q(128), full_q(128)]
    args += [o, st, do]
    out_specs = [full_q(hw), kblk(hw), kblk(128)]
    out_shape = [jax.ShapeDtypeStruct((t, 3 * hw), BF16), jax.ShapeDtypeStruct((t, 3 * hw), BF16),
                 jax.ShapeDtypeStruct((t, GW), BF16)]
    scratch = [pltpu.VMEM((seq, hw), F32), pltpu.VMEM((tk, hw), F32), pltpu.VMEM((tk, 128), F32)]
    if use_f:
        out_specs += [pl.BlockSpec((1, 1, 8, tk), lambda b, p, i: (b, p, 0, i)), full_q(128)]
        out_shape += [jax.ShapeDtypeStruct((nb, 3, 8, seq), F32), jax.ShapeDtypeStruct((t, GW), F32)]
    body, in_specs, args = _after(dep, body, in_specs, args)
    return pl.pallas_call(
        body, name="attn_bwd_" + kind, grid=(nb, 3, nk), in_specs=in_specs, out_specs=out_specs,
        out_shape=out_shape, scratch_shapes=scratch,
        compiler_params=_cparams(("arbitrary", "arbitrary", "arbitrary")),
    )(*args)


BQ = 256
BWIN = BQ + B_LEFT


def _band_geometry():
    r = lax.broadcasted_iota(jnp.int32, (BQ, BWIN), 0)
    j = lax.broadcasted_iota(jnp.int32, (BQ, BWIN), 1)
    rc = lax.shift_right_logical(r, 6)
    jc = lax.shift_right_logical(j, 6)
    allowed = (jc - 8 <= rc) & (rc <= jc)
    return (r + B_LEFT - j) >= REL_CLIP, allowed, j < r


def _band_onehot(transposed, offset=0):
    shape = (BWIN, GW) if transposed else (GW, BWIN)
    kk = lax.broadcasted_iota(jnp.int32, shape, 1 if transposed else 0)
    x = lax.broadcasted_iota(jnp.int32, shape, 0 if transposed else 1) - offset
    x = jnp.where(x < 0, x + BWIN, x)
    return (kk == jnp.clip(B_LEFT - x, -REL_CLIP, REL_CLIP) + REL_CLIP).astype(F32)


def _band_table(rel_bias8):
    def body(b_ref, o_ref):
        hh = pl.program_id(0)
        u8 = jnp.dot(b_ref[...], _band_onehot(False), precision=HI, preferred_element_type=F32)
        rid = lax.broadcasted_iota(jnp.int32, (8, BWIN), 0)
        row = jnp.sum(jnp.where(rid == hh, u8, 0.0), axis=0, keepdims=True)
        far, allowed, _ = _band_geometry()
        tbl = pltpu.roll(jnp.broadcast_to(row, (BQ, BWIN)), 0, 1, stride=1, stride_axis=0)
        tbl = jnp.where(far, row[:, 0:1], tbl)
        o_ref[0] = jnp.where(allowed, tbl, NEG)

    return pl.pallas_call(
        body, name="band_table", grid=(6,),
        in_specs=[pl.BlockSpec((8, GW), lambda h: (0, 0))],
        out_specs=pl.BlockSpec((1, BQ, BWIN), lambda h: (h, 0, 0)),
        out_shape=jax.ShapeDtypeStruct((6, BQ, BWIN), F32),
        compiler_params=_cparams(("arbitrary",)),
    )(rel_bias8)


def _band_table_bwd(gtab):
    def body(g_ref, o_ref):
        gv = g_ref[0]
        _, _, wrapped = _band_geometry()
        gfar = jnp.sum(jnp.sum(jnp.where(wrapped, gv, 0.0), axis=-1, keepdims=True), axis=0, keepdims=True)
        anti = (lax.broadcasted_iota(jnp.int32, (BQ, BQ), 0) + lax.broadcasted_iota(jnp.int32, (BQ, BQ), 1)
                == BQ - 1).astype(F32)
        grev = jnp.dot(anti, jnp.where(wrapped, 0.0, gv), precision=HI, preferred_element_type=F32)
        near = pltpu.roll(grev, 0, 1, stride=1, stride_axis=0)
        y = jnp.broadcast_to(jnp.sum(near, axis=0, keepdims=True), (8, BWIN))
        gb = jnp.dot(y, _band_onehot(True, BQ - 1), precision=HI, preferred_element_type=F32)
        lane = lax.broadcasted_iota(jnp.int32, (8, GW), 1)
        o_ref[0] = gb + jnp.where(lane == 2 * REL_CLIP, gfar, 0.0)

    return pl.pallas_call(
        body, name="band_table_bwd", grid=(B_HEADS,),
        in_specs=[pl.BlockSpec((1, BQ, BWIN), lambda h: (h, 0, 0))],
        out_specs=pl.BlockSpec((1, 8, GW), lambda h: (h, 0, 0)),
        out_shape=jax.ShapeDtypeStruct((B_HEADS, 8, GW), F32),
        compiler_params=_cparams(("arbitrary",)),
    )(gtab)


def _band_fwd(q, k, v, table, seq, scale):
    t = q.shape[0]
    nb = t // seq
    nq = seq // BQ

    def body(q_ref, k_ref, v_ref, tb_ref, o_ref, st_ref, kpad, vpad):
        qi = pl.program_id(2)
        q0 = pl.multiple_of(qi * BQ, BQ)
        lane = lax.broadcasted_iota(jnp.int32, (1, 128), 1)
        half = lane >= 64

        @pl.when(qi == 0)
        def _():
            kpad[0:B_LEFT, :] = jnp.zeros((B_LEFT, 128), BF16)
            vpad[0:B_LEFT, :] = jnp.zeros((B_LEFT, 128), BF16)
            kpad[B_LEFT:, :] = k_ref[...]
            vpad[B_LEFT:, :] = v_ref[...]

        kw = kpad[pl.ds(q0, BWIN), :]
        vw = vpad[pl.ds(q0, BWIN), :]
        inside = lax.broadcasted_iota(jnp.int32, (BQ, BWIN), 1) >= B_LEFT - q0
        assert math.frexp(scale)[0] == 0.5
        qall = q_ref[...] * jnp.asarray(scale, BF16)

        def run(heads):
            o, st = jnp.zeros((BQ, 128), F32), jnp.zeros((BQ, 128), F32)
            for j in heads:
                qh = jnp.where(half == bool(j), qall, jnp.zeros_like(qall))
                s = jnp.where(inside, _nt(qh, kw) + tb_ref[j], NEG)
                m = jnp.max(s, axis=-1, keepdims=True)
                p = jnp.exp(s - m)
                l = jnp.sum(p, axis=-1, keepdims=True)
                o = jnp.where(half == bool(j), jnp.dot(p.astype(BF16), vw, preferred_element_type=F32) / l, o)
                st = jnp.where(lane == j, m + jnp.log(l), st)
            o_ref[...] = o
            st_ref[...] = st

        last = pl.program_id(1) == B_HEADS // 2
        pl.when(jnp.logical_not(last))(lambda: run((0, 1)))
        pl.when(last)(lambda: run((0,)))

    qblk = pl.BlockSpec((BQ, 128), lambda b, p, i: (b * nq + i, p))
    full = pl.BlockSpec((seq, 128), lambda b, p, i: (b, p))
    return pl.pallas_call(
        body, name="band_fwd", grid=(nb, 3, nq),
        in_specs=[qblk, full, full, pl.BlockSpec((2, BQ, BWIN), lambda b, p, i: (p, 0, 0))],
        out_specs=[qblk, qblk],
        out_shape=[jax.ShapeDtypeStruct((t, GW), F32), jax.ShapeDtypeStruct((t, GW), F32)],
        scratch_shapes=[pltpu.VMEM((seq + B_LEFT, 128), BF16), pltpu.VMEM((seq + B_LEFT, 128), BF16)],
        compiler_params=_cparams(("arbitrary", "arbitrary", "arbitrary")),
    )(q, k, v, table)


def _band_bwd(q, k, v, table, o, st, do, seq, scale, dep=None):
    t = q.shape[0]
    nb = t // seq
    nq = seq // BQ

    def body(q_ref, k_ref, v_ref, tb_ref, o_ref, st_ref, do_ref, dq_ref, dk_ref, dv_ref, g_ref,
             kpad, vpad, dkpad, dvpad):
        b = pl.program_id(1)
        qi = pl.program_id(2)
        q0 = pl.multiple_of(qi * BQ, BQ)
        lane = lax.broadcasted_iota(jnp.int32, (1, 128), 1)
        half = lane >= 64

        @pl.when(qi == 0)
        def _():
            kpad[0:B_LEFT, :] = jnp.zeros((B_LEFT, 128), BF16)
            vpad[0:B_LEFT, :] = jnp.zeros((B_LEFT, 128), BF16)
            kpad[B_LEFT:, :] = k_ref[...]
            vpad[B_LEFT:, :] = v_ref[...]
            dkpad[...] = jnp.zeros_like(dkpad)
            dvpad[...] = jnp.zeros_like(dvpad)

        @pl.when((qi == 0) & (b == 0))
        def _():
            g_ref[...] = jnp.zeros_like(g_ref)

        win = pl.ds(q0, BWIN)
        kw = kpad[win, :]
        vw = vpad[win, :]
        inside = lax.broadcasted_iota(jnp.int32, (BQ, BWIN), 1) >= B_LEFT - q0
        qall = q_ref[...]
        dov = do_ref[...]
        dd = dov.astype(F32) * o_ref[...]
        stv = st_ref[...]

        def run(heads):
            dq = jnp.zeros((BQ, 128), F32)
            for j in heads:
                hm = half == bool(j)
                qh = jnp.where(hm, qall, jnp.zeros_like(qall))
                delta = jnp.sum(jnp.where(hm, dd, 0.0), axis=-1, keepdims=True)
                s = jnp.where(inside, _nt(qh, kw) * scale + tb_ref[j], NEG)
                p = jnp.exp(s - stv[:, j:j + 1])
                doh = jnp.where(hm, dov, jnp.zeros_like(dov))
                ds = p * (_nt(doh, vw) - delta)
                g_ref[j] += ds
                dsb = (ds * scale).astype(BF16)
                dvpad[win, :] += _tn(p.astype(BF16), doh)
                dkpad[win, :] += _tn(dsb, qh)
                dq = dq + jnp.where(hm, jnp.dot(dsb, kw, preferred_element_type=F32), 0.0)
            dq_ref[...] = dq.astype(BF16)

        last = pl.program_id(0) == B_HEADS // 2
        pl.when(jnp.logical_not(last))(lambda: run((0, 1)))
        pl.when(last)(lambda: run((0,)))

        @pl.when(qi == nq - 1)
        def _():
            dk_ref[...] = dkpad[B_LEFT:, :].astype(BF16)
            dv_ref[...] = dvpad[B_LEFT:, :].astype(BF16)

    qblk = pl.BlockSpec((BQ, 128), lambda p, b, i: (b * nq + i, p))
    full = pl.BlockSpec((seq, 128), lambda p, b, i: (b, p))
    tblk = pl.BlockSpec((2, BQ, BWIN), lambda p, b, i: (p, 0, 0))
    body, in_specs, args = _after(dep, body, [qblk, full, full, tblk, qblk, qblk, qblk], [q, k, v, table, o, st, do])
    return pl.pallas_call(
        body, name="band_bwd", grid=(3, nb, nq),
        in_specs=in_specs,
        out_specs=[qblk, full, full, tblk],
        out_shape=[jax.ShapeDtypeStruct((t, GW), BF16), jax.ShapeDtypeStruct((t, GW), BF16),
                   jax.ShapeDtypeStruct((t, GW), BF16), jax.ShapeDtypeStruct((6, BQ, BWIN), F32)],
        scratch_shapes=[pltpu.VMEM((seq + B_LEFT, 128), BF16), pltpu.VMEM((seq + B_LEFT, 128), BF16),
                        pltpu.VMEM((seq + B_LEFT, 128), F32), pltpu.VMEM((seq + B_LEFT, 128), F32)],
        compiler_params=_cparams(("arbitrary", "arbitrary", "arbitrary")),
    )(*args)


def _fox_prep(cf, fb, seq):
    nb = cf.shape[0] // seq
    nblk = seq // 128

    def body(cf_ref, fb_ref, f_ref):
        x = cf_ref[...] + fb_ref[...]
        lf = jnp.minimum(x, 0.0) - jnp.log1p(jnp.exp(-jnp.abs(x)))
        rows = lf.T[0:8, :]
        upper = (lax.broadcasted_iota(jnp.int32, (128, 128), 0)
                 <= lax.broadcasted_iota(jnp.int32, (128, 128), 1)).astype(F32)
        carry = jnp.zeros((8, 1), F32)
        for blk in range(nblk):
            sl = slice(blk * 128, (blk + 1) * 128)
            cs = jnp.dot(rows[:, sl], upper, precision=HI, preferred_element_type=F32) + carry
            carry = cs[:, 127:128]
            f_ref[0, 0, :, sl] = cs
            f_ref[0, 1, :, sl] = pltpu.roll(cs, 6, 0)
            f_ref[0, 2, :, sl] = pltpu.roll(cs, 4, 0)

    return pl.pallas_call(
        body, name="fox_prep", grid=(nb,),
        in_specs=[pl.BlockSpec((seq, 128), lambda b: (b, 0)), pl.BlockSpec((1, 128), lambda b: (0, 0))],
        out_specs=pl.BlockSpec((1, 3, 8, seq), lambda b: (b, 0, 0, 0)),
        out_shape=jax.ShapeDtypeStruct((nb, 3, 8, seq), F32),
        compiler_params=_cparams(("arbitrary",)),
    )(cf, fb)


def _fox_prep_bwd(df, dfq, cf, fb, seq):
    nb = cf.shape[0] // seq
    nblk = seq // 128

    def body(df_ref, dfq_ref, cf_ref, fb_ref, dcf_ref, dfb_ref, wide):
        b = pl.program_id(0)
        row = lax.broadcasted_iota(jnp.int32, (8, seq), 0)
        dfh = None
        for p in range(3):
            both = df_ref[0, p] + dfq_ref[:, p * 128:(p + 1) * 128].T[0:8, :]
            both = jnp.where(row < 2, both, 0.0)
            if p:
                both = pltpu.roll(both, 2 * p, 0)
            dfh = both if dfh is None else dfh + both
        lower = (lax.broadcasted_iota(jnp.int32, (128, 128), 0)
                 >= lax.broadcasted_iota(jnp.int32, (128, 128), 1)).astype(F32)
        wide[...] = jnp.zeros_like(wide)
        carry = jnp.zeros((8, 1), F32)
        for blk in reversed(range(nblk)):
            sl = slice(blk * 128, (blk + 1) * 128)
            rc = jnp.dot(dfh[:, sl], lower, precision=HI, preferred_element_type=F32) + carry
            carry = rc[:, 0:1]
            wide[0:8, sl] = rc
        dl = wide[...].T
        x = cf_ref[...] + fb_ref[...]
        dcf = dl * (1.0 / (1.0 + jnp.exp(x)))
        dcf_ref[...] = dcf.astype(BF16)
        part = jnp.sum(dcf, axis=0, keepdims=True)

        @pl.when(b == 0)
        def _():
            dfb_ref[...] = part

        @pl.when(b != 0)
        def _():
            dfb_ref[...] += part

    return pl.pallas_call(
        body, name="fox_prep_bwd", grid=(nb,),
        in_specs=[pl.BlockSpec((1, 3, 8, seq), lambda b: (b, 0, 0, 0)), pl.BlockSpec((seq, GW), lambda b: (b, 0)),
                  pl.BlockSpec((seq, 128), lambda b: (b, 0)), pl.BlockSpec((1, 128), lambda b: (0, 0))],
        out_specs=[pl.BlockSpec((seq, 128), lambda b: (b, 0)), pl.BlockSpec((1, 128), lambda b: (0, 0))],
        out_shape=[jax.ShapeDtypeStruct(cf.shape, BF16), jax.ShapeDtypeStruct((1, 128), F32)],
        scratch_shapes=[pltpu.VMEM((128, seq), F32)],
        compiler_params=_cparams(("arbitrary",)),
    )(df, dfq, cf, fb)


def _gate_out(oa, ob, oc, gates, w, x, gate, seq, tm=ROW_TILE):
    t = x.shape[0]
    tm = min(tm, seq)
    tps = seq // tm

    def body(oa_ref, ob_ref, oc_ref, g_ref, w_ref, x_ref, gt_ref, xo_ref, y_ref, u_ref):
        for n, o_ref in enumerate((oa_ref, ob_ref, oc_ref)):
            sl = slice(n * GW, (n + 1) * GW)
            gv = g_ref[:, sl].astype(F32)
            u_ref[:, sl] = (o_ref[...] * (gv * _sigmoid(gv))).astype(BF16)
        y = jnp.dot(u_ref[...], w_ref[...], preferred_element_type=F32)
        y_ref[...] = y.astype(BF16)
        xo_ref[...] = x_ref[...] + gt_ref[0] * y

    row = lambda wd: pl.BlockSpec((tm, wd), lambda i: (i, 0))
    return pl.pallas_call(
        body, name="gate_out", grid=(t // tm,),
        in_specs=[row(GW), row(GW), row(GW), row(U_PAD), pl.BlockSpec((U_PAD, D_MODEL), lambda i: (0, 0)),
                  row(D_MODEL), pl.BlockSpec((1, 1, D_MODEL), lambda i: (i // tps, 0, 0))],
        out_specs=[row(D_MODEL), row(D_MODEL), row(U_PAD)],
        out_shape=[jax.ShapeDtypeStruct((t, D_MODEL), F32), jax.ShapeDtypeStruct((t, D_MODEL), BF16),
                   jax.ShapeDtypeStruct((t, U_PAD), BF16)],
        compiler_params=_cparams(("arbitrary",)),
    )(oa, ob, oc, gates, w, x, gate)


def _gate_out_bwd(dxo, y, gate, oa, ob, oc, gates, w_t, seq, tm=ROW_TILE, dep=None):
    t = dxo.shape[0]
    tm = min(tm, seq)
    tps = seq // tm
    nb = t // seq

    def body(dxo_ref, y_ref, gt_ref, oa_ref, ob_ref, oc_ref, g_ref, wt_ref,
             dy_ref, doa_ref, dob_ref, doc_ref, dg_ref, dgt_ref):
        i = pl.program_id(0)
        dxo_v = dxo_ref[...]
        dgt = jnp.sum(dxo_v * y_ref[...].astype(F32), axis=0, keepdims=True)
        dyb = (dxo_v * gt_ref[0]).astype(BF16)
        dy_ref[...] = dyb
        du = _nt(dyb, wt_ref[...])
        for n, (o_ref, do_ref) in enumerate(((oa_ref, doa_ref), (ob_ref, dob_ref), (oc_ref, doc_ref))):
            sl = slice(n * GW, (n + 1) * GW)
            gv = g_ref[:, sl].astype(F32)
            sg = _sigmoid(gv)
            dun = du[:, sl]
            do_ref[...] = (dun * (gv * sg)).astype(BF16)
            dg_ref[:, sl] = (dun * o_ref[...] * (sg * (1.0 + gv * (1.0 - sg)))).astype(BF16)

        @pl.when(i % tps == 0)
        def _():
            dgt_ref[0] = dgt

        @pl.when(i % tps != 0)
        def _():
            dgt_ref[0] += dgt

    row = lambda wd: pl.BlockSpec((tm, wd), lambda i: (i, 0))
    per_b = pl.BlockSpec((1, 1, D_MODEL), lambda i: (i // tps, 0, 0))
    in_specs = [row(D_MODEL), row(D_MODEL), per_b, row(GW), row(GW), row(GW), row(U_PAD),
                pl.BlockSpec((U_PAD, D_MODEL), lambda i: (0, 0))]
    body, in_specs, args = _after(dep, body, in_specs, [dxo, y, gate, oa, ob, oc, gates, w_t])
    return pl.pallas_call(
        body, name="gate_out_bwd", grid=(t // tm,), in_specs=in_specs,
        out_specs=[row(D_MODEL), row(GW), row(GW), row(GW), row(U_PAD), per_b],
        out_shape=[jax.ShapeDtypeStruct((t, D_MODEL), BF16), jax.ShapeDtypeStruct((t, GW), BF16),
                   jax.ShapeDtypeStruct((t, GW), BF16), jax.ShapeDtypeStruct((t, GW), BF16),
                   jax.ShapeDtypeStruct((t, U_PAD), BF16), jax.ShapeDtypeStruct((nb, 1, D_MODEL), F32)],
        compiler_params=_cparams(("arbitrary",)),
    )(*args)


def _final_loss(x, target, g, tm=ROW_TILE):
    t = x.shape[0]
    tm = min(tm, t)

    def body(x_ref, t_ref, g_ref, dx_ref, loss_ref, dg_ref):
        i = pl.program_id(0)
        xv = x_ref[...]
        rstd = lax.rsqrt(jnp.mean(xv * xv, axis=-1, keepdims=True) + EPS)
        xn = xv * rstd
        gv = g_ref[...]
        err = xn * gv - t_ref[...]
        dy = err * (1.0 / D_MODEL)
        dxn = dy * gv
        dx_ref[...] = rstd * (dxn - xn * jnp.mean(dxn * xn, axis=-1, keepdims=True))
        lp = jnp.sum(err * err, axis=0, keepdims=True) * (0.5 / D_MODEL)
        dgp = jnp.sum(dy * xn, axis=0, keepdims=True)

        @pl.when(i == 0)
        def _():
            loss_ref[...] = lp
            dg_ref[...] = dgp

        @pl.when(i != 0)
        def _():
            loss_ref[...] += lp
            dg_ref[...] += dgp

    row = pl.BlockSpec((tm, D_MODEL), lambda i: (i, 0))
    vec = pl.BlockSpec((1, D_MODEL), lambda i: (0, 0))
    return pl.pallas_call(
        body, name="final_loss", grid=(t // tm,),
        in_specs=[row, row, vec], out_specs=[row, vec, vec],
        out_shape=[jax.ShapeDtypeStruct((t, D_MODEL), F32), jax.ShapeDtypeStruct((1, D_MODEL), F32),
                   jax.ShapeDtypeStruct((1, D_MODEL), F32)],
        compiler_params=_cparams(("arbitrary",)),
    )(x, target, g)


def _adamw(w, gslots, m, v, name, tr=None):
    nl, r, c = w.shape
    ns = gslots.shape[0]
    tr = r if tr is None else tr

    def body(w_ref, g_ref, m_ref, v_ref, go_ref, d_ref, mo_ref, vo_ref):
        g = g_ref[0].astype(F32)
        for j in range(1, ns):
            g = g + g_ref[j].astype(F32)
        mn = ADAM_B1 * m_ref[...] + (1.0 - ADAM_B1) * g
        vn = ADAM_B2 * v_ref[...] + (1.0 - ADAM_B2) * jnp.square(g)
        m_hat = mn / (1.0 - ADAM_B1 ** ADAM_STEP)
        v_hat = vn / (1.0 - ADAM_B2 ** ADAM_STEP)
        go_ref[...] = g
        d_ref[...] = -ADAM_LR * (m_hat / (jnp.sqrt(v_hat) + ADAM_EPS) + ADAM_WD * w_ref[...])
        mo_ref[...] = mn
        vo_ref[...] = vn

    blk = pl.BlockSpec((1, tr, c), lambda l, i: (l, i, 0))
    return pl.pallas_call(
        body, name=name, grid=(nl, r // tr),
        in_specs=[blk, pl.BlockSpec((ns, 1, tr, c), lambda l, i: (0, l, i, 0)), blk, blk],
        out_specs=[blk] * 4, out_shape=[jax.ShapeDtypeStruct((nl, r, c), F32)] * 4,
        compiler_params=_cparams(("arbitrary", "arbitrary")),
    )(w, gslots, m, v)


def _rope_tables(positions):
    inv = ROPE_THETA ** (-jnp.arange(0, A_ROPE, 2, dtype=F32) / A_ROPE)
    ang = positions.astype(F32)[:, None] * inv
    cos, sin = jnp.cos(ang), jnp.sin(ang)
    t = positions.shape[0]
    one = jnp.ones((t, 64), F32)
    zero16 = jnp.zeros((t, 16), F32)
    cos_t = jnp.concatenate([one, cos, cos, jnp.ones((t, 32), F32)], axis=1)
    sin_a = jnp.concatenate([jnp.zeros((t, 64), F32), -sin, zero16, jnp.zeros((t, 32), F32)], axis=1)
    sin_b = jnp.concatenate([jnp.zeros((t, 64), F32), zero16, sin, jnp.zeros((t, 32), F32)], axis=1)
    return cos_t, sin_a, sin_b


def _pad_heads(w, real, padded, nheads, axis):
    shp = w.shape[:axis] + (nheads, real) + w.shape[axis + 1:]
    w = w.reshape(shp)
    pad = [(0, 0)] * w.ndim
    pad[axis + 1] = (0, padded - real)
    w = jnp.pad(w, pad)
    return w.reshape(w.shape[:axis] + (nheads * padded,) + w.shape[axis + 2:])


def kernel(x, c, positions, w_ada, b_ada, norm_g, w_in, a_q_norm_g, a_w_uq, a_kv_norm_g, a_w_ukv, b_rel_bias, c_forget_b, w_out, final_g, loss_target, m_w_ada, m_b_ada, m_norm_g, m_w_in, m_a_q_norm_g, m_a_w_uq, m_a_kv_norm_g, m_a_w_ukv, m_b_rel_bias, m_c_forget_b, m_w_out, m_final_g, v_w_ada, v_b_ada, v_norm_g, v_w_in, v_a_q_norm_g, v_a_w_uq, v_a_kv_norm_g, v_a_w_ukv, v_b_rel_bias, v_c_forget_b, v_w_out, v_final_g):
    nb, seq, _ = x.shape
    t = nb * seq
    me = 4 * lax.axis_index("x") + 2 * lax.axis_index("y") + lax.axis_index("c")
    x2 = x.reshape(t, D_MODEL)
    tgt = loss_target.reshape(t, D_MODEL)
    cos_t, sin_a, sin_b = _rope_tables(positions.reshape(t))

    def shards(l):
        return [_pad_runs(w_in[l].astype(BF16), IN_RUNS, N_PAD, 1), w_out[l].astype(BF16),
                a_w_uq[l].astype(BF16), a_w_ukv[l].astype(BF16)]

    def prepare(gi, go, gq, gkv):
        return dict(w_in=gi.reshape(D_MODEL, N_PAD), **prepare_rest(go, gq, gkv))

    def prepare_rest(go, gq, gkv):
        wo = _pad_runs(go.reshape(D_MODEL, D_MODEL), OUT_RUNS, U_PAD, 0)
        wq = jnp.transpose(gq, (1, 0, 2)).reshape(A_Q_RANK, A_HEADS * (A_NOPE + A_ROPE))
        wq = _pad_heads(wq, A_NOPE + A_ROPE, HEAD_PAD, A_HEADS, 1)
        wkv = jnp.transpose(gkv, (1, 0, 2)).reshape(A_KV_RANK, A_HEADS, 2 * A_NOPE)
        wk = jnp.pad(wkv[:, :, :A_NOPE], ((0, 0), (0, 0), (0, HEAD_PAD - A_NOPE))).reshape(A_KV_RANK, A_HEADS * HEAD_PAD)
        wv = wkv[:, :, A_NOPE:].reshape(A_KV_RANK, GW)
        return dict(w_out=wo, wuq=wq, wuq_t=wq.T, wk=wk, wk_t=wk.T, wv=wv, wv_t=wv.T)

    shards0 = shards(0)
    w_in0_g, c_g = _gather([shards0[0], c], "gather_w_in0")
    c_all = c_g.reshape(N_DEV * nb, D_MODEL)
    weights = [dict(w_in=w_in0_g.reshape(D_MODEL, N_PAD)), None]

    c_act, mod_cols = _ada_fwd(c_all, w_ada)
    (mod_g,) = _gather([mod_cols], "gather_mod")
    rest0, rest0_token = _split_start("gather", shards0[1:], "gather_rest0_start", after=mod_g)
    mod_all = jnp.transpose(mod_g, (1, 2, 0, 3)).reshape(DEPTH, N_DEV * nb, 3 * D_MODEL)
    mod = lax.dynamic_slice_in_dim(mod_all, me * nb, nb, axis=1) + b_ada[:, None, :]

    fb_pad = jnp.pad(c_forget_b, ((0, 0), (0, 128 - C_HEADS)))
    a_scale = (A_NOPE + A_ROPE) ** -0.5
    h_scale = CHUNK ** -0.5

    saved = []
    xl = x2
    for l in range(DEPTH):
        if l == 1:
            weights[1] = prepare(*_split_wait(gather1, xl, "gather_weights1_wait")[1])
        w = weights[l]
        shift, scale, gate = mod[l, :, :D_MODEL], mod[l, :, D_MODEL:2 * D_MODEL], mod[l, :, 2 * D_MODEL:]
        ss = jnp.stack([shift, 1.0 + scale], axis=1)
        gate3 = gate[:, None, :]
        h, cq, ckv, kpe, gates, bq, bk, bv, cq2, ck, cv, cf = _ln_in(
            xl, ss, norm_g[l:l + 1], w["w_in"], seq, dep=rest0_token if l == 0 else None)
        gather1_token = None
        if l == 0:
            w.update(prepare_rest(*_split_wait(rest0, h, "gather_rest0_wait")[1]))
            gather1, gather1_token = _split_start("gather", shards(1), "gather_weights1_start", after=w["w_out"])
        q, k, v, cqn, ckvn = _mla_prep(cq, ckv, kpe, a_q_norm_g[l:l + 1], a_kv_norm_g[l:l + 1],
                                       w["wuq"], w["wk"], w["wv"], cos_t, sin_a, sin_b, dep=gather1_token)
        oa, sta = _attn_fwd("mla", q, k, v, None, seq, a_scale)
        table = _band_table(jnp.pad(b_rel_bias[l], ((0, 8 - B_HEADS), (0, GW - N_REL))))
        ob, stb = _band_fwd(bq, bk, bv, table, seq, h_scale)
        fcum = _fox_prep(cf, fb_pad[l:l + 1], seq)
        oc, stc = _attn_fwd("fox", cq2, ck, cv, fcum, seq, h_scale)
        xn, y, u = _gate_out(oa, ob, oc, gates, w["w_out"], xl, gate3, seq)
        saved.append(dict(x=xl, ss=ss, gate3=gate3, h=h, cq=cq, ckv=ckv, gates=gates, bq=bq, bk=bk, bv=bv,
                          cq2=cq2, ck=ck, cv=cv, cf=cf, q=q, k=k, v=v, cqn=cqn, ckvn=ckvn, oa=oa, sta=sta,
                          table=table, ob=ob, stb=stb, fcum=fcum, oc=oc, stc=stc, y=y, u=u))
        xl = xn

    dx, loss_lanes, g_final = _final_loss(xl, tgt, final_g[None, :])
    loss = lax.psum(jnp.sum(loss_lanes), AXES)

    rows = D_MODEL // N_DEV
    core = lax.axis_index("c").astype(jnp.int32).reshape(1)
    n_seg_a = 4
    dmods, smalls, parts = [None] * DEPTH, [None] * DEPTH, [None] * DEPTH
    pair1 = chips1 = pair1_token = chips1_token = None
    for l in reversed(range(DEPTH)):
        s, w = saved[l], weights[l]
        dy, doa, dob, doc, dgates, dgate = _gate_out_bwd(dx, s["y"], s["gate3"], s["oa"], s["ob"], s["oc"],
                                                         s["gates"], w["w_out"], seq, dep=pair1_token)
        g_out = _unpad_runs(_matmul_tn(s["u"], dy, "dw_out"), OUT_RUNS, 0)
        if l == 0:
            own, from_sib = _split_wait(pair1, g_out, "grads1_pair_wait")
            chips1, chips1_token = _split_start("chips", _pair_add(core, own, from_sib, "grads1_add"), "grads1_chips_start")
        dq, dk, dv = _attn_bwd("mla", s["q"], s["k"], s["v"], None, s["oa"], s["sta"], doa, seq, a_scale,
                               dep=chips1_token)
        dbq, dbk, dbv, gtab = _band_bwd(s["bq"], s["bk"], s["bv"], s["table"], s["ob"], s["stb"], dob, seq, h_scale,
                                        dep=chips1_token)
        g_rel = _band_table_bwd(gtab)[:, 0, :N_REL]
        dcq2, dck, dcv, dfc, dfq = _attn_bwd("fox", s["cq2"], s["ck"], s["cv"], s["fcum"], s["oc"], s["stc"], doc,
                                             seq, h_scale, dep=chips1_token)
        dcf, dfb = _fox_prep_bwd(dfc, dfq, s["cf"], fb_pad[l:l + 1], seq)
        dcq, dckv, dkpe, dqlin, dklin, dgq, dgkv = _mla_prep_bwd(
            dq, dk, dv, s["cq"], s["ckv"], a_q_norm_g[l:l + 1], a_kv_norm_g[l:l + 1],
            w["wuq_t"], w["wk_t"], w["wv_t"], cos_t, sin_a, sin_b)
        gq_pad = _matmul_tn(s["cqn"], dqlin, "dw_uq")
        g_uq = gq_pad.reshape(A_Q_RANK, A_HEADS, HEAD_PAD)[:, :, :A_NOPE + A_ROPE].reshape(A_Q_RANK, -1)
        gkv_pad = _matmul_tn(s["ckvn"], [dklin, dv], "dw_ukv")
        gk_pad = gkv_pad[:, :A_HEADS * HEAD_PAD].reshape(A_KV_RANK, A_HEADS, HEAD_PAD)[:, :, :A_NOPE]
        gv_pad = gkv_pad[:, A_HEADS * HEAD_PAD:].reshape(A_KV_RANK, A_HEADS, A_NOPE)
        g_ukv = jnp.concatenate([gk_pad, gv_pad], axis=2).reshape(A_KV_RANK, -1)
        dz = [dcq, dckv, dkpe, dgates, dbq, dbk, dbv, dcq2, dck, dcv, dcf]
        g_in_a = _matmul_tn(s["h"], dz[:n_seg_a], "dw_in_a")
        first = [g_in_a.reshape(N_DEV, rows, -1), g_out.reshape(N_DEV, rows, D_MODEL),
                 g_uq.reshape(A_Q_RANK, N_DEV, -1).transpose(1, 0, 2), g_ukv.reshape(A_KV_RANK, N_DEV, -1).transpose(1, 0, 2)]
        if l == 1:
            g_in_b = _matmul_tn(s["h"], dz[n_seg_a:], "dw_in_b")
            pair1, pair1_token = _split_start("pair", first + [g_in_b.reshape(N_DEV, rows, -1)], "grads1_pair_start")
            tail_token = None
        else:
            pair0a, pair0a_token = _split_start("pair", first, "grads0a_pair_start")
            g_in_b = _matmul_tn(s["h"], dz[n_seg_a:], "dw_in_b", dep=pair0a_token)
            own, from_sib = _split_wait(pair0a, g_in_b, "grads0a_pair_wait")
            sums0a = _pair_add(core, own, from_sib, "grads0a_add")
            pair0b, pair0b_token = _split_start("pair", [g_in_b.reshape(N_DEV, rows, -1)], "grads0b_pair_start",
                                                after=sums0a[0])
            chips0a, tail_token = _split_start("chips", sums0a, "grads0a_chips_start", after=pair0b_token)
        dx, dss, dg_norm = _ln_in_bwd(dz, w["w_in"], s["x"], s["ss"], norm_g[l:l + 1], dx, seq, dep=tail_token)
        dmods[l] = jnp.concatenate([dss[:, 0, :], dss[:, 1, :], dgate[:, 0, :]], axis=1)
        smalls[l] = [dg_norm.reshape(-1), dgq.reshape(-1), dgkv.reshape(-1), g_rel.reshape(-1),
                     dfb[0, :C_HEADS]]
    grad_x = dx.reshape(nb, seq, D_MODEL)
    parts[1] = _split_wait(chips1, dx, "grads1_chips_wait")[1]
    parts0a = _split_wait(chips0a, dx, "grads0a_chips_wait")[1]
    own, from_sib = _split_wait(pair0b, dx, "grads0b_pair_wait")

    small = jnp.concatenate([p for l in range(DEPTH) for p in smalls[l]] + [g_final.reshape(-1)])
    n_small = small.shape[0]
    small_rows = -(-n_small // 1024) * 8
    small = jnp.pad(small, (0, small_rows * 128 - n_small)).reshape(small_rows, 128)
    dmod_local = jnp.stack(dmods)
    dmod_g, small_g = _gather([dmod_local, small], "gather_small", dep=parts0a[0])
    chips0, chips0_token = _split_start("chips", _pair_add(core, own, from_sib, "grads0b_add"), "grads0b_chips_start",
                                        after=small_g)
    dmod_all = jnp.transpose(dmod_g, (1, 0, 2, 3)).reshape(DEPTH, N_DEV * nb, 3 * D_MODEL)
    cols = 3 * D_MODEL // N_DEV
    dmod_mine = lax.dynamic_slice_in_dim(dmod_all, me * cols, cols, axis=2)
    g_w_ada, g_b_ada = _ada_bwd(c_act, dmod_all, dmod_mine, chips0_token)
    small_sum = _sum_slots(small_g, "sum_small").reshape(-1)

    def split_small():
        out, pos = [], 0
        sizes = [D_MODEL, A_Q_RANK, A_KV_RANK, B_HEADS * N_REL, C_HEADS]
        per_layer = []
        for l in range(DEPTH):
            parts = []
            for sz in sizes:
                parts.append(small_sum[pos:pos + sz])
                pos += sz
            per_layer.append(parts)
        for j in range(len(sizes)):
            out.append(jnp.stack([per_layer[l][j] for l in range(DEPTH)]))
        out.append(small_sum[pos:pos + D_MODEL])
        return out

    g_norm, g_qn, g_kvn, g_relb, g_fb, g_fin = split_small()

    def adam(w, g, m, v, name, tr=None):
        shp = w.shape
        w3 = w.reshape((1,) * (3 - w.ndim) + shp)
        outs = _adamw(w3, g.reshape((-1,) + w3.shape), m.reshape(w3.shape), v.reshape(w3.shape), name, tr)
        return [o.reshape(shp) for o in outs]

    res = {
        "w_ada": adam(w_ada, g_w_ada, m_w_ada, v_w_ada, "adam_w_ada", 256),
        "b_ada": adam(b_ada, g_b_ada, m_b_ada, v_b_ada, "adam_b_ada"),
        "norm_g": adam(norm_g, g_norm, m_norm_g, v_norm_g, "adam_norm_g"),
        "a_q_norm_g": adam(a_q_norm_g, g_qn, m_a_q_norm_g, v_a_q_norm_g, "adam_q_norm"),
        "a_kv_norm_g": adam(a_kv_norm_g, g_kvn, m_a_kv_norm_g, v_a_kv_norm_g, "adam_kv_norm"),
        "b_rel_bias": adam(b_rel_bias, g_relb.reshape(b_rel_bias.shape), m_b_rel_bias, v_b_rel_bias, "adam_rel_bias"),
        "c_forget_b": adam(c_forget_b, g_fb, m_c_forget_b, v_c_forget_b, "adam_forget_b"),
        "final_g": adam(final_g, g_fin, m_final_g, v_final_g, "adam_final_g"),
    }
    parts[0] = list(parts0a) + list(_split_wait(chips0, res["w_ada"][1], "grads0b_chips_wait")[1])
    p_in = jnp.stack([_unpad_runs(jnp.concatenate([parts[l][0], parts[l][4]], axis=2), IN_RUNS, 2)
                      for l in range(DEPTH)], axis=1)
    p_out, p_uq, p_ukv = (jnp.stack([parts[l][j] for l in range(DEPTH)], axis=1) for j in (1, 2, 3))
    res.update({
        "w_in": adam(w_in, p_in, m_w_in, v_w_in, "adam_w_in", 64),
        "a_w_uq": adam(a_w_uq, p_uq, m_a_w_uq, v_a_w_uq, "adam_w_uq"),
        "a_w_ukv": adam(a_w_ukv, p_ukv, m_a_w_ukv, v_a_w_ukv, "adam_w_ukv"),
        "w_out": adam(w_out, p_out, m_w_out, v_w_out, "adam_w_out", 64),
    })
    names = ["w_ada", "b_ada", "norm_g", "w_in", "a_q_norm_g", "a_w_uq", "a_kv_norm_g", "a_w_ukv", "b_rel_bias",
             "c_forget_b", "w_out", "final_g"]
    outs = [loss, grad_x]
    for j in range(4):
        outs += [res[n][j] for n in names]
    return tuple(outs)
```

```python
import math

import jax
import jax.numpy as jnp
from jax import lax
from jax.experimental import pallas as pl
from jax.experimental.pallas import tpu as pltpu

F32 = jnp.float32
BF16 = jnp.bfloat16
HI = lax.Precision.HIGHEST

N_DEV = 8
AXES = ("x", "y", "c")
D_MODEL = 1024
DEPTH = 2
CHUNK = 64
EPS = 1e-6
NEG = -1e30
A_HEADS = 6
A_NOPE = 64
A_ROPE = 32
A_Q_RANK = 384
A_KV_RANK = 256
ROPE_THETA = 10000.0
B_HEADS = 5
B_LEFT = 512
REL_CLIP = 128
N_REL = 2 * REL_CLIP + 1
C_HEADS = 5
HEAD_PAD = 128
GW = 384
N_IN = 3621
ADAM_LR = 0.001
ADAM_B1 = 0.9
ADAM_B2 = 0.999
ADAM_EPS = 1e-08
ADAM_WD = 0.01
ADAM_STEP = 10
VMEM_LIMIT = 56 * 1024 * 1024
ROW_TILE = 512

Z_SEGS = (
    ("cq", 0, 384, F32), ("ckv", 384, 256, F32), ("kpe", 640, 128, F32), ("gates", 768, 1152, BF16),
    ("bq", 1920, 384, BF16), ("bk", 2304, 384, BF16), ("bv", 2688, 384, BF16),
    ("cq2", 3072, 384, BF16), ("ck", 3456, 384, BF16), ("cv", 3840, 384, BF16), ("cf", 4224, 128, F32),
)
N_PAD = 4352
IN_RUNS = (
    (0, 384, 0), (384, 256, 384), (640 + 64, 32, 640),
    (768, 384, 672), (768 + 384, 320, 2016), (768 + 768, 320, 3301),
    (1920, 320, 1056), (2304, 320, 1376), (2688, 320, 1696),
    (3072, 320, 2336), (3456, 320, 2656), (3840, 320, 2976), (4224, 5, 3296),
)
OUT_RUNS = ((0, 384, 0), (384, 320, 384), (768, 320, 704))
U_PAD = 1152


def _cparams(sem=None, vmem=VMEM_LIMIT):
    return pltpu.CompilerParams(dimension_semantics=sem, vmem_limit_bytes=vmem)


def _after(dep, body, in_specs, args):
    if dep is None:
        return body, in_specs, args
    n = len(args)

    def ordered(*refs):
        return body(*refs[:n], *refs[n + 1:])

    return ordered, list(in_specs) + [pl.BlockSpec((8, 128), lambda *_: (0, 0))], list(args) + [dep]


def _pad_runs(w, runs, total, axis):
    order = sorted(runs)
    parts, pos = [], 0
    for off, wd, src in order:
        if off > pos:
            shp = list(w.shape)
            shp[axis] = off - pos
            parts.append(jnp.zeros(shp, w.dtype))
        parts.append(lax.slice_in_dim(w, src, src + wd, axis=axis))
        pos = off + wd
    if pos < total:
        shp = list(w.shape)
        shp[axis] = total - pos
        parts.append(jnp.zeros(shp, w.dtype))
    return jnp.concatenate(parts, axis=axis)


def _unpad_runs(w, runs, axis):
    order = sorted(runs, key=lambda r: r[2])
    return jnp.concatenate([lax.slice_in_dim(w, off, off + wd, axis=axis) for off, wd, _ in order], axis=axis)


def _sigmoid(x):
    return 1.0 / (1.0 + jnp.exp(-x))


N_CHIP = 4
ANY_SPEC = pl.BlockSpec(memory_space=pl.ANY)
MESH_ID = pl.DeviceIdType.MESH


def _gather(arrs, name, dep=None):
    n = len(arrs)
    nin = n + (dep is not None)

    def body(*refs):
        ins, outs = refs[:n], refs[nin:nin + n]
        send_sems, recv_sems, local_sems = refs[nin + n:]
        x, y, c = lax.axis_index("x"), lax.axis_index("y"), lax.axis_index("c")
        me, sib = (x, y, c), (x, y, 1 - c)
        chips = [(1 - x, y), (x, 1 - y), (1 - x, 1 - y)]

        def slot(px, py, pc):
            return 4 * px + 2 * py + pc

        def copy(a, k, block, to, src=None):
            dst = outs[a].at[slot(*block)]
            return pltpu.make_async_remote_copy(
                src_ref=dst if src is None else src, dst_ref=dst, send_sem=send_sems.at[a, k],
                recv_sem=recv_sems.at[a, k], device_id=to, device_id_type=MESH_ID)

        local = [pltpu.make_async_copy(ins[a], outs[a].at[slot(*me)], local_sems.at[a]) for a in range(n)]
        first = []
        for a in range(n):
            first.append(copy(a, 0, me, sib, src=ins[a]))
            first += [copy(a, 1 + j, me, (*chip, c), src=ins[a]) for j, chip in enumerate(chips)]
        for cp in local + first:
            cp.start()
        passed = []
        for j, chip in enumerate(chips):
            for a in range(n):
                copy(a, 1 + j, (*chip, c), me).wait_recv()
                fwd = copy(a, 4 + j, (*chip, c), sib)
                fwd.start()
                passed.append(fwd)
        for a in range(n):
            copy(a, 0, sib, me).wait_recv()
            for j, chip in enumerate(chips):
                copy(a, 4 + j, (*chip, 1 - c), me).wait_recv()
        for cp in first + passed:
            cp.wait_send()
        for cp in local:
            cp.wait()

    return pl.pallas_call(
        body, name=name, out_shape=[jax.ShapeDtypeStruct((N_DEV,) + a.shape, a.dtype) for a in arrs],
        in_specs=[ANY_SPEC] * nin, out_specs=[ANY_SPEC] * n,
        scratch_shapes=[pltpu.SemaphoreType.DMA((n, N_DEV - 1)), pltpu.SemaphoreType.DMA((n, N_DEV - 1)),
                        pltpu.SemaphoreType.DMA((n,))],
    )(*arrs, *([] if dep is None else [dep]))


HBM_SPEC = pl.BlockSpec(memory_space=pltpu.HBM)
SEM_SPEC = pl.BlockSpec(memory_space=pltpu.SEMAPHORE)
SPLIT_EFFECT = pltpu.SideEffectType.DATAFLOW_SIDE_EFFECTING
SPLIT_SEMS = {"gather": (N_DEV - 1, True), "pair": (N_CHIP, False), "chips": (N_CHIP - 1, True)}


def _split_descriptors(pattern, srcs, lands, sems):
    x, y, c = lax.axis_index("x"), lax.axis_index("y"), lax.axis_index("c")
    nsem, has_local = SPLIT_SEMS[pattern]
    per = 2 * nsem + int(has_local)
    starts, arrivals, local = [], [], []

    def remote(a, k, src, dst, to):
        return pltpu.make_async_remote_copy(src_ref=src, dst_ref=dst, send_sem=sems[a * per + k],
                                            recv_sem=sems[a * per + nsem + k], device_id=to, device_id_type=MESH_ID)

    for a in range(len(srcs)):
        if pattern == "gather":
            me = 4 * x + 2 * y + c
            local.append(pltpu.make_async_copy(srcs[a], lands[a].at[me], sems[a * per + 2 * nsem]))
            for k in range(1, N_DEV):
                px = (1 - x) if (k >> 2) & 1 else x
                py = (1 - y) if (k >> 1) & 1 else y
                pc = (1 - c) if k & 1 else c
                starts.append(remote(a, k - 1, srcs[a], lands[a].at[me], (px, py, pc)))
                arrivals.append(remote(a, k - 1, srcs[a], lands[a].at[4 * px + 2 * py + pc], (px, py, pc)))
        elif pattern == "pair":
            for q in range(N_CHIP):
                cp = remote(a, q, srcs[a].at[2 * q + 1 - c], lands[a].at[q], (x, y, 1 - c))
                starts.append(cp)
                arrivals.append(cp)
        else:
            mine = 2 * x + y
            local.append(pltpu.make_async_copy(srcs[a].at[mine], lands[a].at[mine], sems[a * per + 2 * nsem]))
            for k in range(1, N_CHIP):
                px = (1 - x) if (k >> 1) & 1 else x
                py = (1 - y) if k & 1 else y
                starts.append(remote(a, k - 1, srcs[a].at[2 * px + py], lands[a].at[mine], (px, py, c)))
                arrivals.append(remote(a, k - 1, srcs[a].at[2 * px + py], lands[a].at[2 * px + py], (px, py, c)))
    return starts, arrivals, local


def _split_start(pattern, arrs, name, after=None):
    n = len(arrs)
    extra = [] if after is None else [after]
    nsem, has_local = SPLIT_SEMS[pattern]
    if pattern == "gather":
        land_shapes = [(N_DEV,) + a.shape for a in arrs]
    elif pattern == "pair":
        land_shapes = [(N_CHIP,) + a.shape[1:] for a in arrs]
    else:
        land_shapes = [a.shape for a in arrs]
    nsem_out = n * (2 * nsem + int(has_local))

    def body(*refs):
        srcs, lands = refs[:n], refs[n:2 * n]
        first_sem = 2 * n + len(extra)
        sems = refs[first_sem:first_sem + nsem_out]
        token = refs[-1]
        starts, _, local = _split_descriptors(pattern, srcs, lands, sems)
        for cp in local + starts:
            cp.start()
        token[...] = jnp.zeros_like(token)

    out_shape = ([pltpu.SemaphoreType.DMA(())] * nsem_out + [pltpu.HBM(a.shape, a.dtype) for a in arrs]
                 + [pltpu.HBM(s, a.dtype) for s, a in zip(land_shapes, arrs)] + [jax.ShapeDtypeStruct((8, 128), F32)])
    ins = ([pltpu.with_memory_space_constraint(a, pltpu.HBM) for a in arrs]
           + [pltpu.with_memory_space_constraint(lax.empty(s, a.dtype), pltpu.HBM) for s, a in zip(land_shapes, arrs)])
    outs = pl.pallas_call(
        body, name=name, out_shape=out_shape, in_specs=[HBM_SPEC] * (2 * n) + [ANY_SPEC] * len(extra),
        out_specs=[SEM_SPEC] * nsem_out + [HBM_SPEC] * (2 * n) + [pl.BlockSpec(memory_space=pltpu.VMEM)],
        input_output_aliases={i: nsem_out + i for i in range(2 * n)},
        compiler_params=pltpu.CompilerParams(has_side_effects=SPLIT_EFFECT),
    )(*ins, *extra)
    handle = dict(pattern=pattern, n=n, sems=outs[:nsem_out], srcs=outs[nsem_out:nsem_out + n],
                  lands=outs[nsem_out + n:nsem_out + 2 * n])
    return handle, outs[-1]


def _split_wait(handle, after, name):
    pattern, n = handle["pattern"], handle["n"]
    nsem_in = len(handle["sems"])

    def body(*refs):
        srcs, lands = refs[:n], refs[n:2 * n]
        starts, arrivals, local = _split_descriptors(pattern, srcs, lands, refs[2 * n:2 * n + nsem_in])
        for cp in starts:
            cp.wait_send()
        for cp in arrivals:
            cp.wait_recv()
        for cp in local:
            cp.wait()

    srcs, lands = handle["srcs"], handle["lands"]
    outs = pl.pallas_call(
        body, name=name,
        out_shape=[pltpu.HBM(a.shape, a.dtype) for a in srcs] + [pltpu.HBM(a.shape, a.dtype) for a in lands],
        in_specs=[HBM_SPEC] * (2 * n) + [SEM_SPEC] * nsem_in + [ANY_SPEC], out_specs=[HBM_SPEC] * (2 * n),
        input_output_aliases={i: i for i in range(2 * n)},
        compiler_params=pltpu.CompilerParams(has_side_effects=SPLIT_EFFECT),
    )(*srcs, *lands, *handle["sems"], after)
    return outs[:n], outs[n:]


def _pair_add(core, a8s, b4s, name):
    n = len(a8s)

    def body(core_ref, *refs):
        for i in range(n):
            refs[2 * n + i][...] = (refs[i][...] + refs[n + i][...]).astype(BF16)

    own = [pl.BlockSpec((1,) + b.shape[1:], lambda q, core_ref: (2 * q + core_ref[0], 0, 0)) for b in b4s]
    slot = [pl.BlockSpec((1,) + b.shape[1:], lambda q, core_ref: (q, 0, 0)) for b in b4s]
    grid_spec = pltpu.PrefetchScalarGridSpec(num_scalar_prefetch=1, grid=(N_CHIP,), in_specs=own + slot, out_specs=slot)
    return pl.pallas_call(
        body, name=name, grid_spec=grid_spec, out_shape=[jax.ShapeDtypeStruct(b.shape, BF16) for b in b4s],
        compiler_params=_cparams(("arbitrary",)),
    )(core, *a8s, *b4s)


def _sum_slots(x, name):
    _, r, c = x.shape

    def body(x_ref, o_ref):
        acc = x_ref[0]
        for j in range(1, N_DEV):
            acc = acc + x_ref[j]
        o_ref[...] = acc

    return pl.pallas_call(body, name=name, out_shape=jax.ShapeDtypeStruct((r, c), F32))(x)


def _ada_fwd(c_all, w_ada):
    nb = c_all.shape[0]
    cols = w_ada.shape[2]

    def body(c_ref, w_ref, act_ref, mod_ref):
        cv = c_ref[...]
        act = cv * _sigmoid(cv)
        act_ref[...] = act
        for l in range(DEPTH):
            mod_ref[l] = jnp.dot(act, w_ref[l], precision=HI, preferred_element_type=F32)

    return pl.pallas_call(
        body, name="ada_fwd",
        out_shape=[jax.ShapeDtypeStruct((nb, D_MODEL), F32), jax.ShapeDtypeStruct((DEPTH, nb, cols), F32)],
        compiler_params=_cparams(),
    )(c_all, w_ada)


def _ada_bwd(c_act, dmod_all, dmod_mine, dep):
    nb = c_act.shape[0]
    cols = dmod_mine.shape[2]

    def body(act_ref, dall_ref, dmine_ref, dep_ref, gw_ref, gb_ref):
        act = act_ref[...]
        for l in range(DEPTH):
            gw_ref[l] = lax.dot_general(act, dmine_ref[l], (((0,), (0,)), ((), ())),
                                        precision=HI, preferred_element_type=F32)
            gb_ref[l:l + 1, :] = jnp.sum(dall_ref[l], axis=0, keepdims=True)

    return pl.pallas_call(
        body, name="ada_bwd",
        out_shape=[jax.ShapeDtypeStruct((DEPTH, D_MODEL, cols), F32),
                   jax.ShapeDtypeStruct((DEPTH, 3 * D_MODEL), F32)],
        compiler_params=_cparams(),
    )(c_act, dmod_all, dmod_mine, dep)


def _ln_in(x, ss, g, w, seq, tm=ROW_TILE, dep=None):
    t = x.shape[0]
    tm = min(tm, seq)
    tps = seq // tm

    def body(x_ref, ss_ref, g_ref, w_ref, h_ref, *outs):
        xv = x_ref[...]
        xn = xv * lax.rsqrt(jnp.mean(xv * xv, axis=-1, keepdims=True) + EPS)
        h = xn * g_ref[...] * ss_ref[0, 1:2, :] + ss_ref[0, 0:1, :]
        hb = h.astype(BF16)
        h_ref[...] = hb
        z = jnp.dot(hb, w_ref[...], preferred_element_type=F32)
        for o_ref, (_, off, wd, _) in zip(outs, Z_SEGS):
            o_ref[...] = z[:, off:off + wd].astype(o_ref.dtype)

    row = lambda wd: pl.BlockSpec((tm, wd), lambda i: (i, 0))
    in_specs = [row(D_MODEL), pl.BlockSpec((1, 2, D_MODEL), lambda i: (i // tps, 0, 0)),
                pl.BlockSpec((1, D_MODEL), lambda i: (0, 0)), pl.BlockSpec((D_MODEL, N_PAD), lambda i: (0, 0))]
    body, in_specs, args = _after(dep, body, in_specs, [x, ss, g, w])
    return pl.pallas_call(
        body, name="ln_in", grid=(t // tm,), in_specs=in_specs,
        out_specs=[row(D_MODEL)] + [row(wd) for _, _, wd, _ in Z_SEGS],
        out_shape=[jax.ShapeDtypeStruct((t, D_MODEL), BF16)]
        + [jax.ShapeDtypeStruct((t, wd), dt) for _, _, wd, dt in Z_SEGS],
        compiler_params=_cparams(("arbitrary",)),
    )(*args)


def _ln_in_bwd(dz, w_t, x, ss, g, dxo, seq, tm=ROW_TILE, dep=None):
    t = x.shape[0]
    tm = min(tm, seq)
    tps = seq // tm
    nb = t // seq
    nz = len(Z_SEGS)

    def body(*refs):
        dz_refs = refs[:nz]
        wt_ref, x_ref, ss_ref, g_ref, dxo_ref, dx_ref, dss_ref, dg_ref = refs[nz:]
        i = pl.program_id(0)
        dzc = jnp.concatenate([r[...].astype(BF16) for r in dz_refs], axis=1)
        dh = _nt(dzc, wt_ref[...])
        xv = x_ref[...]
        rstd = lax.rsqrt(jnp.mean(xv * xv, axis=-1, keepdims=True) + EPS)
        xn = xv * rstd
        gv = g_ref[...]
        s1 = ss_ref[0, 1:2, :]
        dxg = dh * s1
        dxn = dxg * gv
        dx = rstd * (dxn - xn * jnp.mean(dxn * xn, axis=-1, keepdims=True))
        dx_ref[...] = dxo_ref[...] + dx
        dshift = jnp.sum(dh, axis=0, keepdims=True)
        dscale = jnp.sum(dh * (xn * gv), axis=0, keepdims=True)
        dgp = jnp.sum(dxg * xn, axis=0, keepdims=True)

        @pl.when(i % tps == 0)
        def _():
            dss_ref[0, 0:1, :] = dshift
            dss_ref[0, 1:2, :] = dscale

        @pl.when(i % tps != 0)
        def _():
            dss_ref[0, 0:1, :] += dshift
            dss_ref[0, 1:2, :] += dscale

        @pl.when(i == 0)
        def _():
            dg_ref[...] = dgp

        @pl.when(i != 0)
        def _():
            dg_ref[...] += dgp

    row = lambda wd: pl.BlockSpec((tm, wd), lambda i: (i, 0))
    in_specs = ([row(wd) for _, _, wd, _ in Z_SEGS]
                + [pl.BlockSpec((D_MODEL, N_PAD), lambda i: (0, 0)), row(D_MODEL),
                   pl.BlockSpec((1, 2, D_MODEL), lambda i: (i // tps, 0, 0)),
                   pl.BlockSpec((1, D_MODEL), lambda i: (0, 0)), row(D_MODEL)])
    body, in_specs, args = _after(dep, body, in_specs, [*dz, w_t, x, ss, g, dxo])
    return pl.pallas_call(
        body, name="ln_in_bwd", grid=(t // tm,), in_specs=in_specs,
        out_specs=[row(D_MODEL), pl.BlockSpec((1, 2, D_MODEL), lambda i: (i // tps, 0, 0)),
                   pl.BlockSpec((1, D_MODEL), lambda i: (0, 0))],
        out_shape=[jax.ShapeDtypeStruct((t, D_MODEL), F32), jax.ShapeDtypeStruct((nb, 2, D_MODEL), F32),
                   jax.ShapeDtypeStruct((1, D_MODEL), F32)],
        compiler_params=_cparams(("arbitrary",)),
    )(*args)


def _matmul_tn(a, bs, name, tm=2048, dep=None):
    bs = list(bs) if isinstance(bs, (list, tuple)) else [bs]
    t, k = a.shape
    widths = [b.shape[1] for b in bs]
    n = sum(widths)
    tm = min(tm, t)

    def body(a_ref, *refs):
        b_refs, o_ref = refs[:-1], refs[-1]
        i = pl.program_id(0)
        av = a_ref[...].astype(BF16)
        parts = [b_ref[...].astype(BF16) for b_ref in b_refs]
        bv = parts[0] if len(parts) == 1 else jnp.concatenate(parts, axis=1)
        part = lax.dot_general(av, bv, (((0,), (0,)), ((), ())), preferred_element_type=F32)

        @pl.when(i == 0)
        def _():
            o_ref[...] = part

        @pl.when(i != 0)
        def _():
            o_ref[...] += part

    in_specs = [pl.BlockSpec((tm, k), lambda i: (i, 0))] + [pl.BlockSpec((tm, wd), lambda i: (i, 0)) for wd in widths]
    body, in_specs, args = _after(dep, body, in_specs, [a, *bs])
    return pl.pallas_call(
        body, name=name, grid=(t // tm,), in_specs=in_specs,
        out_specs=pl.BlockSpec((k, n), lambda i: (0, 0)),
        out_shape=jax.ShapeDtypeStruct((k, n), F32),
        compiler_params=_cparams(("arbitrary",)),
    )(*args)


def _rope(blk, cos_t, sin_a, sin_b):
    return blk * cos_t + pltpu.roll(blk, 112, 1) * sin_a + pltpu.roll(blk, 16, 1) * sin_b


def _unrope(d, cos_t, sin_a, sin_b):
    return d * cos_t + pltpu.roll(d * sin_a, 16, 1) + pltpu.roll(d * sin_b, 112, 1)


def _mla_prep(cq, ckv, kpe, gq, gkv, wuq, wk, wv, cos_t, sin_a, sin_b, tm=ROW_TILE, dep=None):
    t = cq.shape[0]
    tm = min(tm, t)
    qw = A_HEADS * HEAD_PAD

    def body(cq_ref, ckv_ref, kpe_ref, gq_ref, gkv_ref, wuq_ref, wk_ref, wv_ref, c_ref, sa_ref, sb_ref,
             q_ref, k_ref, v_ref, cqn_ref, ckvn_ref):
        ct, sa, sb = c_ref[...], sa_ref[...], sb_ref[...]
        a = cq_ref[...]
        cqn = (a * lax.rsqrt(jnp.mean(a * a, axis=-1, keepdims=True) + EPS) * gq_ref[...]).astype(BF16)
        cqn_ref[...] = cqn
        b = ckv_ref[...]
        ckvn = (b * lax.rsqrt(jnp.mean(b * b, axis=-1, keepdims=True) + EPS) * gkv_ref[...]).astype(BF16)
        ckvn_ref[...] = ckvn
        qlin = jnp.dot(cqn, wuq_ref[...], preferred_element_type=F32)
        klin = jnp.dot(ckvn, wk_ref[...], preferred_element_type=F32)
        v_ref[...] = jnp.dot(ckvn, wv_ref[...], preferred_element_type=F32).astype(BF16)
        kr = _rope(kpe_ref[...], ct, sa, sb)
        for h in range(A_HEADS):
            sl = slice(h * HEAD_PAD, (h + 1) * HEAD_PAD)
            q_ref[:, sl] = _rope(qlin[:, sl], ct, sa, sb).astype(BF16)
            k_ref[:, sl] = (klin[:, sl] + kr).astype(BF16)

    row = lambda wd: pl.BlockSpec((tm, wd), lambda i: (i, 0))
    full = lambda r, c: pl.BlockSpec((r, c), lambda i: (0, 0))
    in_specs = [row(A_Q_RANK), row(A_KV_RANK), row(128), full(1, A_Q_RANK), full(1, A_KV_RANK),
                full(A_Q_RANK, qw), full(A_KV_RANK, qw), full(A_KV_RANK, GW), row(128), row(128), row(128)]
    body, in_specs, args = _after(dep, body, in_specs, [cq, ckv, kpe, gq, gkv, wuq, wk, wv, cos_t, sin_a, sin_b])
    return pl.pallas_call(
        body, name="mla_prep", grid=(t // tm,), in_specs=in_specs,
        out_specs=[row(qw), row(qw), row(GW), row(A_Q_RANK), row(A_KV_RANK)],
        out_shape=[jax.ShapeDtypeStruct((t, qw), BF16), jax.ShapeDtypeStruct((t, qw), BF16),
                   jax.ShapeDtypeStruct((t, GW), BF16), jax.ShapeDtypeStruct((t, A_Q_RANK), BF16),
                   jax.ShapeDtypeStruct((t, A_KV_RANK), BF16)],
        compiler_params=_cparams(("arbitrary",)),
    )(*args)


def _mla_prep_bwd(dq, dk, dv, cq, ckv, gq, gkv, wuq_t, wk_t, wv_t, cos_t, sin_a, sin_b, tm=ROW_TILE):
    t = cq.shape[0]
    tm = min(tm, t)
    qw = A_HEADS * HEAD_PAD

    def body(dq_ref, dk_ref, dv_ref, cq_ref, ckv_ref, gq_ref, gkv_ref, wuqt_ref, wkt_ref, wvt_ref,
             c_ref, sa_ref, sb_ref, dcq_ref, dckv_ref, dkpe_ref, dql_ref, dkl_ref, dgq_ref, dgkv_ref):
        i = pl.program_id(0)
        ct, sa, sb = c_ref[...], sa_ref[...], sb_ref[...]
        lane = lax.broadcasted_iota(jnp.int32, (1, HEAD_PAD), 1)
        nope = lane < A_NOPE
        rope = (lane >= A_NOPE) & (lane < A_NOPE + A_ROPE)
        dksum = None
        for h in range(A_HEADS):
            sl = slice(h * HEAD_PAD, (h + 1) * HEAD_PAD)
            dql_ref[:, sl] = _unrope(dq_ref[:, sl].astype(F32), ct, sa, sb).astype(BF16)
            dkh = dk_ref[:, sl].astype(F32)
            dkl_ref[:, sl] = jnp.where(nope, dkh, 0.0).astype(BF16)
            dksum = dkh if dksum is None else dksum + dkh
        dkpe_ref[...] = jnp.where(rope, _unrope(jnp.where(rope, dksum, 0.0), ct, sa, sb), 0.0).astype(BF16)
        dcqn = jnp.dot(dql_ref[...], wuqt_ref[...], preferred_element_type=F32)
        dckvn = (jnp.dot(dkl_ref[...], wkt_ref[...], preferred_element_type=F32)
                 + jnp.dot(dv_ref[...].astype(BF16), wvt_ref[...], preferred_element_type=F32))

        def norm_bwd(xv, gv, dy):
            rstd = lax.rsqrt(jnp.mean(xv * xv, axis=-1, keepdims=True) + EPS)
            xn = xv * rstd
            dxn = dy * gv
            dx = rstd * (dxn - xn * jnp.mean(dxn * xn, axis=-1, keepdims=True))
            return dx, jnp.sum(dy * xn, axis=0, keepdims=True)

        dcq, dgq = norm_bwd(cq_ref[...], gq_ref[...], dcqn)
        dckv, dgkv = norm_bwd(ckv_ref[...], gkv_ref[...], dckvn)
        dcq_ref[...] = dcq.astype(BF16)
        dckv_ref[...] = dckv.astype(BF16)

        @pl.when(i == 0)
        def _():
            dgq_ref[...] = dgq
            dgkv_ref[...] = dgkv

        @pl.when(i != 0)
        def _():
            dgq_ref[...] += dgq
            dgkv_ref[...] += dgkv

    row = lambda wd: pl.BlockSpec((tm, wd), lambda i: (i, 0))
    full = lambda r, c: pl.BlockSpec((r, c), lambda i: (0, 0))
    return pl.pallas_call(
        body, name="mla_prep_bwd", grid=(t // tm,),
        in_specs=[row(qw), row(qw), row(GW), row(A_Q_RANK), row(A_KV_RANK), full(1, A_Q_RANK), full(1, A_KV_RANK),
                  full(qw, A_Q_RANK), full(qw, A_KV_RANK), full(GW, A_KV_RANK), row(128), row(128), row(128)],
        out_specs=[row(A_Q_RANK), row(A_KV_RANK), row(128), row(qw), row(qw), full(1, A_Q_RANK), full(1, A_KV_RANK)],
        out_shape=[jax.ShapeDtypeStruct((t, A_Q_RANK), BF16), jax.ShapeDtypeStruct((t, A_KV_RANK), BF16),
                   jax.ShapeDtypeStruct((t, 128), BF16), jax.ShapeDtypeStruct((t, qw), BF16),
                   jax.ShapeDtypeStruct((t, qw), BF16), jax.ShapeDtypeStruct((1, A_Q_RANK), F32),
                   jax.ShapeDtypeStruct((1, A_KV_RANK), F32)],
        compiler_params=_cparams(("arbitrary",)),
    )(dq, dk, dv, cq, ckv, gq, gkv, wuq_t, wk_t, wv_t, cos_t, sin_a, sin_b)


def _nt(a, b):
    return lax.dot_general(a, b, (((1,), (1,)), ((), ())), preferred_element_type=F32)


def _tn(a, b):
    return lax.dot_general(a, b, (((0,), (0,)), ((), ())), preferred_element_type=F32)


def _causal_mask(kind, q0, k0, tq, tk):
    qpos = q0 + lax.broadcasted_iota(jnp.int32, (tq, tk), 0)
    kpos = k0 + lax.broadcasted_iota(jnp.int32, (tq, tk), 1)
    if kind == "mla":
        return lax.shift_right_logical(kpos, 6) <= lax.shift_right_logical(qpos, 6)
    return kpos <= qpos


def _attn_fwd(kind, q, k, v, f, seq, scale, tq=512, tk=512):
    t = v.shape[0]
    nb = t // seq
    nq = seq // tq
    hw = 256 if kind == "mla" else 128
    n_heads = A_HEADS if kind == "mla" else C_HEADS
    use_f = f is not None
    tq, tk = min(tq, seq), min(tk, seq)
    nq = seq // tq
    assert tk == tq

    def body(*refs):
        if use_f:
            q_ref, k_ref, v_ref, f_ref, o_ref, st_ref = refs
        else:
            q_ref, k_ref, v_ref, o_ref, st_ref = refs
        qi = pl.program_id(2)
        q0 = qi * tq
        lane = lax.broadcasted_iota(jnp.int32, (1, 128), 1)
        half = lane >= 64
        qall = q_ref[...]
        if kind == "mla":
            qhs = [qall[:, 0:128], qall[:, 128:256]]
            post = scale * math.log2(math.e)
        else:
            assert math.frexp(scale)[0] == 0.5
            qall = qall * jnp.asarray(scale, BF16)
            qhs = [jnp.where(half, jnp.zeros_like(qall), qall), jnp.where(half, qall, jnp.zeros_like(qall))]
            post = None
        kd = pl.multiple_of(q0, tq)
        diag = _causal_mask(kind, 0, 0, tq, tk)

        def block(j, k0, state, masked):
            m, l, acc = state
            kh = k_ref[pl.ds(k0, tk), j * 128:(j + 1) * 128] if kind == "mla" else k_ref[pl.ds(k0, tk), :]
            s = _nt(qhs[j], kh)
            if post is not None:
                s = s * post
            if use_f:
                s = s - f_ref[0, 0, j:j + 1, pl.ds(k0, tk)]
            if masked:
                s = jnp.where(diag, s, NEG)
            mn = jnp.maximum(m, jnp.max(s, axis=-1, keepdims=True))
            alpha = jnp.exp2(m - mn) if post is not None else jnp.exp(m - mn)
            p = jnp.exp2(s - mn) if post is not None else jnp.exp(s - mn)
            l = alpha * l + jnp.sum(p, axis=-1, keepdims=True)
            acc = alpha * acc + jnp.dot(p.astype(BF16), v_ref[pl.ds(k0, tk), :], preferred_element_type=F32)
            return mn, l, acc

        def run(heads):
            def kstep(kb, carry):
                k0 = pl.multiple_of(kb * tk, tk)
                out = ()
                for n, j in enumerate(heads):
                    out += block(j, k0, carry[3 * n:3 * n + 3], False)
                return out

            init = (jnp.full((tq, 1), NEG, F32), jnp.zeros((tq, 1), F32), jnp.zeros((tq, 128), F32)) * len(heads)
            carry = lax.fori_loop(0, qi, kstep, init)
            o, st = jnp.zeros((tq, 128), F32), jnp.zeros((tq, 128), F32)
            for n, j in enumerate(heads):
                m, l, acc = block(j, kd, carry[3 * n:3 * n + 3], True)
                o = jnp.where(half == bool(j), acc / l, o)
                if post is not None:
                    m = m * math.log(2.0)
                st = jnp.where(lane == j, m + jnp.log(l), st)
            o_ref[...] = o.astype(BF16)
            st_ref[...] = st

        if n_heads % 2 == 0:
            run((0, 1))
        else:
            last = pl.program_id(1) == n_heads // 2
            pl.when(jnp.logical_not(last))(lambda: run((0, 1)))
            pl.when(last)(lambda: run((0,)))

    in_specs = [pl.BlockSpec((tq, hw), lambda b, p, i: (b * nq + i, p)),
                pl.BlockSpec((seq, hw), lambda b, p, i: (b, p)),
                pl.BlockSpec((seq, 128), lambda b, p, i: (b, p))]
    args = [q, k, v]
    if use_f:
        in_specs.append(pl.BlockSpec((1, 1, 8, seq), lambda b, p, i: (b, p, 0, 0)))
        args.append(f)
    oblk = pl.BlockSpec((tq, 128), lambda b, p, i: (b * nq + i, p))
    return pl.pallas_call(
        body, name="attn_fwd_" + kind, grid=(nb, 3, nq), in_specs=in_specs, out_specs=[oblk, oblk],
        out_shape=[jax.ShapeDtypeStruct((t, GW), BF16), jax.ShapeDtypeStruct((t, GW), F32)],
        compiler_params=_cparams(("arbitrary", "arbitrary", "arbitrary")),
    )(*args)


def _attn_bwd(kind, q, k, v, f, o, st, do, seq, scale, tq=512, tk=512, dep=None):
    t = v.shape[0]
    nb = t // seq
    tq, tk = min(tq, seq), min(tk, seq)
    nq = seq // tq
    nk = seq // tk
    hw = 256 if kind == "mla" else 128
    n_heads = A_HEADS if kind == "mla" else C_HEADS
    use_f = f is not None
    assert tq == tk

    def body(*refs):
        if use_f:
            (q_ref, k_ref, v_ref, f_ref, o_ref, st_ref, do_ref, dq_out, dk_out, dv_out, df_ref, dfq_ref,
             dq_ref, dk_ref, dv_ref) = refs
        else:
            q_ref, k_ref, v_ref, o_ref, st_ref, do_ref, dq_out, dk_out, dv_out, dq_ref, dk_ref, dv_ref = refs
        kj = pl.program_id(2)
        lane = lax.broadcasted_iota(jnp.int32, (1, 128), 1)
        half = lane >= 64

        @pl.when(kj == 0)
        def _():
            dq_ref[...] = jnp.zeros_like(dq_ref)
            if use_f:
                dfq_ref[...] = jnp.zeros_like(dfq_ref)

        dk_ref[...] = jnp.zeros_like(dk_ref)
        dv_ref[...] = jnp.zeros_like(dv_ref)
        if use_f:
            df_ref[...] = jnp.zeros_like(df_ref)
        vv = v_ref[...]
        diag = _causal_mask(kind, 0, 0, tq, tk)

        def qstep(qi, masked):
            q0 = pl.multiple_of(qi * tq, tq)
            rows = pl.ds(q0, tq)
            dov = do_ref[rows, :]
            dd = dov.astype(F32) * o_ref[rows, :]
            stv = st_ref[rows, :]

            def one_head(j):
                hm = half == bool(j)
                delta = jnp.sum(jnp.where(hm, dd, 0.0), axis=-1, keepdims=True)
                lse = stv[:, j:j + 1]
                if kind == "mla":
                    cols = slice(j * 128, (j + 1) * 128)
                    qh = q_ref[rows, cols]
                    kh = k_ref[:, cols]
                else:
                    cols = slice(0, 128)
                    qa = q_ref[rows, :]
                    qh = jnp.where(hm, qa, jnp.zeros_like(qa))
                    kh = k_ref[...]
                s = _nt(qh, kh) * scale
                if use_f:
                    s = s - f_ref[0, 0, j:j + 1, :]
                if masked:
                    s = jnp.where(diag, s, NEG)
                p = jnp.exp(s - lse)
                doh = jnp.where(hm, dov, jnp.zeros_like(dov))
                ds = p * (_nt(doh, vv) - delta)
                dsb = (ds * scale).astype(BF16)
                dv_ref[...] += _tn(p.astype(BF16), doh)
                dk_ref[:, cols] += _tn(dsb, qh)
                dqc = jnp.dot(dsb, kh, preferred_element_type=F32)
                if kind != "mla":
                    dqc = jnp.where(hm, dqc, 0.0)
                dq_ref[rows, cols] += dqc
                if use_f:
                    df_ref[0, 0, j:j + 1, :] += -jnp.sum(ds, axis=0, keepdims=True)
                    dfq_ref[rows, :] += jnp.where(lane == j, jnp.sum(ds, axis=-1, keepdims=True), 0.0)

            def both():
                one_head(0)
                one_head(1)

            if n_heads % 2 == 0:
                both()
            else:
                last = pl.program_id(1) == n_heads // 2
                pl.when(jnp.logical_not(last))(both)
                pl.when(last)(lambda: one_head(0))

        qstep(kj, True)

        def rest(qi, carry):
            qstep(qi, False)
            return carry

        lax.fori_loop(kj + 1, nq, rest, 0)
        dk_out[...] = dk_ref[...].astype(BF16)
        dv_out[...] = dv_ref[...].astype(BF16)

        @pl.when(kj == nk - 1)
        def _():
            dq_out[...] = dq_ref[...].astype(BF16)

    full_q = lambda wd: pl.BlockSpec((seq, wd), lambda b, p, i: (b, p))
    kblk = lambda wd: pl.BlockSpec((tk, wd), lambda b, p, i: (b * nk + i, p))
    in_specs = [full_q(hw), kblk(hw), kblk(128)]
    args = [q, k, v]
    if use_f:
        in_specs.append(pl.BlockSpec((1, 1, 8, tk), lambda b, p, i: (b, p, 0, i)))
        args.append(f)
    in_specs += [full_q(128), full_q(128), full_q(128)]
    args += [o, st, do]
    out_specs = [full_q(hw), kblk(hw), kblk(128)]
    out_shape = [jax.ShapeDtypeStruct((t, 3 * hw), BF16), jax.ShapeDtypeStruct((t, 3 * hw), BF16),
                 jax.ShapeDtypeStruct((t, GW), BF16)]
    scratch = [pltpu.VMEM((seq, hw), F32), pltpu.VMEM((tk, hw), F32), pltpu.VMEM((tk, 128), F32)]
    if use_f:
        out_specs += [pl.BlockSpec((1, 1, 8, tk), lambda b, p, i: (b, p, 0, i)), full_q(128)]
        out_shape += [jax.ShapeDtypeStruct((nb, 3, 8, seq), F32), jax.ShapeDtypeStruct((t, GW), F32)]
    body, in_specs, args = _after(dep, body, in_specs, args)
    return pl.pallas_call(
        body, name="attn_bwd_" + kind, grid=(nb, 3, nk), in_specs=in_specs, out_specs=out_specs,
        out_shape=out_shape, scratch_shapes=scratch,
        compiler_params=_cparams(("arbitrary", "arbitrary", "arbitrary")),
    )(*args)


BQ = 256
BWIN = BQ + B_LEFT


def _band_geometry():
    r = lax.broadcasted_iota(jnp.int32, (BQ, BWIN), 0)
    j = lax.broadcasted_iota(jnp.int32, (BQ, BWIN), 1)
    rc = lax.shift_right_logical(r, 6)
    jc = lax.shift_right_logical(j, 6)
    allowed = (jc - 8 <= rc) & (rc <= jc)
    return (r + B_LEFT - j) >= REL_CLIP, allowed, j < r


def _band_onehot(transposed, offset=0):
    shape = (BWIN, GW) if transposed else (GW, BWIN)
    kk = lax.broadcasted_iota(jnp.int32, shape, 1 if transposed else 0)
    x = lax.broadcasted_iota(jnp.int32, shape, 0 if transposed else 1) - offset
    x = jnp.where(x < 0, x + BWIN, x)
    return (kk == jnp.clip(B_LEFT - x, -REL_CLIP, REL_CLIP) + REL_CLIP).astype(F32)


def _band_table(rel_bias8):
    def body(b_ref, o_ref):
        hh = pl.program_id(0)
        u8 = jnp.dot(b_ref[...], _band_onehot(False), precision=HI, preferred_element_type=F32)
        rid = lax.broadcasted_iota(jnp.int32, (8, BWIN), 0)
        row = jnp.sum(jnp.where(rid == hh, u8, 0.0), axis=0, keepdims=True)
        far, allowed, _ = _band_geometry()
        tbl = pltpu.roll(jnp.broadcast_to(row, (BQ, BWIN)), 0, 1, stride=1, stride_axis=0)
        tbl = jnp.where(far, row[:, 0:1], tbl)
        o_ref[0] = jnp.where(allowed, tbl, NEG)

    return pl.pallas_call(
        body, name="band_table", grid=(6,),
        in_specs=[pl.BlockSpec((8, GW), lambda h: (0, 0))],
        out_specs=pl.BlockSpec((1, BQ, BWIN), lambda h: (h, 0, 0)),
        out_shape=jax.ShapeDtypeStruct((6, BQ, BWIN), F32),
        compiler_params=_cparams(("arbitrary",)),
    )(rel_bias8)


def _band_table_bwd(gtab):
    def body(g_ref, o_ref):
        gv = g_ref[0]
        _, _, wrapped = _band_geometry()
        gfar = jnp.sum(jnp.sum(jnp.where(wrapped, gv, 0.0), axis=-1, keepdims=True), axis=0, keepdims=True)
        anti = (lax.broadcasted_iota(jnp.int32, (BQ, BQ), 0) + lax.broadcasted_iota(jnp.int32, (BQ, BQ), 1)
                == BQ - 1).astype(F32)
        grev = jnp.dot(anti, jnp.where(wrapped, 0.0, gv), precision=HI, preferred_element_type=F32)
        near = pltpu.roll(grev, 0, 1, stride=1, stride_axis=0)
        y = jnp.broadcast_to(jnp.sum(near, axis=0, keepdims=True), (8, BWIN))
        gb = jnp.dot(y, _band_onehot(True, BQ - 1), precision=HI, preferred_element_type=F32)
        lane = lax.broadcasted_iota(jnp.int32, (8, GW), 1)
        o_ref[0] = gb + jnp.where(lane == 2 * REL_CLIP, gfar, 0.0)

    return pl.pallas_call(
        body, name="band_table_bwd", grid=(B_HEADS,),
        in_specs=[pl.BlockSpec((1, BQ, BWIN), lambda h: (h, 0, 0))],
        out_specs=pl.BlockSpec((1, 8, GW), lambda h: (h, 0, 0)),
        out_shape=jax.ShapeDtypeStruct((B_HEADS, 8, GW), F32),
        compiler_params=_cparams(("arbitrary",)),
    )(gtab)


def _band_fwd(q, k, v, table, seq, scale):
    t = q.shape[0]
    nb = t // seq
    nq = seq // BQ

    def body(q_ref, k_ref, v_ref, tb_ref, o_ref, st_ref, kpad, vpad):
        qi = pl.program_id(2)
        q0 = pl.multiple_of(qi * BQ, BQ)
        lane = lax.broadcasted_iota(jnp.int32, (1, 128), 1)
        half = lane >= 64

        @pl.when(qi == 0)
        def _():
            kpad[0:B_LEFT, :] = jnp.zeros((B_LEFT, 128), BF16)
            vpad[0:B_LEFT, :] = jnp.zeros((B_LEFT, 128), BF16)
            kpad[B_LEFT:, :] = k_ref[...]
            vpad[B_LEFT:, :] = v_ref[...]

        kw = kpad[pl.ds(q0, BWIN), :]
        vw = vpad[pl.ds(q0, BWIN), :]
        inside = lax.broadcasted_iota(jnp.int32, (BQ, BWIN), 1) >= B_LEFT - q0
        assert math.frexp(scale)[0] == 0.5
        qall = q_ref[...] * jnp.asarray(scale, BF16)

        def run(heads):
            o, st = jnp.zeros((BQ, 128), F32), jnp.zeros((BQ, 128), F32)
            for j in heads:
                qh = jnp.where(half == bool(j), qall, jnp.zeros_like(qall))
                s = jnp.where(inside, _nt(qh, kw) + tb_ref[j], NEG)
                m = jnp.max(s, axis=-1, keepdims=True)
                p = jnp.exp(s - m)
                l = jnp.sum(p, axis=-1, keepdims=True)
                o = jnp.where(half == bool(j), jnp.dot(p.astype(BF16), vw, preferred_element_type=F32) / l, o)
                st = jnp.where(lane == j, m + jnp.log(l), st)
            o_ref[...] = o.astype(BF16)
            st_ref[...] = st

        last = pl.program_id(1) == B_HEADS // 2
        pl.when(jnp.logical_not(last))(lambda: run((0, 1)))
        pl.when(last)(lambda: run((0,)))

    qblk = pl.BlockSpec((BQ, 128), lambda b, p, i: (b * nq + i, p))
    full = pl.BlockSpec((seq, 128), lambda b, p, i: (b, p))
    return pl.pallas_call(
        body, name="band_fwd", grid=(nb, 3, nq),
        in_specs=[qblk, full, full, pl.BlockSpec((2, BQ, BWIN), lambda b, p, i: (p, 0, 0))],
        out_specs=[qblk, qblk],
        out_shape=[jax.ShapeDtypeStruct((t, GW), BF16), jax.ShapeDtypeStruct((t, GW), F32)],
        scratch_shapes=[pltpu.VMEM((seq + B_LEFT, 128), BF16), pltpu.VMEM((seq + B_LEFT, 128), BF16)],
        compiler_params=_cparams(("arbitrary", "arbitrary", "arbitrary")),
    )(q, k, v, table)


def _band_bwd(q, k, v, table, o, st, do, seq, scale, dep=None):
    t = q.shape[0]
    nb = t // seq
    nq = seq // BQ

    def body(q_ref, k_ref, v_ref, tb_ref, o_ref, st_ref, do_ref, dq_ref, dk_ref, dv_ref, g_ref,
             kpad, vpad, dkpad, dvpad):
        b = pl.program_id(1)
        qi = pl.program_id(2)
        q0 = pl.multiple_of(qi * BQ, BQ)
        lane = lax.broadcasted_iota(jnp.int32, (1, 128), 1)
        half = lane >= 64

        @pl.when(qi == 0)
        def _():
            kpad[0:B_LEFT, :] = jnp.zeros((B_LEFT, 128), BF16)
            vpad[0:B_LEFT, :] = jnp.zeros((B_LEFT, 128), BF16)
            kpad[B_LEFT:, :] = k_ref[...]
            vpad[B_LEFT:, :] = v_ref[...]
            dkpad[...] = jnp.zeros_like(dkpad)
            dvpad[...] = jnp.zeros_like(dvpad)

        @pl.when((qi == 0) & (b == 0))
        def _():
            g_ref[...] = jnp.zeros_like(g_ref)

        win = pl.ds(q0, BWIN)
        kw = kpad[win, :]
        vw = vpad[win, :]
        inside = lax.broadcasted_iota(jnp.int32, (BQ, BWIN), 1) >= B_LEFT - q0
        qall = q_ref[...]
        dov = do_ref[...]
        dd = dov.astype(F32) * o_ref[...]
        stv = st_ref[...]

        def run(heads):
            dq = jnp.zeros((BQ, 128), F32)
            for j in heads:
                hm = half == bool(j)
                qh = jnp.where(hm, qall, jnp.zeros_like(qall))
                delta = jnp.sum(jnp.where(hm, dd, 0.0), axis=-1, keepdims=True)
                s = jnp.where(inside, _nt(qh, kw) * scale + tb_ref[j], NEG)
                p = jnp.exp(s - stv[:, j:j + 1])
                doh = jnp.where(hm, dov, jnp.zeros_like(dov))
                ds = p * (_nt(doh, vw) - delta)
                g_ref[j] += ds
                dsb = (ds * scale).astype(BF16)
                dvpad[win, :] += _tn(p.astype(BF16), doh)
                dkpad[win, :] += _tn(dsb, qh)
                dq = dq + jnp.where(hm, jnp.dot(dsb, kw, preferred_element_type=F32), 0.0)
            dq_ref[...] = dq.astype(BF16)

        last = pl.program_id(0) == B_HEADS // 2
        pl.when(jnp.logical_not(last))(lambda: run((0, 1)))
        pl.when(last)(lambda: run((0,)))

        @pl.when(qi == nq - 1)
        def _():
            dk_ref[...] = dkpad[B_LEFT:, :].astype(BF16)
            dv_ref[...] = dvpad[B_LEFT:, :].astype(BF16)

    qblk = pl.BlockSpec((BQ, 128), lambda p, b, i: (b * nq + i, p))
    full = pl.BlockSpec((seq, 128), lambda p, b, i: (b, p))
    tblk = pl.BlockSpec((2, BQ, BWIN), lambda p, b, i: (p, 0, 0))
    body, in_specs, args = _after(dep, body, [qblk, full, full, tblk, qblk, qblk, qblk], [q, k, v, table, o, st, do])
    return pl.pallas_call(
        body, name="band_bwd", grid=(3, nb, nq),
        in_specs=in_specs,
        out_specs=[qblk, full, full, tblk],
        out_shape=[jax.ShapeDtypeStruct((t, GW), BF16), jax.ShapeDtypeStruct((t, GW), BF16),
                   jax.ShapeDtypeStruct((t, GW), BF16), jax.ShapeDtypeStruct((6, BQ, BWIN), F32)],
        scratch_shapes=[pltpu.VMEM((seq + B_LEFT, 128), BF16), pltpu.VMEM((seq + B_LEFT, 128), BF16),
                        pltpu.VMEM((seq + B_LEFT, 128), F32), pltpu.VMEM((seq + B_LEFT, 128), F32)],
        compiler_params=_cparams(("arbitrary", "arbitrary", "arbitrary")),
    )(*args)


def _fox_prep(cf, fb, seq):
    nb = cf.shape[0] // seq
    nblk = seq // 128

    def body(cf_ref, fb_ref, f_ref):
        x = cf_ref[...] + fb_ref[...]
        lf = jnp.minimum(x, 0.0) - jnp.log1p(jnp.exp(-jnp.abs(x)))
        rows = lf.T[0:8, :]
        upper = (lax.broadcasted_iota(jnp.int32, (128, 128), 0)
                 <= lax.broadcasted_iota(jnp.int32, (128, 128), 1)).astype(F32)
        carry = jnp.zeros((8, 1), F32)
        for blk in range(nblk):
            sl = slice(blk * 128, (blk + 1) * 128)
            cs = jnp.dot(rows[:, sl], upper, precision=HI, preferred_element_type=F32) + carry
            carry = cs[:, 127:128]
            f_ref[0, 0, :, sl] = cs
            f_ref[0, 1, :, sl] = pltpu.roll(cs, 6, 0)
            f_ref[0, 2, :, sl] = pltpu.roll(cs, 4, 0)

    return pl.pallas_call(
        body, name="fox_prep", grid=(nb,),
        in_specs=[pl.BlockSpec((seq, 128), lambda b: (b, 0)), pl.BlockSpec((1, 128), lambda b: (0, 0))],
        out_specs=pl.BlockSpec((1, 3, 8, seq), lambda b: (b, 0, 0, 0)),
        out_shape=jax.ShapeDtypeStruct((nb, 3, 8, seq), F32),
        compiler_params=_cparams(("arbitrary",)),
    )(cf, fb)


def _fox_prep_bwd(df, dfq, cf, fb, seq):
    nb = cf.shape[0] // seq
    nblk = seq // 128

    def body(df_ref, dfq_ref, cf_ref, fb_ref, dcf_ref, dfb_ref, wide):
        b = pl.program_id(0)
        row = lax.broadcasted_iota(jnp.int32, (8, seq), 0)
        dfh = None
        for p in range(3):
            both = df_ref[0, p] + dfq_ref[:, p * 128:(p + 1) * 128].T[0:8, :]
            both = jnp.where(row < 2, both, 0.0)
            if p:
                both = pltpu.roll(both, 2 * p, 0)
            dfh = both if dfh is None else dfh + both
        lower = (lax.broadcasted_iota(jnp.int32, (128, 128), 0)
                 >= lax.broadcasted_iota(jnp.int32, (128, 128), 1)).astype(F32)
        wide[...] = jnp.zeros_like(wide)
        carry = jnp.zeros((8, 1), F32)
        for blk in reversed(range(nblk)):
            sl = slice(blk * 128, (blk + 1) * 128)
            rc = jnp.dot(dfh[:, sl], lower, precision=HI, preferred_element_type=F32) + carry
            carry = rc[:, 0:1]
            wide[0:8, sl] = rc
        dl = wide[...].T
        x = cf_ref[...] + fb_ref[...]
        dcf = dl * (1.0 / (1.0 + jnp.exp(x)))
        dcf_ref[...] = dcf.astype(BF16)
        part = jnp.sum(dcf, axis=0, keepdims=True)

        @pl.when(b == 0)
        def _():
            dfb_ref[...] = part

        @pl.when(b != 0)
        def _():
            dfb_ref[...] += part

    return pl.pallas_call(
        body, name="fox_prep_bwd", grid=(nb,),
        in_specs=[pl.BlockSpec((1, 3, 8, seq), lambda b: (b, 0, 0, 0)), pl.BlockSpec((seq, GW), lambda b: (b, 0)),
                  pl.BlockSpec((seq, 128), lambda b: (b, 0)), pl.BlockSpec((1, 128), lambda b: (0, 0))],
        out_specs=[pl.BlockSpec((seq, 128), lambda b: (b, 0)), pl.BlockSpec((1, 128), lambda b: (0, 0))],
        out_shape=[jax.ShapeDtypeStruct(cf.shape, BF16), jax.ShapeDtypeStruct((1, 128), F32)],
        scratch_shapes=[pltpu.VMEM((128, seq), F32)],
        compiler_params=_cparams(("arbitrary",)),
    )(df, dfq, cf, fb)


def _gate_out(oa, ob, oc, gates, w, x, gate, seq, tm=ROW_TILE):
    t = x.shape[0]
    tm = min(tm, seq)
    tps = seq // tm

    def body(oa_ref, ob_ref, oc_ref, g_ref, w_ref, x_ref, gt_ref, xo_ref, y_ref, u_ref):
        for n, o_ref in enumerate((oa_ref, ob_ref, oc_ref)):
            sl = slice(n * GW, (n + 1) * GW)
            gv = g_ref[:, sl].astype(F32)
            u_ref[:, sl] = (o_ref[...] * (gv * _sigmoid(gv))).astype(BF16)
        y = jnp.dot(u_ref[...], w_ref[...], preferred_element_type=F32)
        y_ref[...] = y.astype(BF16)
        xo_ref[...] = x_ref[...] + gt_ref[0] * y

    row = lambda wd: pl.BlockSpec((tm, wd), lambda i: (i, 0))
    return pl.pallas_call(
        body, name="gate_out", grid=(t // tm,),
        in_specs=[row(GW), row(GW), row(GW), row(U_PAD), pl.BlockSpec((U_PAD, D_MODEL), lambda i: (0, 0)),
                  row(D_MODEL), pl.BlockSpec((1, 1, D_MODEL), lambda i: (i // tps, 0, 0))],
        out_specs=[row(D_MODEL), row(D_MODEL), row(U_PAD)],
        out_shape=[jax.ShapeDtypeStruct((t, D_MODEL), F32), jax.ShapeDtypeStruct((t, D_MODEL), BF16),
                   jax.ShapeDtypeStruct((t, U_PAD), BF16)],
        compiler_params=_cparams(("arbitrary",)),
    )(oa, ob, oc, gates, w, x, gate)


def _gate_out_bwd(dxo, y, gate, oa, ob, oc, gates, w_t, seq, tm=ROW_TILE, dep=None):
    t = dxo.shape[0]
    tm = min(tm, seq)
    tps = seq // tm
    nb = t // seq

    def body(dxo_ref, y_ref, gt_ref, oa_ref, ob_ref, oc_ref, g_ref, wt_ref,
             dy_ref, doa_ref, dob_ref, doc_ref, dg_ref, dgt_ref):
        i = pl.program_id(0)
        dxo_v = dxo_ref[...]
        dgt = jnp.sum(dxo_v * y_ref[...].astype(F32), axis=0, keepdims=True)
        dyb = (dxo_v * gt_ref[0]).astype(BF16)
        dy_ref[...] = dyb
        du = _nt(dyb, wt_ref[...])
        for n, (o_ref, do_ref) in enumerate(((oa_ref, doa_ref), (ob_ref, dob_ref), (oc_ref, doc_ref))):
            sl = slice(n * GW, (n + 1) * GW)
            gv = g_ref[:, sl].astype(F32)
            sg = _sigmoid(gv)
            dun = du[:, sl]
            do_ref[...] = (dun * (gv * sg)).astype(BF16)
            dg_ref[:, sl] = (dun * o_ref[...] * (sg * (1.0 + gv * (1.0 - sg)))).astype(BF16)

        @pl.when(i % tps == 0)
        def _():
            dgt_ref[0] = dgt

        @pl.when(i % tps != 0)
        def _():
            dgt_ref[0] += dgt

    row = lambda wd: pl.BlockSpec((tm, wd), lambda i: (i, 0))
    per_b = pl.BlockSpec((1, 1, D_MODEL), lambda i: (i // tps, 0, 0))
    in_specs = [row(D_MODEL), row(D_MODEL), per_b, row(GW), row(GW), row(GW), row(U_PAD),
                pl.BlockSpec((U_PAD, D_MODEL), lambda i: (0, 0))]
    body, in_specs, args = _after(dep, body, in_specs, [dxo, y, gate, oa, ob, oc, gates, w_t])
    return pl.pallas_call(
        body, name="gate_out_bwd", grid=(t // tm,), in_specs=in_specs,
        out_specs=[row(D_MODEL), row(GW), row(GW), row(GW), row(U_PAD), per_b],
        out_shape=[jax.ShapeDtypeStruct((t, D_MODEL), BF16), jax.ShapeDtypeStruct((t, GW), BF16),
                   jax.ShapeDtypeStruct((t, GW), BF16), jax.ShapeDtypeStruct((t, GW), BF16),
                   jax.ShapeDtypeStruct((t, U_PAD), BF16), jax.ShapeDtypeStruct((nb, 1, D_MODEL), F32)],
        compiler_params=_cparams(("arbitrary",)),
    )(*args)


def _final_loss(x, target, g, tm=ROW_TILE):
    t = x.shape[0]
    tm = min(tm, t)

    def body(x_ref, t_ref, g_ref, dx_ref, loss_ref, dg_ref):
        i = pl.program_id(0)
        xv = x_ref[...]
        rstd = lax.rsqrt(jnp.mean(xv * xv, axis=-1, keepdims=True) + EPS)
        xn = xv * rstd
        gv = g_ref[...]
        err = xn * gv - t_ref[...]
        dy = err * (1.0 / D_MODEL)
        dxn = dy * gv
        dx_ref[...] = rstd * (dxn - xn * jnp.mean(dxn * xn, axis=-1, keepdims=True))
        lp = jnp.sum(err * err, axis=0, keepdims=True) * (0.5 / D_MODEL)
        dgp = jnp.sum(dy * xn, axis=0, keepdims=True)

        @pl.when(i == 0)
        def _():
            loss_ref[...] = lp
            dg_ref[...] = dgp

        @pl.when(i != 0)
        def _():
            loss_ref[...] += lp
            dg_ref[...] += dgp

    row = pl.BlockSpec((tm, D_MODEL), lambda i: (i, 0))
    vec = pl.BlockSpec((1, D_MODEL), lambda i: (0, 0))
    return pl.pallas_call(
        body, name="final_loss", grid=(t // tm,),
        in_specs=[row, row, vec], out_specs=[row, vec, vec],
        out_shape=[jax.ShapeDtypeStruct((t, D_MODEL), F32), jax.ShapeDtypeStruct((1, D_MODEL), F32),
                   jax.ShapeDtypeStruct((1, D_MODEL), F32)],
        compiler_params=_cparams(("arbitrary",)),
    )(x, target, g)


def _adamw(w, gslots, m, v, name, tr=None):
    nl, r, c = w.shape
    ns = gslots.shape[0]
    tr = r if tr is None else tr

    def body(w_ref, g_ref, m_ref, v_ref, go_ref, d_ref, mo_ref, vo_ref):
        g = g_ref[0].astype(F32)
        for j in range(1, ns):
            g = g + g_ref[j].astype(F32)
        mn = ADAM_B1 * m_ref[...] + (1.0 - ADAM_B1) * g
        vn = ADAM_B2 * v_ref[...] + (1.0 - ADAM_B2) * jnp.square(g)
        m_hat = mn / (1.0 - ADAM_B1 ** ADAM_STEP)
        v_hat = vn / (1.0 - ADAM_B2 ** ADAM_STEP)
        go_ref[...] = g
        d_ref[...] = -ADAM_LR * (m_hat / (jnp.sqrt(v_hat) + ADAM_EPS) + ADAM_WD * w_ref[...])
        mo_ref[...] = mn
        vo_ref[...] = vn

    blk = pl.BlockSpec((1, tr, c), lambda l, i: (l, i, 0))
    return pl.pallas_call(
        body, name=name, grid=(nl, r // tr),
        in_specs=[blk, pl.BlockSpec((ns, 1, tr, c), lambda l, i: (0, l, i, 0)), blk, blk],
        out_specs=[blk] * 4, out_shape=[jax.ShapeDtypeStruct((nl, r, c), F32)] * 4,
        compiler_params=_cparams(("arbitrary", "arbitrary")),
    )(w, gslots, m, v)


def _rope_tables(positions):
    inv = ROPE_THETA ** (-jnp.arange(0, A_ROPE, 2, dtype=F32) / A_ROPE)
    ang = positions.astype(F32)[:, None] * inv
    cos, sin = jnp.cos(ang), jnp.sin(ang)
    t = positions.shape[0]
    one = jnp.ones((t, 64), F32)
    zero16 = jnp.zeros((t, 16), F32)
    cos_t = jnp.concatenate([one, cos, cos, jnp.ones((t, 32), F32)], axis=1)
    sin_a = jnp.concatenate([jnp.zeros((t, 64), F32), -sin, zero16, jnp.zeros((t, 32), F32)], axis=1)
    sin_b = jnp.concatenate([jnp.zeros((t, 64), F32), zero16, sin, jnp.zeros((t, 32), F32)], axis=1)
    return cos_t, sin_a, sin_b


def _pad_heads(w, real, padded, nheads, axis):
    shp = w.shape[:axis] + (nheads, real) + w.shape[axis + 1:]
    w = w.reshape(shp)
    pad = [(0, 0)] * w.ndim
    pad[axis + 1] = (0, padded - real)
    w = jnp.pad(w, pad)
    return w.reshape(w.shape[:axis] + (nheads * padded,) + w.shape[axis + 2:])


def kernel(x, c, positions, w_ada, b_ada, norm_g, w_in, a_q_norm_g, a_w_uq, a_kv_norm_g, a_w_ukv, b_rel_bias, c_forget_b, w_out, final_g, loss_target, m_w_ada, m_b_ada, m_norm_g, m_w_in, m_a_q_norm_g, m_a_w_uq, m_a_kv_norm_g, m_a_w_ukv, m_b_rel_bias, m_c_forget_b, m_w_out, m_final_g, v_w_ada, v_b_ada, v_norm_g, v_w_in, v_a_q_norm_g, v_a_w_uq, v_a_kv_norm_g, v_a_w_ukv, v_b_rel_bias, v_c_forget_b, v_w_out, v_final_g):
    nb, seq, _ = x.shape
    t = nb * seq
    me = 4 * lax.axis_index("x") + 2 * lax.axis_index("y") + lax.axis_index("c")
    x2 = x.reshape(t, D_MODEL)
    tgt = loss_target.reshape(t, D_MODEL)
    cos_t, sin_a, sin_b = _rope_tables(positions.reshape(t))

    def shards(l):
        return [_pad_runs(w_in[l].astype(BF16), IN_RUNS, N_PAD, 1), w_out[l].astype(BF16),
                a_w_uq[l].astype(BF16), a_w_ukv[l].astype(BF16)]

    def prepare(gi, go, gq, gkv):
        return dict(w_in=gi.reshape(D_MODEL, N_PAD), **prepare_rest(go, gq, gkv))

    def prepare_rest(go, gq, gkv):
        wo = _pad_runs(go.reshape(D_MODEL, D_MODEL), OUT_RUNS, U_PAD, 0)
        wq = jnp.transpose(gq, (1, 0, 2)).reshape(A_Q_RANK, A_HEADS * (A_NOPE + A_ROPE))
        wq = _pad_heads(wq, A_NOPE + A_ROPE, HEAD_PAD, A_HEADS, 1)
        wkv = jnp.transpose(gkv, (1, 0, 2)).reshape(A_KV_RANK, A_HEADS, 2 * A_NOPE)
        wk = jnp.pad(wkv[:, :, :A_NOPE], ((0, 0), (0, 0), (0, HEAD_PAD - A_NOPE))).reshape(A_KV_RANK, A_HEADS * HEAD_PAD)
        wv = wkv[:, :, A_NOPE:].reshape(A_KV_RANK, GW)
        return dict(w_out=wo, wuq=wq, wuq_t=wq.T, wk=wk, wk_t=wk.T, wv=wv, wv_t=wv.T)

    shards0 = shards(0)
    w_in0_g, c_g = _gather([shards0[0], c], "gather_w_in0")
    c_all = c_g.reshape(N_DEV * nb, D_MODEL)
    weights = [dict(w_in=w_in0_g.reshape(D_MODEL, N_PAD)), None]

    c_act, mod_cols = _ada_fwd(c_all, w_ada)
    (mod_g,) = _gather([mod_cols], "gather_mod")
    rest0, rest0_token = _split_start("gather", shards0[1:], "gather_rest0_start", after=mod_g)
    mod_all = jnp.transpose(mod_g, (1, 2, 0, 3)).reshape(DEPTH, N_DEV * nb, 3 * D_MODEL)
    mod = lax.dynamic_slice_in_dim(mod_all, me * nb, nb, axis=1) + b_ada[:, None, :]

    fb_pad = jnp.pad(c_forget_b, ((0, 0), (0, 128 - C_HEADS)))
    a_scale = (A_NOPE + A_ROPE) ** -0.5
    h_scale = CHUNK ** -0.5

    saved = []
    xl = x2
    for l in range(DEPTH):
        if l == 1:
            weights[1] = prepare(*_split_wait(gather1, xl, "gather_weights1_wait")[1])
        w = weights[l]
        shift, scale, gate = mod[l, :, :D_MODEL], mod[l, :, D_MODEL:2 * D_MODEL], mod[l, :, 2 * D_MODEL:]
        ss = jnp.stack([shift, 1.0 + scale], axis=1)
        gate3 = gate[:, None, :]
        h, cq, ckv, kpe, gates, bq, bk, bv, cq2, ck, cv, cf = _ln_in(
            xl, ss, norm_g[l:l + 1], w["w_in"], seq, dep=rest0_token if l == 0 else None)
        gather1_token = None
        if l == 0:
            w.update(prepare_rest(*_split_wait(rest0, h, "gather_rest0_wait")[1]))
            gather1, gather1_token = _split_start("gather", shards(1), "gather_weights1_start", after=w["w_out"])
        q, k, v, cqn, ckvn = _mla_prep(cq, ckv, kpe, a_q_norm_g[l:l + 1], a_kv_norm_g[l:l + 1],
                                       w["wuq"], w["wk"], w["wv"], cos_t, sin_a, sin_b, dep=gather1_token)
        oa, sta = _attn_fwd("mla", q, k, v, None, seq, a_scale)
        table = _band_table(jnp.pad(b_rel_bias[l], ((0, 8 - B_HEADS), (0, GW - N_REL))))
        ob, stb = _band_fwd(bq, bk, bv, table, seq, h_scale)
        fcum = _fox_prep(cf, fb_pad[l:l + 1], seq)
        oc, stc = _attn_fwd("fox", cq2, ck, cv, fcum, seq, h_scale)
        xn, y, u = _gate_out(oa, ob, oc, gates, w["w_out"], xl, gate3, seq)
        saved.append(dict(x=xl, ss=ss, gate3=gate3, h=h, cq=cq, ckv=ckv, gates=gates, bq=bq, bk=bk, bv=bv,
                          cq2=cq2, ck=ck, cv=cv, cf=cf, q=q, k=k, v=v, cqn=cqn, ckvn=ckvn, oa=oa, sta=sta,
                          table=table, ob=ob, stb=stb, fcum=fcum, oc=oc, stc=stc, y=y, u=u))
        xl = xn

    dx, loss_lanes, g_final = _final_loss(xl, tgt, final_g[None, :])
    loss = lax.psum(jnp.sum(loss_lanes), AXES)

    rows = D_MODEL // N_DEV
    core = lax.axis_index("c").astype(jnp.int32).reshape(1)
    n_seg_a = 4
    dmods, smalls, parts = [None] * DEPTH, [None] * DEPTH, [None] * DEPTH
    pair1 = chips1 = pair1_token = chips1_token = None
    for l in reversed(range(DEPTH)):
        s, w = saved[l], weights[l]
        dy, doa, dob, doc, dgates, dgate = _gate_out_bwd(dx, s["y"], s["gate3"], s["oa"], s["ob"], s["oc"],
                                                         s["gates"], w["w_out"], seq, dep=pair1_token)
        g_out = _unpad_runs(_matmul_tn(s["u"], dy, "dw_out"), OUT_RUNS, 0)
        if l == 0:
            own, from_sib = _split_wait(pair1, g_out, "grads1_pair_wait")
            chips1, chips1_token = _split_start("chips", _pair_add(core, own, from_sib, "grads1_add"), "grads1_chips_start")
        dq, dk, dv = _attn_bwd("mla", s["q"], s["k"], s["v"], None, s["oa"], s["sta"], doa, seq, a_scale,
                               dep=chips1_token)
        dbq, dbk, dbv, gtab = _band_bwd(s["bq"], s["bk"], s["bv"], s["table"], s["ob"], s["stb"], dob, seq, h_scale,
                                        dep=chips1_token)
        g_rel = _band_table_bwd(gtab)[:, 0, :N_REL]
        dcq2, dck, dcv, dfc, dfq = _attn_bwd("fox", s["cq2"], s["ck"], s["cv"], s["fcum"], s["oc"], s["stc"], doc,
                                             seq, h_scale, dep=chips1_token)
        dcf, dfb = _fox_prep_bwd(dfc, dfq, s["cf"], fb_pad[l:l + 1], seq)
        dcq, dckv, dkpe, dqlin, dklin, dgq, dgkv = _mla_prep_bwd(
            dq, dk, dv, s["cq"], s["ckv"], a_q_norm_g[l:l + 1], a_kv_norm_g[l:l + 1],
            w["wuq_t"], w["wk_t"], w["wv_t"], cos_t, sin_a, sin_b)
        gq_pad = _matmul_tn(s["cqn"], dqlin, "dw_uq")
        g_uq = gq_pad.reshape(A_Q_RANK, A_HEADS, HEAD_PAD)[:, :, :A_NOPE + A_ROPE].reshape(A_Q_RANK, -1)
        gkv_pad = _matmul_tn(s["ckvn"], [dklin, dv], "dw_ukv")
        gk_pad = gkv_pad[:, :A_HEADS * HEAD_PAD].reshape(A_KV_RANK, A_HEADS, HEAD_PAD)[:, :, :A_NOPE]
        gv_pad = gkv_pad[:, A_HEADS * HEAD_PAD:].reshape(A_KV_RANK, A_HEADS, A_NOPE)
        g_ukv = jnp.concatenate([gk_pad, gv_pad], axis=2).reshape(A_KV_RANK, -1)
        dz = [dcq, dckv, dkpe, dgates, dbq, dbk, dbv, dcq2, dck, dcv, dcf]
        g_in_a = _matmul_tn(s["h"], dz[:n_seg_a], "dw_in_a")
        first = [g_in_a.reshape(N_DEV, rows, -1), g_out.reshape(N_DEV, rows, D_MODEL),
                 g_uq.reshape(A_Q_RANK, N_DEV, -1).transpose(1, 0, 2), g_ukv.reshape(A_KV_RANK, N_DEV, -1).transpose(1, 0, 2)]
        if l == 1:
            g_in_b = _matmul_tn(s["h"], dz[n_seg_a:], "dw_in_b")
            pair1, pair1_token = _split_start("pair", first + [g_in_b.reshape(N_DEV, rows, -1)], "grads1_pair_start")
            tail_token = None
        else:
            pair0a, pair0a_token = _split_start("pair", first, "grads0a_pair_start")
            g_in_b = _matmul_tn(s["h"], dz[n_seg_a:], "dw_in_b", dep=pair0a_token)
            own, from_sib = _split_wait(pair0a, g_in_b, "grads0a_pair_wait")
            sums0a = _pair_add(core, own, from_sib, "grads0a_add")
            pair0b, pair0b_token = _split_start("pair", [g_in_b.reshape(N_DEV, rows, -1)], "grads0b_pair_start",
                                                after=sums0a[0])
            chips0a, tail_token = _split_start("chips", sums0a, "grads0a_chips_start", after=pair0b_token)
        dx, dss, dg_norm = _ln_in_bwd(dz, w["w_in"], s["x"], s["ss"], norm_g[l:l + 1], dx, seq, dep=tail_token)
        dmods[l] = jnp.concatenate([dss[:, 0, :], dss[:, 1, :], dgate[:, 0, :]], axis=1)
        smalls[l] = [dg_norm.reshape(-1), dgq.reshape(-1), dgkv.reshape(-1), g_rel.reshape(-1),
                     dfb[0, :C_HEADS]]
    grad_x = dx.reshape(nb, seq, D_MODEL)
    parts[1] = _split_wait(chips1, dx, "grads1_chips_wait")[1]
    parts0a = _split_wait(chips0a, dx, "grads0a_chips_wait")[1]
    own, from_sib = _split_wait(pair0b, dx, "grads0b_pair_wait")

    small = jnp.concatenate([p for l in range(DEPTH) for p in smalls[l]] + [g_final.reshape(-1)])
    n_small = small.shape[0]
    small_rows = -(-n_small // 1024) * 8
    small = jnp.pad(small, (0, small_rows * 128 - n_small)).reshape(small_rows, 128)
    dmod_local = jnp.stack(dmods)
    dmod_g, small_g = _gather([dmod_local, small], "gather_small", dep=parts0a[0])
    chips0, chips0_token = _split_start("chips", _pair_add(core, own, from_sib, "grads0b_add"), "grads0b_chips_start",
                                        after=small_g)
    dmod_all = jnp.transpose(dmod_g, (1, 0, 2, 3)).reshape(DEPTH, N_DEV * nb, 3 * D_MODEL)
    cols = 3 * D_MODEL // N_DEV
    dmod_mine = lax.dynamic_slice_in_dim(dmod_all, me * cols, cols, axis=2)
    g_w_ada, g_b_ada = _ada_bwd(c_act, dmod_all, dmod_mine, chips0_token)
    small_sum = _sum_slots(small_g, "sum_small").reshape(-1)

    def split_small():
        out, pos = [], 0
        sizes = [D_MODEL, A_Q_RANK, A_KV_RANK, B_HEADS * N_REL, C_HEADS]
        per_layer = []
        for l in range(DEPTH):
            parts = []
            for sz in sizes:
                parts.append(small_sum[pos:pos + sz])
                pos += sz
            per_layer.append(parts)
        for j in range(len(sizes)):
            out.append(jnp.stack([per_layer[l][j] for l in range(DEPTH)]))
        out.append(small_sum[pos:pos + D_MODEL])
        return out

    g_norm, g_qn, g_kvn, g_relb, g_fb, g_fin = split_small()

    def adam(w, g, m, v, name, tr=None):
        shp = w.shape
        w3 = w.reshape((1,) * (3 - w.ndim) + shp)
        outs = _adamw(w3, g.reshape((-1,) + w3.shape), m.reshape(w3.shape), v.reshape(w3.shape), name, tr)
        return [o.reshape(shp) for o in outs]

    res = {
        "w_ada": adam(w_ada, g_w_ada, m_w_ada, v_w_ada, "adam_w_ada", 256),
        "b_ada": adam(b_ada, g_b_ada, m_b_ada, v_b_ada, "adam_b_ada"),
        "norm_g": adam(norm_g, g_norm, m_norm_g, v_norm_g, "adam_norm_g"),
        "a_q_norm_g": adam(a_q_norm_g, g_qn, m_a_q_norm_g, v_a_q_norm_g, "adam_q_norm"),
        "a_kv_norm_g": adam(a_kv_norm_g, g_kvn, m_a_kv_norm_g, v_a_kv_norm_g, "adam_kv_norm"),
        "b_rel_bias": adam(b_rel_bias, g_relb.reshape(b_rel_bias.shape), m_b_rel_bias, v_b_rel_bias, "adam_rel_bias"),
        "c_forget_b": adam(c_forget_b, g_fb, m_c_forget_b, v_c_forget_b, "adam_forget_b"),
        "final_g": adam(final_g, g_fin, m_final_g, v_final_g, "adam_final_g"),
    }
    parts[0] = list(parts0a) + list(_split_wait(chips0, res["w_ada"][1], "grads0b_chips_wait")[1])
    p_in = jnp.stack([_unpad_runs(jnp.concatenate([parts[l][0], parts[l][4]], axis=2), IN_RUNS, 2)
                      for l in range(DEPTH)], axis=1)
    p_out, p_uq, p_ukv = (jnp.stack([parts[l][j] for l in range(DEPTH)], axis=1) for j in (1, 2, 3))
    res.update({
        "w_in": adam(w_in, p_in, m_w_in, v_w_in, "adam_w_in", 64),
        "a_w_uq": adam(a_w_uq, p_uq, m_a_w_uq, v_a_w_uq, "adam_w_uq"),
        "a_w_ukv": adam(a_w_ukv, p_ukv, m_a_w_ukv, v_a_w_ukv, "adam_w_ukv"),
        "w_out": adam(w_out, p_out, m_w_out, v_w_out, "adam_w_out", 64),
    })
    names = ["w_ada", "b_ada", "norm_g", "w_in", "a_q_norm_g", "a_w_uq", "a_kv_norm_g", "a_w_ukv", "b_rel_bias",
             "c_forget_b", "w_out", "final_g"]
    outs = [loss, grad_x]
    for j in range(4):
        outs += [res[n][j] for n in names]
    return tuple(outs)
```

```python
import math

import jax
import jax.numpy as jnp
from jax import lax
from jax.experimental import pallas as pl
from jax.experimental.pallas import tpu as pltpu

F32 = jnp.float32
BF16 = jnp.bfloat16
HI = lax.Precision.HIGHEST

N_DEV = 8
AXES = ("x", "y", "c")
D_MODEL = 1024
DEPTH = 2
CHUNK = 64
EPS = 1e-6
NEG = -1e30
A_HEADS = 6
A_NOPE = 64
A_ROPE = 32
A_Q_RANK = 384
A_KV_RANK = 256
ROPE_THETA = 10000.0
B_HEADS = 5
B_LEFT = 512
REL_CLIP = 128
N_REL = 2 * REL_CLIP + 1
C_HEADS = 5
HEAD_PAD = 128
GW = 384
N_IN = 3621
ADAM_LR = 0.001
ADAM_B1 = 0.9
ADAM_B2 = 0.999
ADAM_EPS = 1e-08
ADAM_WD = 0.01
ADAM_STEP = 10
VMEM_LIMIT = 56 * 1024 * 1024
ROW_TILE = 512
WIDE_ROW_TILE = 1024

Z_SEGS = (
    ("cq", 0, 384, F32), ("ckv", 384, 256, F32), ("kpe", 640, 128, F32), ("gates", 768, 1152, BF16),
    ("bq", 1920, 384, BF16), ("bk", 2304, 384, BF16), ("bv", 2688, 384, BF16),
    ("cq2", 3072, 384, BF16), ("ck", 3456, 384, BF16), ("cv", 3840, 384, BF16), ("cf", 4224, 128, F32),
)
N_PAD = 4352
IN_RUNS = (
    (0, 384, 0), (384, 256, 384), (640 + 64, 32, 640),
    (768, 384, 672), (768 + 384, 320, 2016), (768 + 768, 320, 3301),
    (1920, 320, 1056), (2304, 320, 1376), (2688, 320, 1696),
    (3072, 320, 2336), (3456, 320, 2656), (3840, 320, 2976), (4224, 5, 3296),
)
OUT_RUNS = ((0, 384, 0), (384, 320, 384), (768, 320, 704))
U_PAD = 1152


def _cparams(sem=None, vmem=VMEM_LIMIT):
    return pltpu.CompilerParams(dimension_semantics=sem, vmem_limit_bytes=vmem)


def _after(dep, body, in_specs, args):
    if dep is None:
        return body, in_specs, args
    n = len(args)

    def ordered(*refs):
        return body(*refs[:n], *refs[n + 1:])

    return ordered, list(in_specs) + [pl.BlockSpec((8, 128), lambda *_: (0, 0))], list(args) + [dep]


def _pad_runs(w, runs, total, axis):
    order = sorted(runs)
    parts, pos = [], 0
    for off, wd, src in order:
        if off > pos:
            shp = list(w.shape)
            shp[axis] = off - pos
            parts.append(jnp.zeros(shp, w.dtype))
        parts.append(lax.slice_in_dim(w, src, src + wd, axis=axis))
        pos = off + wd
    if pos < total:
        shp = list(w.shape)
        shp[axis] = total - pos
        parts.append(jnp.zeros(shp, w.dtype))
    return jnp.concatenate(parts, axis=axis)


def _unpad_runs(w, runs, axis):
    order = sorted(runs, key=lambda r: r[2])
    return jnp.concatenate([lax.slice_in_dim(w, off, off + wd, axis=axis) for off, wd, _ in order], axis=axis)


def _sigmoid(x):
    return 1.0 / (1.0 + jnp.exp(-x))


N_CHIP = 4
ANY_SPEC = pl.BlockSpec(memory_space=pl.ANY)
MESH_ID = pl.DeviceIdType.MESH


def _gather(arrs, name, dep=None):
    n = len(arrs)
    nin = n + (dep is not None)

    def body(*refs):
        ins, outs = refs[:n], refs[nin:nin + n]
        send_sems, recv_sems, local_sems = refs[nin + n:]
        x, y, c = lax.axis_index("x"), lax.axis_index("y"), lax.axis_index("c")
        me, sib = (x, y, c), (x, y, 1 - c)
        chips = [(1 - x, y), (x, 1 - y), (1 - x, 1 - y)]

        def slot(px, py, pc):
            return 4 * px + 2 * py + pc

        def copy(a, k, block, to, src=None):
            dst = outs[a].at[slot(*block)]
            return pltpu.make_async_remote_copy(
                src_ref=dst if src is None else src, dst_ref=dst, send_sem=send_sems.at[a, k],
                recv_sem=recv_sems.at[a, k], device_id=to, device_id_type=MESH_ID)

        local = [pltpu.make_async_copy(ins[a], outs[a].at[slot(*me)], local_sems.at[a]) for a in range(n)]
        first = []
        for a in range(n):
            first.append(copy(a, 0, me, sib, src=ins[a]))
            first += [copy(a, 1 + j, me, (*chip, c), src=ins[a]) for j, chip in enumerate(chips)]
        for cp in local + first:
            cp.start()
        passed = []
        for j, chip in enumerate(chips):
            for a in range(n):
                copy(a, 1 + j, (*chip, c), me).wait_recv()
                fwd = copy(a, 4 + j, (*chip, c), sib)
                fwd.start()
                passed.append(fwd)
        for a in range(n):
            copy(a, 0, sib, me).wait_recv()
            for j, chip in enumerate(chips):
                copy(a, 4 + j, (*chip, 1 - c), me).wait_recv()
        for cp in first + passed:
            cp.wait_send()
        for cp in local:
            cp.wait()

    return pl.pallas_call(
        body, name=name, out_shape=[jax.ShapeDtypeStruct((N_DEV,) + a.shape, a.dtype) for a in arrs],
        in_specs=[ANY_SPEC] * nin, out_specs=[ANY_SPEC] * n,
        scratch_shapes=[pltpu.SemaphoreType.DMA((n, N_DEV - 1)), pltpu.SemaphoreType.DMA((n, N_DEV - 1)),
                        pltpu.SemaphoreType.DMA((n,))],
    )(*arrs, *([] if dep is None else [dep]))


HBM_SPEC = pl.BlockSpec(memory_space=pltpu.HBM)
SEM_SPEC = pl.BlockSpec(memory_space=pltpu.SEMAPHORE)
SPLIT_EFFECT = pltpu.SideEffectType.DATAFLOW_SIDE_EFFECTING
SPLIT_SEMS = {"gather": (N_DEV - 1, True), "pair": (N_CHIP, False), "chips": (N_CHIP - 1, True)}


def _split_descriptors(pattern, srcs, lands, sems):
    x, y, c = lax.axis_index("x"), lax.axis_index("y"), lax.axis_index("c")
    nsem, has_local = SPLIT_SEMS[pattern]
    per = 2 * nsem + int(has_local)
    starts, arrivals, local = [], [], []

    def remote(a, k, src, dst, to):
        return pltpu.make_async_remote_copy(src_ref=src, dst_ref=dst, send_sem=sems[a * per + k],
                                            recv_sem=sems[a * per + nsem + k], device_id=to, device_id_type=MESH_ID)

    for a in range(len(srcs)):
        if pattern == "gather":
            me = 4 * x + 2 * y + c
            local.append(pltpu.make_async_copy(srcs[a], lands[a].at[me], sems[a * per + 2 * nsem]))
            for k in range(1, N_DEV):
                px = (1 - x) if (k >> 2) & 1 else x
                py = (1 - y) if (k >> 1) & 1 else y
                pc = (1 - c) if k & 1 else c
                starts.append(remote(a, k - 1, srcs[a], lands[a].at[me], (px, py, pc)))
                arrivals.append(remote(a, k - 1, srcs[a], lands[a].at[4 * px + 2 * py + pc], (px, py, pc)))
        elif pattern == "pair":
            for q in range(N_CHIP):
                cp = remote(a, q, srcs[a].at[2 * q + 1 - c], lands[a].at[q], (x, y, 1 - c))
                starts.append(cp)
                arrivals.append(cp)
        else:
            mine = 2 * x + y
            local.append(pltpu.make_async_copy(srcs[a].at[mine], lands[a].at[mine], sems[a * per + 2 * nsem]))
            for k in range(1, N_CHIP):
                px = (1 - x) if (k >> 1) & 1 else x
                py = (1 - y) if k & 1 else y
                starts.append(remote(a, k - 1, srcs[a].at[2 * px + py], lands[a].at[mine], (px, py, c)))
                arrivals.append(remote(a, k - 1, srcs[a].at[2 * px + py], lands[a].at[2 * px + py], (px, py, c)))
    return starts, arrivals, local


def _split_start(pattern, arrs, name, after=None):
    n = len(arrs)
    extra = [] if after is None else [after]
    nsem, has_local = SPLIT_SEMS[pattern]
    if pattern == "gather":
        land_shapes = [(N_DEV,) + a.shape for a in arrs]
    elif pattern == "pair":
        land_shapes = [(N_CHIP,) + a.shape[1:] for a in arrs]
    else:
        land_shapes = [a.shape for a in arrs]
    nsem_out = n * (2 * nsem + int(has_local))

    def body(*refs):
        srcs, lands = refs[:n], refs[n:2 * n]
        first_sem = 2 * n + len(extra)
        sems = refs[first_sem:first_sem + nsem_out]
        token = refs[-1]
        starts, _, local = _split_descriptors(pattern, srcs, lands, sems)
        for cp in local + starts:
            cp.start()
        token[...] = jnp.zeros_like(token)

    out_shape = ([pltpu.SemaphoreType.DMA(())] * nsem_out + [pltpu.HBM(a.shape, a.dtype) for a in arrs]
                 + [pltpu.HBM(s, a.dtype) for s, a in zip(land_shapes, arrs)] + [jax.ShapeDtypeStruct((8, 128), F32)])
    ins = ([pltpu.with_memory_space_constraint(a, pltpu.HBM) for a in arrs]
           + [pltpu.with_memory_space_constraint(lax.empty(s, a.dtype), pltpu.HBM) for s, a in zip(land_shapes, arrs)])
    outs = pl.pallas_call(
        body, name=name, out_shape=out_shape, in_specs=[HBM_SPEC] * (2 * n) + [ANY_SPEC] * len(extra),
        out_specs=[SEM_SPEC] * nsem_out + [HBM_SPEC] * (2 * n) + [pl.BlockSpec(memory_space=pltpu.VMEM)],
        input_output_aliases={i: nsem_out + i for i in range(2 * n)},
        compiler_params=pltpu.CompilerParams(has_side_effects=SPLIT_EFFECT),
    )(*ins, *extra)
    handle = dict(pattern=pattern, n=n, sems=outs[:nsem_out], srcs=outs[nsem_out:nsem_out + n],
                  lands=outs[nsem_out + n:nsem_out + 2 * n])
    return handle, outs[-1]


def _split_wait(handle, after, name):
    pattern, n = handle["pattern"], handle["n"]
    nsem_in = len(handle["sems"])

    def body(*refs):
        srcs, lands = refs[:n], refs[n:2 * n]
        starts, arrivals, local = _split_descriptors(pattern, srcs, lands, refs[2 * n:2 * n + nsem_in])
        for cp in starts:
            cp.wait_send()
        for cp in arrivals:
            cp.wait_recv()
        for cp in local:
            cp.wait()

    srcs, lands = handle["srcs"], handle["lands"]
    outs = pl.pallas_call(
        body, name=name,
        out_shape=[pltpu.HBM(a.shape, a.dtype) for a in srcs] + [pltpu.HBM(a.shape, a.dtype) for a in lands],
        in_specs=[HBM_SPEC] * (2 * n) + [SEM_SPEC] * nsem_in + [ANY_SPEC], out_specs=[HBM_SPEC] * (2 * n),
        input_output_aliases={i: i for i in range(2 * n)},
        compiler_params=pltpu.CompilerParams(has_side_effects=SPLIT_EFFECT),
    )(*srcs, *lands, *handle["sems"], after)
    return outs[:n], outs[n:]


def _pair_add(core, a8s, b4s, name):
    n = len(a8s)

    def body(core_ref, *refs):
        for i in range(n):
            refs[2 * n + i][...] = (refs[i][...] + refs[n + i][...]).astype(BF16)

    own = [pl.BlockSpec((1,) + b.shape[1:], lambda q, core_ref: (2 * q + core_ref[0], 0, 0)) for b in b4s]
    slot = [pl.BlockSpec((1,) + b.shape[1:], lambda q, core_ref: (q, 0, 0)) for b in b4s]
    grid_spec = pltpu.PrefetchScalarGridSpec(num_scalar_prefetch=1, grid=(N_CHIP,), in_specs=own + slot, out_specs=slot)
    return pl.pallas_call(
        body, name=name, grid_spec=grid_spec, out_shape=[jax.ShapeDtypeStruct(b.shape, BF16) for b in b4s],
        compiler_params=_cparams(("arbitrary",)),
    )(core, *a8s, *b4s)


def _sum_slots(x, name):
    _, r, c = x.shape

    def body(x_ref, o_ref):
        acc = x_ref[0]
        for j in range(1, N_DEV):
            acc = acc + x_ref[j]
        o_ref[...] = acc

    return pl.pallas_call(body, name=name, out_shape=jax.ShapeDtypeStruct((r, c), F32))(x)


def _ada_fwd(c_all, w_ada):
    nb = c_all.shape[0]
    cols = w_ada.shape[2]

    def body(c_ref, w_ref, act_ref, mod_ref):
        cv = c_ref[...]
        act = cv * _sigmoid(cv)
        act_ref[...] = act
        for l in range(DEPTH):
            mod_ref[l] = jnp.dot(act, w_ref[l], precision=HI, preferred_element_type=F32)

    return pl.pallas_call(
        body, name="ada_fwd",
        out_shape=[jax.ShapeDtypeStruct((nb, D_MODEL), F32), jax.ShapeDtypeStruct((DEPTH, nb, cols), F32)],
        compiler_params=_cparams(),
    )(c_all, w_ada)


def _ada_bwd(c_act, dmod_all, dmod_mine, dep):
    nb = c_act.shape[0]
    cols = dmod_mine.shape[2]

    def body(act_ref, dall_ref, dmine_ref, dep_ref, gw_ref, gb_ref):
        act = act_ref[...]
        for l in range(DEPTH):
            gw_ref[l] = lax.dot_general(act, dmine_ref[l], (((0,), (0,)), ((), ())),
                                        precision=HI, preferred_element_type=F32)
            gb_ref[l:l + 1, :] = jnp.sum(dall_ref[l], axis=0, keepdims=True)

    return pl.pallas_call(
        body, name="ada_bwd",
        out_shape=[jax.ShapeDtypeStruct((DEPTH, D_MODEL, cols), F32),
                   jax.ShapeDtypeStruct((DEPTH, 3 * D_MODEL), F32)],
        compiler_params=_cparams(),
    )(c_act, dmod_all, dmod_mine, dep)


def _ln_in(x, ss, g, w, seq, tm=ROW_TILE, dep=None):
    t = x.shape[0]
    tm = min(tm, seq)
    tps = seq // tm

    def body(x_ref, ss_ref, g_ref, w_ref, h_ref, *outs):
        xv = x_ref[...]
        xn = xv * lax.rsqrt(jnp.mean(xv * xv, axis=-1, keepdims=True) + EPS)
        h = xn * g_ref[...] * ss_ref[0, 1:2, :] + ss_ref[0, 0:1, :]
        hb = h.astype(BF16)
        h_ref[...] = hb
        z = jnp.dot(hb, w_ref[...], preferred_element_type=F32)
        for o_ref, (_, off, wd, _) in zip(outs, Z_SEGS):
            o_ref[...] = z[:, off:off + wd].astype(o_ref.dtype)

    row = lambda wd: pl.BlockSpec((tm, wd), lambda i: (i, 0))
    in_specs = [row(D_MODEL), pl.BlockSpec((1, 2, D_MODEL), lambda i: (i // tps, 0, 0)),
                pl.BlockSpec((1, D_MODEL), lambda i: (0, 0)), pl.BlockSpec((D_MODEL, N_PAD), lambda i: (0, 0))]
    body, in_specs, args = _after(dep, body, in_specs, [x, ss, g, w])
    return pl.pallas_call(
        body, name="ln_in", grid=(t // tm,), in_specs=in_specs,
        out_specs=[row(D_MODEL)] + [row(wd) for _, _, wd, _ in Z_SEGS],
        out_shape=[jax.ShapeDtypeStruct((t, D_MODEL), BF16)]
        + [jax.ShapeDtypeStruct((t, wd), dt) for _, _, wd, dt in Z_SEGS],
        compiler_params=_cparams(("arbitrary",)),
    )(*args)


def _ln_in_bwd(dz, w_t, x, ss, g, dxo, seq, tm=ROW_TILE, dep=None):
    t = x.shape[0]
    tm = min(tm, seq)
    tps = seq // tm
    nb = t // seq
    nz = len(Z_SEGS)

    def body(*refs):
        dz_refs = refs[:nz]
        wt_ref, x_ref, ss_ref, g_ref, dxo_ref, dx_ref, dss_ref, dg_ref = refs[nz:]
        i = pl.program_id(0)
        dzc = jnp.concatenate([r[...].astype(BF16) for r in dz_refs], axis=1)
        dh = _nt(dzc, wt_ref[...])
        xv = x_ref[...]
        rstd = lax.rsqrt(jnp.mean(xv * xv, axis=-1, keepdims=True) + EPS)
        xn = xv * rstd
        gv = g_ref[...]
        s1 = ss_ref[0, 1:2, :]
        dxg = dh * s1
        dxn = dxg * gv
        dx = rstd * (dxn - xn * jnp.mean(dxn * xn, axis=-1, keepdims=True))
        dx_ref[...] = dxo_ref[...] + dx
        dshift = jnp.sum(dh, axis=0, keepdims=True)
        dscale = jnp.sum(dh * (xn * gv), axis=0, keepdims=True)
        dgp = jnp.sum(dxg * xn, axis=0, keepdims=True)

        @pl.when(i % tps == 0)
        def _():
            dss_ref[0, 0:1, :] = dshift
            dss_ref[0, 1:2, :] = dscale

        @pl.when(i % tps != 0)
        def _():
            dss_ref[0, 0:1, :] += dshift
            dss_ref[0, 1:2, :] += dscale

        @pl.when(i == 0)
        def _():
            dg_ref[...] = dgp

        @pl.when(i != 0)
        def _():
            dg_ref[...] += dgp

    row = lambda wd: pl.BlockSpec((tm, wd), lambda i: (i, 0))
    in_specs = ([row(wd) for _, _, wd, _ in Z_SEGS]
                + [pl.BlockSpec((D_MODEL, N_PAD), lambda i: (0, 0)), row(D_MODEL),
                   pl.BlockSpec((1, 2, D_MODEL), lambda i: (i // tps, 0, 0)),
                   pl.BlockSpec((1, D_MODEL), lambda i: (0, 0)), row(D_MODEL)])
    body, in_specs, args = _after(dep, body, in_specs, [*dz, w_t, x, ss, g, dxo])
    return pl.pallas_call(
        body, name="ln_in_bwd", grid=(t // tm,), in_specs=in_specs,
        out_specs=[row(D_MODEL), pl.BlockSpec((1, 2, D_MODEL), lambda i: (i // tps, 0, 0)),
                   pl.BlockSpec((1, D_MODEL), lambda i: (0, 0))],
        out_shape=[jax.ShapeDtypeStruct((t, D_MODEL), F32), jax.ShapeDtypeStruct((nb, 2, D_MODEL), F32),
                   jax.ShapeDtypeStruct((1, D_MODEL), F32)],
        compiler_params=_cparams(("arbitrary",)),
    )(*args)


def _matmul_tn(a, bs, name, tm=2048, dep=None):
    bs = list(bs) if isinstance(bs, (list, tuple)) else [bs]
    t, k = a.shape
    widths = [b.shape[1] for b in bs]
    n = sum(widths)
    tm = min(tm, t)

    def body(a_ref, *refs):
        b_refs, o_ref = refs[:-1], refs[-1]
        i = pl.program_id(0)
        av = a_ref[...].astype(BF16)
        parts = [b_ref[...].astype(BF16) for b_ref in b_refs]
        bv = parts[0] if len(parts) == 1 else jnp.concatenate(parts, axis=1)
        part = lax.dot_general(av, bv, (((0,), (0,)), ((), ())), preferred_element_type=F32)

        @pl.when(i == 0)
        def _():
            o_ref[...] = part

        @pl.when(i != 0)
        def _():
            o_ref[...] += part

    in_specs = [pl.BlockSpec((tm, k), lambda i: (i, 0))] + [pl.BlockSpec((tm, wd), lambda i: (i, 0)) for wd in widths]
    body, in_specs, args = _after(dep, body, in_specs, [a, *bs])
    return pl.pallas_call(
        body, name=name, grid=(t // tm,), in_specs=in_specs,
        out_specs=pl.BlockSpec((k, n), lambda i: (0, 0)),
        out_shape=jax.ShapeDtypeStruct((k, n), F32),
        compiler_params=_cparams(("arbitrary",)),
    )(*args)


def _rope(blk, cos_t, sin_a, sin_b):
    return blk * cos_t + pltpu.roll(blk, 112, 1) * sin_a + pltpu.roll(blk, 16, 1) * sin_b


def _unrope(d, cos_t, sin_a, sin_b):
    return d * cos_t + pltpu.roll(d * sin_a, 16, 1) + pltpu.roll(d * sin_b, 112, 1)


def _mla_prep(cq, ckv, kpe, gq, gkv, wuq, wk, wv, cos_t, sin_a, sin_b, tm=ROW_TILE, dep=None):
    t = cq.shape[0]
    tm = min(tm, t)
    qw = A_HEADS * HEAD_PAD

    def body(cq_ref, ckv_ref, kpe_ref, gq_ref, gkv_ref, wuq_ref, wk_ref, wv_ref, c_ref, sa_ref, sb_ref,
             q_ref, k_ref, v_ref, cqn_ref, ckvn_ref):
        ct, sa, sb = c_ref[...], sa_ref[...], sb_ref[...]
        a = cq_ref[...]
        cqn = (a * lax.rsqrt(jnp.mean(a * a, axis=-1, keepdims=True) + EPS) * gq_ref[...]).astype(BF16)
        cqn_ref[...] = cqn
        b = ckv_ref[...]
        ckvn = (b * lax.rsqrt(jnp.mean(b * b, axis=-1, keepdims=True) + EPS) * gkv_ref[...]).astype(BF16)
        ckvn_ref[...] = ckvn
        qlin = jnp.dot(cqn, wuq_ref[...], preferred_element_type=F32)
        klin = jnp.dot(ckvn, wk_ref[...], preferred_element_type=F32)
        v_ref[...] = jnp.dot(ckvn, wv_ref[...], preferred_element_type=F32).astype(BF16)
        kr = _rope(kpe_ref[...], ct, sa, sb)
        for h in range(A_HEADS):
            sl = slice(h * HEAD_PAD, (h + 1) * HEAD_PAD)
            q_ref[:, sl] = _rope(qlin[:, sl], ct, sa, sb).astype(BF16)
            k_ref[:, sl] = (klin[:, sl] + kr).astype(BF16)

    row = lambda wd: pl.BlockSpec((tm, wd), lambda i: (i, 0))
    full = lambda r, c: pl.BlockSpec((r, c), lambda i: (0, 0))
    in_specs = [row(A_Q_RANK), row(A_KV_RANK), row(128), full(1, A_Q_RANK), full(1, A_KV_RANK),
                full(A_Q_RANK, qw), full(A_KV_RANK, qw), full(A_KV_RANK, GW), row(128), row(128), row(128)]
    body, in_specs, args = _after(dep, body, in_specs, [cq, ckv, kpe, gq, gkv, wuq, wk, wv, cos_t, sin_a, sin_b])
    return pl.pallas_call(
        body, name="mla_prep", grid=(t // tm,), in_specs=in_specs,
        out_specs=[row(qw), row(qw), row(GW), row(A_Q_RANK), row(A_KV_RANK)],
        out_shape=[jax.ShapeDtypeStruct((t, qw), BF16), jax.ShapeDtypeStruct((t, qw), BF16),
                   jax.ShapeDtypeStruct((t, GW), BF16), jax.ShapeDtypeStruct((t, A_Q_RANK), BF16),
                   jax.ShapeDtypeStruct((t, A_KV_RANK), BF16)],
        compiler_params=_cparams(("arbitrary",)),
    )(*args)


def _mla_prep_bwd(dq, dk, dv, cq, ckv, gq, gkv, wuq_t, wk_t, wv_t, cos_t, sin_a, sin_b, tm=ROW_TILE):
    t = cq.shape[0]
    tm = min(tm, t)
    qw = A_HEADS * HEAD_PAD

    def body(dq_ref, dk_ref, dv_ref, cq_ref, ckv_ref, gq_ref, gkv_ref, wuqt_ref, wkt_ref, wvt_ref,
             c_ref, sa_ref, sb_ref, dcq_ref, dckv_ref, dkpe_ref, dql_ref, dkl_ref, dgq_ref, dgkv_ref):
        i = pl.program_id(0)
        ct, sa, sb = c_ref[...], sa_ref[...], sb_ref[...]
        lane = lax.broadcasted_iota(jnp.int32, (1, HEAD_PAD), 1)
        nope = lane < A_NOPE
        rope = (lane >= A_NOPE) & (lane < A_NOPE + A_ROPE)
        dksum = None
        for h in range(A_HEADS):
            sl = slice(h * HEAD_PAD, (h + 1) * HEAD_PAD)
            dql_ref[:, sl] = _unrope(dq_ref[:, sl].astype(F32), ct, sa, sb).astype(BF16)
            dkh = dk_ref[:, sl].astype(F32)
            dkl_ref[:, sl] = jnp.where(nope, dkh, 0.0).astype(BF16)
            dksum = dkh if dksum is None else dksum + dkh
        dkpe_ref[...] = jnp.where(rope, _unrope(jnp.where(rope, dksum, 0.0), ct, sa, sb), 0.0).astype(BF16)
        dcqn = jnp.dot(dql_ref[...], wuqt_ref[...], preferred_element_type=F32)
        dckvn = (jnp.dot(dkl_ref[...], wkt_ref[...], preferred_element_type=F32)
                 + jnp.dot(dv_ref[...].astype(BF16), wvt_ref[...], preferred_element_type=F32))

        def norm_bwd(xv, gv, dy):
            rstd = lax.rsqrt(jnp.mean(xv * xv, axis=-1, keepdims=True) + EPS)
            xn = xv * rstd
            dxn = dy * gv
            dx = rstd * (dxn - xn * jnp.mean(dxn * xn, axis=-1, keepdims=True))
            return dx, jnp.sum(dy * xn, axis=0, keepdims=True)

        dcq, dgq = norm_bwd(cq_ref[...], gq_ref[...], dcqn)
        dckv, dgkv = norm_bwd(ckv_ref[...], gkv_ref[...], dckvn)
        dcq_ref[...] = dcq.astype(BF16)
        dckv_ref[...] = dckv.astype(BF16)

        @pl.when(i == 0)
        def _():
            dgq_ref[...] = dgq
            dgkv_ref[...] = dgkv

        @pl.when(i != 0)
        def _():
            dgq_ref[...] += dgq
            dgkv_ref[...] += dgkv

    row = lambda wd: pl.BlockSpec((tm, wd), lambda i: (i, 0))
    full = lambda r, c: pl.BlockSpec((r, c), lambda i: (0, 0))
    return pl.pallas_call(
        body, name="mla_prep_bwd", grid=(t // tm,),
        in_specs=[row(qw), row(qw), row(GW), row(A_Q_RANK), row(A_KV_RANK), full(1, A_Q_RANK), full(1, A_KV_RANK),
                  full(qw, A_Q_RANK), full(qw, A_KV_RANK), full(GW, A_KV_RANK), row(128), row(128), row(128)],
        out_specs=[row(A_Q_RANK), row(A_KV_RANK), row(128), row(qw), row(qw), full(1, A_Q_RANK), full(1, A_KV_RANK)],
        out_shape=[jax.ShapeDtypeStruct((t, A_Q_RANK), BF16), jax.ShapeDtypeStruct((t, A_KV_RANK), BF16),
                   jax.ShapeDtypeStruct((t, 128), BF16), jax.ShapeDtypeStruct((t, qw), BF16),
                   jax.ShapeDtypeStruct((t, qw), BF16), jax.ShapeDtypeStruct((1, A_Q_RANK), F32),
                   jax.ShapeDtypeStruct((1, A_KV_RANK), F32)],
        compiler_params=_cparams(("arbitrary",)),
    )(dq, dk, dv, cq, ckv, gq, gkv, wuq_t, wk_t, wv_t, cos_t, sin_a, sin_b)


def _nt(a, b):
    return lax.dot_general(a, b, (((1,), (1,)), ((), ())), preferred_element_type=F32)


def _tn(a, b):
    return lax.dot_general(a, b, (((0,), (0,)), ((), ())), preferred_element_type=F32)


def _causal_mask(kind, q0, k0, tq, tk):
    qpos = q0 + lax.broadcasted_iota(jnp.int32, (tq, tk), 0)
    kpos = k0 + lax.broadcasted_iota(jnp.int32, (tq, tk), 1)
    if kind == "mla":
        return lax.shift_right_logical(kpos, 6) <= lax.shift_right_logical(qpos, 6)
    return kpos <= qpos


def _attn_fwd(kind, q, k, v, f, seq, scale, tq=512, tk=512):
    t = v.shape[0]
    nb = t // seq
    nq = seq // tq
    hw = 256 if kind == "mla" else 128
    n_heads = A_HEADS if kind == "mla" else C_HEADS
    use_f = f is not None
    tq, tk = min(tq, seq), min(tk, seq)
    nq = seq // tq
    assert tk == tq

    def body(*refs):
        if use_f:
            q_ref, k_ref, v_ref, f_ref, o_ref, st_ref = refs
        else:
            q_ref, k_ref, v_ref, o_ref, st_ref = refs
        qi = pl.program_id(2)
        q0 = qi * tq
        lane = lax.broadcasted_iota(jnp.int32, (1, 128), 1)
        half = lane >= 64
        qall = q_ref[...]
        if kind == "mla":
            qhs = [qall[:, 0:128], qall[:, 128:256]]
            post = scale * math.log2(math.e)
        else:
            assert math.frexp(scale)[0] == 0.5
            qall = qall * jnp.asarray(scale, BF16)
            qhs = [jnp.where(half, jnp.zeros_like(qall), qall), jnp.where(half, qall, jnp.zeros_like(qall))]
            post = None
        kd = pl.multiple_of(q0, tq)
        diag = _causal_mask(kind, 0, 0, tq, tk)

        def block(j, k0, state, masked):
            m, l, acc = state
            kh = k_ref[pl.ds(k0, tk), j * 128:(j + 1) * 128] if kind == "mla" else k_ref[pl.ds(k0, tk), :]
            s = _nt(qhs[j], kh)
            if post is not None:
                s = s * post
            if use_f:
                s = s - f_ref[0, 0, j:j + 1, pl.ds(k0, tk)]
            if masked:
                s = jnp.where(diag, s, NEG)
            mn = jnp.maximum(m, jnp.max(s, axis=-1, keepdims=True))
            alpha = jnp.exp2(m - mn) if post is not None else jnp.exp(m - mn)
            p = jnp.exp2(s - mn) if post is not None else jnp.exp(s - mn)
            l = alpha * l + jnp.sum(p, axis=-1, keepdims=True)
            acc = alpha * acc + jnp.dot(p.astype(BF16), v_ref[pl.ds(k0, tk), :], preferred_element_type=F32)
            return mn, l, acc

        def run(heads):
            def kstep(kb, carry):
                k0 = pl.multiple_of(kb * tk, tk)
                out = ()
                for n, j in enumerate(heads):
                    out += block(j, k0, carry[3 * n:3 * n + 3], False)
                return out

            init = (jnp.full((tq, 1), NEG, F32), jnp.zeros((tq, 1), F32), jnp.zeros((tq, 128), F32)) * len(heads)
            carry = lax.fori_loop(0, qi, kstep, init)
            o, st = jnp.zeros((tq, 128), F32), jnp.zeros((tq, 128), F32)
            for n, j in enumerate(heads):
                m, l, acc = block(j, kd, carry[3 * n:3 * n + 3], True)
                o = jnp.where(half == bool(j), acc / l, o)
                if post is not None:
                    m = m * math.log(2.0)
                st = jnp.where(lane == j, m + jnp.log(l), st)
            o_ref[...] = o.astype(BF16)
            st_ref[...] = st

        if n_heads % 2 == 0:
            run((0, 1))
        else:
            last = pl.program_id(1) == n_heads // 2
            pl.when(jnp.logical_not(last))(lambda: run((0, 1)))
            pl.when(last)(lambda: run((0,)))

    in_specs = [pl.BlockSpec((tq, hw), lambda b, p, i: (b * nq + i, p)),
                pl.BlockSpec((seq, hw), lambda b, p, i: (b, p)),
                pl.BlockSpec((seq, 128), lambda b, p, i: (b, p))]
    args = [q, k, v]
    if use_f:
        in_specs.append(pl.BlockSpec((1, 1, 8, seq), lambda b, p, i: (b, p, 0, 0)))
        args.append(f)
    oblk = pl.BlockSpec((tq, 128), lambda b, p, i: (b * nq + i, p))
    return pl.pallas_call(
        body, name="attn_fwd_" + kind, grid=(nb, 3, nq), in_specs=in_specs, out_specs=[oblk, oblk],
        out_shape=[jax.ShapeDtypeStruct((t, GW), BF16), jax.ShapeDtypeStruct((t, GW), F32)],
        compiler_params=_cparams(("arbitrary", "arbitrary", "arbitrary")),
    )(*args)


def _attn_bwd(kind, q, k, v, f, o, st, do, seq, scale, tq=512, tk=512, dep=None):
    t = v.shape[0]
    nb = t // seq
    tq, tk = min(tq, seq), min(tk, seq)
    nq = seq // tq
    nk = seq // tk
    hw = 256 if kind == "mla" else 128
    n_heads = A_HEADS if kind == "mla" else C_HEADS
    use_f = f is not None
    assert tq == tk

    def body(*refs):
        if use_f:
            (q_ref, k_ref, v_ref, f_ref, o_ref, st_ref, do_ref, dq_out, dk_out, dv_out, df_ref, dfq_ref,
             dq_ref, dk_ref, dv_ref) = refs
        else:
            q_ref, k_ref, v_ref, o_ref, st_ref, do_ref, dq_out, dk_out, dv_out, dq_ref, dk_ref, dv_ref = refs
        kj = pl.program_id(2)
        lane = lax.broadcasted_iota(jnp.int32, (1, 128), 1)
        half = lane >= 64

        @pl.when(kj == 0)
        def _():
            dq_ref[...] = jnp.zeros_like(dq_ref)
            if use_f:
                dfq_ref[...] = jnp.zeros_like(dfq_ref)

        dk_ref[...] = jnp.zeros_like(dk_ref)
        dv_ref[...] = jnp.zeros_like(dv_ref)
        if use_f:
            df_ref[...] = jnp.zeros_like(df_ref)
        vv = v_ref[...]
        diag = _causal_mask(kind, 0, 0, tq, tk)

        def qstep(qi, masked):
            q0 = pl.multiple_of(qi * tq, tq)
            rows = pl.ds(q0, tq)
            dov = do_ref[rows, :]
            dd = dov.astype(F32) * o_ref[rows, :]
            stv = st_ref[rows, :]

            def one_head(j):
                hm = half == bool(j)
                delta = jnp.sum(jnp.where(hm, dd, 0.0), axis=-1, keepdims=True)
                lse = stv[:, j:j + 1]
                if kind == "mla":
                    cols = slice(j * 128, (j + 1) * 128)
                    qh = q_ref[rows, cols]
                    kh = k_ref[:, cols]
                else:
                    cols = slice(0, 128)
                    qa = q_ref[rows, :]
                    qh = jnp.where(hm, qa, jnp.zeros_like(qa))
                    kh = k_ref[...]
                s = _nt(qh, kh) * scale
                if use_f:
                    s = s - f_ref[0, 0, j:j + 1, :]
                if masked:
                    s = jnp.where(diag, s, NEG)
                p = jnp.exp(s - lse)
                doh = jnp.where(hm, dov, jnp.zeros_like(dov))
                ds = p * (_nt(doh, vv) - delta)
                dsb = (ds * scale).astype(BF16)
                dv_ref[...] += _tn(p.astype(BF16), doh)
                dk_ref[:, cols] += _tn(dsb, qh)
                dqc = jnp.dot(dsb, kh, preferred_element_type=F32)
                if kind != "mla":
                    dqc = jnp.where(hm, dqc, 0.0)
                dq_ref[rows, cols] += dqc
                if use_f:
                    df_ref[0, 0, j:j + 1, :] += -jnp.sum(ds, axis=0, keepdims=True)
                    dfq_ref[rows, :] += jnp.where(lane == j, jnp.sum(ds, axis=-1, keepdims=True), 0.0)

            def both():
                one_head(0)
                one_head(1)

            if n_heads % 2 == 0:
                both()
            else:
                last = pl.program_id(1) == n_heads // 2
                pl.when(jnp.logical_not(last))(both)
                pl.when(last)(lambda: one_head(0))

        qstep(kj, True)

        def rest(qi, carry):
            qstep(qi, False)
            return carry

        lax.fori_loop(kj + 1, nq, rest, 0)
        dk_out[...] = dk_ref[...].astype(BF16)
        dv_out[...] = dv_ref[...].astype(BF16)

        @pl.when(kj == nk - 1)
        def _():
            dq_out[...] = dq_ref[...].astype(BF16)

    full_q = lambda wd: pl.BlockSpec((seq, wd), lambda b, p, i: (b, p))
    kblk = lambda wd: pl.BlockSpec((tk, wd), lambda b, p, i: (b * nk + i, p))
    in_specs = [full_q(hw), kblk(hw), kblk(128)]
    args = [q, k, v]
    if use_f:
        in_specs.append(pl.BlockSpec((1, 1, 8, tk), lambda b, p, i: (b, p, 0, i)))
        args.append(f)
    in_specs += [full_q(128), full_q(128), full_q(128)]
    args += [o, st, do]
    out_specs = [full_q(hw), kblk(hw), kblk(128)]
    out_shape = [jax.ShapeDtypeStruct((t, 3 * hw), BF16), jax.ShapeDtypeStruct((t, 3 * hw), BF16),
                 jax.ShapeDtypeStruct((t, GW), BF16)]
    scratch = [pltpu.VMEM((seq, hw), F32), pltpu.VMEM((tk, hw), F32), pltpu.VMEM((tk, 128), F32)]
    if use_f:
        out_specs += [pl.BlockSpec((1, 1, 8, tk), lambda b, p, i: (b, p, 0, i)), full_q(128)]
        out_shape += [jax.ShapeDtypeStruct((nb, 3, 8, seq), F32), jax.ShapeDtypeStruct((t, GW), F32)]
    body, in_specs, args = _after(dep, body, in_specs, args)
    return pl.pallas_call(
        body, name="attn_bwd_" + kind, grid=(nb, 3, nk), in_specs=in_specs, out_specs=out_specs,
        out_shape=out_shape, scratch_shapes=scratch,
        compiler_params=_cparams(("arbitrary", "arbitrary", "arbitrary")),
    )(*args)


BQ = 256
BWIN = BQ + B_LEFT


def _band_geometry():
    r = lax.broadcasted_iota(jnp.int32, (BQ, BWIN), 0)
    j = lax.broadcasted_iota(jnp.int32, (BQ, BWIN), 1)
    rc = lax.shift_right_logical(r, 6)
    jc = lax.shift_right_logical(j, 6)
    allowed = (jc - 8 <= rc) & (rc <= jc)
    return (r + B_LEFT - j) >= REL_CLIP, allowed, j < r


def _band_onehot(transposed, offset=0):
    shape = (BWIN, GW) if transposed else (GW, BWIN)
    kk = lax.broadcasted_iota(jnp.int32, shape, 1 if transposed else 0)
    x = lax.broadcasted_iota(jnp.int32, shape, 0 if transposed else 1) - offset
    x = jnp.where(x < 0, x + BWIN, x)
    return (kk == jnp.clip(B_LEFT - x, -REL_CLIP, REL_CLIP) + REL_CLIP).astype(F32)


def _band_table(rel_bias8):
    def body(b_ref, o_ref):
        hh = pl.program_id(0)
        u8 = jnp.dot(b_ref[...], _band_onehot(False), precision=HI, preferred_element_type=F32)
        rid = lax.broadcasted_iota(jnp.int32, (8, BWIN), 0)
        row = jnp.sum(jnp.where(rid == hh, u8, 0.0), axis=0, keepdims=True)
        far, allowed, _ = _band_geometry()
        tbl = pltpu.roll(jnp.broadcast_to(row, (BQ, BWIN)), 0, 1, stride=1, stride_axis=0)
        tbl = jnp.where(far, row[:, 0:1], tbl)
        o_ref[0] = jnp.where(allowed, tbl, NEG)

    return pl.pallas_call(
        body, name="band_table", grid=(6,),
        in_specs=[pl.BlockSpec((8, GW), lambda h: (0, 0))],
        out_specs=pl.BlockSpec((1, BQ, BWIN), lambda h: (h, 0, 0)),
        out_shape=jax.ShapeDtypeStruct((6, BQ, BWIN), F32),
        compiler_params=_cparams(("arbitrary",)),
    )(rel_bias8)


def _band_table_bwd(gtab):
    def body(g_ref, o_ref):
        gv = g_ref[0]
        _, _, wrapped = _band_geometry()
        gfar = jnp.sum(jnp.sum(jnp.where(wrapped, gv, 0.0), axis=-1, keepdims=True), axis=0, keepdims=True)
        anti = (lax.broadcasted_iota(jnp.int32, (BQ, BQ), 0) + lax.broadcasted_iota(jnp.int32, (BQ, BQ), 1)
                == BQ - 1).astype(F32)
        grev = jnp.dot(anti, jnp.where(wrapped, 0.0, gv), precision=HI, preferred_element_type=F32)
        near = pltpu.roll(grev, 0, 1, stride=1, stride_axis=0)
        y = jnp.broadcast_to(jnp.sum(near, axis=0, keepdims=True), (8, BWIN))
        gb = jnp.dot(y, _band_onehot(True, BQ - 1), precision=HI, preferred_element_type=F32)
        lane = lax.broadcasted_iota(jnp.int32, (8, GW), 1)
        o_ref[0] = gb + jnp.where(lane == 2 * REL_CLIP, gfar, 0.0)

    return pl.pallas_call(
        body, name="band_table_bwd", grid=(B_HEADS,),
        in_specs=[pl.BlockSpec((1, BQ, BWIN), lambda h: (h, 0, 0))],
        out_specs=pl.BlockSpec((1, 8, GW), lambda h: (h, 0, 0)),
        out_shape=jax.ShapeDtypeStruct((B_HEADS, 8, GW), F32),
        compiler_params=_cparams(("arbitrary",)),
    )(gtab)


def _band_fwd(q, k, v, table, seq, scale):
    t = q.shape[0]
    nb = t // seq
    nq = seq // BQ

    def body(q_ref, k_ref, v_ref, tb_ref, o_ref, st_ref, kpad, vpad):
        qi = pl.program_id(2)
        q0 = pl.multiple_of(qi * BQ, BQ)
        lane = lax.broadcasted_iota(jnp.int32, (1, 128), 1)
        half = lane >= 64

        @pl.when(qi == 0)
        def _():
            kpad[0:B_LEFT, :] = jnp.zeros((B_LEFT, 128), BF16)
            vpad[0:B_LEFT, :] = jnp.zeros((B_LEFT, 128), BF16)
            kpad[B_LEFT:, :] = k_ref[...]
            vpad[B_LEFT:, :] = v_ref[...]

        kw = kpad[pl.ds(q0, BWIN), :]
        vw = vpad[pl.ds(q0, BWIN), :]
        inside = lax.broadcasted_iota(jnp.int32, (BQ, BWIN), 1) >= B_LEFT - q0
        assert math.frexp(scale)[0] == 0.5
        qall = q_ref[...] * jnp.asarray(scale, BF16)

        def run(heads):
            o, st = jnp.zeros((BQ, 128), F32), jnp.zeros((BQ, 128), F32)
            for j in heads:
                qh = jnp.where(half == bool(j), qall, jnp.zeros_like(qall))
                s = jnp.where(inside, _nt(qh, kw) + tb_ref[j], NEG)
                m = jnp.max(s, axis=-1, keepdims=True)
                p = jnp.exp(s - m)
                l = jnp.sum(p, axis=-1, keepdims=True)
                o = jnp.where(half == bool(j), jnp.dot(p.astype(BF16), vw, preferred_element_type=F32) / l, o)
                st = jnp.where(lane == j, m + jnp.log(l), st)
            o_ref[...] = o.astype(BF16)
            st_ref[...] = st

        last = pl.program_id(1) == B_HEADS // 2
        pl.when(jnp.logical_not(last))(lambda: run((0, 1)))
        pl.when(last)(lambda: run((0,)))

    qblk = pl.BlockSpec((BQ, 128), lambda b, p, i: (b * nq + i, p))
    full = pl.BlockSpec((seq, 128), lambda b, p, i: (b, p))
    return pl.pallas_call(
        body, name="band_fwd", grid=(nb, 3, nq),
        in_specs=[qblk, full, full, pl.BlockSpec((2, BQ, BWIN), lambda b, p, i: (p, 0, 0))],
        out_specs=[qblk, qblk],
        out_shape=[jax.ShapeDtypeStruct((t, GW), BF16), jax.ShapeDtypeStruct((t, GW), F32)],
        scratch_shapes=[pltpu.VMEM((seq + B_LEFT, 128), BF16), pltpu.VMEM((seq + B_LEFT, 128), BF16)],
        compiler_params=_cparams(("arbitrary", "arbitrary", "arbitrary")),
    )(q, k, v, table)


def _band_bwd(q, k, v, table, o, st, do, seq, scale, dep=None):
    t = q.shape[0]
    nb = t // seq
    nq = seq // BQ

    def body(q_ref, k_ref, v_ref, tb_ref, o_ref, st_ref, do_ref, dq_ref, dk_ref, dv_ref, g_ref,
             kpad, vpad, dkpad, dvpad):
        b = pl.program_id(1)
        qi = pl.program_id(2)
        q0 = pl.multiple_of(qi * BQ, BQ)
        lane = lax.broadcasted_iota(jnp.int32, (1, 128), 1)
        half = lane >= 64

        @pl.when(qi == 0)
        def _():
            kpad[0:B_LEFT, :] = jnp.zeros((B_LEFT, 128), BF16)
            vpad[0:B_LEFT, :] = jnp.zeros((B_LEFT, 128), BF16)
            kpad[B_LEFT:, :] = k_ref[...]
            vpad[B_LEFT:, :] = v_ref[...]
            dkpad[...] = jnp.zeros_like(dkpad)
            dvpad[...] = jnp.zeros_like(dvpad)

        @pl.when((qi == 0) & (b == 0))
        def _():
            g_ref[...] = jnp.zeros_like(g_ref)

        win = pl.ds(q0, BWIN)
        kw = kpad[win, :]
        vw = vpad[win, :]
        inside = lax.broadcasted_iota(jnp.int32, (BQ, BWIN), 1) >= B_LEFT - q0
        qall = q_ref[...]
        dov = do_ref[...]
        dd = dov.astype(F32) * o_ref[...]
        stv = st_ref[...]

        def run(heads):
            dq = jnp.zeros((BQ, 128), F32)
            for j in heads:
                hm = half == bool(j)
                qh = jnp.where(hm, qall, jnp.zeros_like(qall))
                delta = jnp.sum(jnp.where(hm, dd, 0.0), axis=-1, keepdims=True)
                s = jnp.where(inside, _nt(qh, kw) * scale + tb_ref[j], NEG)
                p = jnp.exp(s - stv[:, j:j + 1])
                doh = jnp.where(hm, dov, jnp.zeros_like(dov))
                ds = p * (_nt(doh, vw) - delta)
                g_ref[j] += ds
                dsb = (ds * scale).astype(BF16)
                dvpad[win, :] += _tn(p.astype(BF16), doh)
                dkpad[win, :] += _tn(dsb, qh)
                dq = dq + jnp.where(hm, jnp.dot(dsb, kw, preferred_element_type=F32), 0.0)
            dq_ref[...] = dq.astype(BF16)

        last = pl.program_id(0) == B_HEADS // 2
        pl.when(jnp.logical_not(last))(lambda: run((0, 1)))
        pl.when(last)(lambda: run((0,)))

        @pl.when(qi == nq - 1)
        def _():
            dk_ref[...] = dkpad[B_LEFT:, :].astype(BF16)
            dv_ref[...] = dvpad[B_LEFT:, :].astype(BF16)

    qblk = pl.BlockSpec((BQ, 128), lambda p, b, i: (b * nq + i, p))
    full = pl.BlockSpec((seq, 128), lambda p, b, i: (b, p))
    tblk = pl.BlockSpec((2, BQ, BWIN), lambda p, b, i: (p, 0, 0))
    body, in_specs, args = _after(dep, body, [qblk, full, full, tblk, qblk, qblk, qblk], [q, k, v, table, o, st, do])
    return pl.pallas_call(
        body, name="band_bwd", grid=(3, nb, nq),
        in_specs=in_specs,
        out_specs=[qblk, full, full, tblk],
        out_shape=[jax.ShapeDtypeStruct((t, GW), BF16), jax.ShapeDtypeStruct((t, GW), BF16),
                   jax.ShapeDtypeStruct((t, GW), BF16), jax.ShapeDtypeStruct((6, BQ, BWIN), F32)],
        scratch_shapes=[pltpu.VMEM((seq + B_LEFT, 128), BF16), pltpu.VMEM((seq + B_LEFT, 128), BF16),
                        pltpu.VMEM((seq + B_LEFT, 128), F32), pltpu.VMEM((seq + B_LEFT, 128), F32)],
        compiler_params=_cparams(("arbitrary", "arbitrary", "arbitrary")),
    )(*args)


def _fox_prep(cf, fb, seq):
    nb = cf.shape[0] // seq
    nblk = seq // 128

    def body(cf_ref, fb_ref, f_ref):
        x = cf_ref[...] + fb_ref[...]
        lf = jnp.minimum(x, 0.0) - jnp.log1p(jnp.exp(-jnp.abs(x)))
        rows = lf.T[0:8, :]
        upper = (lax.broadcasted_iota(jnp.int32, (128, 128), 0)
                 <= lax.broadcasted_iota(jnp.int32, (128, 128), 1)).astype(F32)
        carry = jnp.zeros((8, 1), F32)
        for blk in range(nblk):
            sl = slice(blk * 128, (blk + 1) * 128)
            cs = jnp.dot(rows[:, sl], upper, precision=HI, preferred_element_type=F32) + carry
            carry = cs[:, 127:128]
            f_ref[0, 0, :, sl] = cs
            f_ref[0, 1, :, sl] = pltpu.roll(cs, 6, 0)
            f_ref[0, 2, :, sl] = pltpu.roll(cs, 4, 0)

    return pl.pallas_call(
        body, name="fox_prep", grid=(nb,),
        in_specs=[pl.BlockSpec((seq, 128), lambda b: (b, 0)), pl.BlockSpec((1, 128), lambda b: (0, 0))],
        out_specs=pl.BlockSpec((1, 3, 8, seq), lambda b: (b, 0, 0, 0)),
        out_shape=jax.ShapeDtypeStruct((nb, 3, 8, seq), F32),
        compiler_params=_cparams(("arbitrary",)),
    )(cf, fb)


def _fox_prep_bwd(df, dfq, cf, fb, seq):
    nb = cf.shape[0] // seq
    nblk = seq // 128

    def body(df_ref, dfq_ref, cf_ref, fb_ref, dcf_ref, dfb_ref, wide):
        b = pl.program_id(0)
        row = lax.broadcasted_iota(jnp.int32, (8, seq), 0)
        dfh = None
        for p in range(3):
            both = df_ref[0, p] + dfq_ref[:, p * 128:(p + 1) * 128].T[0:8, :]
            both = jnp.where(row < 2, both, 0.0)
            if p:
                both = pltpu.roll(both, 2 * p, 0)
            dfh = both if dfh is None else dfh + both
        lower = (lax.broadcasted_iota(jnp.int32, (128, 128), 0)
                 >= lax.broadcasted_iota(jnp.int32, (128, 128), 1)).astype(F32)
        wide[...] = jnp.zeros_like(wide)
        carry = jnp.zeros((8, 1), F32)
        for blk in reversed(range(nblk)):
            sl = slice(blk * 128, (blk + 1) * 128)
            rc = jnp.dot(dfh[:, sl], lower, precision=HI, preferred_element_type=F32) + carry
            carry = rc[:, 0:1]
            wide[0:8, sl] = rc
        dl = wide[...].T
        x = cf_ref[...] + fb_ref[...]
        dcf = dl * (1.0 / (1.0 + jnp.exp(x)))
        dcf_ref[...] = dcf.astype(BF16)
        part = jnp.sum(dcf, axis=0, keepdims=True)

        @pl.when(b == 0)
        def _():
            dfb_ref[...] = part

        @pl.when(b != 0)
        def _():
            dfb_ref[...] += part

    return pl.pallas_call(
        body, name="fox_prep_bwd", grid=(nb,),
        in_specs=[pl.BlockSpec((1, 3, 8, seq), lambda b: (b, 0, 0, 0)), pl.BlockSpec((seq, GW), lambda b: (b, 0)),
                  pl.BlockSpec((seq, 128), lambda b: (b, 0)), pl.BlockSpec((1, 128), lambda b: (0, 0))],
        out_specs=[pl.BlockSpec((seq, 128), lambda b: (b, 0)), pl.BlockSpec((1, 128), lambda b: (0, 0))],
        out_shape=[jax.ShapeDtypeStruct(cf.shape, BF16), jax.ShapeDtypeStruct((1, 128), F32)],
        scratch_shapes=[pltpu.VMEM((128, seq), F32)],
        compiler_params=_cparams(("arbitrary",)),
    )(df, dfq, cf, fb)


def _gate_out(oa, ob, oc, gates, w, x, gate, seq, tm=WIDE_ROW_TILE):
    t = x.shape[0]
    tm = min(tm, seq)
    tps = seq // tm

    def body(oa_ref, ob_ref, oc_ref, g_ref, w_ref, x_ref, gt_ref, xo_ref, y_ref, u_ref):
        for n, o_ref in enumerate((oa_ref, ob_ref, oc_ref)):
            sl = slice(n * GW, (n + 1) * GW)
            gv = g_ref[:, sl].astype(F32)
            u_ref[:, sl] = (o_ref[...] * (gv * _sigmoid(gv))).astype(BF16)
        y = jnp.dot(u_ref[...], w_ref[...], preferred_element_type=F32)
        y_ref[...] = y.astype(BF16)
        xo_ref[...] = x_ref[...] + gt_ref[0] * y

    row = lambda wd: pl.BlockSpec((tm, wd), lambda i: (i, 0))
    return pl.pallas_call(
        body, name="gate_out", grid=(t // tm,),
        in_specs=[row(GW), row(GW), row(GW), row(U_PAD), pl.BlockSpec((U_PAD, D_MODEL), lambda i: (0, 0)),
                  row(D_MODEL), pl.BlockSpec((1, 1, D_MODEL), lambda i: (i // tps, 0, 0))],
        out_specs=[row(D_MODEL), row(D_MODEL), row(U_PAD)],
        out_shape=[jax.ShapeDtypeStruct((t, D_MODEL), F32), jax.ShapeDtypeStruct((t, D_MODEL), BF16),
                   jax.ShapeDtypeStruct((t, U_PAD), BF16)],
        compiler_params=_cparams(("arbitrary",)),
    )(oa, ob, oc, gates, w, x, gate)


def _gate_out_bwd(dxo, y, gate, oa, ob, oc, gates, w_t, seq, tm=WIDE_ROW_TILE, dep=None):
    t = dxo.shape[0]
    tm = min(tm, seq)
    tps = seq // tm
    nb = t // seq

    def body(dxo_ref, y_ref, gt_ref, oa_ref, ob_ref, oc_ref, g_ref, wt_ref,
             dy_ref, doa_ref, dob_ref, doc_ref, dg_ref, dgt_ref):
        i = pl.program_id(0)
        dxo_v = dxo_ref[...]
        dgt = jnp.sum(dxo_v * y_ref[...].astype(F32), axis=0, keepdims=True)
        dyb = (dxo_v * gt_ref[0]).astype(BF16)
        dy_ref[...] = dyb
        du = _nt(dyb, wt_ref[...])
        for n, (o_ref, do_ref) in enumerate(((oa_ref, doa_ref), (ob_ref, dob_ref), (oc_ref, doc_ref))):
            sl = slice(n * GW, (n + 1) * GW)
            gv = g_ref[:, sl].astype(F32)
            sg = _sigmoid(gv)
            dun = du[:, sl]
            do_ref[...] = (dun * (gv * sg)).astype(BF16)
            dg_ref[:, sl] = (dun * o_ref[...] * (sg * (1.0 + gv * (1.0 - sg)))).astype(BF16)

        @pl.when(i % tps == 0)
        def _():
            dgt_ref[0] = dgt

        @pl.when(i % tps != 0)
        def _():
            dgt_ref[0] += dgt

    row = lambda wd: pl.BlockSpec((tm, wd), lambda i: (i, 0))
    per_b = pl.BlockSpec((1, 1, D_MODEL), lambda i: (i // tps, 0, 0))
    in_specs = [row(D_MODEL), row(D_MODEL), per_b, row(GW), row(GW), row(GW), row(U_PAD),
                pl.BlockSpec((U_PAD, D_MODEL), lambda i: (0, 0))]
    body, in_specs, args = _after(dep, body, in_specs, [dxo, y, gate, oa, ob, oc, gates, w_t])
    return pl.pallas_call(
        body, name="gate_out_bwd", grid=(t // tm,), in_specs=in_specs,
        out_specs=[row(D_MODEL), row(GW), row(GW), row(GW), row(U_PAD), per_b],
        out_shape=[jax.ShapeDtypeStruct((t, D_MODEL), BF16), jax.ShapeDtypeStruct((t, GW), BF16),
                   jax.ShapeDtypeStruct((t, GW), BF16), jax.ShapeDtypeStruct((t, GW), BF16),
                   jax.ShapeDtypeStruct((t, U_PAD), BF16), jax.ShapeDtypeStruct((nb, 1, D_MODEL), F32)],
        compiler_params=_cparams(("arbitrary",)),
    )(*args)


def _final_loss(x, target, g, tm=WIDE_ROW_TILE):
    t = x.shape[0]
    tm = min(tm, t)

    def body(x_ref, t_ref, g_ref, dx_ref, loss_ref, dg_ref):
        i = pl.program_id(0)
        xv = x_ref[...]
        rstd = lax.rsqrt(jnp.mean(xv * xv, axis=-1, keepdims=True) + EPS)
        xn = xv * rstd
        gv = g_ref[...]
        err = xn * gv - t_ref[...]
        dy = err * (1.0 / D_MODEL)
        dxn = dy * gv
        dx_ref[...] = rstd * (dxn - xn * jnp.mean(dxn * xn, axis=-1, keepdims=True))
        lp = jnp.sum(err * err, axis=0, keepdims=True) * (0.5 / D_MODEL)
        dgp = jnp.sum(dy * xn, axis=0, keepdims=True)

        @pl.when(i == 0)
        def _():
            loss_ref[...] = lp
            dg_ref[...] = dgp

        @pl.when(i != 0)
        def _():
            loss_ref[...] += lp
            dg_ref[...] += dgp

    row = pl.BlockSpec((tm, D_MODEL), lambda i: (i, 0))
    vec = pl.BlockSpec((1, D_MODEL), lambda i: (0, 0))
    return pl.pallas_call(
        body, name="final_loss", grid=(t // tm,),
        in_specs=[row, row, vec], out_specs=[row, vec, vec],
        out_shape=[jax.ShapeDtypeStruct((t, D_MODEL), F32), jax.ShapeDtypeStruct((1, D_MODEL), F32),
                   jax.ShapeDtypeStruct((1, D_MODEL), F32)],
        compiler_params=_cparams(("arbitrary",)),
    )(x, target, g)


def _adamw(w, gslots, m, v, name, tr=None):
    nl, r, c = w.shape
    ns = gslots.shape[0]
    tr = r if tr is None else tr

    def body(w_ref, g_ref, m_ref, v_ref, go_ref, d_ref, mo_ref, vo_ref):
        g = g_ref[0].astype(F32)
        for j in range(1, ns):
            g = g + g_ref[j].astype(F32)
        mn = ADAM_B1 * m_ref[...] + (1.0 - ADAM_B1) * g
        vn = ADAM_B2 * v_ref[...] + (1.0 - ADAM_B2) * jnp.square(g)
        m_hat = mn / (1.0 - ADAM_B1 ** ADAM_STEP)
        v_hat = vn / (1.0 - ADAM_B2 ** ADAM_STEP)
        go_ref[...] = g
        d_ref[...] = -ADAM_LR * (m_hat / (jnp.sqrt(v_hat) + ADAM_EPS) + ADAM_WD * w_ref[...])
        mo_ref[...] = mn
        vo_ref[...] = vn

    blk = pl.BlockSpec((1, tr, c), lambda l, i: (l, i, 0))
    return pl.pallas_call(
        body, name=name, grid=(nl, r // tr),
        in_specs=[blk, pl.BlockSpec((ns, 1, tr, c), lambda l, i: (0, l, i, 0)), blk, blk],
        out_specs=[blk] * 4, out_shape=[jax.ShapeDtypeStruct((nl, r, c), F32)] * 4,
        compiler_params=_cparams(("arbitrary", "arbitrary")),
    )(w, gslots, m, v)


def _rope_tables(positions):
    inv = ROPE_THETA ** (-jnp.arange(0, A_ROPE, 2, dtype=F32) / A_ROPE)
    ang = positions.astype(F32)[:, None] * inv
    cos, sin = jnp.cos(ang), jnp.sin(ang)
    t = positions.shape[0]
    one = jnp.ones((t, 64), F32)
    zero16 = jnp.zeros((t, 16), F32)
    cos_t = jnp.concatenate([one, cos, cos, jnp.ones((t, 32), F32)], axis=1)
    sin_a = jnp.concatenate([jnp.zeros((t, 64), F32), -sin, zero16, jnp.zeros((t, 32), F32)], axis=1)
    sin_b = jnp.concatenate([jnp.zeros((t, 64), F32), zero16, sin, jnp.zeros((t, 32), F32)], axis=1)
    return cos_t, sin_a, sin_b


def _pad_heads(w, real, padded, nheads, axis):
    shp = w.shape[:axis] + (nheads, real) + w.shape[axis + 1:]
    w = w.reshape(shp)
    pad = [(0, 0)] * w.ndim
    pad[axis + 1] = (0, padded - real)
    w = jnp.pad(w, pad)
    return w.reshape(w.shape[:axis] + (nheads * padded,) + w.shape[axis + 2:])


def kernel(x, c, positions, w_ada, b_ada, norm_g, w_in, a_q_norm_g, a_w_uq, a_kv_norm_g, a_w_ukv, b_rel_bias, c_forget_b, w_out, final_g, loss_target, m_w_ada, m_b_ada, m_norm_g, m_w_in, m_a_q_norm_g, m_a_w_uq, m_a_kv_norm_g, m_a_w_ukv, m_b_rel_bias, m_c_forget_b, m_w_out, m_final_g, v_w_ada, v_b_ada, v_norm_g, v_w_in, v_a_q_norm_g, v_a_w_uq, v_a_kv_norm_g, v_a_w_ukv, v_b_rel_bias, v_c_forget_b, v_w_out, v_final_g):
    nb, seq, _ = x.shape
    t = nb * seq
    me = 4 * lax.axis_index("x") + 2 * lax.axis_index("y") + lax.axis_index("c")
    x2 = x.reshape(t, D_MODEL)
    tgt = loss_target.reshape(t, D_MODEL)
    cos_t, sin_a, sin_b = _rope_tables(positions.reshape(t))

    def shards(l):
        return [_pad_runs(w_in[l].astype(BF16), IN_RUNS, N_PAD, 1), w_out[l].astype(BF16),
                a_w_uq[l].astype(BF16), a_w_ukv[l].astype(BF16)]

    def prepare(gi, go, gq, gkv):
        return dict(w_in=gi.reshape(D_MODEL, N_PAD), **prepare_rest(go, gq, gkv))

    def prepare_rest(go, gq, gkv):
        wo = _pad_runs(go.reshape(D_MODEL, D_MODEL), OUT_RUNS, U_PAD, 0)
        wq = jnp.transpose(gq, (1, 0, 2)).reshape(A_Q_RANK, A_HEADS * (A_NOPE + A_ROPE))
        wq = _pad_heads(wq, A_NOPE + A_ROPE, HEAD_PAD, A_HEADS, 1)
        wkv = jnp.transpose(gkv, (1, 0, 2)).reshape(A_KV_RANK, A_HEADS, 2 * A_NOPE)
        wk = jnp.pad(wkv[:, :, :A_NOPE], ((0, 0), (0, 0), (0, HEAD_PAD - A_NOPE))).reshape(A_KV_RANK, A_HEADS * HEAD_PAD)
        wv = wkv[:, :, A_NOPE:].reshape(A_KV_RANK, GW)
        return dict(w_out=wo, wuq=wq, wuq_t=wq.T, wk=wk, wk_t=wk.T, wv=wv, wv_t=wv.T)

    shards0 = shards(0)
    w_in0_g, c_g = _gather([shards0[0], c], "gather_w_in0")
    c_all = c_g.reshape(N_DEV * nb, D_MODEL)
    weights = [dict(w_in=w_in0_g.reshape(D_MODEL, N_PAD)), None]

    c_act, mod_cols = _ada_fwd(c_all, w_ada)
    (mod_g,) = _gather([mod_cols], "gather_mod")
    rest0, rest0_token = _split_start("gather", shards0[1:], "gather_rest0_start", after=mod_g)
    mod_all = jnp.transpose(mod_g, (1, 2, 0, 3)).reshape(DEPTH, N_DEV * nb, 3 * D_MODEL)
    mod = lax.dynamic_slice_in_dim(mod_all, me * nb, nb, axis=1) + b_ada[:, None, :]

    fb_pad = jnp.pad(c_forget_b, ((0, 0), (0, 128 - C_HEADS)))
    a_scale = (A_NOPE + A_ROPE) ** -0.5
    h_scale = CHUNK ** -0.5

    saved = []
    xl = x2
    for l in range(DEPTH):
        if l == 1:
            weights[1] = prepare(*_split_wait(gather1, xl, "gather_weights1_wait")[1])
        w = weights[l]
        shift, scale, gate = mod[l, :, :D_MODEL], mod[l, :, D_MODEL:2 * D_MODEL], mod[l, :, 2 * D_MODEL:]
        ss = jnp.stack([shift, 1.0 + scale], axis=1)
        gate3 = gate[:, None, :]
        h, cq, ckv, kpe, gates, bq, bk, bv, cq2, ck, cv, cf = _ln_in(
            xl, ss, norm_g[l:l + 1], w["w_in"], seq, dep=rest0_token if l == 0 else None)
        gather1_token = None
        if l == 0:
            w.update(prepare_rest(*_split_wait(rest0, h, "gather_rest0_wait")[1]))
            gather1, gather1_token = _split_start("gather", shards(1), "gather_weights1_start", after=w["w_out"])
        q, k, v, cqn, ckvn = _mla_prep(cq, ckv, kpe, a_q_norm_g[l:l + 1], a_kv_norm_g[l:l + 1],
                                       w["wuq"], w["wk"], w["wv"], cos_t, sin_a, sin_b, dep=gather1_token)
        oa, sta = _attn_fwd("mla", q, k, v, None, seq, a_scale)
        table = _band_table(jnp.pad(b_rel_bias[l], ((0, 8 - B_HEADS), (0, GW - N_REL))))
        ob, stb = _band_fwd(bq, bk, bv, table, seq, h_scale)
        fcum = _fox_prep(cf, fb_pad[l:l + 1], seq)
        oc, stc = _attn_fwd("fox", cq2, ck, cv, fcum, seq, h_scale)
        xn, y, u = _gate_out(oa, ob, oc, gates, w["w_out"], xl, gate3, seq)
        saved.append(dict(x=xl, ss=ss, gate3=gate3, h=h, cq=cq, ckv=ckv, gates=gates, bq=bq, bk=bk, bv=bv,
                          cq2=cq2, ck=ck, cv=cv, cf=cf, q=q, k=k, v=v, cqn=cqn, ckvn=ckvn, oa=oa, sta=sta,
                          table=table, ob=ob, stb=stb, fcum=fcum, oc=oc, stc=stc, y=y, u=u))
        xl = xn

    dx, loss_lanes, g_final = _final_loss(xl, tgt, final_g[None, :])
    loss = lax.psum(jnp.sum(loss_lanes), AXES)

    rows = D_MODEL // N_DEV
    core = lax.axis_index("c").astype(jnp.int32).reshape(1)
    n_seg_a = 4
    dmods, smalls, parts = [None] * DEPTH, [None] * DEPTH, [None] * DEPTH
    pair1 = chips1 = pair1_token = chips1_token = None
    for l in reversed(range(DEPTH)):
        s, w = saved[l], weights[l]
        dy, doa, dob, doc, dgates, dgate = _gate_out_bwd(dx, s["y"], s["gate3"], s["oa"], s["ob"], s["oc"],
                                                         s["gates"], w["w_out"], seq, dep=pair1_token)
        g_out = _unpad_runs(_matmul_tn(s["u"], dy, "dw_out"), OUT_RUNS, 0)
        if l == 0:
            own, from_sib = _split_wait(pair1, g_out, "grads1_pair_wait")
            chips1, chips1_token = _split_start("chips", _pair_add(core, own, from_sib, "grads1_add"), "grads1_chips_start")
        dq, dk, dv = _attn_bwd("mla", s["q"], s["k"], s["v"], None, s["oa"], s["sta"], doa, seq, a_scale,
                               dep=chips1_token)
        dbq, dbk, dbv, gtab = _band_bwd(s["bq"], s["bk"], s["bv"], s["table"], s["ob"], s["stb"], dob, seq, h_scale,
                                        dep=chips1_token)
        g_rel = _band_table_bwd(gtab)[:, 0, :N_REL]
        dcq2, dck, dcv, dfc, dfq = _attn_bwd("fox", s["cq2"], s["ck"], s["cv"], s["fcum"], s["oc"], s["stc"], doc,
                                             seq, h_scale, dep=chips1_token)
        dcf, dfb = _fox_prep_bwd(dfc, dfq, s["cf"], fb_pad[l:l + 1], seq)
        dcq, dckv, dkpe, dqlin, dklin, dgq, dgkv = _mla_prep_bwd(
            dq, dk, dv, s["cq"], s["ckv"], a_q_norm_g[l:l + 1], a_kv_norm_g[l:l + 1],
            w["wuq_t"], w["wk_t"], w["wv_t"], cos_t, sin_a, sin_b)
        gq_pad = _matmul_tn(s["cqn"], dqlin, "dw_uq")
        g_uq = gq_pad.reshape(A_Q_RANK, A_HEADS, HEAD_PAD)[:, :, :A_NOPE + A_ROPE].reshape(A_Q_RANK, -1)
        gkv_pad = _matmul_tn(s["ckvn"], [dklin, dv], "dw_ukv")
        gk_pad = gkv_pad[:, :A_HEADS * HEAD_PAD].reshape(A_KV_RANK, A_HEADS, HEAD_PAD)[:, :, :A_NOPE]
        gv_pad = gkv_pad[:, A_HEADS * HEAD_PAD:].reshape(A_KV_RANK, A_HEADS, A_NOPE)
        g_ukv = jnp.concatenate([gk_pad, gv_pad], axis=2).reshape(A_KV_RANK, -1)
        dz = [dcq, dckv, dkpe, dgates, dbq, dbk, dbv, dcq2, dck, dcv, dcf]
        g_in_a = _matmul_tn(s["h"], dz[:n_seg_a], "dw_in_a")
        first = [g_in_a.reshape(N_DEV, rows, -1), g_out.reshape(N_DEV, rows, D_MODEL),
                 g_uq.reshape(A_Q_RANK, N_DEV, -1).transpose(1, 0, 2), g_ukv.reshape(A_KV_RANK, N_DEV, -1).transpose(1, 0, 2)]
        if l == 1:
            g_in_b = _matmul_tn(s["h"], dz[n_seg_a:], "dw_in_b")
            pair1, pair1_token = _split_start("pair", first + [g_in_b.reshape(N_DEV, rows, -1)], "grads1_pair_start")
            tail_token = None
        else:
            pair0a, pair0a_token = _split_start("pair", first, "grads0a_pair_start")
            g_in_b = _matmul_tn(s["h"], dz[n_seg_a:], "dw_in_b", dep=pair0a_token)
            own, from_sib = _split_wait(pair0a, g_in_b, "grads0a_pair_wait")
            sums0a = _pair_add(core, own, from_sib, "grads0a_add")
            pair0b, pair0b_token = _split_start("pair", [g_in_b.reshape(N_DEV, rows, -1)], "grads0b_pair_start",
                                                after=sums0a[0])
            chips0a, tail_token = _split_start("chips", sums0a, "grads0a_chips_start", after=pair0b_token)
        dx, dss, dg_norm = _ln_in_bwd(dz, w["w_in"], s["x"], s["ss"], norm_g[l:l + 1], dx, seq, dep=tail_token)
        dmods[l] = jnp.concatenate([dss[:, 0, :], dss[:, 1, :], dgate[:, 0, :]], axis=1)
        smalls[l] = [dg_norm.reshape(-1), dgq.reshape(-1), dgkv.reshape(-1), g_rel.reshape(-1),
                     dfb[0, :C_HEADS]]
    grad_x = dx.reshape(nb, seq, D_MODEL)
    parts[1] = _split_wait(chips1, dx, "grads1_chips_wait")[1]
    parts0a = _split_wait(chips0a, dx, "grads0a_chips_wait")[1]
    own, from_sib = _split_wait(pair0b, dx, "grads0b_pair_wait")

    small = jnp.concatenate([p for l in range(DEPTH) for p in smalls[l]] + [g_final.reshape(-1)])
    n_small = small.shape[0]
    small_rows = -(-n_small // 1024) * 8
    small = jnp.pad(small, (0, small_rows * 128 - n_small)).reshape(small_rows, 128)
    dmod_local = jnp.stack(dmods)
    dmod_g, small_g = _gather([dmod_local, small], "gather_small", dep=parts0a[0])
    chips0, chips0_token = _split_start("chips", _pair_add(core, own, from_sib, "grads0b_add"), "grads0b_chips_start",
                                        after=small_g)
    dmod_all = jnp.transpose(dmod_g, (1, 0, 2, 3)).reshape(DEPTH, N_DEV * nb, 3 * D_MODEL)
    cols = 3 * D_MODEL // N_DEV
    dmod_mine = lax.dynamic_slice_in_dim(dmod_all, me * cols, cols, axis=2)
    g_w_ada, g_b_ada = _ada_bwd(c_act, dmod_all, dmod_mine, chips0_token)
    small_sum = _sum_slots(small_g, "sum_small").reshape(-1)

    def split_small():
        out, pos = [], 0
        sizes = [D_MODEL, A_Q_RANK, A_KV_RANK, B_HEADS * N_REL, C_HEADS]
        per_layer = []
        for l in range(DEPTH):
            parts = []
            for sz in sizes:
                parts.append(small_sum[pos:pos + sz])
                pos += sz
            per_layer.append(parts)
        for j in range(len(sizes)):
            out.append(jnp.stack([per_layer[l][j] for l in range(DEPTH)]))
        out.append(small_sum[pos:pos + D_MODEL])
        return out

    g_norm, g_qn, g_kvn, g_relb, g_fb, g_fin = split_small()

    def adam(w, g, m, v, name, tr=None):
        shp = w.shape
        w3 = w.reshape((1,) * (3 - w.ndim) + shp)
        outs = _adamw(w3, g.reshape((-1,) + w3.shape), m.reshape(w3.shape), v.reshape(w3.shape), name, tr)
        return [o.reshape(shp) for o in outs]

    res = {
        "w_ada": adam(w_ada, g_w_ada, m_w_ada, v_w_ada, "adam_w_ada", 256),
        "b_ada": adam(b_ada, g_b_ada, m_b_ada, v_b_ada, "adam_b_ada"),
        "norm_g": adam(norm_g, g_norm, m_norm_g, v_norm_g, "adam_norm_g"),
        "a_q_norm_g": adam(a_q_norm_g, g_qn, m_a_q_norm_g, v_a_q_norm_g, "adam_q_norm"),
        "a_kv_norm_g": adam(a_kv_norm_g, g_kvn, m_a_kv_norm_g, v_a_kv_norm_g, "adam_kv_norm"),
        "b_rel_bias": adam(b_rel_bias, g_relb.reshape(b_rel_bias.shape), m_b_rel_bias, v_b_rel_bias, "adam_rel_bias"),
        "c_forget_b": adam(c_forget_b, g_fb, m_c_forget_b, v_c_forget_b, "adam_forget_b"),
        "final_g": adam(final_g, g_fin, m_final_g, v_final_g, "adam_final_g"),
    }
    parts[0] = list(parts0a) + list(_split_wait(chips0, res["w_ada"][1], "grads0b_chips_wait")[1])
    p_in = jnp.stack([_unpad_runs(jnp.concatenate([parts[l][0], parts[l][4]], axis=2), IN_RUNS, 2)
                      for l in range(DEPTH)], axis=1)
    p_out, p_uq, p_ukv = (jnp.stack([parts[l][j] for l in range(DEPTH)], axis=1) for j in (1, 2, 3))
    res.update({
        "w_in": adam(w_in, p_in, m_w_in, v_w_in, "adam_w_in", 64),
        "a_w_uq": adam(a_w_uq, p_uq, m_a_w_uq, v_a_w_uq, "adam_w_uq"),
        "a_w_ukv": adam(a_w_ukv, p_ukv, m_a_w_ukv, v_a_w_ukv, "adam_w_ukv"),
        "w_out": adam(w_out, p_out, m_w_out, v_w_out, "adam_w_out", 64),
    })
    names = ["w_ada", "b_ada", "norm_g", "w_in", "a_q_norm_g", "a_w_uq", "a_kv_norm_g", "a_w_ukv", "b_rel_bias",
             "c_forget_b", "w_out", "final_g"]
    outs = [loss, grad_x]
    for j in range(4):
        outs += [res[n][j] for n in names]
    return tuple(outs)
```

```python
import math

import jax
import jax.numpy as jnp
from jax import lax
from jax.experimental import pallas as pl
from jax.experimental.pallas import tpu as pltpu

F32 = jnp.float32
BF16 = jnp.bfloat16
HI = lax.Precision.HIGHEST

N_DEV = 8
AXES = ("x", "y", "c")
D_MODEL = 1024
DEPTH = 2
CHUNK = 64
EPS = 1e-6
NEG = -1e30
A_HEADS = 6
A_NOPE = 64
A_ROPE = 32
A_Q_RANK = 384
A_KV_RANK = 256
ROPE_THETA = 10000.0
B_HEADS = 5
B_LEFT = 512
REL_CLIP = 128
N_REL = 2 * REL_CLIP + 1
C_HEADS = 5
HEAD_PAD = 128
GW = 384
N_IN = 3621
ADAM_LR = 0.001
ADAM_B1 = 0.9
ADAM_B2 = 0.999
ADAM_EPS = 1e-08
ADAM_WD = 0.01
ADAM_STEP = 10
VMEM_LIMIT = 56 * 1024 * 1024
ROW_TILE = 512
WIDE_ROW_TILE = 1024

Z_SEGS = (
    ("cq", 0, 384, F32), ("ckv", 384, 256, F32), ("kpe", 640, 128, F32), ("gates", 768, 1152, BF16),
    ("bq", 1920, 384, BF16), ("bk", 2304, 384, BF16), ("bv", 2688, 384, BF16),
    ("cq2", 3072, 384, BF16), ("ck", 3456, 384, BF16), ("cv", 3840, 384, BF16), ("cf", 4224, 128, F32),
)
N_PAD = 4352
IN_RUNS = (
    (0, 384, 0), (384, 256, 384), (640 + 64, 32, 640),
    (768, 384, 672), (768 + 384, 320, 2016), (768 + 768, 320, 3301),
    (1920, 320, 1056), (2304, 320, 1376), (2688, 320, 1696),
    (3072, 320, 2336), (3456, 320, 2656), (3840, 320, 2976), (4224, 5, 3296),
)
OUT_RUNS = ((0, 384, 0), (384, 320, 384), (768, 320, 704))
U_PAD = 1152


def _cparams(sem=None, vmem=VMEM_LIMIT):
    return pltpu.CompilerParams(dimension_semantics=sem, vmem_limit_bytes=vmem)


def _after(dep, body, in_specs, args):
    if dep is None:
        return body, in_specs, args
    n = len(args)

    def ordered(*refs):
        return body(*refs[:n], *refs[n + 1:])

    return ordered, list(in_specs) + [pl.BlockSpec((8, 128), lambda *_: (0, 0))], list(args) + [dep]


def _pad_runs(w, runs, total, axis):
    order = sorted(runs)
    parts, pos = [], 0
    for off, wd, src in order:
        if off > pos:
            shp = list(w.shape)
            shp[axis] = off - pos
            parts.append(jnp.zeros(shp, w.dtype))
        parts.append(lax.slice_in_dim(w, src, src + wd, axis=axis))
        pos = off + wd
    if pos < total:
        shp = list(w.shape)
        shp[axis] = total - pos
        parts.append(jnp.zeros(shp, w.dtype))
    return jnp.concatenate(parts, axis=axis)


def _unpad_runs(w, runs, axis):
    order = sorted(runs, key=lambda r: r[2])
    return jnp.concatenate([lax.slice_in_dim(w, off, off + wd, axis=axis) for off, wd, _ in order], axis=axis)


def _sigmoid(x):
    return 1.0 / (1.0 + jnp.exp(-x))


N_CHIP = 4
ANY_SPEC = pl.BlockSpec(memory_space=pl.ANY)
MESH_ID = pl.DeviceIdType.MESH


def _gather(arrs, name, dep=None):
    n = len(arrs)
    nin = n + (dep is not None)

    def body(*refs):
        ins, outs = refs[:n], refs[nin:nin + n]
        send_sems, recv_sems, local_sems = refs[nin + n:]
        x, y, c = lax.axis_index("x"), lax.axis_index("y"), lax.axis_index("c")
        me, sib = (x, y, c), (x, y, 1 - c)
        chips = [(1 - x, y), (x, 1 - y), (1 - x, 1 - y)]

        def slot(px, py, pc):
            return 4 * px + 2 * py + pc

        def copy(a, k, block, to, src=None):
            dst = outs[a].at[slot(*block)]
            return pltpu.make_async_remote_copy(
                src_ref=dst if src is None else src, dst_ref=dst, send_sem=send_sems.at[a, k],
                recv_sem=recv_sems.at[a, k], device_id=to, device_id_type=MESH_ID)

        local = [pltpu.make_async_copy(ins[a], outs[a].at[slot(*me)], local_sems.at[a]) for a in range(n)]
        first = []
        for a in range(n):
            first.append(copy(a, 0, me, sib, src=ins[a]))
            first += [copy(a, 1 + j, me, (*chip, c), src=ins[a]) for j, chip in enumerate(chips)]
        for cp in local + first:
            cp.start()
        passed = []
        for j, chip in enumerate(chips):
            for a in range(n):
                copy(a, 1 + j, (*chip, c), me).wait_recv()
                fwd = copy(a, 4 + j, (*chip, c), sib)
                fwd.start()
                passed.append(fwd)
        for a in range(n):
            copy(a, 0, sib, me).wait_recv()
            for j, chip in enumerate(chips):
                copy(a, 4 + j, (*chip, 1 - c), me).wait_recv()
        for cp in first + passed:
            cp.wait_send()
        for cp in local:
            cp.wait()

    return pl.pallas_call(
        body, name=name, out_shape=[jax.ShapeDtypeStruct((N_DEV,) + a.shape, a.dtype) for a in arrs],
        in_specs=[ANY_SPEC] * nin, out_specs=[ANY_SPEC] * n,
        scratch_shapes=[pltpu.SemaphoreType.DMA((n, N_DEV - 1)), pltpu.SemaphoreType.DMA((n, N_DEV - 1)),
                        pltpu.SemaphoreType.DMA((n,))],
    )(*arrs, *([] if dep is None else [dep]))


HBM_SPEC = pl.BlockSpec(memory_space=pltpu.HBM)
SEM_SPEC = pl.BlockSpec(memory_space=pltpu.SEMAPHORE)
SPLIT_EFFECT = pltpu.SideEffectType.DATAFLOW_SIDE_EFFECTING
SPLIT_SEMS = {"gather": (N_DEV - 1, True), "pair": (N_CHIP, False), "chips": (N_CHIP - 1, True)}


def _split_descriptors(pattern, srcs, lands, sems):
    x, y, c = lax.axis_index("x"), lax.axis_index("y"), lax.axis_index("c")
    nsem, has_local = SPLIT_SEMS[pattern]
    per = 2 * nsem + int(has_local)
    starts, arrivals, local = [], [], []

    def remote(a, k, src, dst, to):
        return pltpu.make_async_remote_copy(src_ref=src, dst_ref=dst, send_sem=sems[a * per + k],
                                            recv_sem=sems[a * per + nsem + k], device_id=to, device_id_type=MESH_ID)

    for a in range(len(srcs)):
        if pattern == "gather":
            me = 4 * x + 2 * y + c
            local.append(pltpu.make_async_copy(srcs[a], lands[a].at[me], sems[a * per + 2 * nsem]))
            for k in range(1, N_DEV):
                px = (1 - x) if (k >> 2) & 1 else x
                py = (1 - y) if (k >> 1) & 1 else y
                pc = (1 - c) if k & 1 else c
                starts.append(remote(a, k - 1, srcs[a], lands[a].at[me], (px, py, pc)))
                arrivals.append(remote(a, k - 1, srcs[a], lands[a].at[4 * px + 2 * py + pc], (px, py, pc)))
        elif pattern == "pair":
            for q in range(N_CHIP):
                cp = remote(a, q, srcs[a].at[2 * q + 1 - c], lands[a].at[q], (x, y, 1 - c))
                starts.append(cp)
                arrivals.append(cp)
        else:
            mine = 2 * x + y
            local.append(pltpu.make_async_copy(srcs[a].at[mine], lands[a].at[mine], sems[a * per + 2 * nsem]))
            for k in range(1, N_CHIP):
                px = (1 - x) if (k >> 1) & 1 else x
                py = (1 - y) if k & 1 else y
                starts.append(remote(a, k - 1, srcs[a].at[2 * px + py], lands[a].at[mine], (px, py, c)))
                arrivals.append(remote(a, k - 1, srcs[a].at[2 * px + py], lands[a].at[2 * px + py], (px, py, c)))
    return starts, arrivals, local


def _split_start(pattern, arrs, name, after=None):
    n = len(arrs)
    extra = [] if after is None else [after]
    nsem, has_local = SPLIT_SEMS[pattern]
    if pattern == "gather":
        land_shapes = [(N_DEV,) + a.shape for a in arrs]
    elif pattern == "pair":
        land_shapes = [(N_CHIP,) + a.shape[1:] for a in arrs]
    else:
        land_shapes = [a.shape for a in arrs]
    nsem_out = n * (2 * nsem + int(has_local))

    def body(*refs):
        srcs, lands = refs[:n], refs[n:2 * n]
        first_sem = 2 * n + len(extra)
        sems = refs[first_sem:first_sem + nsem_out]
        token = refs[-1]
        starts, _, local = _split_descriptors(pattern, srcs, lands, sems)
        for cp in local + starts:
            cp.start()
        token[...] = jnp.zeros_like(token)

    out_shape = ([pltpu.SemaphoreType.DMA(())] * nsem_out + [pltpu.HBM(a.shape, a.dtype) for a in arrs]
                 + [pltpu.HBM(s, a.dtype) for s, a in zip(land_shapes, arrs)] + [jax.ShapeDtypeStruct((8, 128), F32)])
    ins = ([pltpu.with_memory_space_constraint(a, pltpu.HBM) for a in arrs]
           + [pltpu.with_memory_space_constraint(lax.empty(s, a.dtype), pltpu.HBM) for s, a in zip(land_shapes, arrs)])
    outs = pl.pallas_call(
        body, name=name, out_shape=out_shape, in_specs=[HBM_SPEC] * (2 * n) + [ANY_SPEC] * len(extra),
        out_specs=[SEM_SPEC] * nsem_out + [HBM_SPEC] * (2 * n) + [pl.BlockSpec(memory_space=pltpu.VMEM)],
        input_output_aliases={i: nsem_out + i for i in range(2 * n)},
        compiler_params=pltpu.CompilerParams(has_side_effects=SPLIT_EFFECT),
    )(*ins, *extra)
    handle = dict(pattern=pattern, n=n, sems=outs[:nsem_out], srcs=outs[nsem_out:nsem_out + n],
                  lands=outs[nsem_out + n:nsem_out + 2 * n])
    return handle, outs[-1]


def _split_wait(handle, after, name):
    pattern, n = handle["pattern"], handle["n"]
    nsem_in = len(handle["sems"])

    def body(*refs):
        srcs, lands = refs[:n], refs[n:2 * n]
        starts, arrivals, local = _split_descriptors(pattern, srcs, lands, refs[2 * n:2 * n + nsem_in])
        for cp in starts:
            cp.wait_send()
        for cp in arrivals:
            cp.wait_recv()
        for cp in local:
            cp.wait()

    srcs, lands = handle["srcs"], handle["lands"]
    outs = pl.pallas_call(
        body, name=name,
        out_shape=[pltpu.HBM(a.shape, a.dtype) for a in srcs] + [pltpu.HBM(a.shape, a.dtype) for a in lands],
        in_specs=[HBM_SPEC] * (2 * n) + [SEM_SPEC] * nsem_in + [ANY_SPEC], out_specs=[HBM_SPEC] * (2 * n),
        input_output_aliases={i: i for i in range(2 * n)},
        compiler_params=pltpu.CompilerParams(has_side_effects=SPLIT_EFFECT),
    )(*srcs, *lands, *handle["sems"], after)
    return outs[:n], outs[n:]


def _pair_add(core, a8s, b4s, name):
    n = len(a8s)

    def body(core_ref, *refs):
        for i in range(n):
            refs[2 * n + i][...] = (refs[i][...] + refs[n + i][...]).astype(BF16)

    own = [pl.BlockSpec((1,) + b.shape[1:], lambda q, core_ref: (2 * q + core_ref[0], 0, 0)) for b in b4s]
    slot = [pl.BlockSpec((1,) + b.shape[1:], lambda q, core_ref: (q, 0, 0)) for b in b4s]
    grid_spec = pltpu.PrefetchScalarGridSpec(num_scalar_prefetch=1, grid=(N_CHIP,), in_specs=own + slot, out_specs=slot)
    return pl.pallas_call(
        body, name=name, grid_spec=grid_spec, out_shape=[jax.ShapeDtypeStruct(b.shape, BF16) for b in b4s],
        compiler_params=_cparams(("arbitrary",)),
    )(core, *a8s, *b4s)


def _sum_slots(x, name):
    _, r, c = x.shape

    def body(x_ref, o_ref):
        acc = x_ref[0]
        for j in range(1, N_DEV):
            acc = acc + x_ref[j]
        o_ref[...] = acc

    return pl.pallas_call(body, name=name, out_shape=jax.ShapeDtypeStruct((r, c), F32))(x)


def _ada_fwd(c_all, w_ada):
    nb = c_all.shape[0]
    cols = w_ada.shape[2]

    def body(c_ref, w_ref, act_ref, mod_ref):
        cv = c_ref[...]
        act = cv * _sigmoid(cv)
        act_ref[...] = act
        for l in range(DEPTH):
            mod_ref[l] = jnp.dot(act, w_ref[l], precision=HI, preferred_element_type=F32)

    return pl.pallas_call(
        body, name="ada_fwd",
        out_shape=[jax.ShapeDtypeStruct((nb, D_MODEL), F32), jax.ShapeDtypeStruct((DEPTH, nb, cols), F32)],
        compiler_params=_cparams(),
    )(c_all, w_ada)


def _ada_bwd(c_act, dmod_all, dmod_mine, dep):
    nb = c_act.shape[0]
    cols = dmod_mine.shape[2]

    def body(act_ref, dall_ref, dmine_ref, dep_ref, gw_ref, gb_ref):
        act = act_ref[...]
        for l in range(DEPTH):
            gw_ref[l] = lax.dot_general(act, dmine_ref[l], (((0,), (0,)), ((), ())),
                                        precision=HI, preferred_element_type=F32)
            gb_ref[l:l + 1, :] = jnp.sum(dall_ref[l], axis=0, keepdims=True)

    return pl.pallas_call(
        body, name="ada_bwd",
        out_shape=[jax.ShapeDtypeStruct((DEPTH, D_MODEL, cols), F32),
                   jax.ShapeDtypeStruct((DEPTH, 3 * D_MODEL), F32)],
        compiler_params=_cparams(),
    )(c_act, dmod_all, dmod_mine, dep)


def _ln_in(x, ss, g, w, seq, tm=ROW_TILE, dep=None):
    t = x.shape[0]
    tm = min(tm, seq)
    tps = seq // tm

    def body(x_ref, ss_ref, g_ref, w_ref, h_ref, *outs):
        xv = x_ref[...]
        xn = xv * lax.rsqrt(jnp.mean(xv * xv, axis=-1, keepdims=True) + EPS)
        h = xn * g_ref[...] * ss_ref[0, 1:2, :] + ss_ref[0, 0:1, :]
        hb = h.astype(BF16)
        h_ref[...] = hb
        z = jnp.dot(hb, w_ref[...], preferred_element_type=F32)
        for o_ref, (_, off, wd, _) in zip(outs, Z_SEGS):
            o_ref[...] = z[:, off:off + wd].astype(o_ref.dtype)

    row = lambda wd: pl.BlockSpec((tm, wd), lambda i: (i, 0))
    in_specs = [row(D_MODEL), pl.BlockSpec((1, 2, D_MODEL), lambda i: (i // tps, 0, 0)),
                pl.BlockSpec((1, D_MODEL), lambda i: (0, 0)), pl.BlockSpec((D_MODEL, N_PAD), lambda i: (0, 0))]
    body, in_specs, args = _after(dep, body, in_specs, [x, ss, g, w])
    return pl.pallas_call(
        body, name="ln_in", grid=(t // tm,), in_specs=in_specs,
        out_specs=[row(D_MODEL)] + [row(wd) for _, _, wd, _ in Z_SEGS],
        out_shape=[jax.ShapeDtypeStruct((t, D_MODEL), BF16)]
        + [jax.ShapeDtypeStruct((t, wd), dt) for _, _, wd, dt in Z_SEGS],
        compiler_params=_cparams(("arbitrary",)),
    )(*args)


def _ln_in_bwd(dz, w_t, x, ss, g, dxo, seq, tm=ROW_TILE, dep=None):
    t = x.shape[0]
    tm = min(tm, seq)
    tps = seq // tm
    nb = t // seq
    nz = len(Z_SEGS)

    def body(*refs):
        dz_refs = refs[:nz]
        wt_ref, x_ref, ss_ref, g_ref, dxo_ref, dx_ref, dss_ref, dg_ref = refs[nz:]
        i = pl.program_id(0)
        dzc = jnp.concatenate([r[...].astype(BF16) for r in dz_refs], axis=1)
        dh = _nt(dzc, wt_ref[...])
        xv = x_ref[...]
        rstd = lax.rsqrt(jnp.mean(xv * xv, axis=-1, keepdims=True) + EPS)
        xn = xv * rstd
        gv = g_ref[...]
        s1 = ss_ref[0, 1:2, :]
        dxg = dh * s1
        dxn = dxg * gv
        dx = rstd * (dxn - xn * jnp.mean(dxn * xn, axis=-1, keepdims=True))
        dx_ref[...] = dxo_ref[...] + dx
        dshift = jnp.sum(dh, axis=0, keepdims=True)
        dscale = jnp.sum(dh * (xn * gv), axis=0, keepdims=True)
        dgp = jnp.sum(dxg * xn, axis=0, keepdims=True)

        @pl.when(i % tps == 0)
        def _():
            dss_ref[0, 0:1, :] = dshift
            dss_ref[0, 1:2, :] = dscale

        @pl.when(i % tps != 0)
        def _():
            dss_ref[0, 0:1, :] += dshift
            dss_ref[0, 1:2, :] += dscale

        @pl.when(i == 0)
        def _():
            dg_ref[...] = dgp

        @pl.when(i != 0)
        def _():
            dg_ref[...] += dgp

    row = lambda wd: pl.BlockSpec((tm, wd), lambda i: (i, 0))
    in_specs = ([row(wd) for _, _, wd, _ in Z_SEGS]
                + [pl.BlockSpec((D_MODEL, N_PAD), lambda i: (0, 0)), row(D_MODEL),
                   pl.BlockSpec((1, 2, D_MODEL), lambda i: (i // tps, 0, 0)),
                   pl.BlockSpec((1, D_MODEL), lambda i: (0, 0)), row(D_MODEL)])
    body, in_specs, args = _after(dep, body, in_specs, [*dz, w_t, x, ss, g, dxo])
    return pl.pallas_call(
        body, name="ln_in_bwd", grid=(t // tm,), in_specs=in_specs,
        out_specs=[row(D_MODEL), pl.BlockSpec((1, 2, D_MODEL), lambda i: (i // tps, 0, 0)),
                   pl.BlockSpec((1, D_MODEL), lambda i: (0, 0))],
        out_shape=[jax.ShapeDtypeStruct((t, D_MODEL), F32), jax.ShapeDtypeStruct((nb, 2, D_MODEL), F32),
                   jax.ShapeDtypeStruct((1, D_MODEL), F32)],
        compiler_params=_cparams(("arbitrary",)),
    )(*args)


def _matmul_tn(a, bs, name, tm=2048, dep=None):
    bs = list(bs) if isinstance(bs, (list, tuple)) else [bs]
    t, k = a.shape
    widths = [b.shape[1] for b in bs]
    n = sum(widths)
    tm = min(tm, t)

    def body(a_ref, *refs):
        b_refs, o_ref = refs[:-1], refs[-1]
        i = pl.program_id(0)
        av = a_ref[...].astype(BF16)
        parts = [b_ref[...].astype(BF16) for b_ref in b_refs]
        bv = parts[0] if len(parts) == 1 else jnp.concatenate(parts, axis=1)
        part = lax.dot_general(av, bv, (((0,), (0,)), ((), ())), preferred_element_type=F32)

        @pl.when(i == 0)
        def _():
            o_ref[...] = part

        @pl.when(i != 0)
        def _():
            o_ref[...] += part

    in_specs = [pl.BlockSpec((tm, k), lambda i: (i, 0))] + [pl.BlockSpec((tm, wd), lambda i: (i, 0)) for wd in widths]
    body, in_specs, args = _after(dep, body, in_specs, [a, *bs])
    return pl.pallas_call(
        body, name=name, grid=(t // tm,), in_specs=in_specs,
        out_specs=pl.BlockSpec((k, n), lambda i: (0, 0)),
        out_shape=jax.ShapeDtypeStruct((k, n), F32),
        compiler_params=_cparams(("arbitrary",)),
    )(*args)


def _rope(blk, cos_t, sin_a, sin_b):
    return blk * cos_t + pltpu.roll(blk, 112, 1) * sin_a + pltpu.roll(blk, 16, 1) * sin_b


def _unrope(d, cos_t, sin_a, sin_b):
    return d * cos_t + pltpu.roll(d * sin_a, 16, 1) + pltpu.roll(d * sin_b, 112, 1)


def _mla_prep(cq, ckv, kpe, gq, gkv, wuq, wk, wv, cos_t, sin_a, sin_b, tm=WIDE_ROW_TILE, dep=None):
    t = cq.shape[0]
    tm = min(tm, t)
    qw = A_HEADS * HEAD_PAD

    def body(cq_ref, ckv_ref, kpe_ref, gq_ref, gkv_ref, wuq_ref, wk_ref, wv_ref, c_ref, sa_ref, sb_ref,
             q_ref, k_ref, v_ref, cqn_ref, ckvn_ref):
        ct, sa, sb = c_ref[...], sa_ref[...], sb_ref[...]
        a = cq_ref[...]
        cqn = (a * lax.rsqrt(jnp.mean(a * a, axis=-1, keepdims=True) + EPS) * gq_ref[...]).astype(BF16)
        cqn_ref[...] = cqn
        b = ckv_ref[...]
        ckvn = (b * lax.rsqrt(jnp.mean(b * b, axis=-1, keepdims=True) + EPS) * gkv_ref[...]).astype(BF16)
        ckvn_ref[...] = ckvn
        qlin = jnp.dot(cqn, wuq_ref[...], preferred_element_type=F32)
        klin = jnp.dot(ckvn, wk_ref[...], preferred_element_type=F32)
        v_ref[...] = jnp.dot(ckvn, wv_ref[...], preferred_element_type=F32).astype(BF16)
        kr = _rope(kpe_ref[...], ct, sa, sb)
        for h in range(A_HEADS):
            sl = slice(h * HEAD_PAD, (h + 1) * HEAD_PAD)
            q_ref[:, sl] = _rope(qlin[:, sl], ct, sa, sb).astype(BF16)
            k_ref[:, sl] = (klin[:, sl] + kr).astype(BF16)

    row = lambda wd: pl.BlockSpec((tm, wd), lambda i: (i, 0))
    full = lambda r, c: pl.BlockSpec((r, c), lambda i: (0, 0))
    in_specs = [row(A_Q_RANK), row(A_KV_RANK), row(128), full(1, A_Q_RANK), full(1, A_KV_RANK),
                full(A_Q_RANK, qw), full(A_KV_RANK, qw), full(A_KV_RANK, GW), row(128), row(128), row(128)]
    body, in_specs, args = _after(dep, body, in_specs, [cq, ckv, kpe, gq, gkv, wuq, wk, wv, cos_t, sin_a, sin_b])
    return pl.pallas_call(
        body, name="mla_prep", grid=(t // tm,), in_specs=in_specs,
        out_specs=[row(qw), row(qw), row(GW), row(A_Q_RANK), row(A_KV_RANK)],
        out_shape=[jax.ShapeDtypeStruct((t, qw), BF16), jax.ShapeDtypeStruct((t, qw), BF16),
                   jax.ShapeDtypeStruct((t, GW), BF16), jax.ShapeDtypeStruct((t, A_Q_RANK), BF16),
                   jax.ShapeDtypeStruct((t, A_KV_RANK), BF16)],
        compiler_params=_cparams(("arbitrary",)),
    )(*args)


def _mla_prep_bwd(dq, dk, dv, cq, ckv, gq, gkv, wuq_t, wk_t, wv_t, cos_t, sin_a, sin_b, tm=WIDE_ROW_TILE):
    t = cq.shape[0]
    tm = min(tm, t)
    qw = A_HEADS * HEAD_PAD

    def body(dq_ref, dk_ref, dv_ref, cq_ref, ckv_ref, gq_ref, gkv_ref, wuqt_ref, wkt_ref, wvt_ref,
             c_ref, sa_ref, sb_ref, dcq_ref, dckv_ref, dkpe_ref, dql_ref, dkl_ref, dgq_ref, dgkv_ref):
        i = pl.program_id(0)
        ct, sa, sb = c_ref[...], sa_ref[...], sb_ref[...]
        lane = lax.broadcasted_iota(jnp.int32, (1, HEAD_PAD), 1)
        nope = lane < A_NOPE
        rope = (lane >= A_NOPE) & (lane < A_NOPE + A_ROPE)
        dksum = None
        for h in range(A_HEADS):
            sl = slice(h * HEAD_PAD, (h + 1) * HEAD_PAD)
            dql_ref[:, sl] = _unrope(dq_ref[:, sl].astype(F32), ct, sa, sb).astype(BF16)
            dkh = dk_ref[:, sl].astype(F32)
            dkl_ref[:, sl] = jnp.where(nope, dkh, 0.0).astype(BF16)
            dksum = dkh if dksum is None else dksum + dkh
        dkpe_ref[...] = jnp.where(rope, _unrope(jnp.where(rope, dksum, 0.0), ct, sa, sb), 0.0).astype(BF16)
        dcqn = jnp.dot(dql_ref[...], wuqt_ref[...], preferred_element_type=F32)
        dckvn = (jnp.dot(dkl_ref[...], wkt_ref[...], preferred_element_type=F32)
                 + jnp.dot(dv_ref[...].astype(BF16), wvt_ref[...], preferred_element_type=F32))

        def norm_bwd(xv, gv, dy):
            rstd = lax.rsqrt(jnp.mean(xv * xv, axis=-1, keepdims=True) + EPS)
            xn = xv * rstd
            dxn = dy * gv
            dx = rstd * (dxn - xn * jnp.mean(dxn * xn, axis=-1, keepdims=True))
            return dx, jnp.sum(dy * xn, axis=0, keepdims=True)

        dcq, dgq = norm_bwd(cq_ref[...], gq_ref[...], dcqn)
        dckv, dgkv = norm_bwd(ckv_ref[...], gkv_ref[...], dckvn)
        dcq_ref[...] = dcq.astype(BF16)
        dckv_ref[...] = dckv.astype(BF16)

        @pl.when(i == 0)
        def _():
            dgq_ref[...] = dgq
            dgkv_ref[...] = dgkv

        @pl.when(i != 0)
        def _():
            dgq_ref[...] += dgq
            dgkv_ref[...] += dgkv

    row = lambda wd: pl.BlockSpec((tm, wd), lambda i: (i, 0))
    full = lambda r, c: pl.BlockSpec((r, c), lambda i: (0, 0))
    return pl.pallas_call(
        body, name="mla_prep_bwd", grid=(t // tm,),
        in_specs=[row(qw), row(qw), row(GW), row(A_Q_RANK), row(A_KV_RANK), full(1, A_Q_RANK), full(1, A_KV_RANK),
                  full(qw, A_Q_RANK), full(qw, A_KV_RANK), full(GW, A_KV_RANK), row(128), row(128), row(128)],
        out_specs=[row(A_Q_RANK), row(A_KV_RANK), row(128), row(qw), row(qw), full(1, A_Q_RANK), full(1, A_KV_RANK)],
        out_shape=[jax.ShapeDtypeStruct((t, A_Q_RANK), BF16), jax.ShapeDtypeStruct((t, A_KV_RANK), BF16),
                   jax.ShapeDtypeStruct((t, 128), BF16), jax.ShapeDtypeStruct((t, qw), BF16),
                   jax.ShapeDtypeStruct((t, qw), BF16), jax.ShapeDtypeStruct((1, A_Q_RANK), F32),
                   jax.ShapeDtypeStruct((1, A_KV_RANK), F32)],
        compiler_params=_cparams(("arbitrary",)),
    )(dq, dk, dv, cq, ckv, gq, gkv, wuq_t, wk_t, wv_t, cos_t, sin_a, sin_b)


def _nt(a, b):
    return lax.dot_general(a, b, (((1,), (1,)), ((), ())), preferred_element_type=F32)


def _tn(a, b):
    return lax.dot_general(a, b, (((0,), (0,)), ((), ())), preferred_element_type=F32)


def _causal_mask(kind, q0, k0, tq, tk):
    qpos = q0 + lax.broadcasted_iota(jnp.int32, (tq, tk), 0)
    kpos = k0 + lax.broadcasted_iota(jnp.int32, (tq, tk), 1)
    if kind == "mla":
        return lax.shift_right_logical(kpos, 6) <= lax.shift_right_logical(qpos, 6)
    return kpos <= qpos


def _attn_fwd(kind, q, k, v, f, seq, scale, tq=512, tk=512):
    t = v.shape[0]
    nb = t // seq
    nq = seq // tq
    hw = 256 if kind == "mla" else 128
    n_heads = A_HEADS if kind == "mla" else C_HEADS
    use_f = f is not None
    tq, tk = min(tq, seq), min(tk, seq)
    nq = seq // tq
    assert tk == tq

    def body(*refs):
        if use_f:
            q_ref, k_ref, v_ref, f_ref, o_ref, st_ref = refs
        else:
            q_ref, k_ref, v_ref, o_ref, st_ref = refs
        qi = pl.program_id(2)
        q0 = qi * tq
        lane = lax.broadcasted_iota(jnp.int32, (1, 128), 1)
        half = lane >= 64
        qall = q_ref[...]
        if kind == "mla":
            qhs = [qall[:, 0:128], qall[:, 128:256]]
            post = scale * math.log2(math.e)
        else:
            assert math.frexp(scale)[0] == 0.5
            qall = qall * jnp.asarray(scale, BF16)
            qhs = [jnp.where(half, jnp.zeros_like(qall), qall), jnp.where(half, qall, jnp.zeros_like(qall))]
            post = None
        kd = pl.multiple_of(q0, tq)
        diag = _causal_mask(kind, 0, 0, tq, tk)

        def block(j, k0, state, masked):
            m, l, acc = state
            kh = k_ref[pl.ds(k0, tk), j * 128:(j + 1) * 128] if kind == "mla" else k_ref[pl.ds(k0, tk), :]
            s = _nt(qhs[j], kh)
            if post is not None:
                s = s * post
            if use_f:
                s = s - f_ref[0, 0, j:j + 1, pl.ds(k0, tk)]
            if masked:
                s = jnp.where(diag, s, NEG)
            mn = jnp.maximum(m, jnp.max(s, axis=-1, keepdims=True))
            alpha = jnp.exp2(m - mn) if post is not None else jnp.exp(m - mn)
            p = jnp.exp2(s - mn) if post is not None else jnp.exp(s - mn)
            l = alpha * l + jnp.sum(p, axis=-1, keepdims=True)
            acc = alpha * acc + jnp.dot(p.astype(BF16), v_ref[pl.ds(k0, tk), :], preferred_element_type=F32)
            return mn, l, acc

        def run(heads):
            def kstep(kb, carry):
                k0 = pl.multiple_of(kb * tk, tk)
                out = ()
                for n, j in enumerate(heads):
                    out += block(j, k0, carry[3 * n:3 * n + 3], False)
                return out

            init = (jnp.full((tq, 1), NEG, F32), jnp.zeros((tq, 1), F32), jnp.zeros((tq, 128), F32)) * len(heads)
            carry = lax.fori_loop(0, qi, kstep, init)
            o, st = jnp.zeros((tq, 128), F32), jnp.zeros((tq, 128), F32)
            for n, j in enumerate(heads):
                m, l, acc = block(j, kd, carry[3 * n:3 * n + 3], True)
                o = jnp.where(half == bool(j), acc / l, o)
                if post is not None:
                    m = m * math.log(2.0)
                st = jnp.where(lane == j, m + jnp.log(l), st)
            o_ref[...] = o.astype(BF16)
            st_ref[...] = st

        if n_heads % 2 == 0:
            run((0, 1))
        else:
            last = pl.program_id(1) == n_heads // 2
            pl.when(jnp.logical_not(last))(lambda: run((0, 1)))
            pl.when(last)(lambda: run((0,)))

    in_specs = [pl.BlockSpec((tq, hw), lambda b, p, i: (b * nq + i, p)),
                pl.BlockSpec((seq, hw), lambda b, p, i: (b, p)),
                pl.BlockSpec((seq, 128), lambda b, p, i: (b, p))]
    args = [q, k, v]
    if use_f:
        in_specs.append(pl.BlockSpec((1, 1, 8, seq), lambda b, p, i: (b, p, 0, 0)))
        args.append(f)
    oblk = pl.BlockSpec((tq, 128), lambda b, p, i: (b * nq + i, p))
    return pl.pallas_call(
        body, name="attn_fwd_" + kind, grid=(nb, 3, nq), in_specs=in_specs, out_specs=[oblk, oblk],
        out_shape=[jax.ShapeDtypeStruct((t, GW), BF16), jax.ShapeDtypeStruct((t, GW), F32)],
        compiler_params=_cparams(("arbitrary", "arbitrary", "arbitrary")),
    )(*args)


def _attn_bwd(kind, q, k, v, f, o, st, do, seq, scale, tq=512, tk=512, dep=None):
    t = v.shape[0]
    nb = t // seq
    tq, tk = min(tq, seq), min(tk, seq)
    nq = seq // tq
    nk = seq // tk
    hw = 256 if kind == "mla" else 128
    n_heads = A_HEADS if kind == "mla" else C_HEADS
    use_f = f is not None
    assert tq == tk

    def body(*refs):
        if use_f:
            (q_ref, k_ref, v_ref, f_ref, o_ref, st_ref, do_ref, dq_out, dk_out, dv_out, df_ref, dfq_ref,
             dq_ref, dk_ref, dv_ref) = refs
        else:
            q_ref, k_ref, v_ref, o_ref, st_ref, do_ref, dq_out, dk_out, dv_out, dq_ref, dk_ref, dv_ref = refs
        kj = pl.program_id(2)
        lane = lax.broadcasted_iota(jnp.int32, (1, 128), 1)
        half = lane >= 64

        @pl.when(kj == 0)
        def _():
            dq_ref[...] = jnp.zeros_like(dq_ref)
            if use_f:
                dfq_ref[...] = jnp.zeros_like(dfq_ref)

        dk_ref[...] = jnp.zeros_like(dk_ref)
        dv_ref[...] = jnp.zeros_like(dv_ref)
        if use_f:
            df_ref[...] = jnp.zeros_like(df_ref)
        vv = v_ref[...]
        diag = _causal_mask(kind, 0, 0, tq, tk)

        def qstep(qi, masked):
            q0 = pl.multiple_of(qi * tq, tq)
            rows = pl.ds(q0, tq)
            dov = do_ref[rows, :]
            dd = dov.astype(F32) * o_ref[rows, :]
            stv = st_ref[rows, :]

            def one_head(j):
                hm = half == bool(j)
                delta = jnp.sum(jnp.where(hm, dd, 0.0), axis=-1, keepdims=True)
                lse = stv[:, j:j + 1]
                if kind == "mla":
                    cols = slice(j * 128, (j + 1) * 128)
                    qh = q_ref[rows, cols]
                    kh = k_ref[:, cols]
                else:
                    cols = slice(0, 128)
                    qa = q_ref[rows, :]
                    qh = jnp.where(hm, qa, jnp.zeros_like(qa))
                    kh = k_ref[...]
                s = _nt(qh, kh) * scale
                if use_f:
                    s = s - f_ref[0, 0, j:j + 1, :]
                if masked:
                    s = jnp.where(diag, s, NEG)
                p = jnp.exp(s - lse)
                doh = jnp.where(hm, dov, jnp.zeros_like(dov))
                ds = p * (_nt(doh, vv) - delta)
                dsb = (ds * scale).astype(BF16)
                dv_ref[...] += _tn(p.astype(BF16), doh)
                dk_ref[:, cols] += _tn(dsb, qh)
                dqc = jnp.dot(dsb, kh, preferred_element_type=F32)
                if kind != "mla":
                    dqc = jnp.where(hm, dqc, 0.0)
                dq_ref[rows, cols] += dqc
                if use_f:
                    df_ref[0, 0, j:j + 1, :] += -jnp.sum(ds, axis=0, keepdims=True)
                    dfq_ref[rows, :] += jnp.where(lane == j, jnp.sum(ds, axis=-1, keepdims=True), 0.0)

            def both():
                one_head(0)
                one_head(1)

            if n_heads % 2 == 0:
                both()
            else:
                last = pl.program_id(1) == n_heads // 2
                pl.when(jnp.logical_not(last))(both)
                pl.when(last)(lambda: one_head(0))

        qstep(kj, True)

        def rest(qi, carry):
            qstep(qi, False)
            return carry

        lax.fori_loop(kj + 1, nq, rest, 0)
        dk_out[...] = dk_ref[...].astype(BF16)
        dv_out[...] = dv_ref[...].astype(BF16)

        @pl.when(kj == nk - 1)
        def _():
            dq_out[...] = dq_ref[...].astype(BF16)

    full_q = lambda wd: pl.BlockSpec((seq, wd), lambda b, p, i: (b, p))
    kblk = lambda wd: pl.BlockSpec((tk, wd), lambda b, p, i: (b * nk + i, p))
    in_specs = [full_q(hw), kblk(hw), kblk(128)]
    args = [q, k, v]
    if use_f:
        in_specs.append(pl.BlockSpec((1, 1, 8, tk), lambda b, p, i: (b, p, 0, i)))
        args.append(f)
    in_specs += [full_q(128), full_q(128), full_q(128)]
    args += [o, st, do]
    out_specs = [full_q(hw), kblk(hw), kblk(128)]
    out_shape = [jax.ShapeDtypeStruct((t, 3 * hw), BF16), jax.ShapeDtypeStruct((t, 3 * hw), BF16),
                 jax.ShapeDtypeStruct((t, GW), BF16)]
    scratch = [pltpu.VMEM((seq, hw), F32), pltpu.VMEM((tk, hw), F32), pltpu.VMEM((tk, 128), F32)]
    if use_f:
        out_specs += [pl.BlockSpec((1, 1, 8, tk), lambda b, p, i: (b, p, 0, i)), full_q(128)]
        out_shape += [jax.ShapeDtypeStruct((nb, 3, 8, seq), F32), jax.ShapeDtypeStruct((t, GW), F32)]
    body, in_specs, args = _after(dep, body, in_specs, args)
    return pl.pallas_call(
        body, name="attn_bwd_" + kind, grid=(nb, 3, nk), in_specs=in_specs, out_specs=out_specs,
        out_shape=out_shape, scratch_shapes=scratch,
        compiler_params=_cparams(("arbitrary", "arbitrary", "arbitrary")),
    )(*args)


BQ = 256
BWIN = BQ + B_LEFT


def _band_geometry():
    r = lax.broadcasted_iota(jnp.int32, (BQ, BWIN), 0)
    j = lax.broadcasted_iota(jnp.int32, (BQ, BWIN), 1)
    rc = lax.shift_right_logical(r, 6)
    jc = lax.shift_right_logical(j, 6)
    allowed = (jc - 8 <= rc) & (rc <= jc)
    return (r + B_LEFT - j) >= REL_CLIP, allowed, j < r


def _band_onehot(transposed, offset=0):
    shape = (BWIN, GW) if transposed else (GW, BWIN)
    kk = lax.broadcasted_iota(jnp.int32, shape, 1 if transposed else 0)
    x = lax.broadcasted_iota(jnp.int32, shape, 0 if transposed else 1) - offset
    x = jnp.where(x < 0, x + BWIN, x)
    return (kk == jnp.clip(B_LEFT - x, -REL_CLIP, REL_CLIP) + REL_CLIP).astype(F32)


def _band_table(rel_bias8):
    def body(b_ref, o_ref):
        hh = pl.program_id(0)
        u8 = jnp.dot(b_ref[...], _band_onehot(False), precision=HI, preferred_element_type=F32)
        rid = lax.broadcasted_iota(jnp.int32, (8, BWIN), 0)
        row = jnp.sum(jnp.where(rid == hh, u8, 0.0), axis=0, keepdims=True)
        far, allowed, _ = _band_geometry()
        tbl = pltpu.roll(jnp.broadcast_to(row, (BQ, BWIN)), 0, 1, stride=1, stride_axis=0)
        tbl = jnp.where(far, row[:, 0:1], tbl)
        o_ref[0] = jnp.where(allowed, tbl, NEG)

    return pl.pallas_call(
        body, name="band_table", grid=(6,),
        in_specs=[pl.BlockSpec((8, GW), lambda h: (0, 0))],
        out_specs=pl.BlockSpec((1, BQ, BWIN), lambda h: (h, 0, 0)),
        out_shape=jax.ShapeDtypeStruct((6, BQ, BWIN), F32),
        compiler_params=_cparams(("arbitrary",)),
    )(rel_bias8)


def _band_table_bwd(gtab):
    def body(g_ref, o_ref):
        gv = g_ref[0]
        _, _, wrapped = _band_geometry()
        gfar = jnp.sum(jnp.sum(jnp.where(wrapped, gv, 0.0), axis=-1, keepdims=True), axis=0, keepdims=True)
        anti = (lax.broadcasted_iota(jnp.int32, (BQ, BQ), 0) + lax.broadcasted_iota(jnp.int32, (BQ, BQ), 1)
                == BQ - 1).astype(F32)
        grev = jnp.dot(anti, jnp.where(wrapped, 0.0, gv), precision=HI, preferred_element_type=F32)
        near = pltpu.roll(grev, 0, 1, stride=1, stride_axis=0)
        y = jnp.broadcast_to(jnp.sum(near, axis=0, keepdims=True), (8, BWIN))
        gb = jnp.dot(y, _band_onehot(True, BQ - 1), precision=HI, preferred_element_type=F32)
        lane = lax.broadcasted_iota(jnp.int32, (8, GW), 1)
        o_ref[0] = gb + jnp.where(lane == 2 * REL_CLIP, gfar, 0.0)

    return pl.pallas_call(
        body, name="band_table_bwd", grid=(B_HEADS,),
        in_specs=[pl.BlockSpec((1, BQ, BWIN), lambda h: (h, 0, 0))],
        out_specs=pl.BlockSpec((1, 8, GW), lambda h: (h, 0, 0)),
        out_shape=jax.ShapeDtypeStruct((B_HEADS, 8, GW), F32),
        compiler_params=_cparams(("arbitrary",)),
    )(gtab)


def _band_fwd(q, k, v, table, seq, scale):
    t = q.shape[0]
    nb = t // seq
    nq = seq // BQ

    def body(q_ref, k_ref, v_ref, tb_ref, o_ref, st_ref, kpad, vpad):
        qi = pl.program_id(2)
        q0 = pl.multiple_of(qi * BQ, BQ)
        lane = lax.broadcasted_iota(jnp.int32, (1, 128), 1)
        half = lane >= 64

        @pl.when(qi == 0)
        def _():
            kpad[0:B_LEFT, :] = jnp.zeros((B_LEFT, 128), BF16)
            vpad[0:B_LEFT, :] = jnp.zeros((B_LEFT, 128), BF16)
            kpad[B_LEFT:, :] = k_ref[...]
            vpad[B_LEFT:, :] = v_ref[...]

        kw = kpad[pl.ds(q0, BWIN), :]
        vw = vpad[pl.ds(q0, BWIN), :]
        inside = lax.broadcasted_iota(jnp.int32, (BQ, BWIN), 1) >= B_LEFT - q0
        assert math.frexp(scale)[0] == 0.5
        qall = q_ref[...] * jnp.asarray(scale, BF16)

        def run(heads):
            o, st = jnp.zeros((BQ, 128), F32), jnp.zeros((BQ, 128), F32)
            for j in heads:
                qh = jnp.where(half == bool(j), qall, jnp.zeros_like(qall))
                s = jnp.where(inside, _nt(qh, kw) + tb_ref[j], NEG)
                m = jnp.max(s, axis=-1, keepdims=True)
                p = jnp.exp(s - m)
                l = jnp.sum(p, axis=-1, keepdims=True)
                o = jnp.where(half == bool(j), jnp.dot(p.astype(BF16), vw, preferred_element_type=F32) / l, o)
                st = jnp.where(lane == j, m + jnp.log(l), st)
            o_ref[...] = o.astype(BF16)
            st_ref[...] = st

        last = pl.program_id(1) == B_HEADS // 2
        pl.when(jnp.logical_not(last))(lambda: run((0, 1)))
        pl.when(last)(lambda: run((0,)))

    qblk = pl.BlockSpec((BQ, 128), lambda b, p, i: (b * nq + i, p))
    full = pl.BlockSpec((seq, 128), lambda b, p, i: (b, p))
    return pl.pallas_call(
        body, name="band_fwd", grid=(nb, 3, nq),
        in_specs=[qblk, full, full, pl.BlockSpec((2, BQ, BWIN), lambda b, p, i: (p, 0, 0))],
        out_specs=[qblk, qblk],
        out_shape=[jax.ShapeDtypeStruct((t, GW), BF16), jax.ShapeDtypeStruct((t, GW), F32)],
        scratch_shapes=[pltpu.VMEM((seq + B_LEFT, 128), BF16), pltpu.VMEM((seq + B_LEFT, 128), BF16)],
        compiler_params=_cparams(("arbitrary", "arbitrary", "arbitrary")),
    )(q, k, v, table)


def _band_bwd(q, k, v, table, o, st, do, seq, scale, dep=None):
    t = q.shape[0]
    nb = t // seq
    nq = seq // BQ

    def body(q_ref, k_ref, v_ref, tb_ref, o_ref, st_ref, do_ref, dq_ref, dk_ref, dv_ref, g_ref,
             kpad, vpad, dkpad, dvpad):
        b = pl.program_id(1)
        qi = pl.program_id(2)
        q0 = pl.multiple_of(qi * BQ, BQ)
        lane = lax.broadcasted_iota(jnp.int32, (1, 128), 1)
        half = lane >= 64

        @pl.when(qi == 0)
        def _():
            kpad[0:B_LEFT, :] = jnp.zeros((B_LEFT, 128), BF16)
            vpad[0:B_LEFT, :] = jnp.zeros((B_LEFT, 128), BF16)
            kpad[B_LEFT:, :] = k_ref[...]
            vpad[B_LEFT:, :] = v_ref[...]
            dkpad[...] = jnp.zeros_like(dkpad)
            dvpad[...] = jnp.zeros_like(dvpad)

        @pl.when((qi == 0) & (b == 0))
        def _():
            g_ref[...] = jnp.zeros_like(g_ref)

        win = pl.ds(q0, BWIN)
        kw = kpad[win, :]
        vw = vpad[win, :]
        inside = lax.broadcasted_iota(jnp.int32, (BQ, BWIN), 1) >= B_LEFT - q0
        qall = q_ref[...]
        dov = do_ref[...]
        dd = dov.astype(F32) * o_ref[...]
        stv = st_ref[...]

        def run(heads):
            dq = jnp.zeros((BQ, 128), F32)
            for j in heads:
                hm = half == bool(j)
                qh = jnp.where(hm, qall, jnp.zeros_like(qall))
                delta = jnp.sum(jnp.where(hm, dd, 0.0), axis=-1, keepdims=True)
                s = jnp.where(inside, _nt(qh, kw) * scale + tb_ref[j], NEG)
                p = jnp.exp(s - stv[:, j:j + 1])
                doh = jnp.where(hm, dov, jnp.zeros_like(dov))
                ds = p * (_nt(doh, vw) - delta)
                g_ref[j] += ds
                dsb = (ds * scale).astype(BF16)
                dvpad[win, :] += _tn(p.astype(BF16), doh)
                dkpad[win, :] += _tn(dsb, qh)
                dq = dq + jnp.where(hm, jnp.dot(dsb, kw, preferred_element_type=F32), 0.0)
            dq_ref[...] = dq.astype(BF16)

        last = pl.program_id(0) == B_HEADS // 2
        pl.when(jnp.logical_not(last))(lambda: run((0, 1)))
        pl.when(last)(lambda: run((0,)))

        @pl.when(qi == nq - 1)
        def _():
            dk_ref[...] = dkpad[B_LEFT:, :].astype(BF16)
            dv_ref[...] = dvpad[B_LEFT:, :].astype(BF16)

    qblk = pl.BlockSpec((BQ, 128), lambda p, b, i: (b * nq + i, p))
    full = pl.BlockSpec((seq, 128), lambda p, b, i: (b, p))
    tblk = pl.BlockSpec((2, BQ, BWIN), lambda p, b, i: (p, 0, 0))
    body, in_specs, args = _after(dep, body, [qblk, full, full, tblk, qblk, qblk, qblk], [q, k, v, table, o, st, do])
    return pl.pallas_call(
        body, name="band_bwd", grid=(3, nb, nq),
        in_specs=in_specs,
        out_specs=[qblk, full, full, tblk],
        out_shape=[jax.ShapeDtypeStruct((t, GW), BF16), jax.ShapeDtypeStruct((t, GW), BF16),
                   jax.ShapeDtypeStruct((t, GW), BF16), jax.ShapeDtypeStruct((6, BQ, BWIN), F32)],
        scratch_shapes=[pltpu.VMEM((seq + B_LEFT, 128), BF16), pltpu.VMEM((seq + B_LEFT, 128), BF16),
                        pltpu.VMEM((seq + B_LEFT, 128), F32), pltpu.VMEM((seq + B_LEFT, 128), F32)],
        compiler_params=_cparams(("arbitrary", "arbitrary", "arbitrary")),
    )(*args)


def _fox_prep(cf, fb, seq):
    nb = cf.shape[0] // seq
    nblk = seq // 128

    def body(cf_ref, fb_ref, f_ref):
        x = cf_ref[...] + fb_ref[...]
        lf = jnp.minimum(x, 0.0) - jnp.log1p(jnp.exp(-jnp.abs(x)))
        rows = lf.T[0:8, :]
        upper = (lax.broadcasted_iota(jnp.int32, (128, 128), 0)
                 <= lax.broadcasted_iota(jnp.int32, (128, 128), 1)).astype(F32)
        carry = jnp.zeros((8, 1), F32)
        for blk in range(nblk):
            sl = slice(blk * 128, (blk + 1) * 128)
            cs = jnp.dot(rows[:, sl], upper, precision=HI, preferred_element_type=F32) + carry
            carry = cs[:, 127:128]
            f_ref[0, 0, :, sl] = cs
            f_ref[0, 1, :, sl] = pltpu.roll(cs, 6, 0)
            f_ref[0, 2, :, sl] = pltpu.roll(cs, 4, 0)

    return pl.pallas_call(
        body, name="fox_prep", grid=(nb,),
        in_specs=[pl.BlockSpec((seq, 128), lambda b: (b, 0)), pl.BlockSpec((1, 128), lambda b: (0, 0))],
        out_specs=pl.BlockSpec((1, 3, 8, seq), lambda b: (b, 0, 0, 0)),
        out_shape=jax.ShapeDtypeStruct((nb, 3, 8, seq), F32),
        compiler_params=_cparams(("arbitrary",)),
    )(cf, fb)


def _fox_prep_bwd(df, dfq, cf, fb, seq):
    nb = cf.shape[0] // seq
    nblk = seq // 128

    def body(df_ref, dfq_ref, cf_ref, fb_ref, dcf_ref, dfb_ref, wide):
        b = pl.program_id(0)
        row = lax.broadcasted_iota(jnp.int32, (8, seq), 0)
        dfh = None
        for p in range(3):
            both = df_ref[0, p] + dfq_ref[:, p * 128:(p + 1) * 128].T[0:8, :]
            both = jnp.where(row < 2, both, 0.0)
            if p:
                both = pltpu.roll(both, 2 * p, 0)
            dfh = both if dfh is None else dfh + both
        lower = (lax.broadcasted_iota(jnp.int32, (128, 128), 0)
                 >= lax.broadcasted_iota(jnp.int32, (128, 128), 1)).astype(F32)
        wide[...] = jnp.zeros_like(wide)
        carry = jnp.zeros((8, 1), F32)
        for blk in reversed(range(nblk)):
            sl = slice(blk * 128, (blk + 1) * 128)
            rc = jnp.dot(dfh[:, sl], lower, precision=HI, preferred_element_type=F32) + carry
            carry = rc[:, 0:1]
            wide[0:8, sl] = rc
        dl = wide[...].T
        x = cf_ref[...] + fb_ref[...]
        dcf = dl * (1.0 / (1.0 + jnp.exp(x)))
        dcf_ref[...] = dcf.astype(BF16)
        part = jnp.sum(dcf, axis=0, keepdims=True)

        @pl.when(b == 0)
        def _():
            dfb_ref[...] = part

        @pl.when(b != 0)
        def _():
            dfb_ref[...] += part

    return pl.pallas_call(
        body, name="fox_prep_bwd", grid=(nb,),
        in_specs=[pl.BlockSpec((1, 3, 8, seq), lambda b: (b, 0, 0, 0)), pl.BlockSpec((seq, GW), lambda b: (b, 0)),
                  pl.BlockSpec((seq, 128), lambda b: (b, 0)), pl.BlockSpec((1, 128), lambda b: (0, 0))],
        out_specs=[pl.BlockSpec((seq, 128), lambda b: (b, 0)), pl.BlockSpec((1, 128), lambda b: (0, 0))],
        out_shape=[jax.ShapeDtypeStruct(cf.shape, BF16), jax.ShapeDtypeStruct((1, 128), F32)],
        scratch_shapes=[pltpu.VMEM((128, seq), F32)],
        compiler_params=_cparams(("arbitrary",)),
    )(df, dfq, cf, fb)


def _gate_out(oa, ob, oc, gates, w, x, gate, seq, tm=WIDE_ROW_TILE):
    t = x.shape[0]
    tm = min(tm, seq)
    tps = seq // tm

    def body(oa_ref, ob_ref, oc_ref, g_ref, w_ref, x_ref, gt_ref, xo_ref, y_ref, u_ref):
        for n, o_ref in enumerate((oa_ref, ob_ref, oc_ref)):
            sl = slice(n * GW, (n + 1) * GW)
            gv = g_ref[:, sl].astype(F32)
            u_ref[:, sl] = (o_ref[...] * (gv * _sigmoid(gv))).astype(BF16)
        y = jnp.dot(u_ref[...], w_ref[...], preferred_element_type=F32)
        y_ref[...] = y.astype(BF16)
        xo_ref[...] = x_ref[...] + gt_ref[0] * y

    row = lambda wd: pl.BlockSpec((tm, wd), lambda i: (i, 0))
    return pl.pallas_call(
        body, name="gate_out", grid=(t // tm,),
        in_specs=[row(GW), row(GW), row(GW), row(U_PAD), pl.BlockSpec((U_PAD, D_MODEL), lambda i: (0, 0)),
                  row(D_MODEL), pl.BlockSpec((1, 1, D_MODEL), lambda i: (i // tps, 0, 0))],
        out_specs=[row(D_MODEL), row(D_MODEL), row(U_PAD)],
        out_shape=[jax.ShapeDtypeStruct((t, D_MODEL), F32), jax.ShapeDtypeStruct((t, D_MODEL), BF16),
                   jax.ShapeDtypeStruct((t, U_PAD), BF16)],
        compiler_params=_cparams(("arbitrary",)),
    )(oa, ob, oc, gates, w, x, gate)


def _gate_out_bwd(dxo, y, gate, oa, ob, oc, gates, w_t, seq, tm=WIDE_ROW_TILE, dep=None):
    t = dxo.shape[0]
    tm = min(tm, seq)
    tps = seq // tm
    nb = t // seq

    def body(dxo_ref, y_ref, gt_ref, oa_ref, ob_ref, oc_ref, g_ref, wt_ref,
             dy_ref, doa_ref, dob_ref, doc_ref, dg_ref, dgt_ref):
        i = pl.program_id(0)
        dxo_v = dxo_ref[...]
        dgt = jnp.sum(dxo_v * y_ref[...].astype(F32), axis=0, keepdims=True)
        dyb = (dxo_v * gt_ref[0]).astype(BF16)
        dy_ref[...] = dyb
        du = _nt(dyb, wt_ref[...])
        for n, (o_ref, do_ref) in enumerate(((oa_ref, doa_ref), (ob_ref, dob_ref), (oc_ref, doc_ref))):
            sl = slice(n * GW, (n + 1) * GW)
            gv = g_ref[:, sl].astype(F32)
            sg = _sigmoid(gv)
            dun = du[:, sl]
            do_ref[...] = (dun * (gv * sg)).astype(BF16)
            dg_ref[:, sl] = (dun * o_ref[...] * (sg * (1.0 + gv * (1.0 - sg)))).astype(BF16)

        @pl.when(i % tps == 0)
        def _():
            dgt_ref[0] = dgt

        @pl.when(i % tps != 0)
        def _():
            dgt_ref[0] += dgt

    row = lambda wd: pl.BlockSpec((tm, wd), lambda i: (i, 0))
    per_b = pl.BlockSpec((1, 1, D_MODEL), lambda i: (i // tps, 0, 0))
    in_specs = [row(D_MODEL), row(D_MODEL), per_b, row(GW), row(GW), row(GW), row(U_PAD),
                pl.BlockSpec((U_PAD, D_MODEL), lambda i: (0, 0))]
    body, in_specs, args = _after(dep, body, in_specs, [dxo, y, gate, oa, ob, oc, gates, w_t])
    return pl.pallas_call(
        body, name="gate_out_bwd", grid=(t // tm,), in_specs=in_specs,
        out_specs=[row(D_MODEL), row(GW), row(GW), row(GW), row(U_PAD), per_b],
        out_shape=[jax.ShapeDtypeStruct((t, D_MODEL), BF16), jax.ShapeDtypeStruct((t, GW), BF16),
                   jax.ShapeDtypeStruct((t, GW), BF16), jax.ShapeDtypeStruct((t, GW), BF16),
                   jax.ShapeDtypeStruct((t, U_PAD), BF16), jax.ShapeDtypeStruct((nb, 1, D_MODEL), F32)],
        compiler_params=_cparams(("arbitrary",)),
    )(*args)


def _final_loss(x, target, g, tm=WIDE_ROW_TILE):
    t = x.shape[0]
    tm = min(tm, t)

    def body(x_ref, t_ref, g_ref, dx_ref, loss_ref, dg_ref):
        i = pl.program_id(0)
        xv = x_ref[...]
        rstd = lax.rsqrt(jnp.mean(xv * xv, axis=-1, keepdims=True) + EPS)
        xn = xv * rstd
        gv = g_ref[...]
        err = xn * gv - t_ref[...]
        dy = err * (1.0 / D_MODEL)
        dxn = dy * gv
        dx_ref[...] = rstd * (dxn - xn * jnp.mean(dxn * xn, axis=-1, keepdims=True))
        lp = jnp.sum(err * err, axis=0, keepdims=True) * (0.5 / D_MODEL)
        dgp = jnp.sum(dy * xn, axis=0, keepdims=True)

        @pl.when(i == 0)
        def _():
            loss_ref[...] = lp
            dg_ref[...] = dgp

        @pl.when(i != 0)
        def _():
            loss_ref[...] += lp
            dg_ref[...] += dgp

    row = pl.BlockSpec((tm, D_MODEL), lambda i: (i, 0))
    vec = pl.BlockSpec((1, D_MODEL), lambda i: (0, 0))
    return pl.pallas_call(
        body, name="final_loss", grid=(t // tm,),
        in_specs=[row, row, vec], out_specs=[row, vec, vec],
        out_shape=[jax.ShapeDtypeStruct((t, D_MODEL), F32), jax.ShapeDtypeStruct((1, D_MODEL), F32),
                   jax.ShapeDtypeStruct((1, D_MODEL), F32)],
        compiler_params=_cparams(("arbitrary",)),
    )(x, target, g)


def _adamw(w, gslots, m, v, name, tr=None):
    nl, r, c = w.shape
    ns = gslots.shape[0]
    tr = r if tr is None else tr

    def body(w_ref, g_ref, m_ref, v_ref, go_ref, d_ref, mo_ref, vo_ref):
        g = g_ref[0].astype(F32)
        for j in range(1, ns):
            g = g + g_ref[j].astype(F32)
        mn = ADAM_B1 * m_ref[...] + (1.0 - ADAM_B1) * g
        vn = ADAM_B2 * v_ref[...] + (1.0 - ADAM_B2) * jnp.square(g)
        m_hat = mn / (1.0 - ADAM_B1 ** ADAM_STEP)
        v_hat = vn / (1.0 - ADAM_B2 ** ADAM_STEP)
        go_ref[...] = g
        d_ref[...] = -ADAM_LR * (m_hat / (jnp.sqrt(v_hat) + ADAM_EPS) + ADAM_WD * w_ref[...])
        mo_ref[...] = mn
        vo_ref[...] = vn

    blk = pl.BlockSpec((1, tr, c), lambda l, i: (l, i, 0))
    return pl.pallas_call(
        body, name=name, grid=(nl, r // tr),
        in_specs=[blk, pl.BlockSpec((ns, 1, tr, c), lambda l, i: (0, l, i, 0)), blk, blk],
        out_specs=[blk] * 4, out_shape=[jax.ShapeDtypeStruct((nl, r, c), F32)] * 4,
        compiler_params=_cparams(("arbitrary", "arbitrary")),
    )(w, gslots, m, v)


def _rope_tables(positions):
    inv = ROPE_THETA ** (-jnp.arange(0, A_ROPE, 2, dtype=F32) / A_ROPE)
    ang = positions.astype(F32)[:, None] * inv
    cos, sin = jnp.cos(ang), jnp.sin(ang)
    t = positions.shape[0]
    one = jnp.ones((t, 64), F32)
    zero16 = jnp.zeros((t, 16), F32)
    cos_t = jnp.concatenate([one, cos, cos, jnp.ones((t, 32), F32)], axis=1)
    sin_a = jnp.concatenate([jnp.zeros((t, 64), F32), -sin, zero16, jnp.zeros((t, 32), F32)], axis=1)
    sin_b = jnp.concatenate([jnp.zeros((t, 64), F32), zero16, sin, jnp.zeros((t, 32), F32)], axis=1)
    return cos_t, sin_a, sin_b


def _pad_heads(w, real, padded, nheads, axis):
    shp = w.shape[:axis] + (nheads, real) + w.shape[axis + 1:]
    w = w.reshape(shp)
    pad = [(0, 0)] * w.ndim
    pad[axis + 1] = (0, padded - real)
    w = jnp.pad(w, pad)
    return w.reshape(w.shape[:axis] + (nheads * padded,) + w.shape[axis + 2:])


def kernel(x, c, positions, w_ada, b_ada, norm_g, w_in, a_q_norm_g, a_w_uq, a_kv_norm_g, a_w_ukv, b_rel_bias, c_forget_b, w_out, final_g, loss_target, m_w_ada, m_b_ada, m_norm_g, m_w_in, m_a_q_norm_g, m_a_w_uq, m_a_kv_norm_g, m_a_w_ukv, m_b_rel_bias, m_c_forget_b, m_w_out, m_final_g, v_w_ada, v_b_ada, v_norm_g, v_w_in, v_a_q_norm_g, v_a_w_uq, v_a_kv_norm_g, v_a_w_ukv, v_b_rel_bias, v_c_forget_b, v_w_out, v_final_g):
    nb, seq, _ = x.shape
    t = nb * seq
    me = 4 * lax.axis_index("x") + 2 * lax.axis_index("y") + lax.axis_index("c")
    x2 = x.reshape(t, D_MODEL)
    tgt = loss_target.reshape(t, D_MODEL)
    cos_t, sin_a, sin_b = _rope_tables(positions.reshape(t))

    def shards(l):
        return [_pad_runs(w_in[l].astype(BF16), IN_RUNS, N_PAD, 1), w_out[l].astype(BF16),
                a_w_uq[l].astype(BF16), a_w_ukv[l].astype(BF16)]

    def prepare(gi, go, gq, gkv):
        return dict(w_in=gi.reshape(D_MODEL, N_PAD), **prepare_rest(go, gq, gkv))

    def prepare_rest(go, gq, gkv):
        wo = _pad_runs(go.reshape(D_MODEL, D_MODEL), OUT_RUNS, U_PAD, 0)
        wq = jnp.transpose(gq, (1, 0, 2)).reshape(A_Q_RANK, A_HEADS * (A_NOPE + A_ROPE))
        wq = _pad_heads(wq, A_NOPE + A_ROPE, HEAD_PAD, A_HEADS, 1)
        wkv = jnp.transpose(gkv, (1, 0, 2)).reshape(A_KV_RANK, A_HEADS, 2 * A_NOPE)
        wk = jnp.pad(wkv[:, :, :A_NOPE], ((0, 0), (0, 0), (0, HEAD_PAD - A_NOPE))).reshape(A_KV_RANK, A_HEADS * HEAD_PAD)
        wv = wkv[:, :, A_NOPE:].reshape(A_KV_RANK, GW)
        return dict(w_out=wo, wuq=wq, wuq_t=wq.T, wk=wk, wk_t=wk.T, wv=wv, wv_t=wv.T)

    shards0 = shards(0)
    w_in0_g, c_g = _gather([shards0[0], c], "gather_w_in0")
    c_all = c_g.reshape(N_DEV * nb, D_MODEL)
    weights = [dict(w_in=w_in0_g.reshape(D_MODEL, N_PAD)), None]

    c_act, mod_cols = _ada_fwd(c_all, w_ada)
    (mod_g,) = _gather([mod_cols], "gather_mod")
    rest0, rest0_token = _split_start("gather", shards0[1:], "gather_rest0_start", after=mod_g)
    mod_all = jnp.transpose(mod_g, (1, 2, 0, 3)).reshape(DEPTH, N_DEV * nb, 3 * D_MODEL)
    mod = lax.dynamic_slice_in_dim(mod_all, me * nb, nb, axis=1) + b_ada[:, None, :]

    fb_pad = jnp.pad(c_forget_b, ((0, 0), (0, 128 - C_HEADS)))
    a_scale = (A_NOPE + A_ROPE) ** -0.5
    h_scale = CHUNK ** -0.5

    saved = []
    xl = x2
    for l in range(DEPTH):
        if l == 1:
            weights[1] = prepare(*_split_wait(gather1, xl, "gather_weights1_wait")[1])
        w = weights[l]
        shift, scale, gate = mod[l, :, :D_MODEL], mod[l, :, D_MODEL:2 * D_MODEL], mod[l, :, 2 * D_MODEL:]
        ss = jnp.stack([shift, 1.0 + scale], axis=1)
        gate3 = gate[:, None, :]
        h, cq, ckv, kpe, gates, bq, bk, bv, cq2, ck, cv, cf = _ln_in(
            xl, ss, norm_g[l:l + 1], w["w_in"], seq, dep=rest0_token if l == 0 else None)
        gather1_token = None
        if l == 0:
            w.update(prepare_rest(*_split_wait(rest0, h, "gather_rest0_wait")[1]))
            gather1, gather1_token = _split_start("gather", shards(1), "gather_weights1_start", after=w["w_out"])
        q, k, v, cqn, ckvn = _mla_prep(cq, ckv, kpe, a_q_norm_g[l:l + 1], a_kv_norm_g[l:l + 1],
                                       w["wuq"], w["wk"], w["wv"], cos_t, sin_a, sin_b, dep=gather1_token)
        oa, sta = _attn_fwd("mla", q, k, v, None, seq, a_scale)
        table = _band_table(jnp.pad(b_rel_bias[l], ((0, 8 - B_HEADS), (0, GW - N_REL))))
        ob, stb = _band_fwd(bq, bk, bv, table, seq, h_scale)
        fcum = _fox_prep(cf, fb_pad[l:l + 1], seq)
        oc, stc = _attn_fwd("fox", cq2, ck, cv, fcum, seq, h_scale)
        xn, y, u = _gate_out(oa, ob, oc, gates, w["w_out"], xl, gate3, seq)
        saved.append(dict(x=xl, ss=ss, gate3=gate3, h=h, cq=cq, ckv=ckv, gates=gates, bq=bq, bk=bk, bv=bv,
                          cq2=cq2, ck=ck, cv=cv, cf=cf, q=q, k=k, v=v, cqn=cqn, ckvn=ckvn, oa=oa, sta=sta,
                          table=table, ob=ob, stb=stb, fcum=fcum, oc=oc, stc=stc, y=y, u=u))
        xl = xn

    dx, loss_lanes, g_final = _final_loss(xl, tgt, final_g[None, :])
    loss = lax.psum(jnp.sum(loss_lanes), AXES)

    rows = D_MODEL // N_DEV
    core = lax.axis_index("c").astype(jnp.int32).reshape(1)
    n_seg_a = 4
    dmods, smalls, parts = [None] * DEPTH, [None] * DEPTH, [None] * DEPTH
    pair1 = chips1 = pair1_token = chips1_token = None
    for l in reversed(range(DEPTH)):
        s, w = saved[l], weights[l]
        dy, doa, dob, doc, dgates, dgate = _gate_out_bwd(dx, s["y"], s["gate3"], s["oa"], s["ob"], s["oc"],
                                                         s["gates"], w["w_out"], seq, dep=pair1_token)
        g_out = _unpad_runs(_matmul_tn(s["u"], dy, "dw_out"), OUT_RUNS, 0)
        if l == 0:
            own, from_sib = _split_wait(pair1, g_out, "grads1_pair_wait")
            chips1, chips1_token = _split_start("chips", _pair_add(core, own, from_sib, "grads1_add"), "grads1_chips_start")
        dq, dk, dv = _attn_bwd("mla", s["q"], s["k"], s["v"], None, s["oa"], s["sta"], doa, seq, a_scale,
                               dep=chips1_token)
        dbq, dbk, dbv, gtab = _band_bwd(s["bq"], s["bk"], s["bv"], s["table"], s["ob"], s["stb"], dob, seq, h_scale,
                                        dep=chips1_token)
        g_rel = _band_table_bwd(gtab)[:, 0, :N_REL]
        dcq2, dck, dcv, dfc, dfq = _attn_bwd("fox", s["cq2"], s["ck"], s["cv"], s["fcum"], s["oc"], s["stc"], doc,
                                             seq, h_scale, dep=chips1_token)
        dcf, dfb = _fox_prep_bwd(dfc, dfq, s["cf"], fb_pad[l:l + 1], seq)
        dcq, dckv, dkpe, dqlin, dklin, dgq, dgkv = _mla_prep_bwd(
            dq, dk, dv, s["cq"], s["ckv"], a_q_norm_g[l:l + 1], a_kv_norm_g[l:l + 1],
            w["wuq_t"], w["wk_t"], w["wv_t"], cos_t, sin_a, sin_b)
        gq_pad = _matmul_tn(s["cqn"], dqlin, "dw_uq")
        g_uq = gq_pad.reshape(A_Q_RANK, A_HEADS, HEAD_PAD)[:, :, :A_NOPE + A_ROPE].reshape(A_Q_RANK, -1)
        gkv_pad = _matmul_tn(s["ckvn"], [dklin, dv], "dw_ukv")
        gk_pad = gkv_pad[:, :A_HEADS * HEAD_PAD].reshape(A_KV_RANK, A_HEADS, HEAD_PAD)[:, :, :A_NOPE]
        gv_pad = gkv_pad[:, A_HEADS * HEAD_PAD:].reshape(A_KV_RANK, A_HEADS, A_NOPE)
        g_ukv = jnp.concatenate([gk_pad, gv_pad], axis=2).reshape(A_KV_RANK, -1)
        dz = [dcq, dckv, dkpe, dgates, dbq, dbk, dbv, dcq2, dck, dcv, dcf]
        g_in_a = _matmul_tn(s["h"], dz[:n_seg_a], "dw_in_a")
        first = [g_in_a.reshape(N_DEV, rows, -1), g_out.reshape(N_DEV, rows, D_MODEL),
                 g_uq.reshape(A_Q_RANK, N_DEV, -1).transpose(1, 0, 2), g_ukv.reshape(A_KV_RANK, N_DEV, -1).transpose(1, 0, 2)]
        if l == 1:
            g_in_b = _matmul_tn(s["h"], dz[n_seg_a:], "dw_in_b")
            pair1, pair1_token = _split_start("pair", first + [g_in_b.reshape(N_DEV, rows, -1)], "grads1_pair_start")
            tail_token = None
        else:
            pair0a, pair0a_token = _split_start("pair", first, "grads0a_pair_start")
            g_in_b = _matmul_tn(s["h"], dz[n_seg_a:], "dw_in_b", dep=pair0a_token)
            own, from_sib = _split_wait(pair0a, g_in_b, "grads0a_pair_wait")
            sums0a = _pair_add(core, own, from_sib, "grads0a_add")
            pair0b, pair0b_token = _split_start("pair", [g_in_b.reshape(N_DEV, rows, -1)], "grads0b_pair_start",
                                                after=sums0a[0])
            chips0a, tail_token = _split_start("chips", sums0a, "grads0a_chips_start", after=pair0b_token)
        dx, dss, dg_norm = _ln_in_bwd(dz, w["w_in"], s["x"], s["ss"], norm_g[l:l + 1], dx, seq, dep=tail_token)
        dmods[l] = jnp.concatenate([dss[:, 0, :], dss[:, 1, :], dgate[:, 0, :]], axis=1)
        smalls[l] = [dg_norm.reshape(-1), dgq.reshape(-1), dgkv.reshape(-1), g_rel.reshape(-1),
                     dfb[0, :C_HEADS]]
    grad_x = dx.reshape(nb, seq, D_MODEL)
    parts[1] = _split_wait(chips1, dx, "grads1_chips_wait")[1]
    parts0a = _split_wait(chips0a, dx, "grads0a_chips_wait")[1]
    own, from_sib = _split_wait(pair0b, dx, "grads0b_pair_wait")

    small = jnp.concatenate([p for l in range(DEPTH) for p in smalls[l]] + [g_final.reshape(-1)])
    n_small = small.shape[0]
    small_rows = -(-n_small // 1024) * 8
    small = jnp.pad(small, (0, small_rows * 128 - n_small)).reshape(small_rows, 128)
    dmod_local = jnp.stack(dmods)
    dmod_g, small_g = _gather([dmod_local, small], "gather_small", dep=parts0a[0])
    chips0, chips0_token = _split_start("chips", _pair_add(core, own, from_sib, "grads0b_add"), "grads0b_chips_start",
                                        after=small_g)
    dmod_all = jnp.transpose(dmod_g, (1, 0, 2, 3)).reshape(DEPTH, N_DEV * nb, 3 * D_MODEL)
    cols = 3 * D_MODEL // N_DEV
    dmod_mine = lax.dynamic_slice_in_dim(dmod_all, me * cols, cols, axis=2)
    g_w_ada, g_b_ada = _ada_bwd(c_act, dmod_all, dmod_mine, chips0_token)
    small_sum = _sum_slots(small_g, "sum_small").reshape(-1)

    def split_small():
        out, pos = [], 0
        sizes = [D_MODEL, A_Q_RANK, A_KV_RANK, B_HEADS * N_REL, C_HEADS]
        per_layer = []
        for l in range(DEPTH):
            parts = []
            for sz in sizes:
                parts.append(small_sum[pos:pos + sz])
                pos += sz
            per_layer.append(parts)
        for j in range(len(sizes)):
            out.append(jnp.stack([per_layer[l][j] for l in range(DEPTH)]))
        out.append(small_sum[pos:pos + D_MODEL])
        return out

    g_norm, g_qn, g_kvn, g_relb, g_fb, g_fin = split_small()

    def adam(w, g, m, v, name, tr=None):
        shp = w.shape
        w3 = w.reshape((1,) * (3 - w.ndim) + shp)
        outs = _adamw(w3, g.reshape((-1,) + w3.shape), m.reshape(w3.shape), v.reshape(w3.shape), name, tr)
        return [o.reshape(shp) for o in outs]

    res = {
        "w_ada": adam(w_ada, g_w_ada, m_w_ada, v_w_ada, "adam_w_ada", 256),
        "b_ada": adam(b_ada, g_b_ada, m_b_ada, v_b_ada, "adam_b_ada"),
        "norm_g": adam(norm_g, g_norm, m_norm_g, v_norm_g, "adam_norm_g"),
        "a_q_norm_g": adam(a_q_norm_g, g_qn, m_a_q_norm_g, v_a_q_norm_g, "adam_q_norm"),
        "a_kv_norm_g": adam(a_kv_norm_g, g_kvn, m_a_kv_norm_g, v_a_kv_norm_g, "adam_kv_norm"),
        "b_rel_bias": adam(b_rel_bias, g_relb.reshape(b_rel_bias.shape), m_b_rel_bias, v_b_rel_bias, "adam_rel_bias"),
        "c_forget_b": adam(c_forget_b, g_fb, m_c_forget_b, v_c_forget_b, "adam_forget_b"),
        "final_g": adam(final_g, g_fin, m_final_g, v_final_g, "adam_final_g"),
    }
    parts[0] = list(parts0a) + list(_split_wait(chips0, res["w_ada"][1], "grads0b_chips_wait")[1])
    p_in = jnp.stack([_unpad_runs(jnp.concatenate([parts[l][0], parts[l][4]], axis=2), IN_RUNS, 2)
                      for l in range(DEPTH)], axis=1)
    p_out, p_uq, p_ukv = (jnp.stack([parts[l][j] for l in range(DEPTH)], axis=1) for j in (1, 2, 3))
    res.update({
        "w_in": adam(w_in, p_in, m_w_in, v_w_in, "adam_w_in", 64),
        "a_w_uq": adam(a_w_uq, p_uq, m_a_w_uq, v_a_w_uq, "adam_w_uq"),
        "a_w_ukv": adam(a_w_ukv, p_ukv, m_a_w_ukv, v_a_w_ukv, "adam_w_ukv"),
        "w_out": adam(w_out, p_out, m_w_out, v_w_out, "adam_w_out", 64),
    })
    names = ["w_ada", "b_ada", "norm_g", "w_in", "a_q_norm_g", "a_w_uq", "a_kv_norm_g", "a_w_ukv", "b_rel_bias",
             "c_forget_b", "w_out", "final_g"]
    outs = [loss, grad_x]
    for j in range(4):
        outs += [res[n][j] for n in names]
    return tuple(outs)
```

```python
import math

import jax
import jax.numpy as jnp
from jax import lax
from jax.experimental import pallas as pl
from jax.experimental.pallas import tpu as pltpu

F32 = jnp.float32
BF16 = jnp.bfloat16
HI = lax.Precision.HIGHEST

N_DEV = 8
AXES = ("x", "y", "c")
D_MODEL = 1024
DEPTH = 2
CHUNK = 64
EPS = 1e-6
NEG = -1e30
A_HEADS = 6
A_NOPE = 64
A_ROPE = 32
A_Q_RANK = 384
A_KV_RANK = 256
ROPE_THETA = 10000.0
B_HEADS = 5
B_LEFT = 512
REL_CLIP = 128
N_REL = 2 * REL_CLIP + 1
C_HEADS = 5
HEAD_PAD = 128
GW = 384
N_IN = 3621
ADAM_LR = 0.001
ADAM_B1 = 0.9
ADAM_B2 = 0.999
ADAM_EPS = 1e-08
ADAM_WD = 0.01
ADAM_STEP = 10
VMEM_LIMIT = 56 * 1024 * 1024
ROW_TILE = 512
WIDE_ROW_TILE = 1024

Z_SEGS = (
    ("cq", 0, 384, F32), ("ckv", 384, 256, F32), ("kpe", 640, 128, F32), ("gates", 768, 1152, BF16),
    ("bq", 1920, 384, BF16), ("bk", 2304, 384, BF16), ("bv", 2688, 384, BF16),
    ("cq2", 3072, 384, BF16), ("ck", 3456, 384, BF16), ("cv", 3840, 384, BF16), ("cf", 4224, 128, F32),
)
N_PAD = 4352
IN_RUNS = (
    (0, 384, 0), (384, 256, 384), (640 + 64, 32, 640),
    (768, 384, 672), (768 + 384, 320, 2016), (768 + 768, 320, 3301),
    (1920, 320, 1056), (2304, 320, 1376), (2688, 320, 1696),
    (3072, 320, 2336), (3456, 320, 2656), (3840, 320, 2976), (4224, 5, 3296),
)
OUT_RUNS = ((0, 384, 0), (384, 320, 384), (768, 320, 704))
U_PAD = 1152


def _cparams(sem=None, vmem=VMEM_LIMIT):
    return pltpu.CompilerParams(dimension_semantics=sem, vmem_limit_bytes=vmem)


def _after(dep, body, in_specs, args):
    if dep is None:
        return body, in_specs, args
    n = len(args)

    def ordered(*refs):
        return body(*refs[:n], *refs[n + 1:])

    return ordered, list(in_specs) + [pl.BlockSpec((8, 128), lambda *_: (0, 0))], list(args) + [dep]


def _pad_runs(w, runs, total, axis):
    order = sorted(runs)
    parts, pos = [], 0
    for off, wd, src in order:
        if off > pos:
            shp = list(w.shape)
            shp[axis] = off - pos
            parts.append(jnp.zeros(shp, w.dtype))
        parts.append(lax.slice_in_dim(w, src, src + wd, axis=axis))
        pos = off + wd
    if pos < total:
        shp = list(w.shape)
        shp[axis] = total - pos
        parts.append(jnp.zeros(shp, w.dtype))
    return jnp.concatenate(parts, axis=axis)


def _unpad_runs(w, runs, axis):
    order = sorted(runs, key=lambda r: r[2])
    return jnp.concatenate([lax.slice_in_dim(w, off, off + wd, axis=axis) for off, wd, _ in order], axis=axis)


def _sigmoid(x):
    return 1.0 / (1.0 + jnp.exp(-x))


N_CHIP = 4
ANY_SPEC = pl.BlockSpec(memory_space=pl.ANY)
MESH_ID = pl.DeviceIdType.MESH


def _gather(arrs, name, dep=None):
    n = len(arrs)
    nin = n + (dep is not None)

    def body(*refs):
        ins, outs = refs[:n], refs[nin:nin + n]
        send_sems, recv_sems, local_sems = refs[nin + n:]
        x, y, c = lax.axis_index("x"), lax.axis_index("y"), lax.axis_index("c")
        me, sib = (x, y, c), (x, y, 1 - c)
        chips = [(1 - x, y), (x, 1 - y), (1 - x, 1 - y)]

        def slot(px, py, pc):
            return 4 * px + 2 * py + pc

        def copy(a, k, block, to, src=None):
            dst = outs[a].at[slot(*block)]
            return pltpu.make_async_remote_copy(
                src_ref=dst if src is None else src, dst_ref=dst, send_sem=send_sems.at[a, k],
                recv_sem=recv_sems.at[a, k], device_id=to, device_id_type=MESH_ID)

        local = [pltpu.make_async_copy(ins[a], outs[a].at[slot(*me)], local_sems.at[a]) for a in range(n)]
        first = []
        for a in range(n):
            first.append(copy(a, 0, me, sib, src=ins[a]))
            first += [copy(a, 1 + j, me, (*chip, c), src=ins[a]) for j, chip in enumerate(chips)]
        for cp in local + first:
            cp.start()
        passed = []
        for j, chip in enumerate(chips):
            for a in range(n):
                copy(a, 1 + j, (*chip, c), me).wait_recv()
                fwd = copy(a, 4 + j, (*chip, c), sib)
                fwd.start()
                passed.append(fwd)
        for a in range(n):
            copy(a, 0, sib, me).wait_recv()
            for j, chip in enumerate(chips):
                copy(a, 4 + j, (*chip, 1 - c), me).wait_recv()
        for cp in first + passed:
            cp.wait_send()
        for cp in local:
            cp.wait()

    return pl.pallas_call(
        body, name=name, out_shape=[jax.ShapeDtypeStruct((N_DEV,) + a.shape, a.dtype) for a in arrs],
        in_specs=[ANY_SPEC] * nin, out_specs=[ANY_SPEC] * n,
        scratch_shapes=[pltpu.SemaphoreType.DMA((n, N_DEV - 1)), pltpu.SemaphoreType.DMA((n, N_DEV - 1)),
                        pltpu.SemaphoreType.DMA((n,))],
    )(*arrs, *([] if dep is None else [dep]))


HBM_SPEC = pl.BlockSpec(memory_space=pltpu.HBM)
SEM_SPEC = pl.BlockSpec(memory_space=pltpu.SEMAPHORE)
SPLIT_EFFECT = pltpu.SideEffectType.DATAFLOW_SIDE_EFFECTING
SPLIT_SEMS = {"gather": (N_DEV - 1, True), "pair": (N_CHIP, False), "chips": (N_CHIP - 1, True)}


def _split_descriptors(pattern, srcs, lands, sems):
    x, y, c = lax.axis_index("x"), lax.axis_index("y"), lax.axis_index("c")
    nsem, has_local = SPLIT_SEMS[pattern]
    per = 2 * nsem + int(has_local)
    starts, arrivals, local = [], [], []

    def remote(a, k, src, dst, to):
        return pltpu.make_async_remote_copy(src_ref=src, dst_ref=dst, send_sem=sems[a * per + k],
                                            recv_sem=sems[a * per + nsem + k], device_id=to, device_id_type=MESH_ID)

    for a in range(len(srcs)):
        if pattern == "gather":
            me = 4 * x + 2 * y + c
            local.append(pltpu.make_async_copy(srcs[a], lands[a].at[me], sems[a * per + 2 * nsem]))
            for k in range(1, N_DEV):
                px = (1 - x) if (k >> 2) & 1 else x
                py = (1 - y) if (k >> 1) & 1 else y
                pc = (1 - c) if k & 1 else c
                starts.append(remote(a, k - 1, srcs[a], lands[a].at[me], (px, py, pc)))
                arrivals.append(remote(a, k - 1, srcs[a], lands[a].at[4 * px + 2 * py + pc], (px, py, pc)))
        elif pattern == "pair":
            for q in range(N_CHIP):
                cp = remote(a, q, srcs[a].at[2 * q + 1 - c], lands[a].at[q], (x, y, 1 - c))
                starts.append(cp)
                arrivals.append(cp)
        else:
            mine = 2 * x + y
            local.append(pltpu.make_async_copy(srcs[a].at[mine], lands[a].at[mine], sems[a * per + 2 * nsem]))
            for k in range(1, N_CHIP):
                px = (1 - x) if (k >> 1) & 1 else x
                py = (1 - y) if k & 1 else y
                starts.append(remote(a, k - 1, srcs[a].at[2 * px + py], lands[a].at[mine], (px, py, c)))
                arrivals.append(remote(a, k - 1, srcs[a].at[2 * px + py], lands[a].at[2 * px + py], (px, py, c)))
    return starts, arrivals, local


def _split_start(pattern, arrs, name, after=None):
    n = len(arrs)
    extra = [] if after is None else [after]
    nsem, has_local = SPLIT_SEMS[pattern]
    if pattern == "gather":
        land_shapes = [(N_DEV,) + a.shape for a in arrs]
    elif pattern == "pair":
        land_shapes = [(N_CHIP,) + a.shape[1:] for a in arrs]
    else:
        land_shapes = [a.shape for a in arrs]
    nsem_out = n * (2 * nsem + int(has_local))

    def body(*refs):
        srcs, lands = refs[:n], refs[n:2 * n]
        first_sem = 2 * n + len(extra)
        sems = refs[first_sem:first_sem + nsem_out]
        token = refs[-1]
        starts, _, local = _split_descriptors(pattern, srcs, lands, sems)
        for cp in local + starts:
            cp.start()
        token[...] = jnp.zeros_like(token)

    out_shape = ([pltpu.SemaphoreType.DMA(())] * nsem_out + [pltpu.HBM(a.shape, a.dtype) for a in arrs]
                 + [pltpu.HBM(s, a.dtype) for s, a in zip(land_shapes, arrs)] + [jax.ShapeDtypeStruct((8, 128), F32)])
    ins = ([pltpu.with_memory_space_constraint(a, pltpu.HBM) for a in arrs]
           + [pltpu.with_memory_space_constraint(lax.empty(s, a.dtype), pltpu.HBM) for s, a in zip(land_shapes, arrs)])
    outs = pl.pallas_call(
        body, name=name, out_shape=out_shape, in_specs=[HBM_SPEC] * (2 * n) + [ANY_SPEC] * len(extra),
        out_specs=[SEM_SPEC] * nsem_out + [HBM_SPEC] * (2 * n) + [pl.BlockSpec(memory_space=pltpu.VMEM)],
        input_output_aliases={i: nsem_out + i for i in range(2 * n)},
        compiler_params=pltpu.CompilerParams(has_side_effects=SPLIT_EFFECT),
    )(*ins, *extra)
    handle = dict(pattern=pattern, n=n, sems=outs[:nsem_out], srcs=outs[nsem_out:nsem_out + n],
                  lands=outs[nsem_out + n:nsem_out + 2 * n])
    return handle, outs[-1]


def _split_wait(handle, after, name):
    pattern, n = handle["pattern"], handle["n"]
    nsem_in = len(handle["sems"])

    def body(*refs):
        srcs, lands = refs[:n], refs[n:2 * n]
        starts, arrivals, local = _split_descriptors(pattern, srcs, lands, refs[2 * n:2 * n + nsem_in])
        for cp in starts:
            cp.wait_send()
        for cp in arrivals:
            cp.wait_recv()
        for cp in local:
            cp.wait()

    srcs, lands = handle["srcs"], handle["lands"]
    outs = pl.pallas_call(
        body, name=name,
        out_shape=[pltpu.HBM(a.shape, a.dtype) for a in srcs] + [pltpu.HBM(a.shape, a.dtype) for a in lands],
        in_specs=[HBM_SPEC] * (2 * n) + [SEM_SPEC] * nsem_in + [ANY_SPEC], out_specs=[HBM_SPEC] * (2 * n),
        input_output_aliases={i: i for i in range(2 * n)},
        compiler_params=pltpu.CompilerParams(has_side_effects=SPLIT_EFFECT),
    )(*srcs, *lands, *handle["sems"], after)
    return outs[:n], outs[n:]


def _pair_add(core, a8s, b4s, name):
    n = len(a8s)

    def body(core_ref, *refs):
        for i in range(n):
            refs[2 * n + i][...] = (refs[i][...] + refs[n + i][...]).astype(BF16)

    own = [pl.BlockSpec((1,) + b.shape[1:], lambda q, core_ref: (2 * q + core_ref[0], 0, 0)) for b in b4s]
    slot = [pl.BlockSpec((1,) + b.shape[1:], lambda q, core_ref: (q, 0, 0)) for b in b4s]
    grid_spec = pltpu.PrefetchScalarGridSpec(num_scalar_prefetch=1, grid=(N_CHIP,), in_specs=own + slot, out_specs=slot)
    return pl.pallas_call(
        body, name=name, grid_spec=grid_spec, out_shape=[jax.ShapeDtypeStruct(b.shape, BF16) for b in b4s],
        compiler_params=_cparams(("arbitrary",)),
    )(core, *a8s, *b4s)


def _sum_slots(x, name):
    _, r, c = x.shape

    def body(x_ref, o_ref):
        acc = x_ref[0]
        for j in range(1, N_DEV):
            acc = acc + x_ref[j]
        o_ref[...] = acc

    return pl.pallas_call(body, name=name, out_shape=jax.ShapeDtypeStruct((r, c), F32))(x)


def _ada_fwd(c_all, w_ada):
    nb = c_all.shape[0]
    cols = w_ada.shape[2]

    def body(c_ref, w_ref, act_ref, mod_ref):
        cv = c_ref[...]
        act = cv * _sigmoid(cv)
        act_ref[...] = act
        for l in range(DEPTH):
            mod_ref[l] = jnp.dot(act, w_ref[l], precision=HI, preferred_element_type=F32)

    return pl.pallas_call(
        body, name="ada_fwd",
        out_shape=[jax.ShapeDtypeStruct((nb, D_MODEL), F32), jax.ShapeDtypeStruct((DEPTH, nb, cols), F32)],
        compiler_params=_cparams(),
    )(c_all, w_ada)


def _ada_bwd(c_act, dmod_all, dmod_mine, dep):
    nb = c_act.shape[0]
    cols = dmod_mine.shape[2]

    def body(act_ref, dall_ref, dmine_ref, dep_ref, gw_ref, gb_ref):
        act = act_ref[...]
        for l in range(DEPTH):
            gw_ref[l] = lax.dot_general(act, dmine_ref[l], (((0,), (0,)), ((), ())),
                                        precision=HI, preferred_element_type=F32)
            gb_ref[l:l + 1, :] = jnp.sum(dall_ref[l], axis=0, keepdims=True)

    return pl.pallas_call(
        body, name="ada_bwd",
        out_shape=[jax.ShapeDtypeStruct((DEPTH, D_MODEL, cols), F32),
                   jax.ShapeDtypeStruct((DEPTH, 3 * D_MODEL), F32)],
        compiler_params=_cparams(),
    )(c_act, dmod_all, dmod_mine, dep)


def _ln_in(x, ss, g, w, seq, segs=Z_SEGS, tm=ROW_TILE, dep=None):
    t = x.shape[0]
    tm = min(tm, seq)
    tps = seq // tm
    base = segs[0][1]

    def body(x_ref, ss_ref, g_ref, w_ref, h_ref, *outs):
        xv = x_ref[...]
        xn = xv * lax.rsqrt(jnp.mean(xv * xv, axis=-1, keepdims=True) + EPS)
        h = xn * g_ref[...] * ss_ref[0, 1:2, :] + ss_ref[0, 0:1, :]
        hb = h.astype(BF16)
        h_ref[...] = hb
        z = jnp.dot(hb, w_ref[...], preferred_element_type=F32)
        for o_ref, (_, off, wd, _) in zip(outs, segs):
            o_ref[...] = z[:, off - base:off - base + wd].astype(o_ref.dtype)

    row = lambda wd: pl.BlockSpec((tm, wd), lambda i: (i, 0))
    in_specs = [row(D_MODEL), pl.BlockSpec((1, 2, D_MODEL), lambda i: (i // tps, 0, 0)),
                pl.BlockSpec((1, D_MODEL), lambda i: (0, 0)), pl.BlockSpec((D_MODEL, w.shape[1]), lambda i: (0, 0))]
    body, in_specs, args = _after(dep, body, in_specs, [x, ss, g, w])
    return pl.pallas_call(
        body, name="ln_in", grid=(t // tm,), in_specs=in_specs,
        out_specs=[row(D_MODEL)] + [row(wd) for _, _, wd, _ in segs],
        out_shape=[jax.ShapeDtypeStruct((t, D_MODEL), BF16)]
        + [jax.ShapeDtypeStruct((t, wd), dt) for _, _, wd, dt in segs],
        compiler_params=_cparams(("arbitrary",)),
    )(*args)


def _z_rest(h, w, segs, tm=ROW_TILE):
    t = h.shape[0]
    base = segs[0][1]

    def body(h_ref, w_ref, *outs):
        z = jnp.dot(h_ref[...], w_ref[...], preferred_element_type=F32)
        for o_ref, (_, off, wd, _) in zip(outs, segs):
            o_ref[...] = z[:, off - base:off - base + wd].astype(o_ref.dtype)

    row = lambda wd: pl.BlockSpec((tm, wd), lambda i: (i, 0))
    return pl.pallas_call(
        body, name="ln_in_rest", grid=(t // tm,),
        in_specs=[row(D_MODEL), pl.BlockSpec((D_MODEL, w.shape[1]), lambda i: (0, 0))],
        out_specs=[row(wd) for _, _, wd, _ in segs],
        out_shape=[jax.ShapeDtypeStruct((t, wd), dt) for _, _, wd, dt in segs],
        compiler_params=_cparams(("arbitrary",)),
    )(h, w)


def _ln_in_bwd(dz, w_t, x, ss, g, dxo, seq, tm=ROW_TILE, dep=None):
    t = x.shape[0]
    tm = min(tm, seq)
    tps = seq // tm
    nb = t // seq
    nz = len(Z_SEGS)

    def body(*refs):
        dz_refs = refs[:nz]
        wt_ref, x_ref, ss_ref, g_ref, dxo_ref, dx_ref, dss_ref, dg_ref = refs[nz:]
        i = pl.program_id(0)
        dzc = jnp.concatenate([r[...].astype(BF16) for r in dz_refs], axis=1)
        dh = _nt(dzc, wt_ref[...])
        xv = x_ref[...]
        rstd = lax.rsqrt(jnp.mean(xv * xv, axis=-1, keepdims=True) + EPS)
        xn = xv * rstd
        gv = g_ref[...]
        s1 = ss_ref[0, 1:2, :]
        dxg = dh * s1
        dxn = dxg * gv
        dx = rstd * (dxn - xn * jnp.mean(dxn * xn, axis=-1, keepdims=True))
        dx_ref[...] = dxo_ref[...] + dx
        dshift = jnp.sum(dh, axis=0, keepdims=True)
        dscale = jnp.sum(dh * (xn * gv), axis=0, keepdims=True)
        dgp = jnp.sum(dxg * xn, axis=0, keepdims=True)

        @pl.when(i % tps == 0)
        def _():
            dss_ref[0, 0:1, :] = dshift
            dss_ref[0, 1:2, :] = dscale

        @pl.when(i % tps != 0)
        def _():
            dss_ref[0, 0:1, :] += dshift
            dss_ref[0, 1:2, :] += dscale

        @pl.when(i == 0)
        def _():
            dg_ref[...] = dgp

        @pl.when(i != 0)
        def _():
            dg_ref[...] += dgp

    row = lambda wd: pl.BlockSpec((tm, wd), lambda i: (i, 0))
    in_specs = ([row(wd) for _, _, wd, _ in Z_SEGS]
                + [pl.BlockSpec((D_MODEL, N_PAD), lambda i: (0, 0)), row(D_MODEL),
                   pl.BlockSpec((1, 2, D_MODEL), lambda i: (i // tps, 0, 0)),
                   pl.BlockSpec((1, D_MODEL), lambda i: (0, 0)), row(D_MODEL)])
    body, in_specs, args = _after(dep, body, in_specs, [*dz, w_t, x, ss, g, dxo])
    return pl.pallas_call(
        body, name="ln_in_bwd", grid=(t // tm,), in_specs=in_specs,
        out_specs=[row(D_MODEL), pl.BlockSpec((1, 2, D_MODEL), lambda i: (i // tps, 0, 0)),
                   pl.BlockSpec((1, D_MODEL), lambda i: (0, 0))],
        out_shape=[jax.ShapeDtypeStruct((t, D_MODEL), F32), jax.ShapeDtypeStruct((nb, 2, D_MODEL), F32),
                   jax.ShapeDtypeStruct((1, D_MODEL), F32)],
        compiler_params=_cparams(("arbitrary",)),
    )(*args)


def _matmul_tn(a, bs, name, tm=2048, dep=None):
    bs = list(bs) if isinstance(bs, (list, tuple)) else [bs]
    t, k = a.shape
    widths = [b.shape[1] for b in bs]
    n = sum(widths)
    tm = min(tm, t)

    def body(a_ref, *refs):
        b_refs, o_ref = refs[:-1], refs[-1]
        i = pl.program_id(0)
        av = a_ref[...].astype(BF16)
        parts = [b_ref[...].astype(BF16) for b_ref in b_refs]
        bv = parts[0] if len(parts) == 1 else jnp.concatenate(parts, axis=1)
        part = lax.dot_general(av, bv, (((0,), (0,)), ((), ())), preferred_element_type=F32)

        @pl.when(i == 0)
        def _():
            o_ref[...] = part

        @pl.when(i != 0)
        def _():
            o_ref[...] += part

    in_specs = [pl.BlockSpec((tm, k), lambda i: (i, 0))] + [pl.BlockSpec((tm, wd), lambda i: (i, 0)) for wd in widths]
    body, in_specs, args = _after(dep, body, in_specs, [a, *bs])
    return pl.pallas_call(
        body, name=name, grid=(t // tm,), in_specs=in_specs,
        out_specs=pl.BlockSpec((k, n), lambda i: (0, 0)),
        out_shape=jax.ShapeDtypeStruct((k, n), F32),
        compiler_params=_cparams(("arbitrary",)),
    )(*args)


def _rope(blk, cos_t, sin_a, sin_b):
    return blk * cos_t + pltpu.roll(blk, 112, 1) * sin_a + pltpu.roll(blk, 16, 1) * sin_b


def _unrope(d, cos_t, sin_a, sin_b):
    return d * cos_t + pltpu.roll(d * sin_a, 16, 1) + pltpu.roll(d * sin_b, 112, 1)


def _mla_prep(cq, ckv, kpe, gq, gkv, wuq, wk, wv, cos_t, sin_a, sin_b, tm=WIDE_ROW_TILE, dep=None):
    t = cq.shape[0]
    tm = min(tm, t)
    qw = A_HEADS * HEAD_PAD

    def body(cq_ref, ckv_ref, kpe_ref, gq_ref, gkv_ref, wuq_ref, wk_ref, wv_ref, c_ref, sa_ref, sb_ref,
             q_ref, k_ref, v_ref, cqn_ref, ckvn_ref):
        ct, sa, sb = c_ref[...], sa_ref[...], sb_ref[...]
        a = cq_ref[...]
        cqn = (a * lax.rsqrt(jnp.mean(a * a, axis=-1, keepdims=True) + EPS) * gq_ref[...]).astype(BF16)
        cqn_ref[...] = cqn
        b = ckv_ref[...]
        ckvn = (b * lax.rsqrt(jnp.mean(b * b, axis=-1, keepdims=True) + EPS) * gkv_ref[...]).astype(BF16)
        ckvn_ref[...] = ckvn
        qlin = jnp.dot(cqn, wuq_ref[...], preferred_element_type=F32)
        klin = jnp.dot(ckvn, wk_ref[...], preferred_element_type=F32)
        v_ref[...] = jnp.dot(ckvn, wv_ref[...], preferred_element_type=F32).astype(BF16)
        kr = _rope(kpe_ref[...], ct, sa, sb)
        for h in range(A_HEADS):
            sl = slice(h * HEAD_PAD, (h + 1) * HEAD_PAD)
            q_ref[:, sl] = _rope(qlin[:, sl], ct, sa, sb).astype(BF16)
            k_ref[:, sl] = (klin[:, sl] + kr).astype(BF16)

    row = lambda wd: pl.BlockSpec((tm, wd), lambda i: (i, 0))
    full = lambda r, c: pl.BlockSpec((r, c), lambda i: (0, 0))
    in_specs = [row(A_Q_RANK), row(A_KV_RANK), row(128), full(1, A_Q_RANK), full(1, A_KV_RANK),
                full(A_Q_RANK, qw), full(A_KV_RANK, qw), full(A_KV_RANK, GW), row(128), row(128), row(128)]
    body, in_specs, args = _after(dep, body, in_specs, [cq, ckv, kpe, gq, gkv, wuq, wk, wv, cos_t, sin_a, sin_b])
    return pl.pallas_call(
        body, name="mla_prep", grid=(t // tm,), in_specs=in_specs,
        out_specs=[row(qw), row(qw), row(GW), row(A_Q_RANK), row(A_KV_RANK)],
        out_shape=[jax.ShapeDtypeStruct((t, qw), BF16), jax.ShapeDtypeStruct((t, qw), BF16),
                   jax.ShapeDtypeStruct((t, GW), BF16), jax.ShapeDtypeStruct((t, A_Q_RANK), BF16),
                   jax.ShapeDtypeStruct((t, A_KV_RANK), BF16)],
        compiler_params=_cparams(("arbitrary",)),
    )(*args)


def _mla_prep_bwd(dq, dk, dv, cq, ckv, gq, gkv, wuq_t, wk_t, wv_t, cos_t, sin_a, sin_b, tm=WIDE_ROW_TILE):
    t = cq.shape[0]
    tm = min(tm, t)
    qw = A_HEADS * HEAD_PAD

    def body(dq_ref, dk_ref, dv_ref, cq_ref, ckv_ref, gq_ref, gkv_ref, wuqt_ref, wkt_ref, wvt_ref,
             c_ref, sa_ref, sb_ref, dcq_ref, dckv_ref, dkpe_ref, dql_ref, dkl_ref, dgq_ref, dgkv_ref):
        i = pl.program_id(0)
        ct, sa, sb = c_ref[...], sa_ref[...], sb_ref[...]
        lane = lax.broadcasted_iota(jnp.int32, (1, HEAD_PAD), 1)
        nope = lane < A_NOPE
        rope = (lane >= A_NOPE) & (lane < A_NOPE + A_ROPE)
        dksum = None
        for h in range(A_HEADS):
            sl = slice(h * HEAD_PAD, (h + 1) * HEAD_PAD)
            dql_ref[:, sl] = _unrope(dq_ref[:, sl].astype(F32), ct, sa, sb).astype(BF16)
            dkh = dk_ref[:, sl].astype(F32)
            dkl_ref[:, sl] = jnp.where(nope, dkh, 0.0).astype(BF16)
            dksum = dkh if dksum is None else dksum + dkh
        dkpe_ref[...] = jnp.where(rope, _unrope(jnp.where(rope, dksum, 0.0), ct, sa, sb), 0.0).astype(BF16)
        dcqn = jnp.dot(dql_ref[...], wuqt_ref[...], preferred_element_type=F32)
        dckvn = (jnp.dot(dkl_ref[...], wkt_ref[...], preferred_element_type=F32)
                 + jnp.dot(dv_ref[...].astype(BF16), wvt_ref[...], preferred_element_type=F32))

        def norm_bwd(xv, gv, dy):
            rstd = lax.rsqrt(jnp.mean(xv * xv, axis=-1, keepdims=True) + EPS)
            xn = xv * rstd
            dxn = dy * gv
            dx = rstd * (dxn - xn * jnp.mean(dxn * xn, axis=-1, keepdims=True))
            return dx, jnp.sum(dy * xn, axis=0, keepdims=True)

        dcq, dgq = norm_bwd(cq_ref[...], gq_ref[...], dcqn)
        dckv, dgkv = norm_bwd(ckv_ref[...], gkv_ref[...], dckvn)
        dcq_ref[...] = dcq.astype(BF16)
        dckv_ref[...] = dckv.astype(BF16)

        @pl.when(i == 0)
        def _():
            dgq_ref[...] = dgq
            dgkv_ref[...] = dgkv

        @pl.when(i != 0)
        def _():
            dgq_ref[...] += dgq
            dgkv_ref[...] += dgkv

    row = lambda wd: pl.BlockSpec((tm, wd), lambda i: (i, 0))
    full = lambda r, c: pl.BlockSpec((r, c), lambda i: (0, 0))
    return pl.pallas_call(
        body, name="mla_prep_bwd", grid=(t // tm,),
        in_specs=[row(qw), row(qw), row(GW), row(A_Q_RANK), row(A_KV_RANK), full(1, A_Q_RANK), full(1, A_KV_RANK),
                  full(qw, A_Q_RANK), full(qw, A_KV_RANK), full(GW, A_KV_RANK), row(128), row(128), row(128)],
        out_specs=[row(A_Q_RANK), row(A_KV_RANK), row(128), row(qw), row(qw), full(1, A_Q_RANK), full(1, A_KV_RANK)],
        out_shape=[jax.ShapeDtypeStruct((t, A_Q_RANK), BF16), jax.ShapeDtypeStruct((t, A_KV_RANK), BF16),
                   jax.ShapeDtypeStruct((t, 128), BF16), jax.ShapeDtypeStruct((t, qw), BF16),
                   jax.ShapeDtypeStruct((t, qw), BF16), jax.ShapeDtypeStruct((1, A_Q_RANK), F32),
                   jax.ShapeDtypeStruct((1, A_KV_RANK), F32)],
        compiler_params=_cparams(("arbitrary",)),
    )(dq, dk, dv, cq, ckv, gq, gkv, wuq_t, wk_t, wv_t, cos_t, sin_a, sin_b)


def _nt(a, b):
    return lax.dot_general(a, b, (((1,), (1,)), ((), ())), preferred_element_type=F32)


def _tn(a, b):
    return lax.dot_general(a, b, (((0,), (0,)), ((), ())), preferred_element_type=F32)


def _causal_mask(kind, q0, k0, tq, tk):
    qpos = q0 + lax.broadcasted_iota(jnp.int32, (tq, tk), 0)
    kpos = k0 + lax.broadcasted_iota(jnp.int32, (tq, tk), 1)
    if kind == "mla":
        return lax.shift_right_logical(kpos, 6) <= lax.shift_right_logical(qpos, 6)
    return kpos <= qpos


def _attn_fwd(kind, q, k, v, f, seq, scale, tq=512, tk=512):
    t = v.shape[0]
    nb = t // seq
    nq = seq // tq
    hw = 256 if kind == "mla" else 128
    n_heads = A_HEADS if kind == "mla" else C_HEADS
    use_f = f is not None
    tq, tk = min(tq, seq), min(tk, seq)
    nq = seq // tq
    assert tk == tq

    def body(*refs):
        if use_f:
            q_ref, k_ref, v_ref, f_ref, o_ref, st_ref = refs
        else:
            q_ref, k_ref, v_ref, o_ref, st_ref = refs
        qi = pl.program_id(2)
        q0 = qi * tq
        lane = lax.broadcasted_iota(jnp.int32, (1, 128), 1)
        half = lane >= 64
        qall = q_ref[...]
        if kind == "mla":
            qhs = [qall[:, 0:128], qall[:, 128:256]]
            post = scale * math.log2(math.e)
        else:
            assert math.frexp(scale)[0] == 0.5
            qall = qall * jnp.asarray(scale, BF16)
            qhs = [jnp.where(half, jnp.zeros_like(qall), qall), jnp.where(half, qall, jnp.zeros_like(qall))]
            post = None
        kd = pl.multiple_of(q0, tq)
        diag = _causal_mask(kind, 0, 0, tq, tk)

        def block(j, k0, state, masked):
            m, l, acc = state
            kh = k_ref[pl.ds(k0, tk), j * 128:(j + 1) * 128] if kind == "mla" else k_ref[pl.ds(k0, tk), :]
            s = _nt(qhs[j], kh)
            if post is not None:
                s = s * post
            if use_f:
                s = s - f_ref[0, 0, j:j + 1, pl.ds(k0, tk)]
            if masked:
                s = jnp.where(diag, s, NEG)
            mn = jnp.maximum(m, jnp.max(s, axis=-1, keepdims=True))
            alpha = jnp.exp2(m - mn) if post is not None else jnp.exp(m - mn)
            p = jnp.exp2(s - mn) if post is not None else jnp.exp(s - mn)
            l = alpha * l + jnp.sum(p, axis=-1, keepdims=True)
            acc = alpha * acc + jnp.dot(p.astype(BF16), v_ref[pl.ds(k0, tk), :], preferred_element_type=F32)
            return mn, l, acc

        def run(heads):
            def kstep(kb, carry):
                k0 = pl.multiple_of(kb * tk, tk)
                out = ()
                for n, j in enumerate(heads):
                    out += block(j, k0, carry[3 * n:3 * n + 3], False)
                return out

            init = (jnp.full((tq, 1), NEG, F32), jnp.zeros((tq, 1), F32), jnp.zeros((tq, 128), F32)) * len(heads)
            carry = lax.fori_loop(0, qi, kstep, init)
            o, st = jnp.zeros((tq, 128), F32), jnp.zeros((tq, 128), F32)
            for n, j in enumerate(heads):
                m, l, acc = block(j, kd, carry[3 * n:3 * n + 3], True)
                o = jnp.where(half == bool(j), acc / l, o)
                if post is not None:
                    m = m * math.log(2.0)
                st = jnp.where(lane == j, m + jnp.log(l), st)
            o_ref[...] = o.astype(BF16)
            st_ref[...] = st

        if n_heads % 2 == 0:
            run((0, 1))
        else:
            last = pl.program_id(1) == n_heads // 2
            pl.when(jnp.logical_not(last))(lambda: run((0, 1)))
            pl.when(last)(lambda: run((0,)))

    in_specs = [pl.BlockSpec((tq, hw), lambda b, p, i: (b * nq + i, p)),
                pl.BlockSpec((seq, hw), lambda b, p, i: (b, p)),
                pl.BlockSpec((seq, 128), lambda b, p, i: (b, p))]
    args = [q, k, v]
    if use_f:
        in_specs.append(pl.BlockSpec((1, 1, 8, seq), lambda b, p, i: (b, p, 0, 0)))
        args.append(f)
    oblk = pl.BlockSpec((tq, 128), lambda b, p, i: (b * nq + i, p))
    return pl.pallas_call(
        body, name="attn_fwd_" + kind, grid=(nb, 3, nq), in_specs=in_specs, out_specs=[oblk, oblk],
        out_shape=[jax.ShapeDtypeStruct((t, GW), BF16), jax.ShapeDtypeStruct((t, GW), F32)],
        compiler_params=_cparams(("arbitrary", "arbitrary", "arbitrary")),
    )(*args)


def _attn_bwd(kind, q, k, v, f, o, st, do, seq, scale, tq=512, tk=512, dep=None):
    t = v.shape[0]
    nb = t // seq
    tq, tk = min(tq, seq), min(tk, seq)
    nq = seq // tq
    nk = seq // tk
    hw = 256 if kind == "mla" else 128
    n_heads = A_HEADS if kind == "mla" else C_HEADS
    use_f = f is not None
    assert tq == tk

    def body(*refs):
        if use_f:
            (q_ref, k_ref, v_ref, f_ref, o_ref, st_ref, do_ref, dq_out, dk_out, dv_out, df_ref, dfq_ref,
             dq_ref, dk_ref, dv_ref) = refs
        else:
            q_ref, k_ref, v_ref, o_ref, st_ref, do_ref, dq_out, dk_out, dv_out, dq_ref, dk_ref, dv_ref = refs
        kj = pl.program_id(2)
        lane = lax.broadcasted_iota(jnp.int32, (1, 128), 1)
        half = lane >= 64

        @pl.when(kj == 0)
        def _():
            dq_ref[...] = jnp.zeros_like(dq_ref)
            if use_f:
                dfq_ref[...] = jnp.zeros_like(dfq_ref)

        dk_ref[...] = jnp.zeros_like(dk_ref)
        dv_ref[...] = jnp.zeros_like(dv_ref)
        if use_f:
            df_ref[...] = jnp.zeros_like(df_ref)
        vv = v_ref[...]
        diag = _causal_mask(kind, 0, 0, tq, tk)

        def qstep(qi, masked):
            q0 = pl.multiple_of(qi * tq, tq)
            rows = pl.ds(q0, tq)
            dov = do_ref[rows, :]
            dd = dov.astype(F32) * o_ref[rows, :]
            stv = st_ref[rows, :]

            def one_head(j):
                hm = half == bool(j)
                delta = jnp.sum(jnp.where(hm, dd, 0.0), axis=-1, keepdims=True)
                lse = stv[:, j:j + 1]
                if kind == "mla":
                    cols = slice(j * 128, (j + 1) * 128)
                    qh = q_ref[rows, cols]
                    kh = k_ref[:, cols]
                else:
                    cols = slice(0, 128)
                    qa = q_ref[rows, :]
                    qh = jnp.where(hm, qa, jnp.zeros_like(qa))
                    kh = k_ref[...]
                s = _nt(qh, kh) * scale
                if use_f:
                    s = s - f_ref[0, 0, j:j + 1, :]
                if masked:
                    s = jnp.where(diag, s, NEG)
                p = jnp.exp(s - lse)
                doh = jnp.where(hm, dov, jnp.zeros_like(dov))
                ds = p * (_nt(doh, vv) - delta)
                dsb = (ds * scale).astype(BF16)
                dv_ref[...] += _tn(p.astype(BF16), doh)
                dk_ref[:, cols] += _tn(dsb, qh)
                dqc = jnp.dot(dsb, kh, preferred_element_type=F32)
                if kind != "mla":
                    dqc = jnp.where(hm, dqc, 0.0)
                dq_ref[rows, cols] += dqc
                if use_f:
                    df_ref[0, 0, j:j + 1, :] += -jnp.sum(ds, axis=0, keepdims=True)
                    dfq_ref[rows, :] += jnp.where(lane == j, jnp.sum(ds, axis=-1, keepdims=True), 0.0)

            def both():
                one_head(0)
                one_head(1)

            if n_heads % 2 == 0:
                both()
            else:
                last = pl.program_id(1) == n_heads // 2
                pl.when(jnp.logical_not(last))(both)
                pl.when(last)(lambda: one_head(0))

        qstep(kj, True)

        def rest(qi, carry):
            qstep(qi, False)
            return carry

        lax.fori_loop(kj + 1, nq, rest, 0)
        dk_out[...] = dk_ref[...].astype(BF16)
        dv_out[...] = dv_ref[...].astype(BF16)

        @pl.when(kj == nk - 1)
        def _():
            dq_out[...] = dq_ref[...].astype(BF16)

    full_q = lambda wd: pl.BlockSpec((seq, wd), lambda b, p, i: (b, p))
    kblk = lambda wd: pl.BlockSpec((tk, wd), lambda b, p, i: (b * nk + i, p))
    in_specs = [full_q(hw), kblk(hw), kblk(128)]
    args = [q, k, v]
    if use_f:
        in_specs.append(pl.BlockSpec((1, 1, 8, tk), lambda b, p, i: (b, p, 0, i)))
        args.append(f)
    in_specs += [full_q(128), full_q(128), full_q(128)]
    args += [o, st, do]
    out_specs = [full_q(hw), kblk(hw), kblk(128)]
    out_shape = [jax.ShapeDtypeStruct((t, 3 * hw), BF16), jax.ShapeDtypeStruct((t, 3 * hw), BF16),
                 jax.ShapeDtypeStruct((t, GW), BF16)]
    scratch = [pltpu.VMEM((seq, hw), F32), pltpu.VMEM((tk, hw), F32), pltpu.VMEM((tk, 128), F32)]
    if use_f:
        out_specs += [pl.BlockSpec((1, 1, 8, tk), lambda b, p, i: (b, p, 0, i)), full_q(128)]
        out_shape += [jax.ShapeDtypeStruct((nb, 3, 8, seq), F32), jax.ShapeDtypeStruct((t, GW), F32)]
    body, in_specs, args = _after(dep, body, in_specs, args)
    return pl.pallas_call(
        body, name="attn_bwd_" + kind, grid=(nb, 3, nk), in_specs=in_specs, out_specs=out_specs,
        out_shape=out_shape, scratch_shapes=scratch,
        compiler_params=_cparams(("arbitrary", "arbitrary", "arbitrary")),
    )(*args)


BQ = 256
BWIN = BQ + B_LEFT


def _band_geometry():
    r = lax.broadcasted_iota(jnp.int32, (BQ, BWIN), 0)
    j = lax.broadcasted_iota(jnp.int32, (BQ, BWIN), 1)
    rc = lax.shift_right_logical(r, 6)
    jc = lax.shift_right_logical(j, 6)
    allowed = (jc - 8 <= rc) & (rc <= jc)
    return (r + B_LEFT - j) >= REL_CLIP, allowed, j < r


def _band_onehot(transposed, offset=0):
    shape = (BWIN, GW) if transposed else (GW, BWIN)
    kk = lax.broadcasted_iota(jnp.int32, shape, 1 if transposed else 0)
    x = lax.broadcasted_iota(jnp.int32, shape, 0 if transposed else 1) - offset
    x = jnp.where(x < 0, x + BWIN, x)
    return (kk == jnp.clip(B_LEFT - x, -REL_CLIP, REL_CLIP) + REL_CLIP).astype(F32)


def _band_table(rel_bias8):
    def body(b_ref, o_ref):
        hh = pl.program_id(0)
        u8 = jnp.dot(b_ref[...], _band_onehot(False), precision=HI, preferred_element_type=F32)
        rid = lax.broadcasted_iota(jnp.int32, (8, BWIN), 0)
        row = jnp.sum(jnp.where(rid == hh, u8, 0.0), axis=0, keepdims=True)
        far, allowed, _ = _band_geometry()
        tbl = pltpu.roll(jnp.broadcast_to(row, (BQ, BWIN)), 0, 1, stride=1, stride_axis=0)
        tbl = jnp.where(far, row[:, 0:1], tbl)
        o_ref[0] = jnp.where(allowed, tbl, NEG)

    return pl.pallas_call(
        body, name="band_table", grid=(6,),
        in_specs=[pl.BlockSpec((8, GW), lambda h: (0, 0))],
        out_specs=pl.BlockSpec((1, BQ, BWIN), lambda h: (h, 0, 0)),
        out_shape=jax.ShapeDtypeStruct((6, BQ, BWIN), F32),
        compiler_params=_cparams(("arbitrary",)),
    )(rel_bias8)


def _band_table_bwd(gtab):
    def body(g_ref, o_ref):
        gv = g_ref[0]
        _, _, wrapped = _band_geometry()
        gfar = jnp.sum(jnp.sum(jnp.where(wrapped, gv, 0.0), axis=-1, keepdims=True), axis=0, keepdims=True)
        anti = (lax.broadcasted_iota(jnp.int32, (BQ, BQ), 0) + lax.broadcasted_iota(jnp.int32, (BQ, BQ), 1)
                == BQ - 1).astype(F32)
        grev = jnp.dot(anti, jnp.where(wrapped, 0.0, gv), precision=HI, preferred_element_type=F32)
        near = pltpu.roll(grev, 0, 1, stride=1, stride_axis=0)
        y = jnp.broadcast_to(jnp.sum(near, axis=0, keepdims=True), (8, BWIN))
        gb = jnp.dot(y, _band_onehot(True, BQ - 1), precision=HI, preferred_element_type=F32)
        lane = lax.broadcasted_iota(jnp.int32, (8, GW), 1)
        o_ref[0] = gb + jnp.where(lane == 2 * REL_CLIP, gfar, 0.0)

    return pl.pallas_call(
        body, name="band_table_bwd", grid=(B_HEADS,),
        in_specs=[pl.BlockSpec((1, BQ, BWIN), lambda h: (h, 0, 0))],
        out_specs=pl.BlockSpec((1, 8, GW), lambda h: (h, 0, 0)),
        out_shape=jax.ShapeDtypeStruct((B_HEADS, 8, GW), F32),
        compiler_params=_cparams(("arbitrary",)),
    )(gtab)


def _band_fwd(q, k, v, table, seq, scale):
    t = q.shape[0]
    nb = t // seq
    nq = seq // BQ

    def body(q_ref, k_ref, v_ref, tb_ref, o_ref, st_ref, kpad, vpad):
        qi = pl.program_id(2)
        q0 = pl.multiple_of(qi * BQ, BQ)
        lane = lax.broadcasted_iota(jnp.int32, (1, 128), 1)
        half = lane >= 64

        @pl.when(qi == 0)
        def _():
            kpad[0:B_LEFT, :] = jnp.zeros((B_LEFT, 128), BF16)
            vpad[0:B_LEFT, :] = jnp.zeros((B_LEFT, 128), BF16)
            kpad[B_LEFT:, :] = k_ref[...]
            vpad[B_LEFT:, :] = v_ref[...]

        kw = kpad[pl.ds(q0, BWIN), :]
        vw = vpad[pl.ds(q0, BWIN), :]
        inside = lax.broadcasted_iota(jnp.int32, (BQ, BWIN), 1) >= B_LEFT - q0
        assert math.frexp(scale)[0] == 0.5
        qall = q_ref[...] * jnp.asarray(scale, BF16)

        def run(heads):
            o, st = jnp.zeros((BQ, 128), F32), jnp.zeros((BQ, 128), F32)
            for j in heads:
                qh = jnp.where(half == bool(j), qall, jnp.zeros_like(qall))
                s = jnp.where(inside, _nt(qh, kw) + tb_ref[j], NEG)
                m = jnp.max(s, axis=-1, keepdims=True)
                p = jnp.exp(s - m)
                l = jnp.sum(p, axis=-1, keepdims=True)
                o = jnp.where(half == bool(j), jnp.dot(p.astype(BF16), vw, preferred_element_type=F32) / l, o)
                st = jnp.where(lane == j, m + jnp.log(l), st)
            o_ref[...] = o.astype(BF16)
            st_ref[...] = st

        last = pl.program_id(1) == B_HEADS // 2
        pl.when(jnp.logical_not(last))(lambda: run((0, 1)))
        pl.when(last)(lambda: run((0,)))

    qblk = pl.BlockSpec((BQ, 128), lambda b, p, i: (b * nq + i, p))
    full = pl.BlockSpec((seq, 128), lambda b, p, i: (b, p))
    return pl.pallas_call(
        body, name="band_fwd", grid=(nb, 3, nq),
        in_specs=[qblk, full, full, pl.BlockSpec((2, BQ, BWIN), lambda b, p, i: (p, 0, 0))],
        out_specs=[qblk, qblk],
        out_shape=[jax.ShapeDtypeStruct((t, GW), BF16), jax.ShapeDtypeStruct((t, GW), F32)],
        scratch_shapes=[pltpu.VMEM((seq + B_LEFT, 128), BF16), pltpu.VMEM((seq + B_LEFT, 128), BF16)],
        compiler_params=_cparams(("arbitrary", "arbitrary", "arbitrary")),
    )(q, k, v, table)


def _band_bwd(q, k, v, table, o, st, do, seq, scale, dep=None):
    t = q.shape[0]
    nb = t // seq
    nq = seq // BQ

    def body(q_ref, k_ref, v_ref, tb_ref, o_ref, st_ref, do_ref, dq_ref, dk_ref, dv_ref, g_ref,
             kpad, vpad, dkpad, dvpad):
        b = pl.program_id(1)
        qi = pl.program_id(2)
        q0 = pl.multiple_of(qi * BQ, BQ)
        lane = lax.broadcasted_iota(jnp.int32, (1, 128), 1)
        half = lane >= 64

        @pl.when(qi == 0)
        def _():
            kpad[0:B_LEFT, :] = jnp.zeros((B_LEFT, 128), BF16)
            vpad[0:B_LEFT, :] = jnp.zeros((B_LEFT, 128), BF16)
            kpad[B_LEFT:, :] = k_ref[...]
            vpad[B_LEFT:, :] = v_ref[...]
            dkpad[...] = jnp.zeros_like(dkpad)
            dvpad[...] = jnp.zeros_like(dvpad)

        @pl.when((qi == 0) & (b == 0))
        def _():
            g_ref[...] = jnp.zeros_like(g_ref)

        win = pl.ds(q0, BWIN)
        kw = kpad[win, :]
        vw = vpad[win, :]
        inside = lax.broadcasted_iota(jnp.int32, (BQ, BWIN), 1) >= B_LEFT - q0
        qall = q_ref[...]
        dov = do_ref[...]
        dd = dov.astype(F32) * o_ref[...]
        stv = st_ref[...]

        def run(heads):
            dq = jnp.zeros((BQ, 128), F32)
            for j in heads:
                hm = half == bool(j)
                qh = jnp.where(hm, qall, jnp.zeros_like(qall))
                delta = jnp.sum(jnp.where(hm, dd, 0.0), axis=-1, keepdims=True)
                s = jnp.where(inside, _nt(qh, kw) * scale + tb_ref[j], NEG)
                p = jnp.exp(s - stv[:, j:j + 1])
                doh = jnp.where(hm, dov, jnp.zeros_like(dov))
                ds = p * (_nt(doh, vw) - delta)
                g_ref[j] += ds
                dsb = (ds * scale).astype(BF16)
                dvpad[win, :] += _tn(p.astype(BF16), doh)
                dkpad[win, :] += _tn(dsb, qh)
                dq = dq + jnp.where(hm, jnp.dot(dsb, kw, preferred_element_type=F32), 0.0)
            dq_ref[...] = dq.astype(BF16)

        last = pl.program_id(0) == B_HEADS // 2
        pl.when(jnp.logical_not(last))(lambda: run((0, 1)))
        pl.when(last)(lambda: run((0,)))

        @pl.when(qi == nq - 1)
        def _():
            dk_ref[...] = dkpad[B_LEFT:, :].astype(BF16)
            dv_ref[...] = dvpad[B_LEFT:, :].astype(BF16)

    qblk = pl.BlockSpec((BQ, 128), lambda p, b, i: (b * nq + i, p))
    full = pl.BlockSpec((seq, 128), lambda p, b, i: (b, p))
    tblk = pl.BlockSpec((2, BQ, BWIN), lambda p, b, i: (p, 0, 0))
    body, in_specs, args = _after(dep, body, [qblk, full, full, tblk, qblk, qblk, qblk], [q, k, v, table, o, st, do])
    return pl.pallas_call(
        body, name="band_bwd", grid=(3, nb, nq),
        in_specs=in_specs,
        out_specs=[qblk, full, full, tblk],
        out_shape=[jax.ShapeDtypeStruct((t, GW), BF16), jax.ShapeDtypeStruct((t, GW), BF16),
                   jax.ShapeDtypeStruct((t, GW), BF16), jax.ShapeDtypeStruct((6, BQ, BWIN), F32)],
        scratch_shapes=[pltpu.VMEM((seq + B_LEFT, 128), BF16), pltpu.VMEM((seq + B_LEFT, 128), BF16),
                        pltpu.VMEM((seq + B_LEFT, 128), F32), pltpu.VMEM((seq + B_LEFT, 128), F32)],
        compiler_params=_cparams(("arbitrary", "arbitrary", "arbitrary")),
    )(*args)


def _fox_prep(cf, fb, seq):
    nb = cf.shape[0] // seq
    nblk = seq // 128

    def body(cf_ref, fb_ref, f_ref):
        x = cf_ref[...] + fb_ref[...]
        lf = jnp.minimum(x, 0.0) - jnp.log1p(jnp.exp(-jnp.abs(x)))
        rows = lf.T[0:8, :]
        upper = (lax.broadcasted_iota(jnp.int32, (128, 128), 0)
                 <= lax.broadcasted_iota(jnp.int32, (128, 128), 1)).astype(F32)
        carry = jnp.zeros((8, 1), F32)
        for blk in range(nblk):
            sl = slice(blk * 128, (blk + 1) * 128)
            cs = jnp.dot(rows[:, sl], upper, precision=HI, preferred_element_type=F32) + carry
            carry = cs[:, 127:128]
            f_ref[0, 0, :, sl] = cs
            f_ref[0, 1, :, sl] = pltpu.roll(cs, 6, 0)
            f_ref[0, 2, :, sl] = pltpu.roll(cs, 4, 0)

    return pl.pallas_call(
        body, name="fox_prep", grid=(nb,),
        in_specs=[pl.BlockSpec((seq, 128), lambda b: (b, 0)), pl.BlockSpec((1, 128), lambda b: (0, 0))],
        out_specs=pl.BlockSpec((1, 3, 8, seq), lambda b: (b, 0, 0, 0)),
        out_shape=jax.ShapeDtypeStruct((nb, 3, 8, seq), F32),
        compiler_params=_cparams(("arbitrary",)),
    )(cf, fb)


def _fox_prep_bwd(df, dfq, cf, fb, seq):
    nb = cf.shape[0] // seq
    nblk = seq // 128

    def body(df_ref, dfq_ref, cf_ref, fb_ref, dcf_ref, dfb_ref, wide):
        b = pl.program_id(0)
        row = lax.broadcasted_iota(jnp.int32, (8, seq), 0)
        dfh = None
        for p in range(3):
            both = df_ref[0, p] + dfq_ref[:, p * 128:(p + 1) * 128].T[0:8, :]
            both = jnp.where(row < 2, both, 0.0)
            if p:
                both = pltpu.roll(both, 2 * p, 0)
            dfh = both if dfh is None else dfh + both
        lower = (lax.broadcasted_iota(jnp.int32, (128, 128), 0)
                 >= lax.broadcasted_iota(jnp.int32, (128, 128), 1)).astype(F32)
        wide[...] = jnp.zeros_like(wide)
        carry = jnp.zeros((8, 1), F32)
        for blk in reversed(range(nblk)):
            sl = slice(blk * 128, (blk + 1) * 128)
            rc = jnp.dot(dfh[:, sl], lower, precision=HI, preferred_element_type=F32) + carry
            carry = rc[:, 0:1]
            wide[0:8, sl] = rc
        dl = wide[...].T
        x = cf_ref[...] + fb_ref[...]
        dcf = dl * (1.0 / (1.0 + jnp.exp(x)))
        dcf_ref[...] = dcf.astype(BF16)
        part = jnp.sum(dcf, axis=0, keepdims=True)

        @pl.when(b == 0)
        def _():
            dfb_ref[...] = part

        @pl.when(b != 0)
        def _():
            dfb_ref[...] += part

    return pl.pallas_call(
        body, name="fox_prep_bwd", grid=(nb,),
        in_specs=[pl.BlockSpec((1, 3, 8, seq), lambda b: (b, 0, 0, 0)), pl.BlockSpec((seq, GW), lambda b: (b, 0)),
                  pl.BlockSpec((seq, 128), lambda b: (b, 0)), pl.BlockSpec((1, 128), lambda b: (0, 0))],
        out_specs=[pl.BlockSpec((seq, 128), lambda b: (b, 0)), pl.BlockSpec((1, 128), lambda b: (0, 0))],
        out_shape=[jax.ShapeDtypeStruct(cf.shape, BF16), jax.ShapeDtypeStruct((1, 128), F32)],
        scratch_shapes=[pltpu.VMEM((128, seq), F32)],
        compiler_params=_cparams(("arbitrary",)),
    )(df, dfq, cf, fb)


def _gate_out(oa, ob, oc, gates, w, x, gate, seq, tm=WIDE_ROW_TILE):
    t = x.shape[0]
    tm = min(tm, seq)
    tps = seq // tm

    def body(oa_ref, ob_ref, oc_ref, g_ref, w_ref, x_ref, gt_ref, xo_ref, y_ref, u_ref):
        for n, o_ref in enumerate((oa_ref, ob_ref, oc_ref)):
            sl = slice(n * GW, (n + 1) * GW)
            gv = g_ref[:, sl].astype(F32)
            u_ref[:, sl] = (o_ref[...] * (gv * _sigmoid(gv))).astype(BF16)
        y = jnp.dot(u_ref[...], w_ref[...], preferred_element_type=F32)
        y_ref[...] = y.astype(BF16)
        xo_ref[...] = x_ref[...] + gt_ref[0] * y

    row = lambda wd: pl.BlockSpec((tm, wd), lambda i: (i, 0))
    return pl.pallas_call(
        body, name="gate_out", grid=(t // tm,),
        in_specs=[row(GW), row(GW), row(GW), row(U_PAD), pl.BlockSpec((U_PAD, D_MODEL), lambda i: (0, 0)),
                  row(D_MODEL), pl.BlockSpec((1, 1, D_MODEL), lambda i: (i // tps, 0, 0))],
        out_specs=[row(D_MODEL), row(D_MODEL), row(U_PAD)],
        out_shape=[jax.ShapeDtypeStruct((t, D_MODEL), F32), jax.ShapeDtypeStruct((t, D_MODEL), BF16),
                   jax.ShapeDtypeStruct((t, U_PAD), BF16)],
        compiler_params=_cparams(("arbitrary",)),
    )(oa, ob, oc, gates, w, x, gate)


def _gate_out_bwd(dxo, y, gate, oa, ob, oc, gates, w_t, seq, tm=WIDE_ROW_TILE, dep=None):
    t = dxo.shape[0]
    tm = min(tm, seq)
    tps = seq // tm
    nb = t // seq

    def body(dxo_ref, y_ref, gt_ref, oa_ref, ob_ref, oc_ref, g_ref, wt_ref,
             dy_ref, doa_ref, dob_ref, doc_ref, dg_ref, dgt_ref):
        i = pl.program_id(0)
        dxo_v = dxo_ref[...]
        dgt = jnp.sum(dxo_v * y_ref[...].astype(F32), axis=0, keepdims=True)
        dyb = (dxo_v * gt_ref[0]).astype(BF16)
        dy_ref[...] = dyb
        du = _nt(dyb, wt_ref[...])
        for n, (o_ref, do_ref) in enumerate(((oa_ref, doa_ref), (ob_ref, dob_ref), (oc_ref, doc_ref))):
            sl = slice(n * GW, (n + 1) * GW)
            gv = g_ref[:, sl].astype(F32)
            sg = _sigmoid(gv)
            dun = du[:, sl]
            do_ref[...] = (dun * (gv * sg)).astype(BF16)
            dg_ref[:, sl] = (dun * o_ref[...] * (sg * (1.0 + gv * (1.0 - sg)))).astype(BF16)

        @pl.when(i % tps == 0)
        def _():
            dgt_ref[0] = dgt

        @pl.when(i % tps != 0)
        def _():
            dgt_ref[0] += dgt

    row = lambda wd: pl.BlockSpec((tm, wd), lambda i: (i, 0))
    per_b = pl.BlockSpec((1, 1, D_MODEL), lambda i: (i // tps, 0, 0))
    in_specs = [row(D_MODEL), row(D_MODEL), per_b, row(GW), row(GW), row(GW), row(U_PAD),
                pl.BlockSpec((U_PAD, D_MODEL), lambda i: (0, 0))]
    body, in_specs, args = _after(dep, body, in_specs, [dxo, y, gate, oa, ob, oc, gates, w_t])
    return pl.pallas_call(
        body, name="gate_out_bwd", grid=(t // tm,), in_specs=in_specs,
        out_specs=[row(D_MODEL), row(GW), row(GW), row(GW), row(U_PAD), per_b],
        out_shape=[jax.ShapeDtypeStruct((t, D_MODEL), BF16), jax.ShapeDtypeStruct((t, GW), BF16),
                   jax.ShapeDtypeStruct((t, GW), BF16), jax.ShapeDtypeStruct((t, GW), BF16),
                   jax.ShapeDtypeStruct((t, U_PAD), BF16), jax.ShapeDtypeStruct((nb, 1, D_MODEL), F32)],
        compiler_params=_cparams(("arbitrary",)),
    )(*args)


def _final_loss(x, target, g, tm=WIDE_ROW_TILE):
    t = x.shape[0]
    tm = min(tm, t)

    def body(x_ref, t_ref, g_ref, dx_ref, loss_ref, dg_ref):
        i = pl.program_id(0)
        xv = x_ref[...]
        rstd = lax.rsqrt(jnp.mean(xv * xv, axis=-1, keepdims=True) + EPS)
        xn = xv * rstd
        gv = g_ref[...]
        err = xn * gv - t_ref[...]
        dy = err * (1.0 / D_MODEL)
        dxn = dy * gv
        dx_ref[...] = rstd * (dxn - xn * jnp.mean(dxn * xn, axis=-1, keepdims=True))
        lp = jnp.sum(err * err, axis=0, keepdims=True) * (0.5 / D_MODEL)
        dgp = jnp.sum(dy * xn, axis=0, keepdims=True)

        @pl.when(i == 0)
        def _():
            loss_ref[...] = lp
            dg_ref[...] = dgp

        @pl.when(i != 0)
        def _():
            loss_ref[...] += lp
            dg_ref[...] += dgp

    row = pl.BlockSpec((tm, D_MODEL), lambda i: (i, 0))
    vec = pl.BlockSpec((1, D_MODEL), lambda i: (0, 0))
    return pl.pallas_call(
        body, name="final_loss", grid=(t // tm,),
        in_specs=[row, row, vec], out_specs=[row, vec, vec],
        out_shape=[jax.ShapeDtypeStruct((t, D_MODEL), F32), jax.ShapeDtypeStruct((1, D_MODEL), F32),
                   jax.ShapeDtypeStruct((1, D_MODEL), F32)],
        compiler_params=_cparams(("arbitrary",)),
    )(x, target, g)


def _adamw(w, gslots, m, v, name, tr=None):
    nl, r, c = w.shape
    ns = gslots.shape[0]
    tr = r if tr is None else tr

    def body(w_ref, g_ref, m_ref, v_ref, go_ref, d_ref, mo_ref, vo_ref):
        g = g_ref[0].astype(F32)
        for j in range(1, ns):
            g = g + g_ref[j].astype(F32)
        mn = ADAM_B1 * m_ref[...] + (1.0 - ADAM_B1) * g
        vn = ADAM_B2 * v_ref[...] + (1.0 - ADAM_B2) * jnp.square(g)
        m_hat = mn / (1.0 - ADAM_B1 ** ADAM_STEP)
        v_hat = vn / (1.0 - ADAM_B2 ** ADAM_STEP)
        go_ref[...] = g
        d_ref[...] = -ADAM_LR * (m_hat / (jnp.sqrt(v_hat) + ADAM_EPS) + ADAM_WD * w_ref[...])
        mo_ref[...] = mn
        vo_ref[...] = vn

    blk = pl.BlockSpec((1, tr, c), lambda l, i: (l, i, 0))
    return pl.pallas_call(
        body, name=name, grid=(nl, r // tr),
        in_specs=[blk, pl.BlockSpec((ns, 1, tr, c), lambda l, i: (0, l, i, 0)), blk, blk],
        out_specs=[blk] * 4, out_shape=[jax.ShapeDtypeStruct((nl, r, c), F32)] * 4,
        compiler_params=_cparams(("arbitrary", "arbitrary")),
    )(w, gslots, m, v)


def _rope_tables(positions):
    inv = ROPE_THETA ** (-jnp.arange(0, A_ROPE, 2, dtype=F32) / A_ROPE)
    ang = positions.astype(F32)[:, None] * inv
    cos, sin = jnp.cos(ang), jnp.sin(ang)
    t = positions.shape[0]
    one = jnp.ones((t, 64), F32)
    zero16 = jnp.zeros((t, 16), F32)
    cos_t = jnp.concatenate([one, cos, cos, jnp.ones((t, 32), F32)], axis=1)
    sin_a = jnp.concatenate([jnp.zeros((t, 64), F32), -sin, zero16, jnp.zeros((t, 32), F32)], axis=1)
    sin_b = jnp.concatenate([jnp.zeros((t, 64), F32), zero16, sin, jnp.zeros((t, 32), F32)], axis=1)
    return cos_t, sin_a, sin_b


def _pad_heads(w, real, padded, nheads, axis):
    shp = w.shape[:axis] + (nheads, real) + w.shape[axis + 1:]
    w = w.reshape(shp)
    pad = [(0, 0)] * w.ndim
    pad[axis + 1] = (0, padded - real)
    w = jnp.pad(w, pad)
    return w.reshape(w.shape[:axis] + (nheads * padded,) + w.shape[axis + 2:])


def kernel(x, c, positions, w_ada, b_ada, norm_g, w_in, a_q_norm_g, a_w_uq, a_kv_norm_g, a_w_ukv, b_rel_bias, c_forget_b, w_out, final_g, loss_target, m_w_ada, m_b_ada, m_norm_g, m_w_in, m_a_q_norm_g, m_a_w_uq, m_a_kv_norm_g, m_a_w_ukv, m_b_rel_bias, m_c_forget_b, m_w_out, m_final_g, v_w_ada, v_b_ada, v_norm_g, v_w_in, v_a_q_norm_g, v_a_w_uq, v_a_kv_norm_g, v_a_w_ukv, v_b_rel_bias, v_c_forget_b, v_w_out, v_final_g):
    nb, seq, _ = x.shape
    t = nb * seq
    me = 4 * lax.axis_index("x") + 2 * lax.axis_index("y") + lax.axis_index("c")
    x2 = x.reshape(t, D_MODEL)
    tgt = loss_target.reshape(t, D_MODEL)
    cos_t, sin_a, sin_b = _rope_tables(positions.reshape(t))

    def shards(l):
        return [_pad_runs(w_in[l].astype(BF16), IN_RUNS, N_PAD, 1), w_out[l].astype(BF16),
                a_w_uq[l].astype(BF16), a_w_ukv[l].astype(BF16)]

    def prepare(gi, go, gq, gkv):
        return dict(w_in=gi.reshape(D_MODEL, N_PAD), **prepare_rest(go, gq, gkv))

    def prepare_rest(go, gq, gkv):
        wo = _pad_runs(go.reshape(D_MODEL, D_MODEL), OUT_RUNS, U_PAD, 0)
        wq = jnp.transpose(gq, (1, 0, 2)).reshape(A_Q_RANK, A_HEADS * (A_NOPE + A_ROPE))
        wq = _pad_heads(wq, A_NOPE + A_ROPE, HEAD_PAD, A_HEADS, 1)
        wkv = jnp.transpose(gkv, (1, 0, 2)).reshape(A_KV_RANK, A_HEADS, 2 * A_NOPE)
        wk = jnp.pad(wkv[:, :, :A_NOPE], ((0, 0), (0, 0), (0, HEAD_PAD - A_NOPE))).reshape(A_KV_RANK, A_HEADS * HEAD_PAD)
        wv = wkv[:, :, A_NOPE:].reshape(A_KV_RANK, GW)
        return dict(w_out=wo, wuq=wq, wuq_t=wq.T, wk=wk, wk_t=wk.T, wv=wv, wv_t=wv.T)

    shards0 = shards(0)
    n_seg_first = 4
    first_cols = Z_SEGS[n_seg_first][1]
    w_in0_g, c_g = _gather([shards0[0][:, :first_cols], c], "gather_w_in0")
    c_all = c_g.reshape(N_DEV * nb, D_MODEL)
    w_in0_first = w_in0_g.reshape(D_MODEL, first_cols)

    c_act, mod_cols = _ada_fwd(c_all, w_ada)
    (mod_g,) = _gather([mod_cols], "gather_mod")
    in0, in0_token = _split_start("gather", [shards0[0][:, first_cols:]], "gather_w_in0_rest_start", after=mod_g)
    rest0, rest0_token = _split_start("gather", shards0[1:], "gather_rest0_start", after=in0_token)
    weights = [None, None]
    mod_all = jnp.transpose(mod_g, (1, 2, 0, 3)).reshape(DEPTH, N_DEV * nb, 3 * D_MODEL)
    mod = lax.dynamic_slice_in_dim(mod_all, me * nb, nb, axis=1) + b_ada[:, None, :]

    fb_pad = jnp.pad(c_forget_b, ((0, 0), (0, 128 - C_HEADS)))
    a_scale = (A_NOPE + A_ROPE) ** -0.5
    h_scale = CHUNK ** -0.5

    saved = []
    xl = x2
    for l in range(DEPTH):
        if l == 1:
            weights[1] = prepare(*_split_wait(gather1, xl, "gather_weights1_wait")[1])
        shift, scale, gate = mod[l, :, :D_MODEL], mod[l, :, D_MODEL:2 * D_MODEL], mod[l, :, 2 * D_MODEL:]
        ss = jnp.stack([shift, 1.0 + scale], axis=1)
        gate3 = gate[:, None, :]
        gather1_token = None
        if l == 0:
            h, cq, ckv, kpe, gates = _ln_in(xl, ss, norm_g[l:l + 1], w_in0_first, seq, segs=Z_SEGS[:n_seg_first],
                                            dep=rest0_token)
            w_in0_rest = _split_wait(in0, h, "gather_w_in0_rest_wait")[1][0].reshape(D_MODEL, N_PAD - first_cols)
            bq, bk, bv, cq2, ck, cv, cf = _z_rest(h, w_in0_rest, Z_SEGS[n_seg_first:])
            weights[0] = dict(w_in=jnp.concatenate([w_in0_first, w_in0_rest], axis=1))
            weights[0].update(prepare_rest(*_split_wait(rest0, cf, "gather_rest0_wait")[1]))
            gather1, gather1_token = _split_start("gather", shards(1), "gather_weights1_start",
                                                  after=weights[0]["w_out"])
        else:
            h, cq, ckv, kpe, gates, bq, bk, bv, cq2, ck, cv, cf = _ln_in(xl, ss, norm_g[l:l + 1], weights[l]["w_in"], seq)
        w = weights[l]
        q, k, v, cqn, ckvn = _mla_prep(cq, ckv, kpe, a_q_norm_g[l:l + 1], a_kv_norm_g[l:l + 1],
                                       w["wuq"], w["wk"], w["wv"], cos_t, sin_a, sin_b, dep=gather1_token)
        oa, sta = _attn_fwd("mla", q, k, v, None, seq, a_scale)
        table = _band_table(jnp.pad(b_rel_bias[l], ((0, 8 - B_HEADS), (0, GW - N_REL))))
        ob, stb = _band_fwd(bq, bk, bv, table, seq, h_scale)
        fcum = _fox_prep(cf, fb_pad[l:l + 1], seq)
        oc, stc = _attn_fwd("fox", cq2, ck, cv, fcum, seq, h_scale)
        xn, y, u = _gate_out(oa, ob, oc, gates, w["w_out"], xl, gate3, seq)
        saved.append(dict(x=xl, ss=ss, gate3=gate3, h=h, cq=cq, ckv=ckv, gates=gates, bq=bq, bk=bk, bv=bv,
                          cq2=cq2, ck=ck, cv=cv, cf=cf, q=q, k=k, v=v, cqn=cqn, ckvn=ckvn, oa=oa, sta=sta,
                          table=table, ob=ob, stb=stb, fcum=fcum, oc=oc, stc=stc, y=y, u=u))
        xl = xn

    dx, loss_lanes, g_final = _final_loss(xl, tgt, final_g[None, :])
    loss = lax.psum(jnp.sum(loss_lanes), AXES)

    rows = D_MODEL // N_DEV
    core = lax.axis_index("c").astype(jnp.int32).reshape(1)
    n_seg_a = 4
    dmods, smalls, parts = [None] * DEPTH, [None] * DEPTH, [None] * DEPTH
    pair1 = chips1 = pair1_token = chips1_token = None
    for l in reversed(range(DEPTH)):
        s, w = saved[l], weights[l]
        dy, doa, dob, doc, dgates, dgate = _gate_out_bwd(dx, s["y"], s["gate3"], s["oa"], s["ob"], s["oc"],
                                                         s["gates"], w["w_out"], seq, dep=pair1_token)
        g_out = _unpad_runs(_matmul_tn(s["u"], dy, "dw_out"), OUT_RUNS, 0)
        if l == 0:
            own, from_sib = _split_wait(pair1, g_out, "grads1_pair_wait")
            chips1, chips1_token = _split_start("chips", _pair_add(core, own, from_sib, "grads1_add"), "grads1_chips_start")
        dq, dk, dv = _attn_bwd("mla", s["q"], s["k"], s["v"], None, s["oa"], s["sta"], doa, seq, a_scale,
                               dep=chips1_token)
        dbq, dbk, dbv, gtab = _band_bwd(s["bq"], s["bk"], s["bv"], s["table"], s["ob"], s["stb"], dob, seq, h_scale,
                                        dep=chips1_token)
        g_rel = _band_table_bwd(gtab)[:, 0, :N_REL]
        dcq2, dck, dcv, dfc, dfq = _attn_bwd("fox", s["cq2"], s["ck"], s["cv"], s["fcum"], s["oc"], s["stc"], doc,
                                             seq, h_scale, dep=chips1_token)
        dcf, dfb = _fox_prep_bwd(dfc, dfq, s["cf"], fb_pad[l:l + 1], seq)
        dcq, dckv, dkpe, dqlin, dklin, dgq, dgkv = _mla_prep_bwd(
            dq, dk, dv, s["cq"], s["ckv"], a_q_norm_g[l:l + 1], a_kv_norm_g[l:l + 1],
            w["wuq_t"], w["wk_t"], w["wv_t"], cos_t, sin_a, sin_b)
        gq_pad = _matmul_tn(s["cqn"], dqlin, "dw_uq")
        g_uq = gq_pad.reshape(A_Q_RANK, A_HEADS, HEAD_PAD)[:, :, :A_NOPE + A_ROPE].reshape(A_Q_RANK, -1)
        gkv_pad = _matmul_tn(s["ckvn"], [dklin, dv], "dw_ukv")
        gk_pad = gkv_pad[:, :A_HEADS * HEAD_PAD].reshape(A_KV_RANK, A_HEADS, HEAD_PAD)[:, :, :A_NOPE]
        gv_pad = gkv_pad[:, A_HEADS * HEAD_PAD:].reshape(A_KV_RANK, A_HEADS, A_NOPE)
        g_ukv = jnp.concatenate([gk_pad, gv_pad], axis=2).reshape(A_KV_RANK, -1)
        dz = [dcq, dckv, dkpe, dgates, dbq, dbk, dbv, dcq2, dck, dcv, dcf]
        g_in_a = _matmul_tn(s["h"], dz[:n_seg_a], "dw_in_a")
        first = [g_in_a.reshape(N_DEV, rows, -1), g_out.reshape(N_DEV, rows, D_MODEL),
                 g_uq.reshape(A_Q_RANK, N_DEV, -1).transpose(1, 0, 2), g_ukv.reshape(A_KV_RANK, N_DEV, -1).transpose(1, 0, 2)]
        if l == 1:
            g_in_b = _matmul_tn(s["h"], dz[n_seg_a:], "dw_in_b")
            pair1, pair1_token = _split_start("pair", first + [g_in_b.reshape(N_DEV, rows, -1)], "grads1_pair_start")
            tail_token = None
        else:
            pair0a, pair0a_token = _split_start("pair", first, "grads0a_pair_start")
            g_in_b = _matmul_tn(s["h"], dz[n_seg_a:], "dw_in_b", dep=pair0a_token)
            own, from_sib = _split_wait(pair0a, g_in_b, "grads0a_pair_wait")
            sums0a = _pair_add(core, own, from_sib, "grads0a_add")
            pair0b, pair0b_token = _split_start("pair", [g_in_b.reshape(N_DEV, rows, -1)], "grads0b_pair_start",
                                                after=sums0a[0])
            chips0a, tail_token = _split_start("chips", sums0a, "grads0a_chips_start", after=pair0b_token)
        dx, dss, dg_norm = _ln_in_bwd(dz, w["w_in"], s["x"], s["ss"], norm_g[l:l + 1], dx, seq, dep=tail_token)
        dmods[l] = jnp.concatenate([dss[:, 0, :], dss[:, 1, :], dgate[:, 0, :]], axis=1)
        smalls[l] = [dg_norm.reshape(-1), dgq.reshape(-1), dgkv.reshape(-1), g_rel.reshape(-1),
                     dfb[0, :C_HEADS]]
    grad_x = dx.reshape(nb, seq, D_MODEL)
    parts[1] = _split_wait(chips1, dx, "grads1_chips_wait")[1]
    parts0a = _split_wait(chips0a, dx, "grads0a_chips_wait")[1]
    own, from_sib = _split_wait(pair0b, dx, "grads0b_pair_wait")

    small = jnp.concatenate([p for l in range(DEPTH) for p in smalls[l]] + [g_final.reshape(-1)])
    n_small = small.shape[0]
    small_rows = -(-n_small // 1024) * 8
    small = jnp.pad(small, (0, small_rows * 128 - n_small)).reshape(small_rows, 128)
    dmod_local = jnp.stack(dmods)
    dmod_g, small_g = _gather([dmod_local, small], "gather_small", dep=parts0a[0])
    chips0, chips0_token = _split_start("chips", _pair_add(core, own, from_sib, "grads0b_add"), "grads0b_chips_start",
                                        after=small_g)
    dmod_all = jnp.transpose(dmod_g, (1, 0, 2, 3)).reshape(DEPTH, N_DEV * nb, 3 * D_MODEL)
    cols = 3 * D_MODEL // N_DEV
    dmod_mine = lax.dynamic_slice_in_dim(dmod_all, me * cols, cols, axis=2)
    g_w_ada, g_b_ada = _ada_bwd(c_act, dmod_all, dmod_mine, chips0_token)
    small_sum = _sum_slots(small_g, "sum_small").reshape(-1)

    def split_small():
        out, pos = [], 0
        sizes = [D_MODEL, A_Q_RANK, A_KV_RANK, B_HEADS * N_REL, C_HEADS]
        per_layer = []
        for l in range(DEPTH):
            parts = []
            for sz in sizes:
                parts.append(small_sum[pos:pos + sz])
                pos += sz
            per_layer.append(parts)
        for j in range(len(sizes)):
            out.append(jnp.stack([per_layer[l][j] for l in range(DEPTH)]))
        out.append(small_sum[pos:pos + D_MODEL])
        return out

    g_norm, g_qn, g_kvn, g_relb, g_fb, g_fin = split_small()

    def adam(w, g, m, v, name, tr=None):
        shp = w.shape
        w3 = w.reshape((1,) * (3 - w.ndim) + shp)
        outs = _adamw(w3, g.reshape((-1,) + w3.shape), m.reshape(w3.shape), v.reshape(w3.shape), name, tr)
        return [o.reshape(shp) for o in outs]

    res = {
        "w_ada": adam(w_ada, g_w_ada, m_w_ada, v_w_ada, "adam_w_ada", 256),
        "b_ada": adam(b_ada, g_b_ada, m_b_ada, v_b_ada, "adam_b_ada"),
        "norm_g": adam(norm_g, g_norm, m_norm_g, v_norm_g, "adam_norm_g"),
        "a_q_norm_g": adam(a_q_norm_g, g_qn, m_a_q_norm_g, v_a_q_norm_g, "adam_q_norm"),
        "a_kv_norm_g": adam(a_kv_norm_g, g_kvn, m_a_kv_norm_g, v_a_kv_norm_g, "adam_kv_norm"),
        "b_rel_bias": adam(b_rel_bias, g_relb.reshape(b_rel_bias.shape), m_b_rel_bias, v_b_rel_bias, "adam_rel_bias"),
        "c_forget_b": adam(c_forget_b, g_fb, m_c_forget_b, v_c_forget_b, "adam_forget_b"),
        "final_g": adam(final_g, g_fin, m_final_g, v_final_g, "adam_final_g"),
    }
    parts[0] = list(parts0a) + list(_split_wait(chips0, res["w_ada"][1], "grads0b_chips_wait")[1])
    p_in = jnp.stack([_unpad_runs(jnp.concatenate([parts[l][0], parts[l][4]], axis=2), IN_RUNS, 2)
                      for l in range(DEPTH)], axis=1)
    p_out, p_uq, p_ukv = (jnp.stack([parts[l][j] for l in range(DEPTH)], axis=1) for j in (1, 2, 3))
    res.update({
        "w_in": adam(w_in, p_in, m_w_in, v_w_in, "adam_w_in", 64),
        "a_w_uq": adam(a_w_uq, p_uq, m_a_w_uq, v_a_w_uq, "adam_w_uq"),
        "a_w_ukv": adam(a_w_ukv, p_ukv, m_a_w_ukv, v_a_w_ukv, "adam_w_ukv"),
        "w_out": adam(w_out, p_out, m_w_out, v_w_out, "adam_w_out", 64),
    })
    names = ["w_ada", "b_ada", "norm_g", "w_in", "a_q_norm_g", "a_w_uq", "a_kv_norm_g", "a_w_ukv", "b_rel_bias",
             "c_forget_b", "w_out", "final_g"]
    outs = [loss, grad_x]
    for j in range(4):
        outs += [res[n][j] for n in names]
    return tuple(outs)
```
